```python
import math
import jax
import jax.numpy as jnp
from jax import lax
import numpy as np

D_MODEL = 1024
BATCH = 2
SEQ = 16384
DEPTH = 1

D_MIX = D_MODEL
D_HYENA = D_MIX // 2
D_POOL = D_MIX - D_HYENA
HYENA_ORDER = 2
SHORT_CONV = 3
FILTER_EMB = 33
FILTER_HIDDEN = 64
FILTER_INNER = 2
DECAY_TARGET = 1e-2
FAST_DECAY_PCT = 0.3
SLOW_DECAY_PCT = 1.5
POOL_WINDOWS = (2, 4, 8, 16)
N_POOL_GROUPS = len(POOL_WINDOWS)
POOL_GROUP = D_POOL // N_POOL_GROUPS
D_HY_STREAMS = (HYENA_ORDER + 1) * D_HYENA
D_IN_PROJ = D_HY_STREAMS + D_HYENA + 2 * D_POOL
DEEPNORM_ALPHA = (2.0 * DEPTH) ** 0.25
DEEPNORM_BETA = (8.0 * DEPTH) ** -0.25
LN_EPS = 1e-5

kernel_name = "hybrid_hyena_pool_adaln_deepnorm_encoder"


def _layer_norm(x, g, b):
    xf = x.astype(jnp.float32)
    mu = jnp.mean(xf, axis=-1, keepdims=True)
    var = jnp.mean(jnp.square(xf - mu), axis=-1, keepdims=True)
    y = (xf - mu) * lax.rsqrt(var + LN_EPS)
    return (y * g.astype(jnp.float32) + b.astype(jnp.float32)).astype(x.dtype)


def _short_conv(u, w, b):
    L = u.shape[1]
    pad = SHORT_CONV // 2
    up = jnp.pad(u, ((0, 0), (pad, SHORT_CONV - 1 - pad), (0, 0)))
    out = b
    for j in range(SHORT_CONV):
        out = out + up[:, j:j + L] * w[j]
    return out


def _filter_pos_features(L):
    t = jnp.linspace(0.0, 1.0, L, dtype=jnp.float32)[:, None]
    bands = (FILTER_EMB - 1) // 2
    w = (2.0 * math.pi / L) * jnp.arange(L, dtype=jnp.float32)[:, None]
    f = jnp.linspace(1e-4, bands - 1, bands, dtype=jnp.float32)[None, :]
    z = jnp.concatenate([t, jnp.cos(f * w), -jnp.sin(f * w)], axis=-1)
    return t, z


def _decay_window(t):
    min_decay = math.log(DECAY_TARGET) / SLOW_DECAY_PCT
    max_decay = math.log(DECAY_TARGET) / FAST_DECAY_PCT
    deltas = jnp.linspace(min_decay, max_decay, D_HYENA, dtype=jnp.float32)
    return jnp.exp(-t * jnp.abs(deltas)[None, :])


def _hyena_filters_freq(z_pos, decay, w1, b1, w_inner, b_inner, w_out, freq):
    L = z_pos.shape[0]
    f32 = jnp.float32
    fr = freq.astype(f32)
    h = jnp.sin(fr * (z_pos @ w1.astype(f32) + b1.astype(f32)))
    for i in range(FILTER_INNER):
        h = jnp.sin(fr * (h @ w_inner[i].astype(f32) + b_inner[i].astype(f32)))
    k = (h @ w_out.astype(f32)).reshape(L, HYENA_ORDER, 2, D_HYENA)
    k = k * decay[:, None, None, :]
    k2 = jnp.concatenate([k[:, :, 0], k[::-1, :, 1]], axis=0)
    return jnp.fft.rfft(k2, n=2 * L, axis=0)


def _bidir_long_conv(u, k_freq, bias):
    L = u.shape[1]
    u_freq = jnp.fft.rfft(u, n=2 * L, axis=1)
    y = jnp.fft.irfft(u_freq * k_freq[None], n=2 * L, axis=1)[:, :L]
    return y + u * bias.astype(jnp.float32)


def _hyena_branch(streams, conv_w, conv_b, k_freq, bias):
    s = _short_conv(streams, conv_w, conv_b).astype(jnp.float32)
    v, x1, x2 = jnp.split(s, HYENA_ORDER + 1, axis=-1)
    z = v
    for o, g in enumerate((x1, x2)):
        z = g * _bidir_long_conv(z, k_freq[:, o], bias[o])
    return z


def _pool_branch(u, pool_w, pool_scale):
    B, L, _ = u.shape
    uf = u.astype(jnp.float32)
    cs = jnp.pad(jnp.cumsum(uf, axis=1), ((0, 0), (1, 0), (0, 0)))
    pos = jnp.arange(L)
    groups = []
    for g, w in enumerate(POOL_WINDOWS):
        sl = slice(g * POOL_GROUP, (g + 1) * POOL_GROUP)
        hi = jnp.minimum(pos + w // 2, L)
        lo = jnp.maximum(pos - w // 2, 0)
        cnt = (hi - lo).astype(jnp.float32)[None, :, None]
        mean = (cs[:, hi, sl] - cs[:, lo, sl]) / cnt
        groups.append(mean - uf[..., sl])
    p = jnp.stack(groups, axis=2)
    p = jnp.einsum('blgc,gcd->blgd', p, pool_w.astype(jnp.float32)).reshape(B, L, D_POOL)
    return p * pool_scale.astype(jnp.float32)


def setup_inputs(seed: int = 0) -> dict:
    key = jax.random.key(seed)
    ks = jax.random.split(key, 21)
    f32 = jnp.float32

    def nrm(k, shape, scale):
        return jax.random.normal(k, shape, f32) * scale

    n = DEPTH
    return {
        "x": nrm(ks[0], (BATCH, SEQ, D_MODEL), 1.0),
        "c": nrm(ks[1], (BATCH, D_MODEL), 1.0),
        "w_ada": nrm(ks[2], (n, D_MODEL, 3 * D_MODEL), 0.5 * D_MODEL ** -0.5),
        "b_ada": nrm(ks[3], (n, 3 * D_MODEL), 0.02),
        "w_in": nrm(ks[4], (n, D_MODEL, D_IN_PROJ), D_MODEL ** -0.5),
        "b_in": nrm(ks[5], (n, D_IN_PROJ), 0.02),
        "conv_w": nrm(ks[6], (n, SHORT_CONV, D_HY_STREAMS), SHORT_CONV ** -0.5),
        "conv_b": nrm(ks[7], (n, D_HY_STREAMS), 0.02),
        "filt_w1": nrm(ks[8], (n, FILTER_EMB, FILTER_HIDDEN), FILTER_EMB ** -0.5),
        "filt_b1": nrm(ks[9], (n, FILTER_HIDDEN), 0.1),
        "filt_w_inner": nrm(ks[10], (n, FILTER_INNER, FILTER_HIDDEN, FILTER_HIDDEN), FILTER_HIDDEN ** -0.5),
        "filt_b_inner": nrm(ks[11], (n, FILTER_INNER, FILTER_HIDDEN), 0.1),
        "filt_w_out": nrm(ks[12], (n, FILTER_HIDDEN, HYENA_ORDER * 2 * D_HYENA),
                          FILTER_HIDDEN ** -0.5 * (SEQ / 8.0) ** -0.5),
        "filt_freq": 1.0 + nrm(ks[13], (n, FILTER_HIDDEN), 0.1),
        "hyena_bias": nrm(ks[14], (n, HYENA_ORDER, D_HYENA), 1.0),
        "pool_w": nrm(ks[15], (n, N_POOL_GROUPS, POOL_GROUP, POOL_GROUP), POOL_GROUP ** -0.5),
        "pool_scale": 1.0 + nrm(ks[16], (n, D_POOL), 0.1),
        "w_out": nrm(ks[17], (n, D_MIX, D_MODEL), D_MIX ** -0.5 * DEEPNORM_BETA),
        "b_out": nrm(ks[18], (n, D_MODEL), 0.02),
        "ln_g": 1.0 + nrm(ks[19], (n, D_MODEL), 0.05),
        "ln_b": nrm(ks[20], (n, D_MODEL), 0.02),
    }


def reference(x, c, w_ada, b_ada, w_in, b_in, conv_w, conv_b, filt_w1, filt_b1,
              filt_w_inner, filt_b_inner, filt_w_out, filt_freq, hyena_bias,
              pool_w, pool_scale, w_out, b_out, ln_g, ln_b):
    L = x.shape[1]
    t, z_pos = _filter_pos_features(L)
    decay = _decay_window(t)
    splits = [D_HY_STREAMS, D_HY_STREAMS + D_HYENA, D_HY_STREAMS + D_HYENA + D_POOL]
    h = x
    for layer in range(DEPTH):
        mod = jax.nn.silu(c) @ w_ada[layer] + b_ada[layer]
        shift, scale, gate = jnp.split(mod, 3, axis=-1)
        u = h * (1.0 + scale[:, None, :]) + shift[:, None, :]
        proj = u @ w_in[layer] + b_in[layer]
        hy_streams, hy_gate, pool_in, pool_gate = jnp.split(proj, splits, axis=-1)
        k_freq = _hyena_filters_freq(z_pos, decay, filt_w1[layer], filt_b1[layer],
                                     filt_w_inner[layer], filt_b_inner[layer],
                                     filt_w_out[layer], filt_freq[layer])
        y_h = _hyena_branch(hy_streams, conv_w[layer], conv_b[layer], k_freq, hyena_bias[layer])
        y_p = _pool_branch(pool_in, pool_w[layer], pool_scale[layer])
        y = jnp.concatenate([y_h * jax.nn.silu(hy_gate.astype(jnp.float32)),
                             y_p * jax.nn.silu(pool_gate.astype(jnp.float32))],
                            axis=-1).astype(x.dtype)
        out = gate[:, None, :] * (y @ w_out[layer] + b_out[layer])
        h = _layer_norm(DEEPNORM_ALPHA * h + out, ln_g[layer], ln_b[layer])
    return h
```

```python
import functools
import math

import jax
import jax.numpy as jnp
from jax import lax
from jax.experimental import pallas as pl
from jax.experimental.pallas import tpu as pltpu

F32 = jnp.float32
BF16 = jnp.bfloat16
HIGHEST = lax.Precision.HIGHEST

POOL_WINDOWS = (2, 4, 8, 16)
LN_EPS = 1e-5
DECAY_TARGET = 1e-2
FAST_DECAY_PCT = 0.3
SLOW_DECAY_PCT = 1.5

LANES = 128
SUBLANES = 8
HALO = 16
MINOR_LEN = 128
PHASE_STEPS = 8
VMEM_LIMIT = 56 * 1024 * 1024


def _silu(x):
    return x * jax.nn.sigmoid(x)


def _dft_tables(seq, n2len):
    n1h = seq // n2len
    n_fft = 2 * seq
    n2i = jnp.arange(n2len, dtype=jnp.int32)
    k1 = jnp.arange(n1h, dtype=jnp.int32)

    def angle(n1count):
        n1 = jnp.arange(n1count, dtype=jnp.int32)
        n = n2i[:, None, None] + n2len * n1[None, None, :]
        m = (n * (2 * k1[None, :, None] + 1)) % (2 * n_fft)
        return m.astype(F32) * (2.0 * math.pi / (2 * n_fft))

    th = angle(n1h)
    cos_t, sin_t = jnp.cos(th), jnp.sin(th)
    g_fwd = jnp.concatenate([cos_t, -sin_t], axis=1).astype(BF16)
    g_inv = (2.0 / n_fft) * jnp.concatenate(
        [jnp.swapaxes(cos_t, 1, 2), -jnp.swapaxes(sin_t, 1, 2)], axis=2)
    g_inv = g_inv.astype(BF16)
    thf = angle(2 * n1h)
    g_full = jnp.concatenate([jnp.cos(thf), -jnp.sin(thf)], axis=1).astype(BF16)

    m = (n2i[:, None] * n2i[None, :]) % n2len
    phi = m.astype(F32) * (2.0 * math.pi / n2len)
    cm, sm = jnp.cos(phi), jnp.sin(phi)
    h2 = jnp.block([[cm, sm], [-sm, cm]]).astype(BF16)
    h2i = jnp.block([[cm, -sm], [sm, cm]]).astype(BF16)
    return g_fwd, g_inv, g_full, h2, h2i


def _filter_pos_features(seq, emb):
    t = jnp.linspace(0.0, 1.0, seq, dtype=F32)[:, None]
    bands = (emb - 1) // 2
    w = (2.0 * math.pi / seq) * jnp.arange(seq, dtype=F32)[:, None]
    f = jnp.linspace(1e-4, bands - 1, bands, dtype=F32)[None, :]
    return jnp.concatenate([t, jnp.cos(f * w), -jnp.sin(f * w)], axis=-1)


def _mod_kernel(c_ref, w_ref, b_ref, o_ref):
    s = _silu(c_ref[...])
    o_ref[...] = jnp.dot(s, w_ref[...], precision=HIGHEST,
                         preferred_element_type=F32) + b_ref[...]


def _modulation(c, w_ada, b_ada):
    bsz, d = c.shape
    n_out = w_ada.shape[1]
    return pl.pallas_call(
        _mod_kernel,
        grid=(n_out // d,),
        in_specs=[pl.BlockSpec((bsz, d), lambda j: (0, 0)),
                  pl.BlockSpec((d, d), lambda j: (0, j)),
                  pl.BlockSpec((1, d), lambda j: (0, j))],
        out_specs=pl.BlockSpec((bsz, d), lambda j: (0, j)),
        out_shape=jax.ShapeDtypeStruct((bsz, n_out), F32),
        name="adaln_mod",
    )(c, w_ada, b_ada.reshape(1, n_out))


def _store_permuted(o_ref, val, n2len):
    for cb in range(val.shape[1] // LANES):
        for j in range(SUBLANES):
            o_ref[cb, pl.ds(j, n2len, stride=SUBLANES), :] = val[
                j * n2len:(j + 1) * n2len, cb * LANES:(cb + 1) * LANES]


def _load_minor(x_ref, j):
    blk = x_ref[:, j * SUBLANES:(j + 1) * SUBLANES, :]
    return blk.reshape(blk.shape[0] * SUBLANES, blk.shape[2])


def _proj_kernel(xp_ref, xc_ref, xn_ref, mod_ref, w_ref, b_ref, cw_ref, cb_ref, pw_ref,
                 ps_ref, v_ref, x1_ref, x2_ref, g_ref, yp_ref, p_scr, *, tile, n2len, ch,
                 seq):
    i = pl.program_id(1)
    nt = pl.num_programs(1)
    shift = mod_ref[0:1, :]
    scale1 = 1.0 + mod_ref[1:2, :]
    xe = jnp.concatenate([xp_ref[...], xc_ref[...], xn_ref[...]], axis=0)
    ue = (xe * scale1 + shift).astype(BF16)
    uc = ue[HALO:HALO + tile, :]
    row = lax.broadcasted_iota(jnp.int32, (tile + 2 * HALO, 1), 0)
    valid = ((row >= HALO) | (i > 0)) & ((row < tile + HALO) | (i < nt - 1))

    for k, o_ref in enumerate((v_ref, x1_ref, x2_ref)):
        cols = slice(k * ch, (k + 1) * ch)
        p = jnp.dot(ue, w_ref[:, cols], preferred_element_type=F32) + b_ref[:, cols]
        p_scr[...] = jnp.where(valid, p, 0.0)
        s = (cb_ref[:, cols]
             + cw_ref[0:1, cols] * p_scr[HALO - 1:HALO - 1 + tile, :]
             + cw_ref[1:2, cols] * p_scr[HALO:HALO + tile, :]
             + cw_ref[2:3, cols] * p_scr[HALO + 1:HALO + 1 + tile, :])
        _store_permuted(o_ref, s, n2len)

    hg = jnp.dot(uc, w_ref[:, 3 * ch:4 * ch], preferred_element_type=F32) + b_ref[:, 3 * ch:4 * ch]
    _store_permuted(g_ref, _silu(hg), n2len)

    dp = ps_ref.shape[1]
    c0 = 4 * ch
    pin = jnp.dot(ue, w_ref[:, c0:c0 + dp], preferred_element_type=F32) + b_ref[:, c0:c0 + dp]
    p_scr[...] = jnp.where(valid, pin, 0.0)
    pgate = jnp.dot(uc, w_ref[:, c0 + dp:c0 + 2 * dp], preferred_element_type=F32) + b_ref[:, c0 + dp:c0 + 2 * dp]
    pos = i * tile + lax.broadcasted_iota(jnp.int32, (tile, 1), 0)
    pg = dp // len(POOL_WINDOWS)
    groups = []
    for g, win in enumerate(POOL_WINDOWS):
        lanes = slice(g * pg, (g + 1) * pg)
        half = win // 2
        acc = p_scr[HALO - half:HALO - half + tile, lanes]
        for d in range(-half + 1, half):
            acc = acc + p_scr[HALO + d:HALO + d + tile, lanes]
        cnt = (jnp.minimum(pos + half, seq) - jnp.maximum(pos - half, 0)).astype(F32)
        diff = acc / cnt - p_scr[HALO:HALO + tile, lanes]
        groups.append(jnp.dot(diff.astype(BF16), pw_ref[g], preferred_element_type=F32))
    yp = jnp.concatenate(groups, axis=1) * ps_ref[...] * _silu(pgate)
    yp_ref[...] = yp.astype(yp_ref.dtype)


def _input_projection(x, mod3, w_in, b_in, conv_w, conv_b, pool_w, pool_scale, n2len):
    bsz, seq, d = x.shape
    ch = conv_w.shape[1] // 3
    dp = pool_scale.shape[0]
    n1h = seq // n2len
    tile = SUBLANES * n2len
    nt = seq // tile
    hb = tile // HALO
    n_proj = w_in.shape[1]
    cbn = ch // LANES
    perm = jax.ShapeDtypeStruct((bsz, n1h // SUBLANES, cbn, tile, LANES), F32)
    perm_spec = pl.BlockSpec((None, None, cbn, tile, LANES), lambda b, i: (b, i, 0, 0, 0))
    full = lambda shape: pl.BlockSpec(shape, lambda b, i: (0,) * len(shape))
    kern = functools.partial(_proj_kernel, tile=tile, n2len=n2len, ch=ch, seq=seq)
    return pl.pallas_call(
        kern,
        grid=(bsz, nt),
        in_specs=[
            pl.BlockSpec((None, HALO, d), lambda b, i: (b, jnp.maximum(i * hb - 1, 0), 0)),
            pl.BlockSpec((None, tile, d), lambda b, i: (b, i, 0)),
            pl.BlockSpec((None, HALO, d), lambda b, i: (b, jnp.minimum((i + 1) * hb, seq // HALO - 1), 0)),
            pl.BlockSpec((None, 3, d), lambda b, i: (b, 0, 0)),
            full((d, n_proj)), full((1, n_proj)), full((3, 3 * ch)), full((1, 3 * ch)),
            full(pool_w.shape), full((1, dp)),
        ],
        out_specs=[perm_spec, perm_spec, perm_spec, perm_spec,
                   pl.BlockSpec((None, tile, dp), lambda b, i: (b, i, 0))],
        out_shape=[perm, perm, perm, perm, jax.ShapeDtypeStruct((bsz, seq, dp), BF16)],
        scratch_shapes=[pltpu.VMEM((tile + 2 * HALO, ch), F32)],
        compiler_params=pltpu.CompilerParams(
            dimension_semantics=("parallel", "arbitrary"), vmem_limit_bytes=VMEM_LIMIT),
        name="in_proj_conv_pool",
    )(x, x, x, mod3, w_in.astype(BF16), b_in.reshape(1, n_proj), conv_w,
      conv_b.reshape(1, 3 * ch), pool_w.astype(BF16), pool_scale.reshape(1, dp))


def _filter_mlp_kernel(z_ref, w1_ref, b1_ref, wi_ref, bi_ref, fr_ref, wo_ref, ad_ref, o_ref,
                       *, n2len, tiles_per_dir):
    z = z_ref[...]
    fr = fr_ref[...]
    h = jnp.sin(fr * (jnp.dot(z, w1_ref[...], precision=HIGHEST,
                              preferred_element_type=F32) + b1_ref[...]))
    for l in range(wi_ref.shape[0]):
        h = jnp.sin(fr * (jnp.dot(h, wi_ref[l], precision=HIGHEST,
                                  preferred_element_type=F32) + bi_ref[l:l + 1, :]))
    k = jnp.dot(h, wo_ref[...], precision=HIGHEST, preferred_element_type=F32)
    decay = jnp.exp(-z[:, 0:1] * ad_ref[...])
    sign = jnp.where(pl.program_id(0) < tiles_per_dir, 1.0, -1.0)
    _store_permuted(o_ref, k * decay * sign, n2len)


def _filter_taps(seq, n2len, w1, b1, w_inner, b_inner, w_out, freq, n_ch, order):
    emb, hid = w1.shape
    n_inner = w_inner.shape[0]
    oc = order * n_ch
    z = _filter_pos_features(seq, emb)
    z_ext = jnp.pad(jnp.concatenate([z, z[::-1]], axis=0), ((0, 0), (0, LANES - emb)))
    ph = LANES - hid
    w1p = jnp.pad(w1, ((0, LANES - emb), (0, ph)))
    b1p = jnp.pad(b1, (0, ph)).reshape(1, LANES)
    wip = jnp.pad(w_inner, ((0, 0), (0, ph), (0, ph)))
    bip = jnp.pad(b_inner, ((0, 0), (0, ph)))
    frp = jnp.pad(freq, (0, ph), constant_values=1.0).reshape(1, LANES)
    wo = jnp.transpose(w_out.reshape(hid, order, 2, n_ch), (2, 0, 1, 3)).reshape(2, hid, oc)
    wop = jnp.pad(wo, ((0, 0), (0, ph), (0, 0)))
    min_decay = math.log(DECAY_TARGET) / SLOW_DECAY_PCT
    max_decay = math.log(DECAY_TARGET) / FAST_DECAY_PCT
    absdelta = jnp.abs(jnp.linspace(min_decay, max_decay, n_ch, dtype=F32))
    absdelta = jnp.tile(absdelta, order).reshape(1, oc)

    tile = SUBLANES * n2len
    tiles_per_dir = seq // tile
    n1_full = 2 * seq // n2len
    full = lambda shape: pl.BlockSpec(shape, lambda i: (0,) * len(shape))
    kern = functools.partial(_filter_mlp_kernel, n2len=n2len, tiles_per_dir=tiles_per_dir)
    return pl.pallas_call(
        kern,
        grid=(2 * tiles_per_dir,),
        in_specs=[
            pl.BlockSpec((tile, LANES), lambda i: (i, 0)),
            full((LANES, LANES)), full((1, LANES)), full((n_inner, LANES, LANES)),
            full((n_inner, LANES)), full((1, LANES)),
            pl.BlockSpec((None, LANES, oc), lambda i: (i // tiles_per_dir, 0, 0)),
            full((1, oc)),
        ],
        out_specs=pl.BlockSpec((None, oc // LANES, tile, LANES), lambda i: (i, 0, 0, 0)),
        out_shape=jax.ShapeDtypeStruct((n1_full // SUBLANES, oc // LANES, tile, LANES), F32),
        compiler_params=pltpu.CompilerParams(
            dimension_semantics=("arbitrary",), vmem_limit_bytes=VMEM_LIMIT),
        name="filter_mlp",
    )(z_ext, w1p, b1p, wip, bip, frp, wop, absdelta)


def _stage1(a_scr, g_ref, x_ref, base, cn2, n_rows, pitch):
    for j in range(cn2):
        r = jnp.dot(g_ref[j], _load_minor(x_ref, j).astype(BF16), preferred_element_type=F32)
        a_scr[pl.ds(base + j, n_rows, stride=pitch), :] = r


def _load_spectrum_rows(a_scr, k1, n1h, n2len, pitch):
    re0 = pl.multiple_of(k1 * pitch, SUBLANES)
    im0 = pl.multiple_of((n1h + k1) * pitch, SUBLANES)
    rows = jnp.concatenate([a_scr[pl.ds(re0, n2len), :], a_scr[pl.ds(im0, n2len), :]], axis=0)
    return rows, re0, im0


def _filter_fft_kernel(h_ref, g_ref, h2_ref, o_ref, a_scr, *, ns, cn2, ck1, n1h, n2len, pitch):
    s = pl.program_id(1)

    @pl.when(s < ns)
    def _():
        _stage1(a_scr, g_ref, h_ref, s * cn2, cn2, 2 * n1h, pitch)

    @pl.when(s >= ns)
    def _():
        def body(j, carry):
            rows, _, _ = _load_spectrum_rows(a_scr, (s - ns) * ck1 + j, n1h, n2len, pitch)
            o_ref[j] = jnp.dot(h2_ref[...], rows.astype(BF16), preferred_element_type=F32)
            return carry
        lax.fori_loop(0, ck1, body, 0)


def _filter_spectrum(taps, g_full, h2, n2len):
    groups, ocb, _, _ = taps.shape
    oc = ocb * LANES
    n1_full = groups * SUBLANES
    n1h = n1_full // 2
    ns = PHASE_STEPS
    cn2, ck1 = n2len // ns, n1h // ns
    pitch = n2len + SUBLANES
    kern = functools.partial(_filter_fft_kernel, ns=ns, cn2=cn2, ck1=ck1, n1h=n1h,
                             n2len=n2len, pitch=pitch)
    return pl.pallas_call(
        kern,
        grid=(oc // LANES, 2 * ns),
        in_specs=[
            pl.BlockSpec((groups, None, cn2 * SUBLANES, LANES),
                         lambda c, s: (0, c, jnp.minimum(s, ns - 1), 0)),
            pl.BlockSpec((cn2, n1_full, n1_full), lambda c, s: (jnp.minimum(s, ns - 1), 0, 0)),
            pl.BlockSpec((2 * n2len, 2 * n2len), lambda c, s: (0, 0)),
        ],
        out_specs=pl.BlockSpec((ck1, 2 * n2len, LANES), lambda c, s: (jnp.maximum(s - ns, 0), 0, c)),
        out_shape=jax.ShapeDtypeStruct((n1h, 2 * n2len, oc), F32),
        scratch_shapes=[pltpu.VMEM((n1_full * pitch, LANES), F32)],
        compiler_params=pltpu.CompilerParams(
            dimension_semantics=("parallel", "arbitrary"), vmem_limit_bytes=VMEM_LIMIT),
        name="filter_spectrum",
    )(taps, g_full, h2)


def _long_conv_kernel(*refs, ns, cn2, ck1, n1h, n2len, pitch, has_gate):
    if has_gate:
        (vf_ref, g_ref, kf_ref, h2_ref, h2i_ref, gi_ref, vb_ref, xm_ref, gate_ref, hb_ref,
         o_ref, a_scr) = refs
    else:
        (vf_ref, g_ref, kf_ref, h2_ref, h2i_ref, gi_ref, vb_ref, xm_ref, hb_ref,
         o_ref, a_scr) = refs
        gate_ref = None
    s = pl.program_id(2)

    @pl.when(s < ns)
    def _():
        _stage1(a_scr, g_ref, vf_ref, s * cn2, cn2, 2 * n1h, pitch)

    @pl.when((s >= ns) & (s < 2 * ns))
    def _():
        def body(j, carry):
            rows, re0, im0 = _load_spectrum_rows(a_scr, (s - ns) * ck1 + j, n1h, n2len, pitch)
            u = jnp.dot(h2_ref[...], rows.astype(BF16), preferred_element_type=F32)
            ur, ui = u[:n2len], u[n2len:]
            kf = kf_ref[j]
            kr, ki = kf[:n2len], kf[n2len:]
            y = jnp.concatenate([ur * kr - ui * ki, ur * ki + ui * kr], axis=0)
            bb = jnp.dot(h2i_ref[...], y.astype(BF16), preferred_element_type=F32)
            a_scr[pl.ds(re0, n2len), :] = bb[:n2len]
            a_scr[pl.ds(im0, n2len), :] = bb[n2len:]
            return carry
        lax.fori_loop(0, ck1, body, 0)

    @pl.when(s >= 2 * ns)
    def _():
        base = (s - 2 * ns) * cn2
        for j in range(cn2):
            rows = a_scr[pl.ds(base + j, 2 * n1h, stride=pitch), :]
            y = jnp.dot(gi_ref[j], rows.astype(BF16), preferred_element_type=F32)
            z = _load_minor(xm_ref, j) * (y + hb_ref[...] * _load_minor(vb_ref, j))
            if gate_ref is not None:
                z = z * _load_minor(gate_ref, j)
            o_ref[:, j * SUBLANES:(j + 1) * SUBLANES, :] = z.reshape(
                n1h // SUBLANES, SUBLANES, z.shape[1])


def _long_conv(u_perm, mult_perm, gate_perm, kf, hbias, order_idx, tables):
    g_fwd, g_inv, _, h2, h2i = tables
    bsz, groups, cbn, tile, _ = u_perm.shape
    n2len = tile // SUBLANES
    n1h = groups * SUBLANES
    ns = PHASE_STEPS
    cn2, ck1 = n2len // ns, n1h // ns
    pitch = n2len + SUBLANES
    has_gate = gate_perm is not None

    def late(s):
        return jnp.clip(s - 2 * ns, 0, ns - 1)

    data_early = pl.BlockSpec((None, groups, None, cn2 * SUBLANES, LANES),
                              lambda c, b, s: (b, 0, c, jnp.minimum(s, ns - 1), 0))
    data_late = pl.BlockSpec((None, groups, None, cn2 * SUBLANES, LANES),
                             lambda c, b, s: (b, 0, c, late(s), 0))
    in_specs = [
        data_early,
        pl.BlockSpec((cn2, 2 * n1h, n1h), lambda c, b, s: (jnp.minimum(s, ns - 1), 0, 0)),
        pl.BlockSpec((ck1, 2 * n2len, LANES),
                     lambda c, b, s: (jnp.clip(s - ns, 0, ns - 1), 0, order_idx * cbn + c)),
        pl.BlockSpec((2 * n2len, 2 * n2len), lambda c, b, s: (0, 0)),
        pl.BlockSpec((2 * n2len, 2 * n2len), lambda c, b, s: (0, 0)),
        pl.BlockSpec((cn2, n1h, 2 * n1h), lambda c, b, s: (late(s), 0, 0)),
        data_late, data_late,
    ]
    args = [u_perm, g_fwd, kf, h2, h2i, g_inv, u_perm, mult_perm]
    if has_gate:
        in_specs.append(data_late)
        args.append(gate_perm)
    in_specs.append(pl.BlockSpec((None, 1, LANES), lambda c, b, s: (order_idx, 0, c)))
    args.append(hbias)
    kern = functools.partial(_long_conv_kernel, ns=ns, cn2=cn2, ck1=ck1, n1h=n1h,
                             n2len=n2len, pitch=pitch, has_gate=has_gate)
    return pl.pallas_call(
        kern,
        grid=(cbn, bsz, 3 * ns),
        in_specs=in_specs,
        out_specs=data_late,
        out_shape=jax.ShapeDtypeStruct(u_perm.shape, F32),
        scratch_shapes=[pltpu.VMEM((2 * n1h * pitch, LANES), F32)],
        compiler_params=pltpu.CompilerParams(
            dimension_semantics=("parallel", "parallel", "arbitrary"),
            vmem_limit_bytes=VMEM_LIMIT),
        name="long_conv",
    )(*args)


def _out_kernel(x_ref, yh_ref, yp_ref, mod_ref, w_ref, b_ref, g_ref, beta_ref, o_ref,
                *, n2len, alpha):
    cbn = yh_ref.shape[0]
    ch = cbn * LANES
    yh = jnp.concatenate(
        [jnp.concatenate([yh_ref[cb, pl.ds(j, n2len, stride=SUBLANES), :]
                          for cb in range(cbn)], axis=1)
         for j in range(SUBLANES)], axis=0)
    acc = jnp.dot(yh.astype(BF16), w_ref[0:ch, :], preferred_element_type=F32)
    acc = acc + jnp.dot(yp_ref[...], w_ref[ch:, :], preferred_element_type=F32)
    h = alpha * x_ref[...] + mod_ref[2:3, :] * (acc + b_ref[...])
    mu = jnp.mean(h, axis=-1, keepdims=True)
    hc = h - mu
    var = jnp.mean(hc * hc, axis=-1, keepdims=True)
    o_ref[...] = hc * lax.rsqrt(var + LN_EPS) * g_ref[...] + beta_ref[...]


def _output_projection(x, yh_perm, yp, mod3, w_out, b_out, ln_g, ln_b, alpha):
    bsz, seq, d = x.shape
    _, _, cbn, tile, _ = yh_perm.shape
    ch = cbn * LANES
    n2len = tile // SUBLANES
    dp = yp.shape[2]
    full = lambda shape: pl.BlockSpec(shape, lambda b, i: (0,) * len(shape))
    kern = functools.partial(_out_kernel, n2len=n2len, alpha=alpha)
    return pl.pallas_call(
        kern,
        grid=(bsz, seq // tile),
        in_specs=[
            pl.BlockSpec((None, tile, d), lambda b, i: (b, i, 0)),
            pl.BlockSpec((None, None, cbn, tile, LANES), lambda b, i: (b, i, 0, 0, 0)),
            pl.BlockSpec((None, tile, dp), lambda b, i: (b, i, 0)),
            pl.BlockSpec((None, 3, d), lambda b, i: (b, 0, 0)),
            full((ch + dp, d)), full((1, d)), full((1, d)), full((1, d)),
        ],
        out_specs=pl.BlockSpec((None, tile, d), lambda b, i: (b, i, 0)),
        out_shape=jax.ShapeDtypeStruct(x.shape, x.dtype),
        compiler_params=pltpu.CompilerParams(
            dimension_semantics=("parallel", "arbitrary"), vmem_limit_bytes=VMEM_LIMIT),
        name="out_proj_deepnorm",
    )(x, yh_perm, yp, mod3, w_out.astype(BF16), b_out.reshape(1, d), ln_g.reshape(1, d),
      ln_b.reshape(1, d))


def _forward(x, c, w_ada, b_ada, w_in, b_in, conv_w, conv_b, filt_w1, filt_b1, filt_w_inner,
             filt_b_inner, filt_w_out, filt_freq, hyena_bias, pool_w, pool_scale, w_out,
             b_out, ln_g, ln_b, *, n2len):
    bsz, seq, d = x.shape
    depth = w_ada.shape[0]
    order, n_ch = hyena_bias.shape[1], hyena_bias.shape[2]
    alpha = (2.0 * depth) ** 0.25
    tables = _dft_tables(seq, n2len)
    h = x
    for layer in range(depth):
        mod3 = _modulation(c, w_ada[layer], b_ada[layer]).reshape(bsz, 3, d)
        v, x1, x2, gate, yp = _input_projection(
            h, mod3, w_in[layer], b_in[layer], conv_w[layer], conv_b[layer], pool_w[layer],
            pool_scale[layer], n2len)
        taps = _filter_taps(seq, n2len, filt_w1[layer], filt_b1[layer], filt_w_inner[layer],
                            filt_b_inner[layer], filt_w_out[layer], filt_freq[layer], n_ch,
                            order)
        kf = _filter_spectrum(taps, tables[2], tables[3], n2len)
        hbias = hyena_bias[layer].reshape(order, 1, n_ch)
        z = v
        mults = (x1, x2)
        for o in range(order):
            z = _long_conv(z, mults[o], gate if o == order - 1 else None, kf, hbias, o, tables)
        h = _output_projection(h, z, yp, mod3, w_out[layer], b_out[layer], ln_g[layer],
                               ln_b[layer], alpha)
    return h


def kernel(x, c, w_ada, b_ada, w_in, b_in, conv_w, conv_b, filt_w1, filt_b1, filt_w_inner,
           filt_b_inner, filt_w_out, filt_freq, hyena_bias, pool_w, pool_scale, w_out, b_out,
           ln_g, ln_b):
    return _forward(x, c, w_ada, b_ada, w_in, b_in, conv_w, conv_b, filt_w1, filt_b1,
                    filt_w_inner, filt_b_inner, filt_w_out, filt_freq, hyena_bias, pool_w,
                    pool_scale, w_out, b_out, ln_g, ln_b, n2len=MINOR_LEN)
```

```python
import functools
import math

import jax
import jax.numpy as jnp
from jax import lax
from jax.experimental import pallas as pl
from jax.experimental.pallas import tpu as pltpu

F32 = jnp.float32
BF16 = jnp.bfloat16
HIGHEST = lax.Precision.HIGHEST

POOL_WINDOWS = (2, 4, 8, 16)
LN_EPS = 1e-5
DECAY_TARGET = 1e-2
FAST_DECAY_PCT = 0.3
SLOW_DECAY_PCT = 1.5

LANES = 128
SUBLANES = 8
HALO = 16
MINOR_LEN = 128
PHASE_STEPS = 8
VMEM_LIMIT = 56 * 1024 * 1024


def _silu(x):
    return x * jax.nn.sigmoid(x)


def _dft_tables(seq, n2len):
    n1h = seq // n2len
    n_fft = 2 * seq
    n2i = jnp.arange(n2len, dtype=jnp.int32)
    k1 = jnp.arange(n1h, dtype=jnp.int32)

    def angle(n1count):
        n1 = jnp.arange(n1count, dtype=jnp.int32)
        n = n2i[:, None, None] + n2len * n1[None, None, :]
        m = (n * (2 * k1[None, :, None] + 1)) % (2 * n_fft)
        return m.astype(F32) * (2.0 * math.pi / (2 * n_fft))

    th = angle(n1h)
    cos_t, sin_t = jnp.cos(th), jnp.sin(th)
    g_fwd = jnp.concatenate([cos_t, -sin_t], axis=1).astype(BF16)
    g_inv = (2.0 / n_fft) * jnp.concatenate(
        [jnp.swapaxes(cos_t, 1, 2), -jnp.swapaxes(sin_t, 1, 2)], axis=2)
    g_inv = g_inv.astype(BF16)
    thf = angle(2 * n1h)
    g_full = jnp.concatenate([jnp.cos(thf), -jnp.sin(thf)], axis=1).astype(BF16)

    m = (n2i[:, None] * n2i[None, :]) % n2len
    phi = m.astype(F32) * (2.0 * math.pi / n2len)
    cm, sm = jnp.cos(phi), jnp.sin(phi)
    h2 = jnp.block([[cm, sm], [-sm, cm]]).astype(BF16)
    h2i = jnp.block([[cm, -sm], [sm, cm]]).astype(BF16)
    return g_fwd, g_inv, g_full, h2, h2i


def _filter_pos_features(seq, emb):
    t = jnp.linspace(0.0, 1.0, seq, dtype=F32)[:, None]
    bands = (emb - 1) // 2
    w = (2.0 * math.pi / seq) * jnp.arange(seq, dtype=F32)[:, None]
    f = jnp.linspace(1e-4, bands - 1, bands, dtype=F32)[None, :]
    return jnp.concatenate([t, jnp.cos(f * w), -jnp.sin(f * w)], axis=-1)


def _mod_kernel(c_ref, w_ref, b_ref, o_ref):
    s = _silu(c_ref[...])
    o_ref[...] = jnp.dot(s, w_ref[...], precision=HIGHEST,
                         preferred_element_type=F32) + b_ref[...]


def _modulation(c, w_ada, b_ada):
    bsz, d = c.shape
    n_out = w_ada.shape[1]
    return pl.pallas_call(
        _mod_kernel,
        grid=(n_out // d,),
        in_specs=[pl.BlockSpec((bsz, d), lambda j: (0, 0)),
                  pl.BlockSpec((d, d), lambda j: (0, j)),
                  pl.BlockSpec((1, d), lambda j: (0, j))],
        out_specs=pl.BlockSpec((bsz, d), lambda j: (0, j)),
        out_shape=jax.ShapeDtypeStruct((bsz, n_out), F32),
        name="adaln_mod",
    )(c, w_ada, b_ada.reshape(1, n_out))


def _store_permuted(o_ref, val, n2len):
    for cb in range(val.shape[1] // LANES):
        for j in range(SUBLANES):
            o_ref[cb, pl.ds(j, n2len, stride=SUBLANES), :] = val[
                j * n2len:(j + 1) * n2len, cb * LANES:(cb + 1) * LANES]


def _load_minor(x_ref, j):
    blk = x_ref[:, j * SUBLANES:(j + 1) * SUBLANES, :]
    return blk.reshape(blk.shape[0] * SUBLANES, blk.shape[2])


def _proj_kernel(xp_ref, xc_ref, xn_ref, mod_ref, w_ref, b_ref, cw_ref, cb_ref, pw_ref,
                 ps_ref, v_ref, x1_ref, x2_ref, g_ref, yp_ref, p_scr, *, tile, n2len, ch,
                 seq):
    i = pl.program_id(1)
    nt = pl.num_programs(1)
    shift = mod_ref[0:1, :]
    scale1 = 1.0 + mod_ref[1:2, :]
    xe = jnp.concatenate([xp_ref[...], xc_ref[...], xn_ref[...]], axis=0)
    ue = (xe * scale1 + shift).astype(BF16)
    uc = ue[HALO:HALO + tile, :]
    row = lax.broadcasted_iota(jnp.int32, (tile + 2 * HALO, 1), 0)
    valid = ((row >= HALO) | (i > 0)) & ((row < tile + HALO) | (i < nt - 1))

    for k, o_ref in enumerate((v_ref, x1_ref, x2_ref)):
        cols = slice(k * ch, (k + 1) * ch)
        p = jnp.dot(ue, w_ref[:, cols], preferred_element_type=F32) + b_ref[:, cols]
        p_scr[...] = jnp.where(valid, p, 0.0)
        s = (cb_ref[:, cols]
             + cw_ref[0:1, cols] * p_scr[HALO - 1:HALO - 1 + tile, :]
             + cw_ref[1:2, cols] * p_scr[HALO:HALO + tile, :]
             + cw_ref[2:3, cols] * p_scr[HALO + 1:HALO + 1 + tile, :])
        _store_permuted(o_ref, s, n2len)

    hg = jnp.dot(uc, w_ref[:, 3 * ch:4 * ch], preferred_element_type=F32) + b_ref[:, 3 * ch:4 * ch]
    _store_permuted(g_ref, _silu(hg), n2len)

    dp = ps_ref.shape[1]
    c0 = 4 * ch
    pin = jnp.dot(ue, w_ref[:, c0:c0 + dp], preferred_element_type=F32) + b_ref[:, c0:c0 + dp]
    p_scr[...] = jnp.where(valid, pin, 0.0)
    pgate = jnp.dot(uc, w_ref[:, c0 + dp:c0 + 2 * dp], preferred_element_type=F32) + b_ref[:, c0 + dp:c0 + 2 * dp]
    pos = i * tile + lax.broadcasted_iota(jnp.int32, (tile, 1), 0)
    pg = dp // len(POOL_WINDOWS)
    groups = []
    for g, win in enumerate(POOL_WINDOWS):
        lanes = slice(g * pg, (g + 1) * pg)
        half = win // 2
        acc = p_scr[HALO - half:HALO - half + tile, lanes]
        for d in range(-half + 1, half):
            acc = acc + p_scr[HALO + d:HALO + d + tile, lanes]
        cnt = (jnp.minimum(pos + half, seq) - jnp.maximum(pos - half, 0)).astype(F32)
        diff = acc / cnt - p_scr[HALO:HALO + tile, lanes]
        groups.append(jnp.dot(diff.astype(BF16), pw_ref[g], preferred_element_type=F32))
    yp = jnp.concatenate(groups, axis=1) * ps_ref[...] * _silu(pgate)
    yp_ref[...] = yp.astype(yp_ref.dtype)


def _input_projection(x, mod3, w_in, b_in, conv_w, conv_b, pool_w, pool_scale, n2len):
    bsz, seq, d = x.shape
    ch = conv_w.shape[1] // 3
    dp = pool_scale.shape[0]
    n1h = seq // n2len
    tile = SUBLANES * n2len
    nt = seq // tile
    hb = tile // HALO
    n_proj = w_in.shape[1]
    cbn = ch // LANES
    perm = jax.ShapeDtypeStruct((bsz, n1h // SUBLANES, cbn, tile, LANES), F32)
    perm_spec = pl.BlockSpec((None, None, cbn, tile, LANES), lambda b, i: (b, i, 0, 0, 0))
    full = lambda shape: pl.BlockSpec(shape, lambda b, i: (0,) * len(shape))
    kern = functools.partial(_proj_kernel, tile=tile, n2len=n2len, ch=ch, seq=seq)
    return pl.pallas_call(
        kern,
        grid=(bsz, nt),
        in_specs=[
            pl.BlockSpec((None, HALO, d), lambda b, i: (b, jnp.maximum(i * hb - 1, 0), 0)),
            pl.BlockSpec((None, tile, d), lambda b, i: (b, i, 0)),
            pl.BlockSpec((None, HALO, d), lambda b, i: (b, jnp.minimum((i + 1) * hb, seq // HALO - 1), 0)),
            pl.BlockSpec((None, 3, d), lambda b, i: (b, 0, 0)),
            full((d, n_proj)), full((1, n_proj)), full((3, 3 * ch)), full((1, 3 * ch)),
            full(pool_w.shape), full((1, dp)),
        ],
        out_specs=[perm_spec, perm_spec, perm_spec, perm_spec,
                   pl.BlockSpec((None, tile, dp), lambda b, i: (b, i, 0))],
        out_shape=[perm, perm, perm, perm, jax.ShapeDtypeStruct((bsz, seq, dp), BF16)],
        scratch_shapes=[pltpu.VMEM((tile + 2 * HALO, ch), F32)],
        compiler_params=pltpu.CompilerParams(
            dimension_semantics=("parallel", "arbitrary"), vmem_limit_bytes=VMEM_LIMIT),
        name="in_proj_conv_pool",
    )(x, x, x, mod3, w_in.astype(BF16), b_in.reshape(1, n_proj), conv_w,
      conv_b.reshape(1, 3 * ch), pool_w.astype(BF16), pool_scale.reshape(1, dp))


def _filter_mlp_kernel(z_ref, w1_ref, b1_ref, wi_ref, bi_ref, fr_ref, wo_ref, ad_ref, o_ref,
                       *, n2len, tiles_per_dir):
    z = z_ref[...]
    fr = fr_ref[...]
    h = jnp.sin(fr * (jnp.dot(z.astype(BF16), w1_ref[...],
                              preferred_element_type=F32) + b1_ref[...]))
    for l in range(wi_ref.shape[0]):
        h = jnp.sin(fr * (jnp.dot(h.astype(BF16), wi_ref[l],
                                  preferred_element_type=F32) + bi_ref[l:l + 1, :]))
    k = jnp.dot(h.astype(BF16), wo_ref[...], preferred_element_type=F32)
    decay = jnp.exp(-z[:, 0:1] * ad_ref[...])
    sign = jnp.where(pl.program_id(0) < tiles_per_dir, 1.0, -1.0)
    _store_permuted(o_ref, k * decay * sign, n2len)


def _filter_taps(seq, n2len, w1, b1, w_inner, b_inner, w_out, freq, n_ch, order):
    emb, hid = w1.shape
    n_inner = w_inner.shape[0]
    oc = order * n_ch
    z = _filter_pos_features(seq, emb)
    z_ext = jnp.pad(jnp.concatenate([z, z[::-1]], axis=0), ((0, 0), (0, LANES - emb)))
    ph = LANES - hid
    w1p = jnp.pad(w1, ((0, LANES - emb), (0, ph)))
    b1p = jnp.pad(b1, (0, ph)).reshape(1, LANES)
    wip = jnp.pad(w_inner, ((0, 0), (0, ph), (0, ph)))
    bip = jnp.pad(b_inner, ((0, 0), (0, ph)))
    frp = jnp.pad(freq, (0, ph), constant_values=1.0).reshape(1, LANES)
    wo = jnp.transpose(w_out.reshape(hid, order, 2, n_ch), (2, 0, 1, 3)).reshape(2, hid, oc)
    wop = jnp.pad(wo, ((0, 0), (0, ph), (0, 0)))
    min_decay = math.log(DECAY_TARGET) / SLOW_DECAY_PCT
    max_decay = math.log(DECAY_TARGET) / FAST_DECAY_PCT
    absdelta = jnp.abs(jnp.linspace(min_decay, max_decay, n_ch, dtype=F32))
    absdelta = jnp.tile(absdelta, order).reshape(1, oc)

    tile = SUBLANES * n2len
    tiles_per_dir = seq // tile
    n1_full = 2 * seq // n2len
    full = lambda shape: pl.BlockSpec(shape, lambda i: (0,) * len(shape))
    kern = functools.partial(_filter_mlp_kernel, n2len=n2len, tiles_per_dir=tiles_per_dir)
    return pl.pallas_call(
        kern,
        grid=(2 * tiles_per_dir,),
        in_specs=[
            pl.BlockSpec((tile, LANES), lambda i: (i, 0)),
            full((LANES, LANES)), full((1, LANES)), full((n_inner, LANES, LANES)),
            full((n_inner, LANES)), full((1, LANES)),
            pl.BlockSpec((None, LANES, oc), lambda i: (i // tiles_per_dir, 0, 0)),
            full((1, oc)),
        ],
        out_specs=pl.BlockSpec((None, oc // LANES, tile, LANES), lambda i: (i, 0, 0, 0)),
        out_shape=jax.ShapeDtypeStruct((n1_full // SUBLANES, oc // LANES, tile, LANES), F32),
        compiler_params=pltpu.CompilerParams(
            dimension_semantics=("arbitrary",), vmem_limit_bytes=VMEM_LIMIT),
        name="filter_mlp",
    )(z_ext, w1p.astype(BF16), b1p, wip.astype(BF16), bip, frp, wop.astype(BF16), absdelta)


def _stage1(a_scr, g_ref, x_ref, base, cn2, n_rows, pitch):
    for j in range(cn2):
        r = jnp.dot(g_ref[j], _load_minor(x_ref, j).astype(BF16), preferred_element_type=F32)
        a_scr[pl.ds(base + j, n_rows, stride=pitch), :] = r


def _spectrum_views(a_scr, chunk, ck1, n1h, pitch):
    re0 = pl.multiple_of(chunk * (ck1 * pitch), SUBLANES)
    im0 = pl.multiple_of(chunk * (ck1 * pitch) + n1h * pitch, SUBLANES)
    return a_scr.at[pl.ds(re0, ck1 * pitch)], a_scr.at[pl.ds(im0, ck1 * pitch)]


def _filter_fft_kernel(h_ref, g_ref, h2_ref, o_ref, a_scr, *, ns, cn2, ck1, n1h, n2len, pitch):
    s = pl.program_id(1)

    @pl.when(s < ns)
    def _():
        _stage1(a_scr, g_ref, h_ref, s * cn2, cn2, 2 * n1h, pitch)

    @pl.when(s >= ns)
    def _():
        a_re, a_im = _spectrum_views(a_scr, s - ns, ck1, n1h, pitch)
        for j in range(ck1):
            rows = jnp.concatenate([a_re[j * pitch:j * pitch + n2len, :],
                                    a_im[j * pitch:j * pitch + n2len, :]], axis=0)
            o_ref[j] = jnp.dot(h2_ref[...], rows.astype(BF16), preferred_element_type=F32)


def _filter_spectrum(taps, g_full, h2, n2len):
    groups, ocb, _, _ = taps.shape
    oc = ocb * LANES
    n1_full = groups * SUBLANES
    n1h = n1_full // 2
    ns = PHASE_STEPS
    cn2, ck1 = n2len // ns, n1h // ns
    pitch = n2len + SUBLANES
    kern = functools.partial(_filter_fft_kernel, ns=ns, cn2=cn2, ck1=ck1, n1h=n1h,
                             n2len=n2len, pitch=pitch)
    return pl.pallas_call(
        kern,
        grid=(oc // LANES, 2 * ns),
        in_specs=[
            pl.BlockSpec((groups, None, cn2 * SUBLANES, LANES),
                         lambda c, s: (0, c, jnp.minimum(s, ns - 1), 0)),
            pl.BlockSpec((cn2, n1_full, n1_full), lambda c, s: (jnp.minimum(s, ns - 1), 0, 0)),
            pl.BlockSpec((2 * n2len, 2 * n2len), lambda c, s: (0, 0)),
        ],
        out_specs=pl.BlockSpec((ck1, 2 * n2len, LANES), lambda c, s: (jnp.maximum(s - ns, 0), 0, c)),
        out_shape=jax.ShapeDtypeStruct((n1h, 2 * n2len, oc), F32),
        scratch_shapes=[pltpu.VMEM((n1_full * pitch, LANES), F32)],
        compiler_params=pltpu.CompilerParams(
            dimension_semantics=("parallel", "arbitrary"), vmem_limit_bytes=VMEM_LIMIT),
        name="filter_spectrum",
    )(taps, g_full, h2)


def _long_conv_kernel(*refs, ns, cn2, ck1, n1h, n2len, pitch, has_gate):
    if has_gate:
        (vf_ref, g_ref, kf_ref, h2_ref, h2i_ref, gi_ref, vb_ref, xm_ref, gate_ref, hb_ref,
         o_ref, a_scr) = refs
    else:
        (vf_ref, g_ref, kf_ref, h2_ref, h2i_ref, gi_ref, vb_ref, xm_ref, hb_ref,
         o_ref, a_scr) = refs
        gate_ref = None
    s = pl.program_id(2)

    @pl.when(s < ns)
    def _():
        _stage1(a_scr, g_ref, vf_ref, s * cn2, cn2, 2 * n1h, pitch)

    @pl.when((s >= ns) & (s < 2 * ns))
    def _():
        a_re, a_im = _spectrum_views(a_scr, s - ns, ck1, n1h, pitch)
        for j in range(ck1):
            rows = jnp.concatenate([a_re[j * pitch:j * pitch + n2len, :],
                                    a_im[j * pitch:j * pitch + n2len, :]], axis=0)
            u = jnp.dot(h2_ref[...], rows.astype(BF16), preferred_element_type=F32)
            ur, ui = u[:n2len], u[n2len:]
            kr, ki = kf_ref[j, :n2len, :], kf_ref[j, n2len:, :]
            y = jnp.concatenate([ur * kr - ui * ki, ur * ki + ui * kr], axis=0)
            bb = jnp.dot(h2i_ref[...], y.astype(BF16), preferred_element_type=F32)
            a_re[j * pitch:j * pitch + n2len, :] = bb[:n2len]
            a_im[j * pitch:j * pitch + n2len, :] = bb[n2len:]

    @pl.when(s >= 2 * ns)
    def _():
        base = (s - 2 * ns) * cn2
        for j in range(cn2):
            rows = a_scr[pl.ds(base + j, 2 * n1h, stride=pitch), :]
            y = jnp.dot(gi_ref[j], rows.astype(BF16), preferred_element_type=F32)
            z = _load_minor(xm_ref, j) * (y + hb_ref[...] * _load_minor(vb_ref, j))
            if gate_ref is not None:
                z = z * _load_minor(gate_ref, j)
            o_ref[:, j * SUBLANES:(j + 1) * SUBLANES, :] = z.reshape(
                n1h // SUBLANES, SUBLANES, z.shape[1])


def _long_conv(u_perm, mult_perm, gate_perm, kf, hbias, order_idx, tables):
    g_fwd, g_inv, _, h2, h2i = tables
    bsz, groups, cbn, tile, _ = u_perm.shape
    n2len = tile // SUBLANES
    n1h = groups * SUBLANES
    ns = PHASE_STEPS
    cn2, ck1 = n2len // ns, n1h // ns
    pitch = n2len + SUBLANES
    has_gate = gate_perm is not None

    def late(s):
        return jnp.clip(s - 2 * ns, 0, ns - 1)

    data_early = pl.BlockSpec((None, groups, None, cn2 * SUBLANES, LANES),
                              lambda c, b, s: (b, 0, c, jnp.minimum(s, ns - 1), 0))
    data_late = pl.BlockSpec((None, groups, None, cn2 * SUBLANES, LANES),
                             lambda c, b, s: (b, 0, c, late(s), 0))
    in_specs = [
        data_early,
        pl.BlockSpec((cn2, 2 * n1h, n1h), lambda c, b, s: (jnp.minimum(s, ns - 1), 0, 0)),
        pl.BlockSpec((ck1, 2 * n2len, LANES),
                     lambda c, b, s: (jnp.clip(s - ns, 0, ns - 1), 0, order_idx * cbn + c)),
        pl.BlockSpec((2 * n2len, 2 * n2len), lambda c, b, s: (0, 0)),
        pl.BlockSpec((2 * n2len, 2 * n2len), lambda c, b, s: (0, 0)),
        pl.BlockSpec((cn2, n1h, 2 * n1h), lambda c, b, s: (late(s), 0, 0)),
        data_late, data_late,
    ]
    args = [u_perm, g_fwd, kf, h2, h2i, g_inv, u_perm, mult_perm]
    if has_gate:
        in_specs.append(data_late)
        args.append(gate_perm)
    in_specs.append(pl.BlockSpec((None, 1, LANES), lambda c, b, s: (order_idx, 0, c)))
    args.append(hbias)
    kern = functools.partial(_long_conv_kernel, ns=ns, cn2=cn2, ck1=ck1, n1h=n1h,
                             n2len=n2len, pitch=pitch, has_gate=has_gate)
    return pl.pallas_call(
        kern,
        grid=(cbn, bsz, 3 * ns),
        in_specs=in_specs,
        out_specs=data_late,
        out_shape=jax.ShapeDtypeStruct(u_perm.shape, F32),
        scratch_shapes=[pltpu.VMEM((2 * n1h * pitch, LANES), F32)],
        compiler_params=pltpu.CompilerParams(
            dimension_semantics=("parallel", "parallel", "arbitrary"),
            vmem_limit_bytes=VMEM_LIMIT),
        name="long_conv",
    )(*args)


def _out_kernel(x_ref, yh_ref, yp_ref, mod_ref, w_ref, b_ref, g_ref, beta_ref, o_ref,
                *, n2len, alpha):
    cbn = yh_ref.shape[0]
    ch = cbn * LANES
    yh = jnp.concatenate(
        [jnp.concatenate([yh_ref[cb, pl.ds(j, n2len, stride=SUBLANES), :]
                          for cb in range(cbn)], axis=1)
         for j in range(SUBLANES)], axis=0)
    acc = jnp.dot(yh.astype(BF16), w_ref[0:ch, :], preferred_element_type=F32)
    acc = acc + jnp.dot(yp_ref[...], w_ref[ch:, :], preferred_element_type=F32)
    h = alpha * x_ref[...] + mod_ref[2:3, :] * (acc + b_ref[...])
    mu = jnp.mean(h, axis=-1, keepdims=True)
    hc = h - mu
    var = jnp.mean(hc * hc, axis=-1, keepdims=True)
    o_ref[...] = hc * lax.rsqrt(var + LN_EPS) * g_ref[...] + beta_ref[...]


def _output_projection(x, yh_perm, yp, mod3, w_out, b_out, ln_g, ln_b, alpha):
    bsz, seq, d = x.shape
    _, _, cbn, tile, _ = yh_perm.shape
    ch = cbn * LANES
    n2len = tile // SUBLANES
    dp = yp.shape[2]
    full = lambda shape: pl.BlockSpec(shape, lambda b, i: (0,) * len(shape))
    kern = functools.partial(_out_kernel, n2len=n2len, alpha=alpha)
    return pl.pallas_call(
        kern,
        grid=(bsz, seq // tile),
        in_specs=[
            pl.BlockSpec((None, tile, d), lambda b, i: (b, i, 0)),
            pl.BlockSpec((None, None, cbn, tile, LANES), lambda b, i: (b, i, 0, 0, 0)),
            pl.BlockSpec((None, tile, dp), lambda b, i: (b, i, 0)),
            pl.BlockSpec((None, 3, d), lambda b, i: (b, 0, 0)),
            full((ch + dp, d)), full((1, d)), full((1, d)), full((1, d)),
        ],
        out_specs=pl.BlockSpec((None, tile, d), lambda b, i: (b, i, 0)),
        out_shape=jax.ShapeDtypeStruct(x.shape, x.dtype),
        compiler_params=pltpu.CompilerParams(
            dimension_semantics=("parallel", "arbitrary"), vmem_limit_bytes=VMEM_LIMIT),
        name="out_proj_deepnorm",
    )(x, yh_perm, yp, mod3, w_out.astype(BF16), b_out.reshape(1, d), ln_g.reshape(1, d),
      ln_b.reshape(1, d))


def _forward(x, c, w_ada, b_ada, w_in, b_in, conv_w, conv_b, filt_w1, filt_b1, filt_w_inner,
             filt_b_inner, filt_w_out, filt_freq, hyena_bias, pool_w, pool_scale, w_out,
             b_out, ln_g, ln_b, *, n2len):
    bsz, seq, d = x.shape
    depth = w_ada.shape[0]
    order, n_ch = hyena_bias.shape[1], hyena_bias.shape[2]
    alpha = (2.0 * depth) ** 0.25
    tables = _dft_tables(seq, n2len)
    h = x
    for layer in range(depth):
        mod3 = _modulation(c, w_ada[layer], b_ada[layer]).reshape(bsz, 3, d)
        v, x1, x2, gate, yp = _input_projection(
            h, mod3, w_in[layer], b_in[layer], conv_w[layer], conv_b[layer], pool_w[layer],
            pool_scale[layer], n2len)
        taps = _filter_taps(seq, n2len, filt_w1[layer], filt_b1[layer], filt_w_inner[layer],
                            filt_b_inner[layer], filt_w_out[layer], filt_freq[layer], n_ch,
                            order)
        kf = _filter_spectrum(taps, tables[2], tables[3], n2len)
        hbias = hyena_bias[layer].reshape(order, 1, n_ch)
        z = v
        mults = (x1, x2)
        for o in range(order):
            z = _long_conv(z, mults[o], gate if o == order - 1 else None, kf, hbias, o, tables)
        h = _output_projection(h, z, yp, mod3, w_out[layer], b_out[layer], ln_g[layer],
                               ln_b[layer], alpha)
    return h


def kernel(x, c, w_ada, b_ada, w_in, b_in, conv_w, conv_b, filt_w1, filt_b1, filt_w_inner,
           filt_b_inner, filt_w_out, filt_freq, hyena_bias, pool_w, pool_scale, w_out, b_out,
           ln_g, ln_b):
    return _forward(x, c, w_ada, b_ada, w_in, b_in, conv_w, conv_b, filt_w1, filt_b1,
                    filt_w_inner, filt_b_inner, filt_w_out, filt_freq, hyena_bias, pool_w,
                    pool_scale, w_out, b_out, ln_g, ln_b, n2len=MINOR_LEN)
```

```python
import functools
import math

import jax
import jax.numpy as jnp
from jax import lax
from jax.experimental import pallas as pl
from jax.experimental.pallas import tpu as pltpu

F32 = jnp.float32
BF16 = jnp.bfloat16
HIGHEST = lax.Precision.HIGHEST

POOL_WINDOWS = (2, 4, 8, 16)
LN_EPS = 1e-5
DECAY_TARGET = 1e-2
FAST_DECAY_PCT = 0.3
SLOW_DECAY_PCT = 1.5

LANES = 128
SUBLANES = 8
HALO = 16
MINOR_LEN = 128
PHASE_STEPS = 8
VMEM_LIMIT = 56 * 1024 * 1024


def _silu(x):
    return x * jax.nn.sigmoid(x)


def _dft_tables(seq, n2len):
    n1h = seq // n2len
    n_fft = 2 * seq
    n2i = jnp.arange(n2len, dtype=jnp.int32)
    k1 = jnp.arange(n1h, dtype=jnp.int32)
    odd = 2 * k1 + 1
    alpha = ((odd[:, None] * k1[None, :]) % (4 * n1h)).astype(F32) * (2.0 * math.pi / (4 * n1h))
    n2e = jnp.arange(n2len + 1, dtype=jnp.int32)
    beta = ((n2e[:, None] * odd[None, :]) % (2 * n_fft)).astype(F32) * (2.0 * math.pi / (2 * n_fft))
    ca, sa = jnp.cos(alpha)[None], jnp.sin(alpha)[None]
    cb, sb = jnp.cos(beta)[:, :, None], jnp.sin(beta)[:, :, None]
    cos_t = ca * cb[:-1] - sa * sb[:-1]
    sin_t = sa * cb[:-1] + ca * sb[:-1]
    g_fwd = jnp.concatenate([cos_t, -sin_t], axis=1).astype(BF16)
    g_inv = (2.0 / n_fft) * jnp.concatenate(
        [jnp.swapaxes(cos_t, 1, 2), -jnp.swapaxes(sin_t, 1, 2)], axis=2)
    g_inv = g_inv.astype(BF16)
    cos_r = ca * cb[1:] - sa * sb[1:]
    sin_r = sa * cb[1:] + ca * sb[1:]
    g_rev = jnp.concatenate([cos_r, sin_r], axis=1).astype(BF16)

    m = (n2i[:, None] * n2i[None, :]) % n2len
    phi = m.astype(F32) * (2.0 * math.pi / n2len)
    cm, sm = jnp.cos(phi), jnp.sin(phi)
    h2 = jnp.block([[cm, sm], [-sm, cm]]).astype(BF16)
    h2i = jnp.block([[cm, -sm], [sm, cm]]).astype(BF16)
    return g_fwd, g_inv, g_rev, h2, h2i


def _filter_pos_features(seq, emb):
    t = jnp.linspace(0.0, 1.0, seq, dtype=F32)[:, None]
    bands = (emb - 1) // 2
    w = (2.0 * math.pi / seq) * jnp.arange(seq, dtype=F32)[:, None]
    f = jnp.linspace(1e-4, bands - 1, bands, dtype=F32)[None, :]
    return jnp.concatenate([t, jnp.cos(f * w), -jnp.sin(f * w)], axis=-1)


def _mod_kernel(c_ref, w_ref, b_ref, o_ref):
    s = _silu(c_ref[...])
    o_ref[...] = jnp.dot(s, w_ref[...], precision=HIGHEST,
                         preferred_element_type=F32) + b_ref[...]


def _modulation(c, w_ada, b_ada):
    bsz, d = c.shape
    n_out = w_ada.shape[1]
    return pl.pallas_call(
        _mod_kernel,
        grid=(n_out // d,),
        in_specs=[pl.BlockSpec((bsz, d), lambda j: (0, 0)),
                  pl.BlockSpec((d, d), lambda j: (0, j)),
                  pl.BlockSpec((1, d), lambda j: (0, j))],
        out_specs=pl.BlockSpec((bsz, d), lambda j: (0, j)),
        out_shape=jax.ShapeDtypeStruct((bsz, n_out), F32),
        name="adaln_mod",
    )(c, w_ada, b_ada.reshape(1, n_out))


def _store_permuted(o_ref, val, n2len, first=0):
    for cb in range(val.shape[1] // LANES):
        for j in range(val.shape[0] // n2len):
            o_ref[cb, pl.ds(first + j, n2len, stride=SUBLANES), :] = val[
                j * n2len:(j + 1) * n2len, cb * LANES:(cb + 1) * LANES]


def _load_minor(x_ref, j):
    blk = x_ref[:, j * SUBLANES:(j + 1) * SUBLANES, :]
    return blk.reshape(blk.shape[0] * SUBLANES, blk.shape[2])


def _proj_kernel(xp_ref, xc_ref, xn_ref, mod_ref, w_ref, b_ref, cw_ref, cb_ref, pw_ref,
                 ps_ref, v_ref, x1_ref, x2_ref, g_ref, yp_ref, p_scr, *, tile, n2len, ch,
                 seq):
    i = pl.program_id(1)
    nt = pl.num_programs(1)
    shift = mod_ref[0:1, :]
    scale1 = 1.0 + mod_ref[1:2, :]
    xe = jnp.concatenate([xp_ref[...], xc_ref[...], xn_ref[...]], axis=0)
    ue = (xe * scale1 + shift).astype(BF16)
    uc = ue[HALO:HALO + tile, :]
    row = lax.broadcasted_iota(jnp.int32, (tile + 2 * HALO, 1), 0)
    valid = ((row >= HALO) | (i > 0)) & ((row < tile + HALO) | (i < nt - 1))

    for k, o_ref in enumerate((v_ref, x1_ref, x2_ref)):
        cols = slice(k * ch, (k + 1) * ch)
        p = jnp.dot(ue, w_ref[:, cols], preferred_element_type=F32) + b_ref[:, cols]
        p_scr[...] = jnp.where(valid, p, 0.0)
        s = (cb_ref[:, cols]
             + cw_ref[0:1, cols] * p_scr[HALO - 1:HALO - 1 + tile, :]
             + cw_ref[1:2, cols] * p_scr[HALO:HALO + tile, :]
             + cw_ref[2:3, cols] * p_scr[HALO + 1:HALO + 1 + tile, :])
        _store_permuted(o_ref, s, n2len)

    hg = jnp.dot(uc, w_ref[:, 3 * ch:4 * ch], preferred_element_type=F32) + b_ref[:, 3 * ch:4 * ch]
    _store_permuted(g_ref, _silu(hg), n2len)

    dp = ps_ref.shape[1]
    c0 = 4 * ch
    pin = jnp.dot(ue, w_ref[:, c0:c0 + dp], preferred_element_type=F32) + b_ref[:, c0:c0 + dp]
    p_scr[...] = jnp.where(valid, pin, 0.0)
    pgate = jnp.dot(uc, w_ref[:, c0 + dp:c0 + 2 * dp], preferred_element_type=F32) + b_ref[:, c0 + dp:c0 + 2 * dp]
    pos = i * tile + lax.broadcasted_iota(jnp.int32, (tile, 1), 0)
    pg = dp // len(POOL_WINDOWS)
    groups = []
    for g, win in enumerate(POOL_WINDOWS):
        lanes = slice(g * pg, (g + 1) * pg)
        half = win // 2
        acc = p_scr[HALO - half:HALO - half + tile, lanes]
        for d in range(-half + 1, half):
            acc = acc + p_scr[HALO + d:HALO + d + tile, lanes]
        cnt = (jnp.minimum(pos + half, seq) - jnp.maximum(pos - half, 0)).astype(F32)
        diff = acc / cnt - p_scr[HALO:HALO + tile, lanes]
        groups.append(jnp.dot(diff.astype(BF16), pw_ref[g], preferred_element_type=F32))
    yp = jnp.concatenate(groups, axis=1) * ps_ref[...] * _silu(pgate)
    yp_ref[...] = yp.astype(yp_ref.dtype)


def _input_projection(x, mod3, w_in, b_in, conv_w, conv_b, pool_w, pool_scale, n2len):
    bsz, seq, d = x.shape
    ch = conv_w.shape[1] // 3
    dp = pool_scale.shape[0]
    n1h = seq // n2len
    tile = SUBLANES * n2len
    nt = seq // tile
    hb = tile // HALO
    n_proj = w_in.shape[1]
    cbn = ch // LANES
    perm = jax.ShapeDtypeStruct((bsz, n1h // SUBLANES, cbn, tile, LANES), F32)
    perm_spec = pl.BlockSpec((None, None, cbn, tile, LANES), lambda b, i: (b, i, 0, 0, 0))
    full = lambda shape: pl.BlockSpec(shape, lambda b, i: (0,) * len(shape))
    kern = functools.partial(_proj_kernel, tile=tile, n2len=n2len, ch=ch, seq=seq)
    return pl.pallas_call(
        kern,
        grid=(bsz, nt),
        in_specs=[
            pl.BlockSpec((None, HALO, d), lambda b, i: (b, jnp.maximum(i * hb - 1, 0), 0)),
            pl.BlockSpec((None, tile, d), lambda b, i: (b, i, 0)),
            pl.BlockSpec((None, HALO, d), lambda b, i: (b, jnp.minimum((i + 1) * hb, seq // HALO - 1), 0)),
            pl.BlockSpec((None, 3, d), lambda b, i: (b, 0, 0)),
            full((d, n_proj)), full((1, n_proj)), full((3, 3 * ch)), full((1, 3 * ch)),
            full(pool_w.shape), full((1, dp)),
        ],
        out_specs=[perm_spec, perm_spec, perm_spec, perm_spec,
                   pl.BlockSpec((None, tile, dp), lambda b, i: (b, i, 0))],
        out_shape=[perm, perm, perm, perm, jax.ShapeDtypeStruct((bsz, seq, dp), BF16)],
        scratch_shapes=[pltpu.VMEM((tile + 2 * HALO, ch), F32)],
        compiler_params=pltpu.CompilerParams(
            dimension_semantics=("parallel", "arbitrary"), vmem_limit_bytes=VMEM_LIMIT),
        name="in_proj_conv_pool",
    )(x, x, x, mod3, w_in.astype(BF16), b_in.reshape(1, n_proj), conv_w,
      conv_b.reshape(1, 3 * ch), pool_w.astype(BF16), pool_scale.reshape(1, dp))


def _filter_mlp_kernel(z_ref, w1_ref, b1_ref, wi_ref, bi_ref, fr_ref, wo_ref, ad_ref,
                       of_ref, or_ref, *, n2len):
    z = z_ref[...]
    fr = fr_ref[...]
    h = jnp.sin(fr * (jnp.dot(z.astype(BF16), w1_ref[...],
                              preferred_element_type=F32) + b1_ref[...]))
    for l in range(wi_ref.shape[0]):
        h = jnp.sin(fr * (jnp.dot(h.astype(BF16), wi_ref[l],
                                  preferred_element_type=F32) + bi_ref[l:l + 1, :]))
    hb = h.astype(BF16)
    half_lanes = LANES // 2
    n1_half = z.shape[0] // n2len
    for half in range(2):
        t = z[:, half * half_lanes:half * half_lanes + 1]
        decay = jnp.exp(-t * ad_ref[...])
        for d, o_ref in enumerate((of_ref, or_ref)):
            k = jnp.dot(hb, wo_ref[d, half], preferred_element_type=F32)
            _store_permuted(o_ref, k * decay, n2len, first=half * n1_half)


def _block_diag2(w):
    zero = jnp.zeros_like(w)
    return jnp.concatenate([jnp.concatenate([w, zero], axis=-1),
                            jnp.concatenate([zero, w], axis=-1)], axis=-2)


def _filter_taps(seq, n2len, w1, b1, w_inner, b_inner, w_out, freq, n_ch, order):
    emb, hid = w1.shape
    n_inner = w_inner.shape[0]
    oc = order * n_ch
    hl = LANES // 2
    assert emb <= hl and hid <= hl
    tile = SUBLANES * n2len
    half = tile // 2
    groups = seq // tile
    z = jnp.pad(_filter_pos_features(seq, emb), ((0, 0), (0, hl - emb)))
    zp = jnp.transpose(z.reshape(groups, 2, half, hl), (0, 2, 1, 3)).reshape(groups * half, LANES)
    ph = hl - hid
    w1d = _block_diag2(jnp.pad(w1, ((0, hl - emb), (0, ph))))
    b1d = jnp.tile(jnp.pad(b1, (0, ph)), 2).reshape(1, LANES)
    wid = _block_diag2(jnp.pad(w_inner, ((0, 0), (0, ph), (0, ph))))
    bid = jnp.tile(jnp.pad(b_inner, ((0, 0), (0, ph))), (1, 2))
    frd = jnp.tile(jnp.pad(freq, (0, ph), constant_values=1.0), 2).reshape(1, LANES)
    wo = jnp.transpose(w_out.reshape(hid, order, 2, n_ch), (2, 0, 1, 3)).reshape(2, hid, oc)
    wo = jnp.pad(wo, ((0, 0), (0, ph), (0, 0)))
    zero = jnp.zeros_like(wo)
    wo4 = jnp.stack([jnp.concatenate([wo, zero], axis=1),
                     jnp.concatenate([zero, wo], axis=1)], axis=1)
    min_decay = math.log(DECAY_TARGET) / SLOW_DECAY_PCT
    max_decay = math.log(DECAY_TARGET) / FAST_DECAY_PCT
    absdelta = jnp.abs(jnp.linspace(min_decay, max_decay, n_ch, dtype=F32))
    absdelta = jnp.tile(absdelta, order).reshape(1, oc)

    full = lambda shape: pl.BlockSpec(shape, lambda i: (0,) * len(shape))
    kern = functools.partial(_filter_mlp_kernel, n2len=n2len)
    taps = jax.ShapeDtypeStruct((groups, oc // LANES, tile, LANES), F32)
    taps_spec = pl.BlockSpec((None, oc // LANES, tile, LANES), lambda i: (i, 0, 0, 0))
    return pl.pallas_call(
        kern,
        grid=(groups,),
        in_specs=[
            pl.BlockSpec((half, LANES), lambda i: (i, 0)),
            full((LANES, LANES)), full((1, LANES)), full((n_inner, LANES, LANES)),
            full((n_inner, LANES)), full((1, LANES)), full((2, 2, LANES, oc)), full((1, oc)),
        ],
        out_specs=[taps_spec, taps_spec],
        out_shape=[taps, taps],
        compiler_params=pltpu.CompilerParams(
            dimension_semantics=("arbitrary",), vmem_limit_bytes=VMEM_LIMIT),
        name="filter_mlp",
    )(zp, w1d.astype(BF16), b1d, wid.astype(BF16), bid, frd, wo4.astype(BF16), absdelta)


def _stage1(a_scr, g_ref, x_ref, base, cn2, n_rows, pitch):
    for j in range(cn2):
        r = jnp.dot(g_ref[j], _load_minor(x_ref, j).astype(BF16), preferred_element_type=F32)
        a_scr[pl.ds(base + j, n_rows, stride=pitch), :] = r


def _spectrum_views(a_scr, chunk, ck1, n1h, pitch):
    re0 = pl.multiple_of(chunk * (ck1 * pitch), SUBLANES)
    im0 = pl.multiple_of(chunk * (ck1 * pitch) + n1h * pitch, SUBLANES)
    return a_scr.at[pl.ds(re0, ck1 * pitch)], a_scr.at[pl.ds(im0, ck1 * pitch)]


def _filter_fft_kernel(hf_ref, hr_ref, g_ref, gr_ref, h2_ref, o_ref, a_scr,
                       *, ns, cn2, ck1, n1h, n2len, pitch):
    s = pl.program_id(1)

    @pl.when(s < ns)
    def _():
        for j in range(cn2):
            jr = cn2 - 1 - j
            r = jnp.dot(g_ref[j], _load_minor(hf_ref, j).astype(BF16),
                        preferred_element_type=F32)
            r = r + jnp.dot(gr_ref[jr], _load_minor(hr_ref, jr).astype(BF16),
                            preferred_element_type=F32)
            a_scr[pl.ds(s * cn2 + j, 2 * n1h, stride=pitch), :] = r

    @pl.when(s >= ns)
    def _():
        a_re, a_im = _spectrum_views(a_scr, s - ns, ck1, n1h, pitch)
        for j in range(ck1):
            rows = jnp.concatenate([a_re[j * pitch:j * pitch + n2len, :],
                                    a_im[j * pitch:j * pitch + n2len, :]], axis=0)
            o_ref[j] = jnp.dot(h2_ref[...], rows.astype(BF16), preferred_element_type=F32)


def _filter_spectrum(taps_f, taps_r, g_fwd, g_rev, h2, n2len):
    groups, ocb, _, _ = taps_f.shape
    oc = ocb * LANES
    n1h = groups * SUBLANES
    ns = PHASE_STEPS
    cn2, ck1 = n2len // ns, n1h // ns
    pitch = n2len + SUBLANES
    kern = functools.partial(_filter_fft_kernel, ns=ns, cn2=cn2, ck1=ck1, n1h=n1h,
                             n2len=n2len, pitch=pitch)
    return pl.pallas_call(
        kern,
        grid=(oc // LANES, 2 * ns),
        in_specs=[
            pl.BlockSpec((groups, None, cn2 * SUBLANES, LANES),
                         lambda c, s: (0, c, jnp.minimum(s, ns - 1), 0)),
            pl.BlockSpec((groups, None, cn2 * SUBLANES, LANES),
                         lambda c, s: (0, c, jnp.maximum(ns - 1 - s, 0), 0)),
            pl.BlockSpec((cn2, 2 * n1h, n1h), lambda c, s: (jnp.minimum(s, ns - 1), 0, 0)),
            pl.BlockSpec((cn2, 2 * n1h, n1h), lambda c, s: (jnp.maximum(ns - 1 - s, 0), 0, 0)),
            pl.BlockSpec((2 * n2len, 2 * n2len), lambda c, s: (0, 0)),
        ],
        out_specs=pl.BlockSpec((ck1, 2 * n2len, LANES), lambda c, s: (jnp.maximum(s - ns, 0), 0, c)),
        out_shape=jax.ShapeDtypeStruct((n1h, 2 * n2len, oc), F32),
        scratch_shapes=[pltpu.VMEM((2 * n1h * pitch, LANES), F32)],
        compiler_params=pltpu.CompilerParams(
            dimension_semantics=("parallel", "arbitrary"), vmem_limit_bytes=VMEM_LIMIT),
        name="filter_spectrum",
    )(taps_f, taps_r, g_fwd, g_rev, h2)


def _long_conv_kernel(*refs, ns, cn2, ck1, n1h, n2len, pitch, has_gate):
    if has_gate:
        (vf_ref, g_ref, kf_ref, h2_ref, h2i_ref, gi_ref, vb_ref, xm_ref, gate_ref, hb_ref,
         o_ref, a_scr) = refs
    else:
        (vf_ref, g_ref, kf_ref, h2_ref, h2i_ref, gi_ref, vb_ref, xm_ref, hb_ref,
         o_ref, a_scr) = refs
        gate_ref = None
    s = pl.program_id(2)

    @pl.when(s < ns)
    def _():
        _stage1(a_scr, g_ref, vf_ref, s * cn2, cn2, 2 * n1h, pitch)

    @pl.when((s >= ns) & (s < 2 * ns))
    def _():
        a_re, a_im = _spectrum_views(a_scr, s - ns, ck1, n1h, pitch)
        for j in range(ck1):
            rows = jnp.concatenate([a_re[j * pitch:j * pitch + n2len, :],
                                    a_im[j * pitch:j * pitch + n2len, :]], axis=0)
            u = jnp.dot(h2_ref[...], rows.astype(BF16), preferred_element_type=F32)
            ur, ui = u[:n2len], u[n2len:]
            kr, ki = kf_ref[j, :n2len, :], kf_ref[j, n2len:, :]
            y = jnp.concatenate([ur * kr - ui * ki, ur * ki + ui * kr], axis=0)
            bb = jnp.dot(h2i_ref[...], y.astype(BF16), preferred_element_type=F32)
            a_re[j * pitch:j * pitch + n2len, :] = bb[:n2len]
            a_im[j * pitch:j * pitch + n2len, :] = bb[n2len:]

    @pl.when(s >= 2 * ns)
    def _():
        base = (s - 2 * ns) * cn2
        for j in range(cn2):
            rows = a_scr[pl.ds(base + j, 2 * n1h, stride=pitch), :]
            y = jnp.dot(gi_ref[j], rows.astype(BF16), preferred_element_type=F32)
            z = _load_minor(xm_ref, j) * (y + hb_ref[...] * _load_minor(vb_ref, j))
            if gate_ref is not None:
                z = z * _load_minor(gate_ref, j)
            o_ref[:, j * SUBLANES:(j + 1) * SUBLANES, :] = z.reshape(
                n1h // SUBLANES, SUBLANES, z.shape[1])


def _long_conv(u_perm, mult_perm, gate_perm, kf, hbias, order_idx, tables):
    g_fwd, g_inv, _, h2, h2i = tables
    bsz, groups, cbn, tile, _ = u_perm.shape
    n2len = tile // SUBLANES
    n1h = groups * SUBLANES
    ns = PHASE_STEPS
    cn2, ck1 = n2len // ns, n1h // ns
    pitch = n2len + SUBLANES
    has_gate = gate_perm is not None

    def late(s):
        return jnp.clip(s - 2 * ns, 0, ns - 1)

    data_early = pl.BlockSpec((None, groups, None, cn2 * SUBLANES, LANES),
                              lambda c, b, s: (b, 0, c, jnp.minimum(s, ns - 1), 0))
    data_late = pl.BlockSpec((None, groups, None, cn2 * SUBLANES, LANES),
                             lambda c, b, s: (b, 0, c, late(s), 0))
    in_specs = [
        data_early,
        pl.BlockSpec((cn2, 2 * n1h, n1h), lambda c, b, s: (jnp.minimum(s, ns - 1), 0, 0)),
        pl.BlockSpec((ck1, 2 * n2len, LANES),
                     lambda c, b, s: (jnp.clip(s - ns, 0, ns - 1), 0, order_idx * cbn + c)),
        pl.BlockSpec((2 * n2len, 2 * n2len), lambda c, b, s: (0, 0)),
        pl.BlockSpec((2 * n2len, 2 * n2len), lambda c, b, s: (0, 0)),
        pl.BlockSpec((cn2, n1h, 2 * n1h), lambda c, b, s: (late(s), 0, 0)),
        data_late, data_late,
    ]
    args = [u_perm, g_fwd, kf, h2, h2i, g_inv, u_perm, mult_perm]
    if has_gate:
        in_specs.append(data_late)
        args.append(gate_perm)
    in_specs.append(pl.BlockSpec((None, 1, LANES), lambda c, b, s: (order_idx, 0, c)))
    args.append(hbias)
    kern = functools.partial(_long_conv_kernel, ns=ns, cn2=cn2, ck1=ck1, n1h=n1h,
                             n2len=n2len, pitch=pitch, has_gate=has_gate)
    return pl.pallas_call(
        kern,
        grid=(cbn, bsz, 3 * ns),
        in_specs=in_specs,
        out_specs=data_late,
        out_shape=jax.ShapeDtypeStruct(u_perm.shape, F32),
        scratch_shapes=[pltpu.VMEM((2 * n1h * pitch, LANES), F32)],
        compiler_params=pltpu.CompilerParams(
            dimension_semantics=("parallel", "parallel", "arbitrary"),
            vmem_limit_bytes=VMEM_LIMIT),
        name="long_conv",
    )(*args)


def _out_kernel(x_ref, yh_ref, yp_ref, mod_ref, w_ref, b_ref, g_ref, beta_ref, o_ref,
                *, n2len, alpha):
    cbn = yh_ref.shape[0]
    ch = cbn * LANES
    yh = jnp.concatenate(
        [jnp.concatenate([yh_ref[cb, pl.ds(j, n2len, stride=SUBLANES), :]
                          for cb in range(cbn)], axis=1)
         for j in range(SUBLANES)], axis=0)
    acc = jnp.dot(yh.astype(BF16), w_ref[0:ch, :], preferred_element_type=F32)
    acc = acc + jnp.dot(yp_ref[...], w_ref[ch:, :], preferred_element_type=F32)
    h = alpha * x_ref[...] + mod_ref[2:3, :] * (acc + b_ref[...])
    mu = jnp.mean(h, axis=-1, keepdims=True)
    hc = h - mu
    var = jnp.mean(hc * hc, axis=-1, keepdims=True)
    o_ref[...] = hc * lax.rsqrt(var + LN_EPS) * g_ref[...] + beta_ref[...]


def _output_projection(x, yh_perm, yp, mod3, w_out, b_out, ln_g, ln_b, alpha):
    bsz, seq, d = x.shape
    _, _, cbn, tile, _ = yh_perm.shape
    ch = cbn * LANES
    n2len = tile // SUBLANES
    dp = yp.shape[2]
    full = lambda shape: pl.BlockSpec(shape, lambda b, i: (0,) * len(shape))
    kern = functools.partial(_out_kernel, n2len=n2len, alpha=alpha)
    return pl.pallas_call(
        kern,
        grid=(bsz, seq // tile),
        in_specs=[
            pl.BlockSpec((None, tile, d), lambda b, i: (b, i, 0)),
            pl.BlockSpec((None, None, cbn, tile, LANES), lambda b, i: (b, i, 0, 0, 0)),
            pl.BlockSpec((None, tile, dp), lambda b, i: (b, i, 0)),
            pl.BlockSpec((None, 3, d), lambda b, i: (b, 0, 0)),
            full((ch + dp, d)), full((1, d)), full((1, d)), full((1, d)),
        ],
        out_specs=pl.BlockSpec((None, tile, d), lambda b, i: (b, i, 0)),
        out_shape=jax.ShapeDtypeStruct(x.shape, x.dtype),
        compiler_params=pltpu.CompilerParams(
            dimension_semantics=("parallel", "arbitrary"), vmem_limit_bytes=VMEM_LIMIT),
        name="out_proj_deepnorm",
    )(x, yh_perm, yp, mod3, w_out.astype(BF16), b_out.reshape(1, d), ln_g.reshape(1, d),
      ln_b.reshape(1, d))


def _forward(x, c, w_ada, b_ada, w_in, b_in, conv_w, conv_b, filt_w1, filt_b1, filt_w_inner,
             filt_b_inner, filt_w_out, filt_freq, hyena_bias, pool_w, pool_scale, w_out,
             b_out, ln_g, ln_b, *, n2len):
    bsz, seq, d = x.shape
    depth = w_ada.shape[0]
    order, n_ch = hyena_bias.shape[1], hyena_bias.shape[2]
    alpha = (2.0 * depth) ** 0.25
    tables = _dft_tables(seq, n2len)
    h = x
    for layer in range(depth):
        mod3 = _modulation(c, w_ada[layer], b_ada[layer]).reshape(bsz, 3, d)
        v, x1, x2, gate, yp = _input_projection(
            h, mod3, w_in[layer], b_in[layer], conv_w[layer], conv_b[layer], pool_w[layer],
            pool_scale[layer], n2len)
        taps_f, taps_r = _filter_taps(
            seq, n2len, filt_w1[layer], filt_b1[layer], filt_w_inner[layer],
            filt_b_inner[layer], filt_w_out[layer], filt_freq[layer], n_ch, order)
        kf = _filter_spectrum(taps_f, taps_r, tables[0], tables[2], tables[3], n2len)
        hbias = hyena_bias[layer].reshape(order, 1, n_ch)
        z = v
        mults = (x1, x2)
        for o in range(order):
            z = _long_conv(z, mults[o], gate if o == order - 1 else None, kf, hbias, o, tables)
        h = _output_projection(h, z, yp, mod3, w_out[layer], b_out[layer], ln_g[layer],
                               ln_b[layer], alpha)
    return h


def kernel(x, c, w_ada, b_ada, w_in, b_in, conv_w, conv_b, filt_w1, filt_b1, filt_w_inner,
           filt_b_inner, filt_w_out, filt_freq, hyena_bias, pool_w, pool_scale, w_out, b_out,
           ln_g, ln_b):
    return _forward(x, c, w_ada, b_ada, w_in, b_in, conv_w, conv_b, filt_w1, filt_b1,
                    filt_w_inner, filt_b_inner, filt_w_out, filt_freq, hyena_bias, pool_w,
                    pool_scale, w_out, b_out, ln_g, ln_b, n2len=MINOR_LEN)
```

```python
import functools
import math

import jax
import jax.numpy as jnp
from jax import lax
from jax.experimental import pallas as pl
from jax.experimental.pallas import tpu as pltpu

F32 = jnp.float32
BF16 = jnp.bfloat16
HIGHEST = lax.Precision.HIGHEST

POOL_WINDOWS = (2, 4, 8, 16)
LN_EPS = 1e-5
DECAY_TARGET = 1e-2
FAST_DECAY_PCT = 0.3
SLOW_DECAY_PCT = 1.5

LANES = 128
SUBLANES = 8
HALO = 16
MINOR_LEN = 128
PHASE_STEPS = 8
VMEM_LIMIT = 56 * 1024 * 1024


def _silu(x):
    return x * jax.nn.sigmoid(x)


def _dft_tables(seq, n2len):
    n1h = seq // n2len
    n_fft = 2 * seq
    n2i = jnp.arange(n2len, dtype=jnp.int32)
    k1 = jnp.arange(n1h, dtype=jnp.int32)
    odd = 2 * k1 + 1
    alpha = ((odd[:, None] * k1[None, :]) % (4 * n1h)).astype(F32) * (2.0 * math.pi / (4 * n1h))
    n2e = jnp.arange(n2len + 1, dtype=jnp.int32)
    beta = ((n2e[:, None] * odd[None, :]) % (2 * n_fft)).astype(F32) * (2.0 * math.pi / (2 * n_fft))
    ca, sa = jnp.cos(alpha)[None], jnp.sin(alpha)[None]
    cb, sb = jnp.cos(beta)[:, :, None], jnp.sin(beta)[:, :, None]
    cos_t = ca * cb[:-1] - sa * sb[:-1]
    sin_t = sa * cb[:-1] + ca * sb[:-1]
    g_fwd = jnp.concatenate([cos_t, -sin_t], axis=1).astype(BF16)
    g_inv = (2.0 / n_fft) * jnp.concatenate(
        [jnp.swapaxes(cos_t, 1, 2), -jnp.swapaxes(sin_t, 1, 2)], axis=2)
    g_inv = g_inv.astype(BF16)
    cos_r = ca * cb[1:] - sa * sb[1:]
    sin_r = sa * cb[1:] + ca * sb[1:]
    g_rev = jnp.concatenate([cos_r, sin_r], axis=1).astype(BF16)

    m = (n2i[:, None] * n2i[None, :]) % n2len
    phi = m.astype(F32) * (2.0 * math.pi / n2len)
    cm, sm = jnp.cos(phi), jnp.sin(phi)
    h2 = jnp.block([[cm, sm], [-sm, cm]]).astype(BF16)
    h2i = jnp.block([[cm, -sm], [sm, cm]]).astype(BF16)
    return g_fwd, g_inv, g_rev, h2, h2i


def _filter_pos_features(seq, emb):
    t = jnp.linspace(0.0, 1.0, seq, dtype=F32)[:, None]
    bands = (emb - 1) // 2
    w = (2.0 * math.pi / seq) * jnp.arange(seq, dtype=F32)[:, None]
    f = jnp.linspace(1e-4, bands - 1, bands, dtype=F32)[None, :]
    return jnp.concatenate([t, jnp.cos(f * w), -jnp.sin(f * w)], axis=-1)


def _mod_kernel(c_ref, w_ref, b_ref, o_ref):
    s = _silu(c_ref[...])
    o_ref[...] = jnp.dot(s, w_ref[...], precision=HIGHEST,
                         preferred_element_type=F32) + b_ref[...]


def _modulation(c, w_ada, b_ada):
    bsz, d = c.shape
    n_out = w_ada.shape[1]
    return pl.pallas_call(
        _mod_kernel,
        grid=(n_out // d,),
        in_specs=[pl.BlockSpec((bsz, d), lambda j: (0, 0)),
                  pl.BlockSpec((d, d), lambda j: (0, j)),
                  pl.BlockSpec((1, d), lambda j: (0, j))],
        out_specs=pl.BlockSpec((bsz, d), lambda j: (0, j)),
        out_shape=jax.ShapeDtypeStruct((bsz, n_out), F32),
        name="adaln_mod",
    )(c, w_ada, b_ada.reshape(1, n_out))


def _store_permuted(o_ref, val, n2len, first=0):
    for cb in range(val.shape[1] // LANES):
        for j in range(val.shape[0] // n2len):
            o_ref[cb, pl.ds(first + j, n2len, stride=SUBLANES), :] = val[
                j * n2len:(j + 1) * n2len, cb * LANES:(cb + 1) * LANES]


def _load_minor(x_ref, j):
    blk = x_ref[:, j * SUBLANES:(j + 1) * SUBLANES, :]
    return blk.reshape(blk.shape[0] * SUBLANES, blk.shape[2])


def _proj_kernel(xp_ref, xc_ref, xn_ref, mod_ref, w_ref, b_ref, cw_ref, cb_ref, pw_ref,
                 ps_ref, v_ref, x1_ref, x2_ref, g_ref, yp_ref, p_scr, *, tile, n2len, ch,
                 seq):
    i = pl.program_id(1)
    nt = pl.num_programs(1)
    shift = mod_ref[0:1, :]
    scale1 = 1.0 + mod_ref[1:2, :]
    xe = jnp.concatenate([xp_ref[...], xc_ref[...], xn_ref[...]], axis=0)
    ue = (xe * scale1 + shift).astype(BF16)
    uc = ue[HALO:HALO + tile, :]
    row = lax.broadcasted_iota(jnp.int32, (tile + 2 * HALO, 1), 0)
    valid = ((row >= HALO) | (i > 0)) & ((row < tile + HALO) | (i < nt - 1))

    for k, o_ref in enumerate((v_ref, x1_ref, x2_ref)):
        cols = slice(k * ch, (k + 1) * ch)
        p = jnp.dot(ue, w_ref[:, cols], preferred_element_type=F32) + b_ref[:, cols]
        p_scr[...] = jnp.where(valid, p, 0.0)
        s = (cb_ref[:, cols]
             + cw_ref[0:1, cols] * p_scr[HALO - 1:HALO - 1 + tile, :]
             + cw_ref[1:2, cols] * p_scr[HALO:HALO + tile, :]
             + cw_ref[2:3, cols] * p_scr[HALO + 1:HALO + 1 + tile, :])
        _store_permuted(o_ref, s, n2len)

    hg = jnp.dot(uc, w_ref[:, 3 * ch:4 * ch], preferred_element_type=F32) + b_ref[:, 3 * ch:4 * ch]
    _store_permuted(g_ref, _silu(hg), n2len)

    dp = ps_ref.shape[1]
    c0 = 4 * ch
    pin = jnp.dot(ue, w_ref[:, c0:c0 + dp], preferred_element_type=F32) + b_ref[:, c0:c0 + dp]
    p_scr[...] = jnp.where(valid, pin, 0.0)
    pgate = jnp.dot(uc, w_ref[:, c0 + dp:c0 + 2 * dp], preferred_element_type=F32) + b_ref[:, c0 + dp:c0 + 2 * dp]
    pos = i * tile + lax.broadcasted_iota(jnp.int32, (tile, 1), 0)
    pg = dp // len(POOL_WINDOWS)
    groups = []
    for g, win in enumerate(POOL_WINDOWS):
        lanes = slice(g * pg, (g + 1) * pg)
        half = win // 2
        acc = p_scr[HALO - half:HALO - half + tile, lanes]
        for d in range(-half + 1, half):
            acc = acc + p_scr[HALO + d:HALO + d + tile, lanes]
        cnt = (jnp.minimum(pos + half, seq) - jnp.maximum(pos - half, 0)).astype(F32)
        diff = acc / cnt - p_scr[HALO:HALO + tile, lanes]
        groups.append(jnp.dot(diff.astype(BF16), pw_ref[g], preferred_element_type=F32))
    yp = jnp.concatenate(groups, axis=1) * ps_ref[...] * _silu(pgate)
    yp_ref[...] = yp.astype(yp_ref.dtype)


def _input_projection(x, mod3, w_in, b_in, conv_w, conv_b, pool_w, pool_scale, n2len):
    bsz, seq, d = x.shape
    ch = conv_w.shape[1] // 3
    dp = pool_scale.shape[0]
    n1h = seq // n2len
    tile = SUBLANES * n2len
    nt = seq // tile
    hb = tile // HALO
    n_proj = w_in.shape[1]
    cbn = ch // LANES
    perm = jax.ShapeDtypeStruct((bsz, n1h // SUBLANES, cbn, tile, LANES), F32)
    perm_spec = pl.BlockSpec((None, None, cbn, tile, LANES), lambda b, i: (b, i, 0, 0, 0))
    full = lambda shape: pl.BlockSpec(shape, lambda b, i: (0,) * len(shape))
    kern = functools.partial(_proj_kernel, tile=tile, n2len=n2len, ch=ch, seq=seq)
    return pl.pallas_call(
        kern,
        grid=(bsz, nt),
        in_specs=[
            pl.BlockSpec((None, HALO, d), lambda b, i: (b, jnp.maximum(i * hb - 1, 0), 0)),
            pl.BlockSpec((None, tile, d), lambda b, i: (b, i, 0)),
            pl.BlockSpec((None, HALO, d), lambda b, i: (b, jnp.minimum((i + 1) * hb, seq // HALO - 1), 0)),
            pl.BlockSpec((None, 3, d), lambda b, i: (b, 0, 0)),
            full((d, n_proj)), full((1, n_proj)), full((3, 3 * ch)), full((1, 3 * ch)),
            full(pool_w.shape), full((1, dp)),
        ],
        out_specs=[perm_spec, perm_spec, perm_spec, perm_spec,
                   pl.BlockSpec((None, tile, dp), lambda b, i: (b, i, 0))],
        out_shape=[perm, perm, perm, perm, jax.ShapeDtypeStruct((bsz, seq, dp), BF16)],
        scratch_shapes=[pltpu.VMEM((tile + 2 * HALO, ch), F32)],
        compiler_params=pltpu.CompilerParams(
            dimension_semantics=("parallel", "arbitrary"), vmem_limit_bytes=VMEM_LIMIT),
        name="in_proj_conv_pool",
    )(x, x, x, mod3, w_in.astype(BF16), b_in.reshape(1, n_proj), conv_w,
      conv_b.reshape(1, 3 * ch), pool_w.astype(BF16), pool_scale.reshape(1, dp))


def _filter_mlp_kernel(z_ref, w1_ref, b1_ref, wi_ref, bi_ref, fr_ref, wo_ref, ad_ref,
                       of_ref, or_ref, *, n2len):
    z = z_ref[...]
    fr = fr_ref[...]
    h = jnp.sin(fr * (jnp.dot(z.astype(BF16), w1_ref[...],
                              preferred_element_type=F32) + b1_ref[...]))
    for l in range(wi_ref.shape[0]):
        h = jnp.sin(fr * (jnp.dot(h.astype(BF16), wi_ref[l],
                                  preferred_element_type=F32) + bi_ref[l:l + 1, :]))
    hb = h.astype(BF16)
    half_lanes = LANES // 2
    n1_half = z.shape[0] // n2len
    for half in range(2):
        t = z[:, half * half_lanes:half * half_lanes + 1]
        decay = jnp.exp(-t * ad_ref[...])
        for d, o_ref in enumerate((of_ref, or_ref)):
            k = jnp.dot(hb, wo_ref[d, half], preferred_element_type=F32)
            _store_permuted(o_ref, k * decay, n2len, first=half * n1_half)


def _block_diag2(w):
    zero = jnp.zeros_like(w)
    return jnp.concatenate([jnp.concatenate([w, zero], axis=-1),
                            jnp.concatenate([zero, w], axis=-1)], axis=-2)


def _filter_taps(seq, n2len, w1, b1, w_inner, b_inner, w_out, freq, n_ch, order):
    emb, hid = w1.shape
    n_inner = w_inner.shape[0]
    oc = order * n_ch
    hl = LANES // 2
    assert emb <= hl and hid <= hl
    tile = SUBLANES * n2len
    half = tile // 2
    groups = seq // tile
    z = jnp.pad(_filter_pos_features(seq, emb), ((0, 0), (0, hl - emb)))
    zp = jnp.transpose(z.reshape(groups, 2, half, hl), (0, 2, 1, 3)).reshape(groups * half, LANES)
    ph = hl - hid
    w1d = _block_diag2(jnp.pad(w1, ((0, hl - emb), (0, ph))))
    b1d = jnp.tile(jnp.pad(b1, (0, ph)), 2).reshape(1, LANES)
    wid = _block_diag2(jnp.pad(w_inner, ((0, 0), (0, ph), (0, ph))))
    bid = jnp.tile(jnp.pad(b_inner, ((0, 0), (0, ph))), (1, 2))
    frd = jnp.tile(jnp.pad(freq, (0, ph), constant_values=1.0), 2).reshape(1, LANES)
    wo = jnp.transpose(w_out.reshape(hid, order, 2, n_ch), (2, 0, 1, 3)).reshape(2, hid, oc)
    wo = jnp.pad(wo, ((0, 0), (0, ph), (0, 0)))
    zero = jnp.zeros_like(wo)
    wo4 = jnp.stack([jnp.concatenate([wo, zero], axis=1),
                     jnp.concatenate([zero, wo], axis=1)], axis=1)
    min_decay = math.log(DECAY_TARGET) / SLOW_DECAY_PCT
    max_decay = math.log(DECAY_TARGET) / FAST_DECAY_PCT
    absdelta = jnp.abs(jnp.linspace(min_decay, max_decay, n_ch, dtype=F32))
    absdelta = jnp.tile(absdelta, order).reshape(1, oc)

    full = lambda shape: pl.BlockSpec(shape, lambda i: (0,) * len(shape))
    kern = functools.partial(_filter_mlp_kernel, n2len=n2len)
    taps = jax.ShapeDtypeStruct((groups, oc // LANES, tile, LANES), F32)
    taps_spec = pl.BlockSpec((None, oc // LANES, tile, LANES), lambda i: (i, 0, 0, 0))
    return pl.pallas_call(
        kern,
        grid=(groups,),
        in_specs=[
            pl.BlockSpec((half, LANES), lambda i: (i, 0)),
            full((LANES, LANES)), full((1, LANES)), full((n_inner, LANES, LANES)),
            full((n_inner, LANES)), full((1, LANES)), full((2, 2, LANES, oc)), full((1, oc)),
        ],
        out_specs=[taps_spec, taps_spec],
        out_shape=[taps, taps],
        compiler_params=pltpu.CompilerParams(
            dimension_semantics=("arbitrary",), vmem_limit_bytes=VMEM_LIMIT),
        name="filter_mlp",
    )(zp, w1d.astype(BF16), b1d, wid.astype(BF16), bid, frd, wo4.astype(BF16), absdelta)


def _stage1(a_scr, g_ref, x_ref, base, cn2, n_rows, pitch):
    for j in range(cn2):
        r = jnp.dot(g_ref[base + j], _load_minor(x_ref, j).astype(BF16),
                    preferred_element_type=F32)
        a_scr[pl.ds(base + j, n_rows, stride=pitch), :] = r


def _resident(shape):
    return pl.BlockSpec(shape, lambda *_: (0,) * len(shape), pipeline_mode=pl.Buffered(1))


def _spectrum_views(a_scr, chunk, ck1, n1h, pitch):
    re0 = pl.multiple_of(chunk * (ck1 * pitch), SUBLANES)
    im0 = pl.multiple_of(chunk * (ck1 * pitch) + n1h * pitch, SUBLANES)
    return a_scr.at[pl.ds(re0, ck1 * pitch)], a_scr.at[pl.ds(im0, ck1 * pitch)]


def _filter_fft_kernel(hf_ref, hr_ref, g_ref, gr_ref, h2_ref, o_ref, a_scr,
                       *, ns, cn2, ck1, n1h, n2len, pitch):
    s = pl.program_id(1)

    @pl.when(s < ns)
    def _():
        for j in range(cn2):
            jr = cn2 - 1 - j
            m = s * cn2 + j
            r = jnp.dot(g_ref[m], _load_minor(hf_ref, j).astype(BF16),
                        preferred_element_type=F32)
            r = r + jnp.dot(gr_ref[n2len - 1 - m], _load_minor(hr_ref, jr).astype(BF16),
                            preferred_element_type=F32)
            a_scr[pl.ds(m, 2 * n1h, stride=pitch), :] = r

    @pl.when(s >= ns)
    def _():
        a_re, a_im = _spectrum_views(a_scr, s - ns, ck1, n1h, pitch)
        for j in range(ck1):
            rows = jnp.concatenate([a_re[j * pitch:j * pitch + n2len, :],
                                    a_im[j * pitch:j * pitch + n2len, :]], axis=0)
            o_ref[j] = jnp.dot(h2_ref[...], rows.astype(BF16),
                               preferred_element_type=F32).astype(o_ref.dtype)


def _filter_spectrum(taps_f, taps_r, g_fwd, g_rev, h2, n2len):
    groups, ocb, _, _ = taps_f.shape
    oc = ocb * LANES
    n1h = groups * SUBLANES
    ns = PHASE_STEPS
    cn2, ck1 = n2len // ns, n1h // ns
    pitch = n2len + SUBLANES
    kern = functools.partial(_filter_fft_kernel, ns=ns, cn2=cn2, ck1=ck1, n1h=n1h,
                             n2len=n2len, pitch=pitch)
    return pl.pallas_call(
        kern,
        grid=(oc // LANES, 2 * ns),
        in_specs=[
            pl.BlockSpec((groups, None, cn2 * SUBLANES, LANES),
                         lambda c, s: (0, c, jnp.minimum(s, ns - 1), 0)),
            pl.BlockSpec((groups, None, cn2 * SUBLANES, LANES),
                         lambda c, s: (0, c, jnp.maximum(ns - 1 - s, 0), 0)),
            _resident(g_fwd.shape), _resident(g_rev.shape), _resident(h2.shape),
        ],
        out_specs=pl.BlockSpec((ck1, 2 * n2len, LANES), lambda c, s: (jnp.maximum(s - ns, 0), 0, c)),
        out_shape=jax.ShapeDtypeStruct((n1h, 2 * n2len, oc), BF16),
        scratch_shapes=[pltpu.VMEM((2 * n1h * pitch, LANES), F32)],
        compiler_params=pltpu.CompilerParams(
            dimension_semantics=("parallel", "arbitrary"), vmem_limit_bytes=VMEM_LIMIT),
        name="filter_spectrum",
    )(taps_f, taps_r, g_fwd, g_rev, h2)


def _long_conv_kernel(*refs, ns, cn2, ck1, n1h, n2len, pitch, has_gate):
    if has_gate:
        (vf_ref, g_ref, kf_ref, h2_ref, h2i_ref, gi_ref, vb_ref, xm_ref, gate_ref, hb_ref,
         o_ref, a_scr) = refs
    else:
        (vf_ref, g_ref, kf_ref, h2_ref, h2i_ref, gi_ref, vb_ref, xm_ref, hb_ref,
         o_ref, a_scr) = refs
        gate_ref = None
    s = pl.program_id(2)

    @pl.when(s < ns)
    def _():
        _stage1(a_scr, g_ref, vf_ref, s * cn2, cn2, 2 * n1h, pitch)

    @pl.when((s >= ns) & (s < 2 * ns))
    def _():
        a_re, a_im = _spectrum_views(a_scr, s - ns, ck1, n1h, pitch)
        for j in range(ck1):
            rows = jnp.concatenate([a_re[j * pitch:j * pitch + n2len, :],
                                    a_im[j * pitch:j * pitch + n2len, :]], axis=0)
            u = jnp.dot(h2_ref[...], rows.astype(BF16), preferred_element_type=F32)
            ur, ui = u[:n2len], u[n2len:]
            kr = kf_ref[j, :n2len, :].astype(F32)
            ki = kf_ref[j, n2len:, :].astype(F32)
            y = jnp.concatenate([ur * kr - ui * ki, ur * ki + ui * kr], axis=0)
            bb = jnp.dot(h2i_ref[...], y.astype(BF16), preferred_element_type=F32)
            a_re[j * pitch:j * pitch + n2len, :] = bb[:n2len]
            a_im[j * pitch:j * pitch + n2len, :] = bb[n2len:]

    @pl.when(s >= 2 * ns)
    def _():
        base = (s - 2 * ns) * cn2
        for j in range(cn2):
            rows = a_scr[pl.ds(base + j, 2 * n1h, stride=pitch), :]
            y = jnp.dot(gi_ref[base + j], rows.astype(BF16), preferred_element_type=F32)
            z = _load_minor(xm_ref, j) * (y + hb_ref[...] * _load_minor(vb_ref, j))
            if gate_ref is not None:
                z = z * _load_minor(gate_ref, j)
            o_ref[:, j * SUBLANES:(j + 1) * SUBLANES, :] = z.reshape(
                n1h // SUBLANES, SUBLANES, z.shape[1])


def _long_conv(u_perm, mult_perm, gate_perm, kf, hbias, order_idx, tables):
    g_fwd, g_inv, _, h2, h2i = tables
    bsz, groups, cbn, tile, _ = u_perm.shape
    n2len = tile // SUBLANES
    n1h = groups * SUBLANES
    ns = PHASE_STEPS
    cn2, ck1 = n2len // ns, n1h // ns
    pitch = n2len + SUBLANES
    has_gate = gate_perm is not None

    def late(s):
        return jnp.clip(s - 2 * ns, 0, ns - 1)

    data_early = pl.BlockSpec((None, groups, None, cn2 * SUBLANES, LANES),
                              lambda c, b, s: (b, 0, c, jnp.minimum(s, ns - 1), 0))
    data_late = pl.BlockSpec((None, groups, None, cn2 * SUBLANES, LANES),
                             lambda c, b, s: (b, 0, c, late(s), 0))
    in_specs = [
        data_early,
        _resident(g_fwd.shape),
        pl.BlockSpec((ck1, 2 * n2len, LANES),
                     lambda c, b, s: (jnp.clip(s - ns, 0, ns - 1), 0, order_idx * cbn + c)),
        _resident(h2.shape), _resident(h2i.shape), _resident(g_inv.shape),
        data_late, data_late,
    ]
    args = [u_perm, g_fwd, kf, h2, h2i, g_inv, u_perm, mult_perm]
    if has_gate:
        in_specs.append(data_late)
        args.append(gate_perm)
    in_specs.append(pl.BlockSpec((None, 1, LANES), lambda c, b, s: (order_idx, 0, c)))
    args.append(hbias)
    kern = functools.partial(_long_conv_kernel, ns=ns, cn2=cn2, ck1=ck1, n1h=n1h,
                             n2len=n2len, pitch=pitch, has_gate=has_gate)
    return pl.pallas_call(
        kern,
        grid=(cbn, bsz, 3 * ns),
        in_specs=in_specs,
        out_specs=data_late,
        out_shape=jax.ShapeDtypeStruct(u_perm.shape, F32),
        scratch_shapes=[pltpu.VMEM((2 * n1h * pitch, LANES), F32)],
        compiler_params=pltpu.CompilerParams(
            dimension_semantics=("parallel", "parallel", "arbitrary"),
            vmem_limit_bytes=VMEM_LIMIT),
        name="long_conv",
    )(*args)


def _out_kernel(x_ref, yh_ref, yp_ref, mod_ref, w_ref, b_ref, g_ref, beta_ref, o_ref,
                *, n2len, alpha):
    cbn = yh_ref.shape[0]
    ch = cbn * LANES
    yh = jnp.concatenate(
        [jnp.concatenate([yh_ref[cb, pl.ds(j, n2len, stride=SUBLANES), :]
                          for cb in range(cbn)], axis=1)
         for j in range(SUBLANES)], axis=0)
    acc = jnp.dot(yh.astype(BF16), w_ref[0:ch, :], preferred_element_type=F32)
    acc = acc + jnp.dot(yp_ref[...], w_ref[ch:, :], preferred_element_type=F32)
    h = alpha * x_ref[...] + mod_ref[2:3, :] * (acc + b_ref[...])
    mu = jnp.mean(h, axis=-1, keepdims=True)
    hc = h - mu
    var = jnp.mean(hc * hc, axis=-1, keepdims=True)
    o_ref[...] = hc * lax.rsqrt(var + LN_EPS) * g_ref[...] + beta_ref[...]


def _output_projection(x, yh_perm, yp, mod3, w_out, b_out, ln_g, ln_b, alpha):
    bsz, seq, d = x.shape
    _, _, cbn, tile, _ = yh_perm.shape
    ch = cbn * LANES
    n2len = tile // SUBLANES
    dp = yp.shape[2]
    full = lambda shape: pl.BlockSpec(shape, lambda b, i: (0,) * len(shape))
    kern = functools.partial(_out_kernel, n2len=n2len, alpha=alpha)
    return pl.pallas_call(
        kern,
        grid=(bsz, seq // tile),
        in_specs=[
            pl.BlockSpec((None, tile, d), lambda b, i: (b, i, 0)),
            pl.BlockSpec((None, None, cbn, tile, LANES), lambda b, i: (b, i, 0, 0, 0)),
            pl.BlockSpec((None, tile, dp), lambda b, i: (b, i, 0)),
            pl.BlockSpec((None, 3, d), lambda b, i: (b, 0, 0)),
            full((ch + dp, d)), full((1, d)), full((1, d)), full((1, d)),
        ],
        out_specs=pl.BlockSpec((None, tile, d), lambda b, i: (b, i, 0)),
        out_shape=jax.ShapeDtypeStruct(x.shape, x.dtype),
        compiler_params=pltpu.CompilerParams(
            dimension_semantics=("parallel", "arbitrary"), vmem_limit_bytes=VMEM_LIMIT),
        name="out_proj_deepnorm",
    )(x, yh_perm, yp, mod3, w_out.astype(BF16), b_out.reshape(1, d), ln_g.reshape(1, d),
      ln_b.reshape(1, d))


def _forward(x, c, w_ada, b_ada, w_in, b_in, conv_w, conv_b, filt_w1, filt_b1, filt_w_inner,
             filt_b_inner, filt_w_out, filt_freq, hyena_bias, pool_w, pool_scale, w_out,
             b_out, ln_g, ln_b, *, n2len):
    bsz, seq, d = x.shape
    depth = w_ada.shape[0]
    order, n_ch = hyena_bias.shape[1], hyena_bias.shape[2]
    alpha = (2.0 * depth) ** 0.25
    tables = _dft_tables(seq, n2len)
    h = x
    for layer in range(depth):
        mod3 = _modulation(c, w_ada[layer], b_ada[layer]).reshape(bsz, 3, d)
        v, x1, x2, gate, yp = _input_projection(
            h, mod3, w_in[layer], b_in[layer], conv_w[layer], conv_b[layer], pool_w[layer],
            pool_scale[layer], n2len)
        taps_f, taps_r = _filter_taps(
            seq, n2len, filt_w1[layer], filt_b1[layer], filt_w_inner[layer],
            filt_b_inner[layer], filt_w_out[layer], filt_freq[layer], n_ch, order)
        kf = _filter_spectrum(taps_f, taps_r, tables[0], tables[2], tables[3], n2len)
        hbias = hyena_bias[layer].reshape(order, 1, n_ch)
        z = v
        mults = (x1, x2)
        for o in range(order):
            z = _long_conv(z, mults[o], gate if o == order - 1 else None, kf, hbias, o, tables)
        h = _output_projection(h, z, yp, mod3, w_out[layer], b_out[layer], ln_g[layer],
                               ln_b[layer], alpha)
    return h


def kernel(x, c, w_ada, b_ada, w_in, b_in, conv_w, conv_b, filt_w1, filt_b1, filt_w_inner,
           filt_b_inner, filt_w_out, filt_freq, hyena_bias, pool_w, pool_scale, w_out, b_out,
           ln_g, ln_b):
    return _forward(x, c, w_ada, b_ada, w_in, b_in, conv_w, conv_b, filt_w1, filt_b1,
                    filt_w_inner, filt_b_inner, filt_w_out, filt_freq, hyena_bias, pool_w,
                    pool_scale, w_out, b_out, ln_g, ln_b, n2len=MINOR_LEN)
```

```python
import functools
import math

import jax
import jax.numpy as jnp
from jax import lax
from jax.experimental import pallas as pl
from jax.experimental.pallas import tpu as pltpu

F32 = jnp.float32
BF16 = jnp.bfloat16
HIGHEST = lax.Precision.HIGHEST

POOL_WINDOWS = (2, 4, 8, 16)
LN_EPS = 1e-5
DECAY_TARGET = 1e-2
FAST_DECAY_PCT = 0.3
SLOW_DECAY_PCT = 1.5

LANES = 128
SUBLANES = 8
HALO = 16
PADS = 8
MINOR_LEN = 128
PHASE_STEPS = 8
VMEM_LIMIT = 56 * 1024 * 1024


def _silu(x):
    return x * jax.nn.sigmoid(x)


def _dft_tables(seq, n2len):
    n1h = seq // n2len
    n_fft = 2 * seq
    n2i = jnp.arange(n2len, dtype=jnp.int32)
    k1 = jnp.arange(n1h, dtype=jnp.int32)
    odd = 2 * k1 + 1
    alpha = ((odd[:, None] * k1[None, :]) % (4 * n1h)).astype(F32) * (2.0 * math.pi / (4 * n1h))
    n2e = jnp.arange(n2len + 1, dtype=jnp.int32)
    beta = ((n2e[:, None] * odd[None, :]) % (2 * n_fft)).astype(F32) * (2.0 * math.pi / (2 * n_fft))
    ca, sa = jnp.cos(alpha)[None], jnp.sin(alpha)[None]
    cb, sb = jnp.cos(beta)[:, :, None], jnp.sin(beta)[:, :, None]
    cos_t = ca * cb[:-1] - sa * sb[:-1]
    sin_t = sa * cb[:-1] + ca * sb[:-1]
    g_fwd = jnp.concatenate([cos_t, -sin_t], axis=1).astype(BF16)
    g_inv = (2.0 / n_fft) * jnp.concatenate(
        [jnp.swapaxes(cos_t, 1, 2), -jnp.swapaxes(sin_t, 1, 2)], axis=2)
    g_inv = g_inv.astype(BF16)
    cos_r = ca * cb[1:] - sa * sb[1:]
    sin_r = sa * cb[1:] + ca * sb[1:]
    g_rev = jnp.concatenate([cos_r, sin_r], axis=1).astype(BF16)

    m = (n2i[:, None] * n2i[None, :]) % n2len
    phi = m.astype(F32) * (2.0 * math.pi / n2len)
    cm, sm = jnp.cos(phi), jnp.sin(phi)
    h2 = jnp.block([[cm, sm], [-sm, cm]]).astype(BF16)
    h2i = jnp.block([[cm, -sm], [sm, cm]]).astype(BF16)
    return g_fwd, g_inv, g_rev, h2, h2i


def _filter_pos_features(seq, emb):
    t = jnp.linspace(0.0, 1.0, seq, dtype=F32)[:, None]
    bands = (emb - 1) // 2
    w = (2.0 * math.pi / seq) * jnp.arange(seq, dtype=F32)[:, None]
    f = jnp.linspace(1e-4, bands - 1, bands, dtype=F32)[None, :]
    return jnp.concatenate([t, jnp.cos(f * w), -jnp.sin(f * w)], axis=-1)


def _mod_kernel(c_ref, w_ref, b_ref, o_ref):
    s = _silu(c_ref[...])
    o_ref[...] = jnp.dot(s, w_ref[...], precision=HIGHEST,
                         preferred_element_type=F32) + b_ref[...]


def _modulation(c, w_ada, b_ada):
    bsz, d = c.shape
    n_out = w_ada.shape[1]
    return pl.pallas_call(
        _mod_kernel,
        grid=(n_out // d,),
        in_specs=[pl.BlockSpec((bsz, d), lambda j: (0, 0)),
                  pl.BlockSpec((d, d), lambda j: (0, j)),
                  pl.BlockSpec((1, d), lambda j: (0, j))],
        out_specs=pl.BlockSpec((bsz, d), lambda j: (0, j)),
        out_shape=jax.ShapeDtypeStruct((bsz, n_out), F32),
        name="adaln_mod",
    )(c, w_ada, b_ada.reshape(1, n_out))


def _store_permuted(o_ref, val, n2len, first=0):
    for cb in range(val.shape[1] // LANES):
        for j in range(val.shape[0] // n2len):
            o_ref[cb, pl.ds(first + j, n2len, stride=SUBLANES), :] = val[
                j * n2len:(j + 1) * n2len, cb * LANES:(cb + 1) * LANES]


def _load_minor(x_ref, j):
    blk = x_ref[:, j * SUBLANES:(j + 1) * SUBLANES, :]
    return blk.reshape(blk.shape[0] * SUBLANES, blk.shape[2])


def _proj_kernel(xp_ref, xc_ref, xn_ref, mod_ref, w_ref, b_ref, cw_ref, cb_ref, pw_ref,
                 ps_ref, v_ref, x1_ref, x2_ref, g_ref, yp_ref, p_scr, *, tile, n2len, ch,
                 seq):
    i = pl.program_id(1)
    nt = pl.num_programs(1)
    shift = mod_ref[0:1, :]
    scale1 = 1.0 + mod_ref[1:2, :]
    xe = jnp.concatenate([xp_ref[...], xc_ref[...], xn_ref[...]], axis=0)
    ue = (xe * scale1 + shift).astype(BF16)
    uc = ue[HALO:HALO + tile, :]
    main = PADS * SUBLANES
    step = SUBLANES

    def project_permuted(cols, buf):
        p = jnp.dot(ue, w_ref[:, cols], preferred_element_type=F32) + b_ref[:, cols]
        p = jnp.concatenate([jnp.where(i > 0, p[:HALO], 0.0), p[HALO:HALO + tile],
                             jnp.where(i < nt - 1, p[HALO + tile:], 0.0)], axis=0)
        for cb in range(p.shape[1] // LANES):
            for j in range(SUBLANES):
                lo = HALO - PADS + j * n2len
                p_scr[buf, cb, pl.ds(j, n2len + 2 * PADS, stride=SUBLANES), :] = p[
                    lo:lo + n2len + 2 * PADS, cb * LANES:(cb + 1) * LANES]

    def shifted(buf, cb, d):
        return p_scr[buf, cb, main + d * step:main + d * step + tile, :]

    for k, o_ref in enumerate((v_ref, x1_ref, x2_ref)):
        buf = k % 2
        project_permuted(slice(k * ch, (k + 1) * ch), buf)
        for cb in range(ch // LANES):
            c = slice(k * ch + cb * LANES, k * ch + (cb + 1) * LANES)
            o_ref[cb] = (cb_ref[:, c] + cw_ref[0:1, c] * shifted(buf, cb, -1)
                         + cw_ref[1:2, c] * shifted(buf, cb, 0)
                         + cw_ref[2:3, c] * shifted(buf, cb, 1))

    hg = jnp.dot(uc, w_ref[:, 3 * ch:4 * ch], preferred_element_type=F32) + b_ref[:, 3 * ch:4 * ch]
    _store_permuted(g_ref, _silu(hg), n2len)

    dp = ps_ref.shape[1]
    c0 = 4 * ch
    pgate = jnp.dot(uc, w_ref[:, c0 + dp:c0 + 2 * dp], preferred_element_type=F32) + b_ref[:, c0 + dp:c0 + 2 * dp]
    _store_permuted(yp_ref, _silu(pgate), n2len)
    project_permuted(slice(c0, c0 + dp), 1)
    rho = lax.broadcasted_iota(jnp.int32, (tile, 1), 0)
    pos = i * tile + (rho % SUBLANES) * n2len + rho // SUBLANES
    for g, win in enumerate(POOL_WINDOWS):
        half = win // 2
        acc = shifted(1, g, -half)
        for d in range(-half + 1, half):
            acc = acc + shifted(1, g, d)
        cnt = (jnp.minimum(pos + half, seq) - jnp.maximum(pos - half, 0)).astype(F32)
        diff = acc / cnt - shifted(1, g, 0)
        yg = jnp.dot(diff.astype(BF16), pw_ref[g], preferred_element_type=F32)
        yp_ref[g] = yg * ps_ref[:, g * LANES:(g + 1) * LANES] * yp_ref[g]


def _input_projection(x, mod3, w_in, b_in, conv_w, conv_b, pool_w, pool_scale, n2len):
    bsz, seq, d = x.shape
    ch = conv_w.shape[1] // 3
    dp = pool_scale.shape[0]
    n1h = seq // n2len
    tile = SUBLANES * n2len
    nt = seq // tile
    hb = tile // HALO
    n_proj = w_in.shape[1]
    cbn = ch // LANES
    assert dp == ch and dp // len(POOL_WINDOWS) == LANES and max(POOL_WINDOWS) // 2 <= PADS
    perm = jax.ShapeDtypeStruct((bsz, n1h // SUBLANES, cbn, tile, LANES), F32)
    perm_spec = pl.BlockSpec((None, None, cbn, tile, LANES), lambda b, i: (b, i, 0, 0, 0))
    full = lambda shape: pl.BlockSpec(shape, lambda b, i: (0,) * len(shape))
    kern = functools.partial(_proj_kernel, tile=tile, n2len=n2len, ch=ch, seq=seq)
    return pl.pallas_call(
        kern,
        grid=(bsz, nt),
        in_specs=[
            pl.BlockSpec((None, HALO, d), lambda b, i: (b, jnp.maximum(i * hb - 1, 0), 0)),
            pl.BlockSpec((None, tile, d), lambda b, i: (b, i, 0)),
            pl.BlockSpec((None, HALO, d), lambda b, i: (b, jnp.minimum((i + 1) * hb, seq // HALO - 1), 0)),
            pl.BlockSpec((None, 3, d), lambda b, i: (b, 0, 0)),
            full((d, n_proj)), full((1, n_proj)), full((3, 3 * ch)), full((1, 3 * ch)),
            full(pool_w.shape), full((1, dp)),
        ],
        out_specs=[perm_spec] * 5,
        out_shape=[perm] * 5,
        scratch_shapes=[pltpu.VMEM((2, cbn, (n2len + 2 * PADS) * SUBLANES, LANES), F32)],
        compiler_params=pltpu.CompilerParams(
            dimension_semantics=("parallel", "arbitrary"), vmem_limit_bytes=VMEM_LIMIT),
        name="in_proj_conv_pool",
    )(x, x, x, mod3, w_in.astype(BF16), b_in.reshape(1, n_proj), conv_w,
      conv_b.reshape(1, 3 * ch), pool_w.astype(BF16), pool_scale.reshape(1, dp))


def _filter_mlp_kernel(z_ref, w1_ref, b1_ref, wi_ref, bi_ref, fr_ref, wo_ref, ad_ref,
                       of_ref, or_ref, *, n2len):
    z = z_ref[...]
    fr = fr_ref[...]
    h = jnp.sin(fr * (jnp.dot(z.astype(BF16), w1_ref[...],
                              preferred_element_type=F32) + b1_ref[...]))
    for l in range(wi_ref.shape[0]):
        h = jnp.sin(fr * (jnp.dot(h.astype(BF16), wi_ref[l],
                                  preferred_element_type=F32) + bi_ref[l:l + 1, :]))
    hb = h.astype(BF16)
    half_lanes = LANES // 2
    n1_half = z.shape[0] // n2len
    for half in range(2):
        t = z[:, half * half_lanes:half * half_lanes + 1]
        decay = jnp.exp(-t * ad_ref[...])
        for d, o_ref in enumerate((of_ref, or_ref)):
            k = jnp.dot(hb, wo_ref[d, half], preferred_element_type=F32)
            _store_permuted(o_ref, k * decay, n2len, first=half * n1_half)


def _block_diag2(w):
    zero = jnp.zeros_like(w)
    return jnp.concatenate([jnp.concatenate([w, zero], axis=-1),
                            jnp.concatenate([zero, w], axis=-1)], axis=-2)


def _filter_taps(seq, n2len, w1, b1, w_inner, b_inner, w_out, freq, n_ch, order):
    emb, hid = w1.shape
    n_inner = w_inner.shape[0]
    oc = order * n_ch
    hl = LANES // 2
    assert emb <= hl and hid <= hl
    tile = SUBLANES * n2len
    half = tile // 2
    groups = seq // tile
    z = jnp.pad(_filter_pos_features(seq, emb), ((0, 0), (0, hl - emb)))
    zp = jnp.transpose(z.reshape(groups, 2, half, hl), (0, 2, 1, 3)).reshape(groups * half, LANES)
    ph = hl - hid
    w1d = _block_diag2(jnp.pad(w1, ((0, hl - emb), (0, ph))))
    b1d = jnp.tile(jnp.pad(b1, (0, ph)), 2).reshape(1, LANES)
    wid = _block_diag2(jnp.pad(w_inner, ((0, 0), (0, ph), (0, ph))))
    bid = jnp.tile(jnp.pad(b_inner, ((0, 0), (0, ph))), (1, 2))
    frd = jnp.tile(jnp.pad(freq, (0, ph), constant_values=1.0), 2).reshape(1, LANES)
    wo = jnp.transpose(w_out.reshape(hid, order, 2, n_ch), (2, 0, 1, 3)).reshape(2, hid, oc)
    wo = jnp.pad(wo, ((0, 0), (0, ph), (0, 0)))
    zero = jnp.zeros_like(wo)
    wo4 = jnp.stack([jnp.concatenate([wo, zero], axis=1),
                     jnp.concatenate([zero, wo], axis=1)], axis=1)
    min_decay = math.log(DECAY_TARGET) / SLOW_DECAY_PCT
    max_decay = math.log(DECAY_TARGET) / FAST_DECAY_PCT
    absdelta = jnp.abs(jnp.linspace(min_decay, max_decay, n_ch, dtype=F32))
    absdelta = jnp.tile(absdelta, order).reshape(1, oc)

    full = lambda shape: pl.BlockSpec(shape, lambda i: (0,) * len(shape))
    kern = functools.partial(_filter_mlp_kernel, n2len=n2len)
    taps = jax.ShapeDtypeStruct((groups, oc // LANES, tile, LANES), F32)
    taps_spec = pl.BlockSpec((None, oc // LANES, tile, LANES), lambda i: (i, 0, 0, 0))
    return pl.pallas_call(
        kern,
        grid=(groups,),
        in_specs=[
            pl.BlockSpec((half, LANES), lambda i: (i, 0)),
            full((LANES, LANES)), full((1, LANES)), full((n_inner, LANES, LANES)),
            full((n_inner, LANES)), full((1, LANES)), full((2, 2, LANES, oc)), full((1, oc)),
        ],
        out_specs=[taps_spec, taps_spec],
        out_shape=[taps, taps],
        compiler_params=pltpu.CompilerParams(
            dimension_semantics=("arbitrary",), vmem_limit_bytes=VMEM_LIMIT),
        name="filter_mlp",
    )(zp, w1d.astype(BF16), b1d, wid.astype(BF16), bid, frd, wo4.astype(BF16), absdelta)


def _stage1(a_scr, g_ref, x_ref, base, cn2, n_rows, pitch):
    for j in range(cn2):
        r = jnp.dot(g_ref[base + j], _load_minor(x_ref, j).astype(BF16),
                    preferred_element_type=F32)
        a_scr[pl.ds(base + j, n_rows, stride=pitch), :] = r


def _resident(shape):
    return pl.BlockSpec(shape, lambda *_: (0,) * len(shape), pipeline_mode=pl.Buffered(1))


def _spectrum_views(a_scr, chunk, ck1, n1h, pitch):
    re0 = pl.multiple_of(chunk * (ck1 * pitch), SUBLANES)
    im0 = pl.multiple_of(chunk * (ck1 * pitch) + n1h * pitch, SUBLANES)
    return a_scr.at[pl.ds(re0, ck1 * pitch)], a_scr.at[pl.ds(im0, ck1 * pitch)]


def _filter_fft_kernel(hf_ref, hr_ref, g_ref, gr_ref, h2_ref, o_ref, a_scr,
                       *, ns, cn2, ck1, n1h, n2len, pitch):
    s = pl.program_id(1)

    @pl.when(s < ns)
    def _():
        for j in range(cn2):
            jr = cn2 - 1 - j
            m = s * cn2 + j
            r = jnp.dot(g_ref[m], _load_minor(hf_ref, j).astype(BF16),
                        preferred_element_type=F32)
            r = r + jnp.dot(gr_ref[n2len - 1 - m], _load_minor(hr_ref, jr).astype(BF16),
                            preferred_element_type=F32)
            a_scr[pl.ds(m, 2 * n1h, stride=pitch), :] = r

    @pl.when(s >= ns)
    def _():
        a_re, a_im = _spectrum_views(a_scr, s - ns, ck1, n1h, pitch)
        for j in range(ck1):
            rows = jnp.concatenate([a_re[j * pitch:j * pitch + n2len, :],
                                    a_im[j * pitch:j * pitch + n2len, :]], axis=0)
            o_ref[j] = jnp.dot(h2_ref[...], rows.astype(BF16),
                               preferred_element_type=F32).astype(o_ref.dtype)


def _filter_spectrum(taps_f, taps_r, g_fwd, g_rev, h2, n2len):
    groups, ocb, _, _ = taps_f.shape
    oc = ocb * LANES
    n1h = groups * SUBLANES
    ns = PHASE_STEPS
    cn2, ck1 = n2len // ns, n1h // ns
    pitch = n2len + SUBLANES
    kern = functools.partial(_filter_fft_kernel, ns=ns, cn2=cn2, ck1=ck1, n1h=n1h,
                             n2len=n2len, pitch=pitch)
    return pl.pallas_call(
        kern,
        grid=(oc // LANES, 2 * ns),
        in_specs=[
            pl.BlockSpec((groups, None, cn2 * SUBLANES, LANES),
                         lambda c, s: (0, c, jnp.minimum(s, ns - 1), 0)),
            pl.BlockSpec((groups, None, cn2 * SUBLANES, LANES),
                         lambda c, s: (0, c, jnp.maximum(ns - 1 - s, 0), 0)),
            _resident(g_fwd.shape), _resident(g_rev.shape), _resident(h2.shape),
        ],
        out_specs=pl.BlockSpec((ck1, 2 * n2len, LANES), lambda c, s: (jnp.maximum(s - ns, 0), 0, c)),
        out_shape=jax.ShapeDtypeStruct((n1h, 2 * n2len, oc), BF16),
        scratch_shapes=[pltpu.VMEM((2 * n1h * pitch, LANES), F32)],
        compiler_params=pltpu.CompilerParams(
            dimension_semantics=("parallel", "arbitrary"), vmem_limit_bytes=VMEM_LIMIT),
        name="filter_spectrum",
    )(taps_f, taps_r, g_fwd, g_rev, h2)


def _long_conv_kernel(*refs, ns, cn2, ck1, n1h, n2len, pitch, has_gate):
    if has_gate:
        (vf_ref, g_ref, kf_ref, h2_ref, h2i_ref, gi_ref, vb_ref, xm_ref, gate_ref, hb_ref,
         o_ref, a_scr) = refs
    else:
        (vf_ref, g_ref, kf_ref, h2_ref, h2i_ref, gi_ref, vb_ref, xm_ref, hb_ref,
         o_ref, a_scr) = refs
        gate_ref = None
    s = pl.program_id(2)

    @pl.when(s < ns)
    def _():
        _stage1(a_scr, g_ref, vf_ref, s * cn2, cn2, 2 * n1h, pitch)

    @pl.when((s >= ns) & (s < 2 * ns))
    def _():
        a_re, a_im = _spectrum_views(a_scr, s - ns, ck1, n1h, pitch)
        for j in range(ck1):
            rows = jnp.concatenate([a_re[j * pitch:j * pitch + n2len, :],
                                    a_im[j * pitch:j * pitch + n2len, :]], axis=0)
            u = jnp.dot(h2_ref[...], rows.astype(BF16), preferred_element_type=F32)
            ur, ui = u[:n2len], u[n2len:]
            kr = kf_ref[j, :n2len, :].astype(F32)
            ki = kf_ref[j, n2len:, :].astype(F32)
            y = jnp.concatenate([ur * kr - ui * ki, ur * ki + ui * kr], axis=0)
            bb = jnp.dot(h2i_ref[...], y.astype(BF16), preferred_element_type=F32)
            a_re[j * pitch:j * pitch + n2len, :] = bb[:n2len]
            a_im[j * pitch:j * pitch + n2len, :] = bb[n2len:]

    @pl.when(s >= 2 * ns)
    def _():
        base = (s - 2 * ns) * cn2
        for j in range(cn2):
            rows = a_scr[pl.ds(base + j, 2 * n1h, stride=pitch), :]
            y = jnp.dot(gi_ref[base + j], rows.astype(BF16), preferred_element_type=F32)
            z = _load_minor(xm_ref, j) * (y + hb_ref[...] * _load_minor(vb_ref, j))
            if gate_ref is not None:
                z = z * _load_minor(gate_ref, j)
            o_ref[:, j * SUBLANES:(j + 1) * SUBLANES, :] = z.reshape(
                n1h // SUBLANES, SUBLANES, z.shape[1])


def _long_conv(u_perm, mult_perm, gate_perm, kf, hbias, order_idx, tables):
    g_fwd, g_inv, _, h2, h2i = tables
    bsz, groups, cbn, tile, _ = u_perm.shape
    n2len = tile // SUBLANES
    n1h = groups * SUBLANES
    ns = PHASE_STEPS
    cn2, ck1 = n2len // ns, n1h // ns
    pitch = n2len + SUBLANES
    has_gate = gate_perm is not None

    def late(s):
        return jnp.clip(s - 2 * ns, 0, ns - 1)

    data_early = pl.BlockSpec((None, groups, None, cn2 * SUBLANES, LANES),
                              lambda c, b, s: (b, 0, c, jnp.minimum(s, ns - 1), 0))
    data_late = pl.BlockSpec((None, groups, None, cn2 * SUBLANES, LANES),
                             lambda c, b, s: (b, 0, c, late(s), 0))
    in_specs = [
        data_early,
        _resident(g_fwd.shape),
        pl.BlockSpec((ck1, 2 * n2len, LANES),
                     lambda c, b, s: (jnp.clip(s - ns, 0, ns - 1), 0, order_idx * cbn + c)),
        _resident(h2.shape), _resident(h2i.shape), _resident(g_inv.shape),
        data_late, data_late,
    ]
    args = [u_perm, g_fwd, kf, h2, h2i, g_inv, u_perm, mult_perm]
    if has_gate:
        in_specs.append(data_late)
        args.append(gate_perm)
    in_specs.append(pl.BlockSpec((None, 1, LANES), lambda c, b, s: (order_idx, 0, c)))
    args.append(hbias)
    kern = functools.partial(_long_conv_kernel, ns=ns, cn2=cn2, ck1=ck1, n1h=n1h,
                             n2len=n2len, pitch=pitch, has_gate=has_gate)
    return pl.pallas_call(
        kern,
        grid=(cbn, bsz, 3 * ns),
        in_specs=in_specs,
        out_specs=data_late,
        out_shape=jax.ShapeDtypeStruct(u_perm.shape, F32),
        scratch_shapes=[pltpu.VMEM((2 * n1h * pitch, LANES), F32)],
        compiler_params=pltpu.CompilerParams(
            dimension_semantics=("parallel", "parallel", "arbitrary"),
            vmem_limit_bytes=VMEM_LIMIT),
        name="long_conv",
    )(*args)


def _out_kernel(x_ref, yh_ref, yp_ref, mod_ref, w_ref, b_ref, g_ref, beta_ref, o_ref,
                *, n2len, alpha):
    def natural_order(ref):
        return jnp.concatenate(
            [jnp.concatenate([ref[cb, pl.ds(j, n2len, stride=SUBLANES), :]
                              for cb in range(ref.shape[0])], axis=1)
             for j in range(SUBLANES)], axis=0).astype(BF16)

    ch = yh_ref.shape[0] * LANES
    acc = jnp.dot(natural_order(yh_ref), w_ref[0:ch, :], preferred_element_type=F32)
    acc = acc + jnp.dot(natural_order(yp_ref), w_ref[ch:, :], preferred_element_type=F32)
    h = alpha * x_ref[...] + mod_ref[2:3, :] * (acc + b_ref[...])
    mu = jnp.mean(h, axis=-1, keepdims=True)
    hc = h - mu
    var = jnp.mean(hc * hc, axis=-1, keepdims=True)
    o_ref[...] = hc * lax.rsqrt(var + LN_EPS) * g_ref[...] + beta_ref[...]


def _output_projection(x, yh_perm, yp, mod3, w_out, b_out, ln_g, ln_b, alpha):
    bsz, seq, d = x.shape
    _, _, cbn, tile, _ = yh_perm.shape
    ch = cbn * LANES
    n2len = tile // SUBLANES
    dp = yp.shape[2] * LANES
    full = lambda shape: pl.BlockSpec(shape, lambda b, i: (0,) * len(shape))
    kern = functools.partial(_out_kernel, n2len=n2len, alpha=alpha)
    return pl.pallas_call(
        kern,
        grid=(bsz, seq // tile),
        in_specs=[
            pl.BlockSpec((None, tile, d), lambda b, i: (b, i, 0)),
            pl.BlockSpec((None, None, cbn, tile, LANES), lambda b, i: (b, i, 0, 0, 0)),
            pl.BlockSpec((None, None, dp // LANES, tile, LANES), lambda b, i: (b, i, 0, 0, 0)),
            pl.BlockSpec((None, 3, d), lambda b, i: (b, 0, 0)),
            full((ch + dp, d)), full((1, d)), full((1, d)), full((1, d)),
        ],
        out_specs=pl.BlockSpec((None, tile, d), lambda b, i: (b, i, 0)),
        out_shape=jax.ShapeDtypeStruct(x.shape, x.dtype),
        compiler_params=pltpu.CompilerParams(
            dimension_semantics=("parallel", "arbitrary"), vmem_limit_bytes=VMEM_LIMIT),
        name="out_proj_deepnorm",
    )(x, yh_perm, yp, mod3, w_out.astype(BF16), b_out.reshape(1, d), ln_g.reshape(1, d),
      ln_b.reshape(1, d))


def _forward(x, c, w_ada, b_ada, w_in, b_in, conv_w, conv_b, filt_w1, filt_b1, filt_w_inner,
             filt_b_inner, filt_w_out, filt_freq, hyena_bias, pool_w, pool_scale, w_out,
             b_out, ln_g, ln_b, *, n2len):
    bsz, seq, d = x.shape
    depth = w_ada.shape[0]
    order, n_ch = hyena_bias.shape[1], hyena_bias.shape[2]
    alpha = (2.0 * depth) ** 0.25
    tables = _dft_tables(seq, n2len)
    h = x
    for layer in range(depth):
        mod3 = _modulation(c, w_ada[layer], b_ada[layer]).reshape(bsz, 3, d)
        v, x1, x2, gate, yp = _input_projection(
            h, mod3, w_in[layer], b_in[layer], conv_w[layer], conv_b[layer], pool_w[layer],
            pool_scale[layer], n2len)
        taps_f, taps_r = _filter_taps(
            seq, n2len, filt_w1[layer], filt_b1[layer], filt_w_inner[layer],
            filt_b_inner[layer], filt_w_out[layer], filt_freq[layer], n_ch, order)
        kf = _filter_spectrum(taps_f, taps_r, tables[0], tables[2], tables[3], n2len)
        hbias = hyena_bias[layer].reshape(order, 1, n_ch)
        z = v
        mults = (x1, x2)
        for o in range(order):
            z = _long_conv(z, mults[o], gate if o == order - 1 else None, kf, hbias, o, tables)
        h = _output_projection(h, z, yp, mod3, w_out[layer], b_out[layer], ln_g[layer],
                               ln_b[layer], alpha)
    return h


def kernel(x, c, w_ada, b_ada, w_in, b_in, conv_w, conv_b, filt_w1, filt_b1, filt_w_inner,
           filt_b_inner, filt_w_out, filt_freq, hyena_bias, pool_w, pool_scale, w_out, b_out,
           ln_g, ln_b):
    return _forward(x, c, w_ada, b_ada, w_in, b_in, conv_w, conv_b, filt_w1, filt_b1,
                    filt_w_inner, filt_b_inner, filt_w_out, filt_freq, hyena_bias, pool_w,
                    pool_scale, w_out, b_out, ln_g, ln_b, n2len=MINOR_LEN)
```

```python
import functools
import math

import jax
import jax.numpy as jnp
from jax import lax
from jax.experimental import pallas as pl
from jax.experimental.pallas import tpu as pltpu

F32 = jnp.float32
BF16 = jnp.bfloat16
HIGHEST = lax.Precision.HIGHEST

POOL_WINDOWS = (2, 4, 8, 16)
LN_EPS = 1e-5
DECAY_TARGET = 1e-2
FAST_DECAY_PCT = 0.3
SLOW_DECAY_PCT = 1.5

LANES = 128
SUBLANES = 8
HALO = 16
PADS = 8
MINOR_LEN = 128
PHASE_STEPS = 8
VMEM_LIMIT = 56 * 1024 * 1024


def _silu(x):
    return x * jax.nn.sigmoid(x)


def _dft_tables(seq, n2len):
    n1h = seq // n2len
    n_fft = 2 * seq
    n2i = jnp.arange(n2len, dtype=jnp.int32)
    k1 = jnp.arange(n1h, dtype=jnp.int32)
    odd = 2 * k1 + 1
    alpha = ((odd[:, None] * k1[None, :]) % (4 * n1h)).astype(F32) * (2.0 * math.pi / (4 * n1h))
    n2e = jnp.arange(n2len + 1, dtype=jnp.int32)
    beta = ((n2e[:, None] * odd[None, :]) % (2 * n_fft)).astype(F32) * (2.0 * math.pi / (2 * n_fft))
    ca, sa = jnp.cos(alpha)[None], jnp.sin(alpha)[None]
    cb, sb = jnp.cos(beta)[:, :, None], jnp.sin(beta)[:, :, None]
    cos_t = ca * cb[:-1] - sa * sb[:-1]
    sin_t = sa * cb[:-1] + ca * sb[:-1]
    g_fwd = jnp.concatenate([cos_t, -sin_t], axis=1).astype(BF16)
    g_inv = (2.0 / n_fft) * jnp.concatenate(
        [jnp.swapaxes(cos_t, 1, 2), -jnp.swapaxes(sin_t, 1, 2)], axis=2)
    g_inv = g_inv.astype(BF16)
    cos_r = ca * cb[1:] - sa * sb[1:]
    sin_r = sa * cb[1:] + ca * sb[1:]
    g_rev = jnp.concatenate([cos_r, sin_r], axis=1).astype(BF16)

    m = (n2i[:, None] * n2i[None, :]) % n2len
    phi = m.astype(F32) * (2.0 * math.pi / n2len)
    cm, sm = jnp.cos(phi), jnp.sin(phi)
    h2 = jnp.block([[cm, sm], [-sm, cm]]).astype(BF16)
    h2i = jnp.block([[cm, -sm], [sm, cm]]).astype(BF16)
    return g_fwd, g_inv, g_rev, h2, h2i


def _filter_pos_features(seq, emb):
    t = jnp.linspace(0.0, 1.0, seq, dtype=F32)[:, None]
    bands = (emb - 1) // 2
    w = (2.0 * math.pi / seq) * jnp.arange(seq, dtype=F32)[:, None]
    f = jnp.linspace(1e-4, bands - 1, bands, dtype=F32)[None, :]
    return jnp.concatenate([t, jnp.cos(f * w), -jnp.sin(f * w)], axis=-1)


def _mod_kernel(c_ref, w_ref, b_ref, o_ref):
    s = _silu(c_ref[...])
    o_ref[...] = jnp.dot(s, w_ref[...], precision=HIGHEST,
                         preferred_element_type=F32) + b_ref[...]


def _modulation(c, w_ada, b_ada):
    bsz, d = c.shape
    n_out = w_ada.shape[1]
    return pl.pallas_call(
        _mod_kernel,
        grid=(n_out // d,),
        in_specs=[pl.BlockSpec((bsz, d), lambda j: (0, 0)),
                  pl.BlockSpec((d, d), lambda j: (0, j)),
                  pl.BlockSpec((1, d), lambda j: (0, j))],
        out_specs=pl.BlockSpec((bsz, d), lambda j: (0, j)),
        out_shape=jax.ShapeDtypeStruct((bsz, n_out), F32),
        name="adaln_mod",
    )(c, w_ada, b_ada.reshape(1, n_out))


def _store_permuted(o_ref, val, n2len, first=0):
    for cb in range(val.shape[1] // LANES):
        for j in range(val.shape[0] // n2len):
            o_ref[cb, pl.ds(first + j, n2len, stride=SUBLANES), :] = val[
                j * n2len:(j + 1) * n2len, cb * LANES:(cb + 1) * LANES]


def _load_minor(x_ref, j):
    blk = x_ref[:, j * SUBLANES:(j + 1) * SUBLANES, :]
    return blk.reshape(blk.shape[0] * SUBLANES, blk.shape[2])


def _proj_kernel(xp_ref, xc_ref, xn_ref, mod_ref, w_ref, b_ref, cw_ref, cb_ref, pw_ref,
                 ps_ref, v_ref, x1_ref, x2_ref, g_ref, yp_ref, p_scr, q_scr, *, tile, n2len,
                 ch, seq):
    i = pl.program_id(1)
    nt = pl.num_programs(1)
    shift = mod_ref[0:1, :]
    scale1 = 1.0 + mod_ref[1:2, :]
    xe = jnp.concatenate([xp_ref[...], xc_ref[...], xn_ref[...]], axis=0)
    ue = (xe * scale1 + shift).astype(BF16)
    uc = ue[HALO:HALO + tile, :]
    main = PADS * SUBLANES
    step = SUBLANES

    def project_permuted(cols, buf):
        p = jnp.dot(ue, w_ref[:, cols], preferred_element_type=F32) + b_ref[:, cols]
        p = jnp.concatenate([jnp.where(i > 0, p[:HALO], 0.0), p[HALO:HALO + tile],
                             jnp.where(i < nt - 1, p[HALO + tile:], 0.0)], axis=0)
        for cb in range(p.shape[1] // LANES):
            for j in range(SUBLANES):
                lo = HALO - PADS + j * n2len
                p_scr[buf, cb, pl.ds(j, n2len + 2 * PADS, stride=SUBLANES), :] = p[
                    lo:lo + n2len + 2 * PADS, cb * LANES:(cb + 1) * LANES]

    def shifted(buf, cb, d):
        return p_scr[buf, cb, main + d * step:main + d * step + tile, :]

    for k, o_ref in enumerate((v_ref, x1_ref, x2_ref)):
        buf = k % 2
        project_permuted(slice(k * ch, (k + 1) * ch), buf)
        for cb in range(ch // LANES):
            c = slice(k * ch + cb * LANES, k * ch + (cb + 1) * LANES)
            o_ref[cb] = (cb_ref[:, c] + cw_ref[0:1, c] * shifted(buf, cb, -1)
                         + cw_ref[1:2, c] * shifted(buf, cb, 0)
                         + cw_ref[2:3, c] * shifted(buf, cb, 1))

    hg = jnp.dot(uc, w_ref[:, 3 * ch:4 * ch], preferred_element_type=F32) + b_ref[:, 3 * ch:4 * ch]
    _store_permuted(g_ref, _silu(hg), n2len)

    dp = ps_ref.shape[1]
    c0 = 4 * ch
    pin = jnp.dot(ue, w_ref[:, c0:c0 + dp], preferred_element_type=F32) + b_ref[:, c0:c0 + dp]
    row = lax.broadcasted_iota(jnp.int32, (tile + 2 * HALO, 1), 0)
    valid = ((row >= HALO) | (i > 0)) & ((row < tile + HALO) | (i < nt - 1))
    q_scr[...] = jnp.where(valid, pin, 0.0)
    pgate = jnp.dot(uc, w_ref[:, c0 + dp:c0 + 2 * dp], preferred_element_type=F32) + b_ref[:, c0 + dp:c0 + 2 * dp]
    pos = i * tile + lax.broadcasted_iota(jnp.int32, (tile, 1), 0)
    pg = dp // len(POOL_WINDOWS)
    groups = []
    for g, win in enumerate(POOL_WINDOWS):
        lanes = slice(g * pg, (g + 1) * pg)
        half = win // 2
        acc = q_scr[HALO - half:HALO - half + tile, lanes]
        for d in range(-half + 1, half):
            acc = acc + q_scr[HALO + d:HALO + d + tile, lanes]
        cnt = (jnp.minimum(pos + half, seq) - jnp.maximum(pos - half, 0)).astype(F32)
        diff = acc / cnt - q_scr[HALO:HALO + tile, lanes]
        groups.append(jnp.dot(diff.astype(BF16), pw_ref[g], preferred_element_type=F32))
    yp = jnp.concatenate(groups, axis=1) * ps_ref[...] * _silu(pgate)
    yp_ref[...] = yp.astype(yp_ref.dtype)


def _input_projection(x, mod3, w_in, b_in, conv_w, conv_b, pool_w, pool_scale, n2len):
    bsz, seq, d = x.shape
    ch = conv_w.shape[1] // 3
    dp = pool_scale.shape[0]
    n1h = seq // n2len
    tile = SUBLANES * n2len
    nt = seq // tile
    hb = tile // HALO
    n_proj = w_in.shape[1]
    cbn = ch // LANES
    assert max(POOL_WINDOWS) // 2 <= HALO and PADS <= HALO
    perm = jax.ShapeDtypeStruct((bsz, n1h // SUBLANES, cbn, tile, LANES), F32)
    perm_spec = pl.BlockSpec((None, None, cbn, tile, LANES), lambda b, i: (b, i, 0, 0, 0))
    full = lambda shape: pl.BlockSpec(shape, lambda b, i: (0,) * len(shape))
    kern = functools.partial(_proj_kernel, tile=tile, n2len=n2len, ch=ch, seq=seq)
    return pl.pallas_call(
        kern,
        grid=(bsz, nt),
        in_specs=[
            pl.BlockSpec((None, HALO, d), lambda b, i: (b, jnp.maximum(i * hb - 1, 0), 0)),
            pl.BlockSpec((None, tile, d), lambda b, i: (b, i, 0)),
            pl.BlockSpec((None, HALO, d), lambda b, i: (b, jnp.minimum((i + 1) * hb, seq // HALO - 1), 0)),
            pl.BlockSpec((None, 3, d), lambda b, i: (b, 0, 0)),
            full((d, n_proj)), full((1, n_proj)), full((3, 3 * ch)), full((1, 3 * ch)),
            full(pool_w.shape), full((1, dp)),
        ],
        out_specs=[perm_spec, perm_spec, perm_spec, perm_spec,
                   pl.BlockSpec((None, tile, dp), lambda b, i: (b, i, 0))],
        out_shape=[perm, perm, perm, perm, jax.ShapeDtypeStruct((bsz, seq, dp), BF16)],
        scratch_shapes=[pltpu.VMEM((2, cbn, (n2len + 2 * PADS) * SUBLANES, LANES), F32),
                        pltpu.VMEM((tile + 2 * HALO, dp), F32)],
        compiler_params=pltpu.CompilerParams(
            dimension_semantics=("parallel", "arbitrary"), vmem_limit_bytes=VMEM_LIMIT),
        name="in_proj_conv_pool",
    )(x, x, x, mod3, w_in.astype(BF16), b_in.reshape(1, n_proj), conv_w,
      conv_b.reshape(1, 3 * ch), pool_w.astype(BF16), pool_scale.reshape(1, dp))


def _filter_mlp_kernel(z_ref, w1_ref, b1_ref, wi_ref, bi_ref, fr_ref, wo_ref, ad_ref,
                       of_ref, or_ref, *, n2len):
    z = z_ref[...]
    fr = fr_ref[...]
    h = jnp.sin(fr * (jnp.dot(z.astype(BF16), w1_ref[...],
                              preferred_element_type=F32) + b1_ref[...]))
    for l in range(wi_ref.shape[0]):
        h = jnp.sin(fr * (jnp.dot(h.astype(BF16), wi_ref[l],
                                  preferred_element_type=F32) + bi_ref[l:l + 1, :]))
    hb = h.astype(BF16)
    half_lanes = LANES // 2
    n1_half = z.shape[0] // n2len
    for half in range(2):
        t = z[:, half * half_lanes:half * half_lanes + 1]
        decay = jnp.exp(-t * ad_ref[...])
        for d, o_ref in enumerate((of_ref, or_ref)):
            k = jnp.dot(hb, wo_ref[d, half], preferred_element_type=F32)
            _store_permuted(o_ref, k * decay, n2len, first=half * n1_half)


def _block_diag2(w):
    zero = jnp.zeros_like(w)
    return jnp.concatenate([jnp.concatenate([w, zero], axis=-1),
                            jnp.concatenate([zero, w], axis=-1)], axis=-2)


def _filter_taps(seq, n2len, w1, b1, w_inner, b_inner, w_out, freq, n_ch, order):
    emb, hid = w1.shape
    n_inner = w_inner.shape[0]
    oc = order * n_ch
    hl = LANES // 2
    assert emb <= hl and hid <= hl
    tile = SUBLANES * n2len
    half = tile // 2
    groups = seq // tile
    z = jnp.pad(_filter_pos_features(seq, emb), ((0, 0), (0, hl - emb)))
    zp = jnp.transpose(z.reshape(groups, 2, half, hl), (0, 2, 1, 3)).reshape(groups * half, LANES)
    ph = hl - hid
    w1d = _block_diag2(jnp.pad(w1, ((0, hl - emb), (0, ph))))
    b1d = jnp.tile(jnp.pad(b1, (0, ph)), 2).reshape(1, LANES)
    wid = _block_diag2(jnp.pad(w_inner, ((0, 0), (0, ph), (0, ph))))
    bid = jnp.tile(jnp.pad(b_inner, ((0, 0), (0, ph))), (1, 2))
    frd = jnp.tile(jnp.pad(freq, (0, ph), constant_values=1.0), 2).reshape(1, LANES)
    wo = jnp.transpose(w_out.reshape(hid, order, 2, n_ch), (2, 0, 1, 3)).reshape(2, hid, oc)
    wo = jnp.pad(wo, ((0, 0), (0, ph), (0, 0)))
    zero = jnp.zeros_like(wo)
    wo4 = jnp.stack([jnp.concatenate([wo, zero], axis=1),
                     jnp.concatenate([zero, wo], axis=1)], axis=1)
    min_decay = math.log(DECAY_TARGET) / SLOW_DECAY_PCT
    max_decay = math.log(DECAY_TARGET) / FAST_DECAY_PCT
    absdelta = jnp.abs(jnp.linspace(min_decay, max_decay, n_ch, dtype=F32))
    absdelta = jnp.tile(absdelta, order).reshape(1, oc)

    full = lambda shape: pl.BlockSpec(shape, lambda i: (0,) * len(shape))
    kern = functools.partial(_filter_mlp_kernel, n2len=n2len)
    taps = jax.ShapeDtypeStruct((groups, oc // LANES, tile, LANES), F32)
    taps_spec = pl.BlockSpec((None, oc // LANES, tile, LANES), lambda i: (i, 0, 0, 0))
    return pl.pallas_call(
        kern,
        grid=(groups,),
        in_specs=[
            pl.BlockSpec((half, LANES), lambda i: (i, 0)),
            full((LANES, LANES)), full((1, LANES)), full((n_inner, LANES, LANES)),
            full((n_inner, LANES)), full((1, LANES)), full((2, 2, LANES, oc)), full((1, oc)),
        ],
        out_specs=[taps_spec, taps_spec],
        out_shape=[taps, taps],
        compiler_params=pltpu.CompilerParams(
            dimension_semantics=("arbitrary",), vmem_limit_bytes=VMEM_LIMIT),
        name="filter_mlp",
    )(zp, w1d.astype(BF16), b1d, wid.astype(BF16), bid, frd, wo4.astype(BF16), absdelta)


def _stage1(a_scr, g_ref, x_ref, base, cn2, n_rows, pitch):
    for j in range(cn2):
        r = jnp.dot(g_ref[base + j], _load_minor(x_ref, j).astype(BF16),
                    preferred_element_type=F32)
        a_scr[pl.ds(base + j, n_rows, stride=pitch), :] = r


def _resident(shape):
    return pl.BlockSpec(shape, lambda *_: (0,) * len(shape), pipeline_mode=pl.Buffered(1))


def _spectrum_views(a_scr, chunk, ck1, n1h, pitch):
    re0 = pl.multiple_of(chunk * (ck1 * pitch), SUBLANES)
    im0 = pl.multiple_of(chunk * (ck1 * pitch) + n1h * pitch, SUBLANES)
    return a_scr.at[pl.ds(re0, ck1 * pitch)], a_scr.at[pl.ds(im0, ck1 * pitch)]


def _spectrum_pair(a_re, a_im, j, n2len, pitch):
    def one(jj):
        return jnp.concatenate([a_re[jj * pitch:jj * pitch + n2len, :],
                                a_im[jj * pitch:jj * pitch + n2len, :]], axis=0)
    return jnp.concatenate([one(j), one(j + 1)], axis=1).astype(BF16)


def _filter_fft_kernel(hf_ref, hr_ref, g_ref, gr_ref, h2_ref, o_ref, a_scr,
                       *, ns, cn2, ck1, n1h, n2len, pitch):
    s = pl.program_id(1)

    @pl.when(s < ns)
    def _():
        for j in range(cn2):
            jr = cn2 - 1 - j
            m = s * cn2 + j
            r = jnp.dot(g_ref[m], _load_minor(hf_ref, j).astype(BF16),
                        preferred_element_type=F32)
            r = r + jnp.dot(gr_ref[n2len - 1 - m], _load_minor(hr_ref, jr).astype(BF16),
                            preferred_element_type=F32)
            a_scr[pl.ds(m, 2 * n1h, stride=pitch), :] = r

    @pl.when(s >= ns)
    def _():
        a_re, a_im = _spectrum_views(a_scr, s - ns, ck1, n1h, pitch)
        for j in range(0, ck1, 2):
            rows = _spectrum_pair(a_re, a_im, j, n2len, pitch)
            u = jnp.dot(h2_ref[...], rows, preferred_element_type=F32).astype(o_ref.dtype)
            o_ref[j] = u[:, :LANES]
            o_ref[j + 1] = u[:, LANES:]


def _filter_spectrum(taps_f, taps_r, g_fwd, g_rev, h2, n2len):
    groups, ocb, _, _ = taps_f.shape
    oc = ocb * LANES
    n1h = groups * SUBLANES
    ns = PHASE_STEPS
    cn2, ck1 = n2len // ns, n1h // ns
    pitch = n2len + SUBLANES
    kern = functools.partial(_filter_fft_kernel, ns=ns, cn2=cn2, ck1=ck1, n1h=n1h,
                             n2len=n2len, pitch=pitch)
    return pl.pallas_call(
        kern,
        grid=(oc // LANES, 2 * ns),
        in_specs=[
            pl.BlockSpec((groups, None, cn2 * SUBLANES, LANES),
                         lambda c, s: (0, c, jnp.minimum(s, ns - 1), 0)),
            pl.BlockSpec((groups, None, cn2 * SUBLANES, LANES),
                         lambda c, s: (0, c, jnp.maximum(ns - 1 - s, 0), 0)),
            _resident(g_fwd.shape), _resident(g_rev.shape), _resident(h2.shape),
        ],
        out_specs=pl.BlockSpec((ck1, 2 * n2len, LANES), lambda c, s: (jnp.maximum(s - ns, 0), 0, c)),
        out_shape=jax.ShapeDtypeStruct((n1h, 2 * n2len, oc), BF16),
        scratch_shapes=[pltpu.VMEM((2 * n1h * pitch, LANES), F32)],
        compiler_params=pltpu.CompilerParams(
            dimension_semantics=("parallel", "arbitrary"), vmem_limit_bytes=VMEM_LIMIT),
        name="filter_spectrum",
    )(taps_f, taps_r, g_fwd, g_rev, h2)


def _long_conv_kernel(*refs, ns, cn2, ck1, n1h, n2len, pitch, has_gate):
    if has_gate:
        (vf_ref, g_ref, kf_ref, h2_ref, h2i_ref, gi_ref, vb_ref, xm_ref, gate_ref, hb_ref,
         o_ref, a_scr) = refs
    else:
        (vf_ref, g_ref, kf_ref, h2_ref, h2i_ref, gi_ref, vb_ref, xm_ref, hb_ref,
         o_ref, a_scr) = refs
        gate_ref = None
    s = pl.program_id(2)

    @pl.when(s < ns)
    def _():
        _stage1(a_scr, g_ref, vf_ref, s * cn2, cn2, 2 * n1h, pitch)

    @pl.when((s >= ns) & (s < 2 * ns))
    def _():
        a_re, a_im = _spectrum_views(a_scr, s - ns, ck1, n1h, pitch)
        for j in range(0, ck1, 2):
            rows = _spectrum_pair(a_re, a_im, j, n2len, pitch)
            u = jnp.dot(h2_ref[...], rows, preferred_element_type=F32)
            ur, ui = u[:n2len], u[n2len:]
            kf = jnp.concatenate([kf_ref[j], kf_ref[j + 1]], axis=1).astype(F32)
            kr, ki = kf[:n2len], kf[n2len:]
            y = jnp.concatenate([ur * kr - ui * ki, ur * ki + ui * kr], axis=0)
            bb = jnp.dot(h2i_ref[...], y.astype(BF16), preferred_element_type=F32)
            for jj in range(2):
                lanes = slice(jj * LANES, (jj + 1) * LANES)
                a_re[(j + jj) * pitch:(j + jj) * pitch + n2len, :] = bb[:n2len, lanes]
                a_im[(j + jj) * pitch:(j + jj) * pitch + n2len, :] = bb[n2len:, lanes]

    @pl.when(s >= 2 * ns)
    def _():
        base = (s - 2 * ns) * cn2
        for j in range(cn2):
            rows = a_scr[pl.ds(base + j, 2 * n1h, stride=pitch), :]
            y = jnp.dot(gi_ref[base + j], rows.astype(BF16), preferred_element_type=F32)
            z = _load_minor(xm_ref, j) * (y + hb_ref[...] * _load_minor(vb_ref, j))
            if gate_ref is not None:
                z = z * _load_minor(gate_ref, j)
            o_ref[:, j * SUBLANES:(j + 1) * SUBLANES, :] = z.reshape(
                n1h // SUBLANES, SUBLANES, z.shape[1])


def _long_conv(u_perm, mult_perm, gate_perm, kf, hbias, order_idx, tables):
    g_fwd, g_inv, _, h2, h2i = tables
    bsz, groups, cbn, tile, _ = u_perm.shape
    n2len = tile // SUBLANES
    n1h = groups * SUBLANES
    ns = PHASE_STEPS
    cn2, ck1 = n2len // ns, n1h // ns
    pitch = n2len + SUBLANES
    has_gate = gate_perm is not None

    def late(s):
        return jnp.clip(s - 2 * ns, 0, ns - 1)

    data_early = pl.BlockSpec((None, groups, None, cn2 * SUBLANES, LANES),
                              lambda c, b, s: (b, 0, c, jnp.minimum(s, ns - 1), 0))
    data_late = pl.BlockSpec((None, groups, None, cn2 * SUBLANES, LANES),
                             lambda c, b, s: (b, 0, c, late(s), 0))
    in_specs = [
        data_early,
        _resident(g_fwd.shape),
        pl.BlockSpec((ck1, 2 * n2len, LANES),
                     lambda c, b, s: (jnp.clip(s - ns, 0, ns - 1), 0, order_idx * cbn + c)),
        _resident(h2.shape), _resident(h2i.shape), _resident(g_inv.shape),
        data_late, data_late,
    ]
    args = [u_perm, g_fwd, kf, h2, h2i, g_inv, u_perm, mult_perm]
    if has_gate:
        in_specs.append(data_late)
        args.append(gate_perm)
    in_specs.append(pl.BlockSpec((None, 1, LANES), lambda c, b, s: (order_idx, 0, c)))
    args.append(hbias)
    kern = functools.partial(_long_conv_kernel, ns=ns, cn2=cn2, ck1=ck1, n1h=n1h,
                             n2len=n2len, pitch=pitch, has_gate=has_gate)
    return pl.pallas_call(
        kern,
        grid=(cbn, bsz, 3 * ns),
        in_specs=in_specs,
        out_specs=data_late,
        out_shape=jax.ShapeDtypeStruct(u_perm.shape, F32),
        scratch_shapes=[pltpu.VMEM((2 * n1h * pitch, LANES), F32)],
        compiler_params=pltpu.CompilerParams(
            dimension_semantics=("parallel", "parallel", "arbitrary"),
            vmem_limit_bytes=VMEM_LIMIT),
        name="long_conv",
    )(*args)


def _out_kernel(x_ref, yh_ref, yp_ref, mod_ref, w_ref, b_ref, g_ref, beta_ref, o_ref,
                *, n2len, alpha):
    cbn = yh_ref.shape[0]
    ch = cbn * LANES
    yh = jnp.concatenate(
        [jnp.concatenate([yh_ref[cb, pl.ds(j, n2len, stride=SUBLANES), :]
                          for cb in range(cbn)], axis=1)
         for j in range(SUBLANES)], axis=0)
    acc = jnp.dot(yh.astype(BF16), w_ref[0:ch, :], preferred_element_type=F32)
    acc = acc + jnp.dot(yp_ref[...], w_ref[ch:, :], preferred_element_type=F32)
    h = alpha * x_ref[...] + mod_ref[2:3, :] * (acc + b_ref[...])
    mu = jnp.mean(h, axis=-1, keepdims=True)
    hc = h - mu
    var = jnp.mean(hc * hc, axis=-1, keepdims=True)
    o_ref[...] = hc * lax.rsqrt(var + LN_EPS) * g_ref[...] + beta_ref[...]


def _output_projection(x, yh_perm, yp, mod3, w_out, b_out, ln_g, ln_b, alpha):
    bsz, seq, d = x.shape
    _, _, cbn, tile, _ = yh_perm.shape
    ch = cbn * LANES
    n2len = tile // SUBLANES
    dp = yp.shape[2]
    full = lambda shape: pl.BlockSpec(shape, lambda b, i: (0,) * len(shape))
    kern = functools.partial(_out_kernel, n2len=n2len, alpha=alpha)
    return pl.pallas_call(
        kern,
        grid=(bsz, seq // tile),
        in_specs=[
            pl.BlockSpec((None, tile, d), lambda b, i: (b, i, 0)),
            pl.BlockSpec((None, None, cbn, tile, LANES), lambda b, i: (b, i, 0, 0, 0)),
            pl.BlockSpec((None, tile, dp), lambda b, i: (b, i, 0)),
            pl.BlockSpec((None, 3, d), lambda b, i: (b, 0, 0)),
            full((ch + dp, d)), full((1, d)), full((1, d)), full((1, d)),
        ],
        out_specs=pl.BlockSpec((None, tile, d), lambda b, i: (b, i, 0)),
        out_shape=jax.ShapeDtypeStruct(x.shape, x.dtype),
        compiler_params=pltpu.CompilerParams(
            dimension_semantics=("parallel", "arbitrary"), vmem_limit_bytes=VMEM_LIMIT),
        name="out_proj_deepnorm",
    )(x, yh_perm, yp, mod3, w_out.astype(BF16), b_out.reshape(1, d), ln_g.reshape(1, d),
      ln_b.reshape(1, d))


def _forward(x, c, w_ada, b_ada, w_in, b_in, conv_w, conv_b, filt_w1, filt_b1, filt_w_inner,
             filt_b_inner, filt_w_out, filt_freq, hyena_bias, pool_w, pool_scale, w_out,
             b_out, ln_g, ln_b, *, n2len):
    bsz, seq, d = x.shape
    depth = w_ada.shape[0]
    order, n_ch = hyena_bias.shape[1], hyena_bias.shape[2]
    alpha = (2.0 * depth) ** 0.25
    tables = _dft_tables(seq, n2len)
    h = x
    for layer in range(depth):
        mod3 = _modulation(c, w_ada[layer], b_ada[layer]).reshape(bsz, 3, d)
        v, x1, x2, gate, yp = _input_projection(
            h, mod3, w_in[layer], b_in[layer], conv_w[layer], conv_b[layer], pool_w[layer],
            pool_scale[layer], n2len)
        taps_f, taps_r = _filter_taps(
            seq, n2len, filt_w1[layer], filt_b1[layer], filt_w_inner[layer],
            filt_b_inner[layer], filt_w_out[layer], filt_freq[layer], n_ch, order)
        kf = _filter_spectrum(taps_f, taps_r, tables[0], tables[2], tables[3], n2len)
        hbias = hyena_bias[layer].reshape(order, 1, n_ch)
        z = v
        mults = (x1, x2)
        for o in range(order):
            z = _long_conv(z, mults[o], gate if o == order - 1 else None, kf, hbias, o, tables)
        h = _output_projection(h, z, yp, mod3, w_out[layer], b_out[layer], ln_g[layer],
                               ln_b[layer], alpha)
    return h


def kernel(x, c, w_ada, b_ada, w_in, b_in, conv_w, conv_b, filt_w1, filt_b1, filt_w_inner,
           filt_b_inner, filt_w_out, filt_freq, hyena_bias, pool_w, pool_scale, w_out, b_out,
           ln_g, ln_b):
    return _forward(x, c, w_ada, b_ada, w_in, b_in, conv_w, conv_b, filt_w1, filt_b1,
                    filt_w_inner, filt_b_inner, filt_w_out, filt_freq, hyena_bias, pool_w,
                    pool_scale, w_out, b_out, ln_g, ln_b, n2len=MINOR_LEN)
```

```python
import functools
import math

import jax
import jax.numpy as jnp
from jax import lax
from jax.experimental import pallas as pl
from jax.experimental.pallas import tpu as pltpu

F32 = jnp.float32
BF16 = jnp.bfloat16
HIGHEST = lax.Precision.HIGHEST

POOL_WINDOWS = (2, 4, 8, 16)
LN_EPS = 1e-5
DECAY_TARGET = 1e-2
FAST_DECAY_PCT = 0.3
SLOW_DECAY_PCT = 1.5

LANES = 128
SUBLANES = 8
HALO = 16
MINOR_LEN = 128
PHASE_STEPS = 8
CONV_STEPS = (4, 8, 8)
VMEM_LIMIT = 56 * 1024 * 1024


def _silu(x):
    return x * jax.nn.sigmoid(x)


def _dft_tables(seq, n2len):
    n1h = seq // n2len
    n_fft = 2 * seq
    n2i = jnp.arange(n2len, dtype=jnp.int32)
    k1 = jnp.arange(n1h, dtype=jnp.int32)
    odd = 2 * k1 + 1
    alpha = ((odd[:, None] * k1[None, :]) % (4 * n1h)).astype(F32) * (2.0 * math.pi / (4 * n1h))
    n2e = jnp.arange(n2len + 1, dtype=jnp.int32)
    beta = ((n2e[:, None] * odd[None, :]) % (2 * n_fft)).astype(F32) * (2.0 * math.pi / (2 * n_fft))
    ca, sa = jnp.cos(alpha)[None], jnp.sin(alpha)[None]
    cb, sb = jnp.cos(beta)[:, :, None], jnp.sin(beta)[:, :, None]
    cos_t = ca * cb[:-1] - sa * sb[:-1]
    sin_t = sa * cb[:-1] + ca * sb[:-1]
    g_fwd = jnp.concatenate([cos_t, -sin_t], axis=1).astype(BF16)
    g_inv = (2.0 / n_fft) * jnp.concatenate(
        [jnp.swapaxes(cos_t, 1, 2), -jnp.swapaxes(sin_t, 1, 2)], axis=2)
    g_inv = g_inv.astype(BF16)
    cos_r = ca * cb[1:] - sa * sb[1:]
    sin_r = sa * cb[1:] + ca * sb[1:]
    g_rev = jnp.concatenate([cos_r, sin_r], axis=1).astype(BF16)

    m = (n2i[:, None] * n2i[None, :]) % n2len
    phi = m.astype(F32) * (2.0 * math.pi / n2len)
    cm, sm = jnp.cos(phi), jnp.sin(phi)
    h2 = jnp.block([[cm, sm], [-sm, cm]]).astype(BF16)
    h2i = jnp.block([[cm, -sm], [sm, cm]]).astype(BF16)
    return g_fwd, g_inv, g_rev, h2, h2i


def _filter_pos_features(seq, emb):
    t = jnp.linspace(0.0, 1.0, seq, dtype=F32)[:, None]
    bands = (emb - 1) // 2
    w = (2.0 * math.pi / seq) * jnp.arange(seq, dtype=F32)[:, None]
    f = jnp.linspace(1e-4, bands - 1, bands, dtype=F32)[None, :]
    return jnp.concatenate([t, jnp.cos(f * w), -jnp.sin(f * w)], axis=-1)


def _mod_kernel(c_ref, w_ref, b_ref, o_ref):
    s = _silu(c_ref[...])
    o_ref[...] = jnp.dot(s, w_ref[...], precision=HIGHEST,
                         preferred_element_type=F32) + b_ref[...]


def _modulation(c, w_ada, b_ada):
    bsz, d = c.shape
    n_out = w_ada.shape[1]
    return pl.pallas_call(
        _mod_kernel,
        grid=(n_out // d,),
        in_specs=[pl.BlockSpec((bsz, d), lambda j: (0, 0)),
                  pl.BlockSpec((d, d), lambda j: (0, j)),
                  pl.BlockSpec((1, d), lambda j: (0, j))],
        out_specs=pl.BlockSpec((bsz, d), lambda j: (0, j)),
        out_shape=jax.ShapeDtypeStruct((bsz, n_out), F32),
        name="adaln_mod",
    )(c, w_ada, b_ada.reshape(1, n_out))


def _store_permuted(o_ref, val, n2len, first=0):
    for cb in range(val.shape[1] // LANES):
        for j in range(val.shape[0] // n2len):
            o_ref[cb, pl.ds(first + j, n2len, stride=SUBLANES), :] = val[
                j * n2len:(j + 1) * n2len, cb * LANES:(cb + 1) * LANES]


def _load_minor(x_ref, j):
    blk = x_ref[:, j * SUBLANES:(j + 1) * SUBLANES, :]
    return blk.reshape(blk.shape[0] * SUBLANES, blk.shape[2])


def _proj_kernel(xp_ref, xc_ref, xn_ref, mod_ref, w_ref, b_ref, cw_ref, cb_ref, pw_ref,
                 ps_ref, v_ref, x1_ref, x2_ref, yp_ref, p_scr, *, tile, n2len, ch, seq):
    i = pl.program_id(1)
    nt = pl.num_programs(1)
    shift = mod_ref[0:1, :]
    scale1 = 1.0 + mod_ref[1:2, :]
    xe = jnp.concatenate([xp_ref[...], xc_ref[...], xn_ref[...]], axis=0)
    ue = (xe * scale1 + shift).astype(BF16)
    uc = ue[HALO:HALO + tile, :]
    row = lax.broadcasted_iota(jnp.int32, (tile + 2 * HALO, 1), 0)
    valid = ((row >= HALO) | (i > 0)) & ((row < tile + HALO) | (i < nt - 1))

    hg = jnp.dot(uc, w_ref[:, 3 * ch:4 * ch], preferred_element_type=F32) + b_ref[:, 3 * ch:4 * ch]
    gate = _silu(hg)

    for k, o_ref in enumerate((v_ref, x1_ref, x2_ref)):
        cols = slice(k * ch, (k + 1) * ch)
        p = jnp.dot(ue, w_ref[:, cols], preferred_element_type=F32) + b_ref[:, cols]
        p_scr[...] = jnp.where(valid, p, 0.0)
        s = (cb_ref[:, cols]
             + cw_ref[0:1, cols] * p_scr[HALO - 1:HALO - 1 + tile, :]
             + cw_ref[1:2, cols] * p_scr[HALO:HALO + tile, :]
             + cw_ref[2:3, cols] * p_scr[HALO + 1:HALO + 1 + tile, :])
        if k == 2:
            s = s * gate
        _store_permuted(o_ref, s, n2len)

    dp = ps_ref.shape[1]
    c0 = 4 * ch
    q_scr = p_scr
    pin = jnp.dot(ue, w_ref[:, c0:c0 + dp], preferred_element_type=F32) + b_ref[:, c0:c0 + dp]
    q_scr[...] = jnp.where(valid, pin, 0.0)
    pgate = jnp.dot(uc, w_ref[:, c0 + dp:c0 + 2 * dp], preferred_element_type=F32) + b_ref[:, c0 + dp:c0 + 2 * dp]
    pos = i * tile + lax.broadcasted_iota(jnp.int32, (tile, 1), 0)
    pg = dp // len(POOL_WINDOWS)
    groups = []
    for g, win in enumerate(POOL_WINDOWS):
        lanes = slice(g * pg, (g + 1) * pg)
        half = win // 2
        acc = q_scr[HALO - half:HALO - half + tile, lanes]
        for d in range(-half + 1, half):
            acc = acc + q_scr[HALO + d:HALO + d + tile, lanes]
        cnt = (jnp.minimum(pos + half, seq) - jnp.maximum(pos - half, 0)).astype(F32)
        diff = acc / cnt - q_scr[HALO:HALO + tile, lanes]
        groups.append(jnp.dot(diff.astype(BF16), pw_ref[g], preferred_element_type=F32))
    yp = jnp.concatenate(groups, axis=1) * ps_ref[...] * _silu(pgate)
    yp_ref[...] = yp.astype(yp_ref.dtype)


def _input_projection(x, mod3, w_in, b_in, conv_w, conv_b, pool_w, pool_scale, n2len):
    bsz, seq, d = x.shape
    ch = conv_w.shape[1] // 3
    dp = pool_scale.shape[0]
    n1h = seq // n2len
    tile = SUBLANES * n2len
    nt = seq // tile
    hb = tile // HALO
    n_proj = w_in.shape[1]
    cbn = ch // LANES
    assert max(POOL_WINDOWS) // 2 <= HALO and dp == ch
    perm = jax.ShapeDtypeStruct((bsz, n1h // SUBLANES, cbn, tile, LANES), F32)
    perm_spec = pl.BlockSpec((None, None, cbn, tile, LANES), lambda b, i: (b, i, 0, 0, 0))
    full = lambda shape: pl.BlockSpec(shape, lambda b, i: (0,) * len(shape))
    kern = functools.partial(_proj_kernel, tile=tile, n2len=n2len, ch=ch, seq=seq)
    return pl.pallas_call(
        kern,
        grid=(bsz, nt),
        in_specs=[
            pl.BlockSpec((None, HALO, d), lambda b, i: (b, jnp.maximum(i * hb - 1, 0), 0)),
            pl.BlockSpec((None, tile, d), lambda b, i: (b, i, 0)),
            pl.BlockSpec((None, HALO, d), lambda b, i: (b, jnp.minimum((i + 1) * hb, seq // HALO - 1), 0)),
            pl.BlockSpec((None, 3, d), lambda b, i: (b, 0, 0)),
            full((d, n_proj)), full((1, n_proj)), full((3, 3 * ch)), full((1, 3 * ch)),
            full(pool_w.shape), full((1, dp)),
        ],
        out_specs=[perm_spec, perm_spec, perm_spec,
                   pl.BlockSpec((None, tile, dp), lambda b, i: (b, i, 0))],
        out_shape=[perm, perm, perm, jax.ShapeDtypeStruct((bsz, seq, dp), BF16)],
        scratch_shapes=[pltpu.VMEM((tile + 2 * HALO, ch), F32)],
        compiler_params=pltpu.CompilerParams(
            dimension_semantics=("parallel", "arbitrary"), vmem_limit_bytes=VMEM_LIMIT),
        name="in_proj_conv_pool",
    )(x, x, x, mod3, w_in.astype(BF16), b_in.reshape(1, n_proj), conv_w,
      conv_b.reshape(1, 3 * ch), pool_w.astype(BF16), pool_scale.reshape(1, dp))


def _filter_mlp_kernel(z_ref, w1_ref, b1_ref, wi_ref, bi_ref, fr_ref, wo_ref, ad_ref,
                       of_ref, or_ref, *, n2len):
    z = z_ref[...]
    fr = fr_ref[...]
    h = jnp.sin(fr * (jnp.dot(z.astype(BF16), w1_ref[...],
                              preferred_element_type=F32) + b1_ref[...]))
    for l in range(wi_ref.shape[0]):
        h = jnp.sin(fr * (jnp.dot(h.astype(BF16), wi_ref[l],
                                  preferred_element_type=F32) + bi_ref[l:l + 1, :]))
    hb = h.astype(BF16)
    half_lanes = LANES // 2
    n1_half = z.shape[0] // n2len
    for half in range(2):
        t = z[:, half * half_lanes:half * half_lanes + 1]
        decay = jnp.exp(-t * ad_ref[...])
        for d, o_ref in enumerate((of_ref, or_ref)):
            k = jnp.dot(hb, wo_ref[d, half], preferred_element_type=F32)
            _store_permuted(o_ref, k * decay, n2len, first=half * n1_half)


def _block_diag2(w):
    zero = jnp.zeros_like(w)
    return jnp.concatenate([jnp.concatenate([w, zero], axis=-1),
                            jnp.concatenate([zero, w], axis=-1)], axis=-2)


def _filter_taps(seq, n2len, w1, b1, w_inner, b_inner, w_out, freq, n_ch, order):
    emb, hid = w1.shape
    n_inner = w_inner.shape[0]
    oc = order * n_ch
    hl = LANES // 2
    assert emb <= hl and hid <= hl
    tile = SUBLANES * n2len
    half = tile // 2
    groups = seq // tile
    z = jnp.pad(_filter_pos_features(seq, emb), ((0, 0), (0, hl - emb)))
    zp = jnp.transpose(z.reshape(groups, 2, half, hl), (0, 2, 1, 3)).reshape(groups * half, LANES)
    ph = hl - hid
    w1d = _block_diag2(jnp.pad(w1, ((0, hl - emb), (0, ph))))
    b1d = jnp.tile(jnp.pad(b1, (0, ph)), 2).reshape(1, LANES)
    wid = _block_diag2(jnp.pad(w_inner, ((0, 0), (0, ph), (0, ph))))
    bid = jnp.tile(jnp.pad(b_inner, ((0, 0), (0, ph))), (1, 2))
    frd = jnp.tile(jnp.pad(freq, (0, ph), constant_values=1.0), 2).reshape(1, LANES)
    wo = jnp.transpose(w_out.reshape(hid, order, 2, n_ch), (2, 0, 1, 3)).reshape(2, hid, oc)
    wo = jnp.pad(wo, ((0, 0), (0, ph), (0, 0)))
    zero = jnp.zeros_like(wo)
    wo4 = jnp.stack([jnp.concatenate([wo, zero], axis=1),
                     jnp.concatenate([zero, wo], axis=1)], axis=1)
    min_decay = math.log(DECAY_TARGET) / SLOW_DECAY_PCT
    max_decay = math.log(DECAY_TARGET) / FAST_DECAY_PCT
    absdelta = jnp.abs(jnp.linspace(min_decay, max_decay, n_ch, dtype=F32))
    absdelta = jnp.tile(absdelta, order).reshape(1, oc)

    full = lambda shape: pl.BlockSpec(shape, lambda i: (0,) * len(shape))
    kern = functools.partial(_filter_mlp_kernel, n2len=n2len)
    taps = jax.ShapeDtypeStruct((groups, oc // LANES, tile, LANES), F32)
    taps_spec = pl.BlockSpec((None, oc // LANES, tile, LANES), lambda i: (i, 0, 0, 0))
    return pl.pallas_call(
        kern,
        grid=(groups,),
        in_specs=[
            pl.BlockSpec((half, LANES), lambda i: (i, 0)),
            full((LANES, LANES)), full((1, LANES)), full((n_inner, LANES, LANES)),
            full((n_inner, LANES)), full((1, LANES)), full((2, 2, LANES, oc)), full((1, oc)),
        ],
        out_specs=[taps_spec, taps_spec],
        out_shape=[taps, taps],
        compiler_params=pltpu.CompilerParams(
            dimension_semantics=("arbitrary",), vmem_limit_bytes=VMEM_LIMIT),
        name="filter_mlp",
    )(zp, w1d.astype(BF16), b1d, wid.astype(BF16), bid, frd, wo4.astype(BF16), absdelta)


def _resident(shape):
    return pl.BlockSpec(shape, lambda *_: (0,) * len(shape), pipeline_mode=pl.Buffered(1))


def _spectrum_views(a_scr, chunk, ck1, n1h, pitch):
    re0 = pl.multiple_of(chunk * (ck1 * pitch), SUBLANES)
    im0 = pl.multiple_of(chunk * (ck1 * pitch) + n1h * pitch, SUBLANES)
    return a_scr.at[pl.ds(re0, ck1 * pitch)], a_scr.at[pl.ds(im0, ck1 * pitch)]


def _spectrum_pair(a_re, a_im, j, n2len, pitch):
    def one(jj):
        return jnp.concatenate([a_re[jj * pitch:jj * pitch + n2len, :],
                                a_im[jj * pitch:jj * pitch + n2len, :]], axis=0)
    return jnp.concatenate([one(j), one(j + 1)], axis=1).astype(BF16)


def _filter_fft_kernel(hf_ref, hr_ref, g_ref, gr_ref, h2_ref, o_ref, a_scr,
                       *, ns, cn2, ck1, n1h, n2len, pitch):
    s = pl.program_id(1)

    @pl.when(s < ns)
    def _():
        for j in range(cn2):
            jr = cn2 - 1 - j
            m = s * cn2 + j
            r = jnp.dot(g_ref[m], _load_minor(hf_ref, j).astype(BF16),
                        preferred_element_type=F32)
            r = r + jnp.dot(gr_ref[n2len - 1 - m], _load_minor(hr_ref, jr).astype(BF16),
                            preferred_element_type=F32)
            a_scr[pl.ds(m, 2 * n1h, stride=pitch), :] = r

    @pl.when(s >= ns)
    def _():
        a_re, a_im = _spectrum_views(a_scr, s - ns, ck1, n1h, pitch)
        for j in range(0, ck1, 2):
            rows = _spectrum_pair(a_re, a_im, j, n2len, pitch)
            u = jnp.dot(h2_ref[...], rows, preferred_element_type=F32).astype(o_ref.dtype)
            o_ref[j] = u[:, :LANES]
            o_ref[j + 1] = u[:, LANES:]


def _filter_spectrum(taps_f, taps_r, g_fwd, g_rev, h2, n2len):
    groups, ocb, _, _ = taps_f.shape
    oc = ocb * LANES
    n1h = groups * SUBLANES
    ns = PHASE_STEPS
    cn2, ck1 = n2len // ns, n1h // ns
    pitch = n2len + SUBLANES
    kern = functools.partial(_filter_fft_kernel, ns=ns, cn2=cn2, ck1=ck1, n1h=n1h,
                             n2len=n2len, pitch=pitch)
    return pl.pallas_call(
        kern,
        grid=(oc // LANES, 2 * ns),
        in_specs=[
            pl.BlockSpec((groups, None, cn2 * SUBLANES, LANES),
                         lambda c, s: (0, c, jnp.minimum(s, ns - 1), 0)),
            pl.BlockSpec((groups, None, cn2 * SUBLANES, LANES),
                         lambda c, s: (0, c, jnp.maximum(ns - 1 - s, 0), 0)),
            _resident(g_fwd.shape), _resident(g_rev.shape), _resident(h2.shape),
        ],
        out_specs=pl.BlockSpec((ck1, 2 * n2len, LANES), lambda c, s: (jnp.maximum(s - ns, 0), 0, c)),
        out_shape=jax.ShapeDtypeStruct((n1h, 2 * n2len, oc), BF16),
        scratch_shapes=[pltpu.VMEM((2 * n1h * pitch, LANES), F32)],
        compiler_params=pltpu.CompilerParams(
            dimension_semantics=("parallel", "arbitrary"), vmem_limit_bytes=VMEM_LIMIT),
        name="filter_spectrum",
    )(taps_f, taps_r, g_fwd, g_rev, h2)


def _long_conv_kernel(vf_ref, g_ref, kf_ref, h2_ref, h2i_ref, gi_ref, xm_ref, hb_ref, o_ref,
                      a_scr, keep_scr, *, steps, n1h, n2len, pitch):
    ns1, ns2, ns3 = steps
    cn2a, ck1, cn2 = n2len // ns1, n1h // ns2, n2len // ns3
    s = pl.program_id(2)

    @pl.when(s < ns1)
    def _():
        base = s * cn2a
        for j in range(cn2a):
            x = _load_minor(vf_ref, j)
            keep_scr[base + j] = x
            r = jnp.dot(g_ref[base + j], x.astype(BF16), preferred_element_type=F32)
            a_scr[pl.ds(base + j, 2 * n1h, stride=pitch), :] = r

    @pl.when((s >= ns1) & (s < ns1 + ns2))
    def _():
        a_re, a_im = _spectrum_views(a_scr, s - ns1, ck1, n1h, pitch)
        for j in range(0, ck1, 2):
            rows = _spectrum_pair(a_re, a_im, j, n2len, pitch)
            u = jnp.dot(h2_ref[...], rows, preferred_element_type=F32)
            ur, ui = u[:n2len], u[n2len:]
            kf = jnp.concatenate([kf_ref[j], kf_ref[j + 1]], axis=1).astype(F32)
            kr, ki = kf[:n2len], kf[n2len:]
            y = jnp.concatenate([ur * kr - ui * ki, ur * ki + ui * kr], axis=0)
            bb = jnp.dot(h2i_ref[...], y.astype(BF16), preferred_element_type=F32)
            for jj in range(2):
                lanes = slice(jj * LANES, (jj + 1) * LANES)
                a_re[(j + jj) * pitch:(j + jj) * pitch + n2len, :] = bb[:n2len, lanes]
                a_im[(j + jj) * pitch:(j + jj) * pitch + n2len, :] = bb[n2len:, lanes]

    @pl.when(s >= ns1 + ns2)
    def _():
        base = (s - ns1 - ns2) * cn2
        for j in range(cn2):
            rows = a_scr[pl.ds(base + j, 2 * n1h, stride=pitch), :]
            y = jnp.dot(gi_ref[base + j], rows.astype(BF16), preferred_element_type=F32)
            z = _load_minor(xm_ref, j) * (y + hb_ref[...] * keep_scr[base + j])
            o_ref[:, j * SUBLANES:(j + 1) * SUBLANES, :] = z.reshape(
                n1h // SUBLANES, SUBLANES, z.shape[1])


def _long_conv(u_perm, mult_perm, kf, hbias, order_idx, tables):
    g_fwd, g_inv, _, h2, h2i = tables
    bsz, groups, cbn, tile, _ = u_perm.shape
    n2len = tile // SUBLANES
    n1h = groups * SUBLANES
    steps = CONV_STEPS
    ns1, ns2, ns3 = steps
    pitch = n2len + SUBLANES

    def late(s):
        return jnp.clip(s - ns1 - ns2, 0, ns3 - 1)

    data_early = pl.BlockSpec((None, groups, None, (n2len // ns1) * SUBLANES, LANES),
                              lambda c, b, s: (b, 0, c, jnp.minimum(s, ns1 - 1), 0))
    data_late = pl.BlockSpec((None, groups, None, (n2len // ns3) * SUBLANES, LANES),
                             lambda c, b, s: (b, 0, c, late(s), 0))
    in_specs = [
        data_early,
        _resident(g_fwd.shape),
        pl.BlockSpec((n1h // ns2, 2 * n2len, LANES),
                     lambda c, b, s: (jnp.clip(s - ns1, 0, ns2 - 1), 0, order_idx * cbn + c)),
        _resident(h2.shape), _resident(h2i.shape), _resident(g_inv.shape),
        data_late,
        pl.BlockSpec((None, 1, LANES), lambda c, b, s: (order_idx, 0, c)),
    ]
    kern = functools.partial(_long_conv_kernel, steps=steps, n1h=n1h, n2len=n2len,
                             pitch=pitch)
    return pl.pallas_call(
        kern,
        grid=(cbn, bsz, sum(steps)),
        in_specs=in_specs,
        out_specs=data_late,
        out_shape=jax.ShapeDtypeStruct(u_perm.shape, F32),
        scratch_shapes=[pltpu.VMEM((2 * n1h * pitch, LANES), F32),
                        pltpu.VMEM((n2len, n1h, LANES), F32)],
        compiler_params=pltpu.CompilerParams(
            dimension_semantics=("parallel", "parallel", "arbitrary"),
            vmem_limit_bytes=VMEM_LIMIT),
        name="long_conv",
    )(u_perm, g_fwd, kf, h2, h2i, g_inv, mult_perm, hbias)


def _out_kernel(x_ref, yh_ref, yp_ref, mod_ref, w_ref, b_ref, g_ref, beta_ref, o_ref,
                *, n2len, alpha):
    cbn = yh_ref.shape[0]
    ch = cbn * LANES
    yh = jnp.concatenate(
        [jnp.concatenate([yh_ref[cb, pl.ds(j, n2len, stride=SUBLANES), :]
                          for cb in range(cbn)], axis=1)
         for j in range(SUBLANES)], axis=0)
    acc = jnp.dot(yh.astype(BF16), w_ref[0:ch, :], preferred_element_type=F32)
    acc = acc + jnp.dot(yp_ref[...], w_ref[ch:, :], preferred_element_type=F32)
    h = alpha * x_ref[...] + mod_ref[2:3, :] * (acc + b_ref[...])
    mu = jnp.mean(h, axis=-1, keepdims=True)
    hc = h - mu
    var = jnp.mean(hc * hc, axis=-1, keepdims=True)
    o_ref[...] = hc * lax.rsqrt(var + LN_EPS) * g_ref[...] + beta_ref[...]


def _output_projection(x, yh_perm, yp, mod3, w_out, b_out, ln_g, ln_b, alpha):
    bsz, seq, d = x.shape
    _, _, cbn, tile, _ = yh_perm.shape
    ch = cbn * LANES
    n2len = tile // SUBLANES
    dp = yp.shape[2]
    full = lambda shape: pl.BlockSpec(shape, lambda b, i: (0,) * len(shape))
    kern = functools.partial(_out_kernel, n2len=n2len, alpha=alpha)
    return pl.pallas_call(
        kern,
        grid=(bsz, seq // tile),
        in_specs=[
            pl.BlockSpec((None, tile, d), lambda b, i: (b, i, 0)),
            pl.BlockSpec((None, None, cbn, tile, LANES), lambda b, i: (b, i, 0, 0, 0)),
            pl.BlockSpec((None, tile, dp), lambda b, i: (b, i, 0)),
            pl.BlockSpec((None, 3, d), lambda b, i: (b, 0, 0)),
            full((ch + dp, d)), full((1, d)), full((1, d)), full((1, d)),
        ],
        out_specs=pl.BlockSpec((None, tile, d), lambda b, i: (b, i, 0)),
        out_shape=jax.ShapeDtypeStruct(x.shape, x.dtype),
        compiler_params=pltpu.CompilerParams(
            dimension_semantics=("parallel", "arbitrary"), vmem_limit_bytes=VMEM_LIMIT),
        name="out_proj_deepnorm",
    )(x, yh_perm, yp, mod3, w_out.astype(BF16), b_out.reshape(1, d), ln_g.reshape(1, d),
      ln_b.reshape(1, d))


def _forward(x, c, w_ada, b_ada, w_in, b_in, conv_w, conv_b, filt_w1, filt_b1, filt_w_inner,
             filt_b_inner, filt_w_out, filt_freq, hyena_bias, pool_w, pool_scale, w_out,
             b_out, ln_g, ln_b, *, n2len):
    bsz, seq, d = x.shape
    depth = w_ada.shape[0]
    order, n_ch = hyena_bias.shape[1], hyena_bias.shape[2]
    alpha = (2.0 * depth) ** 0.25
    tables = _dft_tables(seq, n2len)
    h = x
    for layer in range(depth):
        mod3 = _modulation(c, w_ada[layer], b_ada[layer]).reshape(bsz, 3, d)
        v, x1, x2g, yp = _input_projection(
            h, mod3, w_in[layer], b_in[layer], conv_w[layer], conv_b[layer], pool_w[layer],
            pool_scale[layer], n2len)
        taps_f, taps_r = _filter_taps(
            seq, n2len, filt_w1[layer], filt_b1[layer], filt_w_inner[layer],
            filt_b_inner[layer], filt_w_out[layer], filt_freq[layer], n_ch, order)
        kf = _filter_spectrum(taps_f, taps_r, tables[0], tables[2], tables[3], n2len)
        hbias = hyena_bias[layer].reshape(order, 1, n_ch)
        z = v
        mults = (x1, x2g)
        for o in range(order):
            z = _long_conv(z, mults[o], kf, hbias, o, tables)
        h = _output_projection(h, z, yp, mod3, w_out[layer], b_out[layer], ln_g[layer],
                               ln_b[layer], alpha)
    return h


def kernel(x, c, w_ada, b_ada, w_in, b_in, conv_w, conv_b, filt_w1, filt_b1, filt_w_inner,
           filt_b_inner, filt_w_out, filt_freq, hyena_bias, pool_w, pool_scale, w_out, b_out,
           ln_g, ln_b):
    return _forward(x, c, w_ada, b_ada, w_in, b_in, conv_w, conv_b, filt_w1, filt_b1,
                    filt_w_inner, filt_b_inner, filt_w_out, filt_freq, hyena_bias, pool_w,
                    pool_scale, w_out, b_out, ln_g, ln_b, n2len=MINOR_LEN)
```

```python
import functools
import math

import jax
import jax.numpy as jnp
from jax import lax
from jax.experimental import pallas as pl
from jax.experimental.pallas import tpu as pltpu

F32 = jnp.float32
BF16 = jnp.bfloat16
HIGHEST = lax.Precision.HIGHEST

POOL_WINDOWS = (2, 4, 8, 16)
LN_EPS = 1e-5
DECAY_TARGET = 1e-2
FAST_DECAY_PCT = 0.3
SLOW_DECAY_PCT = 1.5

LANES = 128
SUBLANES = 8
HALO = 16
MINOR_LEN = 128
PHASE_STEPS = 4
CONV_STEPS = (4, 8, 8)
VMEM_LIMIT = 56 * 1024 * 1024


def _silu(x):
    return x * jax.nn.sigmoid(x)


def _dft_tables(seq, n2len):
    n1h = seq // n2len
    n_fft = 2 * seq
    n2i = jnp.arange(n2len, dtype=jnp.int32)
    k1 = jnp.arange(n1h, dtype=jnp.int32)
    odd = 2 * k1 + 1
    alpha = ((odd[:, None] * k1[None, :]) % (4 * n1h)).astype(F32) * (2.0 * math.pi / (4 * n1h))
    n2e = jnp.arange(n2len + 1, dtype=jnp.int32)
    beta = ((n2e[:, None] * odd[None, :]) % (2 * n_fft)).astype(F32) * (2.0 * math.pi / (2 * n_fft))
    ca, sa = jnp.cos(alpha)[None], jnp.sin(alpha)[None]
    cb, sb = jnp.cos(beta)[:, :, None], jnp.sin(beta)[:, :, None]
    cos_t = ca * cb[:-1] - sa * sb[:-1]
    sin_t = sa * cb[:-1] + ca * sb[:-1]
    g_fwd = jnp.concatenate([cos_t, -sin_t], axis=1).astype(BF16)
    cat, sat = jnp.cos(alpha).T[None], jnp.sin(alpha).T[None]
    cbt, sbt = jnp.cos(beta)[:-1, None, :], jnp.sin(beta)[:-1, None, :]
    g_inv = (2.0 / n_fft) * jnp.concatenate(
        [cat * cbt - sat * sbt, -(sat * cbt + cat * sbt)], axis=2)
    g_inv = g_inv.astype(BF16)
    cos_r = ca * cb[1:] - sa * sb[1:]
    sin_r = sa * cb[1:] + ca * sb[1:]
    g_rev = jnp.concatenate([cos_r, sin_r], axis=1).astype(BF16)

    m = (n2i[:, None] * n2i[None, :]) % n2len
    phi = m.astype(F32) * (2.0 * math.pi / n2len)
    cm, sm = jnp.cos(phi), jnp.sin(phi)
    h2 = jnp.block([[cm, sm], [-sm, cm]]).astype(BF16)
    h2i = jnp.block([[cm, -sm], [sm, cm]]).astype(BF16)
    return g_fwd, g_inv, g_rev, h2, h2i


def _filter_pos_features(seq, emb):
    t = jnp.linspace(0.0, 1.0, seq, dtype=F32)[:, None]
    bands = (emb - 1) // 2
    w = (2.0 * math.pi / seq) * jnp.arange(seq, dtype=F32)[:, None]
    f = jnp.linspace(1e-4, bands - 1, bands, dtype=F32)[None, :]
    return jnp.concatenate([t, jnp.cos(f * w), -jnp.sin(f * w)], axis=-1)


def _mod_kernel(c_ref, w_ref, b_ref, o_ref):
    s = _silu(c_ref[...])
    o_ref[...] = jnp.dot(s, w_ref[...], precision=HIGHEST,
                         preferred_element_type=F32) + b_ref[...]


def _modulation(c, w_ada, b_ada):
    bsz, d = c.shape
    n_out = w_ada.shape[1]
    return pl.pallas_call(
        _mod_kernel,
        grid=(n_out // d,),
        in_specs=[pl.BlockSpec((bsz, d), lambda j: (0, 0)),
                  pl.BlockSpec((d, d), lambda j: (0, j)),
                  pl.BlockSpec((1, d), lambda j: (0, j))],
        out_specs=pl.BlockSpec((bsz, d), lambda j: (0, j)),
        out_shape=jax.ShapeDtypeStruct((bsz, n_out), F32),
        name="adaln_mod",
    )(c, w_ada, b_ada.reshape(1, n_out))


def _store_permuted(o_ref, val, n2len, first=0):
    for cb in range(val.shape[1] // LANES):
        for j in range(val.shape[0] // n2len):
            o_ref[cb, pl.ds(first + j, n2len, stride=SUBLANES), :] = val[
                j * n2len:(j + 1) * n2len, cb * LANES:(cb + 1) * LANES]


def _load_minor(x_ref, j):
    blk = x_ref[:, j * SUBLANES:(j + 1) * SUBLANES, :]
    return blk.reshape(blk.shape[0] * SUBLANES, blk.shape[2])


def _proj_kernel(xp_ref, xc_ref, xn_ref, mod_ref, w_ref, b_ref, cw_ref, cb_ref, pw_ref,
                 ps_ref, v_ref, x1_ref, x2_ref, yp_ref, p_scr, *, tile, n2len, ch, seq):
    i = pl.program_id(1)
    nt = pl.num_programs(1)
    shift = mod_ref[0:1, :]
    scale1 = 1.0 + mod_ref[1:2, :]
    xe = jnp.concatenate([xp_ref[...], xc_ref[...], xn_ref[...]], axis=0)
    ue = (xe * scale1 + shift).astype(BF16)
    uc = ue[HALO:HALO + tile, :]
    row = lax.broadcasted_iota(jnp.int32, (tile + 2 * HALO, 1), 0)
    valid = ((row >= HALO) | (i > 0)) & ((row < tile + HALO) | (i < nt - 1))

    hg = jnp.dot(uc, w_ref[:, 3 * ch:4 * ch], preferred_element_type=F32) + b_ref[:, 3 * ch:4 * ch]
    gate = _silu(hg)

    for k, o_ref in enumerate((v_ref, x1_ref, x2_ref)):
        cols = slice(k * ch, (k + 1) * ch)
        p = jnp.dot(ue, w_ref[:, cols], preferred_element_type=F32) + b_ref[:, cols]
        p_scr[...] = jnp.where(valid, p, 0.0)
        s = (cb_ref[:, cols]
             + cw_ref[0:1, cols] * p_scr[HALO - 1:HALO - 1 + tile, :]
             + cw_ref[1:2, cols] * p_scr[HALO:HALO + tile, :]
             + cw_ref[2:3, cols] * p_scr[HALO + 1:HALO + 1 + tile, :])
        if k == 2:
            s = s * gate
        _store_permuted(o_ref, s, n2len)

    dp = ps_ref.shape[1]
    c0 = 4 * ch
    q_scr = p_scr
    pin = jnp.dot(ue, w_ref[:, c0:c0 + dp], preferred_element_type=F32) + b_ref[:, c0:c0 + dp]
    q_scr[...] = jnp.where(valid, pin, 0.0)
    pgate = jnp.dot(uc, w_ref[:, c0 + dp:c0 + 2 * dp], preferred_element_type=F32) + b_ref[:, c0 + dp:c0 + 2 * dp]
    pos = i * tile + lax.broadcasted_iota(jnp.int32, (tile, 1), 0)
    pg = dp // len(POOL_WINDOWS)
    groups = []
    for g, win in enumerate(POOL_WINDOWS):
        lanes = slice(g * pg, (g + 1) * pg)
        half = win // 2
        acc = q_scr[HALO - half:HALO - half + tile, lanes]
        for d in range(-half + 1, half):
            acc = acc + q_scr[HALO + d:HALO + d + tile, lanes]
        cnt = (jnp.minimum(pos + half, seq) - jnp.maximum(pos - half, 0)).astype(F32)
        diff = acc / cnt - q_scr[HALO:HALO + tile, lanes]
        groups.append(jnp.dot(diff.astype(BF16), pw_ref[g], preferred_element_type=F32))
    yp = jnp.concatenate(groups, axis=1) * ps_ref[...] * _silu(pgate)
    yp_ref[...] = yp.astype(yp_ref.dtype)


def _input_projection(x, mod3, w_in, b_in, conv_w, conv_b, pool_w, pool_scale, n2len):
    bsz, seq, d = x.shape
    ch = conv_w.shape[1] // 3
    dp = pool_scale.shape[0]
    n1h = seq // n2len
    tile = SUBLANES * n2len
    nt = seq // tile
    hb = tile // HALO
    n_proj = w_in.shape[1]
    cbn = ch // LANES
    assert max(POOL_WINDOWS) // 2 <= HALO and dp == ch
    perm = jax.ShapeDtypeStruct((bsz, n1h // SUBLANES, cbn, tile, LANES), F32)
    perm_spec = pl.BlockSpec((None, None, cbn, tile, LANES), lambda b, i: (b, i, 0, 0, 0))
    full = lambda shape: pl.BlockSpec(shape, lambda b, i: (0,) * len(shape))
    kern = functools.partial(_proj_kernel, tile=tile, n2len=n2len, ch=ch, seq=seq)
    return pl.pallas_call(
        kern,
        grid=(bsz, nt),
        in_specs=[
            pl.BlockSpec((None, HALO, d), lambda b, i: (b, jnp.maximum(i * hb - 1, 0), 0)),
            pl.BlockSpec((None, tile, d), lambda b, i: (b, i, 0)),
            pl.BlockSpec((None, HALO, d), lambda b, i: (b, jnp.minimum((i + 1) * hb, seq // HALO - 1), 0)),
            pl.BlockSpec((None, 3, d), lambda b, i: (b, 0, 0)),
            full((d, n_proj)), full((1, n_proj)), full((3, 3 * ch)), full((1, 3 * ch)),
            full(pool_w.shape), full((1, dp)),
        ],
        out_specs=[perm_spec, perm_spec, perm_spec,
                   pl.BlockSpec((None, tile, dp), lambda b, i: (b, i, 0))],
        out_shape=[perm, perm, perm, jax.ShapeDtypeStruct((bsz, seq, dp), BF16)],
        scratch_shapes=[pltpu.VMEM((tile + 2 * HALO, ch), F32)],
        compiler_params=pltpu.CompilerParams(
            dimension_semantics=("parallel", "arbitrary"), vmem_limit_bytes=VMEM_LIMIT),
        name="in_proj_conv_pool",
    )(x, x, x, mod3, w_in.astype(BF16), b_in.reshape(1, n_proj), conv_w,
      conv_b.reshape(1, 3 * ch), pool_w.astype(BF16), pool_scale.reshape(1, dp))


def _filter_mlp_kernel(z_ref, w1_ref, b1_ref, wi_ref, bi_ref, fr_ref, wo_ref, ad_ref,
                       of_ref, or_ref, *, n2len):
    z = z_ref[...]
    fr = fr_ref[...]
    h = jnp.sin(fr * (jnp.dot(z.astype(BF16), w1_ref[...],
                              preferred_element_type=F32) + b1_ref[...]))
    for l in range(wi_ref.shape[0]):
        h = jnp.sin(fr * (jnp.dot(h.astype(BF16), wi_ref[l],
                                  preferred_element_type=F32) + bi_ref[l:l + 1, :]))
    hb = h.astype(BF16)
    half_lanes = LANES // 2
    n1_half = z.shape[0] // n2len
    for half in range(2):
        t = z[:, half * half_lanes:half * half_lanes + 1]
        decay = jnp.exp(-t * ad_ref[...])
        for d, o_ref in enumerate((of_ref, or_ref)):
            k = jnp.dot(hb, wo_ref[d, half], preferred_element_type=F32)
            _store_permuted(o_ref, k * decay, n2len, first=half * n1_half)


def _block_diag2(w):
    zero = jnp.zeros_like(w)
    return jnp.concatenate([jnp.concatenate([w, zero], axis=-1),
                            jnp.concatenate([zero, w], axis=-1)], axis=-2)


def _filter_taps(seq, n2len, w1, b1, w_inner, b_inner, w_out, freq, n_ch, order):
    emb, hid = w1.shape
    n_inner = w_inner.shape[0]
    oc = order * n_ch
    hl = LANES // 2
    assert emb <= hl and hid <= hl
    tile = SUBLANES * n2len
    half = tile // 2
    groups = seq // tile
    z = jnp.pad(_filter_pos_features(seq, emb), ((0, 0), (0, hl - emb)))
    zp = jnp.transpose(z.reshape(groups, 2, half, hl), (0, 2, 1, 3)).reshape(groups * half, LANES)
    ph = hl - hid
    w1d = _block_diag2(jnp.pad(w1, ((0, hl - emb), (0, ph))))
    b1d = jnp.tile(jnp.pad(b1, (0, ph)), 2).reshape(1, LANES)
    wid = _block_diag2(jnp.pad(w_inner, ((0, 0), (0, ph), (0, ph))))
    bid = jnp.tile(jnp.pad(b_inner, ((0, 0), (0, ph))), (1, 2))
    frd = jnp.tile(jnp.pad(freq, (0, ph), constant_values=1.0), 2).reshape(1, LANES)
    wo = jnp.transpose(w_out.reshape(hid, order, 2, n_ch), (2, 0, 1, 3)).reshape(2, hid, oc)
    wo = jnp.pad(wo, ((0, 0), (0, ph), (0, 0)))
    zero = jnp.zeros_like(wo)
    wo4 = jnp.stack([jnp.concatenate([wo, zero], axis=1),
                     jnp.concatenate([zero, wo], axis=1)], axis=1)
    min_decay = math.log(DECAY_TARGET) / SLOW_DECAY_PCT
    max_decay = math.log(DECAY_TARGET) / FAST_DECAY_PCT
    absdelta = jnp.abs(jnp.linspace(min_decay, max_decay, n_ch, dtype=F32))
    absdelta = jnp.tile(absdelta, order).reshape(1, oc)

    full = lambda shape: pl.BlockSpec(shape, lambda i: (0,) * len(shape))
    kern = functools.partial(_filter_mlp_kernel, n2len=n2len)
    taps = jax.ShapeDtypeStruct((groups, oc // LANES, tile, LANES), F32)
    taps_spec = pl.BlockSpec((None, oc // LANES, tile, LANES), lambda i: (i, 0, 0, 0))
    return pl.pallas_call(
        kern,
        grid=(groups,),
        in_specs=[
            pl.BlockSpec((half, LANES), lambda i: (i, 0)),
            full((LANES, LANES)), full((1, LANES)), full((n_inner, LANES, LANES)),
            full((n_inner, LANES)), full((1, LANES)), full((2, 2, LANES, oc)), full((1, oc)),
        ],
        out_specs=[taps_spec, taps_spec],
        out_shape=[taps, taps],
        compiler_params=pltpu.CompilerParams(
            dimension_semantics=("arbitrary",), vmem_limit_bytes=VMEM_LIMIT),
        name="filter_mlp",
    )(zp, w1d.astype(BF16), b1d, wid.astype(BF16), bid, frd, wo4.astype(BF16), absdelta)


def _resident(shape):
    return pl.BlockSpec(shape, lambda *_: (0,) * len(shape), pipeline_mode=pl.Buffered(1))


def _spectrum_views(a_scr, chunk, ck1, n1h, pitch):
    re0 = pl.multiple_of(chunk * (ck1 * pitch), SUBLANES)
    im0 = pl.multiple_of(chunk * (ck1 * pitch) + n1h * pitch, SUBLANES)
    return a_scr.at[pl.ds(re0, ck1 * pitch)], a_scr.at[pl.ds(im0, ck1 * pitch)]


def _spectrum_pair(a_re, a_im, j, n2len, pitch):
    def one(jj):
        return jnp.concatenate([a_re[jj * pitch:jj * pitch + n2len, :],
                                a_im[jj * pitch:jj * pitch + n2len, :]], axis=0)
    return jnp.concatenate([one(j), one(j + 1)], axis=1).astype(BF16)


def _filter_fft_kernel(hf_ref, hr_ref, g_ref, gr_ref, h2_ref, o_ref, a_scr,
                       *, ns, cn2, ck1, n1h, n2len, pitch):
    s = pl.program_id(1)

    @pl.when(s < ns)
    def _():
        for j in range(cn2):
            jr = cn2 - 1 - j
            m = s * cn2 + j
            g = jnp.concatenate([g_ref[m], gr_ref[n2len - 1 - m]], axis=1)
            x = jnp.concatenate([_load_minor(hf_ref, j), _load_minor(hr_ref, jr)], axis=0)
            r = jnp.dot(g, x.astype(BF16), preferred_element_type=F32)
            a_scr[pl.ds(m, 2 * n1h, stride=pitch), :] = r

    @pl.when(s >= ns)
    def _():
        a_re, a_im = _spectrum_views(a_scr, s - ns, ck1, n1h, pitch)
        for j in range(0, ck1, 2):
            rows = _spectrum_pair(a_re, a_im, j, n2len, pitch)
            u = jnp.dot(h2_ref[...], rows, preferred_element_type=F32).astype(o_ref.dtype)
            o_ref[j] = u[:, :LANES]
            o_ref[j + 1] = u[:, LANES:]


def _filter_spectrum(taps_f, taps_r, g_fwd, g_rev, h2, n2len):
    groups, ocb, _, _ = taps_f.shape
    oc = ocb * LANES
    n1h = groups * SUBLANES
    ns = PHASE_STEPS
    cn2, ck1 = n2len // ns, n1h // ns
    pitch = n2len + SUBLANES
    kern = functools.partial(_filter_fft_kernel, ns=ns, cn2=cn2, ck1=ck1, n1h=n1h,
                             n2len=n2len, pitch=pitch)
    return pl.pallas_call(
        kern,
        grid=(oc // LANES, 2 * ns),
        in_specs=[
            pl.BlockSpec((groups, None, cn2 * SUBLANES, LANES),
                         lambda c, s: (0, c, jnp.minimum(s, ns - 1), 0)),
            pl.BlockSpec((groups, None, cn2 * SUBLANES, LANES),
                         lambda c, s: (0, c, jnp.maximum(ns - 1 - s, 0), 0)),
            _resident(g_fwd.shape), _resident(g_rev.shape), _resident(h2.shape),
        ],
        out_specs=pl.BlockSpec((ck1, 2 * n2len, LANES), lambda c, s: (jnp.maximum(s - ns, 0), 0, c)),
        out_shape=jax.ShapeDtypeStruct((n1h, 2 * n2len, oc), BF16),
        scratch_shapes=[pltpu.VMEM((2 * n1h * pitch, LANES), F32)],
        compiler_params=pltpu.CompilerParams(
            dimension_semantics=("parallel", "arbitrary"), vmem_limit_bytes=VMEM_LIMIT),
        name="filter_spectrum",
    )(taps_f, taps_r, g_fwd, g_rev, h2)


def _long_conv_kernel(vf_ref, g_ref, kf_ref, h2_ref, h2i_ref, gi_ref, xm_ref, hb_ref, o_ref,
                      a_scr, keep_scr, *, steps, n1h, n2len, pitch):
    ns1, ns2, ns3 = steps
    cn2a, ck1, cn2 = n2len // ns1, n1h // ns2, n2len // ns3
    s = pl.program_id(2)

    @pl.when(s < ns1)
    def _():
        base = s * cn2a
        for j in range(cn2a):
            x = _load_minor(vf_ref, j)
            keep_scr[base + j] = x
            r = jnp.dot(g_ref[base + j], x.astype(BF16), preferred_element_type=F32)
            a_scr[pl.ds(base + j, 2 * n1h, stride=pitch), :] = r

    @pl.when((s >= ns1) & (s < ns1 + ns2))
    def _():
        a_re, a_im = _spectrum_views(a_scr, s - ns1, ck1, n1h, pitch)
        for j in range(0, ck1, 2):
            rows = _spectrum_pair(a_re, a_im, j, n2len, pitch)
            u = jnp.dot(h2_ref[...], rows, preferred_element_type=F32)
            ur, ui = u[:n2len], u[n2len:]
            kf = jnp.concatenate([kf_ref[j], kf_ref[j + 1]], axis=1).astype(F32)
            kr, ki = kf[:n2len], kf[n2len:]
            y = jnp.concatenate([ur * kr - ui * ki, ur * ki + ui * kr], axis=0)
            bb = jnp.dot(h2i_ref[...], y.astype(BF16), preferred_element_type=F32)
            for jj in range(2):
                lanes = slice(jj * LANES, (jj + 1) * LANES)
                a_re[(j + jj) * pitch:(j + jj) * pitch + n2len, :] = bb[:n2len, lanes]
                a_im[(j + jj) * pitch:(j + jj) * pitch + n2len, :] = bb[n2len:, lanes]

    @pl.when(s >= ns1 + ns2)
    def _():
        base = (s - ns1 - ns2) * cn2
        for j in range(cn2):
            rows = a_scr[pl.ds(base + j, 2 * n1h, stride=pitch), :]
            y = jnp.dot(gi_ref[base + j], rows.astype(BF16), preferred_element_type=F32)
            z = _load_minor(xm_ref, j) * (y + hb_ref[...] * keep_scr[base + j])
            o_ref[:, j * SUBLANES:(j + 1) * SUBLANES, :] = z.reshape(
                n1h // SUBLANES, SUBLANES, z.shape[1])


def _long_conv(u_perm, mult_perm, kf, hbias, order_idx, tables):
    g_fwd, g_inv, _, h2, h2i = tables
    bsz, groups, cbn, tile, _ = u_perm.shape
    n2len = tile // SUBLANES
    n1h = groups * SUBLANES
    steps = CONV_STEPS
    ns1, ns2, ns3 = steps
    pitch = n2len + SUBLANES

    def late(s):
        return jnp.clip(s - ns1 - ns2, 0, ns3 - 1)

    data_early = pl.BlockSpec((None, groups, None, (n2len // ns1) * SUBLANES, LANES),
                              lambda c, b, s: (b, 0, c, jnp.minimum(s, ns1 - 1), 0))
    data_late = pl.BlockSpec((None, groups, None, (n2len // ns3) * SUBLANES, LANES),
                             lambda c, b, s: (b, 0, c, late(s), 0))
    in_specs = [
        data_early,
        _resident(g_fwd.shape),
        pl.BlockSpec((n1h // ns2, 2 * n2len, LANES),
                     lambda c, b, s: (jnp.clip(s - ns1, 0, ns2 - 1), 0, order_idx * cbn + c)),
        _resident(h2.shape), _resident(h2i.shape), _resident(g_inv.shape),
        data_late,
        pl.BlockSpec((None, 1, LANES), lambda c, b, s: (order_idx, 0, c)),
    ]
    kern = functools.partial(_long_conv_kernel, steps=steps, n1h=n1h, n2len=n2len,
                             pitch=pitch)
    return pl.pallas_call(
        kern,
        grid=(cbn, bsz, sum(steps)),
        in_specs=in_specs,
        out_specs=data_late,
        out_shape=jax.ShapeDtypeStruct(u_perm.shape, F32),
        scratch_shapes=[pltpu.VMEM((2 * n1h * pitch, LANES), F32),
                        pltpu.VMEM((n2len, n1h, LANES), F32)],
        compiler_params=pltpu.CompilerParams(
            dimension_semantics=("parallel", "parallel", "arbitrary"),
            vmem_limit_bytes=VMEM_LIMIT),
        name="long_conv",
    )(u_perm, g_fwd, kf, h2, h2i, g_inv, mult_perm, hbias)


def _out_kernel(x_ref, yh_ref, yp_ref, mod_ref, w_ref, b_ref, g_ref, beta_ref, o_ref,
                *, n2len, alpha):
    cbn = yh_ref.shape[0]
    yh = jnp.concatenate(
        [jnp.concatenate([yh_ref[cb, pl.ds(j, n2len, stride=SUBLANES), :]
                          for cb in range(cbn)], axis=1)
         for j in range(SUBLANES)], axis=0)
    y = jnp.concatenate([yh.astype(BF16), yp_ref[...]], axis=1)
    acc = jnp.dot(y, w_ref[...], preferred_element_type=F32)
    h = alpha * x_ref[...] + mod_ref[2:3, :] * (acc + b_ref[...])
    mu = jnp.mean(h, axis=-1, keepdims=True)
    hc = h - mu
    var = jnp.mean(hc * hc, axis=-1, keepdims=True)
    o_ref[...] = hc * lax.rsqrt(var + LN_EPS) * g_ref[...] + beta_ref[...]


def _output_projection(x, yh_perm, yp, mod3, w_out, b_out, ln_g, ln_b, alpha):
    bsz, seq, d = x.shape
    _, _, cbn, tile, _ = yh_perm.shape
    ch = cbn * LANES
    n2len = tile // SUBLANES
    dp = yp.shape[2]
    full = lambda shape: pl.BlockSpec(shape, lambda b, i: (0,) * len(shape))
    kern = functools.partial(_out_kernel, n2len=n2len, alpha=alpha)
    return pl.pallas_call(
        kern,
        grid=(bsz, seq // tile),
        in_specs=[
            pl.BlockSpec((None, tile, d), lambda b, i: (b, i, 0)),
            pl.BlockSpec((None, None, cbn, tile, LANES), lambda b, i: (b, i, 0, 0, 0)),
            pl.BlockSpec((None, tile, dp), lambda b, i: (b, i, 0)),
            pl.BlockSpec((None, 3, d), lambda b, i: (b, 0, 0)),
            full((ch + dp, d)), full((1, d)), full((1, d)), full((1, d)),
        ],
        out_specs=pl.BlockSpec((None, tile, d), lambda b, i: (b, i, 0)),
        out_shape=jax.ShapeDtypeStruct(x.shape, x.dtype),
        compiler_params=pltpu.CompilerParams(
            dimension_semantics=("parallel", "arbitrary"), vmem_limit_bytes=VMEM_LIMIT),
        name="out_proj_deepnorm",
    )(x, yh_perm, yp, mod3, w_out.astype(BF16), b_out.reshape(1, d), ln_g.reshape(1, d),
      ln_b.reshape(1, d))


def _forward(x, c, w_ada, b_ada, w_in, b_in, conv_w, conv_b, filt_w1, filt_b1, filt_w_inner,
             filt_b_inner, filt_w_out, filt_freq, hyena_bias, pool_w, pool_scale, w_out,
             b_out, ln_g, ln_b, *, n2len):
    bsz, seq, d = x.shape
    depth = w_ada.shape[0]
    order, n_ch = hyena_bias.shape[1], hyena_bias.shape[2]
    alpha = (2.0 * depth) ** 0.25
    tables = _dft_tables(seq, n2len)
    h = x
    for layer in range(depth):
        mod3 = _modulation(c, w_ada[layer], b_ada[layer]).reshape(bsz, 3, d)
        v, x1, x2g, yp = _input_projection(
            h, mod3, w_in[layer], b_in[layer], conv_w[layer], conv_b[layer], pool_w[layer],
            pool_scale[layer], n2len)
        taps_f, taps_r = _filter_taps(
            seq, n2len, filt_w1[layer], filt_b1[layer], filt_w_inner[layer],
            filt_b_inner[layer], filt_w_out[layer], filt_freq[layer], n_ch, order)
        kf = _filter_spectrum(taps_f, taps_r, tables[0], tables[2], tables[3], n2len)
        hbias = hyena_bias[layer].reshape(order, 1, n_ch)
        z = v
        mults = (x1, x2g)
        for o in range(order):
            z = _long_conv(z, mults[o], kf, hbias, o, tables)
        h = _output_projection(h, z, yp, mod3, w_out[layer], b_out[layer], ln_g[layer],
                               ln_b[layer], alpha)
    return h


def kernel(x, c, w_ada, b_ada, w_in, b_in, conv_w, conv_b, filt_w1, filt_b1, filt_w_inner,
           filt_b_inner, filt_w_out, filt_freq, hyena_bias, pool_w, pool_scale, w_out, b_out,
           ln_g, ln_b):
    return _forward(x, c, w_ada, b_ada, w_in, b_in, conv_w, conv_b, filt_w1, filt_b1,
                    filt_w_inner, filt_b_inner, filt_w_out, filt_freq, hyena_bias, pool_w,
                    pool_scale, w_out, b_out, ln_g, ln_b, n2len=MINOR_LEN)
```

```python
import functools
import math

import jax
import jax.numpy as jnp
from jax import lax
from jax.experimental import pallas as pl
from jax.experimental.pallas import tpu as pltpu

F32 = jnp.float32
BF16 = jnp.bfloat16
HIGHEST = lax.Precision.HIGHEST

POOL_WINDOWS = (2, 4, 8, 16)
LN_EPS = 1e-5
DECAY_TARGET = 1e-2
FAST_DECAY_PCT = 0.3
SLOW_DECAY_PCT = 1.5

LANES = 128
SUBLANES = 8
HALO = 16
BAND_ROWS = 128
BAND_SPAN = 256
MINOR_LEN = 128
PHASE_STEPS = 4
CONV_STEPS = (4, 4, 8)
VMEM_LIMIT = 56 * 1024 * 1024


def _silu(x):
    return x * jax.nn.sigmoid(x)


def _dft_tables(seq, n2len):
    n1h = seq // n2len
    n_fft = 2 * seq
    n2i = jnp.arange(n2len, dtype=jnp.int32)
    k1 = jnp.arange(n1h, dtype=jnp.int32)
    odd = 2 * k1 + 1
    alpha = ((odd[:, None] * k1[None, :]) % (4 * n1h)).astype(F32) * (2.0 * math.pi / (4 * n1h))
    n2e = jnp.arange(n2len + 1, dtype=jnp.int32)
    beta = ((n2e[:, None] * odd[None, :]) % (2 * n_fft)).astype(F32) * (2.0 * math.pi / (2 * n_fft))
    ca, sa = jnp.cos(alpha)[None], jnp.sin(alpha)[None]
    cb, sb = jnp.cos(beta)[:, :, None], jnp.sin(beta)[:, :, None]
    cos_t = ca * cb[:-1] - sa * sb[:-1]
    sin_t = sa * cb[:-1] + ca * sb[:-1]
    g_fwd = jnp.concatenate([cos_t, -sin_t], axis=1).astype(BF16)
    cat, sat = jnp.cos(alpha).T[None], jnp.sin(alpha).T[None]
    cbt, sbt = jnp.cos(beta)[:-1, None, :], jnp.sin(beta)[:-1, None, :]
    g_inv = (2.0 / n_fft) * jnp.concatenate(
        [cat * cbt - sat * sbt, -(sat * cbt + cat * sbt)], axis=2)
    g_inv = g_inv.astype(BF16)
    cos_r = ca * cb[1:] - sa * sb[1:]
    sin_r = sa * cb[1:] + ca * sb[1:]
    g_rev = jnp.concatenate([cos_r, sin_r], axis=1).astype(BF16)

    m = (n2i[:, None] * n2i[None, :]) % n2len
    phi = m.astype(F32) * (2.0 * math.pi / n2len)
    cm, sm = jnp.cos(phi), jnp.sin(phi)
    h2 = jnp.block([[cm, sm], [-sm, cm]]).astype(BF16)
    h2i = jnp.block([[cm, -sm], [sm, cm]]).astype(BF16)
    return g_fwd, g_inv, g_rev, h2, h2i


def _packed_pos_features(seq, emb, tile):
    hl = LANES // 2
    half = tile // 2
    bands = (emb - 1) // 2
    row = jnp.arange((seq // tile) * half, dtype=jnp.int32)[:, None]
    lane = jnp.arange(LANES, dtype=jnp.int32)[None, :]
    pos = ((row // half) * tile + (lane // hl) * half + row % half).astype(F32)
    feat = lane % hl
    t = pos / (seq - 1)
    w = (2.0 * math.pi / seq) * pos
    f = jnp.linspace(1e-4, bands - 1, bands, dtype=F32)[(feat - 1) % bands]
    return jnp.where(feat == 0, t,
                     jnp.where(feat <= bands, jnp.cos(f * w),
                               jnp.where(feat <= 2 * bands, -jnp.sin(f * w), 0.0)))


def _mod_kernel(c_ref, w_ref, b_ref, o_ref):
    s = _silu(c_ref[...])
    o_ref[...] = jnp.dot(s, w_ref[...], precision=HIGHEST,
                         preferred_element_type=F32) + b_ref[...]


def _modulation(c, w_ada, b_ada):
    bsz, d = c.shape
    n_out = w_ada.shape[1]
    return pl.pallas_call(
        _mod_kernel,
        grid=(n_out // d,),
        in_specs=[pl.BlockSpec((bsz, d), lambda j: (0, 0)),
                  pl.BlockSpec((d, d), lambda j: (0, j)),
                  pl.BlockSpec((1, d), lambda j: (0, j))],
        out_specs=pl.BlockSpec((bsz, d), lambda j: (0, j)),
        out_shape=jax.ShapeDtypeStruct((bsz, n_out), F32),
        name="adaln_mod",
    )(c, w_ada, b_ada.reshape(1, n_out))


def _store_permuted(o_ref, val, n2len, first=0):
    for cb in range(val.shape[1] // LANES):
        for j in range(val.shape[0] // n2len):
            o_ref[cb, pl.ds(first + j, n2len, stride=SUBLANES), :] = val[
                j * n2len:(j + 1) * n2len, cb * LANES:(cb + 1) * LANES]


def _load_minor(x_ref, j):
    blk = x_ref[:, j * SUBLANES:(j + 1) * SUBLANES, :]
    return blk.reshape(blk.shape[0] * SUBLANES, blk.shape[2])


def _band_window(b, tile):
    last = tile + 2 * HALO - BAND_SPAN
    start = min(b * BAND_ROWS, last)
    return start, 0 if start == b * BAND_ROWS else 1


def _band_matrices(tile):
    t = jnp.arange(BAND_ROWS, dtype=jnp.int32)[:, None]
    k = jnp.arange(BAND_SPAN, dtype=jnp.int32)[None, :]
    mats = []
    for win in POOL_WINDOWS:
        half = win // 2
        per_variant = []
        for b in (0, tile // BAND_ROWS - 1):
            start, _ = _band_window(b, tile)
            rel = k + start - HALO - (b * BAND_ROWS + t)
            per_variant.append(((rel >= -half) & (rel < half)).astype(BF16))
        mats.append(jnp.stack(per_variant))
    return jnp.stack(mats)


def _proj_kernel(xp_ref, xc_ref, xn_ref, mod_ref, w_ref, b_ref, cw_ref, cb_ref, pw_ref,
                 ps_ref, band_ref, v_ref, x1_ref, x2_ref, yp_ref, p_scr, q_scr,
                 *, tile, n2len, ch, seq):
    i = pl.program_id(1)
    nt = pl.num_programs(1)
    shift = mod_ref[0:1, :]
    scale1 = 1.0 + mod_ref[1:2, :]
    xe = jnp.concatenate([xp_ref[...], xc_ref[...], xn_ref[...]], axis=0)
    ue = (xe * scale1 + shift).astype(BF16)
    uc = ue[HALO:HALO + tile, :]
    row = lax.broadcasted_iota(jnp.int32, (tile + 2 * HALO, 1), 0)
    valid = ((row >= HALO) | (i > 0)) & ((row < tile + HALO) | (i < nt - 1))

    hg = jnp.dot(uc, w_ref[:, 3 * ch:4 * ch], preferred_element_type=F32) + b_ref[:, 3 * ch:4 * ch]
    gate = _silu(hg)

    for k, o_ref in enumerate((v_ref, x1_ref, x2_ref)):
        cols = slice(k * ch, (k + 1) * ch)
        p = jnp.dot(ue, w_ref[:, cols], preferred_element_type=F32) + b_ref[:, cols]
        p_scr[...] = jnp.where(valid, p, 0.0)
        s = (cb_ref[:, cols]
             + cw_ref[0:1, cols] * p_scr[HALO - 1:HALO - 1 + tile, :]
             + cw_ref[1:2, cols] * p_scr[HALO:HALO + tile, :]
             + cw_ref[2:3, cols] * p_scr[HALO + 1:HALO + 1 + tile, :])
        if k == 2:
            s = s * gate
        _store_permuted(o_ref, s, n2len)

    dp = ps_ref.shape[1]
    c0 = 4 * ch
    pin = jnp.dot(ue, w_ref[:, c0:c0 + dp], preferred_element_type=F32) + b_ref[:, c0:c0 + dp]
    pin = jnp.where(valid, pin, 0.0)
    q_scr[...] = pin.astype(BF16)
    pin_c = pin[HALO:HALO + tile, :]
    pgate = jnp.dot(uc, w_ref[:, c0 + dp:c0 + 2 * dp], preferred_element_type=F32) + b_ref[:, c0 + dp:c0 + 2 * dp]
    pos = i * tile + lax.broadcasted_iota(jnp.int32, (tile, 1), 0)
    pg = dp // len(POOL_WINDOWS)
    groups = []
    for g, win in enumerate(POOL_WINDOWS):
        lanes = slice(g * pg, (g + 1) * pg)
        half = win // 2
        sums = []
        for b in range(tile // BAND_ROWS):
            start, variant = _band_window(b, tile)
            sums.append(jnp.dot(band_ref[g, variant], q_scr[start:start + BAND_SPAN, lanes],
                                preferred_element_type=F32))
        acc = jnp.concatenate(sums, axis=0)
        cnt = (jnp.minimum(pos + half, seq) - jnp.maximum(pos - half, 0)).astype(F32)
        diff = acc / cnt - pin_c[:, lanes]
        groups.append(jnp.dot(diff.astype(BF16), pw_ref[g], preferred_element_type=F32))
    yp = jnp.concatenate(groups, axis=1) * ps_ref[...] * _silu(pgate)
    yp_ref[...] = yp.astype(yp_ref.dtype)


def _input_projection(x, mod3, w_in, b_in, conv_w, conv_b, pool_w, pool_scale, n2len):
    bsz, seq, d = x.shape
    ch = conv_w.shape[1] // 3
    dp = pool_scale.shape[0]
    n1h = seq // n2len
    tile = SUBLANES * n2len
    nt = seq // tile
    hb = tile // HALO
    n_proj = w_in.shape[1]
    cbn = ch // LANES
    assert max(POOL_WINDOWS) // 2 <= HALO and dp == ch and tile % BAND_ROWS == 0
    band = _band_matrices(tile)
    perm =jax.ShapeDtypeStruct((bsz, n1h // SUBLANES, cbn, tile, LANES), F32)
    perm_spec = pl.BlockSpec((None, None, cbn, tile, LANES), lambda b, i: (b, i, 0, 0, 0))
    full = lambda shape: pl.BlockSpec(shape, lambda b, i: (0,) * len(shape))
    kern = functools.partial(_proj_kernel, tile=tile, n2len=n2len, ch=ch, seq=seq)
    return pl.pallas_call(
        kern,
        grid=(bsz, nt),
        in_specs=[
            pl.BlockSpec((None, HALO, d), lambda b, i: (b, jnp.maximum(i * hb - 1, 0), 0)),
            pl.BlockSpec((None, tile, d), lambda b, i: (b, i, 0)),
            pl.BlockSpec((None, HALO, d), lambda b, i: (b, jnp.minimum((i + 1) * hb, seq // HALO - 1), 0)),
            pl.BlockSpec((None, 3, d), lambda b, i: (b, 0, 0)),
            full((d, n_proj)), full((1, n_proj)), full((3, 3 * ch)), full((1, 3 * ch)),
            full(pool_w.shape), full((1, dp)), full(band.shape),
        ],
        out_specs=[perm_spec, perm_spec, perm_spec,
                   pl.BlockSpec((None, tile, dp), lambda b, i: (b, i, 0))],
        out_shape=[perm, perm, perm, jax.ShapeDtypeStruct((bsz, seq, dp), BF16)],
        scratch_shapes=[pltpu.VMEM((tile + 2 * HALO, ch), F32),
                        pltpu.VMEM((tile + 2 * HALO, dp), BF16)],
        compiler_params=pltpu.CompilerParams(
            dimension_semantics=("parallel", "arbitrary"), vmem_limit_bytes=VMEM_LIMIT),
        name="in_proj_conv_pool",
    )(x, x, x, mod3, w_in.astype(BF16), b_in.reshape(1, n_proj), conv_w,
      conv_b.reshape(1, 3 * ch), pool_w.astype(BF16), pool_scale.reshape(1, dp), band)


def _filter_mlp_kernel(z_ref, w1_ref, b1_ref, wi_ref, bi_ref, fr_ref, wo_ref, ad_ref,
                       of_ref, or_ref, *, n2len):
    z = z_ref[...]
    fr = fr_ref[...]
    h = jnp.sin(fr * (jnp.dot(z.astype(BF16), w1_ref[...],
                              preferred_element_type=F32) + b1_ref[...]))
    for l in range(wi_ref.shape[0]):
        h = jnp.sin(fr * (jnp.dot(h.astype(BF16), wi_ref[l],
                                  preferred_element_type=F32) + bi_ref[l:l + 1, :]))
    hb = h.astype(BF16)
    half_lanes = LANES // 2
    n1_half = z.shape[0] // n2len
    for half in range(2):
        t = z[:, half * half_lanes:half * half_lanes + 1]
        decay = jnp.exp(-t * ad_ref[...])
        for d, o_ref in enumerate((of_ref, or_ref)):
            k = jnp.dot(hb, wo_ref[d, half], preferred_element_type=F32)
            _store_permuted(o_ref, k * decay, n2len, first=half * n1_half)


def _block_diag2(w):
    zero = jnp.zeros_like(w)
    return jnp.concatenate([jnp.concatenate([w, zero], axis=-1),
                            jnp.concatenate([zero, w], axis=-1)], axis=-2)


def _filter_taps(seq, n2len, w1, b1, w_inner, b_inner, w_out, freq, n_ch, order):
    emb, hid = w1.shape
    n_inner = w_inner.shape[0]
    oc = order * n_ch
    hl = LANES // 2
    assert emb <= hl and hid <= hl
    tile = SUBLANES * n2len
    half = tile // 2
    groups = seq // tile
    zp = _packed_pos_features(seq, emb, tile)
    ph = hl - hid
    w1d = _block_diag2(jnp.pad(w1, ((0, hl - emb), (0, ph))))
    b1d = jnp.tile(jnp.pad(b1, (0, ph)), 2).reshape(1, LANES)
    wid = _block_diag2(jnp.pad(w_inner, ((0, 0), (0, ph), (0, ph))))
    bid = jnp.tile(jnp.pad(b_inner, ((0, 0), (0, ph))), (1, 2))
    frd = jnp.tile(jnp.pad(freq, (0, ph), constant_values=1.0), 2).reshape(1, LANES)
    wo = jnp.transpose(w_out.reshape(hid, order, 2, n_ch), (2, 0, 1, 3)).reshape(2, hid, oc)
    wo = jnp.pad(wo, ((0, 0), (0, ph), (0, 0)))
    zero = jnp.zeros_like(wo)
    wo4 = jnp.stack([jnp.concatenate([wo, zero], axis=1),
                     jnp.concatenate([zero, wo], axis=1)], axis=1)
    min_decay = math.log(DECAY_TARGET) / SLOW_DECAY_PCT
    max_decay = math.log(DECAY_TARGET) / FAST_DECAY_PCT
    absdelta = jnp.abs(jnp.linspace(min_decay, max_decay, n_ch, dtype=F32))
    absdelta = jnp.tile(absdelta, order).reshape(1, oc)

    full = lambda shape: pl.BlockSpec(shape, lambda i: (0,) * len(shape))
    kern = functools.partial(_filter_mlp_kernel, n2len=n2len)
    taps = jax.ShapeDtypeStruct((groups, oc // LANES, tile, LANES), F32)
    taps_spec = pl.BlockSpec((None, oc // LANES, tile, LANES), lambda i: (i, 0, 0, 0))
    return pl.pallas_call(
        kern,
        grid=(groups,),
        in_specs=[
            pl.BlockSpec((half, LANES), lambda i: (i, 0)),
            full((LANES, LANES)), full((1, LANES)), full((n_inner, LANES, LANES)),
            full((n_inner, LANES)), full((1, LANES)), full((2, 2, LANES, oc)), full((1, oc)),
        ],
        out_specs=[taps_spec, taps_spec],
        out_shape=[taps, taps],
        compiler_params=pltpu.CompilerParams(
            dimension_semantics=("arbitrary",), vmem_limit_bytes=VMEM_LIMIT),
        name="filter_mlp",
    )(zp, w1d.astype(BF16), b1d, wid.astype(BF16), bid, frd, wo4.astype(BF16), absdelta)


def _resident(shape):
    return pl.BlockSpec(shape, lambda *_: (0,) * len(shape), pipeline_mode=pl.Buffered(1))


def _spectrum_views(a_scr, chunk, ck1, n1h, pitch):
    re0 = pl.multiple_of(chunk * (ck1 * pitch), SUBLANES)
    im0 = pl.multiple_of(chunk * (ck1 * pitch) + n1h * pitch, SUBLANES)
    return a_scr.at[pl.ds(re0, ck1 * pitch)], a_scr.at[pl.ds(im0, ck1 * pitch)]


def _spectrum_pair(a_re, a_im, j, n2len, pitch):
    def one(jj):
        return jnp.concatenate([a_re[jj * pitch:jj * pitch + n2len, :],
                                a_im[jj * pitch:jj * pitch + n2len, :]], axis=0)
    return jnp.concatenate([one(j), one(j + 1)], axis=1).astype(BF16)


def _filter_fft_kernel(hf_ref, hr_ref, g_ref, gr_ref, h2_ref, o_ref, a_scr,
                       *, ns, cn2, ck1, n1h, n2len, pitch):
    s = pl.program_id(1)

    @pl.when(s < ns)
    def _():
        for j in range(cn2):
            jr = cn2 - 1 - j
            m = s * cn2 + j
            g = jnp.concatenate([g_ref[m], gr_ref[n2len - 1 - m]], axis=1)
            x = jnp.concatenate([_load_minor(hf_ref, j), _load_minor(hr_ref, jr)], axis=0)
            r = jnp.dot(g, x.astype(BF16), preferred_element_type=F32)
            a_scr[pl.ds(m, 2 * n1h, stride=pitch), :] = r

    @pl.when(s >= ns)
    def _():
        a_re, a_im = _spectrum_views(a_scr, s - ns, ck1, n1h, pitch)
        for j in range(0, ck1, 2):
            rows = _spectrum_pair(a_re, a_im, j, n2len, pitch)
            u = jnp.dot(h2_ref[...], rows, preferred_element_type=F32).astype(o_ref.dtype)
            o_ref[j] = u[:, :LANES]
            o_ref[j + 1] = u[:, LANES:]


def _filter_spectrum(taps_f, taps_r, g_fwd, g_rev, h2, n2len):
    groups, ocb, _, _ = taps_f.shape
    oc = ocb * LANES
    n1h = groups * SUBLANES
    ns = PHASE_STEPS
    cn2, ck1 = n2len // ns, n1h // ns
    pitch = n2len + SUBLANES
    kern = functools.partial(_filter_fft_kernel, ns=ns, cn2=cn2, ck1=ck1, n1h=n1h,
                             n2len=n2len, pitch=pitch)
    return pl.pallas_call(
        kern,
        grid=(oc // LANES, 2 * ns),
        in_specs=[
            pl.BlockSpec((groups, None, cn2 * SUBLANES, LANES),
                         lambda c, s: (0, c, jnp.minimum(s, ns - 1), 0)),
            pl.BlockSpec((groups, None, cn2 * SUBLANES, LANES),
                         lambda c, s: (0, c, jnp.maximum(ns - 1 - s, 0), 0)),
            _resident(g_fwd.shape), _resident(g_rev.shape), _resident(h2.shape),
        ],
        out_specs=pl.BlockSpec((ck1, 2 * n2len, LANES), lambda c, s: (jnp.maximum(s - ns, 0), 0, c)),
        out_shape=jax.ShapeDtypeStruct((n1h, 2 * n2len, oc), BF16),
        scratch_shapes=[pltpu.VMEM((2 * n1h * pitch, LANES), F32)],
        compiler_params=pltpu.CompilerParams(
            dimension_semantics=("parallel", "arbitrary"), vmem_limit_bytes=VMEM_LIMIT),
        name="filter_spectrum",
    )(taps_f, taps_r, g_fwd, g_rev, h2)


def _long_conv_kernel(vf_ref, g_ref, kf_ref, h2_ref, h2i_ref, gi_ref, xm_ref, hb_ref, o_ref,
                      a_scr, keep_scr, *, steps, n1h, n2len, pitch):
    ns1, ns2, ns3 = steps
    cn2a, ck1, cn2 = n2len // ns1, n1h // ns2, n2len // ns3
    s = pl.program_id(2)

    @pl.when(s < ns1)
    def _():
        base = s * cn2a
        for j in range(cn2a):
            x = _load_minor(vf_ref, j)
            keep_scr[base + j] = x
            r = jnp.dot(g_ref[base + j], x.astype(BF16), preferred_element_type=F32)
            a_scr[pl.ds(base + j, 2 * n1h, stride=pitch), :] = r

    @pl.when((s >= ns1) & (s < ns1 + ns2))
    def _():
        a_re, a_im = _spectrum_views(a_scr, s - ns1, ck1, n1h, pitch)
        for j in range(0, ck1, 2):
            rows = _spectrum_pair(a_re, a_im, j, n2len, pitch)
            u = jnp.dot(h2_ref[...], rows, preferred_element_type=F32)
            ur, ui = u[:n2len], u[n2len:]
            kf = jnp.concatenate([kf_ref[j], kf_ref[j + 1]], axis=1).astype(F32)
            kr, ki = kf[:n2len], kf[n2len:]
            y = jnp.concatenate([ur * kr - ui * ki, ur * ki + ui * kr], axis=0)
            bb = jnp.dot(h2i_ref[...], y.astype(BF16), preferred_element_type=F32)
            for jj in range(2):
                lanes = slice(jj * LANES, (jj + 1) * LANES)
                a_re[(j + jj) * pitch:(j + jj) * pitch + n2len, :] = bb[:n2len, lanes]
                a_im[(j + jj) * pitch:(j + jj) * pitch + n2len, :] = bb[n2len:, lanes]

    @pl.when(s >= ns1 + ns2)
    def _():
        base = (s - ns1 - ns2) * cn2
        for j in range(cn2):
            rows = a_scr[pl.ds(base + j, 2 * n1h, stride=pitch), :]
            y = jnp.dot(gi_ref[base + j], rows.astype(BF16), preferred_element_type=F32)
            z = _load_minor(xm_ref, j) * (y + hb_ref[...] * keep_scr[base + j])
            o_ref[:, j * SUBLANES:(j + 1) * SUBLANES, :] = z.reshape(
                n1h // SUBLANES, SUBLANES, z.shape[1])


def _long_conv(u_perm, mult_perm, kf, hbias, order_idx, tables):
    g_fwd, g_inv, _, h2, h2i = tables
    bsz, groups, cbn, tile, _ = u_perm.shape
    n2len = tile // SUBLANES
    n1h = groups * SUBLANES
    steps = CONV_STEPS
    ns1, ns2, ns3 = steps
    pitch = n2len + SUBLANES

    def late(s):
        return jnp.clip(s - ns1 - ns2, 0, ns3 - 1)

    data_early = pl.BlockSpec((None, groups, None, (n2len // ns1) * SUBLANES, LANES),
                              lambda c, b, s: (b, 0, c, jnp.minimum(s, ns1 - 1), 0))
    data_late = pl.BlockSpec((None, groups, None, (n2len // ns3) * SUBLANES, LANES),
                             lambda c, b, s: (b, 0, c, late(s), 0))
    in_specs = [
        data_early,
        _resident(g_fwd.shape),
        pl.BlockSpec((n1h // ns2, 2 * n2len, LANES),
                     lambda c, b, s: (jnp.clip(s - ns1, 0, ns2 - 1), 0, order_idx * cbn + c)),
        _resident(h2.shape), _resident(h2i.shape), _resident(g_inv.shape),
        data_late,
        pl.BlockSpec((None, 1, LANES), lambda c, b, s: (order_idx, 0, c)),
    ]
    kern = functools.partial(_long_conv_kernel, steps=steps, n1h=n1h, n2len=n2len,
                             pitch=pitch)
    return pl.pallas_call(
        kern,
        grid=(cbn, bsz, sum(steps)),
        in_specs=in_specs,
        out_specs=data_late,
        out_shape=jax.ShapeDtypeStruct(u_perm.shape, F32),
        scratch_shapes=[pltpu.VMEM((2 * n1h * pitch, LANES), F32),
                        pltpu.VMEM((n2len, n1h, LANES), F32)],
        compiler_params=pltpu.CompilerParams(
            dimension_semantics=("parallel", "parallel", "arbitrary"),
            vmem_limit_bytes=VMEM_LIMIT),
        name="long_conv",
    )(u_perm, g_fwd, kf, h2, h2i, g_inv, mult_perm, hbias)


def _out_kernel(x_ref, yh_ref, yp_ref, mod_ref, w_ref, b_ref, g_ref, beta_ref, o_ref,
                *, n2len, alpha):
    cbn = yh_ref.shape[0]
    yh = jnp.concatenate(
        [jnp.concatenate([yh_ref[cb, pl.ds(j, n2len, stride=SUBLANES), :]
                          for cb in range(cbn)], axis=1)
         for j in range(SUBLANES)], axis=0)
    y = jnp.concatenate([yh.astype(BF16), yp_ref[...]], axis=1)
    acc = jnp.dot(y, w_ref[...], preferred_element_type=F32)
    h = alpha * x_ref[...] + mod_ref[2:3, :] * (acc + b_ref[...])
    mu = jnp.mean(h, axis=-1, keepdims=True)
    hc = h - mu
    var = jnp.mean(hc * hc, axis=-1, keepdims=True)
    o_ref[...] = hc * lax.rsqrt(var + LN_EPS) * g_ref[...] + beta_ref[...]


def _output_projection(x, yh_perm, yp, mod3, w_out, b_out, ln_g, ln_b, alpha):
    bsz, seq, d = x.shape
    _, _, cbn, tile, _ = yh_perm.shape
    ch = cbn * LANES
    n2len = tile // SUBLANES
    dp = yp.shape[2]
    full = lambda shape: pl.BlockSpec(shape, lambda b, i: (0,) * len(shape))
    kern = functools.partial(_out_kernel, n2len=n2len, alpha=alpha)
    return pl.pallas_call(
        kern,
        grid=(bsz, seq // tile),
        in_specs=[
            pl.BlockSpec((None, tile, d), lambda b, i: (b, i, 0)),
            pl.BlockSpec((None, None, cbn, tile, LANES), lambda b, i: (b, i, 0, 0, 0)),
            pl.BlockSpec((None, tile, dp), lambda b, i: (b, i, 0)),
            pl.BlockSpec((None, 3, d), lambda b, i: (b, 0, 0)),
            full((ch + dp, d)), full((1, d)), full((1, d)), full((1, d)),
        ],
        out_specs=pl.BlockSpec((None, tile, d), lambda b, i: (b, i, 0)),
        out_shape=jax.ShapeDtypeStruct(x.shape, x.dtype),
        compiler_params=pltpu.CompilerParams(
            dimension_semantics=("parallel", "arbitrary"), vmem_limit_bytes=VMEM_LIMIT),
        name="out_proj_deepnorm",
    )(x, yh_perm, yp, mod3, w_out.astype(BF16), b_out.reshape(1, d), ln_g.reshape(1, d),
      ln_b.reshape(1, d))


def _forward(x, c, w_ada, b_ada, w_in, b_in, conv_w, conv_b, filt_w1, filt_b1, filt_w_inner,
             filt_b_inner, filt_w_out, filt_freq, hyena_bias, pool_w, pool_scale, w_out,
             b_out, ln_g, ln_b, *, n2len):
    bsz, seq, d = x.shape
    depth = w_ada.shape[0]
    order, n_ch = hyena_bias.shape[1], hyena_bias.shape[2]
    alpha = (2.0 * depth) ** 0.25
    tables = _dft_tables(seq, n2len)
    h = x
    for layer in range(depth):
        mod3 = _modulation(c, w_ada[layer], b_ada[layer]).reshape(bsz, 3, d)
        v, x1, x2g, yp = _input_projection(
            h, mod3, w_in[layer], b_in[layer], conv_w[layer], conv_b[layer], pool_w[layer],
            pool_scale[layer], n2len)
        taps_f, taps_r = _filter_taps(
            seq, n2len, filt_w1[layer], filt_b1[layer], filt_w_inner[layer],
            filt_b_inner[layer], filt_w_out[layer], filt_freq[layer], n_ch, order)
        kf = _filter_spectrum(taps_f, taps_r, tables[0], tables[2], tables[3], n2len)
        hbias = hyena_bias[layer].reshape(order, 1, n_ch)
        z = v
        mults = (x1, x2g)
        for o in range(order):
            z = _long_conv(z, mults[o], kf, hbias, o, tables)
        h = _output_projection(h, z, yp, mod3, w_out[layer], b_out[layer], ln_g[layer],
                               ln_b[layer], alpha)
    return h


def kernel(x, c, w_ada, b_ada, w_in, b_in, conv_w, conv_b, filt_w1, filt_b1, filt_w_inner,
           filt_b_inner, filt_w_out, filt_freq, hyena_bias, pool_w, pool_scale, w_out, b_out,
           ln_g, ln_b):
    return _forward(x, c, w_ada, b_ada, w_in, b_in, conv_w, conv_b, filt_w1, filt_b1,
                    filt_w_inner, filt_b_inner, filt_w_out, filt_freq, hyena_bias, pool_w,
                    pool_scale, w_out, b_out, ln_g, ln_b, n2len=MINOR_LEN)
```

```python
import functools
import math

import jax
import jax.numpy as jnp
from jax import lax
from jax.experimental import pallas as pl
from jax.experimental.pallas import tpu as pltpu

F32 = jnp.float32
BF16 = jnp.bfloat16
HIGHEST = lax.Precision.HIGHEST

POOL_WINDOWS = (2, 4, 8, 16)
LN_EPS = 1e-5
DECAY_TARGET = 1e-2
FAST_DECAY_PCT = 0.3
SLOW_DECAY_PCT = 1.5

LANES = 128
SUBLANES = 8
HALO = 16
BAND_ROWS = 128
BAND_SPAN = 256
MINOR_LEN = 128
PHASE_STEPS = 4
CONV_STEPS = (4, 4, 4)
VMEM_LIMIT = 56 * 1024 * 1024
CONV_VMEM_LIMIT = 60 * 1024 * 1024


def _silu(x):
    return x * jax.nn.sigmoid(x)


def _dft_tables(seq, n2len):
    n1h = seq // n2len
    n_fft = 2 * seq
    n2i = jnp.arange(n2len, dtype=jnp.int32)
    k1 = jnp.arange(n1h, dtype=jnp.int32)
    odd = 2 * k1 + 1
    alpha = ((odd[:, None] * k1[None, :]) % (4 * n1h)).astype(F32) * (2.0 * math.pi / (4 * n1h))
    n2e = jnp.arange(n2len + 1, dtype=jnp.int32)
    beta = ((n2e[:, None] * odd[None, :]) % (2 * n_fft)).astype(F32) * (2.0 * math.pi / (2 * n_fft))
    ca, sa = jnp.cos(alpha)[None], jnp.sin(alpha)[None]
    cb, sb = jnp.cos(beta)[:, :, None], jnp.sin(beta)[:, :, None]
    cos_t = ca * cb[:-1] - sa * sb[:-1]
    sin_t = sa * cb[:-1] + ca * sb[:-1]
    g_fwd = jnp.concatenate([cos_t, -sin_t], axis=1).astype(BF16)
    cat, sat = jnp.cos(alpha).T[None], jnp.sin(alpha).T[None]
    cbt, sbt = jnp.cos(beta)[:-1, None, :], jnp.sin(beta)[:-1, None, :]
    g_inv = (2.0 / n_fft) * jnp.concatenate(
        [cat * cbt - sat * sbt, -(sat * cbt + cat * sbt)], axis=2)
    g_inv = g_inv.astype(BF16)
    cos_r = ca * cb[1:] - sa * sb[1:]
    sin_r = sa * cb[1:] + ca * sb[1:]
    g_rev = jnp.concatenate([cos_r, sin_r], axis=1).astype(BF16)

    m = (n2i[:, None] * n2i[None, :]) % n2len
    phi = m.astype(F32) * (2.0 * math.pi / n2len)
    cm, sm = jnp.cos(phi), jnp.sin(phi)
    h2 = jnp.block([[cm, sm], [-sm, cm]]).astype(BF16)
    h2i = jnp.block([[cm, -sm], [sm, cm]]).astype(BF16)
    return g_fwd, g_inv, g_rev, h2, h2i


def _packed_pos_features(seq, emb, tile):
    hl = LANES // 2
    half = tile // 2
    bands = (emb - 1) // 2
    row = jnp.arange((seq // tile) * half, dtype=jnp.int32)[:, None]
    lane = jnp.arange(LANES, dtype=jnp.int32)[None, :]
    pos = ((row // half) * tile + (lane // hl) * half + row % half).astype(F32)
    feat = lane % hl
    t = pos / (seq - 1)
    w = (2.0 * math.pi / seq) * pos
    f = jnp.linspace(1e-4, bands - 1, bands, dtype=F32)[(feat - 1) % bands]
    phase = jnp.where(feat > bands, 0.5 * math.pi, 0.0)
    return jnp.where(feat == 0, t,
                     jnp.where(feat <= 2 * bands, jnp.cos(f * w + phase), 0.0))


def _mod_kernel(c_ref, w_ref, b_ref, o_ref):
    s = _silu(c_ref[...])
    o_ref[...] = jnp.dot(s, w_ref[...], precision=HIGHEST,
                         preferred_element_type=F32) + b_ref[...]


def _modulation(c, w_ada, b_ada):
    bsz, d = c.shape
    n_out = w_ada.shape[1]
    return pl.pallas_call(
        _mod_kernel,
        grid=(n_out // d,),
        in_specs=[pl.BlockSpec((bsz, d), lambda j: (0, 0)),
                  pl.BlockSpec((d, d), lambda j: (0, j)),
                  pl.BlockSpec((1, d), lambda j: (0, j))],
        out_specs=pl.BlockSpec((bsz, d), lambda j: (0, j)),
        out_shape=jax.ShapeDtypeStruct((bsz, n_out), F32),
        name="adaln_mod",
    )(c, w_ada, b_ada.reshape(1, n_out))


def _store_permuted(o_ref, val, n2len, first=0):
    for cb in range(val.shape[1] // LANES):
        for j in range(val.shape[0] // n2len):
            o_ref[cb, pl.ds(first + j, n2len, stride=SUBLANES), :] = val[
                j * n2len:(j + 1) * n2len, cb * LANES:(cb + 1) * LANES]


def _load_minor(x_ref, j):
    blk = x_ref[:, j * SUBLANES:(j + 1) * SUBLANES, :]
    return blk.reshape(blk.shape[0] * SUBLANES, blk.shape[2])


def _band_window(b, tile):
    last = tile + 2 * HALO - BAND_SPAN
    start = min(b * BAND_ROWS, last)
    return start, 0 if start == b * BAND_ROWS else 1


def _band_matrices(tile):
    t = jnp.arange(BAND_ROWS, dtype=jnp.int32)[:, None]
    k = jnp.arange(BAND_SPAN, dtype=jnp.int32)[None, :]
    mats = []
    for win in POOL_WINDOWS:
        half = win // 2
        per_variant = []
        for b in (0, tile // BAND_ROWS - 1):
            start, _ = _band_window(b, tile)
            rel = k + start - HALO - (b * BAND_ROWS + t)
            per_variant.append(((rel >= -half) & (rel < half)).astype(BF16))
        mats.append(jnp.stack(per_variant))
    return jnp.stack(mats)


def _proj_kernel(xp_ref, xc_ref, xn_ref, mod_ref, w_ref, b_ref, cw_ref, cb_ref, pw_ref,
                 ps_ref, band_ref, v_ref, x1_ref, x2_ref, yp_ref, p_scr, q_scr,
                 *, tile, n2len, ch, seq):
    i = pl.program_id(1)
    nt = pl.num_programs(1)
    shift = mod_ref[0:1, :]
    scale1 = 1.0 + mod_ref[1:2, :]
    xe = jnp.concatenate([xp_ref[...], xc_ref[...], xn_ref[...]], axis=0)
    ue = (xe * scale1 + shift).astype(BF16)
    uc = ue[HALO:HALO + tile, :]

    def zero_outside_sequence(p):
        return jnp.concatenate([jnp.where(i > 0, p[:HALO], 0.0), p[HALO:HALO + tile],
                                jnp.where(i < nt - 1, p[HALO + tile:], 0.0)], axis=0)

    hg = jnp.dot(uc, w_ref[:, 3 * ch:4 * ch], preferred_element_type=F32) + b_ref[:, 3 * ch:4 * ch]
    gate = _silu(hg)

    for k, o_ref in enumerate((v_ref, x1_ref, x2_ref)):
        cols = slice(k * ch, (k + 1) * ch)
        p = jnp.dot(ue, w_ref[:, cols], preferred_element_type=F32) + b_ref[:, cols]
        p_scr[...] = zero_outside_sequence(p)
        s = (cb_ref[:, cols]
             + cw_ref[0:1, cols] * p_scr[HALO - 1:HALO - 1 + tile, :]
             + cw_ref[1:2, cols] * p_scr[HALO:HALO + tile, :]
             + cw_ref[2:3, cols] * p_scr[HALO + 1:HALO + 1 + tile, :])
        if k == 2:
            s = s * gate
        _store_permuted(o_ref, s, n2len)

    dp = ps_ref.shape[1]
    c0 = 4 * ch
    pin = jnp.dot(ue, w_ref[:, c0:c0 + dp], preferred_element_type=F32) + b_ref[:, c0:c0 + dp]
    pin = zero_outside_sequence(pin)
    q_scr[...] = pin.astype(BF16)
    pin_c = pin[HALO:HALO + tile, :]
    pgate = jnp.dot(uc, w_ref[:, c0 + dp:c0 + 2 * dp], preferred_element_type=F32) + b_ref[:, c0 + dp:c0 + 2 * dp]
    pos = i * tile + lax.broadcasted_iota(jnp.int32, (tile, 1), 0)
    pg = dp // len(POOL_WINDOWS)
    groups = []
    for g, win in enumerate(POOL_WINDOWS):
        lanes = slice(g * pg, (g + 1) * pg)
        half = win // 2
        sums = []
        for b in range(tile // BAND_ROWS):
            start, variant = _band_window(b, tile)
            sums.append(jnp.dot(band_ref[g, variant], q_scr[start:start + BAND_SPAN, lanes],
                                preferred_element_type=F32))
        acc = jnp.concatenate(sums, axis=0)
        cnt = (jnp.minimum(pos + half, seq) - jnp.maximum(pos - half, 0)).astype(F32)
        diff = acc / cnt - pin_c[:, lanes]
        groups.append(jnp.dot(diff.astype(BF16), pw_ref[g], preferred_element_type=F32))
    yp = jnp.concatenate(groups, axis=1) * ps_ref[...] * _silu(pgate)
    yp_ref[...] = yp.astype(yp_ref.dtype)


def _input_projection(x, mod3, w_in, b_in, conv_w, conv_b, pool_w, pool_scale, n2len):
    bsz, seq, d = x.shape
    ch = conv_w.shape[1] // 3
    dp = pool_scale.shape[0]
    n1h = seq // n2len
    tile = SUBLANES * n2len
    nt = seq // tile
    hb = tile // HALO
    n_proj = w_in.shape[1]
    cbn = ch // LANES
    assert max(POOL_WINDOWS) // 2 <= HALO and dp == ch and tile % BAND_ROWS == 0
    band = _band_matrices(tile)
    perm =jax.ShapeDtypeStruct((bsz, n1h // SUBLANES, cbn, tile, LANES), F32)
    perm_spec = pl.BlockSpec((None, None, cbn, tile, LANES), lambda b, i: (b, i, 0, 0, 0))
    full = lambda shape: pl.BlockSpec(shape, lambda b, i: (0,) * len(shape))
    kern = functools.partial(_proj_kernel, tile=tile, n2len=n2len, ch=ch, seq=seq)
    return pl.pallas_call(
        kern,
        grid=(bsz, nt),
        in_specs=[
            pl.BlockSpec((None, HALO, d), lambda b, i: (b, jnp.maximum(i * hb - 1, 0), 0)),
            pl.BlockSpec((None, tile, d), lambda b, i: (b, i, 0)),
            pl.BlockSpec((None, HALO, d), lambda b, i: (b, jnp.minimum((i + 1) * hb, seq // HALO - 1), 0)),
            pl.BlockSpec((None, 3, d), lambda b, i: (b, 0, 0)),
            full((d, n_proj)), full((1, n_proj)), full((3, 3 * ch)), full((1, 3 * ch)),
            full(pool_w.shape), full((1, dp)), full(band.shape),
        ],
        out_specs=[perm_spec, perm_spec, perm_spec,
                   pl.BlockSpec((None, tile, dp), lambda b, i: (b, i, 0))],
        out_shape=[perm, perm, perm, jax.ShapeDtypeStruct((bsz, seq, dp), BF16)],
        scratch_shapes=[pltpu.VMEM((tile + 2 * HALO, ch), F32),
                        pltpu.VMEM((tile + 2 * HALO, dp), BF16)],
        compiler_params=pltpu.CompilerParams(
            dimension_semantics=("parallel", "arbitrary"), vmem_limit_bytes=VMEM_LIMIT),
        name="in_proj_conv_pool",
    )(x, x, x, mod3, w_in.astype(BF16), b_in.reshape(1, n_proj), conv_w,
      conv_b.reshape(1, 3 * ch), pool_w.astype(BF16), pool_scale.reshape(1, dp), band)


def _filter_mlp_kernel(z_ref, w1_ref, b1_ref, wi_ref, bi_ref, fr_ref, wo_ref, ad_ref,
                       of_ref, or_ref, *, n2len):
    z = z_ref[...]
    fr = fr_ref[...]
    h = jnp.sin(fr * (jnp.dot(z.astype(BF16), w1_ref[...],
                              preferred_element_type=F32) + b1_ref[...]))
    for l in range(wi_ref.shape[0]):
        h = jnp.sin(fr * (jnp.dot(h.astype(BF16), wi_ref[l],
                                  preferred_element_type=F32) + bi_ref[l:l + 1, :]))
    hb = h.astype(BF16)
    half_lanes = LANES // 2
    n1_half = z.shape[0] // n2len
    for half in range(2):
        t = z[:, half * half_lanes:half * half_lanes + 1]
        decay = jnp.exp(-t * ad_ref[...])
        for d, o_ref in enumerate((of_ref, or_ref)):
            k = jnp.dot(hb, wo_ref[d, half], preferred_element_type=F32)
            _store_permuted(o_ref, k * decay, n2len, first=half * n1_half)


def _block_diag2(w):
    zero = jnp.zeros_like(w)
    return jnp.concatenate([jnp.concatenate([w, zero], axis=-1),
                            jnp.concatenate([zero, w], axis=-1)], axis=-2)


def _filter_taps(seq, n2len, w1, b1, w_inner, b_inner, w_out, freq, n_ch, order):
    emb, hid = w1.shape
    n_inner = w_inner.shape[0]
    oc = order * n_ch
    hl = LANES // 2
    assert emb <= hl and hid <= hl
    tile = SUBLANES * n2len
    half = tile // 2
    groups = seq // tile
    zp = _packed_pos_features(seq, emb, tile)
    ph = hl - hid
    w1d = _block_diag2(jnp.pad(w1, ((0, hl - emb), (0, ph))))
    b1d = jnp.tile(jnp.pad(b1, (0, ph)), 2).reshape(1, LANES)
    wid = _block_diag2(jnp.pad(w_inner, ((0, 0), (0, ph), (0, ph))))
    bid = jnp.tile(jnp.pad(b_inner, ((0, 0), (0, ph))), (1, 2))
    frd = jnp.tile(jnp.pad(freq, (0, ph), constant_values=1.0), 2).reshape(1, LANES)
    wo = jnp.transpose(w_out.reshape(hid, order, 2, n_ch), (2, 0, 1, 3)).reshape(2, hid, oc)
    wo = jnp.pad(wo, ((0, 0), (0, ph), (0, 0)))
    zero = jnp.zeros_like(wo)
    wo4 = jnp.stack([jnp.concatenate([wo, zero], axis=1),
                     jnp.concatenate([zero, wo], axis=1)], axis=1)
    min_decay = math.log(DECAY_TARGET) / SLOW_DECAY_PCT
    max_decay = math.log(DECAY_TARGET) / FAST_DECAY_PCT
    absdelta = jnp.abs(jnp.linspace(min_decay, max_decay, n_ch, dtype=F32))
    absdelta = jnp.tile(absdelta, order).reshape(1, oc)

    full = lambda shape: pl.BlockSpec(shape, lambda i: (0,) * len(shape))
    kern = functools.partial(_filter_mlp_kernel, n2len=n2len)
    taps = jax.ShapeDtypeStruct((groups, oc // LANES, tile, LANES), F32)
    taps_spec = pl.BlockSpec((None, oc // LANES, tile, LANES), lambda i: (i, 0, 0, 0))
    return pl.pallas_call(
        kern,
        grid=(groups,),
        in_specs=[
            pl.BlockSpec((half, LANES), lambda i: (i, 0)),
            full((LANES, LANES)), full((1, LANES)), full((n_inner, LANES, LANES)),
            full((n_inner, LANES)), full((1, LANES)), full((2, 2, LANES, oc)), full((1, oc)),
        ],
        out_specs=[taps_spec, taps_spec],
        out_shape=[taps, taps],
        compiler_params=pltpu.CompilerParams(
            dimension_semantics=("arbitrary",), vmem_limit_bytes=VMEM_LIMIT),
        name="filter_mlp",
    )(zp, w1d.astype(BF16), b1d, wid.astype(BF16), bid, frd, wo4.astype(BF16), absdelta)


def _resident(shape):
    return pl.BlockSpec(shape, lambda *_: (0,) * len(shape), pipeline_mode=pl.Buffered(1))


def _spectrum_views(a_scr, chunk, ck1, n1h, pitch):
    re0 = pl.multiple_of(chunk * (ck1 * pitch), SUBLANES)
    im0 = pl.multiple_of(chunk * (ck1 * pitch) + n1h * pitch, SUBLANES)
    return a_scr.at[pl.ds(re0, ck1 * pitch)], a_scr.at[pl.ds(im0, ck1 * pitch)]


def _spectrum_pair(a_re, a_im, j, n2len, pitch):
    def one(jj):
        return jnp.concatenate([a_re[jj * pitch:jj * pitch + n2len, :],
                                a_im[jj * pitch:jj * pitch + n2len, :]], axis=0)
    return jnp.concatenate([one(j), one(j + 1)], axis=1).astype(BF16)


def _filter_fft_kernel(hf_ref, hr_ref, g_ref, gr_ref, h2_ref, o_ref, a_scr,
                       *, ns, cn2, ck1, n1h, n2len, pitch):
    s = pl.program_id(1)

    @pl.when(s < ns)
    def _():
        for j in range(cn2):
            jr = cn2 - 1 - j
            m = s * cn2 + j
            g = jnp.concatenate([g_ref[m], gr_ref[n2len - 1 - m]], axis=1)
            x = jnp.concatenate([_load_minor(hf_ref, j), _load_minor(hr_ref, jr)], axis=0)
            r = jnp.dot(g, x.astype(BF16), preferred_element_type=F32)
            a_scr[pl.ds(m, 2 * n1h, stride=pitch), :] = r

    @pl.when(s >= ns)
    def _():
        a_re, a_im = _spectrum_views(a_scr, s - ns, ck1, n1h, pitch)
        for j in range(0, ck1, 2):
            rows = _spectrum_pair(a_re, a_im, j, n2len, pitch)
            u = jnp.dot(h2_ref[...], rows, preferred_element_type=F32).astype(o_ref.dtype)
            o_ref[j] = u[:, :LANES]
            o_ref[j + 1] = u[:, LANES:]


def _filter_spectrum(taps_f, taps_r, g_fwd, g_rev, h2, n2len):
    groups, ocb, _, _ = taps_f.shape
    oc = ocb * LANES
    n1h = groups * SUBLANES
    ns = PHASE_STEPS
    cn2, ck1 = n2len // ns, n1h // ns
    pitch = n2len + SUBLANES
    kern = functools.partial(_filter_fft_kernel, ns=ns, cn2=cn2, ck1=ck1, n1h=n1h,
                             n2len=n2len, pitch=pitch)
    return pl.pallas_call(
        kern,
        grid=(oc // LANES, 2 * ns),
        in_specs=[
            pl.BlockSpec((groups, None, cn2 * SUBLANES, LANES),
                         lambda c, s: (0, c, jnp.minimum(s, ns - 1), 0)),
            pl.BlockSpec((groups, None, cn2 * SUBLANES, LANES),
                         lambda c, s: (0, c, jnp.maximum(ns - 1 - s, 0), 0)),
            _resident(g_fwd.shape), _resident(g_rev.shape), _resident(h2.shape),
        ],
        out_specs=pl.BlockSpec((ck1, 2 * n2len, LANES), lambda c, s: (jnp.maximum(s - ns, 0), 0, c)),
        out_shape=jax.ShapeDtypeStruct((n1h, 2 * n2len, oc), BF16),
        scratch_shapes=[pltpu.VMEM((2 * n1h * pitch, LANES), F32)],
        compiler_params=pltpu.CompilerParams(
            dimension_semantics=("parallel", "arbitrary"), vmem_limit_bytes=VMEM_LIMIT),
        name="filter_spectrum",
    )(taps_f, taps_r, g_fwd, g_rev, h2)


def _long_conv_kernel(vf_ref, g_ref, kf_ref, h2_ref, h2i_ref, gi_ref, xm_ref, hb_ref, o_ref,
                      a_scr, keep_scr, *, steps, n1h, n2len, pitch):
    ns1, ns2, ns3 = steps
    cn2a, ck1, cn2 = n2len // ns1, n1h // ns2, n2len // ns3
    s = pl.program_id(2)

    @pl.when(s < ns1)
    def _():
        base = s * cn2a
        for j in range(cn2a):
            x = _load_minor(vf_ref, j)
            keep_scr[base + j] = x
            r = jnp.dot(g_ref[base + j], x.astype(BF16), preferred_element_type=F32)
            a_scr[pl.ds(base + j, 2 * n1h, stride=pitch), :] = r

    @pl.when((s >= ns1) & (s < ns1 + ns2))
    def _():
        a_re, a_im = _spectrum_views(a_scr, s - ns1, ck1, n1h, pitch)
        for j in range(0, ck1, 2):
            rows = _spectrum_pair(a_re, a_im, j, n2len, pitch)
            u = jnp.dot(h2_ref[...], rows, preferred_element_type=F32)
            ur, ui = u[:n2len], u[n2len:]
            kf = jnp.concatenate([kf_ref[j], kf_ref[j + 1]], axis=1).astype(F32)
            kr, ki = kf[:n2len], kf[n2len:]
            y = jnp.concatenate([ur * kr - ui * ki, ur * ki + ui * kr], axis=0)
            bb = jnp.dot(h2i_ref[...], y.astype(BF16), preferred_element_type=F32)
            for jj in range(2):
                lanes = slice(jj * LANES, (jj + 1) * LANES)
                a_re[(j + jj) * pitch:(j + jj) * pitch + n2len, :] = bb[:n2len, lanes]
                a_im[(j + jj) * pitch:(j + jj) * pitch + n2len, :] = bb[n2len:, lanes]

    @pl.when(s >= ns1 + ns2)
    def _():
        base = (s - ns1 - ns2) * cn2
        for j in range(cn2):
            rows = a_scr[pl.ds(base + j, 2 * n1h, stride=pitch), :]
            y = jnp.dot(gi_ref[base + j], rows.astype(BF16), preferred_element_type=F32)
            z = _load_minor(xm_ref, j) * (y + hb_ref[...] * keep_scr[base + j])
            o_ref[:, j * SUBLANES:(j + 1) * SUBLANES, :] = z.reshape(
                n1h // SUBLANES, SUBLANES, z.shape[1])


def _long_conv(u_perm, mult_perm, kf, hbias, order_idx, tables):
    g_fwd, g_inv, _, h2, h2i = tables
    bsz, groups, cbn, tile, _ = u_perm.shape
    n2len = tile // SUBLANES
    n1h = groups * SUBLANES
    steps = CONV_STEPS
    ns1, ns2, ns3 = steps
    pitch = n2len + SUBLANES

    def late(s):
        return jnp.clip(s - ns1 - ns2, 0, ns3 - 1)

    data_early = pl.BlockSpec((None, groups, None, (n2len // ns1) * SUBLANES, LANES),
                              lambda c, b, s: (b, 0, c, jnp.minimum(s, ns1 - 1), 0))
    data_late = pl.BlockSpec((None, groups, None, (n2len // ns3) * SUBLANES, LANES),
                             lambda c, b, s: (b, 0, c, late(s), 0))
    in_specs = [
        data_early,
        _resident(g_fwd.shape),
        pl.BlockSpec((n1h // ns2, 2 * n2len, LANES),
                     lambda c, b, s: (jnp.clip(s - ns1, 0, ns2 - 1), 0, order_idx * cbn + c)),
        _resident(h2.shape), _resident(h2i.shape), _resident(g_inv.shape),
        data_late,
        pl.BlockSpec((None, 1, LANES), lambda c, b, s: (order_idx, 0, c)),
    ]
    kern = functools.partial(_long_conv_kernel, steps=steps, n1h=n1h, n2len=n2len,
                             pitch=pitch)
    return pl.pallas_call(
        kern,
        grid=(cbn, bsz, sum(steps)),
        in_specs=in_specs,
        out_specs=data_late,
        out_shape=jax.ShapeDtypeStruct(u_perm.shape, F32),
        scratch_shapes=[pltpu.VMEM((2 * n1h * pitch, LANES), F32),
                        pltpu.VMEM((n2len, n1h, LANES), F32)],
        compiler_params=pltpu.CompilerParams(
            dimension_semantics=("parallel", "parallel", "arbitrary"),
            vmem_limit_bytes=CONV_VMEM_LIMIT),
        name="long_conv",
    )(u_perm, g_fwd, kf, h2, h2i, g_inv, mult_perm, hbias)


def _out_kernel(x_ref, yh_ref, yp_ref, mod_ref, w_ref, b_ref, g_ref, beta_ref, o_ref,
                *, n2len, alpha):
    cbn = yh_ref.shape[0]
    yh = jnp.concatenate(
        [jnp.concatenate([yh_ref[cb, pl.ds(j, n2len, stride=SUBLANES), :]
                          for cb in range(cbn)], axis=1)
         for j in range(SUBLANES)], axis=0)
    y = jnp.concatenate([yh.astype(BF16), yp_ref[...]], axis=1)
    acc = jnp.dot(y, w_ref[...], preferred_element_type=F32)
    h = alpha * x_ref[...] + mod_ref[2:3, :] * (acc + b_ref[...])
    mu = jnp.mean(h, axis=-1, keepdims=True)
    hc = h - mu
    var = jnp.mean(hc * hc, axis=-1, keepdims=True)
    o_ref[...] = hc * lax.rsqrt(var + LN_EPS) * g_ref[...] + beta_ref[...]


def _output_projection(x, yh_perm, yp, mod3, w_out, b_out, ln_g, ln_b, alpha):
    bsz, seq, d = x.shape
    _, _, cbn, tile, _ = yh_perm.shape
    ch = cbn * LANES
    n2len = tile // SUBLANES
    dp = yp.shape[2]
    full = lambda shape: pl.BlockSpec(shape, lambda b, i: (0,) * len(shape))
    kern = functools.partial(_out_kernel, n2len=n2len, alpha=alpha)
    return pl.pallas_call(
        kern,
        grid=(bsz, seq // tile),
        in_specs=[
            pl.BlockSpec((None, tile, d), lambda b, i: (b, i, 0)),
            pl.BlockSpec((None, None, cbn, tile, LANES), lambda b, i: (b, i, 0, 0, 0)),
            pl.BlockSpec((None, tile, dp), lambda b, i: (b, i, 0)),
            pl.BlockSpec((None, 3, d), lambda b, i: (b, 0, 0)),
            full((ch + dp, d)), full((1, d)), full((1, d)), full((1, d)),
        ],
        out_specs=pl.BlockSpec((None, tile, d), lambda b, i: (b, i, 0)),
        out_shape=jax.ShapeDtypeStruct(x.shape, x.dtype),
        compiler_params=pltpu.CompilerParams(
            dimension_semantics=("parallel", "arbitrary"), vmem_limit_bytes=VMEM_LIMIT),
        name="out_proj_deepnorm",
    )(x, yh_perm, yp, mod3, w_out.astype(BF16), b_out.reshape(1, d), ln_g.reshape(1, d),
      ln_b.reshape(1, d))


def _forward(x, c, w_ada, b_ada, w_in, b_in, conv_w, conv_b, filt_w1, filt_b1, filt_w_inner,
             filt_b_inner, filt_w_out, filt_freq, hyena_bias, pool_w, pool_scale, w_out,
             b_out, ln_g, ln_b, *, n2len):
    bsz, seq, d = x.shape
    depth = w_ada.shape[0]
    order, n_ch = hyena_bias.shape[1], hyena_bias.shape[2]
    alpha = (2.0 * depth) ** 0.25
    tables = _dft_tables(seq, n2len)
    h = x
    for layer in range(depth):
        mod3 = _modulation(c, w_ada[layer], b_ada[layer]).reshape(bsz, 3, d)
        v, x1, x2g, yp = _input_projection(
            h, mod3, w_in[layer], b_in[layer], conv_w[layer], conv_b[layer], pool_w[layer],
            pool_scale[layer], n2len)
        taps_f, taps_r = _filter_taps(
            seq, n2len, filt_w1[layer], filt_b1[layer], filt_w_inner[layer],
            filt_b_inner[layer], filt_w_out[layer], filt_freq[layer], n_ch, order)
        kf = _filter_spectrum(taps_f, taps_r, tables[0], tables[2], tables[3], n2len)
        hbias = hyena_bias[layer].reshape(order, 1, n_ch)
        z = v
        mults = (x1, x2g)
        for o in range(order):
            z = _long_conv(z, mults[o], kf, hbias, o, tables)
        h = _output_projection(h, z, yp, mod3, w_out[layer], b_out[layer], ln_g[layer],
                               ln_b[layer], alpha)
    return h


def kernel(x, c, w_ada, b_ada, w_in, b_in, conv_w, conv_b, filt_w1, filt_b1, filt_w_inner,
           filt_b_inner, filt_w_out, filt_freq, hyena_bias, pool_w, pool_scale, w_out, b_out,
           ln_g, ln_b):
    return _forward(x, c, w_ada, b_ada, w_in, b_in, conv_w, conv_b, filt_w1, filt_b1,
                    filt_w_inner, filt_b_inner, filt_w_out, filt_freq, hyena_bias, pool_w,
                    pool_scale, w_out, b_out, ln_g, ln_b, n2len=MINOR_LEN)
```

```python
import functools
import math

import jax
import jax.numpy as jnp
from jax import lax
from jax.experimental import pallas as pl
from jax.experimental.pallas import tpu as pltpu

F32 = jnp.float32
BF16 = jnp.bfloat16
HIGHEST = lax.Precision.HIGHEST

POOL_WINDOWS = (2, 4, 8, 16)
LN_EPS = 1e-5
DECAY_TARGET = 1e-2
FAST_DECAY_PCT = 0.3
SLOW_DECAY_PCT = 1.5

LANES = 128
SUBLANES = 8
HALO = 16
OUT_SLABS = 2
BAND_ROWS = 128
BAND_SPAN = 256
MINOR_LEN = 128
PHASE_STEPS = 4
CONV_STEPS = (2, 2, 4)
VMEM_LIMIT = 56 * 1024 * 1024
CONV_VMEM_LIMIT = 60 * 1024 * 1024


def _silu(x):
    return x * jax.nn.sigmoid(x)


def _dft_tables(seq, n2len):
    n1h = seq // n2len
    n_fft = 2 * seq
    n2i = jnp.arange(n2len, dtype=jnp.int32)
    k1 = jnp.arange(n1h, dtype=jnp.int32)
    odd = 2 * k1 + 1
    alpha = ((odd[:, None] * k1[None, :]) % (4 * n1h)).astype(F32) * (2.0 * math.pi / (4 * n1h))
    n2e = jnp.arange(n2len + 1, dtype=jnp.int32)
    beta = ((n2e[:, None] * odd[None, :]) % (2 * n_fft)).astype(F32) * (2.0 * math.pi / (2 * n_fft))
    ca, sa = jnp.cos(alpha)[None], jnp.sin(alpha)[None]
    cb, sb = jnp.cos(beta)[:, :, None], jnp.sin(beta)[:, :, None]
    cos_t = ca * cb[:-1] - sa * sb[:-1]
    sin_t = sa * cb[:-1] + ca * sb[:-1]
    g_fwd = jnp.concatenate([cos_t, -sin_t], axis=1).astype(BF16)
    cos_r = ca * cb[1:] - sa * sb[1:]
    sin_r = sa * cb[1:] + ca * sb[1:]
    g_rev = jnp.concatenate([cos_r, sin_r], axis=1).astype(BF16)

    m = (n2i[:, None] * n2i[None, :]) % n2len
    phi = m.astype(F32) * (2.0 * math.pi / n2len)
    cm, sm = jnp.cos(phi), jnp.sin(phi)
    h2 = jnp.block([[cm, sm], [-sm, cm]]).astype(BF16)
    h2i = jnp.block([[cm, -sm], [sm, cm]]).astype(BF16)
    return g_fwd, g_rev, h2, h2i


def _packed_pos_features(seq, emb, tile):
    hl = LANES // 2
    half = tile // 2
    bands = (emb - 1) // 2
    row = jnp.arange((seq // tile) * half, dtype=jnp.int32)[:, None]
    lane = jnp.arange(LANES, dtype=jnp.int32)[None, :]
    pos = ((row // half) * tile + (lane // hl) * half + row % half).astype(F32)
    feat = lane % hl
    t = pos / (seq - 1)
    w = (2.0 * math.pi / seq) * pos
    f = jnp.linspace(1e-4, bands - 1, bands, dtype=F32)[(feat - 1) % bands]
    phase = jnp.where(feat > bands, 0.5 * math.pi, 0.0)
    return jnp.where(feat == 0, t,
                     jnp.where(feat <= 2 * bands, jnp.cos(f * w + phase), 0.0))


def _mod_kernel(c_ref, w_ref, b_ref, o_ref):
    s = _silu(c_ref[...])
    o_ref[...] = jnp.dot(s, w_ref[...], precision=HIGHEST,
                         preferred_element_type=F32) + b_ref[...]


def _modulation(c, w_ada, b_ada):
    bsz, d = c.shape
    n_out = w_ada.shape[1]
    return pl.pallas_call(
        _mod_kernel,
        grid=(n_out // d,),
        in_specs=[pl.BlockSpec((bsz, d), lambda j: (0, 0)),
                  pl.BlockSpec((d, d), lambda j: (0, j)),
                  pl.BlockSpec((1, d), lambda j: (0, j))],
        out_specs=pl.BlockSpec((bsz, d), lambda j: (0, j)),
        out_shape=jax.ShapeDtypeStruct((bsz, n_out), F32),
        name="adaln_mod",
    )(c, w_ada, b_ada.reshape(1, n_out))


def _store_permuted(o_ref, val, n2len, first=0):
    for cb in range(val.shape[1] // LANES):
        for j in range(val.shape[0] // n2len):
            o_ref[cb, pl.ds(first + j, n2len, stride=SUBLANES), :] = val[
                j * n2len:(j + 1) * n2len, cb * LANES:(cb + 1) * LANES]


def _load_minor(x_ref, j):
    blk = x_ref[:, j * SUBLANES:(j + 1) * SUBLANES, :]
    return blk.reshape(blk.shape[0] * SUBLANES, blk.shape[2])


def _band_window(b, tile):
    last = tile + 2 * HALO - BAND_SPAN
    start = min(b * BAND_ROWS, last)
    return start, 0 if start == b * BAND_ROWS else 1


def _band_matrices(tile):
    t = jnp.arange(BAND_ROWS, dtype=jnp.int32)[:, None]
    k = jnp.arange(BAND_SPAN, dtype=jnp.int32)[None, :]
    mats = []
    for win in POOL_WINDOWS:
        half = win // 2
        per_variant = []
        for b in (0, tile // BAND_ROWS - 1):
            start, _ = _band_window(b, tile)
            rel = k + start - HALO - (b * BAND_ROWS + t)
            per_variant.append(((rel >= -half) & (rel < half)).astype(BF16))
        mats.append(jnp.stack(per_variant))
    return jnp.stack(mats)


def _proj_kernel(xp_ref, xc_ref, xn_ref, mod_ref, w_ref, b_ref, cw_ref, cb_ref, pw_ref,
                 ps_ref, band_ref, v_ref, x1_ref, x2_ref, yp_ref, p_scr, q_scr,
                 *, tile, n2len, ch, seq):
    i = pl.program_id(1)
    nt = pl.num_programs(1)
    shift = mod_ref[0:1, :]
    scale1 = 1.0 + mod_ref[1:2, :]
    xe = jnp.concatenate([xp_ref[...], xc_ref[...], xn_ref[...]], axis=0)
    ue = (xe * scale1 + shift).astype(BF16)
    uc = ue[HALO:HALO + tile, :]

    def zero_outside_sequence(p):
        return jnp.concatenate([jnp.where(i > 0, p[:HALO], 0.0), p[HALO:HALO + tile],
                                jnp.where(i < nt - 1, p[HALO + tile:], 0.0)], axis=0)

    hg = jnp.dot(uc, w_ref[:, 3 * ch:4 * ch], preferred_element_type=F32) + b_ref[:, 3 * ch:4 * ch]
    gate = _silu(hg)

    for k, o_ref in enumerate((v_ref, x1_ref, x2_ref)):
        cols = slice(k * ch, (k + 1) * ch)
        p = jnp.dot(ue, w_ref[:, cols], preferred_element_type=F32) + b_ref[:, cols]
        p_scr[...] = zero_outside_sequence(p)
        s = (cb_ref[:, cols]
             + cw_ref[0:1, cols] * p_scr[HALO - 1:HALO - 1 + tile, :]
             + cw_ref[1:2, cols] * p_scr[HALO:HALO + tile, :]
             + cw_ref[2:3, cols] * p_scr[HALO + 1:HALO + 1 + tile, :])
        if k == 2:
            s = s * gate
        _store_permuted(o_ref, s, n2len)

    dp = ps_ref.shape[1]
    c0 = 4 * ch
    pin = jnp.dot(ue, w_ref[:, c0:c0 + dp], preferred_element_type=F32) + b_ref[:, c0:c0 + dp]
    pin = zero_outside_sequence(pin)
    q_scr[...] = pin.astype(BF16)
    pin_c = pin[HALO:HALO + tile, :]
    pgate = jnp.dot(uc, w_ref[:, c0 + dp:c0 + 2 * dp], preferred_element_type=F32) + b_ref[:, c0 + dp:c0 + 2 * dp]
    pos = i * tile + lax.broadcasted_iota(jnp.int32, (tile, 1), 0)
    pg = dp // len(POOL_WINDOWS)
    groups = []
    for g, win in enumerate(POOL_WINDOWS):
        lanes = slice(g * pg, (g + 1) * pg)
        half = win // 2
        sums = []
        for b in range(tile // BAND_ROWS):
            start, variant = _band_window(b, tile)
            sums.append(jnp.dot(band_ref[g, variant], q_scr[start:start + BAND_SPAN, lanes],
                                preferred_element_type=F32))
        acc = jnp.concatenate(sums, axis=0)
        cnt = (jnp.minimum(pos + half, seq) - jnp.maximum(pos - half, 0)).astype(F32)
        diff = acc / cnt - pin_c[:, lanes]
        groups.append(jnp.dot(diff.astype(BF16), pw_ref[g], preferred_element_type=F32))
    yp = jnp.concatenate(groups, axis=1) * ps_ref[...] * _silu(pgate)
    yp_ref[...] = yp.astype(yp_ref.dtype)


def _input_projection(x, mod3, w_in, b_in, conv_w, conv_b, pool_w, pool_scale, n2len):
    bsz, seq, d = x.shape
    ch = conv_w.shape[1] // 3
    dp = pool_scale.shape[0]
    n1h = seq // n2len
    tile = SUBLANES * n2len
    nt = seq // tile
    hb = tile // HALO
    n_proj = w_in.shape[1]
    cbn = ch // LANES
    assert max(POOL_WINDOWS) // 2 <= HALO and dp == ch and tile % BAND_ROWS == 0
    band = _band_matrices(tile)
    perm =jax.ShapeDtypeStruct((bsz, n1h // SUBLANES, cbn, tile, LANES), F32)
    perm_spec = pl.BlockSpec((None, None, cbn, tile, LANES), lambda b, i: (b, i, 0, 0, 0))
    full = lambda shape: pl.BlockSpec(shape, lambda b, i: (0,) * len(shape))
    kern = functools.partial(_proj_kernel, tile=tile, n2len=n2len, ch=ch, seq=seq)
    return pl.pallas_call(
        kern,
        grid=(bsz, nt),
        in_specs=[
            pl.BlockSpec((None, HALO, d), lambda b, i: (b, jnp.maximum(i * hb - 1, 0), 0)),
            pl.BlockSpec((None, tile, d), lambda b, i: (b, i, 0)),
            pl.BlockSpec((None, HALO, d), lambda b, i: (b, jnp.minimum((i + 1) * hb, seq // HALO - 1), 0)),
            pl.BlockSpec((None, 3, d), lambda b, i: (b, 0, 0)),
            full((d, n_proj)), full((1, n_proj)), full((3, 3 * ch)), full((1, 3 * ch)),
            full(pool_w.shape), full((1, dp)), full(band.shape),
        ],
        out_specs=[perm_spec, perm_spec, perm_spec,
                   pl.BlockSpec((None, tile, dp), lambda b, i: (b, i, 0))],
        out_shape=[perm, perm, perm, jax.ShapeDtypeStruct((bsz, seq, dp), BF16)],
        scratch_shapes=[pltpu.VMEM((tile + 2 * HALO, ch), F32),
                        pltpu.VMEM((tile + 2 * HALO, dp), BF16)],
        compiler_params=pltpu.CompilerParams(
            dimension_semantics=("parallel", "arbitrary"), vmem_limit_bytes=VMEM_LIMIT),
        name="in_proj_conv_pool",
    )(x, x, x, mod3, w_in.astype(BF16), b_in.reshape(1, n_proj), conv_w,
      conv_b.reshape(1, 3 * ch), pool_w.astype(BF16), pool_scale.reshape(1, dp), band)


def _filter_mlp_kernel(z_ref, w1_ref, b1_ref, wi_ref, bi_ref, fr_ref, wo_ref, ad_ref,
                       of_ref, or_ref, *, n2len):
    z = z_ref[...]
    fr = fr_ref[...]
    h = jnp.sin(fr * (jnp.dot(z.astype(BF16), w1_ref[...],
                              preferred_element_type=F32) + b1_ref[...]))
    for l in range(wi_ref.shape[0]):
        h = jnp.sin(fr * (jnp.dot(h.astype(BF16), wi_ref[l],
                                  preferred_element_type=F32) + bi_ref[l:l + 1, :]))
    hb = h.astype(BF16)
    half_lanes = LANES // 2
    n1_half = z.shape[0] // n2len
    for half in range(2):
        t = z[:, half * half_lanes:half * half_lanes + 1]
        decay = jnp.exp(-t * ad_ref[...])
        for d, o_ref in enumerate((of_ref, or_ref)):
            k = jnp.dot(hb, wo_ref[d, half], preferred_element_type=F32)
            _store_permuted(o_ref, k * decay, n2len, first=half * n1_half)


def _block_diag2(w):
    zero = jnp.zeros_like(w)
    return jnp.concatenate([jnp.concatenate([w, zero], axis=-1),
                            jnp.concatenate([zero, w], axis=-1)], axis=-2)


def _filter_taps(seq, n2len, w1, b1, w_inner, b_inner, w_out, freq, n_ch, order):
    emb, hid = w1.shape
    n_inner = w_inner.shape[0]
    oc = order * n_ch
    hl = LANES // 2
    assert emb <= hl and hid <= hl
    tile = SUBLANES * n2len
    half = tile // 2
    groups = seq // tile
    zp = _packed_pos_features(seq, emb, tile)
    ph = hl - hid
    w1d = _block_diag2(jnp.pad(w1, ((0, hl - emb), (0, ph))))
    b1d = jnp.tile(jnp.pad(b1, (0, ph)), 2).reshape(1, LANES)
    wid = _block_diag2(jnp.pad(w_inner, ((0, 0), (0, ph), (0, ph))))
    bid = jnp.tile(jnp.pad(b_inner, ((0, 0), (0, ph))), (1, 2))
    frd = jnp.tile(jnp.pad(freq, (0, ph), constant_values=1.0), 2).reshape(1, LANES)
    wo = jnp.transpose(w_out.reshape(hid, order, 2, n_ch), (2, 0, 1, 3)).reshape(2, hid, oc)
    wo = jnp.pad(wo, ((0, 0), (0, ph), (0, 0)))
    zero = jnp.zeros_like(wo)
    wo4 = jnp.stack([jnp.concatenate([wo, zero], axis=1),
                     jnp.concatenate([zero, wo], axis=1)], axis=1)
    min_decay = math.log(DECAY_TARGET) / SLOW_DECAY_PCT
    max_decay = math.log(DECAY_TARGET) / FAST_DECAY_PCT
    absdelta = jnp.abs(jnp.linspace(min_decay, max_decay, n_ch, dtype=F32))
    absdelta = jnp.tile(absdelta, order).reshape(1, oc)

    full = lambda shape: pl.BlockSpec(shape, lambda i: (0,) * len(shape))
    kern = functools.partial(_filter_mlp_kernel, n2len=n2len)
    taps = jax.ShapeDtypeStruct((groups, oc // LANES, tile, LANES), F32)
    taps_spec = pl.BlockSpec((None, oc // LANES, tile, LANES), lambda i: (i, 0, 0, 0))
    return pl.pallas_call(
        kern,
        grid=(groups,),
        in_specs=[
            pl.BlockSpec((half, LANES), lambda i: (i, 0)),
            full((LANES, LANES)), full((1, LANES)), full((n_inner, LANES, LANES)),
            full((n_inner, LANES)), full((1, LANES)), full((2, 2, LANES, oc)), full((1, oc)),
        ],
        out_specs=[taps_spec, taps_spec],
        out_shape=[taps, taps],
        compiler_params=pltpu.CompilerParams(
            dimension_semantics=("arbitrary",), vmem_limit_bytes=VMEM_LIMIT),
        name="filter_mlp",
    )(zp, w1d.astype(BF16), b1d, wid.astype(BF16), bid, frd, wo4.astype(BF16), absdelta)


def _resident(shape):
    return pl.BlockSpec(shape, lambda *_: (0,) * len(shape), pipeline_mode=pl.Buffered(1))


def _spectrum_views(a_scr, chunk, ck1, n1h, pitch):
    re0 = pl.multiple_of(chunk * (ck1 * pitch), SUBLANES)
    im0 = pl.multiple_of(chunk * (ck1 * pitch) + n1h * pitch, SUBLANES)
    return a_scr.at[pl.ds(re0, ck1 * pitch)], a_scr.at[pl.ds(im0, ck1 * pitch)]


def _spectrum_pair(a_re, a_im, j, n2len, pitch):
    def one(jj):
        return jnp.concatenate([a_re[jj * pitch:jj * pitch + n2len, :],
                                a_im[jj * pitch:jj * pitch + n2len, :]], axis=0)
    return jnp.concatenate([one(j), one(j + 1)], axis=1).astype(BF16)


def _filter_fft_kernel(hf_ref, hr_ref, g_ref, gr_ref, h2_ref, o_ref, a_scr,
                       *, ns, cn2, ck1, n1h, n2len, pitch):
    s = pl.program_id(1)

    @pl.when(s < ns)
    def _():
        for j in range(cn2):
            jr = cn2 - 1 - j
            m = s * cn2 + j
            g = jnp.concatenate([g_ref[m], gr_ref[n2len - 1 - m]], axis=1)
            x = jnp.concatenate([_load_minor(hf_ref, j), _load_minor(hr_ref, jr)], axis=0)
            r = jnp.dot(g, x.astype(BF16), preferred_element_type=F32)
            a_scr[pl.ds(m, 2 * n1h, stride=pitch), :] = r

    @pl.when(s >= ns)
    def _():
        a_re, a_im = _spectrum_views(a_scr, s - ns, ck1, n1h, pitch)
        for j in range(0, ck1, 2):
            rows = _spectrum_pair(a_re, a_im, j, n2len, pitch)
            u = jnp.dot(h2_ref[...], rows, preferred_element_type=F32)
            u = (u * (1.0 / (n1h * n2len))).astype(o_ref.dtype)
            o_ref[j] = u[:, :LANES]
            o_ref[j + 1] = u[:, LANES:]


def _filter_spectrum(taps_f, taps_r, g_fwd, g_rev, h2, n2len):
    groups, ocb, _, _ = taps_f.shape
    oc = ocb * LANES
    n1h = groups * SUBLANES
    ns = PHASE_STEPS
    cn2, ck1 = n2len // ns, n1h // ns
    pitch = n2len + SUBLANES
    kern = functools.partial(_filter_fft_kernel, ns=ns, cn2=cn2, ck1=ck1, n1h=n1h,
                             n2len=n2len, pitch=pitch)
    return pl.pallas_call(
        kern,
        grid=(oc // LANES, 2 * ns),
        in_specs=[
            pl.BlockSpec((groups, None, cn2 * SUBLANES, LANES),
                         lambda c, s: (0, c, jnp.minimum(s, ns - 1), 0)),
            pl.BlockSpec((groups, None, cn2 * SUBLANES, LANES),
                         lambda c, s: (0, c, jnp.maximum(ns - 1 - s, 0), 0)),
            _resident(g_fwd.shape), _resident(g_rev.shape), _resident(h2.shape),
        ],
        out_specs=pl.BlockSpec((ck1, 2 * n2len, LANES), lambda c, s: (jnp.maximum(s - ns, 0), 0, c)),
        out_shape=jax.ShapeDtypeStruct((n1h, 2 * n2len, oc), BF16),
        scratch_shapes=[pltpu.VMEM((2 * n1h * pitch, LANES), F32)],
        compiler_params=pltpu.CompilerParams(
            dimension_semantics=("parallel", "arbitrary"), vmem_limit_bytes=VMEM_LIMIT),
        name="filter_spectrum",
    )(taps_f, taps_r, g_fwd, g_rev, h2)


def _long_conv_kernel(vf_ref, g_ref, kf_ref, h2_ref, h2i_ref, xm_ref, hb_ref, o_ref,
                      a_scr, keep_scr, *, steps, n1h, n2len, pitch):
    ns1, ns2, ns3 = steps
    cn2a, ck1, cn2 = n2len // ns1, n1h // ns2, n2len // ns3
    s = pl.program_id(2)

    @pl.when(s < ns1)
    def _():
        base = s * cn2a
        for j in range(cn2a):
            x = _load_minor(vf_ref, j)
            keep_scr[base + j] = x
            r = jnp.dot(g_ref[base + j], x.astype(BF16), preferred_element_type=F32)
            a_scr[pl.ds(base + j, 2 * n1h, stride=pitch), :] = r

    @pl.when((s >= ns1) & (s < ns1 + ns2))
    def _():
        a_re, a_im = _spectrum_views(a_scr, s - ns1, ck1, n1h, pitch)
        for j in range(0, ck1, 2):
            rows = _spectrum_pair(a_re, a_im, j, n2len, pitch)
            u = jnp.dot(h2_ref[...], rows, preferred_element_type=F32)
            ur, ui = u[:n2len], u[n2len:]
            kf = jnp.concatenate([kf_ref[j], kf_ref[j + 1]], axis=1).astype(F32)
            kr, ki = kf[:n2len], kf[n2len:]
            y = jnp.concatenate([ur * kr - ui * ki, ur * ki + ui * kr], axis=0)
            bb = jnp.dot(h2i_ref[...], y.astype(BF16), preferred_element_type=F32)
            for jj in range(2):
                lanes = slice(jj * LANES, (jj + 1) * LANES)
                a_re[(j + jj) * pitch:(j + jj) * pitch + n2len, :] = bb[:n2len, lanes]
                a_im[(j + jj) * pitch:(j + jj) * pitch + n2len, :] = bb[n2len:, lanes]

    @pl.when(s >= ns1 + ns2)
    def _():
        base = (s - ns1 - ns2) * cn2
        for j in range(cn2):
            rows = a_scr[pl.ds(base + j, 2 * n1h, stride=pitch), :]
            y = lax.dot_general(g_ref[base + j], rows.astype(BF16), (((0,), (0,)), ((), ())),
                                preferred_element_type=F32)
            z = _load_minor(xm_ref, j) * (y + hb_ref[...] * keep_scr[base + j])
            o_ref[:, j * SUBLANES:(j + 1) * SUBLANES, :] = z.reshape(
                n1h // SUBLANES, SUBLANES, z.shape[1])


def _long_conv(u_perm, mult_perm, kf, hbias, order_idx, tables):
    g_fwd, _, h2, h2i = tables
    bsz, groups, cbn, tile, _ = u_perm.shape
    n2len = tile // SUBLANES
    n1h = groups * SUBLANES
    steps = CONV_STEPS
    ns1, ns2, ns3 = steps
    pitch = n2len + SUBLANES

    def late(s):
        return jnp.clip(s - ns1 - ns2, 0, ns3 - 1)

    data_early = pl.BlockSpec((None, groups, None, (n2len // ns1) * SUBLANES, LANES),
                              lambda c, b, s: (b, 0, c, jnp.minimum(s, ns1 - 1), 0))
    data_late = pl.BlockSpec((None, groups, None, (n2len // ns3) * SUBLANES, LANES),
                             lambda c, b, s: (b, 0, c, late(s), 0))
    in_specs = [
        data_early,
        _resident(g_fwd.shape),
        pl.BlockSpec((n1h // ns2, 2 * n2len, LANES),
                     lambda c, b, s: (jnp.clip(s - ns1, 0, ns2 - 1), 0, order_idx * cbn + c)),
        _resident(h2.shape), _resident(h2i.shape),
        data_late,
        pl.BlockSpec((None, 1, LANES), lambda c, b, s: (order_idx, 0, c)),
    ]
    kern = functools.partial(_long_conv_kernel, steps=steps, n1h=n1h, n2len=n2len,
                             pitch=pitch)
    return pl.pallas_call(
        kern,
        grid=(cbn, bsz, sum(steps)),
        in_specs=in_specs,
        out_specs=data_late,
        out_shape=jax.ShapeDtypeStruct(u_perm.shape, F32),
        scratch_shapes=[pltpu.VMEM((2 * n1h * pitch, LANES), F32),
                        pltpu.VMEM((n2len, n1h, LANES), F32)],
        compiler_params=pltpu.CompilerParams(
            dimension_semantics=("parallel", "parallel", "arbitrary"),
            vmem_limit_bytes=CONV_VMEM_LIMIT),
        name="long_conv",
    )(u_perm, g_fwd, kf, h2, h2i, mult_perm, hbias)


def _out_kernel(x_ref, yh_ref, yp_ref, mod_ref, w_ref, b_ref, g_ref, beta_ref, o_ref,
                *, n2len, alpha):
    cbn = yh_ref.shape[0]
    for j0 in range(0, SUBLANES, OUT_SLABS):
        rows = slice(j0 * n2len, (j0 + OUT_SLABS) * n2len)
        yh = jnp.concatenate(
            [jnp.concatenate([yh_ref[cb, pl.ds(j, n2len, stride=SUBLANES), :]
                              for cb in range(cbn)], axis=1)
             for j in range(j0, j0 + OUT_SLABS)], axis=0)
        y = jnp.concatenate([yh.astype(BF16), yp_ref[rows, :]], axis=1)
        acc = jnp.dot(y, w_ref[...], preferred_element_type=F32)
        h = alpha * x_ref[rows, :] + mod_ref[2:3, :] * (acc + b_ref[...])
        mu = jnp.mean(h, axis=-1, keepdims=True)
        hc = h - mu
        var = jnp.mean(hc * hc, axis=-1, keepdims=True)
        o_ref[rows, :] = hc * lax.rsqrt(var + LN_EPS) * g_ref[...] + beta_ref[...]


def _output_projection(x, yh_perm, yp, mod3, w_out, b_out, ln_g, ln_b, alpha):
    bsz, seq, d = x.shape
    _, _, cbn, tile, _ = yh_perm.shape
    ch = cbn * LANES
    n2len = tile // SUBLANES
    dp = yp.shape[2]
    full = lambda shape: pl.BlockSpec(shape, lambda b, i: (0,) * len(shape))
    kern = functools.partial(_out_kernel, n2len=n2len, alpha=alpha)
    return pl.pallas_call(
        kern,
        grid=(bsz, seq // tile),
        in_specs=[
            pl.BlockSpec((None, tile, d), lambda b, i: (b, i, 0)),
            pl.BlockSpec((None, None, cbn, tile, LANES), lambda b, i: (b, i, 0, 0, 0)),
            pl.BlockSpec((None, tile, dp), lambda b, i: (b, i, 0)),
            pl.BlockSpec((None, 3, d), lambda b, i: (b, 0, 0)),
            full((ch + dp, d)), full((1, d)), full((1, d)), full((1, d)),
        ],
        out_specs=pl.BlockSpec((None, tile, d), lambda b, i: (b, i, 0)),
        out_shape=jax.ShapeDtypeStruct(x.shape, x.dtype),
        compiler_params=pltpu.CompilerParams(
            dimension_semantics=("parallel", "arbitrary"), vmem_limit_bytes=VMEM_LIMIT),
        name="out_proj_deepnorm",
    )(x, yh_perm, yp, mod3, w_out.astype(BF16), b_out.reshape(1, d), ln_g.reshape(1, d),
      ln_b.reshape(1, d))


def _forward(x, c, w_ada, b_ada, w_in, b_in, conv_w, conv_b, filt_w1, filt_b1, filt_w_inner,
             filt_b_inner, filt_w_out, filt_freq, hyena_bias, pool_w, pool_scale, w_out,
             b_out, ln_g, ln_b, *, n2len):
    bsz, seq, d = x.shape
    depth = w_ada.shape[0]
    order, n_ch = hyena_bias.shape[1], hyena_bias.shape[2]
    alpha = (2.0 * depth) ** 0.25
    tables = _dft_tables(seq, n2len)
    h = x
    for layer in range(depth):
        mod3 = _modulation(c, w_ada[layer], b_ada[layer]).reshape(bsz, 3, d)
        v, x1, x2g, yp = _input_projection(
            h, mod3, w_in[layer], b_in[layer], conv_w[layer], conv_b[layer], pool_w[layer],
            pool_scale[layer], n2len)
        taps_f, taps_r = _filter_taps(
            seq, n2len, filt_w1[layer], filt_b1[layer], filt_w_inner[layer],
            filt_b_inner[layer], filt_w_out[layer], filt_freq[layer], n_ch, order)
        kf = _filter_spectrum(taps_f, taps_r, tables[0], tables[1], tables[2], n2len)
        hbias = hyena_bias[layer].reshape(order, 1, n_ch)
        z = v
        mults = (x1, x2g)
        for o in range(order):
            z = _long_conv(z, mults[o], kf, hbias, o, tables)
        h = _output_projection(h, z, yp, mod3, w_out[layer], b_out[layer], ln_g[layer],
                               ln_b[layer], alpha)
    return h


def kernel(x, c, w_ada, b_ada, w_in, b_in, conv_w, conv_b, filt_w1, filt_b1, filt_w_inner,
           filt_b_inner, filt_w_out, filt_freq, hyena_bias, pool_w, pool_scale, w_out, b_out,
           ln_g, ln_b):
    return _forward(x, c, w_ada, b_ada, w_in, b_in, conv_w, conv_b, filt_w1, filt_b1,
                    filt_w_inner, filt_b_inner, filt_w_out, filt_freq, hyena_bias, pool_w,
                    pool_scale, w_out, b_out, ln_g, ln_b, n2len=MINOR_LEN)
```

```python
import functools
import math

import jax
import jax.numpy as jnp
from jax import lax
from jax.experimental import pallas as pl
from jax.experimental.pallas import tpu as pltpu

F32 = jnp.float32
BF16 = jnp.bfloat16
HIGHEST = lax.Precision.HIGHEST

POOL_WINDOWS = (2, 4, 8, 16)
LN_EPS = 1e-5
DECAY_TARGET = 1e-2
FAST_DECAY_PCT = 0.3
SLOW_DECAY_PCT = 1.5

LANES = 128
SUBLANES = 8
HALO = 16
OUT_SLABS = 8
BAND_ROWS = 128
BAND_SPAN = 256
MINOR_LEN = 128
FILTER_STEPS = (2, 4)
CONV_STEPS = (2, 2, 4)
VMEM_LIMIT = 56 * 1024 * 1024
CONV_VMEM_LIMIT = 60 * 1024 * 1024


def _silu(x):
    h = 0.5 * x
    return h + h * jnp.tanh(h)


def _dft_tables(seq, n2len):
    n1h = seq // n2len
    n_fft = 2 * seq
    n2i = jnp.arange(n2len, dtype=jnp.int32)
    k1 = jnp.arange(n1h, dtype=jnp.int32)
    odd = 2 * k1 + 1
    alpha = ((odd[:, None] * k1[None, :]) % (4 * n1h)).astype(F32) * (2.0 * math.pi / (4 * n1h))
    n2e = jnp.arange(n2len + 1, dtype=jnp.int32)
    beta = ((n2e[:, None] * odd[None, :]) % (2 * n_fft)).astype(F32) * (2.0 * math.pi / (2 * n_fft))
    ca, sa = jnp.cos(alpha)[None], jnp.sin(alpha)[None]
    cb, sb = jnp.cos(beta)[:, :, None], jnp.sin(beta)[:, :, None]
    cos_t = ca * cb[:-1] - sa * sb[:-1]
    sin_t = sa * cb[:-1] + ca * sb[:-1]
    g_fwd = jnp.concatenate([cos_t, -sin_t], axis=1).astype(BF16)
    cos_r = ca * cb[1:] - sa * sb[1:]
    sin_r = sa * cb[1:] + ca * sb[1:]
    g_rev = jnp.concatenate([cos_r, sin_r], axis=1).astype(BF16)

    m = (n2i[:, None] * n2i[None, :]) % n2len
    phi = m.astype(F32) * (2.0 * math.pi / n2len)
    cm, sm = jnp.cos(phi), jnp.sin(phi)
    h2 = jnp.block([[cm, sm], [-sm, cm]]).astype(BF16)
    h2i = jnp.block([[cm, -sm], [sm, cm]]).astype(BF16)
    return g_fwd, g_rev, h2, h2i


def _packed_pos_features(seq, emb, tile):
    hl = LANES // 2
    half = tile // 2
    bands = (emb - 1) // 2
    row = jnp.arange((seq // tile) * half, dtype=jnp.int32)[:, None]
    lane = jnp.arange(LANES, dtype=jnp.int32)[None, :]
    pos = ((row // half) * tile + (lane // hl) * half + row % half).astype(F32)
    feat = lane % hl
    t = pos / (seq - 1)
    w = (2.0 * math.pi / seq) * pos
    f = jnp.linspace(1e-4, bands - 1, bands, dtype=F32)[(feat - 1) % bands]
    phase = jnp.where(feat > bands, 0.5 * math.pi, 0.0)
    return jnp.where(feat == 0, t,
                     jnp.where(feat <= 2 * bands, jnp.cos(f * w + phase), 0.0))


def _mod_kernel(c_ref, w_ref, b_ref, o_ref):
    s = _silu(c_ref[...])
    o_ref[...] = jnp.dot(s, w_ref[...], precision=HIGHEST,
                         preferred_element_type=F32) + b_ref[...]


def _modulation(c, w_ada, b_ada):
    bsz, d = c.shape
    n_out = w_ada.shape[1]
    return pl.pallas_call(
        _mod_kernel,
        grid=(n_out // d,),
        in_specs=[pl.BlockSpec((bsz, d), lambda j: (0, 0)),
                  pl.BlockSpec((d, d), lambda j: (0, j)),
                  pl.BlockSpec((1, d), lambda j: (0, j))],
        out_specs=pl.BlockSpec((bsz, d), lambda j: (0, j)),
        out_shape=jax.ShapeDtypeStruct((bsz, n_out), F32),
        name="adaln_mod",
    )(c, w_ada, b_ada.reshape(1, n_out))


def _store_permuted(o_ref, val, n2len, first=0):
    for cb in range(val.shape[1] // LANES):
        for j in range(val.shape[0] // n2len):
            o_ref[cb, pl.ds(first + j, n2len, stride=SUBLANES), :] = val[
                j * n2len:(j + 1) * n2len, cb * LANES:(cb + 1) * LANES]


def _load_minor(x_ref, j):
    blk = x_ref[:, j * SUBLANES:(j + 1) * SUBLANES, :]
    return blk.reshape(blk.shape[0] * SUBLANES, blk.shape[2])


def _band_window(b, tile):
    last = tile + 2 * HALO - BAND_SPAN
    start = min(b * BAND_ROWS, last)
    return start, 0 if start == b * BAND_ROWS else 1


def _band_matrices(tile):
    t = jnp.arange(BAND_ROWS, dtype=jnp.int32)[:, None]
    k = jnp.arange(BAND_SPAN, dtype=jnp.int32)[None, :]
    mats = []
    for win in POOL_WINDOWS:
        half = win // 2
        per_variant = []
        for b in (0, tile // BAND_ROWS - 1):
            start, _ = _band_window(b, tile)
            rel = k + start - HALO - (b * BAND_ROWS + t)
            per_variant.append(((rel >= -half) & (rel < half)).astype(BF16))
        mats.append(jnp.stack(per_variant))
    return jnp.stack(mats)


def _proj_kernel(xp_ref, xc_ref, xn_ref, mod_ref, w_ref, b_ref, cw_ref, cb_ref, pw_ref,
                 ps_ref, band_ref, v_ref, x1_ref, x2_ref, yp_ref, p_scr, q_scr,
                 *, tile, n2len, ch, seq):
    i = pl.program_id(1)
    nt = pl.num_programs(1)
    shift = mod_ref[0:1, :]
    scale1 = 1.0 + mod_ref[1:2, :]
    xe = jnp.concatenate([xp_ref[...], xc_ref[...], xn_ref[...]], axis=0)
    ue = (xe * scale1 + shift).astype(BF16)
    uc = ue[HALO:HALO + tile, :]

    def zero_outside_sequence(p):
        return jnp.concatenate([jnp.where(i > 0, p[:HALO], 0.0), p[HALO:HALO + tile],
                                jnp.where(i < nt - 1, p[HALO + tile:], 0.0)], axis=0)

    hg = jnp.dot(uc, w_ref[:, 3 * ch:4 * ch], preferred_element_type=F32) + b_ref[:, 3 * ch:4 * ch]
    gate = _silu(hg)

    for k, o_ref in enumerate((v_ref, x1_ref, x2_ref)):
        cols = slice(k * ch, (k + 1) * ch)
        p = jnp.dot(ue, w_ref[:, cols], preferred_element_type=F32) + b_ref[:, cols]
        p_scr[...] = zero_outside_sequence(p)
        s = (cb_ref[:, cols]
             + cw_ref[0:1, cols] * p_scr[HALO - 1:HALO - 1 + tile, :]
             + cw_ref[1:2, cols] * p_scr[HALO:HALO + tile, :]
             + cw_ref[2:3, cols] * p_scr[HALO + 1:HALO + 1 + tile, :])
        if k == 2:
            s = s * gate
        _store_permuted(o_ref, s, n2len)

    dp = ps_ref.shape[1]
    c0 = 4 * ch
    pin = jnp.dot(ue, w_ref[:, c0:c0 + dp], preferred_element_type=F32) + b_ref[:, c0:c0 + dp]
    pin = zero_outside_sequence(pin)
    q_scr[...] = pin.astype(BF16)
    pin_c = pin[HALO:HALO + tile, :]
    pgate = jnp.dot(uc, w_ref[:, c0 + dp:c0 + 2 * dp], preferred_element_type=F32) + b_ref[:, c0 + dp:c0 + 2 * dp]
    pos = i * tile + lax.broadcasted_iota(jnp.int32, (tile, 1), 0)
    pg = dp // len(POOL_WINDOWS)
    groups = []
    for g, win in enumerate(POOL_WINDOWS):
        lanes = slice(g * pg, (g + 1) * pg)
        half = win // 2
        sums = []
        for b in range(tile // BAND_ROWS):
            start, variant = _band_window(b, tile)
            sums.append(jnp.dot(band_ref[g, variant], q_scr[start:start + BAND_SPAN, lanes],
                                preferred_element_type=F32))
        acc = jnp.concatenate(sums, axis=0)
        cnt = (jnp.minimum(pos + half, seq) - jnp.maximum(pos - half, 0)).astype(F32)
        diff = acc / cnt - pin_c[:, lanes]
        groups.append(jnp.dot(diff.astype(BF16), pw_ref[g], preferred_element_type=F32))
    yp = jnp.concatenate(groups, axis=1) * ps_ref[...] * _silu(pgate)
    yp_ref[...] = yp.astype(yp_ref.dtype)


def _input_projection(x, mod3, w_in, b_in, conv_w, conv_b, pool_w, pool_scale, n2len):
    bsz, seq, d = x.shape
    ch = conv_w.shape[1] // 3
    dp = pool_scale.shape[0]
    n1h = seq // n2len
    tile = SUBLANES * n2len
    nt = seq // tile
    hb = tile // HALO
    n_proj = w_in.shape[1]
    cbn = ch // LANES
    assert max(POOL_WINDOWS) // 2 <= HALO and dp == ch and tile % BAND_ROWS == 0
    band = _band_matrices(tile)
    perm =jax.ShapeDtypeStruct((bsz, n1h // SUBLANES, cbn, tile, LANES), F32)
    perm_spec = pl.BlockSpec((None, None, cbn, tile, LANES), lambda b, i: (b, i, 0, 0, 0))
    full = lambda shape: pl.BlockSpec(shape, lambda b, i: (0,) * len(shape))
    kern = functools.partial(_proj_kernel, tile=tile, n2len=n2len, ch=ch, seq=seq)
    return pl.pallas_call(
        kern,
        grid=(bsz, nt),
        in_specs=[
            pl.BlockSpec((None, HALO, d), lambda b, i: (b, jnp.maximum(i * hb - 1, 0), 0)),
            pl.BlockSpec((None, tile, d), lambda b, i: (b, i, 0)),
            pl.BlockSpec((None, HALO, d), lambda b, i: (b, jnp.minimum((i + 1) * hb, seq // HALO - 1), 0)),
            pl.BlockSpec((None, 3, d), lambda b, i: (b, 0, 0)),
            full((d, n_proj)), full((1, n_proj)), full((3, 3 * ch)), full((1, 3 * ch)),
            full(pool_w.shape), full((1, dp)), full(band.shape),
        ],
        out_specs=[perm_spec, perm_spec, perm_spec,
                   pl.BlockSpec((None, tile, dp), lambda b, i: (b, i, 0))],
        out_shape=[perm, perm, perm, jax.ShapeDtypeStruct((bsz, seq, dp), BF16)],
        scratch_shapes=[pltpu.VMEM((tile + 2 * HALO, ch), F32),
                        pltpu.VMEM((tile + 2 * HALO, dp), BF16)],
        compiler_params=pltpu.CompilerParams(
            dimension_semantics=("parallel", "arbitrary"), vmem_limit_bytes=VMEM_LIMIT),
        name="in_proj_conv_pool",
    )(x, x, x, mod3, w_in.astype(BF16), b_in.reshape(1, n_proj), conv_w,
      conv_b.reshape(1, 3 * ch), pool_w.astype(BF16), pool_scale.reshape(1, dp), band)


def _filter_mlp_kernel(z_ref, w1_ref, b1_ref, wi_ref, bi_ref, fr_ref, wo_ref, ad_ref,
                       of_ref, or_ref, *, n2len):
    z = z_ref[...]
    fr = fr_ref[...]
    h = jnp.sin(fr * (jnp.dot(z.astype(BF16), w1_ref[...],
                              preferred_element_type=F32) + b1_ref[...]))
    for l in range(wi_ref.shape[0]):
        h = jnp.sin(fr * (jnp.dot(h.astype(BF16), wi_ref[l],
                                  preferred_element_type=F32) + bi_ref[l:l + 1, :]))
    hb = h.astype(BF16)
    half_lanes = LANES // 2
    n1_half = z.shape[0] // n2len
    for half in range(2):
        t = z[:, half * half_lanes:half * half_lanes + 1]
        decay = jnp.exp(-t * ad_ref[...])
        for d, o_ref in enumerate((of_ref, or_ref)):
            k = jnp.dot(hb, wo_ref[d, half], preferred_element_type=F32)
            _store_permuted(o_ref, k * decay, n2len, first=half * n1_half)


def _block_diag2(w):
    zero = jnp.zeros_like(w)
    return jnp.concatenate([jnp.concatenate([w, zero], axis=-1),
                            jnp.concatenate([zero, w], axis=-1)], axis=-2)


def _filter_taps(seq, n2len, w1, b1, w_inner, b_inner, w_out, freq, n_ch, order):
    emb, hid = w1.shape
    n_inner = w_inner.shape[0]
    oc = order * n_ch
    hl = LANES // 2
    assert emb <= hl and hid <= hl
    tile = SUBLANES * n2len
    half = tile // 2
    groups = seq // tile
    zp = _packed_pos_features(seq, emb, tile)
    ph = hl - hid
    w1d = _block_diag2(jnp.pad(w1, ((0, hl - emb), (0, ph))))
    b1d = jnp.tile(jnp.pad(b1, (0, ph)), 2).reshape(1, LANES)
    wid = _block_diag2(jnp.pad(w_inner, ((0, 0), (0, ph), (0, ph))))
    bid = jnp.tile(jnp.pad(b_inner, ((0, 0), (0, ph))), (1, 2))
    frd = jnp.tile(jnp.pad(freq, (0, ph), constant_values=1.0), 2).reshape(1, LANES)
    wo = jnp.transpose(w_out.reshape(hid, order, 2, n_ch), (2, 0, 1, 3)).reshape(2, hid, oc)
    wo = jnp.pad(wo, ((0, 0), (0, ph), (0, 0)))
    zero = jnp.zeros_like(wo)
    wo4 = jnp.stack([jnp.concatenate([wo, zero], axis=1),
                     jnp.concatenate([zero, wo], axis=1)], axis=1)
    min_decay = math.log(DECAY_TARGET) / SLOW_DECAY_PCT
    max_decay = math.log(DECAY_TARGET) / FAST_DECAY_PCT
    absdelta = jnp.abs(jnp.linspace(min_decay, max_decay, n_ch, dtype=F32))
    absdelta = jnp.tile(absdelta, order).reshape(1, oc)

    full = lambda shape: pl.BlockSpec(shape, lambda i: (0,) * len(shape))
    kern = functools.partial(_filter_mlp_kernel, n2len=n2len)
    taps = jax.ShapeDtypeStruct((groups, oc // LANES, tile, LANES), F32)
    taps_spec = pl.BlockSpec((None, oc // LANES, tile, LANES), lambda i: (i, 0, 0, 0))
    return pl.pallas_call(
        kern,
        grid=(groups,),
        in_specs=[
            pl.BlockSpec((half, LANES), lambda i: (i, 0)),
            full((LANES, LANES)), full((1, LANES)), full((n_inner, LANES, LANES)),
            full((n_inner, LANES)), full((1, LANES)), full((2, 2, LANES, oc)), full((1, oc)),
        ],
        out_specs=[taps_spec, taps_spec],
        out_shape=[taps, taps],
        compiler_params=pltpu.CompilerParams(
            dimension_semantics=("arbitrary",), vmem_limit_bytes=VMEM_LIMIT),
        name="filter_mlp",
    )(zp, w1d.astype(BF16), b1d, wid.astype(BF16), bid, frd, wo4.astype(BF16), absdelta)


def _resident(shape):
    return pl.BlockSpec(shape, lambda *_: (0,) * len(shape), pipeline_mode=pl.Buffered(1))


def _spectrum_views(a_scr, chunk, ck1, n1h, pitch):
    re0 = pl.multiple_of(chunk * (ck1 * pitch), SUBLANES)
    im0 = pl.multiple_of(chunk * (ck1 * pitch) + n1h * pitch, SUBLANES)
    return a_scr.at[pl.ds(re0, ck1 * pitch)], a_scr.at[pl.ds(im0, ck1 * pitch)]


def _spectrum_pair(a_re, a_im, j, n2len, pitch):
    def one(jj):
        return jnp.concatenate([a_re[jj * pitch:jj * pitch + n2len, :],
                                a_im[jj * pitch:jj * pitch + n2len, :]], axis=0)
    return jnp.concatenate([one(j), one(j + 1)], axis=1).astype(BF16)


def _filter_fft_kernel(hf_ref, hr_ref, g_ref, gr_ref, h2_ref, o_ref, a_scr,
                       *, ns, cn2, ck1, n1h, n2len, pitch):
    s = pl.program_id(1)

    @pl.when(s < ns)
    def _():
        for j in range(cn2):
            jr = cn2 - 1 - j
            m = s * cn2 + j
            g = jnp.concatenate([g_ref[m], gr_ref[n2len - 1 - m]], axis=1)
            x = jnp.concatenate([_load_minor(hf_ref, j), _load_minor(hr_ref, jr)], axis=0)
            r = jnp.dot(g, x.astype(BF16), preferred_element_type=F32)
            a_scr[pl.ds(m, 2 * n1h, stride=pitch), :] = r

    @pl.when(s >= ns)
    def _():
        a_re, a_im = _spectrum_views(a_scr, s - ns, ck1, n1h, pitch)
        for j in range(0, ck1, 2):
            rows = _spectrum_pair(a_re, a_im, j, n2len, pitch)
            u = jnp.dot(h2_ref[...], rows, preferred_element_type=F32)
            u = (u * (1.0 / (n1h * n2len))).astype(o_ref.dtype)
            o_ref[j] = u[:, :LANES]
            o_ref[j + 1] = u[:, LANES:]


def _filter_spectrum(taps_f, taps_r, g_fwd, g_rev, h2, n2len):
    groups, ocb, _, _ = taps_f.shape
    oc = ocb * LANES
    n1h = groups * SUBLANES
    ns, ns2 = FILTER_STEPS
    cn2, ck1 = n2len // ns, n1h // ns2
    pitch = n2len + SUBLANES
    kern = functools.partial(_filter_fft_kernel, ns=ns, cn2=cn2, ck1=ck1, n1h=n1h,
                             n2len=n2len, pitch=pitch)
    return pl.pallas_call(
        kern,
        grid=(oc // LANES, ns + ns2),
        in_specs=[
            pl.BlockSpec((groups, None, cn2 * SUBLANES, LANES),
                         lambda c, s: (0, c, jnp.minimum(s, ns - 1), 0)),
            pl.BlockSpec((groups, None, cn2 * SUBLANES, LANES),
                         lambda c, s: (0, c, jnp.maximum(ns - 1 - s, 0), 0)),
            _resident(g_fwd.shape), _resident(g_rev.shape), _resident(h2.shape),
        ],
        out_specs=pl.BlockSpec((ck1, 2 * n2len, LANES), lambda c, s: (jnp.maximum(s - ns, 0), 0, c)),
        out_shape=jax.ShapeDtypeStruct((n1h, 2 * n2len, oc), BF16),
        scratch_shapes=[pltpu.VMEM((2 * n1h * pitch, LANES), F32)],
        compiler_params=pltpu.CompilerParams(
            dimension_semantics=("parallel", "arbitrary"), vmem_limit_bytes=VMEM_LIMIT),
        name="filter_spectrum",
    )(taps_f, taps_r, g_fwd, g_rev, h2)


def _long_conv_kernel(vf_ref, g_ref, kf_ref, h2_ref, h2i_ref, xm_ref, hb_ref, o_ref,
                      a_scr, keep_scr, *, steps, n1h, n2len, pitch):
    ns1, ns2, ns3 = steps
    cn2a, ck1, cn2 = n2len // ns1, n1h // ns2, n2len // ns3
    s = pl.program_id(2)

    @pl.when(s < ns1)
    def _():
        base = s * cn2a
        for j in range(cn2a):
            x = _load_minor(vf_ref, j)
            keep_scr[base + j] = x
            r = jnp.dot(g_ref[base + j], x.astype(BF16), preferred_element_type=F32)
            a_scr[pl.ds(base + j, 2 * n1h, stride=pitch), :] = r

    @pl.when((s >= ns1) & (s < ns1 + ns2))
    def _():
        a_re, a_im = _spectrum_views(a_scr, s - ns1, ck1, n1h, pitch)
        for j in range(0, ck1, 2):
            rows = _spectrum_pair(a_re, a_im, j, n2len, pitch)
            u = jnp.dot(h2_ref[...], rows, preferred_element_type=F32)
            ur, ui = u[:n2len], u[n2len:]
            kf = jnp.concatenate([kf_ref[j], kf_ref[j + 1]], axis=1).astype(F32)
            kr, ki = kf[:n2len], kf[n2len:]
            y = jnp.concatenate([ur * kr - ui * ki, ur * ki + ui * kr], axis=0)
            bb = jnp.dot(h2i_ref[...], y.astype(BF16), preferred_element_type=F32)
            for jj in range(2):
                lanes = slice(jj * LANES, (jj + 1) * LANES)
                a_re[(j + jj) * pitch:(j + jj) * pitch + n2len, :] = bb[:n2len, lanes]
                a_im[(j + jj) * pitch:(j + jj) * pitch + n2len, :] = bb[n2len:, lanes]

    @pl.when(s >= ns1 + ns2)
    def _():
        base = (s - ns1 - ns2) * cn2
        for j in range(cn2):
            rows = a_scr[pl.ds(base + j, 2 * n1h, stride=pitch), :]
            y = lax.dot_general(g_ref[base + j], rows.astype(BF16), (((0,), (0,)), ((), ())),
                                preferred_element_type=F32)
            z = _load_minor(xm_ref, j) * (y + hb_ref[...] * keep_scr[base + j])
            o_ref[:, j * SUBLANES:(j + 1) * SUBLANES, :] = z.reshape(
                n1h // SUBLANES, SUBLANES, z.shape[1])


def _long_conv(u_perm, mult_perm, kf, hbias, order_idx, tables):
    g_fwd, _, h2, h2i = tables
    bsz, groups, cbn, tile, _ = u_perm.shape
    n2len = tile // SUBLANES
    n1h = groups * SUBLANES
    steps = CONV_STEPS
    ns1, ns2, ns3 = steps
    pitch = n2len + SUBLANES

    def late(s):
        return jnp.clip(s - ns1 - ns2, 0, ns3 - 1)

    data_early = pl.BlockSpec((None, groups, None, (n2len // ns1) * SUBLANES, LANES),
                              lambda c, b, s: (b, 0, c, jnp.minimum(s, ns1 - 1), 0))
    data_late = pl.BlockSpec((None, groups, None, (n2len // ns3) * SUBLANES, LANES),
                             lambda c, b, s: (b, 0, c, late(s), 0))
    in_specs = [
        data_early,
        _resident(g_fwd.shape),
        pl.BlockSpec((n1h // ns2, 2 * n2len, LANES),
                     lambda c, b, s: (jnp.clip(s - ns1, 0, ns2 - 1), 0, order_idx * cbn + c)),
        _resident(h2.shape), _resident(h2i.shape),
        data_late,
        pl.BlockSpec((None, 1, LANES), lambda c, b, s: (order_idx, 0, c)),
    ]
    kern = functools.partial(_long_conv_kernel, steps=steps, n1h=n1h, n2len=n2len,
                             pitch=pitch)
    return pl.pallas_call(
        kern,
        grid=(cbn, bsz, sum(steps)),
        in_specs=in_specs,
        out_specs=data_late,
        out_shape=jax.ShapeDtypeStruct(u_perm.shape, F32),
        scratch_shapes=[pltpu.VMEM((2 * n1h * pitch, LANES), F32),
                        pltpu.VMEM((n2len, n1h, LANES), F32)],
        compiler_params=pltpu.CompilerParams(
            dimension_semantics=("parallel", "parallel", "arbitrary"),
            vmem_limit_bytes=CONV_VMEM_LIMIT),
        name="long_conv",
    )(u_perm, g_fwd, kf, h2, h2i, mult_perm, hbias)


def _out_kernel(x_ref, yh_ref, yp_ref, mod_ref, w_ref, b_ref, g_ref, beta_ref, o_ref,
                *, n2len, alpha):
    cbn = yh_ref.shape[0]
    for j0 in range(0, SUBLANES, OUT_SLABS):
        rows = slice(j0 * n2len, (j0 + OUT_SLABS) * n2len)
        yh = jnp.concatenate(
            [jnp.concatenate([yh_ref[cb, pl.ds(j, n2len, stride=SUBLANES), :]
                              for cb in range(cbn)], axis=1)
             for j in range(j0, j0 + OUT_SLABS)], axis=0)
        y = jnp.concatenate([yh.astype(BF16), yp_ref[rows, :]], axis=1)
        acc = jnp.dot(y, w_ref[...], preferred_element_type=F32)
        h = alpha * x_ref[rows, :] + mod_ref[2:3, :] * (acc + b_ref[...])
        mu = jnp.mean(h, axis=-1, keepdims=True)
        hc = h - mu
        var = jnp.mean(hc * hc, axis=-1, keepdims=True)
        o_ref[rows, :] = hc * lax.rsqrt(var + LN_EPS) * g_ref[...] + beta_ref[...]


def _output_projection(x, yh_perm, yp, mod3, w_out, b_out, ln_g, ln_b, alpha):
    bsz, seq, d = x.shape
    _, _, cbn, tile, _ = yh_perm.shape
    ch = cbn * LANES
    n2len = tile // SUBLANES
    dp = yp.shape[2]
    full = lambda shape: pl.BlockSpec(shape, lambda b, i: (0,) * len(shape))
    kern = functools.partial(_out_kernel, n2len=n2len, alpha=alpha)
    return pl.pallas_call(
        kern,
        grid=(bsz, seq // tile),
        in_specs=[
            pl.BlockSpec((None, tile, d), lambda b, i: (b, i, 0)),
            pl.BlockSpec((None, None, cbn, tile, LANES), lambda b, i: (b, i, 0, 0, 0)),
            pl.BlockSpec((None, tile, dp), lambda b, i: (b, i, 0)),
            pl.BlockSpec((None, 3, d), lambda b, i: (b, 0, 0)),
            full((ch + dp, d)), full((1, d)), full((1, d)), full((1, d)),
        ],
        out_specs=pl.BlockSpec((None, tile, d), lambda b, i: (b, i, 0)),
        out_shape=jax.ShapeDtypeStruct(x.shape, x.dtype),
        compiler_params=pltpu.CompilerParams(
            dimension_semantics=("parallel", "arbitrary"), vmem_limit_bytes=VMEM_LIMIT),
        name="out_proj_deepnorm",
    )(x, yh_perm, yp, mod3, w_out.astype(BF16), b_out.reshape(1, d), ln_g.reshape(1, d),
      ln_b.reshape(1, d))


def _forward(x, c, w_ada, b_ada, w_in, b_in, conv_w, conv_b, filt_w1, filt_b1, filt_w_inner,
             filt_b_inner, filt_w_out, filt_freq, hyena_bias, pool_w, pool_scale, w_out,
             b_out, ln_g, ln_b, *, n2len):
    bsz, seq, d = x.shape
    depth = w_ada.shape[0]
    order, n_ch = hyena_bias.shape[1], hyena_bias.shape[2]
    alpha = (2.0 * depth) ** 0.25
    tables = _dft_tables(seq, n2len)
    h = x
    for layer in range(depth):
        mod3 = _modulation(c, w_ada[layer], b_ada[layer]).reshape(bsz, 3, d)
        v, x1, x2g, yp = _input_projection(
            h, mod3, w_in[layer], b_in[layer], conv_w[layer], conv_b[layer], pool_w[layer],
            pool_scale[layer], n2len)
        taps_f, taps_r = _filter_taps(
            seq, n2len, filt_w1[layer], filt_b1[layer], filt_w_inner[layer],
            filt_b_inner[layer], filt_w_out[layer], filt_freq[layer], n_ch, order)
        kf = _filter_spectrum(taps_f, taps_r, tables[0], tables[1], tables[2], n2len)
        hbias = hyena_bias[layer].reshape(order, 1, n_ch)
        z = v
        mults = (x1, x2g)
        for o in range(order):
            z = _long_conv(z, mults[o], kf, hbias, o, tables)
        h = _output_projection(h, z, yp, mod3, w_out[layer], b_out[layer], ln_g[layer],
                               ln_b[layer], alpha)
    return h


def kernel(x, c, w_ada, b_ada, w_in, b_in, conv_w, conv_b, filt_w1, filt_b1, filt_w_inner,
           filt_b_inner, filt_w_out, filt_freq, hyena_bias, pool_w, pool_scale, w_out, b_out,
           ln_g, ln_b):
    return _forward(x, c, w_ada, b_ada, w_in, b_in, conv_w, conv_b, filt_w1, filt_b1,
                    filt_w_inner, filt_b_inner, filt_w_out, filt_freq, hyena_bias, pool_w,
                    pool_scale, w_out, b_out, ln_g, ln_b, n2len=MINOR_LEN)
```

```python
import functools
import math

import jax
import jax.numpy as jnp
import numpy as np
from jax import lax
from jax.experimental import pallas as pl
from jax.experimental.pallas import tpu as pltpu

F32 = jnp.float32
BF16 = jnp.bfloat16
HIGHEST = lax.Precision.HIGHEST

POOL_WINDOWS = (2, 4, 8, 16)
LN_EPS = 1e-5
DECAY_TARGET = 1e-2
FAST_DECAY_PCT = 0.3
SLOW_DECAY_PCT = 1.5

LANES = 128
SUBLANES = 8
HALO = 16
OUT_SLABS = 8
BAND_ROWS = 128
BAND_SPAN = 256
MINOR_LEN = 128
FILTER_STEPS = (2, 4)
CONV_STEPS = (2, 2, 4)
VMEM_LIMIT = 56 * 1024 * 1024
CONV_VMEM_LIMIT = 60 * 1024 * 1024


def _silu(x):
    h = 0.5 * x
    return h + h * jnp.tanh(h)


@functools.lru_cache(maxsize=None)
def _small_dft_tables(seq, n2len):
    n1h = seq // n2len
    n_fft = 2 * seq
    k1 = np.arange(n1h, dtype=np.int64)
    odd = 2 * k1 + 1
    alpha = ((odd[:, None] * k1[None, :]) % (4 * n1h)) * (2.0 * math.pi / (4 * n1h))
    n2e = np.arange(n2len + 1, dtype=np.int64)
    beta = ((n2e[:, None] * odd[None, :]) % (2 * n_fft)) * (2.0 * math.pi / (2 * n_fft))
    n2i = np.arange(n2len, dtype=np.int64)
    phi = ((n2i[:, None] * n2i[None, :]) % n2len) * (2.0 * math.pi / n2len)
    cm, sm = np.cos(phi), np.sin(phi)
    h2 = np.block([[cm, sm], [-sm, cm]]).astype(BF16)
    h2i = np.block([[cm, -sm], [sm, cm]]).astype(BF16)
    f32 = lambda a: a.astype(np.float32)
    return (f32(np.cos(alpha)), f32(np.sin(alpha)), f32(np.cos(beta)), f32(np.sin(beta)),
            h2, h2i)


def _dft_tables(seq, n2len):
    ca, sa, cb, sb, h2, h2i = _small_dft_tables(seq, n2len)
    ca, sa = ca[None], sa[None]
    cb, sb = jnp.asarray(cb)[:, :, None], jnp.asarray(sb)[:, :, None]
    cos_t = ca * cb[:-1] - sa * sb[:-1]
    sin_t = sa * cb[:-1] + ca * sb[:-1]
    g_fwd = jnp.concatenate([cos_t, -sin_t], axis=1).astype(BF16)
    cos_r = ca * cb[1:] - sa * sb[1:]
    sin_r = sa * cb[1:] + ca * sb[1:]
    g_rev = jnp.concatenate([cos_r, sin_r], axis=1).astype(BF16)
    return g_fwd, g_rev, jnp.asarray(h2), jnp.asarray(h2i)


@functools.lru_cache(maxsize=None)
def _packed_pos_features(seq, emb, tile):
    hl = LANES // 2
    half = tile // 2
    bands = (emb - 1) // 2
    row = np.arange((seq // tile) * half, dtype=np.int64)[:, None]
    lane = np.arange(LANES, dtype=np.int64)[None, :]
    pos = ((row // half) * tile + (lane // hl) * half + row % half).astype(np.float64)
    feat = np.broadcast_to(lane % hl, pos.shape)
    t = pos / (seq - 1)
    w = (2.0 * math.pi / seq) * pos
    f = np.linspace(1e-4, bands - 1, bands)[(feat - 1) % bands]
    z = np.where(feat == 0, t,
                 np.where(feat <= bands, np.cos(f * w),
                          np.where(feat <= 2 * bands, -np.sin(f * w), 0.0)))
    return z.astype(np.float32)


def _mod_kernel(c_ref, w_ref, b_ref, o_ref):
    s = _silu(c_ref[...])
    o_ref[...] = jnp.dot(s, w_ref[...], precision=HIGHEST,
                         preferred_element_type=F32) + b_ref[...]


def _modulation(c, w_ada, b_ada):
    bsz, d = c.shape
    n_out = w_ada.shape[1]
    return pl.pallas_call(
        _mod_kernel,
        grid=(n_out // d,),
        in_specs=[pl.BlockSpec((bsz, d), lambda j: (0, 0)),
                  pl.BlockSpec((d, d), lambda j: (0, j)),
                  pl.BlockSpec((1, d), lambda j: (0, j))],
        out_specs=pl.BlockSpec((bsz, d), lambda j: (0, j)),
        out_shape=jax.ShapeDtypeStruct((bsz, n_out), F32),
        name="adaln_mod",
    )(c, w_ada, b_ada.reshape(1, n_out))


def _store_permuted(o_ref, val, n2len, first=0):
    for cb in range(val.shape[1] // LANES):
        for j in range(val.shape[0] // n2len):
            o_ref[cb, pl.ds(first + j, n2len, stride=SUBLANES), :] = val[
                j * n2len:(j + 1) * n2len, cb * LANES:(cb + 1) * LANES]


def _load_minor(x_ref, j):
    blk = x_ref[:, j * SUBLANES:(j + 1) * SUBLANES, :]
    return blk.reshape(blk.shape[0] * SUBLANES, blk.shape[2])


def _band_window(b, tile):
    last = tile + 2 * HALO - BAND_SPAN
    start = min(b * BAND_ROWS, last)
    return start, 0 if start == b * BAND_ROWS else 1


@functools.lru_cache(maxsize=None)
def _band_matrices(tile):
    t = np.arange(BAND_ROWS)[:, None]
    k = np.arange(BAND_SPAN)[None, :]
    mats = []
    for win in POOL_WINDOWS:
        half = win // 2
        per_variant = []
        for b in (0, tile // BAND_ROWS - 1):
            start, _ = _band_window(b, tile)
            rel = k + start - HALO - (b * BAND_ROWS + t)
            per_variant.append(((rel >= -half) & (rel < half)).astype(BF16))
        mats.append(np.stack(per_variant))
    return np.stack(mats)


def _proj_kernel(xp_ref, xc_ref, xn_ref, mod_ref, w_ref, b_ref, cw_ref, cb_ref, pw_ref,
                 ps_ref, band_ref, v_ref, x1_ref, x2_ref, yp_ref, p_scr, q_scr,
                 *, tile, n2len, ch, seq):
    i = pl.program_id(1)
    nt = pl.num_programs(1)
    shift = mod_ref[0:1, :]
    scale1 = 1.0 + mod_ref[1:2, :]
    xe = jnp.concatenate([xp_ref[...], xc_ref[...], xn_ref[...]], axis=0)
    ue = (xe * scale1 + shift).astype(BF16)
    uc = ue[HALO:HALO + tile, :]

    def zero_outside_sequence(p):
        return jnp.concatenate([jnp.where(i > 0, p[:HALO], 0.0), p[HALO:HALO + tile],
                                jnp.where(i < nt - 1, p[HALO + tile:], 0.0)], axis=0)

    hg = jnp.dot(uc, w_ref[:, 3 * ch:4 * ch], preferred_element_type=F32) + b_ref[:, 3 * ch:4 * ch]
    gate = _silu(hg)

    for k, o_ref in enumerate((v_ref, x1_ref, x2_ref)):
        cols = slice(k * ch, (k + 1) * ch)
        p = jnp.dot(ue, w_ref[:, cols], preferred_element_type=F32) + b_ref[:, cols]
        p_scr[...] = zero_outside_sequence(p)
        s = (cb_ref[:, cols]
             + cw_ref[0:1, cols] * p_scr[HALO - 1:HALO - 1 + tile, :]
             + cw_ref[1:2, cols] * p_scr[HALO:HALO + tile, :]
             + cw_ref[2:3, cols] * p_scr[HALO + 1:HALO + 1 + tile, :])
        if k == 2:
            s = s * gate
        _store_permuted(o_ref, s, n2len)

    dp = ps_ref.shape[1]
    c0 = 4 * ch
    pin = jnp.dot(ue, w_ref[:, c0:c0 + dp], preferred_element_type=F32) + b_ref[:, c0:c0 + dp]
    pin = zero_outside_sequence(pin)
    q_scr[...] = pin.astype(BF16)
    pin_c = pin[HALO:HALO + tile, :]
    pgate = jnp.dot(uc, w_ref[:, c0 + dp:c0 + 2 * dp], preferred_element_type=F32) + b_ref[:, c0 + dp:c0 + 2 * dp]
    pos = i * tile + lax.broadcasted_iota(jnp.int32, (tile, 1), 0)
    pg = dp // len(POOL_WINDOWS)
    groups = []
    for g, win in enumerate(POOL_WINDOWS):
        lanes = slice(g * pg, (g + 1) * pg)
        half = win // 2
        sums = []
        for b in range(tile // BAND_ROWS):
            start, variant = _band_window(b, tile)
            sums.append(jnp.dot(band_ref[g, variant], q_scr[start:start + BAND_SPAN, lanes],
                                preferred_element_type=F32))
        acc = jnp.concatenate(sums, axis=0)
        cnt = (jnp.minimum(pos + half, seq) - jnp.maximum(pos - half, 0)).astype(F32)
        diff = acc / cnt - pin_c[:, lanes]
        groups.append(jnp.dot(diff.astype(BF16), pw_ref[g], preferred_element_type=F32))
    yp = jnp.concatenate(groups, axis=1) * ps_ref[...] * _silu(pgate)
    yp_ref[...] = yp.astype(yp_ref.dtype)


def _input_projection(x, mod3, w_in, b_in, conv_w, conv_b, pool_w, pool_scale, n2len):
    bsz, seq, d = x.shape
    ch = conv_w.shape[1] // 3
    dp = pool_scale.shape[0]
    n1h = seq // n2len
    tile = SUBLANES * n2len
    nt = seq // tile
    hb = tile // HALO
    n_proj = w_in.shape[1]
    cbn = ch // LANES
    assert max(POOL_WINDOWS) // 2 <= HALO and dp == ch and tile % BAND_ROWS == 0
    band = jnp.asarray(_band_matrices(tile))
    perm =jax.ShapeDtypeStruct((bsz, n1h // SUBLANES, cbn, tile, LANES), F32)
    perm_spec = pl.BlockSpec((None, None, cbn, tile, LANES), lambda b, i: (b, i, 0, 0, 0))
    full = lambda shape: pl.BlockSpec(shape, lambda b, i: (0,) * len(shape))
    kern = functools.partial(_proj_kernel, tile=tile, n2len=n2len, ch=ch, seq=seq)
    return pl.pallas_call(
        kern,
        grid=(bsz, nt),
        in_specs=[
            pl.BlockSpec((None, HALO, d), lambda b, i: (b, jnp.maximum(i * hb - 1, 0), 0)),
            pl.BlockSpec((None, tile, d), lambda b, i: (b, i, 0)),
            pl.BlockSpec((None, HALO, d), lambda b, i: (b, jnp.minimum((i + 1) * hb, seq // HALO - 1), 0)),
            pl.BlockSpec((None, 3, d), lambda b, i: (b, 0, 0)),
            full((d, n_proj)), full((1, n_proj)), full((3, 3 * ch)), full((1, 3 * ch)),
            full(pool_w.shape), full((1, dp)), full(band.shape),
        ],
        out_specs=[perm_spec, perm_spec, perm_spec,
                   pl.BlockSpec((None, tile, dp), lambda b, i: (b, i, 0))],
        out_shape=[perm, perm, perm, jax.ShapeDtypeStruct((bsz, seq, dp), BF16)],
        scratch_shapes=[pltpu.VMEM((tile + 2 * HALO, ch), F32),
                        pltpu.VMEM((tile + 2 * HALO, dp), BF16)],
        compiler_params=pltpu.CompilerParams(
            dimension_semantics=("parallel", "arbitrary"), vmem_limit_bytes=VMEM_LIMIT),
        name="in_proj_conv_pool",
    )(x, x, x, mod3, w_in.astype(BF16), b_in.reshape(1, n_proj), conv_w,
      conv_b.reshape(1, 3 * ch), pool_w.astype(BF16), pool_scale.reshape(1, dp), band)


def _filter_mlp_kernel(z_ref, w1_ref, b1_ref, wi_ref, bi_ref, fr_ref, wo_ref, ad_ref,
                       of_ref, or_ref, *, n2len):
    z = z_ref[...]
    fr = fr_ref[...]
    h = jnp.sin(fr * (jnp.dot(z.astype(BF16), w1_ref[...],
                              preferred_element_type=F32) + b1_ref[...]))
    for l in range(wi_ref.shape[0]):
        h = jnp.sin(fr * (jnp.dot(h.astype(BF16), wi_ref[l],
                                  preferred_element_type=F32) + bi_ref[l:l + 1, :]))
    hb = h.astype(BF16)
    half_lanes = LANES // 2
    n1_half = z.shape[0] // n2len
    for half in range(2):
        t = z[:, half * half_lanes:half * half_lanes + 1]
        decay = jnp.exp(-t * ad_ref[...])
        for d, o_ref in enumerate((of_ref, or_ref)):
            k = jnp.dot(hb, wo_ref[d, half], preferred_element_type=F32)
            _store_permuted(o_ref, k * decay, n2len, first=half * n1_half)


def _block_diag2(w):
    zero = jnp.zeros_like(w)
    return jnp.concatenate([jnp.concatenate([w, zero], axis=-1),
                            jnp.concatenate([zero, w], axis=-1)], axis=-2)


def _filter_taps(seq, n2len, w1, b1, w_inner, b_inner, w_out, freq, n_ch, order):
    emb, hid = w1.shape
    n_inner = w_inner.shape[0]
    oc = order * n_ch
    hl = LANES // 2
    assert emb <= hl and hid <= hl
    tile = SUBLANES * n2len
    half = tile // 2
    groups = seq // tile
    zp = jnp.asarray(_packed_pos_features(seq, emb, tile))
    ph = hl - hid
    w1d = _block_diag2(jnp.pad(w1, ((0, hl - emb), (0, ph))))
    b1d = jnp.tile(jnp.pad(b1, (0, ph)), 2).reshape(1, LANES)
    wid = _block_diag2(jnp.pad(w_inner, ((0, 0), (0, ph), (0, ph))))
    bid = jnp.tile(jnp.pad(b_inner, ((0, 0), (0, ph))), (1, 2))
    frd = jnp.tile(jnp.pad(freq, (0, ph), constant_values=1.0), 2).reshape(1, LANES)
    wo = jnp.transpose(w_out.reshape(hid, order, 2, n_ch), (2, 0, 1, 3)).reshape(2, hid, oc)
    wo = jnp.pad(wo, ((0, 0), (0, ph), (0, 0)))
    zero = jnp.zeros_like(wo)
    wo4 = jnp.stack([jnp.concatenate([wo, zero], axis=1),
                     jnp.concatenate([zero, wo], axis=1)], axis=1)
    min_decay = math.log(DECAY_TARGET) / SLOW_DECAY_PCT
    max_decay = math.log(DECAY_TARGET) / FAST_DECAY_PCT
    absdelta = jnp.abs(jnp.linspace(min_decay, max_decay, n_ch, dtype=F32))
    absdelta = jnp.tile(absdelta, order).reshape(1, oc)

    full = lambda shape: pl.BlockSpec(shape, lambda i: (0,) * len(shape))
    kern = functools.partial(_filter_mlp_kernel, n2len=n2len)
    taps = jax.ShapeDtypeStruct((groups, oc // LANES, tile, LANES), F32)
    taps_spec = pl.BlockSpec((None, oc // LANES, tile, LANES), lambda i: (i, 0, 0, 0))
    return pl.pallas_call(
        kern,
        grid=(groups,),
        in_specs=[
            pl.BlockSpec((half, LANES), lambda i: (i, 0)),
            full((LANES, LANES)), full((1, LANES)), full((n_inner, LANES, LANES)),
            full((n_inner, LANES)), full((1, LANES)), full((2, 2, LANES, oc)), full((1, oc)),
        ],
        out_specs=[taps_spec, taps_spec],
        out_shape=[taps, taps],
        compiler_params=pltpu.CompilerParams(
            dimension_semantics=("arbitrary",), vmem_limit_bytes=VMEM_LIMIT),
        name="filter_mlp",
    )(zp, w1d.astype(BF16), b1d, wid.astype(BF16), bid, frd, wo4.astype(BF16), absdelta)


def _resident(shape):
    return pl.BlockSpec(shape, lambda *_: (0,) * len(shape), pipeline_mode=pl.Buffered(1))


def _spectrum_views(a_scr, chunk, ck1, n1h, pitch):
    re0 = pl.multiple_of(chunk * (ck1 * pitch), SUBLANES)
    im0 = pl.multiple_of(chunk * (ck1 * pitch) + n1h * pitch, SUBLANES)
    return a_scr.at[pl.ds(re0, ck1 * pitch)], a_scr.at[pl.ds(im0, ck1 * pitch)]


def _spectrum_pair(a_re, a_im, j, n2len, pitch):
    def one(jj):
        return jnp.concatenate([a_re[jj * pitch:jj * pitch + n2len, :],
                                a_im[jj * pitch:jj * pitch + n2len, :]], axis=0)
    return jnp.concatenate([one(j), one(j + 1)], axis=1).astype(BF16)


def _filter_fft_kernel(hf_ref, hr_ref, g_ref, gr_ref, h2_ref, o_ref, a_scr,
                       *, ns, cn2, ck1, n1h, n2len, pitch):
    s = pl.program_id(1)

    @pl.when(s < ns)
    def _():
        for j in range(cn2):
            jr = cn2 - 1 - j
            m = s * cn2 + j
            g = jnp.concatenate([g_ref[m], gr_ref[n2len - 1 - m]], axis=1)
            x = jnp.concatenate([_load_minor(hf_ref, j), _load_minor(hr_ref, jr)], axis=0)
            r = jnp.dot(g, x.astype(BF16), preferred_element_type=F32)
            a_scr[pl.ds(m, 2 * n1h, stride=pitch), :] = r

    @pl.when(s >= ns)
    def _():
        a_re, a_im = _spectrum_views(a_scr, s - ns, ck1, n1h, pitch)
        for j in range(0, ck1, 2):
            rows = _spectrum_pair(a_re, a_im, j, n2len, pitch)
            u = jnp.dot(h2_ref[...], rows, preferred_element_type=F32)
            u = (u * (1.0 / (n1h * n2len))).astype(o_ref.dtype)
            o_ref[j] = u[:, :LANES]
            o_ref[j + 1] = u[:, LANES:]


def _filter_spectrum(taps_f, taps_r, g_fwd, g_rev, h2, n2len):
    groups, ocb, _, _ = taps_f.shape
    oc = ocb * LANES
    n1h = groups * SUBLANES
    ns, ns2 = FILTER_STEPS
    cn2, ck1 = n2len // ns, n1h // ns2
    pitch = n2len + SUBLANES
    kern = functools.partial(_filter_fft_kernel, ns=ns, cn2=cn2, ck1=ck1, n1h=n1h,
                             n2len=n2len, pitch=pitch)
    return pl.pallas_call(
        kern,
        grid=(oc // LANES, ns + ns2),
        in_specs=[
            pl.BlockSpec((groups, None, cn2 * SUBLANES, LANES),
                         lambda c, s: (0, c, jnp.minimum(s, ns - 1), 0)),
            pl.BlockSpec((groups, None, cn2 * SUBLANES, LANES),
                         lambda c, s: (0, c, jnp.maximum(ns - 1 - s, 0), 0)),
            _resident(g_fwd.shape), _resident(g_rev.shape), _resident(h2.shape),
        ],
        out_specs=pl.BlockSpec((ck1, 2 * n2len, LANES), lambda c, s: (jnp.maximum(s - ns, 0), 0, c)),
        out_shape=jax.ShapeDtypeStruct((n1h, 2 * n2len, oc), BF16),
        scratch_shapes=[pltpu.VMEM((2 * n1h * pitch, LANES), F32)],
        compiler_params=pltpu.CompilerParams(
            dimension_semantics=("parallel", "arbitrary"), vmem_limit_bytes=VMEM_LIMIT),
        name="filter_spectrum",
    )(taps_f, taps_r, g_fwd, g_rev, h2)


def _long_conv_kernel(vf_ref, g_ref, kf_ref, h2_ref, h2i_ref, xm_ref, hb_ref, o_ref,
                      a_scr, keep_scr, *, steps, n1h, n2len, pitch):
    ns1, ns2, ns3 = steps
    cn2a, ck1, cn2 = n2len // ns1, n1h // ns2, n2len // ns3
    s = pl.program_id(2)

    @pl.when(s < ns1)
    def _():
        base = s * cn2a
        for j in range(cn2a):
            x = _load_minor(vf_ref, j)
            keep_scr[base + j] = x
            r = jnp.dot(g_ref[base + j], x.astype(BF16), preferred_element_type=F32)
            a_scr[pl.ds(base + j, 2 * n1h, stride=pitch), :] = r

    @pl.when((s >= ns1) & (s < ns1 + ns2))
    def _():
        a_re, a_im = _spectrum_views(a_scr, s - ns1, ck1, n1h, pitch)
        for j in range(0, ck1, 2):
            rows = _spectrum_pair(a_re, a_im, j, n2len, pitch)
            u = jnp.dot(h2_ref[...], rows, preferred_element_type=F32)
            ur, ui = u[:n2len], u[n2len:]
            kf = jnp.concatenate([kf_ref[j], kf_ref[j + 1]], axis=1).astype(F32)
            kr, ki = kf[:n2len], kf[n2len:]
            y = jnp.concatenate([ur * kr - ui * ki, ur * ki + ui * kr], axis=0)
            bb = jnp.dot(h2i_ref[...], y.astype(BF16), preferred_element_type=F32)
            for jj in range(2):
                lanes = slice(jj * LANES, (jj + 1) * LANES)
                a_re[(j + jj) * pitch:(j + jj) * pitch + n2len, :] = bb[:n2len, lanes]
                a_im[(j + jj) * pitch:(j + jj) * pitch + n2len, :] = bb[n2len:, lanes]

    @pl.when(s >= ns1 + ns2)
    def _():
        base = (s - ns1 - ns2) * cn2
        for j in range(cn2):
            rows = a_scr[pl.ds(base + j, 2 * n1h, stride=pitch), :]
            y = lax.dot_general(g_ref[base + j], rows.astype(BF16), (((0,), (0,)), ((), ())),
                                preferred_element_type=F32)
            z = _load_minor(xm_ref, j) * (y + hb_ref[...] * keep_scr[base + j])
            o_ref[:, j * SUBLANES:(j + 1) * SUBLANES, :] = z.reshape(
                n1h // SUBLANES, SUBLANES, z.shape[1])


def _long_conv(u_perm, mult_perm, kf, hbias, order_idx, tables):
    g_fwd, _, h2, h2i = tables
    bsz, groups, cbn, tile, _ = u_perm.shape
    n2len = tile // SUBLANES
    n1h = groups * SUBLANES
    steps = CONV_STEPS
    ns1, ns2, ns3 = steps
    pitch = n2len + SUBLANES

    def late(s):
        return jnp.clip(s - ns1 - ns2, 0, ns3 - 1)

    data_early = pl.BlockSpec((None, groups, None, (n2len // ns1) * SUBLANES, LANES),
                              lambda c, b, s: (b, 0, c, jnp.minimum(s, ns1 - 1), 0))
    data_late = pl.BlockSpec((None, groups, None, (n2len // ns3) * SUBLANES, LANES),
                             lambda c, b, s: (b, 0, c, late(s), 0))
    in_specs = [
        data_early,
        _resident(g_fwd.shape),
        pl.BlockSpec((n1h // ns2, 2 * n2len, LANES),
                     lambda c, b, s: (jnp.clip(s - ns1, 0, ns2 - 1), 0, order_idx * cbn + c)),
        _resident(h2.shape), _resident(h2i.shape),
        data_late,
        pl.BlockSpec((None, 1, LANES), lambda c, b, s: (order_idx, 0, c)),
    ]
    kern = functools.partial(_long_conv_kernel, steps=steps, n1h=n1h, n2len=n2len,
                             pitch=pitch)
    return pl.pallas_call(
        kern,
        grid=(cbn, bsz, sum(steps)),
        in_specs=in_specs,
        out_specs=data_late,
        out_shape=jax.ShapeDtypeStruct(u_perm.shape, F32),
        scratch_shapes=[pltpu.VMEM((2 * n1h * pitch, LANES), F32),
                        pltpu.VMEM((n2len, n1h, LANES), F32)],
        compiler_params=pltpu.CompilerParams(
            dimension_semantics=("parallel", "parallel", "arbitrary"),
            vmem_limit_bytes=CONV_VMEM_LIMIT),
        name="long_conv",
    )(u_perm, g_fwd, kf, h2, h2i, mult_perm, hbias)


def _out_kernel(x_ref, yh_ref, yp_ref, mod_ref, w_ref, b_ref, g_ref, beta_ref, o_ref,
                *, n2len, alpha):
    cbn = yh_ref.shape[0]
    for j0 in range(0, SUBLANES, OUT_SLABS):
        rows = slice(j0 * n2len, (j0 + OUT_SLABS) * n2len)
        yh = jnp.concatenate(
            [jnp.concatenate([yh_ref[cb, pl.ds(j, n2len, stride=SUBLANES), :]
                              for cb in range(cbn)], axis=1)
             for j in range(j0, j0 + OUT_SLABS)], axis=0)
        y = jnp.concatenate([yh.astype(BF16), yp_ref[rows, :]], axis=1)
        acc = jnp.dot(y, w_ref[...], preferred_element_type=F32)
        h = alpha * x_ref[rows, :] + mod_ref[2:3, :] * (acc + b_ref[...])
        mu = jnp.mean(h, axis=-1, keepdims=True)
        hc = h - mu
        var = jnp.mean(hc * hc, axis=-1, keepdims=True)
        o_ref[rows, :] = hc * lax.rsqrt(var + LN_EPS) * g_ref[...] + beta_ref[...]


def _output_projection(x, yh_perm, yp, mod3, w_out, b_out, ln_g, ln_b, alpha):
    bsz, seq, d = x.shape
    _, _, cbn, tile, _ = yh_perm.shape
    ch = cbn * LANES
    n2len = tile // SUBLANES
    dp = yp.shape[2]
    full = lambda shape: pl.BlockSpec(shape, lambda b, i: (0,) * len(shape))
    kern = functools.partial(_out_kernel, n2len=n2len, alpha=alpha)
    return pl.pallas_call(
        kern,
        grid=(bsz, seq // tile),
        in_specs=[
            pl.BlockSpec((None, tile, d), lambda b, i: (b, i, 0)),
            pl.BlockSpec((None, None, cbn, tile, LANES), lambda b, i: (b, i, 0, 0, 0)),
            pl.BlockSpec((None, tile, dp), lambda b, i: (b, i, 0)),
            pl.BlockSpec((None, 3, d), lambda b, i: (b, 0, 0)),
            full((ch + dp, d)), full((1, d)), full((1, d)), full((1, d)),
        ],
        out_specs=pl.BlockSpec((None, tile, d), lambda b, i: (b, i, 0)),
        out_shape=jax.ShapeDtypeStruct(x.shape, x.dtype),
        compiler_params=pltpu.CompilerParams(
            dimension_semantics=("parallel", "arbitrary"), vmem_limit_bytes=VMEM_LIMIT),
        name="out_proj_deepnorm",
    )(x, yh_perm, yp, mod3, w_out.astype(BF16), b_out.reshape(1, d), ln_g.reshape(1, d),
      ln_b.reshape(1, d))


def _forward(x, c, w_ada, b_ada, w_in, b_in, conv_w, conv_b, filt_w1, filt_b1, filt_w_inner,
             filt_b_inner, filt_w_out, filt_freq, hyena_bias, pool_w, pool_scale, w_out,
             b_out, ln_g, ln_b, *, n2len):
    bsz, seq, d = x.shape
    depth = w_ada.shape[0]
    order, n_ch = hyena_bias.shape[1], hyena_bias.shape[2]
    alpha = (2.0 * depth) ** 0.25
    tables = _dft_tables(seq, n2len)
    h = x
    for layer in range(depth):
        mod3 = _modulation(c, w_ada[layer], b_ada[layer]).reshape(bsz, 3, d)
        v, x1, x2g, yp = _input_projection(
            h, mod3, w_in[layer], b_in[layer], conv_w[layer], conv_b[layer], pool_w[layer],
            pool_scale[layer], n2len)
        taps_f, taps_r = _filter_taps(
            seq, n2len, filt_w1[layer], filt_b1[layer], filt_w_inner[layer],
            filt_b_inner[layer], filt_w_out[layer], filt_freq[layer], n_ch, order)
        kf = _filter_spectrum(taps_f, taps_r, tables[0], tables[1], tables[2], n2len)
        hbias = hyena_bias[layer].reshape(order, 1, n_ch)
        z = v
        mults = (x1, x2g)
        for o in range(order):
            z = _long_conv(z, mults[o], kf, hbias, o, tables)
        h = _output_projection(h, z, yp, mod3, w_out[layer], b_out[layer], ln_g[layer],
                               ln_b[layer], alpha)
    return h


def kernel(x, c, w_ada, b_ada, w_in, b_in, conv_w, conv_b, filt_w1, filt_b1, filt_w_inner,
           filt_b_inner, filt_w_out, filt_freq, hyena_bias, pool_w, pool_scale, w_out, b_out,
           ln_g, ln_b):
    return _forward(x, c, w_ada, b_ada, w_in, b_in, conv_w, conv_b, filt_w1, filt_b1,
                    filt_w_inner, filt_b_inner, filt_w_out, filt_freq, hyena_bias, pool_w,
                    pool_scale, w_out, b_out, ln_g, ln_b, n2len=MINOR_LEN)
```

```python
import functools
import math

import jax
import jax.numpy as jnp
import numpy as np
from jax import lax
from jax.experimental import pallas as pl
from jax.experimental.pallas import tpu as pltpu

F32 = jnp.float32
BF16 = jnp.bfloat16
HIGHEST = lax.Precision.HIGHEST

POOL_WINDOWS = (2, 4, 8, 16)
LN_EPS = 1e-5
DECAY_TARGET = 1e-2
FAST_DECAY_PCT = 0.3
SLOW_DECAY_PCT = 1.5

LANES = 128
SUBLANES = 8
HALO = 16
OUT_SLABS = 8
BAND_ROWS = 128
BAND_SPAN = 256
MINOR_LEN = 128
FILTER_STEPS = (2, 4)
CONV_STEPS = (2, 2, 4)
VMEM_LIMIT = 56 * 1024 * 1024
CONV_VMEM_LIMIT = 60 * 1024 * 1024


def _silu(x):
    h = 0.5 * x
    return h + h * jnp.tanh(h)


@functools.lru_cache(maxsize=None)
def _small_dft_tables(seq, n2len):
    n1h = seq // n2len
    n_fft = 2 * seq
    k1 = np.arange(n1h, dtype=np.int64)
    odd = 2 * k1 + 1
    alpha = ((odd[:, None] * k1[None, :]) % (4 * n1h)) * (2.0 * math.pi / (4 * n1h))
    n2e = np.arange(n2len + 1, dtype=np.int64)
    beta = ((n2e[:, None] * odd[None, :]) % (2 * n_fft)) * (2.0 * math.pi / (2 * n_fft))
    n2i = np.arange(n2len, dtype=np.int64)
    phi = ((n2i[:, None] * n2i[None, :]) % n2len) * (2.0 * math.pi / n2len)
    cm, sm = np.cos(phi), np.sin(phi)
    h2 = np.block([[cm, sm], [-sm, cm]]).astype(BF16)
    h2i = np.block([[cm, -sm], [sm, cm]]).astype(BF16)
    f32 = lambda a: a.astype(np.float32)
    return (f32(np.cos(alpha)), f32(np.sin(alpha)), f32(np.cos(beta)), f32(np.sin(beta)),
            h2, h2i)


def _dft_tables(seq, n2len):
    ca, sa, cb, sb, h2, h2i = _small_dft_tables(seq, n2len)
    ca, sa = ca[None], sa[None]
    cb, sb = jnp.asarray(cb)[:, :, None], jnp.asarray(sb)[:, :, None]
    cos_t = ca * cb[:-1] - sa * sb[:-1]
    sin_t = sa * cb[:-1] + ca * sb[:-1]
    g_fwd = jnp.concatenate([cos_t, -sin_t], axis=1).astype(BF16)
    cos_r = ca * cb[1:] - sa * sb[1:]
    sin_r = sa * cb[1:] + ca * sb[1:]
    g_rev = jnp.concatenate([cos_r, sin_r], axis=1).astype(BF16)
    return g_fwd, g_rev, jnp.asarray(h2), jnp.asarray(h2i)


@functools.lru_cache(maxsize=None)
def _packed_pos_features(seq, emb, tile):
    hl = LANES // 2
    half = tile // 2
    bands = (emb - 1) // 2
    row = np.arange((seq // tile) * half, dtype=np.int64)[:, None]
    lane = np.arange(LANES, dtype=np.int64)[None, :]
    pos = ((row // half) * tile + (lane // hl) * half + row % half).astype(np.float64)
    feat = np.broadcast_to(lane % hl, pos.shape)
    t = pos / (seq - 1)
    w = (2.0 * math.pi / seq) * pos
    f = np.linspace(1e-4, bands - 1, bands)[(feat - 1) % bands]
    z = np.where(feat == 0, t,
                 np.where(feat <= bands, np.cos(f * w),
                          np.where(feat <= 2 * bands, -np.sin(f * w), 0.0)))
    return z.astype(np.float32)


def _mod_kernel(c_ref, w_ref, b_ref, o_ref):
    s = _silu(c_ref[...])
    o_ref[...] = jnp.dot(s, w_ref[...], precision=HIGHEST,
                         preferred_element_type=F32) + b_ref[...]


def _modulation(c, w_ada, b_ada):
    bsz, d = c.shape
    n_out = w_ada.shape[1]
    return pl.pallas_call(
        _mod_kernel,
        grid=(n_out // d,),
        in_specs=[pl.BlockSpec((bsz, d), lambda j: (0, 0)),
                  pl.BlockSpec((d, d), lambda j: (0, j)),
                  pl.BlockSpec((1, d), lambda j: (0, j))],
        out_specs=pl.BlockSpec((bsz, d), lambda j: (0, j)),
        out_shape=jax.ShapeDtypeStruct((bsz, n_out), F32),
        name="adaln_mod",
    )(c, w_ada, b_ada.reshape(1, n_out))


def _store_permuted(o_ref, val, n2len, first=0):
    for cb in range(val.shape[1] // LANES):
        for j in range(val.shape[0] // n2len):
            o_ref[cb, pl.ds(first + j, n2len, stride=SUBLANES), :] = val[
                j * n2len:(j + 1) * n2len, cb * LANES:(cb + 1) * LANES]


def _load_minor(x_ref, j):
    blk = x_ref[:, j * SUBLANES:(j + 1) * SUBLANES, :]
    return blk.reshape(blk.shape[0] * SUBLANES, blk.shape[2])


def _band_window(b, tile):
    last = tile + 2 * HALO - BAND_SPAN
    start = min(b * BAND_ROWS, last)
    return start, 0 if start == b * BAND_ROWS else 1


@functools.lru_cache(maxsize=None)
def _band_matrices(tile):
    t = np.arange(BAND_ROWS)[:, None]
    k = np.arange(BAND_SPAN)[None, :]
    mats = []
    for win in POOL_WINDOWS:
        half = win // 2
        per_variant = []
        for b in (0, tile // BAND_ROWS - 1):
            start, _ = _band_window(b, tile)
            rel = k + start - HALO - (b * BAND_ROWS + t)
            per_variant.append(((rel >= -half) & (rel < half)).astype(BF16))
        mats.append(np.stack(per_variant))
    return np.stack(mats)


def _proj_kernel(xp_ref, xc_ref, xn_ref, mod_ref, w_ref, b_ref, cw_ref, cb_ref, pw_ref,
                 ps_ref, band_ref, v_ref, x1_ref, x2_ref, yp_ref, p_scr, q_scr,
                 *, tile, n2len, ch, seq):
    i = pl.program_id(1)
    nt = pl.num_programs(1)
    shift = mod_ref[0:1, :]
    scale1 = 1.0 + mod_ref[1:2, :]
    xe = jnp.concatenate([xp_ref[...], xc_ref[...], xn_ref[...]], axis=0)
    ue = (xe * scale1 + shift).astype(BF16)
    uc = ue[HALO:HALO + tile, :]

    def zero_outside_sequence(p):
        return jnp.concatenate([jnp.where(i > 0, p[:HALO], 0.0), p[HALO:HALO + tile],
                                jnp.where(i < nt - 1, p[HALO + tile:], 0.0)], axis=0)

    hg = jnp.dot(uc, w_ref[:, 3 * ch:4 * ch], preferred_element_type=F32) + b_ref[:, 3 * ch:4 * ch]
    gate = _silu(hg)

    for k, o_ref in enumerate((v_ref, x1_ref)):
        cols = slice(k * ch, (k + 1) * ch)
        p = jnp.dot(uc, w_ref[:, cols], preferred_element_type=F32) + b_ref[:, cols]
        _store_permuted(o_ref, p, n2len)

    cols = slice(2 * ch, 3 * ch)
    p = jnp.dot(ue, w_ref[:, cols], preferred_element_type=F32) + b_ref[:, cols]
    p_scr[...] = zero_outside_sequence(p)
    s = (cb_ref[:, cols]
         + cw_ref[0:1, cols] * p_scr[HALO - 1:HALO - 1 + tile, :]
         + cw_ref[1:2, cols] * p_scr[HALO:HALO + tile, :]
         + cw_ref[2:3, cols] * p_scr[HALO + 1:HALO + 1 + tile, :])
    _store_permuted(x2_ref, s * gate, n2len)

    dp = ps_ref.shape[1]
    c0 = 4 * ch
    pin = jnp.dot(ue, w_ref[:, c0:c0 + dp], preferred_element_type=F32) + b_ref[:, c0:c0 + dp]
    pin = zero_outside_sequence(pin)
    q_scr[...] = pin.astype(BF16)
    pin_c = pin[HALO:HALO + tile, :]
    pgate = jnp.dot(uc, w_ref[:, c0 + dp:c0 + 2 * dp], preferred_element_type=F32) + b_ref[:, c0 + dp:c0 + 2 * dp]
    pos = i * tile + lax.broadcasted_iota(jnp.int32, (tile, 1), 0)
    pg = dp // len(POOL_WINDOWS)
    groups = []
    for g, win in enumerate(POOL_WINDOWS):
        lanes = slice(g * pg, (g + 1) * pg)
        half = win // 2
        sums = []
        for b in range(tile // BAND_ROWS):
            start, variant = _band_window(b, tile)
            sums.append(jnp.dot(band_ref[g, variant], q_scr[start:start + BAND_SPAN, lanes],
                                preferred_element_type=F32))
        acc = jnp.concatenate(sums, axis=0)
        cnt = (jnp.minimum(pos + half, seq) - jnp.maximum(pos - half, 0)).astype(F32)
        diff = acc / cnt - pin_c[:, lanes]
        groups.append(jnp.dot(diff.astype(BF16), pw_ref[g], preferred_element_type=F32))
    yp = jnp.concatenate(groups, axis=1) * ps_ref[...] * _silu(pgate)
    yp_ref[...] = yp.astype(yp_ref.dtype)


def _input_projection(x, mod3, w_in, b_in, conv_w, conv_b, pool_w, pool_scale, n2len):
    bsz, seq, d = x.shape
    ch = conv_w.shape[1] // 3
    dp = pool_scale.shape[0]
    n1h = seq // n2len
    tile = SUBLANES * n2len
    nt = seq // tile
    hb = tile // HALO
    n_proj = w_in.shape[1]
    cbn = ch // LANES
    assert max(POOL_WINDOWS) // 2 <= HALO and dp == ch and tile % BAND_ROWS == 0
    band = jnp.asarray(_band_matrices(tile))
    perm =jax.ShapeDtypeStruct((bsz, n1h // SUBLANES, cbn, tile, LANES), F32)
    perm_spec = pl.BlockSpec((None, None, cbn, tile, LANES), lambda b, i: (b, i, 0, 0, 0))
    full = lambda shape: pl.BlockSpec(shape, lambda b, i: (0,) * len(shape))
    kern = functools.partial(_proj_kernel, tile=tile, n2len=n2len, ch=ch, seq=seq)
    return pl.pallas_call(
        kern,
        grid=(bsz, nt),
        in_specs=[
            pl.BlockSpec((None, HALO, d), lambda b, i: (b, jnp.maximum(i * hb - 1, 0), 0)),
            pl.BlockSpec((None, tile, d), lambda b, i: (b, i, 0)),
            pl.BlockSpec((None, HALO, d), lambda b, i: (b, jnp.minimum((i + 1) * hb, seq // HALO - 1), 0)),
            pl.BlockSpec((None, 3, d), lambda b, i: (b, 0, 0)),
            full((d, n_proj)), full((1, n_proj)), full((3, 3 * ch)), full((1, 3 * ch)),
            full(pool_w.shape), full((1, dp)), full(band.shape),
        ],
        out_specs=[perm_spec, perm_spec, perm_spec,
                   pl.BlockSpec((None, tile, dp), lambda b, i: (b, i, 0))],
        out_shape=[perm, perm, perm, jax.ShapeDtypeStruct((bsz, seq, dp), BF16)],
        scratch_shapes=[pltpu.VMEM((tile + 2 * HALO, ch), F32),
                        pltpu.VMEM((tile + 2 * HALO, dp), BF16)],
        compiler_params=pltpu.CompilerParams(
            dimension_semantics=("parallel", "arbitrary"), vmem_limit_bytes=VMEM_LIMIT),
        name="in_proj_conv_pool",
    )(x, x, x, mod3, w_in.astype(BF16), b_in.reshape(1, n_proj), conv_w,
      conv_b.reshape(1, 3 * ch), pool_w.astype(BF16), pool_scale.reshape(1, dp), band)


def _filter_mlp_kernel(z_ref, w1_ref, b1_ref, wi_ref, bi_ref, fr_ref, wo_ref, ad_ref,
                       of_ref, or_ref, *, n2len):
    z = z_ref[...]
    fr = fr_ref[...]
    h = jnp.sin(fr * (jnp.dot(z.astype(BF16), w1_ref[...],
                              preferred_element_type=F32) + b1_ref[...]))
    for l in range(wi_ref.shape[0]):
        h = jnp.sin(fr * (jnp.dot(h.astype(BF16), wi_ref[l],
                                  preferred_element_type=F32) + bi_ref[l:l + 1, :]))
    hb = h.astype(BF16)
    half_lanes = LANES // 2
    n1_half = z.shape[0] // n2len
    for half in range(2):
        t = z[:, half * half_lanes:half * half_lanes + 1]
        decay = jnp.exp(-t * ad_ref[...])
        for d, o_ref in enumerate((of_ref, or_ref)):
            k = jnp.dot(hb, wo_ref[d, half], preferred_element_type=F32)
            _store_permuted(o_ref, k * decay, n2len, first=half * n1_half)


def _block_diag2(w):
    zero = jnp.zeros_like(w)
    return jnp.concatenate([jnp.concatenate([w, zero], axis=-1),
                            jnp.concatenate([zero, w], axis=-1)], axis=-2)


def _filter_taps(seq, n2len, w1, b1, w_inner, b_inner, w_out, freq, n_ch, order):
    emb, hid = w1.shape
    n_inner = w_inner.shape[0]
    oc = order * n_ch
    hl = LANES // 2
    assert emb <= hl and hid <= hl
    tile = SUBLANES * n2len
    half = tile // 2
    groups = seq // tile
    zp = jnp.asarray(_packed_pos_features(seq, emb, tile))
    ph = hl - hid
    w1d = _block_diag2(jnp.pad(w1, ((0, hl - emb), (0, ph))))
    b1d = jnp.tile(jnp.pad(b1, (0, ph)), 2).reshape(1, LANES)
    wid = _block_diag2(jnp.pad(w_inner, ((0, 0), (0, ph), (0, ph))))
    bid = jnp.tile(jnp.pad(b_inner, ((0, 0), (0, ph))), (1, 2))
    frd = jnp.tile(jnp.pad(freq, (0, ph), constant_values=1.0), 2).reshape(1, LANES)
    wo = jnp.transpose(w_out.reshape(hid, order, 2, n_ch), (2, 0, 1, 3)).reshape(2, hid, oc)
    wo = jnp.pad(wo, ((0, 0), (0, ph), (0, 0)))
    zero = jnp.zeros_like(wo)
    wo4 = jnp.stack([jnp.concatenate([wo, zero], axis=1),
                     jnp.concatenate([zero, wo], axis=1)], axis=1)
    min_decay = math.log(DECAY_TARGET) / SLOW_DECAY_PCT
    max_decay = math.log(DECAY_TARGET) / FAST_DECAY_PCT
    absdelta = jnp.abs(jnp.linspace(min_decay, max_decay, n_ch, dtype=F32))
    absdelta = jnp.tile(absdelta, order).reshape(1, oc)

    full = lambda shape: pl.BlockSpec(shape, lambda i: (0,) * len(shape))
    kern = functools.partial(_filter_mlp_kernel, n2len=n2len)
    taps = jax.ShapeDtypeStruct((groups, oc // LANES, tile, LANES), F32)
    taps_spec = pl.BlockSpec((None, oc // LANES, tile, LANES), lambda i: (i, 0, 0, 0))
    return pl.pallas_call(
        kern,
        grid=(groups,),
        in_specs=[
            pl.BlockSpec((half, LANES), lambda i: (i, 0)),
            full((LANES, LANES)), full((1, LANES)), full((n_inner, LANES, LANES)),
            full((n_inner, LANES)), full((1, LANES)), full((2, 2, LANES, oc)), full((1, oc)),
        ],
        out_specs=[taps_spec, taps_spec],
        out_shape=[taps, taps],
        compiler_params=pltpu.CompilerParams(
            dimension_semantics=("arbitrary",), vmem_limit_bytes=VMEM_LIMIT),
        name="filter_mlp",
    )(zp, w1d.astype(BF16), b1d, wid.astype(BF16), bid, frd, wo4.astype(BF16), absdelta)


def _resident(shape):
    return pl.BlockSpec(shape, lambda *_: (0,) * len(shape), pipeline_mode=pl.Buffered(1))


def _spectrum_views(a_scr, chunk, ck1, n1h, pitch):
    re0 = pl.multiple_of(chunk * (ck1 * pitch), SUBLANES)
    im0 = pl.multiple_of(chunk * (ck1 * pitch) + n1h * pitch, SUBLANES)
    return a_scr.at[pl.ds(re0, ck1 * pitch)], a_scr.at[pl.ds(im0, ck1 * pitch)]


def _spectrum_pair(a_re, a_im, j, n2len, pitch):
    def one(jj):
        return jnp.concatenate([a_re[jj * pitch:jj * pitch + n2len, :],
                                a_im[jj * pitch:jj * pitch + n2len, :]], axis=0)
    return jnp.concatenate([one(j), one(j + 1)], axis=1).astype(BF16)


def _filter_fft_kernel(hf_ref, hr_ref, g_ref, gr_ref, h2_ref, o_ref, a_scr,
                       *, ns, cn2, ck1, n1h, n2len, pitch):
    s = pl.program_id(1)

    @pl.when(s < ns)
    def _():
        for j in range(cn2):
            jr = cn2 - 1 - j
            m = s * cn2 + j
            g = jnp.concatenate([g_ref[m], gr_ref[n2len - 1 - m]], axis=1)
            x = jnp.concatenate([_load_minor(hf_ref, j), _load_minor(hr_ref, jr)], axis=0)
            r = jnp.dot(g, x.astype(BF16), preferred_element_type=F32)
            a_scr[pl.ds(m, 2 * n1h, stride=pitch), :] = r

    @pl.when(s >= ns)
    def _():
        a_re, a_im = _spectrum_views(a_scr, s - ns, ck1, n1h, pitch)
        for j in range(0, ck1, 2):
            rows = _spectrum_pair(a_re, a_im, j, n2len, pitch)
            u = jnp.dot(h2_ref[...], rows, preferred_element_type=F32)
            u = (u * (1.0 / (n1h * n2len))).astype(o_ref.dtype)
            o_ref[j] = u[:, :LANES]
            o_ref[j + 1] = u[:, LANES:]


def _filter_spectrum(taps_f, taps_r, g_fwd, g_rev, h2, n2len):
    groups, ocb, _, _ = taps_f.shape
    oc = ocb * LANES
    n1h = groups * SUBLANES
    ns, ns2 = FILTER_STEPS
    cn2, ck1 = n2len // ns, n1h // ns2
    pitch = n2len + SUBLANES
    kern = functools.partial(_filter_fft_kernel, ns=ns, cn2=cn2, ck1=ck1, n1h=n1h,
                             n2len=n2len, pitch=pitch)
    return pl.pallas_call(
        kern,
        grid=(oc // LANES, ns + ns2),
        in_specs=[
            pl.BlockSpec((groups, None, cn2 * SUBLANES, LANES),
                         lambda c, s: (0, c, jnp.minimum(s, ns - 1), 0)),
            pl.BlockSpec((groups, None, cn2 * SUBLANES, LANES),
                         lambda c, s: (0, c, jnp.maximum(ns - 1 - s, 0), 0)),
            _resident(g_fwd.shape), _resident(g_rev.shape), _resident(h2.shape),
        ],
        out_specs=pl.BlockSpec((ck1, 2 * n2len, LANES), lambda c, s: (jnp.maximum(s - ns, 0), 0, c)),
        out_shape=jax.ShapeDtypeStruct((n1h, 2 * n2len, oc), BF16),
        scratch_shapes=[pltpu.VMEM((2 * n1h * pitch, LANES), F32)],
        compiler_params=pltpu.CompilerParams(
            dimension_semantics=("parallel", "arbitrary"), vmem_limit_bytes=VMEM_LIMIT),
        name="filter_spectrum",
    )(taps_f, taps_r, g_fwd, g_rev, h2)


def _short_conv_slab(x_ref, lo_ref, hi_ref, taps_ref, j, count, first_chunk, last_chunk):
    cur = _load_minor(x_ref, j)
    zero_row = jnp.zeros((1, cur.shape[1]), F32)
    if j > 0:
        prev = _load_minor(x_ref, j - 1)
    else:
        prev = _load_minor(lo_ref, 0)
        prev = jnp.where(first_chunk, jnp.concatenate([zero_row, prev[:-1]], axis=0), prev)
    if j < count - 1:
        nxt = _load_minor(x_ref, j + 1)
    else:
        nxt = _load_minor(hi_ref, 0)
        nxt = jnp.where(last_chunk, jnp.concatenate([nxt[1:], zero_row], axis=0), nxt)
    return (taps_ref[3:4, :] + taps_ref[0:1, :] * prev + taps_ref[1:2, :] * cur
            + taps_ref[2:3, :] * nxt)


def _long_conv_kernel(*refs, steps, n1h, n2len, pitch, conv_in, conv_mult):
    refs = list(refs)
    vf_ref = refs.pop(0)
    vlo_ref, vhi_ref, vtaps_ref = (refs.pop(0), refs.pop(0), refs.pop(0)) if conv_in else (None,) * 3
    g_ref, kf_ref, h2_ref, h2i_ref, xm_ref = (refs.pop(0) for _ in range(5))
    xlo_ref, xhi_ref, xtaps_ref = (refs.pop(0), refs.pop(0), refs.pop(0)) if conv_mult else (None,) * 3
    hb_ref, o_ref, a_scr, keep_scr = refs
    ns1, ns2, ns3 = steps
    cn2a, ck1, cn2 = n2len // ns1, n1h // ns2, n2len // ns3
    s = pl.program_id(2)

    @pl.when(s < ns1)
    def _():
        base = s * cn2a
        for j in range(cn2a):
            if conv_in:
                x = _short_conv_slab(vf_ref, vlo_ref, vhi_ref, vtaps_ref, j, cn2a,
                                     s == 0, s == ns1 - 1)
            else:
                x = _load_minor(vf_ref, j)
            keep_scr[base + j] = x
            r = jnp.dot(g_ref[base + j], x.astype(BF16), preferred_element_type=F32)
            a_scr[pl.ds(base + j, 2 * n1h, stride=pitch), :] = r

    @pl.when((s >= ns1) & (s < ns1 + ns2))
    def _():
        a_re, a_im = _spectrum_views(a_scr, s - ns1, ck1, n1h, pitch)
        for j in range(0, ck1, 2):
            rows = _spectrum_pair(a_re, a_im, j, n2len, pitch)
            u = jnp.dot(h2_ref[...], rows, preferred_element_type=F32)
            ur, ui = u[:n2len], u[n2len:]
            kf = jnp.concatenate([kf_ref[j], kf_ref[j + 1]], axis=1).astype(F32)
            kr, ki = kf[:n2len], kf[n2len:]
            y = jnp.concatenate([ur * kr - ui * ki, ur * ki + ui * kr], axis=0)
            bb = jnp.dot(h2i_ref[...], y.astype(BF16), preferred_element_type=F32)
            for jj in range(2):
                lanes = slice(jj * LANES, (jj + 1) * LANES)
                a_re[(j + jj) * pitch:(j + jj) * pitch + n2len, :] = bb[:n2len, lanes]
                a_im[(j + jj) * pitch:(j + jj) * pitch + n2len, :] = bb[n2len:, lanes]

    @pl.when(s >= ns1 + ns2)
    def _():
        base = (s - ns1 - ns2) * cn2
        for j in range(cn2):
            rows = a_scr[pl.ds(base + j, 2 * n1h, stride=pitch), :]
            y = lax.dot_general(g_ref[base + j], rows.astype(BF16), (((0,), (0,)), ((), ())),
                                preferred_element_type=F32)
            if conv_mult:
                xm = _short_conv_slab(xm_ref, xlo_ref, xhi_ref, xtaps_ref, j, cn2,
                                      s == ns1 + ns2, s == ns1 + ns2 + ns3 - 1)
            else:
                xm = _load_minor(xm_ref, j)
            z = xm * (y + hb_ref[...] * keep_scr[base + j])
            o_ref[:, j * SUBLANES:(j + 1) * SUBLANES, :] = z.reshape(
                n1h // SUBLANES, SUBLANES, z.shape[1])


def _long_conv(u_perm, mult_perm, kf, hbias, order_idx, tables, in_taps=None, mult_taps=None):
    g_fwd, _, h2, h2i = tables
    bsz, groups, cbn, tile, _ = u_perm.shape
    n2len = tile // SUBLANES
    n1h = groups * SUBLANES
    steps = CONV_STEPS
    ns1, ns2, ns3 = steps
    cn2a, cn2 = n2len // ns1, n2len // ns3
    pitch = n2len + SUBLANES

    def early(s):
        return jnp.minimum(s, ns1 - 1)

    def late(s):
        return jnp.clip(s - ns1 - ns2, 0, ns3 - 1)

    def chunk_specs(chunk_of, width):
        slab = (None, groups, None, SUBLANES, LANES)
        return [
            pl.BlockSpec((None, groups, None, width * SUBLANES, LANES),
                         lambda c, b, s: (b, 0, c, chunk_of(s), 0)),
            pl.BlockSpec(slab, lambda c, b, s: (b, 0, c, (chunk_of(s) * width - 1) % n2len, 0)),
            pl.BlockSpec(slab, lambda c, b, s: (b, 0, c, ((chunk_of(s) + 1) * width) % n2len, 0)),
        ]

    taps_spec = pl.BlockSpec((4, LANES), lambda c, b, s: (0, c))
    in_specs, args = [], []
    specs = chunk_specs(early, cn2a)
    if in_taps is None:
        in_specs += specs[:1]
        args += [u_perm]
    else:
        in_specs += specs + [taps_spec]
        args += [u_perm, u_perm, u_perm, in_taps]
    in_specs += [
        _resident(g_fwd.shape),
        pl.BlockSpec((n1h // ns2, 2 * n2len, LANES),
                     lambda c, b, s: (jnp.clip(s - ns1, 0, ns2 - 1), 0, order_idx * cbn + c)),
        _resident(h2.shape), _resident(h2i.shape),
    ]
    args += [g_fwd, kf, h2, h2i]
    specs = chunk_specs(late, cn2)
    if mult_taps is None:
        in_specs += specs[:1]
        args += [mult_perm]
    else:
        in_specs += specs + [taps_spec]
        args += [mult_perm, mult_perm, mult_perm, mult_taps]
    in_specs.append(pl.BlockSpec((None, 1, LANES), lambda c, b, s: (order_idx, 0, c)))
    args.append(hbias)
    data_late = specs[0]
    kern = functools.partial(_long_conv_kernel, steps=steps, n1h=n1h, n2len=n2len,
                             pitch=pitch, conv_in=in_taps is not None,
                             conv_mult=mult_taps is not None)
    return pl.pallas_call(
        kern,
        grid=(cbn, bsz, sum(steps)),
        in_specs=in_specs,
        out_specs=data_late,
        out_shape=jax.ShapeDtypeStruct(u_perm.shape, F32),
        scratch_shapes=[pltpu.VMEM((2 * n1h * pitch, LANES), F32),
                        pltpu.VMEM((n2len, n1h, LANES), F32)],
        compiler_params=pltpu.CompilerParams(
            dimension_semantics=("parallel", "parallel", "arbitrary"),
            vmem_limit_bytes=CONV_VMEM_LIMIT),
        name="long_conv",
    )(*args)


def _out_kernel(x_ref, yh_ref, yp_ref, mod_ref, w_ref, b_ref, g_ref, beta_ref, o_ref,
                *, n2len, alpha):
    cbn = yh_ref.shape[0]
    for j0 in range(0, SUBLANES, OUT_SLABS):
        rows = slice(j0 * n2len, (j0 + OUT_SLABS) * n2len)
        yh = jnp.concatenate(
            [jnp.concatenate([yh_ref[cb, pl.ds(j, n2len, stride=SUBLANES), :]
                              for cb in range(cbn)], axis=1)
             for j in range(j0, j0 + OUT_SLABS)], axis=0)
        y = jnp.concatenate([yh.astype(BF16), yp_ref[rows, :]], axis=1)
        acc = jnp.dot(y, w_ref[...], preferred_element_type=F32)
        h = alpha * x_ref[rows, :] + mod_ref[2:3, :] * (acc + b_ref[...])
        mu = jnp.mean(h, axis=-1, keepdims=True)
        hc = h - mu
        var = jnp.mean(hc * hc, axis=-1, keepdims=True)
        o_ref[rows, :] = hc * lax.rsqrt(var + LN_EPS) * g_ref[...] + beta_ref[...]


def _output_projection(x, yh_perm, yp, mod3, w_out, b_out, ln_g, ln_b, alpha):
    bsz, seq, d = x.shape
    _, _, cbn, tile, _ = yh_perm.shape
    ch = cbn * LANES
    n2len = tile // SUBLANES
    dp = yp.shape[2]
    full = lambda shape: pl.BlockSpec(shape, lambda b, i: (0,) * len(shape))
    kern = functools.partial(_out_kernel, n2len=n2len, alpha=alpha)
    return pl.pallas_call(
        kern,
        grid=(bsz, seq // tile),
        in_specs=[
            pl.BlockSpec((None, tile, d), lambda b, i: (b, i, 0)),
            pl.BlockSpec((None, None, cbn, tile, LANES), lambda b, i: (b, i, 0, 0, 0)),
            pl.BlockSpec((None, tile, dp), lambda b, i: (b, i, 0)),
            pl.BlockSpec((None, 3, d), lambda b, i: (b, 0, 0)),
            full((ch + dp, d)), full((1, d)), full((1, d)), full((1, d)),
        ],
        out_specs=pl.BlockSpec((None, tile, d), lambda b, i: (b, i, 0)),
        out_shape=jax.ShapeDtypeStruct(x.shape, x.dtype),
        compiler_params=pltpu.CompilerParams(
            dimension_semantics=("parallel", "arbitrary"), vmem_limit_bytes=VMEM_LIMIT),
        name="out_proj_deepnorm",
    )(x, yh_perm, yp, mod3, w_out.astype(BF16), b_out.reshape(1, d), ln_g.reshape(1, d),
      ln_b.reshape(1, d))


def _forward(x, c, w_ada, b_ada, w_in, b_in, conv_w, conv_b, filt_w1, filt_b1, filt_w_inner,
             filt_b_inner, filt_w_out, filt_freq, hyena_bias, pool_w, pool_scale, w_out,
             b_out, ln_g, ln_b, *, n2len):
    bsz, seq, d = x.shape
    depth = w_ada.shape[0]
    order, n_ch = hyena_bias.shape[1], hyena_bias.shape[2]
    alpha = (2.0 * depth) ** 0.25
    tables = _dft_tables(seq, n2len)
    h = x
    for layer in range(depth):
        mod3 = _modulation(c, w_ada[layer], b_ada[layer]).reshape(bsz, 3, d)
        v, x1, x2g, yp = _input_projection(
            h, mod3, w_in[layer], b_in[layer], conv_w[layer], conv_b[layer], pool_w[layer],
            pool_scale[layer], n2len)
        taps_f, taps_r = _filter_taps(
            seq, n2len, filt_w1[layer], filt_b1[layer], filt_w_inner[layer],
            filt_b_inner[layer], filt_w_out[layer], filt_freq[layer], n_ch, order)
        kf = _filter_spectrum(taps_f, taps_r, tables[0], tables[1], tables[2], n2len)
        hbias = hyena_bias[layer].reshape(order, 1, n_ch)
        taps = [jnp.concatenate([conv_w[layer][:, k * n_ch:(k + 1) * n_ch],
                                 conv_b[layer][None, k * n_ch:(k + 1) * n_ch]], axis=0)
                for k in range(2)]
        z = _long_conv(v, x1, kf, hbias, 0, tables, in_taps=taps[0], mult_taps=taps[1])
        for o in range(1, order):
            z = _long_conv(z, x2g, kf, hbias, o, tables)
        h = _output_projection(h, z, yp, mod3, w_out[layer], b_out[layer], ln_g[layer],
                               ln_b[layer], alpha)
    return h


def kernel(x, c, w_ada, b_ada, w_in, b_in, conv_w, conv_b, filt_w1, filt_b1, filt_w_inner,
           filt_b_inner, filt_w_out, filt_freq, hyena_bias, pool_w, pool_scale, w_out, b_out,
           ln_g, ln_b):
    return _forward(x, c, w_ada, b_ada, w_in, b_in, conv_w, conv_b, filt_w1, filt_b1,
                    filt_w_inner, filt_b_inner, filt_w_out, filt_freq, hyena_bias, pool_w,
                    pool_scale, w_out, b_out, ln_g, ln_b, n2len=MINOR_LEN)
```

```python
import functools
import math

import jax
import jax.numpy as jnp
import numpy as np
from jax import lax
from jax.experimental import pallas as pl
from jax.experimental.pallas import tpu as pltpu

F32 = jnp.float32
BF16 = jnp.bfloat16
HIGHEST = lax.Precision.HIGHEST

POOL_WINDOWS = (2, 4, 8, 16)
LN_EPS = 1e-5
DECAY_TARGET = 1e-2
FAST_DECAY_PCT = 0.3
SLOW_DECAY_PCT = 1.5

LANES = 128
SUBLANES = 8
HALO = 16
BAND_ROWS = 128
BAND_SPAN = 256
MINOR_LEN = 128
FILTER_STEPS = (2, 4)
CONV_STEPS = (2, 2, 4)
VMEM_LIMIT = 56 * 1024 * 1024
CONV_VMEM_LIMIT = 60 * 1024 * 1024


def _silu(x):
    h = 0.5 * x
    return h + h * jnp.tanh(h)


@functools.lru_cache(maxsize=None)
def _small_dft_tables(seq, n2len):
    n1h = seq // n2len
    n_fft = 2 * seq
    k1 = np.arange(n1h, dtype=np.int64)
    odd = 2 * k1 + 1
    alpha = ((odd[:, None] * k1[None, :]) % (4 * n1h)) * (2.0 * math.pi / (4 * n1h))
    n2e = np.arange(n2len + 1, dtype=np.int64)
    beta = ((n2e[:, None] * odd[None, :]) % (2 * n_fft)) * (2.0 * math.pi / (2 * n_fft))
    n2i = np.arange(n2len, dtype=np.int64)
    phi = ((n2i[:, None] * n2i[None, :]) % n2len) * (2.0 * math.pi / n2len)
    cm, sm = np.cos(phi), np.sin(phi)
    h2 = np.block([[cm, sm], [-sm, cm]])
    h2 = h2.reshape(2 * n2len, 2, n2len).transpose(0, 2, 1).reshape(2 * n2len, 2 * n2len)
    h2i = np.block([[cm, -sm], [sm, cm]])
    h2i = h2i.reshape(2, n2len, 2 * n2len).transpose(1, 0, 2).reshape(2 * n2len, 2 * n2len)
    h2, h2i = h2.astype(BF16), h2i.astype(BF16)
    f32 = lambda a: a.astype(np.float32)
    return (f32(np.cos(alpha)), f32(np.sin(alpha)), f32(np.cos(beta)), f32(np.sin(beta)),
            h2, h2i)


def _dft_tables(seq, n2len):
    ca, sa, cb, sb, h2, h2i = _small_dft_tables(seq, n2len)
    ca, sa = ca[None], sa[None]
    cb, sb = jnp.asarray(cb)[:, :, None], jnp.asarray(sb)[:, :, None]
    cos_t = ca * cb[:-1] - sa * sb[:-1]
    sin_t = sa * cb[:-1] + ca * sb[:-1]
    n2len, n1h = cos_t.shape[0], cos_t.shape[1]
    g_fwd = jnp.stack([cos_t, -sin_t], axis=2).reshape(n2len, 2 * n1h, n1h).astype(BF16)
    cos_r = ca * cb[1:] - sa * sb[1:]
    sin_r = sa * cb[1:] + ca * sb[1:]
    g_rev = jnp.stack([cos_r, sin_r], axis=2).reshape(n2len, 2 * n1h, n1h).astype(BF16)
    return g_fwd, g_rev, jnp.asarray(h2), jnp.asarray(h2i)


@functools.lru_cache(maxsize=None)
def _packed_pos_features(seq, emb, tile):
    hl = LANES // 2
    half = tile // 2
    bands = (emb - 1) // 2
    row = np.arange((seq // tile) * half, dtype=np.int64)[:, None]
    lane = np.arange(LANES, dtype=np.int64)[None, :]
    pos = ((row // half) * tile + (lane // hl) * half + row % half).astype(np.float64)
    feat = np.broadcast_to(lane % hl, pos.shape)
    t = pos / (seq - 1)
    w = (2.0 * math.pi / seq) * pos
    f = np.linspace(1e-4, bands - 1, bands)[(feat - 1) % bands]
    z = np.where(feat == 0, t,
                 np.where(feat <= bands, np.cos(f * w),
                          np.where(feat <= 2 * bands, -np.sin(f * w), 0.0)))
    return z.astype(np.float32)


def _mod_kernel(c_ref, w_ref, b_ref, o_ref):
    s = _silu(c_ref[...])
    o_ref[...] = jnp.dot(s, w_ref[...], precision=HIGHEST,
                         preferred_element_type=F32) + b_ref[...]


def _modulation(c, w_ada, b_ada):
    bsz, d = c.shape
    n_out = w_ada.shape[1]
    return pl.pallas_call(
        _mod_kernel,
        grid=(n_out // d,),
        in_specs=[pl.BlockSpec((bsz, d), lambda j: (0, 0)),
                  pl.BlockSpec((d, d), lambda j: (0, j)),
                  pl.BlockSpec((1, d), lambda j: (0, j))],
        out_specs=pl.BlockSpec((bsz, d), lambda j: (0, j)),
        out_shape=jax.ShapeDtypeStruct((bsz, n_out), F32),
        name="adaln_mod",
    )(c, w_ada, b_ada.reshape(1, n_out))


def _store_permuted(o_ref, val, n2len, first=0):
    for cb in range(val.shape[1] // LANES):
        for j in range(val.shape[0] // n2len):
            o_ref[cb, pl.ds(first + j, n2len, stride=SUBLANES), :] = val[
                j * n2len:(j + 1) * n2len, cb * LANES:(cb + 1) * LANES]


def _load_minor(x_ref, j):
    blk = x_ref[:, j * SUBLANES:(j + 1) * SUBLANES, :]
    return blk.reshape(blk.shape[0] * SUBLANES, blk.shape[2])


def _band_window(b, tile):
    last = tile + 2 * HALO - BAND_SPAN
    start = min(b * BAND_ROWS, last)
    return start, 0 if start == b * BAND_ROWS else 1


@functools.lru_cache(maxsize=None)
def _band_matrices(tile):
    t = np.arange(BAND_ROWS)[:, None]
    k = np.arange(BAND_SPAN)[None, :]
    mats = []
    for win in POOL_WINDOWS:
        half = win // 2
        per_variant = []
        for b in (0, tile // BAND_ROWS - 1):
            start, _ = _band_window(b, tile)
            rel = k + start - HALO - (b * BAND_ROWS + t)
            per_variant.append(((rel >= -half) & (rel < half)).astype(BF16))
        mats.append(np.stack(per_variant))
    return np.stack(mats)


def _proj_kernel(xp_ref, xc_ref, xn_ref, mod_ref, w_ref, b_ref, cw_ref, cb_ref, pw_ref,
                 ps_ref, band_ref, v_ref, x1_ref, x2_ref, yp_ref, p_scr, q_scr,
                 *, tile, n2len, ch, seq):
    i = pl.program_id(1)
    nt = pl.num_programs(1)
    shift = mod_ref[0:1, :]
    scale1 = 1.0 + mod_ref[1:2, :]
    xe = jnp.concatenate([xp_ref[...], xc_ref[...], xn_ref[...]], axis=0)
    ue = (xe * scale1 + shift).astype(BF16)
    uc = ue[HALO:HALO + tile, :]

    def zero_outside_sequence(p):
        return jnp.concatenate([jnp.where(i > 0, p[:HALO], 0.0), p[HALO:HALO + tile],
                                jnp.where(i < nt - 1, p[HALO + tile:], 0.0)], axis=0)

    hg = jnp.dot(uc, w_ref[:, 3 * ch:4 * ch], preferred_element_type=F32) + b_ref[:, 3 * ch:4 * ch]
    gate = _silu(hg)

    for k, o_ref in enumerate((v_ref, x1_ref)):
        cols = slice(k * ch, (k + 1) * ch)
        p = jnp.dot(uc, w_ref[:, cols], preferred_element_type=F32) + b_ref[:, cols]
        _store_permuted(o_ref, p, n2len)

    cols = slice(2 * ch, 3 * ch)
    p = jnp.dot(ue, w_ref[:, cols], preferred_element_type=F32) + b_ref[:, cols]
    p_scr[...] = zero_outside_sequence(p)
    s = (cb_ref[:, cols]
         + cw_ref[0:1, cols] * p_scr[HALO - 1:HALO - 1 + tile, :]
         + cw_ref[1:2, cols] * p_scr[HALO:HALO + tile, :]
         + cw_ref[2:3, cols] * p_scr[HALO + 1:HALO + 1 + tile, :])
    _store_permuted(x2_ref, s * gate, n2len)

    dp = ps_ref.shape[1]
    c0 = 4 * ch
    pin = jnp.dot(ue, w_ref[:, c0:c0 + dp], preferred_element_type=F32) + b_ref[:, c0:c0 + dp]
    pin = zero_outside_sequence(pin)
    q_scr[...] = pin.astype(BF16)
    pin_c = pin[HALO:HALO + tile, :]
    pgate = jnp.dot(uc, w_ref[:, c0 + dp:c0 + 2 * dp], preferred_element_type=F32) + b_ref[:, c0 + dp:c0 + 2 * dp]
    pos = i * tile + lax.broadcasted_iota(jnp.int32, (tile, 1), 0)
    pg = dp // len(POOL_WINDOWS)
    groups = []
    for g, win in enumerate(POOL_WINDOWS):
        lanes = slice(g * pg, (g + 1) * pg)
        half = win // 2
        sums = []
        for b in range(tile // BAND_ROWS):
            start, variant = _band_window(b, tile)
            sums.append(jnp.dot(band_ref[g, variant], q_scr[start:start + BAND_SPAN, lanes],
                                preferred_element_type=F32))
        acc = jnp.concatenate(sums, axis=0)
        cnt = (jnp.minimum(pos + half, seq) - jnp.maximum(pos - half, 0)).astype(F32)
        diff = acc / cnt - pin_c[:, lanes]
        groups.append(jnp.dot(diff.astype(BF16), pw_ref[g], preferred_element_type=F32))
    yp = jnp.concatenate(groups, axis=1) * ps_ref[...] * _silu(pgate)
    yp_ref[...] = yp.astype(yp_ref.dtype)


def _input_projection(x, mod3, w_in, b_in, conv_w, conv_b, pool_w, pool_scale, n2len):
    bsz, seq, d = x.shape
    ch = conv_w.shape[1] // 3
    dp = pool_scale.shape[0]
    n1h = seq // n2len
    tile = SUBLANES * n2len
    nt = seq // tile
    hb = tile // HALO
    n_proj = w_in.shape[1]
    cbn = ch // LANES
    assert max(POOL_WINDOWS) // 2 <= HALO and dp == ch and tile % BAND_ROWS == 0
    band = jnp.asarray(_band_matrices(tile))
    perm = jax.ShapeDtypeStruct((bsz, n1h // SUBLANES, cbn, tile, LANES), F32)
    perm_spec = pl.BlockSpec((None, None, cbn, tile, LANES), lambda b, i: (b, i, 0, 0, 0))
    full = lambda shape: pl.BlockSpec(shape, lambda b, i: (0,) * len(shape))
    kern = functools.partial(_proj_kernel, tile=tile, n2len=n2len, ch=ch, seq=seq)
    return pl.pallas_call(
        kern,
        grid=(bsz, nt),
        in_specs=[
            pl.BlockSpec((None, HALO, d), lambda b, i: (b, jnp.maximum(i * hb - 1, 0), 0)),
            pl.BlockSpec((None, tile, d), lambda b, i: (b, i, 0)),
            pl.BlockSpec((None, HALO, d), lambda b, i: (b, jnp.minimum((i + 1) * hb, seq // HALO - 1), 0)),
            pl.BlockSpec((None, 3, d), lambda b, i: (b, 0, 0)),
            full((d, n_proj)), full((1, n_proj)), full((3, 3 * ch)), full((1, 3 * ch)),
            full(pool_w.shape), full((1, dp)), full(band.shape),
        ],
        out_specs=[perm_spec, perm_spec, perm_spec,
                   pl.BlockSpec((None, tile, dp), lambda b, i: (b, i, 0))],
        out_shape=[perm, perm, perm, jax.ShapeDtypeStruct((bsz, seq, dp), BF16)],
        scratch_shapes=[pltpu.VMEM((tile + 2 * HALO, ch), F32),
                        pltpu.VMEM((tile + 2 * HALO, dp), BF16)],
        compiler_params=pltpu.CompilerParams(
            dimension_semantics=("parallel", "arbitrary"), vmem_limit_bytes=VMEM_LIMIT),
        name="in_proj_conv_pool",
    )(x, x, x, mod3, w_in.astype(BF16), b_in.reshape(1, n_proj), conv_w,
      conv_b.reshape(1, 3 * ch), pool_w.astype(BF16), pool_scale.reshape(1, dp), band)


def _filter_mlp_kernel(z_ref, w1_ref, b1_ref, wi_ref, bi_ref, fr_ref, wo_ref, ad_ref,
                       of_ref, or_ref, *, n2len):
    z = z_ref[...]
    fr = fr_ref[...]
    h = jnp.sin(fr * (jnp.dot(z.astype(BF16), w1_ref[...],
                              preferred_element_type=F32) + b1_ref[...]))
    for l in range(wi_ref.shape[0]):
        h = jnp.sin(fr * (jnp.dot(h.astype(BF16), wi_ref[l],
                                  preferred_element_type=F32) + bi_ref[l:l + 1, :]))
    hb = h.astype(BF16)
    half_lanes = LANES // 2
    n1_half = z.shape[0] // n2len
    for half in range(2):
        t = z[:, half * half_lanes:half * half_lanes + 1]
        decay = jnp.exp(-t * ad_ref[...])
        for d, o_ref in enumerate((of_ref, or_ref)):
            k = jnp.dot(hb, wo_ref[d, half], preferred_element_type=F32)
            _store_permuted(o_ref, k * decay, n2len, first=half * n1_half)


def _block_diag2(w):
    zero = jnp.zeros_like(w)
    return jnp.concatenate([jnp.concatenate([w, zero], axis=-1),
                            jnp.concatenate([zero, w], axis=-1)], axis=-2)


def _filter_taps(seq, n2len, w1, b1, w_inner, b_inner, w_out, freq, n_ch, order):
    emb, hid = w1.shape
    n_inner = w_inner.shape[0]
    oc = order * n_ch
    hl = LANES // 2
    assert emb <= hl and hid <= hl
    tile = SUBLANES * n2len
    half = tile // 2
    groups = seq // tile
    zp = jnp.asarray(_packed_pos_features(seq, emb, tile))
    ph = hl - hid
    w1d = _block_diag2(jnp.pad(w1, ((0, hl - emb), (0, ph))))
    b1d = jnp.tile(jnp.pad(b1, (0, ph)), 2).reshape(1, LANES)
    wid = _block_diag2(jnp.pad(w_inner, ((0, 0), (0, ph), (0, ph))))
    bid = jnp.tile(jnp.pad(b_inner, ((0, 0), (0, ph))), (1, 2))
    frd = jnp.tile(jnp.pad(freq, (0, ph), constant_values=1.0), 2).reshape(1, LANES)
    wo = jnp.transpose(w_out.reshape(hid, order, 2, n_ch), (2, 0, 1, 3)).reshape(2, hid, oc)
    wo = jnp.pad(wo, ((0, 0), (0, ph), (0, 0)))
    zero = jnp.zeros_like(wo)
    wo4 = jnp.stack([jnp.concatenate([wo, zero], axis=1),
                     jnp.concatenate([zero, wo], axis=1)], axis=1)
    min_decay = math.log(DECAY_TARGET) / SLOW_DECAY_PCT
    max_decay = math.log(DECAY_TARGET) / FAST_DECAY_PCT
    absdelta = jnp.abs(jnp.linspace(min_decay, max_decay, n_ch, dtype=F32))
    absdelta = jnp.tile(absdelta, order).reshape(1, oc)

    full = lambda shape: pl.BlockSpec(shape, lambda i: (0,) * len(shape))
    kern = functools.partial(_filter_mlp_kernel, n2len=n2len)
    taps = jax.ShapeDtypeStruct((groups, oc // LANES, tile, LANES), F32)
    taps_spec = pl.BlockSpec((None, oc // LANES, tile, LANES), lambda i: (i, 0, 0, 0))
    return pl.pallas_call(
        kern,
        grid=(groups,),
        in_specs=[
            pl.BlockSpec((half, LANES), lambda i: (i, 0)),
            full((LANES, LANES)), full((1, LANES)), full((n_inner, LANES, LANES)),
            full((n_inner, LANES)), full((1, LANES)), full((2, 2, LANES, oc)), full((1, oc)),
        ],
        out_specs=[taps_spec, taps_spec],
        out_shape=[taps, taps],
        compiler_params=pltpu.CompilerParams(
            dimension_semantics=("arbitrary",), vmem_limit_bytes=VMEM_LIMIT),
        name="filter_mlp",
    )(zp, w1d.astype(BF16), b1d, wid.astype(BF16), bid, frd, wo4.astype(BF16), absdelta)


def _resident(shape):
    return pl.BlockSpec(shape, lambda *_: (0,) * len(shape), pipeline_mode=pl.Buffered(1))


def _pack_pairs(x):
    return pltpu.bitcast(x.astype(BF16), jnp.uint32)


def _unpack_pairs(w):
    return pltpu.bitcast(w, BF16)


def _spectrum_view(a_scr, chunk, ck1, pitch):
    start = pl.multiple_of(chunk * (ck1 * pitch), SUBLANES)
    return a_scr.at[pl.ds(start, ck1 * pitch)]


def _spectrum_pair(view, j, n2len, pitch):
    return jnp.concatenate([_unpack_pairs(view[jj * pitch:jj * pitch + n2len, :])
                            for jj in (j, j + 1)], axis=1)


def _filter_fft_kernel(hf_ref, hr_ref, g_ref, gr_ref, h2_ref, o_ref, a_scr,
                       *, ns, cn2, ck1, n1h, n2len, pitch):
    s = pl.program_id(1)

    @pl.when(s < ns)
    def _():
        for j in range(cn2):
            jr = cn2 - 1 - j
            m = s * cn2 + j
            g = jnp.concatenate([g_ref[m], gr_ref[n2len - 1 - m]], axis=1)
            x = jnp.concatenate([_load_minor(hf_ref, j), _load_minor(hr_ref, jr)], axis=0)
            r = jnp.dot(g, x.astype(BF16), preferred_element_type=F32)
            a_scr[pl.ds(m, n1h, stride=pitch), :] = _pack_pairs(r)

    @pl.when(s >= ns)
    def _():
        view = _spectrum_view(a_scr, s - ns, ck1, pitch)
        for j in range(0, ck1, 2):
            rows = _spectrum_pair(view, j, n2len, pitch)
            u = jnp.dot(h2_ref[...], rows, preferred_element_type=F32)
            u = (u * (1.0 / (n1h * n2len))).astype(o_ref.dtype)
            o_ref[j] = u[:, :LANES]
            o_ref[j + 1] = u[:, LANES:]


def _filter_spectrum(taps_f, taps_r, g_fwd, g_rev, h2, n2len):
    groups, ocb, _, _ = taps_f.shape
    oc = ocb * LANES
    n1h = groups * SUBLANES
    ns, ns2 = FILTER_STEPS
    cn2, ck1 = n2len // ns, n1h // ns2
    pitch = n2len + SUBLANES
    kern = functools.partial(_filter_fft_kernel, ns=ns, cn2=cn2, ck1=ck1, n1h=n1h,
                             n2len=n2len, pitch=pitch)
    return pl.pallas_call(
        kern,
        grid=(oc // LANES, ns + ns2),
        in_specs=[
            pl.BlockSpec((groups, None, cn2 * SUBLANES, LANES),
                         lambda c, s: (0, c, jnp.minimum(s, ns - 1), 0)),
            pl.BlockSpec((groups, None, cn2 * SUBLANES, LANES),
                         lambda c, s: (0, c, jnp.maximum(ns - 1 - s, 0), 0)),
            _resident(g_fwd.shape), _resident(g_rev.shape), _resident(h2.shape),
        ],
        out_specs=pl.BlockSpec((ck1, 2 * n2len, LANES), lambda c, s: (jnp.maximum(s - ns, 0), 0, c)),
        out_shape=jax.ShapeDtypeStruct((n1h, 2 * n2len, oc), BF16),
        scratch_shapes=[pltpu.VMEM((n1h * pitch, LANES), jnp.uint32)],
        compiler_params=pltpu.CompilerParams(
            dimension_semantics=("parallel", "arbitrary"), vmem_limit_bytes=VMEM_LIMIT),
        name="filter_spectrum",
    )(taps_f, taps_r, g_fwd, g_rev, h2)


def _short_conv_slab(x_ref, lo_ref, hi_ref, taps_ref, j, count, first_chunk, last_chunk):
    cur = _load_minor(x_ref, j)
    zero_row = jnp.zeros((1, cur.shape[1]), F32)
    if j > 0:
        prev = _load_minor(x_ref, j - 1)
    else:
        prev = _load_minor(lo_ref, 0)
        prev = jnp.where(first_chunk, jnp.concatenate([zero_row, prev[:-1]], axis=0), prev)
    if j < count - 1:
        nxt = _load_minor(x_ref, j + 1)
    else:
        nxt = _load_minor(hi_ref, 0)
        nxt = jnp.where(last_chunk, jnp.concatenate([nxt[1:], zero_row], axis=0), nxt)
    return (taps_ref[3:4, :] + taps_ref[0:1, :] * prev + taps_ref[1:2, :] * cur
            + taps_ref[2:3, :] * nxt)


def _long_conv_kernel(*refs, steps, n1h, n2len, pitch, conv_in, conv_mult):
    refs = list(refs)
    vf_ref = refs.pop(0)
    vlo_ref, vhi_ref, vtaps_ref = (refs.pop(0), refs.pop(0), refs.pop(0)) if conv_in else (None,) * 3
    g_ref, kf_ref, h2_ref, h2i_ref, xm_ref = (refs.pop(0) for _ in range(5))
    xlo_ref, xhi_ref, xtaps_ref = (refs.pop(0), refs.pop(0), refs.pop(0)) if conv_mult else (None,) * 3
    hb_ref, o_ref, a_scr, keep_scr = refs
    ns1, ns2, ns3 = steps
    cn2a, ck1, cn2 = n2len // ns1, n1h // ns2, n2len // ns3
    s = pl.program_id(2)

    @pl.when(s < ns1)
    def _():
        base = s * cn2a
        for j in range(cn2a):
            if conv_in:
                x = _short_conv_slab(vf_ref, vlo_ref, vhi_ref, vtaps_ref, j, cn2a,
                                     s == 0, s == ns1 - 1)
            else:
                x = _load_minor(vf_ref, j)
            keep_scr[base + j] = x
            r = jnp.dot(g_ref[base + j], x.astype(BF16), preferred_element_type=F32)
            a_scr[pl.ds(base + j, n1h, stride=pitch), :] = _pack_pairs(r)

    @pl.when((s >= ns1) & (s < ns1 + ns2))
    def _():
        view = _spectrum_view(a_scr, s - ns1, ck1, pitch)
        for j in range(0, ck1, 2):
            rows = _spectrum_pair(view, j, n2len, pitch)
            u = jnp.dot(h2_ref[...], rows, preferred_element_type=F32)
            ur, ui = u[:n2len], u[n2len:]
            kf = jnp.concatenate([kf_ref[j], kf_ref[j + 1]], axis=1).astype(F32)
            kr, ki = kf[:n2len], kf[n2len:]
            y = jnp.concatenate([ur * kr - ui * ki, ur * ki + ui * kr], axis=0)
            bb = _pack_pairs(jnp.dot(h2i_ref[...], y.astype(BF16), preferred_element_type=F32))
            for jj in range(2):
                view[(j + jj) * pitch:(j + jj) * pitch + n2len, :] = bb[
                    :, jj * LANES:(jj + 1) * LANES]

    @pl.when(s >= ns1 + ns2)
    def _():
        base = (s - ns1 - ns2) * cn2
        for j in range(cn2):
            rows = _unpack_pairs(a_scr[pl.ds(base + j, n1h, stride=pitch), :])
            y = lax.dot_general(g_ref[base + j], rows, (((0,), (0,)), ((), ())),
                                preferred_element_type=F32)
            if conv_mult:
                xm = _short_conv_slab(xm_ref, xlo_ref, xhi_ref, xtaps_ref, j, cn2,
                                      s == ns1 + ns2, s == ns1 + ns2 + ns3 - 1)
            else:
                xm = _load_minor(xm_ref, j)
            z = xm * (y + hb_ref[...] * keep_scr[base + j])
            o_ref[:, j * SUBLANES:(j + 1) * SUBLANES, :] = z.reshape(
                n1h // SUBLANES, SUBLANES, z.shape[1])


def _long_conv(u_perm, mult_perm, kf, hbias, order_idx, tables, in_taps=None, mult_taps=None):
    g_fwd, _, h2, h2i = tables
    bsz, groups, cbn, tile, _ = u_perm.shape
    n2len = tile // SUBLANES
    n1h = groups * SUBLANES
    steps = CONV_STEPS
    ns1, ns2, ns3 = steps
    cn2a, cn2 = n2len // ns1, n2len // ns3
    pitch = n2len + SUBLANES

    def early(s):
        return jnp.minimum(s, ns1 - 1)

    def late(s):
        return jnp.clip(s - ns1 - ns2, 0, ns3 - 1)

    def chunk_specs(chunk_of, width):
        slab = (None, groups, None, SUBLANES, LANES)
        return [
            pl.BlockSpec((None, groups, None, width * SUBLANES, LANES),
                         lambda c, b, s: (b, 0, c, chunk_of(s), 0)),
            pl.BlockSpec(slab, lambda c, b, s: (b, 0, c, (chunk_of(s) * width - 1) % n2len, 0)),
            pl.BlockSpec(slab, lambda c, b, s: (b, 0, c, ((chunk_of(s) + 1) * width) % n2len, 0)),
        ]

    taps_spec = pl.BlockSpec((4, LANES), lambda c, b, s: (0, c))
    in_specs, args = [], []
    specs = chunk_specs(early, cn2a)
    if in_taps is None:
        in_specs += specs[:1]
        args += [u_perm]
    else:
        in_specs += specs + [taps_spec]
        args += [u_perm, u_perm, u_perm, in_taps]
    in_specs += [
        _resident(g_fwd.shape),
        pl.BlockSpec((n1h // ns2, 2 * n2len, LANES),
                     lambda c, b, s: (jnp.clip(s - ns1, 0, ns2 - 1), 0, order_idx * cbn + c)),
        _resident(h2.shape), _resident(h2i.shape),
    ]
    args += [g_fwd, kf, h2, h2i]
    specs = chunk_specs(late, cn2)
    if mult_taps is None:
        in_specs += specs[:1]
        args += [mult_perm]
    else:
        in_specs += specs + [taps_spec]
        args += [mult_perm, mult_perm, mult_perm, mult_taps]
    in_specs.append(pl.BlockSpec((None, 1, LANES), lambda c, b, s: (order_idx, 0, c)))
    args.append(hbias)
    data_late = specs[0]
    kern = functools.partial(_long_conv_kernel, steps=steps, n1h=n1h, n2len=n2len,
                             pitch=pitch, conv_in=in_taps is not None,
                             conv_mult=mult_taps is not None)
    return pl.pallas_call(
        kern,
        grid=(cbn, bsz, sum(steps)),
        in_specs=in_specs,
        out_specs=data_late,
        out_shape=jax.ShapeDtypeStruct(u_perm.shape, F32),
        scratch_shapes=[pltpu.VMEM((n1h * pitch, LANES), jnp.uint32),
                        pltpu.VMEM((n2len, n1h, LANES), F32)],
        compiler_params=pltpu.CompilerParams(
            dimension_semantics=("parallel", "parallel", "arbitrary"),
            vmem_limit_bytes=CONV_VMEM_LIMIT),
        name="long_conv",
    )(*args)


def _out_kernel(x_ref, yh_ref, yp_ref, mod_ref, w_ref, b_ref, g_ref, beta_ref, o_ref,
                *, n2len, alpha):
    cbn = yh_ref.shape[0]
    yh = jnp.concatenate(
        [jnp.concatenate([yh_ref[cb, pl.ds(j, n2len, stride=SUBLANES), :]
                          for cb in range(cbn)], axis=1)
         for j in range(SUBLANES)], axis=0)
    y = jnp.concatenate([yh.astype(BF16), yp_ref[...]], axis=1)
    acc = jnp.dot(y, w_ref[...], preferred_element_type=F32)
    h = alpha * x_ref[...] + mod_ref[2:3, :] * (acc + b_ref[...])
    mu = jnp.mean(h, axis=-1, keepdims=True)
    hc = h - mu
    var = jnp.mean(hc * hc, axis=-1, keepdims=True)
    o_ref[...] = hc * lax.rsqrt(var + LN_EPS) * g_ref[...] + beta_ref[...]


def _output_projection(x, yh_perm, yp, mod3, w_out, b_out, ln_g, ln_b, alpha):
    bsz, seq, d = x.shape
    _, _, cbn, tile, _ = yh_perm.shape
    ch = cbn * LANES
    n2len = tile // SUBLANES
    dp = yp.shape[2]
    full = lambda shape: pl.BlockSpec(shape, lambda b, i: (0,) * len(shape))
    kern = functools.partial(_out_kernel, n2len=n2len, alpha=alpha)
    return pl.pallas_call(
        kern,
        grid=(bsz, seq // tile),
        in_specs=[
            pl.BlockSpec((None, tile, d), lambda b, i: (b, i, 0)),
            pl.BlockSpec((None, None, cbn, tile, LANES), lambda b, i: (b, i, 0, 0, 0)),
            pl.BlockSpec((None, tile, dp), lambda b, i: (b, i, 0)),
            pl.BlockSpec((None, 3, d), lambda b, i: (b, 0, 0)),
            full((ch + dp, d)), full((1, d)), full((1, d)), full((1, d)),
        ],
        out_specs=pl.BlockSpec((None, tile, d), lambda b, i: (b, i, 0)),
        out_shape=jax.ShapeDtypeStruct(x.shape, x.dtype),
        compiler_params=pltpu.CompilerParams(
            dimension_semantics=("parallel", "arbitrary"), vmem_limit_bytes=VMEM_LIMIT),
        name="out_proj_deepnorm",
    )(x, yh_perm, yp, mod3, w_out.astype(BF16), b_out.reshape(1, d), ln_g.reshape(1, d),
      ln_b.reshape(1, d))


def _forward(x, c, w_ada, b_ada, w_in, b_in, conv_w, conv_b, filt_w1, filt_b1, filt_w_inner,
             filt_b_inner, filt_w_out, filt_freq, hyena_bias, pool_w, pool_scale, w_out,
             b_out, ln_g, ln_b, *, n2len):
    bsz, seq, d = x.shape
    depth = w_ada.shape[0]
    order, n_ch = hyena_bias.shape[1], hyena_bias.shape[2]
    alpha = (2.0 * depth) ** 0.25
    tables = _dft_tables(seq, n2len)
    h = x
    for layer in range(depth):
        mod3 = _modulation(c, w_ada[layer], b_ada[layer]).reshape(bsz, 3, d)
        v, x1, x2g, yp = _input_projection(
            h, mod3, w_in[layer], b_in[layer], conv_w[layer], conv_b[layer], pool_w[layer],
            pool_scale[layer], n2len)
        taps_f, taps_r = _filter_taps(
            seq, n2len, filt_w1[layer], filt_b1[layer], filt_w_inner[layer],
            filt_b_inner[layer], filt_w_out[layer], filt_freq[layer], n_ch, order)
        kf = _filter_spectrum(taps_f, taps_r, tables[0], tables[1], tables[2], n2len)
        hbias = hyena_bias[layer].reshape(order, 1, n_ch)
        taps = [jnp.concatenate([conv_w[layer][:, k * n_ch:(k + 1) * n_ch],
                                 conv_b[layer][None, k * n_ch:(k + 1) * n_ch]], axis=0)
                for k in range(2)]
        z = _long_conv(v, x1, kf, hbias, 0, tables, in_taps=taps[0], mult_taps=taps[1])
        for o in range(1, order):
            z = _long_conv(z, x2g, kf, hbias, o, tables)
        h = _output_projection(h, z, yp, mod3, w_out[layer], b_out[layer], ln_g[layer],
                               ln_b[layer], alpha)
    return h


def kernel(x, c, w_ada, b_ada, w_in, b_in, conv_w, conv_b, filt_w1, filt_b1, filt_w_inner,
           filt_b_inner, filt_w_out, filt_freq, hyena_bias, pool_w, pool_scale, w_out, b_out,
           ln_g, ln_b):
    return _forward(x, c, w_ada, b_ada, w_in, b_in, conv_w, conv_b, filt_w1, filt_b1,
                    filt_w_inner, filt_b_inner, filt_w_out, filt_freq, hyena_bias, pool_w,
                    pool_scale, w_out, b_out, ln_g, ln_b, n2len=MINOR_LEN)
```

```python
import functools
import math

import jax
import jax.numpy as jnp
import numpy as np
from jax import lax
from jax.experimental import pallas as pl
from jax.experimental.pallas import tpu as pltpu

F32 = jnp.float32
BF16 = jnp.bfloat16
HIGHEST = lax.Precision.HIGHEST

POOL_WINDOWS = (2, 4, 8, 16)
LN_EPS = 1e-5
DECAY_TARGET = 1e-2
FAST_DECAY_PCT = 0.3
SLOW_DECAY_PCT = 1.5

LANES = 128
SUBLANES = 8
HALO = 16
BAND_ROWS = 128
BAND_SPAN = 256
MINOR_LEN = 128
FILTER_STEPS = (2, 4)
CONV_STEPS = (2, 2, 4)
VMEM_LIMIT = 56 * 1024 * 1024
CONV_VMEM_LIMIT = 60 * 1024 * 1024


def _silu(x):
    h = 0.5 * x
    return h + h * jnp.tanh(h)


@functools.lru_cache(maxsize=None)
def _small_dft_tables(seq, n2len):
    n1h = seq // n2len
    n_fft = 2 * seq
    k1 = np.arange(n1h, dtype=np.int64)
    odd = 2 * k1 + 1
    alpha = ((odd[:, None] * k1[None, :]) % (4 * n1h)) * (2.0 * math.pi / (4 * n1h))
    n2e = np.arange(n2len + 1, dtype=np.int64)
    beta = ((n2e[:, None] * odd[None, :]) % (2 * n_fft)) * (2.0 * math.pi / (2 * n_fft))
    n2i = np.arange(n2len, dtype=np.int64)
    phi = ((n2i[:, None] * n2i[None, :]) % n2len) * (2.0 * math.pi / n2len)
    cm, sm = np.cos(phi), np.sin(phi)
    h2 = np.block([[cm, sm], [-sm, cm]])
    h2 = h2.reshape(2 * n2len, 2, n2len).transpose(0, 2, 1).reshape(2 * n2len, 2 * n2len)
    h2i = np.block([[cm, -sm], [sm, cm]])
    h2i = h2i.reshape(2, n2len, 2 * n2len).transpose(1, 0, 2).reshape(2 * n2len, 2 * n2len)
    h2, h2i = h2.astype(BF16), h2i.astype(BF16)
    quarter = 0.5 * math.pi * np.tile(np.array([0.0, 1.0]), n1h)[:, None]
    alpha2 = np.repeat(alpha, 2, axis=0)
    beta2 = np.repeat(beta, 2, axis=1)
    f32 = lambda a: a.astype(np.float32)
    seeds = tuple(f32(t) for t in (
        np.cos(alpha2 + quarter), np.sin(alpha2 + quarter),
        np.cos(alpha2 - quarter), np.sin(alpha2 - quarter), np.cos(beta2), np.sin(beta2)))
    return seeds + (h2, h2i)


def _dft_tables(seq, n2len):
    caf, saf, car, sar, cb, sb, h2, h2i = _small_dft_tables(seq, n2len)
    cb, sb = jnp.asarray(cb)[:, :, None], jnp.asarray(sb)[:, :, None]
    g_fwd = (caf[None] * cb[:-1] - saf[None] * sb[:-1]).astype(BF16)
    g_rev = (car[None] * cb[1:] - sar[None] * sb[1:]).astype(BF16)
    return g_fwd, g_rev, jnp.asarray(h2), jnp.asarray(h2i)


@functools.lru_cache(maxsize=None)
def _packed_pos_features(seq, emb, tile):
    hl = LANES // 2
    half = tile // 2
    bands = (emb - 1) // 2
    row = np.arange((seq // tile) * half, dtype=np.int64)[:, None]
    lane = np.arange(LANES, dtype=np.int64)[None, :]
    pos = ((row // half) * tile + (lane // hl) * half + row % half).astype(np.float64)
    feat = np.broadcast_to(lane % hl, pos.shape)
    t = pos / (seq - 1)
    w = (2.0 * math.pi / seq) * pos
    f = np.linspace(1e-4, bands - 1, bands)[(feat - 1) % bands]
    z = np.where(feat == 0, t,
                 np.where(feat <= bands, np.cos(f * w),
                          np.where(feat <= 2 * bands, -np.sin(f * w), 0.0)))
    return z.astype(np.float32)


def _mod_kernel(c_ref, w_ref, b_ref, o_ref):
    s = _silu(c_ref[...])
    o_ref[...] = jnp.dot(s, w_ref[...], precision=HIGHEST,
                         preferred_element_type=F32) + b_ref[...]


def _modulation(c, w_ada, b_ada):
    bsz, d = c.shape
    n_out = w_ada.shape[1]
    return pl.pallas_call(
        _mod_kernel,
        grid=(n_out // d,),
        in_specs=[pl.BlockSpec((bsz, d), lambda j: (0, 0)),
                  pl.BlockSpec((d, d), lambda j: (0, j)),
                  pl.BlockSpec((1, d), lambda j: (0, j))],
        out_specs=pl.BlockSpec((bsz, d), lambda j: (0, j)),
        out_shape=jax.ShapeDtypeStruct((bsz, n_out), F32),
        name="adaln_mod",
    )(c, w_ada, b_ada.reshape(1, n_out))


def _store_permuted(o_ref, val, n2len, first=0):
    for cb in range(val.shape[1] // LANES):
        for j in range(val.shape[0] // n2len):
            o_ref[cb, pl.ds(first + j, n2len, stride=SUBLANES), :] = val[
                j * n2len:(j + 1) * n2len, cb * LANES:(cb + 1) * LANES]


def _load_minor(x_ref, j):
    blk = x_ref[:, j * SUBLANES:(j + 1) * SUBLANES, :]
    return blk.reshape(blk.shape[0] * SUBLANES, blk.shape[2])


def _band_window(b, tile):
    last = tile + 2 * HALO - BAND_SPAN
    start = min(b * BAND_ROWS, last)
    return start, 0 if start == b * BAND_ROWS else 1


@functools.lru_cache(maxsize=None)
def _band_matrices(tile):
    t = np.arange(BAND_ROWS)[:, None]
    k = np.arange(BAND_SPAN)[None, :]
    mats = []
    for win in POOL_WINDOWS:
        half = win // 2
        per_variant = []
        for b in (0, tile // BAND_ROWS - 1):
            start, _ = _band_window(b, tile)
            rel = k + start - HALO - (b * BAND_ROWS + t)
            per_variant.append(((rel >= -half) & (rel < half)).astype(BF16))
        mats.append(np.stack(per_variant))
    return np.stack(mats)


def _proj_kernel(xp_ref, xc_ref, xn_ref, mod_ref, w_ref, b_ref, cw_ref, cb_ref, pw_ref,
                 ps_ref, band_ref, v_ref, x1_ref, x2_ref, yp_ref, p_scr, q_scr,
                 *, tile, n2len, ch, seq):
    i = pl.program_id(1)
    nt = pl.num_programs(1)
    shift = mod_ref[0:1, :]
    scale1 = 1.0 + mod_ref[1:2, :]
    xe = jnp.concatenate([xp_ref[...], xc_ref[...], xn_ref[...]], axis=0)
    ue = (xe * scale1 + shift).astype(BF16)
    uc = ue[HALO:HALO + tile, :]

    def zero_outside_sequence(p):
        return jnp.concatenate([jnp.where(i > 0, p[:HALO], 0.0), p[HALO:HALO + tile],
                                jnp.where(i < nt - 1, p[HALO + tile:], 0.0)], axis=0)

    hg = jnp.dot(uc, w_ref[:, 3 * ch:4 * ch], preferred_element_type=F32) + b_ref[:, 3 * ch:4 * ch]
    gate = _silu(hg)

    for k, o_ref in enumerate((v_ref, x1_ref)):
        cols = slice(k * ch, (k + 1) * ch)
        p = jnp.dot(uc, w_ref[:, cols], preferred_element_type=F32) + b_ref[:, cols]
        _store_permuted(o_ref, p, n2len)

    cols = slice(2 * ch, 3 * ch)
    p = jnp.dot(ue, w_ref[:, cols], preferred_element_type=F32) + b_ref[:, cols]
    p_scr[...] = zero_outside_sequence(p)
    s = (cb_ref[:, cols]
         + cw_ref[0:1, cols] * p_scr[HALO - 1:HALO - 1 + tile, :]
         + cw_ref[1:2, cols] * p_scr[HALO:HALO + tile, :]
         + cw_ref[2:3, cols] * p_scr[HALO + 1:HALO + 1 + tile, :])
    _store_permuted(x2_ref, s * gate, n2len)

    dp = ps_ref.shape[1]
    c0 = 4 * ch
    pin = jnp.dot(ue, w_ref[:, c0:c0 + dp], preferred_element_type=F32) + b_ref[:, c0:c0 + dp]
    pin = zero_outside_sequence(pin)
    q_scr[...] = pin.astype(BF16)
    pin_c = pin[HALO:HALO + tile, :]
    pgate = jnp.dot(uc, w_ref[:, c0 + dp:c0 + 2 * dp], preferred_element_type=F32) + b_ref[:, c0 + dp:c0 + 2 * dp]
    pos = i * tile + lax.broadcasted_iota(jnp.int32, (tile, 1), 0)
    pg = dp // len(POOL_WINDOWS)
    groups = []
    for g, win in enumerate(POOL_WINDOWS):
        lanes = slice(g * pg, (g + 1) * pg)
        half = win // 2
        sums = []
        for b in range(tile // BAND_ROWS):
            start, variant = _band_window(b, tile)
            sums.append(jnp.dot(band_ref[g, variant], q_scr[start:start + BAND_SPAN, lanes],
                                preferred_element_type=F32))
        acc = jnp.concatenate(sums, axis=0)
        cnt = (jnp.minimum(pos + half, seq) - jnp.maximum(pos - half, 0)).astype(F32)
        diff = acc / cnt - pin_c[:, lanes]
        groups.append(jnp.dot(diff.astype(BF16), pw_ref[g], preferred_element_type=F32))
    yp = jnp.concatenate(groups, axis=1) * ps_ref[...] * _silu(pgate)
    yp_ref[...] = yp.astype(yp_ref.dtype)


def _input_projection(x, mod3, w_in, b_in, conv_w, conv_b, pool_w, pool_scale, n2len):
    bsz, seq, d = x.shape
    ch = conv_w.shape[1] // 3
    dp = pool_scale.shape[0]
    n1h = seq // n2len
    tile = SUBLANES * n2len
    nt = seq // tile
    hb = tile // HALO
    n_proj = w_in.shape[1]
    cbn = ch // LANES
    assert max(POOL_WINDOWS) // 2 <= HALO and dp == ch and tile % BAND_ROWS == 0
    band = jnp.asarray(_band_matrices(tile))
    perm = jax.ShapeDtypeStruct((bsz, n1h // SUBLANES, cbn, tile, LANES), F32)
    perm_spec = pl.BlockSpec((None, None, cbn, tile, LANES), lambda b, i: (b, i, 0, 0, 0))
    full = lambda shape: pl.BlockSpec(shape, lambda b, i: (0,) * len(shape))
    kern = functools.partial(_proj_kernel, tile=tile, n2len=n2len, ch=ch, seq=seq)
    return pl.pallas_call(
        kern,
        grid=(bsz, nt),
        in_specs=[
            pl.BlockSpec((None, HALO, d), lambda b, i: (b, jnp.maximum(i * hb - 1, 0), 0)),
            pl.BlockSpec((None, tile, d), lambda b, i: (b, i, 0)),
            pl.BlockSpec((None, HALO, d), lambda b, i: (b, jnp.minimum((i + 1) * hb, seq // HALO - 1), 0)),
            pl.BlockSpec((None, 3, d), lambda b, i: (b, 0, 0)),
            full((d, n_proj)), full((1, n_proj)), full((3, 3 * ch)), full((1, 3 * ch)),
            full(pool_w.shape), full((1, dp)), full(band.shape),
        ],
        out_specs=[perm_spec, perm_spec, perm_spec,
                   pl.BlockSpec((None, tile, dp), lambda b, i: (b, i, 0))],
        out_shape=[perm, perm, perm, jax.ShapeDtypeStruct((bsz, seq, dp), BF16)],
        scratch_shapes=[pltpu.VMEM((tile + 2 * HALO, ch), F32),
                        pltpu.VMEM((tile + 2 * HALO, dp), BF16)],
        compiler_params=pltpu.CompilerParams(
            dimension_semantics=("parallel", "arbitrary"), vmem_limit_bytes=VMEM_LIMIT),
        name="in_proj_conv_pool",
    )(x, x, x, mod3, w_in.astype(BF16), b_in.reshape(1, n_proj), conv_w,
      conv_b.reshape(1, 3 * ch), pool_w.astype(BF16), pool_scale.reshape(1, dp), band)


def _filter_mlp_kernel(z_ref, w1_ref, b1_ref, wi_ref, bi_ref, fr_ref, wo_ref, ad_ref,
                       of_ref, or_ref, *, n2len):
    z = z_ref[...]
    fr = fr_ref[...]
    h = jnp.sin(fr * (jnp.dot(z.astype(BF16), w1_ref[...],
                              preferred_element_type=F32) + b1_ref[...]))
    for l in range(wi_ref.shape[0]):
        h = jnp.sin(fr * (jnp.dot(h.astype(BF16), wi_ref[l],
                                  preferred_element_type=F32) + bi_ref[l:l + 1, :]))
    hb = h.astype(BF16)
    half_lanes = LANES // 2
    n1_half = z.shape[0] // n2len
    for half in range(2):
        t = z[:, half * half_lanes:half * half_lanes + 1]
        decay = jnp.exp(-t * ad_ref[...])
        for d, o_ref in enumerate((of_ref, or_ref)):
            k = jnp.dot(hb, wo_ref[d, half], preferred_element_type=F32)
            _store_permuted(o_ref, k * decay, n2len, first=half * n1_half)


def _block_diag2(w):
    zero = jnp.zeros_like(w)
    return jnp.concatenate([jnp.concatenate([w, zero], axis=-1),
                            jnp.concatenate([zero, w], axis=-1)], axis=-2)


def _filter_taps(seq, n2len, w1, b1, w_inner, b_inner, w_out, freq, n_ch, order):
    emb, hid = w1.shape
    n_inner = w_inner.shape[0]
    oc = order * n_ch
    hl = LANES // 2
    assert emb <= hl and hid <= hl
    tile = SUBLANES * n2len
    half = tile // 2
    groups = seq // tile
    zp = jnp.asarray(_packed_pos_features(seq, emb, tile))
    ph = hl - hid
    w1d = _block_diag2(jnp.pad(w1, ((0, hl - emb), (0, ph))))
    b1d = jnp.tile(jnp.pad(b1, (0, ph)), 2).reshape(1, LANES)
    wid = _block_diag2(jnp.pad(w_inner, ((0, 0), (0, ph), (0, ph))))
    bid = jnp.tile(jnp.pad(b_inner, ((0, 0), (0, ph))), (1, 2))
    frd = jnp.tile(jnp.pad(freq, (0, ph), constant_values=1.0), 2).reshape(1, LANES)
    wo = jnp.transpose(w_out.reshape(hid, order, 2, n_ch), (2, 0, 1, 3)).reshape(2, hid, oc)
    wo = jnp.pad(wo, ((0, 0), (0, ph), (0, 0)))
    zero = jnp.zeros_like(wo)
    wo4 = jnp.stack([jnp.concatenate([wo, zero], axis=1),
                     jnp.concatenate([zero, wo], axis=1)], axis=1)
    min_decay = math.log(DECAY_TARGET) / SLOW_DECAY_PCT
    max_decay = math.log(DECAY_TARGET) / FAST_DECAY_PCT
    absdelta = jnp.abs(jnp.linspace(min_decay, max_decay, n_ch, dtype=F32))
    absdelta = jnp.tile(absdelta, order).reshape(1, oc)

    full = lambda shape: pl.BlockSpec(shape, lambda i: (0,) * len(shape))
    kern = functools.partial(_filter_mlp_kernel, n2len=n2len)
    taps = jax.ShapeDtypeStruct((groups, oc // LANES, tile, LANES), F32)
    taps_spec = pl.BlockSpec((None, oc // LANES, tile, LANES), lambda i: (i, 0, 0, 0))
    return pl.pallas_call(
        kern,
        grid=(groups,),
        in_specs=[
            pl.BlockSpec((half, LANES), lambda i: (i, 0)),
            full((LANES, LANES)), full((1, LANES)), full((n_inner, LANES, LANES)),
            full((n_inner, LANES)), full((1, LANES)), full((2, 2, LANES, oc)), full((1, oc)),
        ],
        out_specs=[taps_spec, taps_spec],
        out_shape=[taps, taps],
        compiler_params=pltpu.CompilerParams(
            dimension_semantics=("arbitrary",), vmem_limit_bytes=VMEM_LIMIT),
        name="filter_mlp",
    )(zp, w1d.astype(BF16), b1d, wid.astype(BF16), bid, frd, wo4.astype(BF16), absdelta)


def _resident(shape):
    return pl.BlockSpec(shape, lambda *_: (0,) * len(shape), pipeline_mode=pl.Buffered(1))


def _pack_pairs(x):
    return pltpu.bitcast(x.astype(BF16), jnp.uint32)


def _unpack_pairs(w):
    return pltpu.bitcast(w, BF16)


def _spectrum_view(a_scr, chunk, ck1, pitch):
    start = pl.multiple_of(chunk * (ck1 * pitch), SUBLANES)
    return a_scr.at[pl.ds(start, ck1 * pitch)]


def _spectrum_pair(view, j, n2len, pitch):
    return jnp.concatenate([_unpack_pairs(view[jj * pitch:jj * pitch + n2len, :])
                            for jj in (j, j + 1)], axis=1)


def _filter_fft_kernel(hf_ref, hr_ref, g_ref, gr_ref, h2_ref, o_ref, a_scr,
                       *, ns, cn2, ck1, n1h, n2len, pitch):
    s = pl.program_id(1)

    @pl.when(s < ns)
    def _():
        for j in range(cn2):
            jr = cn2 - 1 - j
            m = s * cn2 + j
            g = jnp.concatenate([g_ref[m], gr_ref[n2len - 1 - m]], axis=1)
            x = jnp.concatenate([_load_minor(hf_ref, j), _load_minor(hr_ref, jr)], axis=0)
            r = jnp.dot(g, x.astype(BF16), preferred_element_type=F32)
            a_scr[pl.ds(m, n1h, stride=pitch), :] = _pack_pairs(r)

    @pl.when(s >= ns)
    def _():
        view = _spectrum_view(a_scr, s - ns, ck1, pitch)
        for j in range(0, ck1, 2):
            rows = _spectrum_pair(view, j, n2len, pitch)
            u = jnp.dot(h2_ref[...], rows, preferred_element_type=F32)
            u = (u * (1.0 / (n1h * n2len))).astype(o_ref.dtype)
            o_ref[j] = u[:, :LANES]
            o_ref[j + 1] = u[:, LANES:]


def _filter_spectrum(taps_f, taps_r, g_fwd, g_rev, h2, n2len):
    groups, ocb, _, _ = taps_f.shape
    oc = ocb * LANES
    n1h = groups * SUBLANES
    ns, ns2 = FILTER_STEPS
    cn2, ck1 = n2len // ns, n1h // ns2
    pitch = n2len + SUBLANES
    kern = functools.partial(_filter_fft_kernel, ns=ns, cn2=cn2, ck1=ck1, n1h=n1h,
                             n2len=n2len, pitch=pitch)
    return pl.pallas_call(
        kern,
        grid=(oc // LANES, ns + ns2),
        in_specs=[
            pl.BlockSpec((groups, None, cn2 * SUBLANES, LANES),
                         lambda c, s: (0, c, jnp.minimum(s, ns - 1), 0)),
            pl.BlockSpec((groups, None, cn2 * SUBLANES, LANES),
                         lambda c, s: (0, c, jnp.maximum(ns - 1 - s, 0), 0)),
            _resident(g_fwd.shape), _resident(g_rev.shape), _resident(h2.shape),
        ],
        out_specs=pl.BlockSpec((ck1, 2 * n2len, LANES), lambda c, s: (jnp.maximum(s - ns, 0), 0, c)),
        out_shape=jax.ShapeDtypeStruct((n1h, 2 * n2len, oc), BF16),
        scratch_shapes=[pltpu.VMEM((n1h * pitch, LANES), jnp.uint32)],
        compiler_params=pltpu.CompilerParams(
            dimension_semantics=("parallel", "arbitrary"), vmem_limit_bytes=VMEM_LIMIT),
        name="filter_spectrum",
    )(taps_f, taps_r, g_fwd, g_rev, h2)


def _short_conv_slab(x_ref, lo_ref, hi_ref, taps_ref, j, count, first_chunk, last_chunk):
    cur = _load_minor(x_ref, j)
    zero_row = jnp.zeros((1, cur.shape[1]), F32)
    if j > 0:
        prev = _load_minor(x_ref, j - 1)
    else:
        prev = _load_minor(lo_ref, 0)
        prev = jnp.where(first_chunk, jnp.concatenate([zero_row, prev[:-1]], axis=0), prev)
    if j < count - 1:
        nxt = _load_minor(x_ref, j + 1)
    else:
        nxt = _load_minor(hi_ref, 0)
        nxt = jnp.where(last_chunk, jnp.concatenate([nxt[1:], zero_row], axis=0), nxt)
    return (taps_ref[3:4, :] + taps_ref[0:1, :] * prev + taps_ref[1:2, :] * cur
            + taps_ref[2:3, :] * nxt)


def _long_conv_kernel(*refs, steps, n1h, n2len, pitch, conv_in, conv_mult):
    refs = list(refs)
    vf_ref = refs.pop(0)
    vlo_ref, vhi_ref, vtaps_ref = (refs.pop(0), refs.pop(0), refs.pop(0)) if conv_in else (None,) * 3
    g_ref, kf_ref, h2_ref, h2i_ref, xm_ref = (refs.pop(0) for _ in range(5))
    xlo_ref, xhi_ref, xtaps_ref = (refs.pop(0), refs.pop(0), refs.pop(0)) if conv_mult else (None,) * 3
    hb_ref, o_ref, a_scr, keep_scr = refs
    ns1, ns2, ns3 = steps
    cn2a, ck1, cn2 = n2len // ns1, n1h // ns2, n2len // ns3
    s = pl.program_id(2)

    @pl.when(s < ns1)
    def _():
        base = s * cn2a
        for j in range(cn2a):
            if conv_in:
                x = _short_conv_slab(vf_ref, vlo_ref, vhi_ref, vtaps_ref, j, cn2a,
                                     s == 0, s == ns1 - 1)
            else:
                x = _load_minor(vf_ref, j)
            keep_scr[base + j] = x
            r = jnp.dot(g_ref[base + j], x.astype(BF16), preferred_element_type=F32)
            a_scr[pl.ds(base + j, n1h, stride=pitch), :] = _pack_pairs(r)

    @pl.when((s >= ns1) & (s < ns1 + ns2))
    def _():
        view = _spectrum_view(a_scr, s - ns1, ck1, pitch)
        for j in range(0, ck1, 2):
            rows = _spectrum_pair(view, j, n2len, pitch)
            u = jnp.dot(h2_ref[...], rows, preferred_element_type=F32)
            ur, ui = u[:n2len], u[n2len:]
            kf = jnp.concatenate([kf_ref[j], kf_ref[j + 1]], axis=1).astype(F32)
            kr, ki = kf[:n2len], kf[n2len:]
            y = jnp.concatenate([ur * kr - ui * ki, ur * ki + ui * kr], axis=0)
            bb = _pack_pairs(jnp.dot(h2i_ref[...], y.astype(BF16), preferred_element_type=F32))
            for jj in range(2):
                view[(j + jj) * pitch:(j + jj) * pitch + n2len, :] = bb[
                    :, jj * LANES:(jj + 1) * LANES]

    @pl.when(s >= ns1 + ns2)
    def _():
        base = (s - ns1 - ns2) * cn2
        for j in range(cn2):
            rows = _unpack_pairs(a_scr[pl.ds(base + j, n1h, stride=pitch), :])
            y = lax.dot_general(g_ref[base + j], rows, (((0,), (0,)), ((), ())),
                                preferred_element_type=F32)
            if conv_mult:
                xm = _short_conv_slab(xm_ref, xlo_ref, xhi_ref, xtaps_ref, j, cn2,
                                      s == ns1 + ns2, s == ns1 + ns2 + ns3 - 1)
            else:
                xm = _load_minor(xm_ref, j)
            z = xm * (y + hb_ref[...] * keep_scr[base + j])
            o_ref[:, j * SUBLANES:(j + 1) * SUBLANES, :] = z.reshape(
                n1h // SUBLANES, SUBLANES, z.shape[1])


def _long_conv(u_perm, mult_perm, kf, hbias, order_idx, tables, in_taps=None, mult_taps=None):
    g_fwd, _, h2, h2i = tables
    bsz, groups, cbn, tile, _ = u_perm.shape
    n2len = tile // SUBLANES
    n1h = groups * SUBLANES
    steps = CONV_STEPS
    ns1, ns2, ns3 = steps
    cn2a, cn2 = n2len // ns1, n2len // ns3
    pitch = n2len + SUBLANES

    def early(s):
        return jnp.minimum(s, ns1 - 1)

    def late(s):
        return jnp.clip(s - ns1 - ns2, 0, ns3 - 1)

    def chunk_specs(chunk_of, width):
        slab = (None, groups, None, SUBLANES, LANES)
        return [
            pl.BlockSpec((None, groups, None, width * SUBLANES, LANES),
                         lambda c, b, s: (b, 0, c, chunk_of(s), 0)),
            pl.BlockSpec(slab, lambda c, b, s: (b, 0, c, (chunk_of(s) * width - 1) % n2len, 0)),
            pl.BlockSpec(slab, lambda c, b, s: (b, 0, c, ((chunk_of(s) + 1) * width) % n2len, 0)),
        ]

    taps_spec = pl.BlockSpec((4, LANES), lambda c, b, s: (0, c))
    in_specs, args = [], []
    specs = chunk_specs(early, cn2a)
    if in_taps is None:
        in_specs += specs[:1]
        args += [u_perm]
    else:
        in_specs += specs + [taps_spec]
        args += [u_perm, u_perm, u_perm, in_taps]
    in_specs += [
        _resident(g_fwd.shape),
        pl.BlockSpec((n1h // ns2, 2 * n2len, LANES),
                     lambda c, b, s: (jnp.clip(s - ns1, 0, ns2 - 1), 0, order_idx * cbn + c)),
        _resident(h2.shape), _resident(h2i.shape),
    ]
    args += [g_fwd, kf, h2, h2i]
    specs = chunk_specs(late, cn2)
    if mult_taps is None:
        in_specs += specs[:1]
        args += [mult_perm]
    else:
        in_specs += specs + [taps_spec]
        args += [mult_perm, mult_perm, mult_perm, mult_taps]
    in_specs.append(pl.BlockSpec((None, 1, LANES), lambda c, b, s: (order_idx, 0, c)))
    args.append(hbias)
    data_late = specs[0]
    kern = functools.partial(_long_conv_kernel, steps=steps, n1h=n1h, n2len=n2len,
                             pitch=pitch, conv_in=in_taps is not None,
                             conv_mult=mult_taps is not None)
    return pl.pallas_call(
        kern,
        grid=(cbn, bsz, sum(steps)),
        in_specs=in_specs,
        out_specs=data_late,
        out_shape=jax.ShapeDtypeStruct(u_perm.shape, F32),
        scratch_shapes=[pltpu.VMEM((n1h * pitch, LANES), jnp.uint32),
                        pltpu.VMEM((n2len, n1h, LANES), F32)],
        compiler_params=pltpu.CompilerParams(
            dimension_semantics=("parallel", "parallel", "arbitrary"),
            vmem_limit_bytes=CONV_VMEM_LIMIT),
        name="long_conv",
    )(*args)


def _out_kernel(x_ref, yh_ref, yp_ref, mod_ref, w_ref, b_ref, g_ref, beta_ref, o_ref,
                *, n2len, alpha):
    cbn = yh_ref.shape[0]
    yh = jnp.concatenate(
        [jnp.concatenate([yh_ref[cb, pl.ds(j, n2len, stride=SUBLANES), :]
                          for cb in range(cbn)], axis=1)
         for j in range(SUBLANES)], axis=0)
    y = jnp.concatenate([yh.astype(BF16), yp_ref[...]], axis=1)
    acc = jnp.dot(y, w_ref[...], preferred_element_type=F32)
    h = alpha * x_ref[...] + mod_ref[2:3, :] * (acc + b_ref[...])
    mu = jnp.mean(h, axis=-1, keepdims=True)
    hc = h - mu
    var = jnp.mean(hc * hc, axis=-1, keepdims=True)
    o_ref[...] = hc * lax.rsqrt(var + LN_EPS) * g_ref[...] + beta_ref[...]


def _output_projection(x, yh_perm, yp, mod3, w_out, b_out, ln_g, ln_b, alpha):
    bsz, seq, d = x.shape
    _, _, cbn, tile, _ = yh_perm.shape
    ch = cbn * LANES
    n2len = tile // SUBLANES
    dp = yp.shape[2]
    full = lambda shape: pl.BlockSpec(shape, lambda b, i: (0,) * len(shape))
    kern = functools.partial(_out_kernel, n2len=n2len, alpha=alpha)
    return pl.pallas_call(
        kern,
        grid=(bsz, seq // tile),
        in_specs=[
            pl.BlockSpec((None, tile, d), lambda b, i: (b, i, 0)),
            pl.BlockSpec((None, None, cbn, tile, LANES), lambda b, i: (b, i, 0, 0, 0)),
            pl.BlockSpec((None, tile, dp), lambda b, i: (b, i, 0)),
            pl.BlockSpec((None, 3, d), lambda b, i: (b, 0, 0)),
            full((ch + dp, d)), full((1, d)), full((1, d)), full((1, d)),
        ],
        out_specs=pl.BlockSpec((None, tile, d), lambda b, i: (b, i, 0)),
        out_shape=jax.ShapeDtypeStruct(x.shape, x.dtype),
        compiler_params=pltpu.CompilerParams(
            dimension_semantics=("parallel", "arbitrary"), vmem_limit_bytes=VMEM_LIMIT),
        name="out_proj_deepnorm",
    )(x, yh_perm, yp, mod3, w_out.astype(BF16), b_out.reshape(1, d), ln_g.reshape(1, d),
      ln_b.reshape(1, d))


def _forward(x, c, w_ada, b_ada, w_in, b_in, conv_w, conv_b, filt_w1, filt_b1, filt_w_inner,
             filt_b_inner, filt_w_out, filt_freq, hyena_bias, pool_w, pool_scale, w_out,
             b_out, ln_g, ln_b, *, n2len):
    bsz, seq, d = x.shape
    depth = w_ada.shape[0]
    order, n_ch = hyena_bias.shape[1], hyena_bias.shape[2]
    alpha = (2.0 * depth) ** 0.25
    tables = _dft_tables(seq, n2len)
    h = x
    for layer in range(depth):
        mod3 = _modulation(c, w_ada[layer], b_ada[layer]).reshape(bsz, 3, d)
        v, x1, x2g, yp = _input_projection(
            h, mod3, w_in[layer], b_in[layer], conv_w[layer], conv_b[layer], pool_w[layer],
            pool_scale[layer], n2len)
        taps_f, taps_r = _filter_taps(
            seq, n2len, filt_w1[layer], filt_b1[layer], filt_w_inner[layer],
            filt_b_inner[layer], filt_w_out[layer], filt_freq[layer], n_ch, order)
        kf = _filter_spectrum(taps_f, taps_r, tables[0], tables[1], tables[2], n2len)
        hbias = hyena_bias[layer].reshape(order, 1, n_ch)
        taps = [jnp.concatenate([conv_w[layer][:, k * n_ch:(k + 1) * n_ch],
                                 conv_b[layer][None, k * n_ch:(k + 1) * n_ch]], axis=0)
                for k in range(2)]
        z = _long_conv(v, x1, kf, hbias, 0, tables, in_taps=taps[0], mult_taps=taps[1])
        for o in range(1, order):
            z = _long_conv(z, x2g, kf, hbias, o, tables)
        h = _output_projection(h, z, yp, mod3, w_out[layer], b_out[layer], ln_g[layer],
                               ln_b[layer], alpha)
    return h


def kernel(x, c, w_ada, b_ada, w_in, b_in, conv_w, conv_b, filt_w1, filt_b1, filt_w_inner,
           filt_b_inner, filt_w_out, filt_freq, hyena_bias, pool_w, pool_scale, w_out, b_out,
           ln_g, ln_b):
    return _forward(x, c, w_ada, b_ada, w_in, b_in, conv_w, conv_b, filt_w1, filt_b1,
                    filt_w_inner, filt_b_inner, filt_w_out, filt_freq, hyena_bias, pool_w,
                    pool_scale, w_out, b_out, ln_g, ln_b, n2len=MINOR_LEN)
```

```python
import functools
import math

import jax
import jax.numpy as jnp
import numpy as np
from jax import lax
from jax.experimental import pallas as pl
from jax.experimental.pallas import tpu as pltpu

F32 = jnp.float32
BF16 = jnp.bfloat16
HIGHEST = lax.Precision.HIGHEST

POOL_WINDOWS = (2, 4, 8, 16)
LN_EPS = 1e-5
DECAY_TARGET = 1e-2
FAST_DECAY_PCT = 0.3
SLOW_DECAY_PCT = 1.5

LANES = 128
SUBLANES = 8
HALO = 16
BAND_ROWS = 128
BAND_SPAN = 256
MINOR_LEN = 128
FILTER_STEPS = (2, 2)
CONV_STEPS = (2, 2, 2)
VMEM_LIMIT = 56 * 1024 * 1024
CONV_VMEM_LIMIT = 60 * 1024 * 1024


def _silu(x):
    h = 0.5 * x
    return h + h * jnp.tanh(h)


@functools.lru_cache(maxsize=None)
def _host_dft_tables(seq, n2len):
    n1h = seq // n2len
    n_fft = 2 * seq
    k1 = np.arange(n1h, dtype=np.int64)
    odd = 2 * k1 + 1
    alpha = ((odd[:, None] * k1[None, :]) % (4 * n1h)) * (2.0 * math.pi / (4 * n1h))
    n2e = np.arange(n2len + 1, dtype=np.int64)
    beta = ((n2e[:, None] * odd[None, :]) % (2 * n_fft)) * (2.0 * math.pi / (2 * n_fft))
    n2i = np.arange(n2len, dtype=np.int64)
    phi = ((n2i[:, None] * n2i[None, :]) % n2len) * (2.0 * math.pi / n2len)
    cm, sm = np.cos(phi), np.sin(phi)
    h2 = np.block([[cm, sm], [-sm, cm]])
    h2 = h2.reshape(2 * n2len, 2, n2len).transpose(0, 2, 1).reshape(2 * n2len, 2 * n2len)
    h2i = np.block([[cm, -sm], [sm, cm]])
    h2i = h2i.reshape(2, n2len, 2 * n2len).transpose(1, 0, 2).reshape(2 * n2len, 2 * n2len)
    h2, h2i = h2.astype(BF16), h2i.astype(BF16)
    quarter = 0.5 * math.pi * np.tile(np.array([0.0, 1.0]), n1h)[:, None]
    alpha2 = np.repeat(alpha, 2, axis=0)
    beta2 = np.repeat(beta, 2, axis=1)
    f32 = lambda a: a.astype(np.float32)
    seeds = tuple(f32(t) for t in (
        np.cos(alpha2 + quarter), np.sin(alpha2 + quarter),
        np.cos(alpha2 - quarter), np.sin(alpha2 - quarter), np.cos(beta2), np.sin(beta2)))
    return seeds + (h2, h2i)


def _dft_tables(seq, n2len):
    caf, saf, car, sar, cb, sb, h2, h2i = _host_dft_tables(seq, n2len)
    cb, sb = jnp.asarray(cb)[:, :, None], jnp.asarray(sb)[:, :, None]
    g_fwd = (caf[None] * cb[:-1] - saf[None] * sb[:-1]).astype(BF16)
    g_rev = (car[None] * cb[1:] - sar[None] * sb[1:]).astype(BF16)
    return g_fwd, g_rev, jnp.asarray(h2), jnp.asarray(h2i)


@functools.lru_cache(maxsize=None)
def _packed_pos_features(seq, emb, tile):
    hl = LANES // 2
    half = tile // 2
    bands = (emb - 1) // 2
    row = np.arange((seq // tile) * half, dtype=np.int64)[:, None]
    lane = np.arange(LANES, dtype=np.int64)[None, :]
    pos = ((row // half) * tile + (lane // hl) * half + row % half).astype(np.float64)
    feat = np.broadcast_to(lane % hl, pos.shape)
    t = pos / (seq - 1)
    w = (2.0 * math.pi / seq) * pos
    f = np.linspace(1e-4, bands - 1, bands)[(feat - 1) % bands]
    z = np.where(feat == 0, t,
                 np.where(feat <= bands, np.cos(f * w),
                          np.where(feat <= 2 * bands, -np.sin(f * w), 0.0)))
    return z.astype(np.float32)


def _mod_kernel(c_ref, w_ref, b_ref, o_ref):
    s = _silu(c_ref[...])
    o_ref[...] = jnp.dot(s, w_ref[...], precision=HIGHEST,
                         preferred_element_type=F32) + b_ref[...]


def _modulation(c, w_ada, b_ada):
    bsz, d = c.shape
    n_out = w_ada.shape[1]
    return pl.pallas_call(
        _mod_kernel,
        grid=(n_out // d,),
        in_specs=[pl.BlockSpec((bsz, d), lambda j: (0, 0)),
                  pl.BlockSpec((d, d), lambda j: (0, j)),
                  pl.BlockSpec((1, d), lambda j: (0, j))],
        out_specs=pl.BlockSpec((bsz, d), lambda j: (0, j)),
        out_shape=jax.ShapeDtypeStruct((bsz, n_out), F32),
        name="adaln_mod",
    )(c, w_ada, b_ada.reshape(1, n_out))


def _store_permuted(o_ref, val, n2len, first=0):
    for cb in range(val.shape[1] // LANES):
        for j in range(val.shape[0] // n2len):
            o_ref[cb, pl.ds(first + j, n2len, stride=SUBLANES), :] = val[
                j * n2len:(j + 1) * n2len, cb * LANES:(cb + 1) * LANES]


def _load_minor(x_ref, j):
    blk = x_ref[:, j * SUBLANES:(j + 1) * SUBLANES, :]
    return blk.reshape(blk.shape[0] * SUBLANES, blk.shape[2])


def _band_window(b, tile):
    last = tile + 2 * HALO - BAND_SPAN
    start = min(b * BAND_ROWS, last)
    return start, 0 if start == b * BAND_ROWS else 1


@functools.lru_cache(maxsize=None)
def _band_matrices(tile):
    t = np.arange(BAND_ROWS)[:, None]
    k = np.arange(BAND_SPAN)[None, :]
    mats = []
    for win in POOL_WINDOWS:
        half = win // 2
        per_variant = []
        for b in (0, tile // BAND_ROWS - 1):
            start, _ = _band_window(b, tile)
            rel = k + start - HALO - (b * BAND_ROWS + t)
            per_variant.append(((rel >= -half) & (rel < half)).astype(BF16))
        mats.append(np.stack(per_variant))
    return np.stack(mats)


def _proj_kernel(xp_ref, xc_ref, xn_ref, mod_ref, w_ref, b_ref, cw_ref, cb_ref, pw_ref,
                 ps_ref, band_ref, v_ref, x1_ref, x2_ref, yp_ref, p_scr, q_scr,
                 *, tile, n2len, ch, seq):
    i = pl.program_id(1)
    nt = pl.num_programs(1)
    shift = mod_ref[0:1, :]
    scale1 = 1.0 + mod_ref[1:2, :]
    xe = jnp.concatenate([xp_ref[...], xc_ref[...], xn_ref[...]], axis=0)
    ue = (xe * scale1 + shift).astype(BF16)
    uc = ue[HALO:HALO + tile, :]

    def zero_outside_sequence(p):
        return jnp.concatenate([jnp.where(i > 0, p[:HALO], 0.0), p[HALO:HALO + tile],
                                jnp.where(i < nt - 1, p[HALO + tile:], 0.0)], axis=0)

    hg = jnp.dot(uc, w_ref[:, 3 * ch:4 * ch], preferred_element_type=F32) + b_ref[:, 3 * ch:4 * ch]
    gate = _silu(hg)

    for k, o_ref in enumerate((v_ref, x1_ref)):
        cols = slice(k * ch, (k + 1) * ch)
        p = jnp.dot(uc, w_ref[:, cols], preferred_element_type=F32) + b_ref[:, cols]
        _store_permuted(o_ref, p, n2len)

    cols = slice(2 * ch, 3 * ch)
    p = jnp.dot(ue, w_ref[:, cols], preferred_element_type=F32) + b_ref[:, cols]
    p_scr[...] = zero_outside_sequence(p)
    s = (cb_ref[:, cols]
         + cw_ref[0:1, cols] * p_scr[HALO - 1:HALO - 1 + tile, :]
         + cw_ref[1:2, cols] * p_scr[HALO:HALO + tile, :]
         + cw_ref[2:3, cols] * p_scr[HALO + 1:HALO + 1 + tile, :])
    _store_permuted(x2_ref, s * gate, n2len)

    dp = ps_ref.shape[1]
    c0 = 4 * ch
    pin = jnp.dot(ue, w_ref[:, c0:c0 + dp], preferred_element_type=F32) + b_ref[:, c0:c0 + dp]
    pin = zero_outside_sequence(pin)
    q_scr[...] = pin.astype(BF16)
    pin_c = pin[HALO:HALO + tile, :]
    pgate = jnp.dot(uc, w_ref[:, c0 + dp:c0 + 2 * dp], preferred_element_type=F32) + b_ref[:, c0 + dp:c0 + 2 * dp]
    pos = i * tile + lax.broadcasted_iota(jnp.int32, (tile, 1), 0)
    pg = dp // len(POOL_WINDOWS)
    groups = []
    for g, win in enumerate(POOL_WINDOWS):
        lanes = slice(g * pg, (g + 1) * pg)
        half = win // 2
        sums = []
        for b in range(tile // BAND_ROWS):
            start, variant = _band_window(b, tile)
            sums.append(jnp.dot(band_ref[g, variant], q_scr[start:start + BAND_SPAN, lanes],
                                preferred_element_type=F32))
        acc = jnp.concatenate(sums, axis=0)
        cnt = (jnp.minimum(pos + half, seq) - jnp.maximum(pos - half, 0)).astype(F32)
        diff = acc / cnt - pin_c[:, lanes]
        groups.append(jnp.dot(diff.astype(BF16), pw_ref[g], preferred_element_type=F32))
    yp = jnp.concatenate(groups, axis=1) * ps_ref[...] * _silu(pgate)
    yp_ref[...] = yp.astype(yp_ref.dtype)


def _input_projection(x, mod3, w_in, b_in, conv_w, conv_b, pool_w, pool_scale, n2len):
    bsz, seq, d = x.shape
    ch = conv_w.shape[1] // 3
    dp = pool_scale.shape[0]
    n1h = seq // n2len
    tile = SUBLANES * n2len
    nt = seq // tile
    hb = tile // HALO
    n_proj = w_in.shape[1]
    cbn = ch // LANES
    assert max(POOL_WINDOWS) // 2 <= HALO and dp == ch and tile % BAND_ROWS == 0
    band = jnp.asarray(_band_matrices(tile))
    perm = jax.ShapeDtypeStruct((bsz, n1h // SUBLANES, cbn, tile, LANES), F32)
    perm_spec = pl.BlockSpec((None, None, cbn, tile, LANES), lambda b, i: (b, i, 0, 0, 0))
    full = lambda shape: pl.BlockSpec(shape, lambda b, i: (0,) * len(shape))
    kern = functools.partial(_proj_kernel, tile=tile, n2len=n2len, ch=ch, seq=seq)
    return pl.pallas_call(
        kern,
        grid=(bsz, nt),
        in_specs=[
            pl.BlockSpec((None, HALO, d), lambda b, i: (b, jnp.maximum(i * hb - 1, 0), 0)),
            pl.BlockSpec((None, tile, d), lambda b, i: (b, i, 0)),
            pl.BlockSpec((None, HALO, d), lambda b, i: (b, jnp.minimum((i + 1) * hb, seq // HALO - 1), 0)),
            pl.BlockSpec((None, 3, d), lambda b, i: (b, 0, 0)),
            full((d, n_proj)), full((1, n_proj)), full((3, 3 * ch)), full((1, 3 * ch)),
            full(pool_w.shape), full((1, dp)), full(band.shape),
        ],
        out_specs=[perm_spec, perm_spec, perm_spec,
                   pl.BlockSpec((None, tile, dp), lambda b, i: (b, i, 0))],
        out_shape=[perm, perm, perm, jax.ShapeDtypeStruct((bsz, seq, dp), BF16)],
        scratch_shapes=[pltpu.VMEM((tile + 2 * HALO, ch), F32),
                        pltpu.VMEM((tile + 2 * HALO, dp), BF16)],
        compiler_params=pltpu.CompilerParams(
            dimension_semantics=("parallel", "arbitrary"), vmem_limit_bytes=VMEM_LIMIT),
        name="in_proj_conv_pool",
    )(x, x, x, mod3, w_in.astype(BF16), b_in.reshape(1, n_proj), conv_w,
      conv_b.reshape(1, 3 * ch), pool_w.astype(BF16), pool_scale.reshape(1, dp), band)


def _filter_mlp_kernel(z_ref, w1_ref, b1_ref, wi_ref, bi_ref, fr_ref, wo_ref, ad_ref,
                       of_ref, or_ref, *, n2len):
    z = z_ref[...]
    fr = fr_ref[...]
    h = jnp.sin(fr * (jnp.dot(z.astype(BF16), w1_ref[...],
                              preferred_element_type=F32) + b1_ref[...]))
    for l in range(wi_ref.shape[0]):
        h = jnp.sin(fr * (jnp.dot(h.astype(BF16), wi_ref[l],
                                  preferred_element_type=F32) + bi_ref[l:l + 1, :]))
    hb = h.astype(BF16)
    half_lanes = LANES // 2
    n1_half = z.shape[0] // n2len
    for half in range(2):
        t = z[:, half * half_lanes:half * half_lanes + 1]
        decay = jnp.exp(-t * ad_ref[...])
        for d, o_ref in enumerate((of_ref, or_ref)):
            k = jnp.dot(hb, wo_ref[d, half], preferred_element_type=F32)
            _store_permuted(o_ref, k * decay, n2len, first=half * n1_half)


def _block_diag2(w):
    zero = jnp.zeros_like(w)
    return jnp.concatenate([jnp.concatenate([w, zero], axis=-1),
                            jnp.concatenate([zero, w], axis=-1)], axis=-2)


def _filter_taps(seq, n2len, w1, b1, w_inner, b_inner, w_out, freq, n_ch, order):
    emb, hid = w1.shape
    n_inner = w_inner.shape[0]
    oc = order * n_ch
    hl = LANES // 2
    assert emb <= hl and hid <= hl
    tile = SUBLANES * n2len
    half = tile // 2
    groups = seq // tile
    zp = jnp.asarray(_packed_pos_features(seq, emb, tile))
    ph = hl - hid
    w1d = _block_diag2(jnp.pad(w1, ((0, hl - emb), (0, ph))))
    b1d = jnp.tile(jnp.pad(b1, (0, ph)), 2).reshape(1, LANES)
    wid = _block_diag2(jnp.pad(w_inner, ((0, 0), (0, ph), (0, ph))))
    bid = jnp.tile(jnp.pad(b_inner, ((0, 0), (0, ph))), (1, 2))
    frd = jnp.tile(jnp.pad(freq, (0, ph), constant_values=1.0), 2).reshape(1, LANES)
    wo = jnp.transpose(w_out.reshape(hid, order, 2, n_ch), (2, 0, 1, 3)).reshape(2, hid, oc)
    wo = jnp.pad(wo, ((0, 0), (0, ph), (0, 0)))
    zero = jnp.zeros_like(wo)
    wo4 = jnp.stack([jnp.concatenate([wo, zero], axis=1),
                     jnp.concatenate([zero, wo], axis=1)], axis=1)
    min_decay = math.log(DECAY_TARGET) / SLOW_DECAY_PCT
    max_decay = math.log(DECAY_TARGET) / FAST_DECAY_PCT
    absdelta = jnp.abs(jnp.linspace(min_decay, max_decay, n_ch, dtype=F32))
    absdelta = jnp.tile(absdelta, order).reshape(1, oc)

    full = lambda shape: pl.BlockSpec(shape, lambda i: (0,) * len(shape))
    kern = functools.partial(_filter_mlp_kernel, n2len=n2len)
    taps = jax.ShapeDtypeStruct((groups, oc // LANES, tile, LANES), F32)
    taps_spec = pl.BlockSpec((None, oc // LANES, tile, LANES), lambda i: (i, 0, 0, 0))
    return pl.pallas_call(
        kern,
        grid=(groups,),
        in_specs=[
            pl.BlockSpec((half, LANES), lambda i: (i, 0)),
            full((LANES, LANES)), full((1, LANES)), full((n_inner, LANES, LANES)),
            full((n_inner, LANES)), full((1, LANES)), full((2, 2, LANES, oc)), full((1, oc)),
        ],
        out_specs=[taps_spec, taps_spec],
        out_shape=[taps, taps],
        compiler_params=pltpu.CompilerParams(
            dimension_semantics=("arbitrary",), vmem_limit_bytes=VMEM_LIMIT),
        name="filter_mlp",
    )(zp, w1d.astype(BF16), b1d, wid.astype(BF16), bid, frd, wo4.astype(BF16), absdelta)


def _resident(shape):
    return pl.BlockSpec(shape, lambda *_: (0,) * len(shape), pipeline_mode=pl.Buffered(1))


def _pack_pairs(x):
    return pltpu.bitcast(x.astype(BF16), jnp.uint32)


def _unpack_pairs(w):
    return pltpu.bitcast(w, BF16)


def _spectrum_view(a_scr, chunk, ck1, pitch):
    start = pl.multiple_of(chunk * (ck1 * pitch), SUBLANES)
    return a_scr.at[pl.ds(start, ck1 * pitch)]


def _spectrum_pair(view, j, n2len, pitch):
    return jnp.concatenate([_unpack_pairs(view[jj * pitch:jj * pitch + n2len, :])
                            for jj in (j, j + 1)], axis=1)


def _filter_fft_kernel(hf_ref, hr_ref, g_ref, gr_ref, h2_ref, o_ref, a_scr,
                       *, ns, cn2, ck1, n1h, n2len, pitch):
    s = pl.program_id(1)

    @pl.when(s < ns)
    def _():
        for j in range(cn2):
            jr = cn2 - 1 - j
            m = s * cn2 + j
            g = jnp.concatenate([g_ref[m], gr_ref[n2len - 1 - m]], axis=1)
            x = jnp.concatenate([_load_minor(hf_ref, j), _load_minor(hr_ref, jr)], axis=0)
            r = jnp.dot(g, x.astype(BF16), preferred_element_type=F32)
            a_scr[pl.ds(m, n1h, stride=pitch), :] = _pack_pairs(r)

    @pl.when(s >= ns)
    def _():
        view = _spectrum_view(a_scr, s - ns, ck1, pitch)
        for j in range(0, ck1, 2):
            rows = _spectrum_pair(view, j, n2len, pitch)
            u = jnp.dot(h2_ref[...], rows, preferred_element_type=F32)
            u = (u * (1.0 / (n1h * n2len))).astype(o_ref.dtype)
            o_ref[j] = u[:, :LANES]
            o_ref[j + 1] = u[:, LANES:]


def _filter_spectrum(taps_f, taps_r, g_fwd, g_rev, h2, n2len):
    groups, ocb, _, _ = taps_f.shape
    oc = ocb * LANES
    n1h = groups * SUBLANES
    ns, ns2 = FILTER_STEPS
    cn2, ck1 = n2len // ns, n1h // ns2
    pitch = n2len + SUBLANES
    kern = functools.partial(_filter_fft_kernel, ns=ns, cn2=cn2, ck1=ck1, n1h=n1h,
                             n2len=n2len, pitch=pitch)
    return pl.pallas_call(
        kern,
        grid=(oc // LANES, ns + ns2),
        in_specs=[
            pl.BlockSpec((groups, None, cn2 * SUBLANES, LANES),
                         lambda c, s: (0, c, jnp.minimum(s, ns - 1), 0)),
            pl.BlockSpec((groups, None, cn2 * SUBLANES, LANES),
                         lambda c, s: (0, c, jnp.maximum(ns - 1 - s, 0), 0)),
            _resident(g_fwd.shape), _resident(g_rev.shape), _resident(h2.shape),
        ],
        out_specs=pl.BlockSpec((ck1, 2 * n2len, LANES), lambda c, s: (jnp.maximum(s - ns, 0), 0, c)),
        out_shape=jax.ShapeDtypeStruct((n1h, 2 * n2len, oc), BF16),
        scratch_shapes=[pltpu.VMEM((n1h * pitch, LANES), jnp.uint32)],
        compiler_params=pltpu.CompilerParams(
            dimension_semantics=("parallel", "arbitrary"), vmem_limit_bytes=VMEM_LIMIT),
        name="filter_spectrum",
    )(taps_f, taps_r, g_fwd, g_rev, h2)


def _short_conv_slab(x_ref, lo_ref, hi_ref, taps_ref, j, count, first_chunk, last_chunk):
    cur = _load_minor(x_ref, j)
    zero_row = jnp.zeros((1, cur.shape[1]), F32)
    if j > 0:
        prev = _load_minor(x_ref, j - 1)
    else:
        prev = _load_minor(lo_ref, 0)
        prev = jnp.where(first_chunk, jnp.concatenate([zero_row, prev[:-1]], axis=0), prev)
    if j < count - 1:
        nxt = _load_minor(x_ref, j + 1)
    else:
        nxt = _load_minor(hi_ref, 0)
        nxt = jnp.where(last_chunk, jnp.concatenate([nxt[1:], zero_row], axis=0), nxt)
    return (taps_ref[3:4, :] + taps_ref[0:1, :] * prev + taps_ref[1:2, :] * cur
            + taps_ref[2:3, :] * nxt)


def _long_conv_kernel(*refs, steps, n1h, n2len, pitch, conv_in, conv_mult):
    refs = list(refs)
    vf_ref = refs.pop(0)
    vlo_ref, vhi_ref, vtaps_ref = (refs.pop(0), refs.pop(0), refs.pop(0)) if conv_in else (None,) * 3
    g_ref, kf_ref, h2_ref, h2i_ref, xm_ref = (refs.pop(0) for _ in range(5))
    xlo_ref, xhi_ref, xtaps_ref = (refs.pop(0), refs.pop(0), refs.pop(0)) if conv_mult else (None,) * 3
    hb_ref, o_ref, a_scr, keep_scr = refs
    ns1, ns2, ns3 = steps
    cn2a, ck1, cn2 = n2len // ns1, n1h // ns2, n2len // ns3
    s = pl.program_id(2)

    @pl.when(s < ns1)
    def _():
        base = s * cn2a
        for j in range(cn2a):
            if conv_in:
                x = _short_conv_slab(vf_ref, vlo_ref, vhi_ref, vtaps_ref, j, cn2a,
                                     s == 0, s == ns1 - 1)
            else:
                x = _load_minor(vf_ref, j)
            keep_scr[base + j] = x
            r = jnp.dot(g_ref[base + j], x.astype(BF16), preferred_element_type=F32)
            a_scr[pl.ds(base + j, n1h, stride=pitch), :] = _pack_pairs(r)

    @pl.when((s >= ns1) & (s < ns1 + ns2))
    def _():
        view = _spectrum_view(a_scr, s - ns1, ck1, pitch)
        for j in range(0, ck1, 2):
            rows = _spectrum_pair(view, j, n2len, pitch)
            u = jnp.dot(h2_ref[...], rows, preferred_element_type=F32)
            ur, ui = u[:n2len], u[n2len:]
            kf = jnp.concatenate([kf_ref[j], kf_ref[j + 1]], axis=1).astype(F32)
            kr, ki = kf[:n2len], kf[n2len:]
            y = jnp.concatenate([ur * kr - ui * ki, ur * ki + ui * kr], axis=0)
            bb = _pack_pairs(jnp.dot(h2i_ref[...], y.astype(BF16), preferred_element_type=F32))
            for jj in range(2):
                view[(j + jj) * pitch:(j + jj) * pitch + n2len, :] = bb[
                    :, jj * LANES:(jj + 1) * LANES]

    @pl.when(s >= ns1 + ns2)
    def _():
        base = (s - ns1 - ns2) * cn2
        for j in range(cn2):
            rows = _unpack_pairs(a_scr[pl.ds(base + j, n1h, stride=pitch), :])
            y = lax.dot_general(g_ref[base + j], rows, (((0,), (0,)), ((), ())),
                                preferred_element_type=F32)
            if conv_mult:
                xm = _short_conv_slab(xm_ref, xlo_ref, xhi_ref, xtaps_ref, j, cn2,
                                      s == ns1 + ns2, s == ns1 + ns2 + ns3 - 1)
            else:
                xm = _load_minor(xm_ref, j)
            z = xm * (y + hb_ref[...] * keep_scr[base + j])
            o_ref[:, j * SUBLANES:(j + 1) * SUBLANES, :] = z.reshape(
                n1h // SUBLANES, SUBLANES, z.shape[1])


def _long_conv(u_perm, mult_perm, kf, hbias, order_idx, tables, in_taps=None, mult_taps=None):
    g_fwd, _, h2, h2i = tables
    bsz, groups, cbn, tile, _ = u_perm.shape
    n2len = tile // SUBLANES
    n1h = groups * SUBLANES
    steps = CONV_STEPS
    ns1, ns2, ns3 = steps
    cn2a, cn2 = n2len // ns1, n2len // ns3
    pitch = n2len + SUBLANES

    def early(s):
        return jnp.minimum(s, ns1 - 1)

    def late(s):
        return jnp.clip(s - ns1 - ns2, 0, ns3 - 1)

    def chunk_specs(chunk_of, width):
        slab = (None, groups, None, SUBLANES, LANES)
        return [
            pl.BlockSpec((None, groups, None, width * SUBLANES, LANES),
                         lambda c, b, s: (b, 0, c, chunk_of(s), 0)),
            pl.BlockSpec(slab, lambda c, b, s: (b, 0, c, (chunk_of(s) * width - 1) % n2len, 0)),
            pl.BlockSpec(slab, lambda c, b, s: (b, 0, c, ((chunk_of(s) + 1) * width) % n2len, 0)),
        ]

    taps_spec = pl.BlockSpec((4, LANES), lambda c, b, s: (0, c))
    in_specs, args = [], []
    specs = chunk_specs(early, cn2a)
    if in_taps is None:
        in_specs += specs[:1]
        args += [u_perm]
    else:
        in_specs += specs + [taps_spec]
        args += [u_perm, u_perm, u_perm, in_taps]
    in_specs += [
        _resident(g_fwd.shape),
        pl.BlockSpec((n1h // ns2, 2 * n2len, LANES),
                     lambda c, b, s: (jnp.clip(s - ns1, 0, ns2 - 1), 0, order_idx * cbn + c)),
        _resident(h2.shape), _resident(h2i.shape),
    ]
    args += [g_fwd, kf, h2, h2i]
    specs = chunk_specs(late, cn2)
    if mult_taps is None:
        in_specs += specs[:1]
        args += [mult_perm]
    else:
        in_specs += specs + [taps_spec]
        args += [mult_perm, mult_perm, mult_perm, mult_taps]
    in_specs.append(pl.BlockSpec((None, 1, LANES), lambda c, b, s: (order_idx, 0, c)))
    args.append(hbias)
    data_late = specs[0]
    kern = functools.partial(_long_conv_kernel, steps=steps, n1h=n1h, n2len=n2len,
                             pitch=pitch, conv_in=in_taps is not None,
                             conv_mult=mult_taps is not None)
    return pl.pallas_call(
        kern,
        grid=(cbn, bsz, sum(steps)),
        in_specs=in_specs,
        out_specs=data_late,
        out_shape=jax.ShapeDtypeStruct(u_perm.shape, F32),
        scratch_shapes=[pltpu.VMEM((n1h * pitch, LANES), jnp.uint32),
                        pltpu.VMEM((n2len, n1h, LANES), F32)],
        compiler_params=pltpu.CompilerParams(
            dimension_semantics=("parallel", "parallel", "arbitrary"),
            vmem_limit_bytes=CONV_VMEM_LIMIT),
        name="long_conv",
    )(*args)


def _out_kernel(x_ref, yh_ref, yp_ref, mod_ref, w_ref, b_ref, g_ref, beta_ref, o_ref,
                *, n2len, alpha):
    cbn = yh_ref.shape[0]
    yh = jnp.concatenate(
        [jnp.concatenate([yh_ref[cb, pl.ds(j, n2len, stride=SUBLANES), :]
                          for cb in range(cbn)], axis=1)
         for j in range(SUBLANES)], axis=0)
    y = jnp.concatenate([yh.astype(BF16), yp_ref[...]], axis=1)
    acc = jnp.dot(y, w_ref[...], preferred_element_type=F32)
    h = alpha * x_ref[...] + mod_ref[2:3, :] * (acc + b_ref[...])
    mu = jnp.mean(h, axis=-1, keepdims=True)
    hc = h - mu
    var = jnp.mean(hc * hc, axis=-1, keepdims=True)
    o_ref[...] = hc * lax.rsqrt(var + LN_EPS) * g_ref[...] + beta_ref[...]


def _output_projection(x, yh_perm, yp, mod3, w_out, b_out, ln_g, ln_b, alpha):
    bsz, seq, d = x.shape
    _, _, cbn, tile, _ = yh_perm.shape
    ch = cbn * LANES
    n2len = tile // SUBLANES
    dp = yp.shape[2]
    full = lambda shape: pl.BlockSpec(shape, lambda b, i: (0,) * len(shape))
    kern = functools.partial(_out_kernel, n2len=n2len, alpha=alpha)
    return pl.pallas_call(
        kern,
        grid=(bsz, seq // tile),
        in_specs=[
            pl.BlockSpec((None, tile, d), lambda b, i: (b, i, 0)),
            pl.BlockSpec((None, None, cbn, tile, LANES), lambda b, i: (b, i, 0, 0, 0)),
            pl.BlockSpec((None, tile, dp), lambda b, i: (b, i, 0)),
            pl.BlockSpec((None, 3, d), lambda b, i: (b, 0, 0)),
            full((ch + dp, d)), full((1, d)), full((1, d)), full((1, d)),
        ],
        out_specs=pl.BlockSpec((None, tile, d), lambda b, i: (b, i, 0)),
        out_shape=jax.ShapeDtypeStruct(x.shape, x.dtype),
        compiler_params=pltpu.CompilerParams(
            dimension_semantics=("parallel", "arbitrary"), vmem_limit_bytes=VMEM_LIMIT),
        name="out_proj_deepnorm",
    )(x, yh_perm, yp, mod3, w_out.astype(BF16), b_out.reshape(1, d), ln_g.reshape(1, d),
      ln_b.reshape(1, d))


def _forward(x, c, w_ada, b_ada, w_in, b_in, conv_w, conv_b, filt_w1, filt_b1, filt_w_inner,
             filt_b_inner, filt_w_out, filt_freq, hyena_bias, pool_w, pool_scale, w_out,
             b_out, ln_g, ln_b, *, n2len):
    bsz, seq, d = x.shape
    depth = w_ada.shape[0]
    order, n_ch = hyena_bias.shape[1], hyena_bias.shape[2]
    alpha = (2.0 * depth) ** 0.25
    tables = _dft_tables(seq, n2len)
    h = x
    for layer in range(depth):
        mod3 = _modulation(c, w_ada[layer], b_ada[layer]).reshape(bsz, 3, d)
        v, x1, x2g, yp = _input_projection(
            h, mod3, w_in[layer], b_in[layer], conv_w[layer], conv_b[layer], pool_w[layer],
            pool_scale[layer], n2len)
        taps_f, taps_r = _filter_taps(
            seq, n2len, filt_w1[layer], filt_b1[layer], filt_w_inner[layer],
            filt_b_inner[layer], filt_w_out[layer], filt_freq[layer], n_ch, order)
        kf = _filter_spectrum(taps_f, taps_r, tables[0], tables[1], tables[2], n2len)
        hbias = hyena_bias[layer].reshape(order, 1, n_ch)
        taps = [jnp.concatenate([conv_w[layer][:, k * n_ch:(k + 1) * n_ch],
                                 conv_b[layer][None, k * n_ch:(k + 1) * n_ch]], axis=0)
                for k in range(2)]
        z = _long_conv(v, x1, kf, hbias, 0, tables, in_taps=taps[0], mult_taps=taps[1])
        for o in range(1, order):
            z = _long_conv(z, x2g, kf, hbias, o, tables)
        h = _output_projection(h, z, yp, mod3, w_out[layer], b_out[layer], ln_g[layer],
                               ln_b[layer], alpha)
    return h


def kernel(x, c, w_ada, b_ada, w_in, b_in, conv_w, conv_b, filt_w1, filt_b1, filt_w_inner,
           filt_b_inner, filt_w_out, filt_freq, hyena_bias, pool_w, pool_scale, w_out, b_out,
           ln_g, ln_b):
    return _forward(x, c, w_ada, b_ada, w_in, b_in, conv_w, conv_b, filt_w1, filt_b1,
                    filt_w_inner, filt_b_inner, filt_w_out, filt_freq, hyena_bias, pool_w,
                    pool_scale, w_out, b_out, ln_g, ln_b, n2len=MINOR_LEN)
```

```python
import functools
import math

import jax
import jax.numpy as jnp
import numpy as np
from jax import lax
from jax.experimental import pallas as pl
from jax.experimental.pallas import tpu as pltpu

F32 = jnp.float32
BF16 = jnp.bfloat16
HIGHEST = lax.Precision.HIGHEST

POOL_WINDOWS = (2, 4, 8, 16)
LN_EPS = 1e-5
DECAY_TARGET = 1e-2
FAST_DECAY_PCT = 0.3
SLOW_DECAY_PCT = 1.5

LANES = 128
SUBLANES = 8
HALO = 16
BAND_ROWS = 128
BAND_SPAN = 256
MINOR_LEN = 128
FILTER_STEPS = (2, 2)
CONV_STEPS = (2, 2, 2)
VMEM_LIMIT = 56 * 1024 * 1024
CONV_VMEM_LIMIT = 60 * 1024 * 1024


def _silu(x):
    h = 0.5 * x
    return h + h * jnp.tanh(h)


@functools.lru_cache(maxsize=None)
def _host_dft_tables(seq, n2len):
    n1h = seq // n2len
    n_fft = 2 * seq
    k1 = np.arange(n1h, dtype=np.int64)
    odd = 2 * k1 + 1
    alpha = ((odd[:, None] * k1[None, :]) % (4 * n1h)) * (2.0 * math.pi / (4 * n1h))
    n2e = np.arange(n2len + 1, dtype=np.int64)
    beta = ((n2e[:, None] * odd[None, :]) % (2 * n_fft)) * (2.0 * math.pi / (2 * n_fft))
    n2i = np.arange(n2len, dtype=np.int64)
    phi = ((n2i[:, None] * n2i[None, :]) % n2len) * (2.0 * math.pi / n2len)
    cm, sm = np.cos(phi), np.sin(phi)
    h2 = np.block([[cm, sm], [-sm, cm]])
    h2 = h2.reshape(2 * n2len, 2, n2len).transpose(0, 2, 1).reshape(2 * n2len, 2 * n2len)
    h2i = np.block([[cm, -sm], [sm, cm]])
    h2i = h2i.reshape(2, n2len, 2 * n2len).transpose(1, 0, 2).reshape(2 * n2len, 2 * n2len)
    h2, h2i = h2.astype(BF16), h2i.astype(BF16)
    quarter = 0.5 * math.pi * np.tile(np.array([0.0, 1.0]), n1h)[:, None]
    alpha2 = np.repeat(alpha, 2, axis=0)
    beta2 = np.repeat(beta, 2, axis=1)
    f32 = lambda a: a.astype(np.float32)
    seeds = tuple(f32(t) for t in (
        np.cos(alpha2 + quarter), np.sin(alpha2 + quarter),
        np.cos(alpha2 - quarter), np.sin(alpha2 - quarter), np.cos(beta2), np.sin(beta2)))
    return seeds + (h2, h2i)


def _dft_tables(seq, n2len):
    caf, saf, car, sar, cb, sb, h2, h2i = _host_dft_tables(seq, n2len)
    cb, sb = jnp.asarray(cb)[:, :, None], jnp.asarray(sb)[:, :, None]
    g_fwd = (caf[None] * cb[:-1] - saf[None] * sb[:-1]).astype(BF16)
    g_rev = (car[None] * cb[1:] - sar[None] * sb[1:]).astype(BF16)
    return g_fwd, g_rev, jnp.asarray(h2), jnp.asarray(h2i)


@functools.lru_cache(maxsize=None)
def _packed_pos_features(seq, emb, tile):
    hl = LANES // 2
    half = tile // 2
    bands = (emb - 1) // 2
    row = np.arange((seq // tile) * half, dtype=np.int64)[:, None]
    lane = np.arange(LANES, dtype=np.int64)[None, :]
    pos = ((row // half) * tile + (lane // hl) * half + row % half).astype(np.float64)
    feat = np.broadcast_to(lane % hl, pos.shape)
    t = pos / (seq - 1)
    w = (2.0 * math.pi / seq) * pos
    f = np.linspace(1e-4, bands - 1, bands)[(feat - 1) % bands]
    z = np.where(feat == 0, t,
                 np.where(feat <= bands, np.cos(f * w),
                          np.where(feat <= 2 * bands, -np.sin(f * w), 0.0)))
    return z.astype(np.float32)


def _mod_kernel(c_ref, w_ref, b_ref, o_ref):
    s = _silu(c_ref[...])
    o_ref[...] = jnp.dot(s, w_ref[...], precision=HIGHEST,
                         preferred_element_type=F32) + b_ref[...]


def _modulation(c, w_ada, b_ada):
    bsz, d = c.shape
    n_out = w_ada.shape[1]
    return pl.pallas_call(
        _mod_kernel,
        grid=(n_out // d,),
        in_specs=[pl.BlockSpec((bsz, d), lambda j: (0, 0)),
                  pl.BlockSpec((d, d), lambda j: (0, j)),
                  pl.BlockSpec((1, d), lambda j: (0, j))],
        out_specs=pl.BlockSpec((bsz, d), lambda j: (0, j)),
        out_shape=jax.ShapeDtypeStruct((bsz, n_out), F32),
        name="adaln_mod",
    )(c, w_ada, b_ada.reshape(1, n_out))


def _store_permuted(o_ref, val, n2len, first=0):
    for cb in range(val.shape[1] // LANES):
        for j in range(val.shape[0] // n2len):
            o_ref[cb, pl.ds(first + j, n2len, stride=SUBLANES), :] = val[
                j * n2len:(j + 1) * n2len, cb * LANES:(cb + 1) * LANES]


def _load_minor(x_ref, j):
    blk = x_ref[:, j * SUBLANES:(j + 1) * SUBLANES, :]
    return blk.reshape(blk.shape[0] * SUBLANES, blk.shape[2])


def _band_window(b, tile):
    last = tile + 2 * HALO - BAND_SPAN
    start = min(b * BAND_ROWS, last)
    return start, 0 if start == b * BAND_ROWS else 1


@functools.lru_cache(maxsize=None)
def _band_matrices(tile):
    t = np.arange(BAND_ROWS)[:, None]
    k = np.arange(BAND_SPAN)[None, :]
    mats = []
    for win in POOL_WINDOWS:
        half = win // 2
        per_variant = []
        for b in (0, tile // BAND_ROWS - 1):
            start, _ = _band_window(b, tile)
            rel = k + start - HALO - (b * BAND_ROWS + t)
            per_variant.append(((rel >= -half) & (rel < half)).astype(BF16))
        mats.append(np.stack(per_variant))
    return np.stack(mats)


def _proj_kernel(xp_ref, xc_ref, xn_ref, mod_ref, w_ref, b_ref, cw_ref, cb_ref, pw_ref,
                 pb_ref, ps_ref, band_ref, v_ref, x1_ref, x2_ref, yp_ref, p_scr, q_scr,
                 *, tile, n2len, ch, seq):
    i = pl.program_id(1)
    nt = pl.num_programs(1)
    shift = mod_ref[0:1, :]
    scale1 = 1.0 + mod_ref[1:2, :]
    xe = jnp.concatenate([xp_ref[...], xc_ref[...], xn_ref[...]], axis=0)
    ue = (xe * scale1 + shift).astype(BF16)
    uc = ue[HALO:HALO + tile, :]

    def zero_outside_sequence(p):
        return jnp.concatenate([jnp.where(i > 0, p[:HALO], 0.0), p[HALO:HALO + tile],
                                jnp.where(i < nt - 1, p[HALO + tile:], 0.0)], axis=0)

    for k, o_ref in enumerate((v_ref, x1_ref)):
        cols = slice(k * ch, (k + 1) * ch)
        p = jnp.dot(uc, w_ref[:, cols], preferred_element_type=F32) + b_ref[:, cols]
        _store_permuted(o_ref, p, n2len)

    cols = slice(2 * ch, 3 * ch)
    p = jnp.dot(ue, w_ref[:, cols], preferred_element_type=F32) + b_ref[:, cols]
    p_scr[...] = zero_outside_sequence(p)
    hg = jnp.dot(uc, w_ref[:, 3 * ch:4 * ch], preferred_element_type=F32) + b_ref[:, 3 * ch:4 * ch]
    gate = _silu(hg)
    s = (cb_ref[:, cols]
         + cw_ref[0:1, cols] * p_scr[HALO - 1:HALO - 1 + tile, :]
         + cw_ref[1:2, cols] * p_scr[HALO:HALO + tile, :]
         + cw_ref[2:3, cols] * p_scr[HALO + 1:HALO + 1 + tile, :])
    _store_permuted(x2_ref, s * gate, n2len)

    dp = ps_ref.shape[1]
    c0 = 4 * ch
    pin = jnp.dot(ue, pw_ref[...], preferred_element_type=F32) + pb_ref[...]
    pin = zero_outside_sequence(pin)
    p_scr[...] = pin
    q_scr[...] = pin.astype(BF16)
    pgate = jnp.dot(uc, w_ref[:, c0 + dp:c0 + 2 * dp], preferred_element_type=F32) + b_ref[:, c0 + dp:c0 + 2 * dp]
    gate_p = _silu(pgate)
    pos = i * tile + lax.broadcasted_iota(jnp.int32, (tile, 1), 0)
    pg = dp // len(POOL_WINDOWS)
    for g, win in enumerate(POOL_WINDOWS):
        lanes = slice(g * pg, (g + 1) * pg)
        half = win // 2
        sums = []
        for b in range(tile // BAND_ROWS):
            start, variant = _band_window(b, tile)
            sums.append(jnp.dot(band_ref[g, variant], q_scr[start:start + BAND_SPAN, lanes],
                                preferred_element_type=F32))
        acc = jnp.concatenate(sums, axis=0)
        cnt = (jnp.minimum(pos + half, seq) - jnp.maximum(pos - half, 0)).astype(F32)
        yg = (acc / cnt - p_scr[HALO:HALO + tile, lanes]) * ps_ref[:, lanes] * gate_p[:, lanes]
        yp_ref[:, lanes] = yg.astype(yp_ref.dtype)


def _fold_kernel(w_ref, b_ref, m_ref, wo_ref, bo_ref):
    wo_ref[...] = jnp.dot(w_ref[...], m_ref[...], precision=HIGHEST, preferred_element_type=F32)
    bo_ref[...] = jnp.dot(b_ref[...], m_ref[...], precision=HIGHEST, preferred_element_type=F32)


def _fold_pool_map(w_pool, b_pool, pool_w):
    d, dp = w_pool.shape
    n_groups, pg, _ = pool_w.shape
    return pl.pallas_call(
        _fold_kernel,
        grid=(n_groups,),
        in_specs=[pl.BlockSpec((d, pg), lambda g: (0, g)),
                  pl.BlockSpec((1, pg), lambda g: (0, g)),
                  pl.BlockSpec((None, pg, pg), lambda g: (g, 0, 0))],
        out_specs=[pl.BlockSpec((d, pg), lambda g: (0, g)),
                   pl.BlockSpec((1, pg), lambda g: (0, g))],
        out_shape=[jax.ShapeDtypeStruct((d, dp), F32), jax.ShapeDtypeStruct((1, dp), F32)],
        name="fold_pool_map",
    )(w_pool, b_pool, pool_w)


def _input_projection(x, mod3, w_in, b_in, conv_w, conv_b, pool_w, pool_scale, n2len):
    bsz, seq, d = x.shape
    ch = conv_w.shape[1] // 3
    dp = pool_scale.shape[0]
    n1h = seq // n2len
    tile = SUBLANES * n2len
    nt = seq // tile
    hb = tile // HALO
    n_proj = w_in.shape[1]
    cbn = ch // LANES
    assert max(POOL_WINDOWS) // 2 <= HALO and dp == ch and tile % BAND_ROWS == 0
    band = jnp.asarray(_band_matrices(tile))
    c0 = 4 * ch
    w_pool, b_pool = _fold_pool_map(w_in[:, c0:c0 + dp], b_in[c0:c0 + dp].reshape(1, dp), pool_w)
    perm = jax.ShapeDtypeStruct((bsz, n1h // SUBLANES, cbn, tile, LANES), F32)
    perm_spec = pl.BlockSpec((None, None, cbn, tile, LANES), lambda b, i: (b, i, 0, 0, 0))
    full = lambda shape: pl.BlockSpec(shape, lambda b, i: (0,) * len(shape))
    kern = functools.partial(_proj_kernel, tile=tile, n2len=n2len, ch=ch, seq=seq)
    return pl.pallas_call(
        kern,
        grid=(bsz, nt),
        in_specs=[
            pl.BlockSpec((None, HALO, d), lambda b, i: (b, jnp.maximum(i * hb - 1, 0), 0)),
            pl.BlockSpec((None, tile, d), lambda b, i: (b, i, 0)),
            pl.BlockSpec((None, HALO, d), lambda b, i: (b, jnp.minimum((i + 1) * hb, seq // HALO - 1), 0)),
            pl.BlockSpec((None, 3, d), lambda b, i: (b, 0, 0)),
            full((d, n_proj)), full((1, n_proj)), full((3, 3 * ch)), full((1, 3 * ch)),
            full((d, dp)), full((1, dp)), full((1, dp)), full(band.shape),
        ],
        out_specs=[perm_spec, perm_spec, perm_spec,
                   pl.BlockSpec((None, tile, dp), lambda b, i: (b, i, 0))],
        out_shape=[perm, perm, perm, jax.ShapeDtypeStruct((bsz, seq, dp), BF16)],
        scratch_shapes=[pltpu.VMEM((tile + 2 * HALO, ch), F32),
                        pltpu.VMEM((tile + 2 * HALO, dp), BF16)],
        compiler_params=pltpu.CompilerParams(
            dimension_semantics=("parallel", "arbitrary"), vmem_limit_bytes=VMEM_LIMIT),
        name="in_proj_conv_pool",
    )(x, x, x, mod3, w_in.astype(BF16), b_in.reshape(1, n_proj), conv_w,
      conv_b.reshape(1, 3 * ch), w_pool.astype(BF16), b_pool, pool_scale.reshape(1, dp), band)


def _filter_mlp_kernel(z_ref, w1_ref, b1_ref, wi_ref, bi_ref, fr_ref, wo_ref, ad_ref,
                       of_ref, or_ref, *, n2len):
    z = z_ref[...]
    fr = fr_ref[...]
    h = jnp.sin(fr * (jnp.dot(z.astype(BF16), w1_ref[...],
                              preferred_element_type=F32) + b1_ref[...]))
    for l in range(wi_ref.shape[0]):
        h = jnp.sin(fr * (jnp.dot(h.astype(BF16), wi_ref[l],
                                  preferred_element_type=F32) + bi_ref[l:l + 1, :]))
    hb = h.astype(BF16)
    half_lanes = LANES // 2
    n1_half = z.shape[0] // n2len
    for half in range(2):
        t = z[:, half * half_lanes:half * half_lanes + 1]
        decay = jnp.exp(-t * ad_ref[...])
        for d, o_ref in enumerate((of_ref, or_ref)):
            k = jnp.dot(hb, wo_ref[d, half], preferred_element_type=F32)
            _store_permuted(o_ref, k * decay, n2len, first=half * n1_half)


def _block_diag2(w):
    zero = jnp.zeros_like(w)
    return jnp.concatenate([jnp.concatenate([w, zero], axis=-1),
                            jnp.concatenate([zero, w], axis=-1)], axis=-2)


def _filter_taps(seq, n2len, w1, b1, w_inner, b_inner, w_out, freq, n_ch, order):
    emb, hid = w1.shape
    n_inner = w_inner.shape[0]
    oc = order * n_ch
    hl = LANES // 2
    assert emb <= hl and hid <= hl
    tile = SUBLANES * n2len
    half = tile // 2
    groups = seq // tile
    zp = jnp.asarray(_packed_pos_features(seq, emb, tile))
    ph = hl - hid
    w1d = _block_diag2(jnp.pad(w1, ((0, hl - emb), (0, ph))))
    b1d = jnp.tile(jnp.pad(b1, (0, ph)), 2).reshape(1, LANES)
    wid = _block_diag2(jnp.pad(w_inner, ((0, 0), (0, ph), (0, ph))))
    bid = jnp.tile(jnp.pad(b_inner, ((0, 0), (0, ph))), (1, 2))
    frd = jnp.tile(jnp.pad(freq, (0, ph), constant_values=1.0), 2).reshape(1, LANES)
    wo = jnp.transpose(w_out.reshape(hid, order, 2, n_ch), (2, 0, 1, 3)).reshape(2, hid, oc)
    wo = jnp.pad(wo, ((0, 0), (0, ph), (0, 0)))
    zero = jnp.zeros_like(wo)
    wo4 = jnp.stack([jnp.concatenate([wo, zero], axis=1),
                     jnp.concatenate([zero, wo], axis=1)], axis=1)
    min_decay = math.log(DECAY_TARGET) / SLOW_DECAY_PCT
    max_decay = math.log(DECAY_TARGET) / FAST_DECAY_PCT
    absdelta = jnp.abs(jnp.linspace(min_decay, max_decay, n_ch, dtype=F32))
    absdelta = jnp.tile(absdelta, order).reshape(1, oc)

    full = lambda shape: pl.BlockSpec(shape, lambda i: (0,) * len(shape))
    kern = functools.partial(_filter_mlp_kernel, n2len=n2len)
    taps = jax.ShapeDtypeStruct((groups, oc // LANES, tile, LANES), F32)
    taps_spec = pl.BlockSpec((None, oc // LANES, tile, LANES), lambda i: (i, 0, 0, 0))
    return pl.pallas_call(
        kern,
        grid=(groups,),
        in_specs=[
            pl.BlockSpec((half, LANES), lambda i: (i, 0)),
            full((LANES, LANES)), full((1, LANES)), full((n_inner, LANES, LANES)),
            full((n_inner, LANES)), full((1, LANES)), full((2, 2, LANES, oc)), full((1, oc)),
        ],
        out_specs=[taps_spec, taps_spec],
        out_shape=[taps, taps],
        compiler_params=pltpu.CompilerParams(
            dimension_semantics=("arbitrary",), vmem_limit_bytes=VMEM_LIMIT),
        name="filter_mlp",
    )(zp, w1d.astype(BF16), b1d, wid.astype(BF16), bid, frd, wo4.astype(BF16), absdelta)


def _resident(shape):
    return pl.BlockSpec(shape, lambda *_: (0,) * len(shape), pipeline_mode=pl.Buffered(1))


def _pack_pairs(x):
    return pltpu.bitcast(x.astype(BF16), jnp.uint32)


def _unpack_pairs(w):
    return pltpu.bitcast(w, BF16)


def _spectrum_view(a_scr, chunk, ck1, pitch):
    start = pl.multiple_of(chunk * (ck1 * pitch), SUBLANES)
    return a_scr.at[pl.ds(start, ck1 * pitch)]


def _spectrum_pair(view, j, n2len, pitch):
    return jnp.concatenate([_unpack_pairs(view[jj * pitch:jj * pitch + n2len, :])
                            for jj in (j, j + 1)], axis=1)


def _filter_fft_kernel(hf_ref, hr_ref, g_ref, gr_ref, h2_ref, o_ref, a_scr,
                       *, ns, cn2, ck1, n1h, n2len, pitch):
    s = pl.program_id(1)

    @pl.when(s < ns)
    def _():
        for j in range(cn2):
            jr = cn2 - 1 - j
            m = s * cn2 + j
            g = jnp.concatenate([g_ref[m], gr_ref[n2len - 1 - m]], axis=1)
            x = jnp.concatenate([_load_minor(hf_ref, j), _load_minor(hr_ref, jr)], axis=0)
            r = jnp.dot(g, x.astype(BF16), preferred_element_type=F32)
            a_scr[pl.ds(m, n1h, stride=pitch), :] = _pack_pairs(r)

    @pl.when(s >= ns)
    def _():
        view = _spectrum_view(a_scr, s - ns, ck1, pitch)
        for j in range(0, ck1, 2):
            rows = _spectrum_pair(view, j, n2len, pitch)
            u = jnp.dot(h2_ref[...], rows, preferred_element_type=F32)
            u = (u * (1.0 / (n1h * n2len))).astype(o_ref.dtype)
            o_ref[j] = u[:, :LANES]
            o_ref[j + 1] = u[:, LANES:]


def _filter_spectrum(taps_f, taps_r, g_fwd, g_rev, h2, n2len):
    groups, ocb, _, _ = taps_f.shape
    oc = ocb * LANES
    n1h = groups * SUBLANES
    ns, ns2 = FILTER_STEPS
    cn2, ck1 = n2len // ns, n1h // ns2
    pitch = n2len + SUBLANES
    kern = functools.partial(_filter_fft_kernel, ns=ns, cn2=cn2, ck1=ck1, n1h=n1h,
                             n2len=n2len, pitch=pitch)
    return pl.pallas_call(
        kern,
        grid=(oc // LANES, ns + ns2),
        in_specs=[
            pl.BlockSpec((groups, None, cn2 * SUBLANES, LANES),
                         lambda c, s: (0, c, jnp.minimum(s, ns - 1), 0)),
            pl.BlockSpec((groups, None, cn2 * SUBLANES, LANES),
                         lambda c, s: (0, c, jnp.maximum(ns - 1 - s, 0), 0)),
            _resident(g_fwd.shape), _resident(g_rev.shape), _resident(h2.shape),
        ],
        out_specs=pl.BlockSpec((ck1, 2 * n2len, LANES), lambda c, s: (jnp.maximum(s - ns, 0), 0, c)),
        out_shape=jax.ShapeDtypeStruct((n1h, 2 * n2len, oc), BF16),
        scratch_shapes=[pltpu.VMEM((n1h * pitch, LANES), jnp.uint32)],
        compiler_params=pltpu.CompilerParams(
            dimension_semantics=("parallel", "arbitrary"), vmem_limit_bytes=VMEM_LIMIT),
        name="filter_spectrum",
    )(taps_f, taps_r, g_fwd, g_rev, h2)


def _short_conv_slab(x_ref, lo_ref, hi_ref, taps_ref, j, count, first_chunk, last_chunk):
    cur = _load_minor(x_ref, j)
    zero_row = jnp.zeros((1, cur.shape[1]), F32)
    if j > 0:
        prev = _load_minor(x_ref, j - 1)
    else:
        prev = _load_minor(lo_ref, 0)
        prev = jnp.where(first_chunk, jnp.concatenate([zero_row, prev[:-1]], axis=0), prev)
    if j < count - 1:
        nxt = _load_minor(x_ref, j + 1)
    else:
        nxt = _load_minor(hi_ref, 0)
        nxt = jnp.where(last_chunk, jnp.concatenate([nxt[1:], zero_row], axis=0), nxt)
    return (taps_ref[3:4, :] + taps_ref[0:1, :] * prev + taps_ref[1:2, :] * cur
            + taps_ref[2:3, :] * nxt)


def _long_conv_kernel(*refs, steps, n1h, n2len, pitch, conv_in, conv_mult):
    refs = list(refs)
    vf_ref = refs.pop(0)
    vlo_ref, vhi_ref, vtaps_ref = (refs.pop(0), refs.pop(0), refs.pop(0)) if conv_in else (None,) * 3
    g_ref, kf_ref, h2_ref, h2i_ref, xm_ref = (refs.pop(0) for _ in range(5))
    xlo_ref, xhi_ref, xtaps_ref = (refs.pop(0), refs.pop(0), refs.pop(0)) if conv_mult else (None,) * 3
    hb_ref, o_ref, a_scr, keep_scr = refs
    ns1, ns2, ns3 = steps
    cn2a, ck1, cn2 = n2len // ns1, n1h // ns2, n2len // ns3
    s = pl.program_id(2)

    @pl.when(s < ns1)
    def _():
        base = s * cn2a
        for j in range(cn2a):
            if conv_in:
                x = _short_conv_slab(vf_ref, vlo_ref, vhi_ref, vtaps_ref, j, cn2a,
                                     s == 0, s == ns1 - 1)
            else:
                x = _load_minor(vf_ref, j)
            keep_scr[base + j] = x
            r = jnp.dot(g_ref[base + j], x.astype(BF16), preferred_element_type=F32)
            a_scr[pl.ds(base + j, n1h, stride=pitch), :] = _pack_pairs(r)

    @pl.when((s >= ns1) & (s < ns1 + ns2))
    def _():
        view = _spectrum_view(a_scr, s - ns1, ck1, pitch)
        for j in range(0, ck1, 2):
            rows = _spectrum_pair(view, j, n2len, pitch)
            u = jnp.dot(h2_ref[...], rows, preferred_element_type=F32)
            ur, ui = u[:n2len], u[n2len:]
            kf = jnp.concatenate([kf_ref[j], kf_ref[j + 1]], axis=1).astype(F32)
            kr, ki = kf[:n2len], kf[n2len:]
            y = jnp.concatenate([ur * kr - ui * ki, ur * ki + ui * kr], axis=0)
            bb = _pack_pairs(jnp.dot(h2i_ref[...], y.astype(BF16), preferred_element_type=F32))
            for jj in range(2):
                view[(j + jj) * pitch:(j + jj) * pitch + n2len, :] = bb[
                    :, jj * LANES:(jj + 1) * LANES]

    @pl.when(s >= ns1 + ns2)
    def _():
        base = (s - ns1 - ns2) * cn2
        for j in range(cn2):
            rows = _unpack_pairs(a_scr[pl.ds(base + j, n1h, stride=pitch), :])
            y = lax.dot_general(g_ref[base + j], rows, (((0,), (0,)), ((), ())),
                                preferred_element_type=F32)
            if conv_mult:
                xm = _short_conv_slab(xm_ref, xlo_ref, xhi_ref, xtaps_ref, j, cn2,
                                      s == ns1 + ns2, s == ns1 + ns2 + ns3 - 1)
            else:
                xm = _load_minor(xm_ref, j)
            z = xm * (y + hb_ref[...] * keep_scr[base + j])
            o_ref[:, j * SUBLANES:(j + 1) * SUBLANES, :] = z.reshape(
                n1h // SUBLANES, SUBLANES, z.shape[1])


def _long_conv(u_perm, mult_perm, kf, hbias, order_idx, tables, in_taps=None, mult_taps=None):
    g_fwd, _, h2, h2i = tables
    bsz, groups, cbn, tile, _ = u_perm.shape
    n2len = tile // SUBLANES
    n1h = groups * SUBLANES
    steps = CONV_STEPS
    ns1, ns2, ns3 = steps
    cn2a, cn2 = n2len // ns1, n2len // ns3
    pitch = n2len + SUBLANES

    def early(s):
        return jnp.minimum(s, ns1 - 1)

    def late(s):
        return jnp.clip(s - ns1 - ns2, 0, ns3 - 1)

    def chunk_specs(chunk_of, width):
        slab = (None, groups, None, SUBLANES, LANES)
        return [
            pl.BlockSpec((None, groups, None, width * SUBLANES, LANES),
                         lambda c, b, s: (b, 0, c, chunk_of(s), 0)),
            pl.BlockSpec(slab, lambda c, b, s: (b, 0, c, (chunk_of(s) * width - 1) % n2len, 0)),
            pl.BlockSpec(slab, lambda c, b, s: (b, 0, c, ((chunk_of(s) + 1) * width) % n2len, 0)),
        ]

    taps_spec = pl.BlockSpec((4, LANES), lambda c, b, s: (0, c))
    in_specs, args = [], []
    specs = chunk_specs(early, cn2a)
    if in_taps is None:
        in_specs += specs[:1]
        args += [u_perm]
    else:
        in_specs += specs + [taps_spec]
        args += [u_perm, u_perm, u_perm, in_taps]
    in_specs += [
        _resident(g_fwd.shape),
        pl.BlockSpec((n1h // ns2, 2 * n2len, LANES),
                     lambda c, b, s: (jnp.clip(s - ns1, 0, ns2 - 1), 0, order_idx * cbn + c)),
        _resident(h2.shape), _resident(h2i.shape),
    ]
    args += [g_fwd, kf, h2, h2i]
    specs = chunk_specs(late, cn2)
    if mult_taps is None:
        in_specs += specs[:1]
        args += [mult_perm]
    else:
        in_specs += specs + [taps_spec]
        args += [mult_perm, mult_perm, mult_perm, mult_taps]
    in_specs.append(pl.BlockSpec((None, 1, LANES), lambda c, b, s: (order_idx, 0, c)))
    args.append(hbias)
    data_late = specs[0]
    kern = functools.partial(_long_conv_kernel, steps=steps, n1h=n1h, n2len=n2len,
                             pitch=pitch, conv_in=in_taps is not None,
                             conv_mult=mult_taps is not None)
    return pl.pallas_call(
        kern,
        grid=(cbn, bsz, sum(steps)),
        in_specs=in_specs,
        out_specs=data_late,
        out_shape=jax.ShapeDtypeStruct(u_perm.shape, F32),
        scratch_shapes=[pltpu.VMEM((n1h * pitch, LANES), jnp.uint32),
                        pltpu.VMEM((n2len, n1h, LANES), F32)],
        compiler_params=pltpu.CompilerParams(
            dimension_semantics=("parallel", "parallel", "arbitrary"),
            vmem_limit_bytes=CONV_VMEM_LIMIT),
        name="long_conv",
    )(*args)


def _out_kernel(x_ref, yh_ref, yp_ref, mod_ref, w_ref, b_ref, g_ref, beta_ref, o_ref,
                *, n2len, alpha):
    cbn = yh_ref.shape[0]
    yh = jnp.concatenate(
        [jnp.concatenate([yh_ref[cb, pl.ds(j, n2len, stride=SUBLANES), :]
                          for cb in range(cbn)], axis=1)
         for j in range(SUBLANES)], axis=0)
    y = jnp.concatenate([yh.astype(BF16), yp_ref[...]], axis=1)
    acc = jnp.dot(y, w_ref[...], preferred_element_type=F32)
    h = alpha * x_ref[...] + mod_ref[2:3, :] * (acc + b_ref[...])
    mu = jnp.mean(h, axis=-1, keepdims=True)
    hc = h - mu
    var = jnp.mean(hc * hc, axis=-1, keepdims=True)
    o_ref[...] = hc * lax.rsqrt(var + LN_EPS) * g_ref[...] + beta_ref[...]


def _output_projection(x, yh_perm, yp, mod3, w_out, b_out, ln_g, ln_b, alpha):
    bsz, seq, d = x.shape
    _, _, cbn, tile, _ = yh_perm.shape
    ch = cbn * LANES
    n2len = tile // SUBLANES
    dp = yp.shape[2]
    full = lambda shape: pl.BlockSpec(shape, lambda b, i: (0,) * len(shape))
    kern = functools.partial(_out_kernel, n2len=n2len, alpha=alpha)
    return pl.pallas_call(
        kern,
        grid=(bsz, seq // tile),
        in_specs=[
            pl.BlockSpec((None, tile, d), lambda b, i: (b, i, 0)),
            pl.BlockSpec((None, None, cbn, tile, LANES), lambda b, i: (b, i, 0, 0, 0)),
            pl.BlockSpec((None, tile, dp), lambda b, i: (b, i, 0)),
            pl.BlockSpec((None, 3, d), lambda b, i: (b, 0, 0)),
            full((ch + dp, d)), full((1, d)), full((1, d)), full((1, d)),
        ],
        out_specs=pl.BlockSpec((None, tile, d), lambda b, i: (b, i, 0)),
        out_shape=jax.ShapeDtypeStruct(x.shape, x.dtype),
        compiler_params=pltpu.CompilerParams(
            dimension_semantics=("parallel", "arbitrary"), vmem_limit_bytes=VMEM_LIMIT),
        name="out_proj_deepnorm",
    )(x, yh_perm, yp, mod3, w_out.astype(BF16), b_out.reshape(1, d), ln_g.reshape(1, d),
      ln_b.reshape(1, d))


def _forward(x, c, w_ada, b_ada, w_in, b_in, conv_w, conv_b, filt_w1, filt_b1, filt_w_inner,
             filt_b_inner, filt_w_out, filt_freq, hyena_bias, pool_w, pool_scale, w_out,
             b_out, ln_g, ln_b, *, n2len):
    bsz, seq, d = x.shape
    depth = w_ada.shape[0]
    order, n_ch = hyena_bias.shape[1], hyena_bias.shape[2]
    alpha = (2.0 * depth) ** 0.25
    tables = _dft_tables(seq, n2len)
    h = x
    for layer in range(depth):
        mod3 = _modulation(c, w_ada[layer], b_ada[layer]).reshape(bsz, 3, d)
        v, x1, x2g, yp = _input_projection(
            h, mod3, w_in[layer], b_in[layer], conv_w[layer], conv_b[layer], pool_w[layer],
            pool_scale[layer], n2len)
        taps_f, taps_r = _filter_taps(
            seq, n2len, filt_w1[layer], filt_b1[layer], filt_w_inner[layer],
            filt_b_inner[layer], filt_w_out[layer], filt_freq[layer], n_ch, order)
        kf = _filter_spectrum(taps_f, taps_r, tables[0], tables[1], tables[2], n2len)
        hbias = hyena_bias[layer].reshape(order, 1, n_ch)
        taps = [jnp.concatenate([conv_w[layer][:, k * n_ch:(k + 1) * n_ch],
                                 conv_b[layer][None, k * n_ch:(k + 1) * n_ch]], axis=0)
                for k in range(2)]
        z = _long_conv(v, x1, kf, hbias, 0, tables, in_taps=taps[0], mult_taps=taps[1])
        for o in range(1, order):
            z = _long_conv(z, x2g, kf, hbias, o, tables)
        h = _output_projection(h, z, yp, mod3, w_out[layer], b_out[layer], ln_g[layer],
                               ln_b[layer], alpha)
    return h


def kernel(x, c, w_ada, b_ada, w_in, b_in, conv_w, conv_b, filt_w1, filt_b1, filt_w_inner,
           filt_b_inner, filt_w_out, filt_freq, hyena_bias, pool_w, pool_scale, w_out, b_out,
           ln_g, ln_b):
    return _forward(x, c, w_ada, b_ada, w_in, b_in, conv_w, conv_b, filt_w1, filt_b1,
                    filt_w_inner, filt_b_inner, filt_w_out, filt_freq, hyena_bias, pool_w,
                    pool_scale, w_out, b_out, ln_g, ln_b, n2len=MINOR_LEN)
```

```python
import functools
import math

import jax
import jax.numpy as jnp
import numpy as np
from jax import lax
from jax.experimental import pallas as pl
from jax.experimental.pallas import tpu as pltpu

F32 = jnp.float32
BF16 = jnp.bfloat16
HIGHEST = lax.Precision.HIGHEST

POOL_WINDOWS = (2, 4, 8, 16)
LN_EPS = 1e-5
DECAY_TARGET = 1e-2
FAST_DECAY_PCT = 0.3
SLOW_DECAY_PCT = 1.5

LANES = 128
SUBLANES = 8
HALO = 16
BAND_ROWS = 128
BAND_SPAN = 256
MINOR_LEN = 128
FILTER_STEPS = (4, 4)
FILTER_COL_BLOCKS = 2
CONV_STEPS = (2, 2, 2)
V7X_VMEM_BYTES = 64 * 1024 * 1024
VMEM_LIMIT = V7X_VMEM_BYTES * 7 // 8
CONV_VMEM_LIMIT = V7X_VMEM_BYTES * 15 // 16


def _silu(x):
    h = 0.5 * x
    return h + h * jnp.tanh(h)


@functools.lru_cache(maxsize=None)
def _host_dft_tables(seq, n2len):
    n1h = seq // n2len
    n_fft = 2 * seq
    k1 = np.arange(n1h, dtype=np.int64)
    odd = 2 * k1 + 1
    alpha = ((odd[:, None] * k1[None, :]) % (4 * n1h)) * (2.0 * math.pi / (4 * n1h))
    n2e = np.arange(n2len + 1, dtype=np.int64)
    beta = ((n2e[:, None] * odd[None, :]) % (2 * n_fft)) * (2.0 * math.pi / (2 * n_fft))
    n2i = np.arange(n2len, dtype=np.int64)
    phi = ((n2i[:, None] * n2i[None, :]) % n2len) * (2.0 * math.pi / n2len)
    cm, sm = np.cos(phi), np.sin(phi)
    h2 = np.block([[cm, sm], [-sm, cm]])
    h2 = h2.reshape(2 * n2len, 2, n2len).transpose(0, 2, 1).reshape(2 * n2len, 2 * n2len)
    h2i = np.block([[cm, -sm], [sm, cm]])
    h2i = h2i.reshape(2, n2len, 2 * n2len).transpose(1, 0, 2).reshape(2 * n2len, 2 * n2len)
    h2, h2i = h2.astype(BF16), h2i.astype(BF16)
    quarter = 0.5 * math.pi * np.tile(np.array([0.0, 1.0]), n1h)[:, None]
    alpha2 = np.repeat(alpha, 2, axis=0)
    beta2 = np.repeat(beta, 2, axis=1)
    f32 = lambda a: a.astype(np.float32)
    seeds = tuple(f32(t) for t in (
        np.cos(alpha2 + quarter), np.sin(alpha2 + quarter),
        np.cos(alpha2 - quarter), np.sin(alpha2 - quarter), np.cos(beta2), np.sin(beta2)))
    return seeds + (h2, h2i)


def _dft_tables(seq, n2len):
    caf, saf, car, sar, cb, sb, h2, h2i = _host_dft_tables(seq, n2len)
    cb, sb = jnp.asarray(cb)[:, :, None], jnp.asarray(sb)[:, :, None]
    g_fwd = (caf[None] * cb[:-1] - saf[None] * sb[:-1]).astype(BF16)
    g_rev = (car[None] * cb[1:] - sar[None] * sb[1:]).astype(BF16)
    return g_fwd, g_rev, jnp.asarray(h2), jnp.asarray(h2i)


@functools.lru_cache(maxsize=None)
def _packed_pos_features(seq, emb, tile):
    hl = LANES // 2
    half = tile // 2
    bands = (emb - 1) // 2
    row = np.arange((seq // tile) * half, dtype=np.int64)[:, None]
    lane = np.arange(LANES, dtype=np.int64)[None, :]
    pos = ((row // half) * tile + (lane // hl) * half + row % half).astype(np.float64)
    feat = np.broadcast_to(lane % hl, pos.shape)
    t = pos / (seq - 1)
    w = (2.0 * math.pi / seq) * pos
    f = np.linspace(1e-4, bands - 1, bands)[(feat - 1) % bands]
    z = np.where(feat == 0, t,
                 np.where(feat <= bands, np.cos(f * w),
                          np.where(feat <= 2 * bands, -np.sin(f * w), 0.0)))
    return z.astype(np.float32)


def _mod_kernel(c_ref, w_ref, b_ref, o_ref):
    s = _silu(c_ref[...])
    o_ref[...] = jnp.dot(s, w_ref[...], precision=HIGHEST,
                         preferred_element_type=F32) + b_ref[...]


def _modulation(c, w_ada, b_ada):
    bsz, d = c.shape
    n_out = w_ada.shape[1]
    return pl.pallas_call(
        _mod_kernel,
        grid=(n_out // d,),
        in_specs=[pl.BlockSpec((bsz, d), lambda j: (0, 0)),
                  pl.BlockSpec((d, d), lambda j: (0, j)),
                  pl.BlockSpec((1, d), lambda j: (0, j))],
        out_specs=pl.BlockSpec((bsz, d), lambda j: (0, j)),
        out_shape=jax.ShapeDtypeStruct((bsz, n_out), F32),
        name="adaln_mod",
    )(c, w_ada, b_ada.reshape(1, n_out))


def _store_permuted(o_ref, val, n2len, first=0):
    for cb in range(val.shape[1] // LANES):
        for j in range(val.shape[0] // n2len):
            o_ref[cb, pl.ds(first + j, n2len, stride=SUBLANES), :] = val[
                j * n2len:(j + 1) * n2len, cb * LANES:(cb + 1) * LANES]


def _load_minor(x_ref, j):
    blk = x_ref[:, j * SUBLANES:(j + 1) * SUBLANES, :]
    return blk.reshape(blk.shape[0] * SUBLANES, blk.shape[2])


def _band_window(b, tile):
    last = tile + 2 * HALO - BAND_SPAN
    start = min(b * BAND_ROWS, last)
    return start, 0 if start == b * BAND_ROWS else 1


@functools.lru_cache(maxsize=None)
def _band_matrices(tile):
    t = np.arange(BAND_ROWS)[:, None]
    k = np.arange(BAND_SPAN)[None, :]
    mats = []
    for win in POOL_WINDOWS:
        half = win // 2
        per_variant = []
        for b in (0, tile // BAND_ROWS - 1):
            start, _ = _band_window(b, tile)
            rel = k + start - HALO - (b * BAND_ROWS + t)
            per_variant.append(((rel >= -half) & (rel < half)).astype(BF16))
        mats.append(np.stack(per_variant))
    return np.stack(mats)


def _proj_kernel(xp_ref, xc_ref, xn_ref, mod_ref, w_ref, b_ref, cw_ref, cb_ref, pw_ref,
                 ps_ref, band_ref, v_ref, x1_ref, x2_ref, yp_ref, p_scr, q_scr,
                 *, tile, n2len, ch, seq):
    i = pl.program_id(1)
    nt = pl.num_programs(1)
    shift = mod_ref[0:1, :]
    scale1 = 1.0 + mod_ref[1:2, :]
    xe = jnp.concatenate([xp_ref[...], xc_ref[...], xn_ref[...]], axis=0)
    ue = (xe * scale1 + shift).astype(BF16)
    uc = ue[HALO:HALO + tile, :]

    def zero_outside_sequence(p):
        return jnp.concatenate([jnp.where(i > 0, p[:HALO], 0.0), p[HALO:HALO + tile],
                                jnp.where(i < nt - 1, p[HALO + tile:], 0.0)], axis=0)

    for k, o_ref in enumerate((v_ref, x1_ref)):
        cols = slice(k * ch, (k + 1) * ch)
        p = jnp.dot(uc, w_ref[:, cols], preferred_element_type=F32) + b_ref[:, cols]
        _store_permuted(o_ref, p, n2len)

    cols = slice(2 * ch, 3 * ch)
    p = jnp.dot(ue, w_ref[:, cols], preferred_element_type=F32) + b_ref[:, cols]
    p_scr[...] = zero_outside_sequence(p)
    hg = jnp.dot(uc, w_ref[:, 3 * ch:4 * ch], preferred_element_type=F32) + b_ref[:, 3 * ch:4 * ch]
    gate = _silu(hg)
    s = (cb_ref[:, cols]
         + cw_ref[0:1, cols] * p_scr[HALO - 1:HALO - 1 + tile, :]
         + cw_ref[1:2, cols] * p_scr[HALO:HALO + tile, :]
         + cw_ref[2:3, cols] * p_scr[HALO + 1:HALO + 1 + tile, :])
    _store_permuted(x2_ref, s * gate, n2len)

    dp = ps_ref.shape[1]
    c0 = 4 * ch
    pin = jnp.dot(ue, w_ref[:, c0:c0 + dp], preferred_element_type=F32) + b_ref[:, c0:c0 + dp]
    pin = zero_outside_sequence(pin)
    q_scr[...] = pin.astype(BF16)
    pin_c = pin[HALO:HALO + tile, :]
    pgate = jnp.dot(uc, w_ref[:, c0 + dp:c0 + 2 * dp], preferred_element_type=F32) + b_ref[:, c0 + dp:c0 + 2 * dp]
    pos = i * tile + lax.broadcasted_iota(jnp.int32, (tile, 1), 0)
    pg = dp // len(POOL_WINDOWS)
    groups = []
    for g, win in enumerate(POOL_WINDOWS):
        lanes = slice(g * pg, (g + 1) * pg)
        half = win // 2
        sums = []
        for b in range(tile // BAND_ROWS):
            start, variant = _band_window(b, tile)
            sums.append(jnp.dot(band_ref[g, variant], q_scr[start:start + BAND_SPAN, lanes],
                                preferred_element_type=F32))
        acc = jnp.concatenate(sums, axis=0)
        cnt = (jnp.minimum(pos + half, seq) - jnp.maximum(pos - half, 0)).astype(F32)
        diff = acc / cnt - pin_c[:, lanes]
        groups.append(jnp.dot(diff.astype(BF16), pw_ref[g], preferred_element_type=F32))
    yp = jnp.concatenate(groups, axis=1) * ps_ref[...] * _silu(pgate)
    yp_ref[...] = yp.astype(yp_ref.dtype)


def _input_projection(x, mod3, w_in, b_in, conv_w, conv_b, pool_w, pool_scale, n2len):
    bsz, seq, d = x.shape
    ch = conv_w.shape[1] // 3
    dp = pool_scale.shape[0]
    n1h = seq // n2len
    tile = SUBLANES * n2len
    nt = seq // tile
    hb = tile // HALO
    n_proj = w_in.shape[1]
    cbn = ch // LANES
    assert max(POOL_WINDOWS) // 2 <= HALO and dp == ch and tile % BAND_ROWS == 0
    band = jnp.asarray(_band_matrices(tile))
    perm = jax.ShapeDtypeStruct((bsz, n1h // SUBLANES, cbn, tile, LANES), F32)
    perm_spec = pl.BlockSpec((None, None, cbn, tile, LANES), lambda b, i: (b, i, 0, 0, 0))
    full = lambda shape: pl.BlockSpec(shape, lambda b, i: (0,) * len(shape))
    kern = functools.partial(_proj_kernel, tile=tile, n2len=n2len, ch=ch, seq=seq)
    return pl.pallas_call(
        kern,
        grid=(bsz, nt),
        in_specs=[
            pl.BlockSpec((None, HALO, d), lambda b, i: (b, jnp.maximum(i * hb - 1, 0), 0)),
            pl.BlockSpec((None, tile, d), lambda b, i: (b, i, 0)),
            pl.BlockSpec((None, HALO, d), lambda b, i: (b, jnp.minimum((i + 1) * hb, seq // HALO - 1), 0)),
            pl.BlockSpec((None, 3, d), lambda b, i: (b, 0, 0)),
            full((d, n_proj)), full((1, n_proj)), full((3, 3 * ch)), full((1, 3 * ch)),
            full(pool_w.shape), full((1, dp)), full(band.shape),
        ],
        out_specs=[perm_spec, perm_spec, perm_spec,
                   pl.BlockSpec((None, tile, dp), lambda b, i: (b, i, 0))],
        out_shape=[perm, perm, perm, jax.ShapeDtypeStruct((bsz, seq, dp), BF16)],
        scratch_shapes=[pltpu.VMEM((tile + 2 * HALO, ch), F32),
                        pltpu.VMEM((tile + 2 * HALO, dp), BF16)],
        compiler_params=pltpu.CompilerParams(
            dimension_semantics=("parallel", "arbitrary"), vmem_limit_bytes=VMEM_LIMIT),
        name="in_proj_conv_pool",
    )(x, x, x, mod3, w_in.astype(BF16), b_in.reshape(1, n_proj), conv_w,
      conv_b.reshape(1, 3 * ch), pool_w.astype(BF16), pool_scale.reshape(1, dp), band)


def _filter_mlp_kernel(z_ref, w1_ref, b1_ref, wi_ref, bi_ref, fr_ref, wo_ref, ad_ref,
                       of_ref, or_ref, *, n2len):
    z = z_ref[...]
    fr = fr_ref[...]
    h = jnp.sin(fr * (jnp.dot(z.astype(BF16), w1_ref[...],
                              preferred_element_type=F32) + b1_ref[...]))
    for l in range(wi_ref.shape[0]):
        h = jnp.sin(fr * (jnp.dot(h.astype(BF16), wi_ref[l],
                                  preferred_element_type=F32) + bi_ref[l:l + 1, :]))
    hb = h.astype(BF16)
    half_lanes = LANES // 2
    n1_half = z.shape[0] // n2len
    for half in range(2):
        t = z[:, half * half_lanes:half * half_lanes + 1]
        decay = jnp.exp(-t * ad_ref[...])
        for d, o_ref in enumerate((of_ref, or_ref)):
            k = jnp.dot(hb, wo_ref[d, half], preferred_element_type=F32)
            _store_permuted(o_ref, k * decay, n2len, first=half * n1_half)


def _block_diag2(w):
    zero = jnp.zeros_like(w)
    return jnp.concatenate([jnp.concatenate([w, zero], axis=-1),
                            jnp.concatenate([zero, w], axis=-1)], axis=-2)


def _filter_taps(seq, n2len, w1, b1, w_inner, b_inner, w_out, freq, n_ch, order):
    emb, hid = w1.shape
    n_inner = w_inner.shape[0]
    oc = order * n_ch
    hl = LANES // 2
    assert emb <= hl and hid <= hl
    tile = SUBLANES * n2len
    half = tile // 2
    groups = seq // tile
    zp = jnp.asarray(_packed_pos_features(seq, emb, tile))
    ph = hl - hid
    w1d = _block_diag2(jnp.pad(w1, ((0, hl - emb), (0, ph))))
    b1d = jnp.tile(jnp.pad(b1, (0, ph)), 2).reshape(1, LANES)
    wid = _block_diag2(jnp.pad(w_inner, ((0, 0), (0, ph), (0, ph))))
    bid = jnp.tile(jnp.pad(b_inner, ((0, 0), (0, ph))), (1, 2))
    frd = jnp.tile(jnp.pad(freq, (0, ph), constant_values=1.0), 2).reshape(1, LANES)
    wo = jnp.transpose(w_out.reshape(hid, order, 2, n_ch), (2, 0, 1, 3)).reshape(2, hid, oc)
    wo = jnp.pad(wo, ((0, 0), (0, ph), (0, 0)))
    zero = jnp.zeros_like(wo)
    wo4 = jnp.stack([jnp.concatenate([wo, zero], axis=1),
                     jnp.concatenate([zero, wo], axis=1)], axis=1)
    min_decay = math.log(DECAY_TARGET) / SLOW_DECAY_PCT
    max_decay = math.log(DECAY_TARGET) / FAST_DECAY_PCT
    absdelta = jnp.abs(jnp.linspace(min_decay, max_decay, n_ch, dtype=F32))
    absdelta = jnp.tile(absdelta, order).reshape(1, oc)

    full = lambda shape: pl.BlockSpec(shape, lambda i: (0,) * len(shape))
    kern = functools.partial(_filter_mlp_kernel, n2len=n2len)
    taps = jax.ShapeDtypeStruct((groups, oc // LANES, tile, LANES), F32)
    taps_spec = pl.BlockSpec((None, oc // LANES, tile, LANES), lambda i: (i, 0, 0, 0))
    return pl.pallas_call(
        kern,
        grid=(groups,),
        in_specs=[
            pl.BlockSpec((half, LANES), lambda i: (i, 0)),
            full((LANES, LANES)), full((1, LANES)), full((n_inner, LANES, LANES)),
            full((n_inner, LANES)), full((1, LANES)), full((2, 2, LANES, oc)), full((1, oc)),
        ],
        out_specs=[taps_spec, taps_spec],
        out_shape=[taps, taps],
        compiler_params=pltpu.CompilerParams(
            dimension_semantics=("arbitrary",), vmem_limit_bytes=VMEM_LIMIT),
        name="filter_mlp",
    )(zp, w1d.astype(BF16), b1d, wid.astype(BF16), bid, frd, wo4.astype(BF16), absdelta)


def _resident(shape):
    return pl.BlockSpec(shape, lambda *_: (0,) * len(shape), pipeline_mode=pl.Buffered(1))


def _pack_pairs(x):
    return pltpu.bitcast(x.astype(BF16), jnp.uint32)


def _unpack_pairs(w):
    return pltpu.bitcast(w, BF16)


def _spectrum_view(a_scr, chunk, ck1, pitch):
    start = pl.multiple_of(chunk * (ck1 * pitch), SUBLANES)
    return a_scr.at[pl.ds(start, ck1 * pitch)]


def _spectrum_pair(view, j, n2len, pitch):
    return jnp.concatenate([_unpack_pairs(view[jj * pitch:jj * pitch + n2len, :])
                            for jj in (j, j + 1)], axis=1)


def _filter_fft_kernel(hf_ref, hr_ref, g_ref, gr_ref, h2_ref, o_ref, a_scr,
                       *, ns, cn2, ck1, n1h, n2len, pitch):
    s = pl.program_id(1)
    ncb = a_scr.shape[0]

    @pl.when(s < ns)
    def _():
        for j in range(cn2):
            jr = cn2 - 1 - j
            m = s * cn2 + j
            g = jnp.concatenate([g_ref[m], gr_ref[n2len - 1 - m]], axis=1)
            x = jnp.concatenate(
                [jnp.concatenate([_load_minor(hf_ref.at[:, cb], j),
                                  _load_minor(hr_ref.at[:, cb], jr)], axis=0)
                 for cb in range(ncb)], axis=1)
            packed = _pack_pairs(jnp.dot(g, x.astype(BF16), preferred_element_type=F32))
            for cb in range(ncb):
                a_scr[cb, pl.ds(m, n1h, stride=pitch), :] = packed[:, cb * LANES:(cb + 1) * LANES]

    @pl.when(s >= ns)
    def _():
        for cb in range(ncb):
            view = _spectrum_view(a_scr.at[cb], s - ns, ck1, pitch)
            lanes = slice(cb * LANES, (cb + 1) * LANES)
            for j in range(0, ck1, 2):
                rows = _spectrum_pair(view, j, n2len, pitch)
                u = jnp.dot(h2_ref[...], rows, preferred_element_type=F32)
                u = (u * (1.0 / (n1h * n2len))).astype(o_ref.dtype)
                o_ref[j, :, lanes] = u[:, :LANES]
                o_ref[j + 1, :, lanes] = u[:, LANES:]


def _filter_spectrum(taps_f, taps_r, g_fwd, g_rev, h2, n2len):
    groups, ocb, _, _ = taps_f.shape
    oc = ocb * LANES
    n1h = groups * SUBLANES
    ns, ns2 = FILTER_STEPS
    cn2, ck1 = n2len // ns, n1h // ns2
    pitch = n2len + SUBLANES
    kern = functools.partial(_filter_fft_kernel, ns=ns, cn2=cn2, ck1=ck1, n1h=n1h,
                             n2len=n2len, pitch=pitch)
    ncb = FILTER_COL_BLOCKS
    return pl.pallas_call(
        kern,
        grid=(ocb // ncb, ns + ns2),
        in_specs=[
            pl.BlockSpec((groups, ncb, cn2 * SUBLANES, LANES),
                         lambda c, s: (0, c, jnp.minimum(s, ns - 1), 0)),
            pl.BlockSpec((groups, ncb, cn2 * SUBLANES, LANES),
                         lambda c, s: (0, c, jnp.maximum(ns - 1 - s, 0), 0)),
            _resident(g_fwd.shape), _resident(g_rev.shape), _resident(h2.shape),
        ],
        out_specs=pl.BlockSpec((ck1, 2 * n2len, ncb * LANES),
                               lambda c, s: (jnp.maximum(s - ns, 0), 0, c)),
        out_shape=jax.ShapeDtypeStruct((n1h, 2 * n2len, oc), BF16),
        scratch_shapes=[pltpu.VMEM((ncb, n1h * pitch, LANES), jnp.uint32)],
        compiler_params=pltpu.CompilerParams(
            dimension_semantics=("parallel", "arbitrary"), vmem_limit_bytes=CONV_VMEM_LIMIT),
        name="filter_spectrum",
    )(taps_f, taps_r, g_fwd, g_rev, h2)


def _short_conv_slab(x_ref, lo_ref, hi_ref, taps_ref, j, count, first_chunk, last_chunk):
    cur = _load_minor(x_ref, j)
    zero_row = jnp.zeros((1, cur.shape[1]), F32)
    if j > 0:
        prev = _load_minor(x_ref, j - 1)
    else:
        prev = _load_minor(lo_ref, 0)
        prev = jnp.where(first_chunk, jnp.concatenate([zero_row, prev[:-1]], axis=0), prev)
    if j < count - 1:
        nxt = _load_minor(x_ref, j + 1)
    else:
        nxt = _load_minor(hi_ref, 0)
        nxt = jnp.where(last_chunk, jnp.concatenate([nxt[1:], zero_row], axis=0), nxt)
    return (taps_ref[3:4, :] + taps_ref[0:1, :] * prev + taps_ref[1:2, :] * cur
            + taps_ref[2:3, :] * nxt)


def _long_conv_kernel(*refs, steps, n1h, n2len, pitch, conv_in, conv_mult):
    refs = list(refs)
    vf_ref = refs.pop(0)
    vlo_ref, vhi_ref, vtaps_ref = (refs.pop(0), refs.pop(0), refs.pop(0)) if conv_in else (None,) * 3
    g_ref, kf_ref, h2_ref, h2i_ref, xm_ref = (refs.pop(0) for _ in range(5))
    xlo_ref, xhi_ref, xtaps_ref = (refs.pop(0), refs.pop(0), refs.pop(0)) if conv_mult else (None,) * 3
    hb_ref, o_ref, a_scr, keep_scr = refs
    ns1, ns2, ns3 = steps
    cn2a, ck1, cn2 = n2len // ns1, n1h // ns2, n2len // ns3
    s = pl.program_id(2)

    @pl.when(s < ns1)
    def _():
        base = s * cn2a
        for j in range(cn2a):
            if conv_in:
                x = _short_conv_slab(vf_ref, vlo_ref, vhi_ref, vtaps_ref, j, cn2a,
                                     s == 0, s == ns1 - 1)
            else:
                x = _load_minor(vf_ref, j)
            keep_scr[base + j] = x
            r = jnp.dot(g_ref[base + j], x.astype(BF16), preferred_element_type=F32)
            a_scr[pl.ds(base + j, n1h, stride=pitch), :] = _pack_pairs(r)

    @pl.when((s >= ns1) & (s < ns1 + ns2))
    def _():
        view = _spectrum_view(a_scr, s - ns1, ck1, pitch)
        for j in range(0, ck1, 2):
            rows = _spectrum_pair(view, j, n2len, pitch)
            u = jnp.dot(h2_ref[...], rows, preferred_element_type=F32)
            ur, ui = u[:n2len], u[n2len:]
            kf = jnp.concatenate([kf_ref[j], kf_ref[j + 1]], axis=1).astype(F32)
            kr, ki = kf[:n2len], kf[n2len:]
            y = jnp.concatenate([ur * kr - ui * ki, ur * ki + ui * kr], axis=0)
            bb = _pack_pairs(jnp.dot(h2i_ref[...], y.astype(BF16), preferred_element_type=F32))
            for jj in range(2):
                view[(j + jj) * pitch:(j + jj) * pitch + n2len, :] = bb[
                    :, jj * LANES:(jj + 1) * LANES]

    @pl.when(s >= ns1 + ns2)
    def _():
        base = (s - ns1 - ns2) * cn2
        for j in range(cn2):
            rows = _unpack_pairs(a_scr[pl.ds(base + j, n1h, stride=pitch), :])
            y = lax.dot_general(g_ref[base + j], rows, (((0,), (0,)), ((), ())),
                                preferred_element_type=F32)
            if conv_mult:
                xm = _short_conv_slab(xm_ref, xlo_ref, xhi_ref, xtaps_ref, j, cn2,
                                      s == ns1 + ns2, s == ns1 + ns2 + ns3 - 1)
            else:
                xm = _load_minor(xm_ref, j)
            z = xm * (y + hb_ref[...] * keep_scr[base + j])
            o_ref[:, j * SUBLANES:(j + 1) * SUBLANES, :] = z.reshape(
                n1h // SUBLANES, SUBLANES, z.shape[1])


def _long_conv(u_perm, mult_perm, kf, hbias, order_idx, tables, in_taps=None, mult_taps=None):
    g_fwd, _, h2, h2i = tables
    bsz, groups, cbn, _, _ = u_perm.shape
    n2len = h2.shape[0] // 2
    n1h = groups * SUBLANES
    steps = CONV_STEPS
    ns1, ns2, ns3 = steps
    cn2a, cn2 = n2len // ns1, n2len // ns3
    pitch = n2len + SUBLANES

    def early(s):
        return jnp.minimum(s, ns1 - 1)

    def late(s):
        return jnp.clip(s - ns1 - ns2, 0, ns3 - 1)

    def chunk_specs(chunk_of, width):
        slab = (None, groups, None, SUBLANES, LANES)
        return [
            pl.BlockSpec((None, groups, None, width * SUBLANES, LANES),
                         lambda c, b, s: (b, 0, c, chunk_of(s), 0)),
            pl.BlockSpec(slab, lambda c, b, s: (b, 0, c, (chunk_of(s) * width - 1) % n2len, 0)),
            pl.BlockSpec(slab, lambda c, b, s: (b, 0, c, ((chunk_of(s) + 1) * width) % n2len, 0)),
        ]

    taps_spec = pl.BlockSpec((4, LANES), lambda c, b, s: (0, c))
    in_specs, args = [], []
    specs = chunk_specs(early, cn2a)
    if in_taps is None:
        in_specs += specs[:1]
        args += [u_perm]
    else:
        in_specs += specs + [taps_spec]
        args += [u_perm, u_perm, u_perm, in_taps]
    in_specs += [
        _resident(g_fwd.shape),
        pl.BlockSpec((n1h // ns2, 2 * n2len, LANES),
                     lambda c, b, s: (jnp.clip(s - ns1, 0, ns2 - 1), 0, order_idx * cbn + c)),
        _resident(h2.shape), _resident(h2i.shape),
    ]
    args += [g_fwd, kf, h2, h2i]
    specs = chunk_specs(late, cn2)
    if mult_taps is None:
        in_specs += specs[:1]
        args += [mult_perm]
    else:
        in_specs += specs + [taps_spec]
        args += [mult_perm, mult_perm, mult_perm, mult_taps]
    in_specs.append(pl.BlockSpec((None, 1, LANES), lambda c, b, s: (order_idx, 0, c)))
    args.append(hbias)
    out_spec = specs[0]
    out_shape = jax.ShapeDtypeStruct(u_perm.shape, F32)
    kern = functools.partial(_long_conv_kernel, steps=steps, n1h=n1h, n2len=n2len,
                             pitch=pitch, conv_in=in_taps is not None,
                             conv_mult=mult_taps is not None)
    return pl.pallas_call(
        kern,
        grid=(cbn, bsz, sum(steps)),
        in_specs=in_specs,
        out_specs=out_spec,
        out_shape=out_shape,
        scratch_shapes=[pltpu.VMEM((n1h * pitch, LANES), jnp.uint32),
                        pltpu.VMEM((n2len, n1h, LANES), F32)],
        compiler_params=pltpu.CompilerParams(
            dimension_semantics=("parallel", "parallel", "arbitrary"),
            vmem_limit_bytes=CONV_VMEM_LIMIT),
        name="long_conv",
    )(*args)


def _out_kernel(x_ref, yh_ref, yp_ref, mod_ref, w_ref, b_ref, g_ref, beta_ref, o_ref,
                *, n2len, alpha):
    cbn = yh_ref.shape[0]
    yh = jnp.concatenate(
        [jnp.concatenate([yh_ref[cb, pl.ds(j, n2len, stride=SUBLANES), :]
                          for cb in range(cbn)], axis=1)
         for j in range(SUBLANES)], axis=0)
    y = jnp.concatenate([yh.astype(BF16), yp_ref[...]], axis=1)
    acc = jnp.dot(y, w_ref[...], preferred_element_type=F32)
    h = alpha * x_ref[...] + mod_ref[2:3, :] * (acc + b_ref[...])
    mu = jnp.mean(h, axis=-1, keepdims=True)
    hc = h - mu
    var = jnp.mean(hc * hc, axis=-1, keepdims=True)
    o_ref[...] = hc * lax.rsqrt(var + LN_EPS) * g_ref[...] + beta_ref[...]


def _output_projection(x, yh_perm, yp, mod3, w_out, b_out, ln_g, ln_b, alpha):
    bsz, seq, d = x.shape
    _, _, cbn, tile, _ = yh_perm.shape
    ch = cbn * LANES
    n2len = tile // SUBLANES
    dp = yp.shape[2]
    full = lambda shape: pl.BlockSpec(shape, lambda b, i: (0,) * len(shape))
    kern = functools.partial(_out_kernel, n2len=n2len, alpha=alpha)
    return pl.pallas_call(
        kern,
        grid=(bsz, seq // tile),
        in_specs=[
            pl.BlockSpec((None, tile, d), lambda b, i: (b, i, 0)),
            pl.BlockSpec((None, None, cbn, tile, LANES), lambda b, i: (b, i, 0, 0, 0)),
            pl.BlockSpec((None, tile, dp), lambda b, i: (b, i, 0)),
            pl.BlockSpec((None, 3, d), lambda b, i: (b, 0, 0)),
            full((ch + dp, d)), full((1, d)), full((1, d)), full((1, d)),
        ],
        out_specs=pl.BlockSpec((None, tile, d), lambda b, i: (b, i, 0)),
        out_shape=jax.ShapeDtypeStruct(x.shape, x.dtype),
        compiler_params=pltpu.CompilerParams(
            dimension_semantics=("parallel", "arbitrary"), vmem_limit_bytes=VMEM_LIMIT),
        name="out_proj_deepnorm",
    )(x, yh_perm, yp, mod3, w_out.astype(BF16), b_out.reshape(1, d), ln_g.reshape(1, d),
      ln_b.reshape(1, d))


def _forward(x, c, w_ada, b_ada, w_in, b_in, conv_w, conv_b, filt_w1, filt_b1, filt_w_inner,
             filt_b_inner, filt_w_out, filt_freq, hyena_bias, pool_w, pool_scale, w_out,
             b_out, ln_g, ln_b, *, n2len):
    bsz, seq, d = x.shape
    depth = w_ada.shape[0]
    order, n_ch = hyena_bias.shape[1], hyena_bias.shape[2]
    alpha = (2.0 * depth) ** 0.25
    tables = _dft_tables(seq, n2len)
    h = x
    for layer in range(depth):
        mod3 = _modulation(c, w_ada[layer], b_ada[layer]).reshape(bsz, 3, d)
        v, x1, x2g, yp = _input_projection(
            h, mod3, w_in[layer], b_in[layer], conv_w[layer], conv_b[layer], pool_w[layer],
            pool_scale[layer], n2len)
        taps_f, taps_r = _filter_taps(
            seq, n2len, filt_w1[layer], filt_b1[layer], filt_w_inner[layer],
            filt_b_inner[layer], filt_w_out[layer], filt_freq[layer], n_ch, order)
        kf = _filter_spectrum(taps_f, taps_r, tables[0], tables[1], tables[2], n2len)
        hbias = hyena_bias[layer].reshape(order, 1, n_ch)
        taps = [jnp.concatenate([conv_w[layer][:, k * n_ch:(k + 1) * n_ch],
                                 conv_b[layer][None, k * n_ch:(k + 1) * n_ch]], axis=0)
                for k in range(2)]
        z = _long_conv(v, x1, kf, hbias, 0, tables, in_taps=taps[0], mult_taps=taps[1])
        for o in range(1, order):
            z = _long_conv(z, x2g, kf, hbias, o, tables)
        h = _output_projection(h, z, yp, mod3, w_out[layer], b_out[layer], ln_g[layer],
                               ln_b[layer], alpha)
    return h


def kernel(x, c, w_ada, b_ada, w_in, b_in, conv_w, conv_b, filt_w1, filt_b1, filt_w_inner,
           filt_b_inner, filt_w_out, filt_freq, hyena_bias, pool_w, pool_scale, w_out, b_out,
           ln_g, ln_b):
    return _forward(x, c, w_ada, b_ada, w_in, b_in, conv_w, conv_b, filt_w1, filt_b1,
                    filt_w_inner, filt_b_inner, filt_w_out, filt_freq, hyena_bias, pool_w,
                    pool_scale, w_out, b_out, ln_g, ln_b, n2len=MINOR_LEN)
```

```python
import functools
import math

import jax
import jax.numpy as jnp
import numpy as np
from jax import lax
from jax.experimental import pallas as pl
from jax.experimental.pallas import tpu as pltpu

F32 = jnp.float32
BF16 = jnp.bfloat16
HIGHEST = lax.Precision.HIGHEST

POOL_WINDOWS = (2, 4, 8, 16)
LN_EPS = 1e-5
DECAY_TARGET = 1e-2
FAST_DECAY_PCT = 0.3
SLOW_DECAY_PCT = 1.5

LANES = 128
SUBLANES = 8
HALO = 16
BAND_ROWS = 128
BAND_SPAN = 256
MINOR_LEN = 128
FILTER_STEPS = (4, 4)
FILTER_COL_BLOCKS = 2
CONV_STEPS = (2, 2, 2)
V7X_VMEM_BYTES = 64 * 1024 * 1024
VMEM_LIMIT = V7X_VMEM_BYTES * 7 // 8
CONV_VMEM_LIMIT = V7X_VMEM_BYTES * 15 // 16


def _silu(x):
    h = 0.5 * x
    return h + h * jnp.tanh(h)


@functools.lru_cache(maxsize=None)
def _host_dft_tables(seq, n2len):
    n1h = seq // n2len
    n_fft = 2 * seq
    k1 = np.arange(n1h, dtype=np.int64)
    odd = 2 * k1 + 1
    alpha = ((odd[:, None] * k1[None, :]) % (4 * n1h)) * (2.0 * math.pi / (4 * n1h))
    n2e = np.arange(n2len + 1, dtype=np.int64)
    beta = ((n2e[:, None] * odd[None, :]) % (2 * n_fft)) * (2.0 * math.pi / (2 * n_fft))
    n2i = np.arange(n2len, dtype=np.int64)
    phi = ((n2i[:, None] * n2i[None, :]) % n2len) * (2.0 * math.pi / n2len)
    cm, sm = np.cos(phi), np.sin(phi)
    h2 = np.block([[cm, sm], [-sm, cm]])
    h2 = h2.reshape(2 * n2len, 2, n2len).transpose(0, 2, 1).reshape(2 * n2len, 2 * n2len)
    h2i = np.block([[cm, -sm], [sm, cm]])
    h2i = h2i.reshape(2, n2len, 2 * n2len).transpose(1, 0, 2).reshape(2 * n2len, 2 * n2len)
    h2, h2i = h2.astype(BF16), h2i.astype(BF16)
    quarter = 0.5 * math.pi * np.tile(np.array([0.0, 1.0]), n1h)[:, None]
    alpha2 = np.repeat(alpha, 2, axis=0)
    beta2 = np.repeat(beta, 2, axis=1)
    f32 = lambda a: a.astype(np.float32)
    seeds = tuple(f32(t) for t in (
        np.cos(alpha2 + quarter), np.sin(alpha2 + quarter),
        np.cos(alpha2 - quarter), np.sin(alpha2 - quarter), np.cos(beta2), np.sin(beta2)))
    return seeds + (h2, h2i)


def _dft_tables(seq, n2len):
    caf, saf, car, sar, cb, sb, h2, h2i = _host_dft_tables(seq, n2len)
    cb, sb = jnp.asarray(cb)[:, :, None], jnp.asarray(sb)[:, :, None]
    g_fwd = (caf[None] * cb[:-1] - saf[None] * sb[:-1]).astype(BF16)
    g_rev = (car[None] * cb[1:] - sar[None] * sb[1:]).astype(BF16)
    return g_fwd, g_rev, jnp.asarray(h2), jnp.asarray(h2i)


@functools.lru_cache(maxsize=None)
def _packed_pos_features(seq, emb, tile):
    hl = LANES // 2
    half = tile // 2
    bands = (emb - 1) // 2
    row = np.arange((seq // tile) * half, dtype=np.int64)[:, None]
    lane = np.arange(LANES, dtype=np.int64)[None, :]
    pos = ((row // half) * tile + (lane // hl) * half + row % half).astype(np.float64)
    feat = np.broadcast_to(lane % hl, pos.shape)
    t = pos / (seq - 1)
    w = (2.0 * math.pi / seq) * pos
    f = np.linspace(1e-4, bands - 1, bands)[(feat - 1) % bands]
    z = np.where(feat == 0, t,
                 np.where(feat <= bands, np.cos(f * w),
                          np.where(feat <= 2 * bands, -np.sin(f * w), 0.0)))
    return z.astype(np.float32)


def _mod_kernel(c_ref, w_ref, b_ref, o_ref):
    s = _silu(c_ref[...])
    o_ref[...] = jnp.dot(s, w_ref[...], precision=HIGHEST,
                         preferred_element_type=F32) + b_ref[...]


def _modulation(c, w_ada, b_ada):
    bsz, d = c.shape
    n_out = w_ada.shape[1]
    return pl.pallas_call(
        _mod_kernel,
        grid=(n_out // d,),
        in_specs=[pl.BlockSpec((bsz, d), lambda j: (0, 0)),
                  pl.BlockSpec((d, d), lambda j: (0, j)),
                  pl.BlockSpec((1, d), lambda j: (0, j))],
        out_specs=pl.BlockSpec((bsz, d), lambda j: (0, j)),
        out_shape=jax.ShapeDtypeStruct((bsz, n_out), F32),
        name="adaln_mod",
    )(c, w_ada, b_ada.reshape(1, n_out))


def _store_permuted(o_ref, val, n2len, first=0):
    for cb in range(val.shape[1] // LANES):
        for j in range(val.shape[0] // n2len):
            o_ref[cb, pl.ds(first + j, n2len, stride=SUBLANES), :] = val[
                j * n2len:(j + 1) * n2len, cb * LANES:(cb + 1) * LANES]


def _load_minor(x_ref, j):
    blk = x_ref[:, j * SUBLANES:(j + 1) * SUBLANES, :]
    return blk.reshape(blk.shape[0] * SUBLANES, blk.shape[2])


def _band_window(b, tile):
    last = tile + 2 * HALO - BAND_SPAN
    start = min(b * BAND_ROWS, last)
    return start, 0 if start == b * BAND_ROWS else 1


@functools.lru_cache(maxsize=None)
def _band_matrices(tile):
    t = np.arange(BAND_ROWS)[:, None]
    k = np.arange(BAND_SPAN)[None, :]
    mats = []
    for win in POOL_WINDOWS:
        half = win // 2
        per_variant = []
        for b in (0, tile // BAND_ROWS - 1):
            start, _ = _band_window(b, tile)
            rel = k + start - HALO - (b * BAND_ROWS + t)
            per_variant.append(((rel >= -half) & (rel < half)).astype(BF16))
        mats.append(np.stack(per_variant))
    return np.stack(mats)


def _proj_kernel(xp_ref, xc_ref, xn_ref, mod_ref, w_ref, b_ref, cw_ref, cb_ref, pw_ref,
                 ps_ref, band_ref, v_ref, x1_ref, x2_ref, yp_ref, p_scr, q_scr,
                 *, tile, n2len, ch, seq):
    i = pl.program_id(1)
    nt = pl.num_programs(1)
    shift = mod_ref[0:1, :]
    scale1 = 1.0 + mod_ref[1:2, :]
    xe = jnp.concatenate([xp_ref[...], xc_ref[...], xn_ref[...]], axis=0)
    ue = (xe * scale1 + shift).astype(BF16)
    uc = ue[HALO:HALO + tile, :]

    def zero_outside_sequence(p):
        return jnp.concatenate([jnp.where(i > 0, p[:HALO], 0.0), p[HALO:HALO + tile],
                                jnp.where(i < nt - 1, p[HALO + tile:], 0.0)], axis=0)

    for k, o_ref in enumerate((v_ref, x1_ref)):
        cols = slice(k * ch, (k + 1) * ch)
        p = jnp.dot(uc, w_ref[:, cols], preferred_element_type=F32) + b_ref[:, cols]
        _store_permuted(o_ref, p, n2len)

    cols = slice(2 * ch, 3 * ch)
    p = jnp.dot(ue, w_ref[:, cols], preferred_element_type=F32) + b_ref[:, cols]
    p_scr[...] = zero_outside_sequence(p)
    hg = jnp.dot(uc, w_ref[:, 3 * ch:4 * ch], preferred_element_type=F32) + b_ref[:, 3 * ch:4 * ch]
    gate = _silu(hg)
    s = (cb_ref[:, cols]
         + cw_ref[0:1, cols] * p_scr[HALO - 1:HALO - 1 + tile, :]
         + cw_ref[1:2, cols] * p_scr[HALO:HALO + tile, :]
         + cw_ref[2:3, cols] * p_scr[HALO + 1:HALO + 1 + tile, :])
    _store_permuted(x2_ref, s * gate, n2len)

    dp = ps_ref.shape[1]
    c0 = 4 * ch
    pin = jnp.dot(ue, w_ref[:, c0:c0 + dp], preferred_element_type=F32) + b_ref[:, c0:c0 + dp]
    pin = zero_outside_sequence(pin)
    q_scr[...] = pin.astype(BF16)
    pin_c = pin[HALO:HALO + tile, :]
    pos = i * tile + lax.broadcasted_iota(jnp.int32, (tile, 1), 0)
    pg = dp // len(POOL_WINDOWS)
    groups = []
    for g, win in enumerate(POOL_WINDOWS):
        lanes = slice(g * pg, (g + 1) * pg)
        half = win // 2
        sums = []
        for b in range(tile // BAND_ROWS):
            start, variant = _band_window(b, tile)
            sums.append(jnp.dot(band_ref[g, variant], q_scr[start:start + BAND_SPAN, lanes],
                                preferred_element_type=F32))
        acc = jnp.concatenate(sums, axis=0)
        cnt = (jnp.minimum(pos + half, seq) - jnp.maximum(pos - half, 0)).astype(F32)
        diff = acc / cnt - pin_c[:, lanes]
        groups.append(jnp.dot(diff.astype(BF16), pw_ref[g], preferred_element_type=F32))
    pgate = jnp.dot(uc, w_ref[:, c0 + dp:c0 + 2 * dp], preferred_element_type=F32) + b_ref[:, c0 + dp:c0 + 2 * dp]
    yp = jnp.concatenate(groups, axis=1) * ps_ref[...] * _silu(pgate)
    yp_ref[...] = yp.astype(yp_ref.dtype)


def _input_projection(x, mod3, w_in, b_in, conv_w, conv_b, pool_w, pool_scale, n2len):
    bsz, seq, d = x.shape
    ch = conv_w.shape[1] // 3
    dp = pool_scale.shape[0]
    n1h = seq // n2len
    tile = SUBLANES * n2len
    nt = seq // tile
    hb = tile // HALO
    n_proj = w_in.shape[1]
    cbn = ch // LANES
    assert max(POOL_WINDOWS) // 2 <= HALO and dp == ch and tile % BAND_ROWS == 0
    band = jnp.asarray(_band_matrices(tile))
    perm = jax.ShapeDtypeStruct((bsz, n1h // SUBLANES, cbn, tile, LANES), F32)
    perm_spec = pl.BlockSpec((None, None, cbn, tile, LANES), lambda b, i: (b, i, 0, 0, 0))
    full = lambda shape: pl.BlockSpec(shape, lambda b, i: (0,) * len(shape))
    kern = functools.partial(_proj_kernel, tile=tile, n2len=n2len, ch=ch, seq=seq)
    return pl.pallas_call(
        kern,
        grid=(bsz, nt),
        in_specs=[
            pl.BlockSpec((None, HALO, d), lambda b, i: (b, jnp.maximum(i * hb - 1, 0), 0)),
            pl.BlockSpec((None, tile, d), lambda b, i: (b, i, 0)),
            pl.BlockSpec((None, HALO, d), lambda b, i: (b, jnp.minimum((i + 1) * hb, seq // HALO - 1), 0)),
            pl.BlockSpec((None, 3, d), lambda b, i: (b, 0, 0)),
            full((d, n_proj)), full((1, n_proj)), full((3, 3 * ch)), full((1, 3 * ch)),
            full(pool_w.shape), full((1, dp)), full(band.shape),
        ],
        out_specs=[perm_spec, perm_spec, perm_spec,
                   pl.BlockSpec((None, tile, dp), lambda b, i: (b, i, 0))],
        out_shape=[perm, perm, perm, jax.ShapeDtypeStruct((bsz, seq, dp), BF16)],
        scratch_shapes=[pltpu.VMEM((tile + 2 * HALO, ch), F32),
                        pltpu.VMEM((tile + 2 * HALO, dp), BF16)],
        compiler_params=pltpu.CompilerParams(
            dimension_semantics=("parallel", "arbitrary"), vmem_limit_bytes=VMEM_LIMIT),
        name="in_proj_conv_pool",
    )(x, x, x, mod3, w_in.astype(BF16), b_in.reshape(1, n_proj), conv_w,
      conv_b.reshape(1, 3 * ch), pool_w.astype(BF16), pool_scale.reshape(1, dp), band)


def _filter_mlp_kernel(z_ref, w1_ref, b1_ref, wi_ref, bi_ref, fr_ref, wo_ref, ad_ref,
                       of_ref, or_ref, *, n2len):
    z = z_ref[...]
    fr = fr_ref[...]
    h = jnp.sin(fr * (jnp.dot(z.astype(BF16), w1_ref[...],
                              preferred_element_type=F32) + b1_ref[...]))
    for l in range(wi_ref.shape[0]):
        h = jnp.sin(fr * (jnp.dot(h.astype(BF16), wi_ref[l],
                                  preferred_element_type=F32) + bi_ref[l:l + 1, :]))
    hb = h.astype(BF16)
    half_lanes = LANES // 2
    n1_half = z.shape[0] // n2len
    for half in range(2):
        t = z[:, half * half_lanes:half * half_lanes + 1]
        decay = jnp.exp(-t * ad_ref[...])
        for d, o_ref in enumerate((of_ref, or_ref)):
            k = jnp.dot(hb, wo_ref[d, half], preferred_element_type=F32)
            _store_permuted(o_ref, k * decay, n2len, first=half * n1_half)


def _block_diag2(w):
    zero = jnp.zeros_like(w)
    return jnp.concatenate([jnp.concatenate([w, zero], axis=-1),
                            jnp.concatenate([zero, w], axis=-1)], axis=-2)


def _filter_taps(seq, n2len, w1, b1, w_inner, b_inner, w_out, freq, n_ch, order):
    emb, hid = w1.shape
    n_inner = w_inner.shape[0]
    oc = order * n_ch
    hl = LANES // 2
    assert emb <= hl and hid <= hl
    tile = SUBLANES * n2len
    half = tile // 2
    groups = seq // tile
    zp = jnp.asarray(_packed_pos_features(seq, emb, tile))
    ph = hl - hid
    w1d = _block_diag2(jnp.pad(w1, ((0, hl - emb), (0, ph))))
    b1d = jnp.tile(jnp.pad(b1, (0, ph)), 2).reshape(1, LANES)
    wid = _block_diag2(jnp.pad(w_inner, ((0, 0), (0, ph), (0, ph))))
    bid = jnp.tile(jnp.pad(b_inner, ((0, 0), (0, ph))), (1, 2))
    frd = jnp.tile(jnp.pad(freq, (0, ph), constant_values=1.0), 2).reshape(1, LANES)
    wo = jnp.transpose(w_out.reshape(hid, order, 2, n_ch), (2, 0, 1, 3)).reshape(2, hid, oc)
    wo = jnp.pad(wo, ((0, 0), (0, ph), (0, 0)))
    zero = jnp.zeros_like(wo)
    wo4 = jnp.stack([jnp.concatenate([wo, zero], axis=1),
                     jnp.concatenate([zero, wo], axis=1)], axis=1)
    min_decay = math.log(DECAY_TARGET) / SLOW_DECAY_PCT
    max_decay = math.log(DECAY_TARGET) / FAST_DECAY_PCT
    absdelta = jnp.abs(jnp.linspace(min_decay, max_decay, n_ch, dtype=F32))
    absdelta = jnp.tile(absdelta, order).reshape(1, oc)

    full = lambda shape: pl.BlockSpec(shape, lambda i: (0,) * len(shape))
    kern = functools.partial(_filter_mlp_kernel, n2len=n2len)
    taps = jax.ShapeDtypeStruct((groups, oc // LANES, tile, LANES), F32)
    taps_spec = pl.BlockSpec((None, oc // LANES, tile, LANES), lambda i: (i, 0, 0, 0))
    return pl.pallas_call(
        kern,
        grid=(groups,),
        in_specs=[
            pl.BlockSpec((half, LANES), lambda i: (i, 0)),
            full((LANES, LANES)), full((1, LANES)), full((n_inner, LANES, LANES)),
            full((n_inner, LANES)), full((1, LANES)), full((2, 2, LANES, oc)), full((1, oc)),
        ],
        out_specs=[taps_spec, taps_spec],
        out_shape=[taps, taps],
        compiler_params=pltpu.CompilerParams(
            dimension_semantics=("arbitrary",), vmem_limit_bytes=VMEM_LIMIT),
        name="filter_mlp",
    )(zp, w1d.astype(BF16), b1d, wid.astype(BF16), bid, frd, wo4.astype(BF16), absdelta)


def _resident(shape):
    return pl.BlockSpec(shape, lambda *_: (0,) * len(shape), pipeline_mode=pl.Buffered(1))


def _pack_pairs(x):
    return pltpu.bitcast(x.astype(BF16), jnp.uint32)


def _unpack_pairs(w):
    return pltpu.bitcast(w, BF16)


def _spectrum_view(a_scr, chunk, ck1, pitch):
    start = pl.multiple_of(chunk * (ck1 * pitch), SUBLANES)
    return a_scr.at[pl.ds(start, ck1 * pitch)]


def _spectrum_pair(view, j, n2len, pitch):
    return jnp.concatenate([_unpack_pairs(view[jj * pitch:jj * pitch + n2len, :])
                            for jj in (j, j + 1)], axis=1)


def _filter_fft_kernel(hf_ref, hr_ref, g_ref, gr_ref, h2_ref, o_ref, a_scr,
                       *, ns, cn2, ck1, n1h, n2len, pitch):
    s = pl.program_id(1)
    ncb = a_scr.shape[0]

    @pl.when(s < ns)
    def _():
        for j in range(cn2):
            jr = cn2 - 1 - j
            m = s * cn2 + j
            g = jnp.concatenate([g_ref[m], gr_ref[n2len - 1 - m]], axis=1)
            x = jnp.concatenate(
                [jnp.concatenate([_load_minor(hf_ref.at[:, cb], j),
                                  _load_minor(hr_ref.at[:, cb], jr)], axis=0)
                 for cb in range(ncb)], axis=1)
            packed = _pack_pairs(jnp.dot(g, x.astype(BF16), preferred_element_type=F32))
            for cb in range(ncb):
                a_scr[cb, pl.ds(m, n1h, stride=pitch), :] = packed[:, cb * LANES:(cb + 1) * LANES]

    @pl.when(s >= ns)
    def _():
        for cb in range(ncb):
            view = _spectrum_view(a_scr.at[cb], s - ns, ck1, pitch)
            lanes = slice(cb * LANES, (cb + 1) * LANES)
            for j in range(0, ck1, 2):
                rows = _spectrum_pair(view, j, n2len, pitch)
                u = jnp.dot(h2_ref[...], rows, preferred_element_type=F32)
                u = (u * (1.0 / (n1h * n2len))).astype(o_ref.dtype)
                o_ref[j, :, lanes] = u[:, :LANES]
                o_ref[j + 1, :, lanes] = u[:, LANES:]


def _filter_spectrum(taps_f, taps_r, g_fwd, g_rev, h2, n2len):
    groups, ocb, _, _ = taps_f.shape
    oc = ocb * LANES
    n1h = groups * SUBLANES
    ns, ns2 = FILTER_STEPS
    cn2, ck1 = n2len // ns, n1h // ns2
    pitch = n2len + SUBLANES
    kern = functools.partial(_filter_fft_kernel, ns=ns, cn2=cn2, ck1=ck1, n1h=n1h,
                             n2len=n2len, pitch=pitch)
    ncb = FILTER_COL_BLOCKS
    return pl.pallas_call(
        kern,
        grid=(ocb // ncb, ns + ns2),
        in_specs=[
            pl.BlockSpec((groups, ncb, cn2 * SUBLANES, LANES),
                         lambda c, s: (0, c, jnp.minimum(s, ns - 1), 0)),
            pl.BlockSpec((groups, ncb, cn2 * SUBLANES, LANES),
                         lambda c, s: (0, c, jnp.maximum(ns - 1 - s, 0), 0)),
            _resident(g_fwd.shape), _resident(g_rev.shape), _resident(h2.shape),
        ],
        out_specs=pl.BlockSpec((ck1, 2 * n2len, ncb * LANES),
                               lambda c, s: (jnp.maximum(s - ns, 0), 0, c)),
        out_shape=jax.ShapeDtypeStruct((n1h, 2 * n2len, oc), BF16),
        scratch_shapes=[pltpu.VMEM((ncb, n1h * pitch, LANES), jnp.uint32)],
        compiler_params=pltpu.CompilerParams(
            dimension_semantics=("parallel", "arbitrary"), vmem_limit_bytes=CONV_VMEM_LIMIT),
        name="filter_spectrum",
    )(taps_f, taps_r, g_fwd, g_rev, h2)


def _short_conv_slab(x_ref, lo_ref, hi_ref, taps_ref, j, count, first_chunk, last_chunk):
    cur = _load_minor(x_ref, j)
    zero_row = jnp.zeros((1, cur.shape[1]), F32)
    if j > 0:
        prev = _load_minor(x_ref, j - 1)
    else:
        prev = _load_minor(lo_ref, 0)
        prev = jnp.where(first_chunk, jnp.concatenate([zero_row, prev[:-1]], axis=0), prev)
    if j < count - 1:
        nxt = _load_minor(x_ref, j + 1)
    else:
        nxt = _load_minor(hi_ref, 0)
        nxt = jnp.where(last_chunk, jnp.concatenate([nxt[1:], zero_row], axis=0), nxt)
    return (taps_ref[3:4, :] + taps_ref[0:1, :] * prev + taps_ref[1:2, :] * cur
            + taps_ref[2:3, :] * nxt)


def _long_conv_kernel(*refs, steps, n1h, n2len, pitch, conv_in, conv_mult):
    refs = list(refs)
    vf_ref = refs.pop(0)
    vlo_ref, vhi_ref, vtaps_ref = (refs.pop(0), refs.pop(0), refs.pop(0)) if conv_in else (None,) * 3
    g_ref, kf_ref, h2_ref, h2i_ref, xm_ref = (refs.pop(0) for _ in range(5))
    xlo_ref, xhi_ref, xtaps_ref = (refs.pop(0), refs.pop(0), refs.pop(0)) if conv_mult else (None,) * 3
    hb_ref, o_ref, a_scr, keep_scr = refs
    ns1, ns2, ns3 = steps
    cn2a, ck1, cn2 = n2len // ns1, n1h // ns2, n2len // ns3
    s = pl.program_id(2)

    @pl.when(s < ns1)
    def _():
        base = s * cn2a
        for j in range(cn2a):
            if conv_in:
                x = _short_conv_slab(vf_ref, vlo_ref, vhi_ref, vtaps_ref, j, cn2a,
                                     s == 0, s == ns1 - 1)
            else:
                x = _load_minor(vf_ref, j)
            keep_scr[base + j] = x
            r = jnp.dot(g_ref[base + j], x.astype(BF16), preferred_element_type=F32)
            a_scr[pl.ds(base + j, n1h, stride=pitch), :] = _pack_pairs(r)

    @pl.when((s >= ns1) & (s < ns1 + ns2))
    def _():
        view = _spectrum_view(a_scr, s - ns1, ck1, pitch)
        for j in range(0, ck1, 2):
            rows = _spectrum_pair(view, j, n2len, pitch)
            u = jnp.dot(h2_ref[...], rows, preferred_element_type=F32)
            ur, ui = u[:n2len], u[n2len:]
            kf = jnp.concatenate([kf_ref[j], kf_ref[j + 1]], axis=1).astype(F32)
            kr, ki = kf[:n2len], kf[n2len:]
            y = jnp.concatenate([ur * kr - ui * ki, ur * ki + ui * kr], axis=0)
            bb = _pack_pairs(jnp.dot(h2i_ref[...], y.astype(BF16), preferred_element_type=F32))
            for jj in range(2):
                view[(j + jj) * pitch:(j + jj) * pitch + n2len, :] = bb[
                    :, jj * LANES:(jj + 1) * LANES]

    @pl.when(s >= ns1 + ns2)
    def _():
        base = (s - ns1 - ns2) * cn2
        for j in range(cn2):
            rows = _unpack_pairs(a_scr[pl.ds(base + j, n1h, stride=pitch), :])
            y = lax.dot_general(g_ref[base + j], rows, (((0,), (0,)), ((), ())),
                                preferred_element_type=F32)
            if conv_mult:
                xm = _short_conv_slab(xm_ref, xlo_ref, xhi_ref, xtaps_ref, j, cn2,
                                      s == ns1 + ns2, s == ns1 + ns2 + ns3 - 1)
            else:
                xm = _load_minor(xm_ref, j)
            z = xm * (y + hb_ref[...] * keep_scr[base + j])
            o_ref[:, j * SUBLANES:(j + 1) * SUBLANES, :] = z.reshape(
                n1h // SUBLANES, SUBLANES, z.shape[1])


def _long_conv(u_perm, mult_perm, kf, hbias, order_idx, tables, in_taps=None, mult_taps=None):
    g_fwd, _, h2, h2i = tables
    bsz, groups, cbn, _, _ = u_perm.shape
    n2len = h2.shape[0] // 2
    n1h = groups * SUBLANES
    steps = CONV_STEPS
    ns1, ns2, ns3 = steps
    cn2a, cn2 = n2len // ns1, n2len // ns3
    pitch = n2len + SUBLANES

    def early(s):
        return jnp.minimum(s, ns1 - 1)

    def late(s):
        return jnp.clip(s - ns1 - ns2, 0, ns3 - 1)

    def chunk_specs(chunk_of, width):
        slab = (None, groups, None, SUBLANES, LANES)
        return [
            pl.BlockSpec((None, groups, None, width * SUBLANES, LANES),
                         lambda c, b, s: (b, 0, c, chunk_of(s), 0)),
            pl.BlockSpec(slab, lambda c, b, s: (b, 0, c, (chunk_of(s) * width - 1) % n2len, 0)),
            pl.BlockSpec(slab, lambda c, b, s: (b, 0, c, ((chunk_of(s) + 1) * width) % n2len, 0)),
        ]

    taps_spec = pl.BlockSpec((4, LANES), lambda c, b, s: (0, c))
    in_specs, args = [], []
    specs = chunk_specs(early, cn2a)
    if in_taps is None:
        in_specs += specs[:1]
        args += [u_perm]
    else:
        in_specs += specs + [taps_spec]
        args += [u_perm, u_perm, u_perm, in_taps]
    in_specs += [
        _resident(g_fwd.shape),
        pl.BlockSpec((n1h // ns2, 2 * n2len, LANES),
                     lambda c, b, s: (jnp.clip(s - ns1, 0, ns2 - 1), 0, order_idx * cbn + c)),
        _resident(h2.shape), _resident(h2i.shape),
    ]
    args += [g_fwd, kf, h2, h2i]
    specs = chunk_specs(late, cn2)
    if mult_taps is None:
        in_specs += specs[:1]
        args += [mult_perm]
    else:
        in_specs += specs + [taps_spec]
        args += [mult_perm, mult_perm, mult_perm, mult_taps]
    in_specs.append(pl.BlockSpec((None, 1, LANES), lambda c, b, s: (order_idx, 0, c)))
    args.append(hbias)
    out_spec = specs[0]
    out_shape = jax.ShapeDtypeStruct(u_perm.shape, F32)
    kern = functools.partial(_long_conv_kernel, steps=steps, n1h=n1h, n2len=n2len,
                             pitch=pitch, conv_in=in_taps is not None,
                             conv_mult=mult_taps is not None)
    return pl.pallas_call(
        kern,
        grid=(cbn, bsz, sum(steps)),
        in_specs=in_specs,
        out_specs=out_spec,
        out_shape=out_shape,
        scratch_shapes=[pltpu.VMEM((n1h * pitch, LANES), jnp.uint32),
                        pltpu.VMEM((n2len, n1h, LANES), F32)],
        compiler_params=pltpu.CompilerParams(
            dimension_semantics=("parallel", "parallel", "arbitrary"),
            vmem_limit_bytes=CONV_VMEM_LIMIT),
        name="long_conv",
    )(*args)


def _out_kernel(x_ref, yh_ref, yp_ref, mod_ref, w_ref, b_ref, g_ref, beta_ref, o_ref,
                *, n2len, alpha):
    cbn = yh_ref.shape[0]
    yh = jnp.concatenate(
        [jnp.concatenate([yh_ref[cb, pl.ds(j, n2len, stride=SUBLANES), :]
                          for cb in range(cbn)], axis=1)
         for j in range(SUBLANES)], axis=0)
    y = jnp.concatenate([yh.astype(BF16), yp_ref[...]], axis=1)
    acc = jnp.dot(y, w_ref[...], preferred_element_type=F32)
    h = alpha * x_ref[...] + mod_ref[2:3, :] * (acc + b_ref[...])
    mu = jnp.mean(h, axis=-1, keepdims=True)
    hc = h - mu
    var = jnp.mean(hc * hc, axis=-1, keepdims=True)
    o_ref[...] = hc * lax.rsqrt(var + LN_EPS) * g_ref[...] + beta_ref[...]


def _output_projection(x, yh_perm, yp, mod3, w_out, b_out, ln_g, ln_b, alpha):
    bsz, seq, d = x.shape
    _, _, cbn, tile, _ = yh_perm.shape
    ch = cbn * LANES
    n2len = tile // SUBLANES
    dp = yp.shape[2]
    full = lambda shape: pl.BlockSpec(shape, lambda b, i: (0,) * len(shape))
    kern = functools.partial(_out_kernel, n2len=n2len, alpha=alpha)
    return pl.pallas_call(
        kern,
        grid=(bsz, seq // tile),
        in_specs=[
            pl.BlockSpec((None, tile, d), lambda b, i: (b, i, 0)),
            pl.BlockSpec((None, None, cbn, tile, LANES), lambda b, i: (b, i, 0, 0, 0)),
            pl.BlockSpec((None, tile, dp), lambda b, i: (b, i, 0)),
            pl.BlockSpec((None, 3, d), lambda b, i: (b, 0, 0)),
            full((ch + dp, d)), full((1, d)), full((1, d)), full((1, d)),
        ],
        out_specs=pl.BlockSpec((None, tile, d), lambda b, i: (b, i, 0)),
        out_shape=jax.ShapeDtypeStruct(x.shape, x.dtype),
        compiler_params=pltpu.CompilerParams(
            dimension_semantics=("parallel", "arbitrary"), vmem_limit_bytes=VMEM_LIMIT),
        name="out_proj_deepnorm",
    )(x, yh_perm, yp, mod3, w_out.astype(BF16), b_out.reshape(1, d), ln_g.reshape(1, d),
      ln_b.reshape(1, d))


def _forward(x, c, w_ada, b_ada, w_in, b_in, conv_w, conv_b, filt_w1, filt_b1, filt_w_inner,
             filt_b_inner, filt_w_out, filt_freq, hyena_bias, pool_w, pool_scale, w_out,
             b_out, ln_g, ln_b, *, n2len):
    bsz, seq, d = x.shape
    depth = w_ada.shape[0]
    order, n_ch = hyena_bias.shape[1], hyena_bias.shape[2]
    alpha = (2.0 * depth) ** 0.25
    tables = _dft_tables(seq, n2len)
    h = x
    for layer in range(depth):
        mod3 = _modulation(c, w_ada[layer], b_ada[layer]).reshape(bsz, 3, d)
        v, x1, x2g, yp = _input_projection(
            h, mod3, w_in[layer], b_in[layer], conv_w[layer], conv_b[layer], pool_w[layer],
            pool_scale[layer], n2len)
        taps_f, taps_r = _filter_taps(
            seq, n2len, filt_w1[layer], filt_b1[layer], filt_w_inner[layer],
            filt_b_inner[layer], filt_w_out[layer], filt_freq[layer], n_ch, order)
        kf = _filter_spectrum(taps_f, taps_r, tables[0], tables[1], tables[2], n2len)
        hbias = hyena_bias[layer].reshape(order, 1, n_ch)
        taps = [jnp.concatenate([conv_w[layer][:, k * n_ch:(k + 1) * n_ch],
                                 conv_b[layer][None, k * n_ch:(k + 1) * n_ch]], axis=0)
                for k in range(2)]
        z = _long_conv(v, x1, kf, hbias, 0, tables, in_taps=taps[0], mult_taps=taps[1])
        for o in range(1, order):
            z = _long_conv(z, x2g, kf, hbias, o, tables)
        h = _output_projection(h, z, yp, mod3, w_out[layer], b_out[layer], ln_g[layer],
                               ln_b[layer], alpha)
    return h


def kernel(x, c, w_ada, b_ada, w_in, b_in, conv_w, conv_b, filt_w1, filt_b1, filt_w_inner,
           filt_b_inner, filt_w_out, filt_freq, hyena_bias, pool_w, pool_scale, w_out, b_out,
           ln_g, ln_b):
    return _forward(x, c, w_ada, b_ada, w_in, b_in, conv_w, conv_b, filt_w1, filt_b1,
                    filt_w_inner, filt_b_inner, filt_w_out, filt_freq, hyena_bias, pool_w,
                    pool_scale, w_out, b_out, ln_g, ln_b, n2len=MINOR_LEN)
```

```python
import functools
import math

import jax
import jax.numpy as jnp
import numpy as np
from jax import lax
from jax.experimental import pallas as pl
from jax.experimental.pallas import tpu as pltpu

F32 = jnp.float32
BF16 = jnp.bfloat16
HIGHEST = lax.Precision.HIGHEST

POOL_WINDOWS = (2, 4, 8, 16)
LN_EPS = 1e-5
DECAY_TARGET = 1e-2
FAST_DECAY_PCT = 0.3
SLOW_DECAY_PCT = 1.5

LANES = 128
SUBLANES = 8
HALO = 16
BAND_ROWS = 128
BAND_SPAN = 256
MINOR_LEN = 128
FILTER_STEPS = (4, 4)
FILTER_COL_BLOCKS = 2
CONV_STEPS = (2, 2, 2)
V7X_VMEM_BYTES = 64 * 1024 * 1024
VMEM_LIMIT = V7X_VMEM_BYTES * 7 // 8
CONV_VMEM_LIMIT = V7X_VMEM_BYTES * 15 // 16


def _silu(x):
    h = 0.5 * x
    return h + h * jnp.tanh(h)


@functools.lru_cache(maxsize=None)
def _host_dft_tables(seq, n2len):
    n1h = seq // n2len
    n_fft = 2 * seq
    k1 = np.arange(n1h, dtype=np.int64)
    odd = 2 * k1 + 1
    alpha = ((odd[:, None] * k1[None, :]) % (4 * n1h)) * (2.0 * math.pi / (4 * n1h))
    n2e = np.arange(n2len + 1, dtype=np.int64)
    beta = ((n2e[:, None] * odd[None, :]) % (2 * n_fft)) * (2.0 * math.pi / (2 * n_fft))
    n2i = np.arange(n2len, dtype=np.int64)
    phi = ((n2i[:, None] * n2i[None, :]) % n2len) * (2.0 * math.pi / n2len)
    cm, sm = np.cos(phi), np.sin(phi)
    h2 = np.block([[cm, sm], [-sm, cm]])
    h2 = h2.reshape(2 * n2len, 2, n2len).transpose(0, 2, 1).reshape(2 * n2len, 2 * n2len)
    h2i = np.block([[cm, -sm], [sm, cm]])
    h2i = h2i.reshape(2, n2len, 2 * n2len).transpose(1, 0, 2).reshape(2 * n2len, 2 * n2len)
    h2, h2i = h2.astype(BF16), h2i.astype(BF16)
    quarter = 0.5 * math.pi * np.tile(np.array([0.0, 1.0]), n1h)[:, None]
    alpha2 = np.repeat(alpha, 2, axis=0)
    beta2 = np.repeat(beta, 2, axis=1)
    f32 = lambda a: a.astype(np.float32)
    seeds = tuple(f32(t) for t in (
        np.cos(alpha2 + quarter), np.sin(alpha2 + quarter),
        np.cos(alpha2 - quarter), np.sin(alpha2 - quarter), np.cos(beta2), np.sin(beta2)))
    return seeds + (h2, h2i)


def _dft_tables(seq, n2len):
    caf, saf, car, sar, cb, sb, h2, h2i = _host_dft_tables(seq, n2len)
    cb, sb = jnp.asarray(cb)[:, None, :], jnp.asarray(sb)[:, None, :]
    g_fwd = (caf.T[None] * cb[:-1] - saf.T[None] * sb[:-1]).astype(BF16)
    g_rev = (car.T[None] * cb[1:] - sar.T[None] * sb[1:]).astype(BF16)
    return g_fwd, g_rev, jnp.asarray(h2), jnp.asarray(h2i)


@functools.lru_cache(maxsize=None)
def _packed_pos_features(seq, emb, tile):
    hl = LANES // 2
    half = tile // 2
    bands = (emb - 1) // 2
    row = np.arange((seq // tile) * half, dtype=np.int64)[:, None]
    lane = np.arange(LANES, dtype=np.int64)[None, :]
    pos = ((row // half) * tile + (lane // hl) * half + row % half).astype(np.float64)
    feat = np.broadcast_to(lane % hl, pos.shape)
    t = pos / (seq - 1)
    w = (2.0 * math.pi / seq) * pos
    f = np.linspace(1e-4, bands - 1, bands)[(feat - 1) % bands]
    z = np.where(feat == 0, t,
                 np.where(feat <= bands, np.cos(f * w),
                          np.where(feat <= 2 * bands, -np.sin(f * w), 0.0)))
    return z.astype(np.float32)


def _mod_kernel(c_ref, w_ref, b_ref, o_ref):
    s = _silu(c_ref[...])
    o_ref[...] = jnp.dot(s, w_ref[...], precision=HIGHEST,
                         preferred_element_type=F32) + b_ref[...]


def _modulation(c, w_ada, b_ada):
    bsz, d = c.shape
    n_out = w_ada.shape[1]
    return pl.pallas_call(
        _mod_kernel,
        grid=(n_out // d,),
        in_specs=[pl.BlockSpec((bsz, d), lambda j: (0, 0)),
                  pl.BlockSpec((d, d), lambda j: (0, j)),
                  pl.BlockSpec((1, d), lambda j: (0, j))],
        out_specs=pl.BlockSpec((bsz, d), lambda j: (0, j)),
        out_shape=jax.ShapeDtypeStruct((bsz, n_out), F32),
        name="adaln_mod",
    )(c, w_ada, b_ada.reshape(1, n_out))


def _store_permuted(o_ref, val, n2len, first=0):
    for cb in range(val.shape[1] // LANES):
        for j in range(val.shape[0] // n2len):
            o_ref[cb, pl.ds(first + j, n2len, stride=SUBLANES), :] = val[
                j * n2len:(j + 1) * n2len, cb * LANES:(cb + 1) * LANES]


def _load_minor(x_ref, j):
    blk = x_ref[:, j * SUBLANES:(j + 1) * SUBLANES, :]
    return blk.reshape(blk.shape[0] * SUBLANES, blk.shape[2])


def _band_window(b, tile):
    last = tile + 2 * HALO - BAND_SPAN
    start = min(b * BAND_ROWS, last)
    return start, 0 if start == b * BAND_ROWS else 1


@functools.lru_cache(maxsize=None)
def _band_matrices(tile):
    t = np.arange(BAND_ROWS)[:, None]
    k = np.arange(BAND_SPAN)[None, :]
    mats = []
    for win in POOL_WINDOWS:
        half = win // 2
        per_variant = []
        for b in (0, tile // BAND_ROWS - 1):
            start, _ = _band_window(b, tile)
            rel = k + start - HALO - (b * BAND_ROWS + t)
            per_variant.append(((rel >= -half) & (rel < half)).astype(BF16))
        mats.append(np.stack(per_variant))
    return np.stack(mats)


def _proj_kernel(xp_ref, xc_ref, xn_ref, mod_ref, w_ref, b_ref, cw_ref, cb_ref, pw_ref,
                 ps_ref, band_ref, v_ref, x1_ref, x2_ref, yp_ref, p_scr, q_scr,
                 *, tile, n2len, ch, seq):
    i = pl.program_id(1)
    nt = pl.num_programs(1)
    shift = mod_ref[0:1, :]
    scale1 = 1.0 + mod_ref[1:2, :]
    xe = jnp.concatenate([xp_ref[...], xc_ref[...], xn_ref[...]], axis=0)
    ue = (xe * scale1 + shift).astype(BF16)
    uc = ue[HALO:HALO + tile, :]

    def zero_outside_sequence(p):
        return jnp.concatenate([jnp.where(i > 0, p[:HALO], 0.0), p[HALO:HALO + tile],
                                jnp.where(i < nt - 1, p[HALO + tile:], 0.0)], axis=0)

    for k, o_ref in enumerate((v_ref, x1_ref)):
        cols = slice(k * ch, (k + 1) * ch)
        p = jnp.dot(uc, w_ref[:, cols], preferred_element_type=F32) + b_ref[:, cols]
        _store_permuted(o_ref, p, n2len)

    cols = slice(2 * ch, 3 * ch)
    p = jnp.dot(ue, w_ref[:, cols], preferred_element_type=F32) + b_ref[:, cols]
    p_scr[...] = zero_outside_sequence(p)
    hg = jnp.dot(uc, w_ref[:, 3 * ch:4 * ch], preferred_element_type=F32) + b_ref[:, 3 * ch:4 * ch]
    gate = _silu(hg)
    s = (cb_ref[:, cols]
         + cw_ref[0:1, cols] * p_scr[HALO - 1:HALO - 1 + tile, :]
         + cw_ref[1:2, cols] * p_scr[HALO:HALO + tile, :]
         + cw_ref[2:3, cols] * p_scr[HALO + 1:HALO + 1 + tile, :])
    _store_permuted(x2_ref, s * gate, n2len)

    dp = ps_ref.shape[1]
    c0 = 4 * ch
    pin = jnp.dot(ue, w_ref[:, c0:c0 + dp], preferred_element_type=F32) + b_ref[:, c0:c0 + dp]
    pin = zero_outside_sequence(pin)
    q_scr[...] = pin.astype(BF16)
    p_scr[...] = pin
    pos = i * tile + lax.broadcasted_iota(jnp.int32, (tile, 1), 0)
    pg = dp // len(POOL_WINDOWS)
    groups = []
    for g, win in enumerate(POOL_WINDOWS):
        lanes = slice(g * pg, (g + 1) * pg)
        half = win // 2
        sums = []
        for b in range(tile // BAND_ROWS):
            start, variant = _band_window(b, tile)
            sums.append(jnp.dot(band_ref[g, variant], q_scr[start:start + BAND_SPAN, lanes],
                                preferred_element_type=F32))
        acc = jnp.concatenate(sums, axis=0)
        cnt = (jnp.minimum(pos + half, seq) - jnp.maximum(pos - half, 0)).astype(F32)
        diff = acc / cnt - p_scr[HALO:HALO + tile, lanes]
        groups.append(jnp.dot(diff.astype(BF16), pw_ref[g], preferred_element_type=F32))
    pgate = jnp.dot(uc, w_ref[:, c0 + dp:c0 + 2 * dp], preferred_element_type=F32) + b_ref[:, c0 + dp:c0 + 2 * dp]
    yp = jnp.concatenate(groups, axis=1) * ps_ref[...] * _silu(pgate)
    yp_ref[...] = yp.astype(yp_ref.dtype)


def _input_projection(x, mod3, w_in, b_in, conv_w, conv_b, pool_w, pool_scale, n2len):
    bsz, seq, d = x.shape
    ch = conv_w.shape[1] // 3
    dp = pool_scale.shape[0]
    n1h = seq // n2len
    tile = SUBLANES * n2len
    nt = seq // tile
    hb = tile // HALO
    n_proj = w_in.shape[1]
    cbn = ch // LANES
    assert max(POOL_WINDOWS) // 2 <= HALO and dp == ch and tile % BAND_ROWS == 0
    band = jnp.asarray(_band_matrices(tile))
    perm = jax.ShapeDtypeStruct((bsz, n1h // SUBLANES, cbn, tile, LANES), F32)
    perm_spec = pl.BlockSpec((None, None, cbn, tile, LANES), lambda b, i: (b, i, 0, 0, 0))
    full = lambda shape: pl.BlockSpec(shape, lambda b, i: (0,) * len(shape))
    kern = functools.partial(_proj_kernel, tile=tile, n2len=n2len, ch=ch, seq=seq)
    return pl.pallas_call(
        kern,
        grid=(bsz, nt),
        in_specs=[
            pl.BlockSpec((None, HALO, d), lambda b, i: (b, jnp.maximum(i * hb - 1, 0), 0)),
            pl.BlockSpec((None, tile, d), lambda b, i: (b, i, 0)),
            pl.BlockSpec((None, HALO, d), lambda b, i: (b, jnp.minimum((i + 1) * hb, seq // HALO - 1), 0)),
            pl.BlockSpec((None, 3, d), lambda b, i: (b, 0, 0)),
            full((d, n_proj)), full((1, n_proj)), full((3, 3 * ch)), full((1, 3 * ch)),
            full(pool_w.shape), full((1, dp)), full(band.shape),
        ],
        out_specs=[perm_spec, perm_spec, perm_spec,
                   pl.BlockSpec((None, tile, dp), lambda b, i: (b, i, 0))],
        out_shape=[perm, perm, perm, jax.ShapeDtypeStruct((bsz, seq, dp), BF16)],
        scratch_shapes=[pltpu.VMEM((tile + 2 * HALO, ch), F32),
                        pltpu.VMEM((tile + 2 * HALO, dp), BF16)],
        compiler_params=pltpu.CompilerParams(
            dimension_semantics=("parallel", "arbitrary"), vmem_limit_bytes=VMEM_LIMIT),
        name="in_proj_conv_pool",
    )(x, x, x, mod3, w_in.astype(BF16), b_in.reshape(1, n_proj), conv_w,
      conv_b.reshape(1, 3 * ch), pool_w.astype(BF16), pool_scale.reshape(1, dp), band)


def _filter_mlp_kernel(z_ref, w1_ref, b1_ref, wi_ref, bi_ref, fr_ref, wo_ref, ad_ref,
                       of_ref, or_ref, *, n2len):
    z = z_ref[...]
    fr = fr_ref[...]
    h = jnp.sin(fr * (jnp.dot(z.astype(BF16), w1_ref[...],
                              preferred_element_type=F32) + b1_ref[...]))
    for l in range(wi_ref.shape[0]):
        h = jnp.sin(fr * (jnp.dot(h.astype(BF16), wi_ref[l],
                                  preferred_element_type=F32) + bi_ref[l:l + 1, :]))
    hb = h.astype(BF16)
    half_lanes = LANES // 2
    n1_half = z.shape[0] // n2len
    for half in range(2):
        t = z[:, half * half_lanes:half * half_lanes + 1]
        decay = jnp.exp(-t * ad_ref[...])
        for d, o_ref in enumerate((of_ref, or_ref)):
            k = jnp.dot(hb, wo_ref[d, half], preferred_element_type=F32)
            _store_permuted(o_ref, k * decay, n2len, first=half * n1_half)


def _block_diag2(w):
    zero = jnp.zeros_like(w)
    return jnp.concatenate([jnp.concatenate([w, zero], axis=-1),
                            jnp.concatenate([zero, w], axis=-1)], axis=-2)


def _filter_taps(seq, n2len, w1, b1, w_inner, b_inner, w_out, freq, n_ch, order):
    emb, hid = w1.shape
    n_inner = w_inner.shape[0]
    oc = order * n_ch
    hl = LANES // 2
    assert emb <= hl and hid <= hl
    tile = SUBLANES * n2len
    half = tile // 2
    groups = seq // tile
    zp = jnp.asarray(_packed_pos_features(seq, emb, tile))
    ph = hl - hid
    w1d = _block_diag2(jnp.pad(w1, ((0, hl - emb), (0, ph))))
    b1d = jnp.tile(jnp.pad(b1, (0, ph)), 2).reshape(1, LANES)
    wid = _block_diag2(jnp.pad(w_inner, ((0, 0), (0, ph), (0, ph))))
    bid = jnp.tile(jnp.pad(b_inner, ((0, 0), (0, ph))), (1, 2))
    frd = jnp.tile(jnp.pad(freq, (0, ph), constant_values=1.0), 2).reshape(1, LANES)
    wo = jnp.transpose(w_out.reshape(hid, order, 2, n_ch), (2, 0, 1, 3)).reshape(2, hid, oc)
    wo = jnp.pad(wo, ((0, 0), (0, ph), (0, 0)))
    zero = jnp.zeros_like(wo)
    wo4 = jnp.stack([jnp.concatenate([wo, zero], axis=1),
                     jnp.concatenate([zero, wo], axis=1)], axis=1)
    min_decay = math.log(DECAY_TARGET) / SLOW_DECAY_PCT
    max_decay = math.log(DECAY_TARGET) / FAST_DECAY_PCT
    absdelta = jnp.abs(jnp.linspace(min_decay, max_decay, n_ch, dtype=F32))
    absdelta = jnp.tile(absdelta, order).reshape(1, oc)

    full = lambda shape: pl.BlockSpec(shape, lambda i: (0,) * len(shape))
    kern = functools.partial(_filter_mlp_kernel, n2len=n2len)
    taps = jax.ShapeDtypeStruct((groups, oc // LANES, tile, LANES), F32)
    taps_spec = pl.BlockSpec((None, oc // LANES, tile, LANES), lambda i: (i, 0, 0, 0))
    return pl.pallas_call(
        kern,
        grid=(groups,),
        in_specs=[
            pl.BlockSpec((half, LANES), lambda i: (i, 0)),
            full((LANES, LANES)), full((1, LANES)), full((n_inner, LANES, LANES)),
            full((n_inner, LANES)), full((1, LANES)), full((2, 2, LANES, oc)), full((1, oc)),
        ],
        out_specs=[taps_spec, taps_spec],
        out_shape=[taps, taps],
        compiler_params=pltpu.CompilerParams(
            dimension_semantics=("arbitrary",), vmem_limit_bytes=VMEM_LIMIT),
        name="filter_mlp",
    )(zp, w1d.astype(BF16), b1d, wid.astype(BF16), bid, frd, wo4.astype(BF16), absdelta)


def _resident(shape):
    return pl.BlockSpec(shape, lambda *_: (0,) * len(shape), pipeline_mode=pl.Buffered(1))


def _dot_rows(a, b):
    return lax.dot_general(a, b, (((0,), (0,)), ((), ())), preferred_element_type=F32)


def _pack_pairs(x):
    return pltpu.bitcast(x.astype(BF16), jnp.uint32)


def _unpack_pairs(w):
    return pltpu.bitcast(w, BF16)


def _spectrum_view(a_scr, chunk, ck1, pitch):
    start = pl.multiple_of(chunk * (ck1 * pitch), SUBLANES)
    return a_scr.at[pl.ds(start, ck1 * pitch)]


def _spectrum_pair(view, j, n2len, pitch):
    return jnp.concatenate([_unpack_pairs(view[jj * pitch:jj * pitch + n2len, :])
                            for jj in (j, j + 1)], axis=1)


def _filter_fft_kernel(hf_ref, hr_ref, g_ref, gr_ref, h2_ref, o_ref, a_scr,
                       *, ns, cn2, ck1, n1h, n2len, pitch):
    s = pl.program_id(1)
    ncb = a_scr.shape[0]

    @pl.when(s < ns)
    def _():
        for j in range(cn2):
            jr = cn2 - 1 - j
            m = s * cn2 + j
            g = jnp.concatenate([g_ref[m], gr_ref[n2len - 1 - m]], axis=0)
            x = jnp.concatenate(
                [jnp.concatenate([_load_minor(hf_ref.at[:, cb], j),
                                  _load_minor(hr_ref.at[:, cb], jr)], axis=0)
                 for cb in range(ncb)], axis=1)
            packed = _pack_pairs(_dot_rows(g, x.astype(BF16)))
            for cb in range(ncb):
                a_scr[cb, pl.ds(m, n1h, stride=pitch), :] = packed[:, cb * LANES:(cb + 1) * LANES]

    @pl.when(s >= ns)
    def _():
        for cb in range(ncb):
            view = _spectrum_view(a_scr.at[cb], s - ns, ck1, pitch)
            lanes = slice(cb * LANES, (cb + 1) * LANES)
            for j in range(0, ck1, 2):
                rows = _spectrum_pair(view, j, n2len, pitch)
                u = jnp.dot(h2_ref[...], rows, preferred_element_type=F32)
                u = (u * (1.0 / (n1h * n2len))).astype(o_ref.dtype)
                o_ref[j, :, lanes] = u[:, :LANES]
                o_ref[j + 1, :, lanes] = u[:, LANES:]


def _filter_spectrum(taps_f, taps_r, g_fwd, g_rev, h2, n2len):
    groups, ocb, _, _ = taps_f.shape
    oc = ocb * LANES
    n1h = groups * SUBLANES
    ns, ns2 = FILTER_STEPS
    cn2, ck1 = n2len // ns, n1h // ns2
    pitch = n2len + SUBLANES
    kern = functools.partial(_filter_fft_kernel, ns=ns, cn2=cn2, ck1=ck1, n1h=n1h,
                             n2len=n2len, pitch=pitch)
    ncb = FILTER_COL_BLOCKS
    return pl.pallas_call(
        kern,
        grid=(ocb // ncb, ns + ns2),
        in_specs=[
            pl.BlockSpec((groups, ncb, cn2 * SUBLANES, LANES),
                         lambda c, s: (0, c, jnp.minimum(s, ns - 1), 0)),
            pl.BlockSpec((groups, ncb, cn2 * SUBLANES, LANES),
                         lambda c, s: (0, c, jnp.maximum(ns - 1 - s, 0), 0)),
            _resident(g_fwd.shape), _resident(g_rev.shape), _resident(h2.shape),
        ],
        out_specs=pl.BlockSpec((ck1, 2 * n2len, ncb * LANES),
                               lambda c, s: (jnp.maximum(s - ns, 0), 0, c)),
        out_shape=jax.ShapeDtypeStruct((n1h, 2 * n2len, oc), BF16),
        scratch_shapes=[pltpu.VMEM((ncb, n1h * pitch, LANES), jnp.uint32)],
        compiler_params=pltpu.CompilerParams(
            dimension_semantics=("parallel", "arbitrary"), vmem_limit_bytes=CONV_VMEM_LIMIT),
        name="filter_spectrum",
    )(taps_f, taps_r, g_fwd, g_rev, h2)


def _short_conv_slab(x_ref, lo_ref, hi_ref, taps_ref, j, count, first_chunk, last_chunk):
    cur = _load_minor(x_ref, j)
    zero_row = jnp.zeros((1, cur.shape[1]), F32)
    if j > 0:
        prev = _load_minor(x_ref, j - 1)
    else:
        prev = _load_minor(lo_ref, 0)
        prev = jnp.where(first_chunk, jnp.concatenate([zero_row, prev[:-1]], axis=0), prev)
    if j < count - 1:
        nxt = _load_minor(x_ref, j + 1)
    else:
        nxt = _load_minor(hi_ref, 0)
        nxt = jnp.where(last_chunk, jnp.concatenate([nxt[1:], zero_row], axis=0), nxt)
    return (taps_ref[3:4, :] + taps_ref[0:1, :] * prev + taps_ref[1:2, :] * cur
            + taps_ref[2:3, :] * nxt)


def _long_conv_kernel(*refs, steps, n1h, n2len, pitch, conv_in, conv_mult):
    refs = list(refs)
    vf_ref = refs.pop(0)
    vlo_ref, vhi_ref, vtaps_ref = (refs.pop(0), refs.pop(0), refs.pop(0)) if conv_in else (None,) * 3
    g_ref, kf_ref, h2_ref, h2i_ref, xm_ref = (refs.pop(0) for _ in range(5))
    xlo_ref, xhi_ref, xtaps_ref = (refs.pop(0), refs.pop(0), refs.pop(0)) if conv_mult else (None,) * 3
    hb_ref, o_ref, a_scr, keep_scr = refs
    ns1, ns2, ns3 = steps
    cn2a, ck1, cn2 = n2len // ns1, n1h // ns2, n2len // ns3
    s = pl.program_id(2)

    @pl.when(s < ns1)
    def _():
        base = s * cn2a
        for j in range(cn2a):
            if conv_in:
                x = _short_conv_slab(vf_ref, vlo_ref, vhi_ref, vtaps_ref, j, cn2a,
                                     s == 0, s == ns1 - 1)
            else:
                x = _load_minor(vf_ref, j)
            keep_scr[base + j] = x
            r = _dot_rows(g_ref[base + j], x.astype(BF16))
            a_scr[pl.ds(base + j, n1h, stride=pitch), :] = _pack_pairs(r)

    @pl.when((s >= ns1) & (s < ns1 + ns2))
    def _():
        view = _spectrum_view(a_scr, s - ns1, ck1, pitch)
        for j in range(0, ck1, 2):
            rows = _spectrum_pair(view, j, n2len, pitch)
            u = jnp.dot(h2_ref[...], rows, preferred_element_type=F32)
            ur, ui = u[:n2len], u[n2len:]
            kf = jnp.concatenate([kf_ref[j], kf_ref[j + 1]], axis=1).astype(F32)
            kr, ki = kf[:n2len], kf[n2len:]
            y = jnp.concatenate([ur * kr - ui * ki, ur * ki + ui * kr], axis=0)
            bb = _pack_pairs(jnp.dot(h2i_ref[...], y.astype(BF16), preferred_element_type=F32))
            for jj in range(2):
                view[(j + jj) * pitch:(j + jj) * pitch + n2len, :] = bb[
                    :, jj * LANES:(jj + 1) * LANES]

    @pl.when(s >= ns1 + ns2)
    def _():
        base = (s - ns1 - ns2) * cn2
        for j in range(cn2):
            rows = _unpack_pairs(a_scr[pl.ds(base + j, n1h, stride=pitch), :])
            y = jnp.dot(g_ref[base + j], rows, preferred_element_type=F32)
            if conv_mult:
                xm = _short_conv_slab(xm_ref, xlo_ref, xhi_ref, xtaps_ref, j, cn2,
                                      s == ns1 + ns2, s == ns1 + ns2 + ns3 - 1)
            else:
                xm = _load_minor(xm_ref, j)
            z = xm * (y + hb_ref[...] * keep_scr[base + j])
            o_ref[:, j * SUBLANES:(j + 1) * SUBLANES, :] = z.reshape(
                n1h // SUBLANES, SUBLANES, z.shape[1])


def _long_conv(u_perm, mult_perm, kf, hbias, order_idx, tables, in_taps=None, mult_taps=None):
    g_fwd, _, h2, h2i = tables
    bsz, groups, cbn, _, _ = u_perm.shape
    n2len = h2.shape[0] // 2
    n1h = groups * SUBLANES
    steps = CONV_STEPS
    ns1, ns2, ns3 = steps
    cn2a, cn2 = n2len // ns1, n2len // ns3
    pitch = n2len + SUBLANES

    def early(s):
        return jnp.minimum(s, ns1 - 1)

    def late(s):
        return jnp.clip(s - ns1 - ns2, 0, ns3 - 1)

    def chunk_specs(chunk_of, width):
        slab = (None, groups, None, SUBLANES, LANES)
        return [
            pl.BlockSpec((None, groups, None, width * SUBLANES, LANES),
                         lambda c, b, s: (b, 0, c, chunk_of(s), 0)),
            pl.BlockSpec(slab, lambda c, b, s: (b, 0, c, (chunk_of(s) * width - 1) % n2len, 0)),
            pl.BlockSpec(slab, lambda c, b, s: (b, 0, c, ((chunk_of(s) + 1) * width) % n2len, 0)),
        ]

    taps_spec = pl.BlockSpec((4, LANES), lambda c, b, s: (0, c))
    in_specs, args = [], []
    specs = chunk_specs(early, cn2a)
    if in_taps is None:
        in_specs += specs[:1]
        args += [u_perm]
    else:
        in_specs += specs + [taps_spec]
        args += [u_perm, u_perm, u_perm, in_taps]
    in_specs += [
        _resident(g_fwd.shape),
        pl.BlockSpec((n1h // ns2, 2 * n2len, LANES),
                     lambda c, b, s: (jnp.clip(s - ns1, 0, ns2 - 1), 0, order_idx * cbn + c)),
        _resident(h2.shape), _resident(h2i.shape),
    ]
    args += [g_fwd, kf, h2, h2i]
    specs = chunk_specs(late, cn2)
    if mult_taps is None:
        in_specs += specs[:1]
        args += [mult_perm]
    else:
        in_specs += specs + [taps_spec]
        args += [mult_perm, mult_perm, mult_perm, mult_taps]
    in_specs.append(pl.BlockSpec((None, 1, LANES), lambda c, b, s: (order_idx, 0, c)))
    args.append(hbias)
    out_spec = specs[0]
    out_shape = jax.ShapeDtypeStruct(u_perm.shape, F32)
    kern = functools.partial(_long_conv_kernel, steps=steps, n1h=n1h, n2len=n2len,
                             pitch=pitch, conv_in=in_taps is not None,
                             conv_mult=mult_taps is not None)
    return pl.pallas_call(
        kern,
        grid=(cbn, bsz, sum(steps)),
        in_specs=in_specs,
        out_specs=out_spec,
        out_shape=out_shape,
        scratch_shapes=[pltpu.VMEM((n1h * pitch, LANES), jnp.uint32),
                        pltpu.VMEM((n2len, n1h, LANES), F32)],
        compiler_params=pltpu.CompilerParams(
            dimension_semantics=("parallel", "parallel", "arbitrary"),
            vmem_limit_bytes=CONV_VMEM_LIMIT),
        name="long_conv",
    )(*args)


def _out_kernel(x_ref, yh_ref, yp_ref, mod_ref, w_ref, b_ref, g_ref, beta_ref, o_ref,
                *, n2len, alpha):
    cbn = yh_ref.shape[0]
    yh = jnp.concatenate(
        [jnp.concatenate([yh_ref[cb, pl.ds(j, n2len, stride=SUBLANES), :]
                          for cb in range(cbn)], axis=1)
         for j in range(SUBLANES)], axis=0)
    y = jnp.concatenate([yh.astype(BF16), yp_ref[...]], axis=1)
    acc = jnp.dot(y, w_ref[...], preferred_element_type=F32)
    h = alpha * x_ref[...] + mod_ref[2:3, :] * (acc + b_ref[...])
    mu = jnp.mean(h, axis=-1, keepdims=True)
    hc = h - mu
    var = jnp.mean(hc * hc, axis=-1, keepdims=True)
    o_ref[...] = hc * lax.rsqrt(var + LN_EPS) * g_ref[...] + beta_ref[...]


def _output_projection(x, yh_perm, yp, mod3, w_out, b_out, ln_g, ln_b, alpha):
    bsz, seq, d = x.shape
    _, _, cbn, tile, _ = yh_perm.shape
    ch = cbn * LANES
    n2len = tile // SUBLANES
    dp = yp.shape[2]
    full = lambda shape: pl.BlockSpec(shape, lambda b, i: (0,) * len(shape))
    kern = functools.partial(_out_kernel, n2len=n2len, alpha=alpha)
    return pl.pallas_call(
        kern,
        grid=(bsz, seq // tile),
        in_specs=[
            pl.BlockSpec((None, tile, d), lambda b, i: (b, i, 0)),
            pl.BlockSpec((None, None, cbn, tile, LANES), lambda b, i: (b, i, 0, 0, 0)),
            pl.BlockSpec((None, tile, dp), lambda b, i: (b, i, 0)),
            pl.BlockSpec((None, 3, d), lambda b, i: (b, 0, 0)),
            full((ch + dp, d)), full((1, d)), full((1, d)), full((1, d)),
        ],
        out_specs=pl.BlockSpec((None, tile, d), lambda b, i: (b, i, 0)),
        out_shape=jax.ShapeDtypeStruct(x.shape, x.dtype),
        compiler_params=pltpu.CompilerParams(
            dimension_semantics=("parallel", "arbitrary"), vmem_limit_bytes=VMEM_LIMIT),
        name="out_proj_deepnorm",
    )(x, yh_perm, yp, mod3, w_out.astype(BF16), b_out.reshape(1, d), ln_g.reshape(1, d),
      ln_b.reshape(1, d))


def _forward(x, c, w_ada, b_ada, w_in, b_in, conv_w, conv_b, filt_w1, filt_b1, filt_w_inner,
             filt_b_inner, filt_w_out, filt_freq, hyena_bias, pool_w, pool_scale, w_out,
             b_out, ln_g, ln_b, *, n2len):
    bsz, seq, d = x.shape
    depth = w_ada.shape[0]
    order, n_ch = hyena_bias.shape[1], hyena_bias.shape[2]
    alpha = (2.0 * depth) ** 0.25
    tables = _dft_tables(seq, n2len)
    h = x
    for layer in range(depth):
        mod3 = _modulation(c, w_ada[layer], b_ada[layer]).reshape(bsz, 3, d)
        v, x1, x2g, yp = _input_projection(
            h, mod3, w_in[layer], b_in[layer], conv_w[layer], conv_b[layer], pool_w[layer],
            pool_scale[layer], n2len)
        taps_f, taps_r = _filter_taps(
            seq, n2len, filt_w1[layer], filt_b1[layer], filt_w_inner[layer],
            filt_b_inner[layer], filt_w_out[layer], filt_freq[layer], n_ch, order)
        kf = _filter_spectrum(taps_f, taps_r, tables[0], tables[1], tables[2], n2len)
        hbias = hyena_bias[layer].reshape(order, 1, n_ch)
        taps = [jnp.concatenate([conv_w[layer][:, k * n_ch:(k + 1) * n_ch],
                                 conv_b[layer][None, k * n_ch:(k + 1) * n_ch]], axis=0)
                for k in range(2)]
        z = _long_conv(v, x1, kf, hbias, 0, tables, in_taps=taps[0], mult_taps=taps[1])
        for o in range(1, order):
            z = _long_conv(z, x2g, kf, hbias, o, tables)
        h = _output_projection(h, z, yp, mod3, w_out[layer], b_out[layer], ln_g[layer],
                               ln_b[layer], alpha)
    return h


def kernel(x, c, w_ada, b_ada, w_in, b_in, conv_w, conv_b, filt_w1, filt_b1, filt_w_inner,
           filt_b_inner, filt_w_out, filt_freq, hyena_bias, pool_w, pool_scale, w_out, b_out,
           ln_g, ln_b):
    return _forward(x, c, w_ada, b_ada, w_in, b_in, conv_w, conv_b, filt_w1, filt_b1,
                    filt_w_inner, filt_b_inner, filt_w_out, filt_freq, hyena_bias, pool_w,
                    pool_scale, w_out, b_out, ln_g, ln_b, n2len=MINOR_LEN)
```

```python
import functools
import math

import jax
import jax.numpy as jnp
import numpy as np
from jax import lax
from jax.experimental import pallas as pl
from jax.experimental.pallas import tpu as pltpu

F32 = jnp.float32
BF16 = jnp.bfloat16
HIGHEST = lax.Precision.HIGHEST

POOL_WINDOWS = (2, 4, 8, 16)
LN_EPS = 1e-5
DECAY_TARGET = 1e-2
FAST_DECAY_PCT = 0.3
SLOW_DECAY_PCT = 1.5

LANES = 128
SUBLANES = 8
HALO = 16
BAND_ROWS = 128
BAND_SPAN = 256
MINOR_LEN = 128
FILTER_STEPS = (4, 4)
FILTER_COL_BLOCKS = 2
CONV_STEPS = (2, 2, 2)
V7X_VMEM_BYTES = 64 * 1024 * 1024
VMEM_LIMIT = V7X_VMEM_BYTES * 7 // 8
CONV_VMEM_LIMIT = V7X_VMEM_BYTES * 15 // 16


def _silu(x):
    h = 0.5 * x
    return h + h * jnp.tanh(h)


@functools.lru_cache(maxsize=None)
def _host_dft_tables(seq, n2len):
    n1h = seq // n2len
    n_fft = 2 * seq
    k1 = np.arange(n1h, dtype=np.int64)
    odd = 2 * k1 + 1
    alpha = ((odd[:, None] * k1[None, :]) % (4 * n1h)) * (2.0 * math.pi / (4 * n1h))
    n2e = np.arange(n2len + 1, dtype=np.int64)
    beta = ((n2e[:, None] * odd[None, :]) % (2 * n_fft)) * (2.0 * math.pi / (2 * n_fft))
    n2i = np.arange(n2len, dtype=np.int64)
    phi = ((n2i[:, None] * n2i[None, :]) % n2len) * (2.0 * math.pi / n2len)
    cm, sm = np.cos(phi), np.sin(phi)
    h2 = np.block([[cm, sm], [-sm, cm]])
    h2 = h2.reshape(2 * n2len, 2, n2len).transpose(0, 2, 1).reshape(2 * n2len, 2 * n2len)
    h2i = np.block([[cm, -sm], [sm, cm]])
    h2i = h2i.reshape(2, n2len, 2 * n2len).transpose(1, 0, 2).reshape(2 * n2len, 2 * n2len)
    h2, h2i = h2.astype(BF16), h2i.astype(BF16)
    quarter = 0.5 * math.pi * np.tile(np.array([0.0, 1.0]), n1h)[:, None]
    alpha2 = np.repeat(alpha, 2, axis=0)
    beta2 = np.repeat(beta, 2, axis=1)
    f32 = lambda a: a.astype(np.float32)
    seeds = tuple(f32(t) for t in (
        np.cos(alpha2 + quarter), np.sin(alpha2 + quarter),
        np.cos(alpha2 - quarter), np.sin(alpha2 - quarter), np.cos(beta2), np.sin(beta2)))
    return seeds + (h2, h2i)


def _dft_tables(seq, n2len):
    caf, saf, car, sar, cb, sb, h2, h2i = _host_dft_tables(seq, n2len)
    cb, sb = jnp.asarray(cb)[:, None, :], jnp.asarray(sb)[:, None, :]
    g_fwd = (caf.T[None] * cb[:-1] - saf.T[None] * sb[:-1]).astype(BF16)
    g_rev = (car.T[None] * cb[1:] - sar.T[None] * sb[1:]).astype(BF16)
    return g_fwd, g_rev, jnp.asarray(h2), jnp.asarray(h2i)


@functools.lru_cache(maxsize=None)
def _packed_pos_features(seq, emb, tile):
    hl = LANES // 2
    half = tile // 2
    bands = (emb - 1) // 2
    row = np.arange((seq // tile) * half, dtype=np.int64)[:, None]
    lane = np.arange(LANES, dtype=np.int64)[None, :]
    pos = ((row // half) * tile + (lane // hl) * half + row % half).astype(np.float64)
    feat = np.broadcast_to(lane % hl, pos.shape)
    t = pos / (seq - 1)
    w = (2.0 * math.pi / seq) * pos
    f = np.linspace(1e-4, bands - 1, bands)[(feat - 1) % bands]
    z = np.where(feat == 0, t,
                 np.where(feat <= bands, np.cos(f * w),
                          np.where(feat <= 2 * bands, -np.sin(f * w), 0.0)))
    return z.astype(np.float32)


def _mod_kernel(c_ref, w_ref, b_ref, o_ref):
    s = _silu(c_ref[...])
    o_ref[...] = jnp.dot(s, w_ref[...], precision=HIGHEST,
                         preferred_element_type=F32) + b_ref[...]


def _modulation(c, w_ada, b_ada):
    bsz, d = c.shape
    n_out = w_ada.shape[1]
    return pl.pallas_call(
        _mod_kernel,
        grid=(n_out // d,),
        in_specs=[pl.BlockSpec((bsz, d), lambda j: (0, 0)),
                  pl.BlockSpec((d, d), lambda j: (0, j)),
                  pl.BlockSpec((1, d), lambda j: (0, j))],
        out_specs=pl.BlockSpec((bsz, d), lambda j: (0, j)),
        out_shape=jax.ShapeDtypeStruct((bsz, n_out), F32),
        name="adaln_mod",
    )(c, w_ada, b_ada.reshape(1, n_out))


def _store_permuted(o_ref, val, n2len, first=0):
    for cb in range(val.shape[1] // LANES):
        for j in range(val.shape[0] // n2len):
            o_ref[cb, pl.ds(first + j, n2len, stride=SUBLANES), :] = val[
                j * n2len:(j + 1) * n2len, cb * LANES:(cb + 1) * LANES]


def _load_minor(x_ref, j):
    blk = x_ref[:, j * SUBLANES:(j + 1) * SUBLANES, :]
    return blk.reshape(blk.shape[0] * SUBLANES, blk.shape[2])


def _band_window(b, tile):
    last = tile + 2 * HALO - BAND_SPAN
    start = min(b * BAND_ROWS, last)
    return start, 0 if start == b * BAND_ROWS else 1


@functools.lru_cache(maxsize=None)
def _band_matrices(tile):
    t = np.arange(BAND_ROWS)[:, None]
    k = np.arange(BAND_SPAN)[None, :]
    mats = []
    for win in POOL_WINDOWS:
        half = win // 2
        per_variant = []
        for b in (0, tile // BAND_ROWS - 1):
            start, _ = _band_window(b, tile)
            rel = k + start - HALO - (b * BAND_ROWS + t)
            per_variant.append(((rel >= -half) & (rel < half)).astype(BF16))
        mats.append(np.stack(per_variant))
    return np.stack(mats)


def _proj_kernel(xp_ref, xc_ref, xn_ref, mod_ref, w_ref, b_ref, cw_ref, cb_ref, pw_ref,
                 ps_ref, band_ref, v_ref, x1_ref, x2_ref, yp_ref, p_scr, q_scr,
                 *, tile, n2len, ch, seq):
    i = pl.program_id(1)
    nt = pl.num_programs(1)
    shift = mod_ref[0:1, :]
    scale1 = 1.0 + mod_ref[1:2, :]
    xe = jnp.concatenate([xp_ref[...], xc_ref[...], xn_ref[...]], axis=0)
    ue = (xe * scale1 + shift).astype(BF16)
    uc = ue[HALO:HALO + tile, :]

    def zero_outside_sequence(p):
        return jnp.concatenate([jnp.where(i > 0, p[:HALO], 0.0), p[HALO:HALO + tile],
                                jnp.where(i < nt - 1, p[HALO + tile:], 0.0)], axis=0)

    for k, o_ref in enumerate((v_ref, x1_ref)):
        cols = slice(k * ch, (k + 1) * ch)
        p = jnp.dot(uc, w_ref[:, cols], preferred_element_type=F32) + b_ref[:, cols]
        _store_permuted(o_ref, p, n2len)

    cols = slice(2 * ch, 3 * ch)
    p = jnp.dot(ue, w_ref[:, cols], preferred_element_type=F32) + b_ref[:, cols]
    p_scr[...] = zero_outside_sequence(p)
    hg = jnp.dot(uc, w_ref[:, 3 * ch:4 * ch], preferred_element_type=F32) + b_ref[:, 3 * ch:4 * ch]
    gate = _silu(hg)
    s = (cb_ref[:, cols]
         + cw_ref[0:1, cols] * p_scr[HALO - 1:HALO - 1 + tile, :]
         + cw_ref[1:2, cols] * p_scr[HALO:HALO + tile, :]
         + cw_ref[2:3, cols] * p_scr[HALO + 1:HALO + 1 + tile, :])
    _store_permuted(x2_ref, s * gate, n2len)

    dp = ps_ref.shape[1]
    c0 = 4 * ch
    pin = jnp.dot(ue, w_ref[:, c0:c0 + dp], preferred_element_type=F32) + b_ref[:, c0:c0 + dp]
    pin = zero_outside_sequence(pin)
    q_scr[...] = pin.astype(BF16)
    p_scr[...] = pin
    pos = i * tile + lax.broadcasted_iota(jnp.int32, (tile, 1), 0)
    pg = dp // len(POOL_WINDOWS)
    nblk = tile // BAND_ROWS
    diffs = []
    for g, win in enumerate(POOL_WINDOWS):
        lanes = slice(g * pg, (g + 1) * pg)
        half = win // 2
        sums = [None] * nblk
        b = 0
        while b < nblk:
            variant = _band_window(b, tile)[1]
            blocks = [b]
            if b + 1 < nblk and _band_window(b + 1, tile)[1] == variant:
                blocks.append(b + 1)
            starts = [_band_window(bb, tile)[0] for bb in blocks]
            rhs = jnp.concatenate([q_scr[st:st + BAND_SPAN, lanes] for st in starts], axis=1)
            out = jnp.dot(band_ref[g, variant], rhs, preferred_element_type=F32)
            for n, bb in enumerate(blocks):
                sums[bb] = out[:, n * pg:(n + 1) * pg]
            b += len(blocks)
        acc = jnp.concatenate(sums, axis=0)
        cnt = (jnp.minimum(pos + half, seq) - jnp.maximum(pos - half, 0)).astype(F32)
        diffs.append((acc / cnt - p_scr[HALO:HALO + tile, lanes]).astype(BF16))
    maps = [jnp.dot(jnp.concatenate(diffs[g:g + 2], axis=1), pw_ref[g // 2],
                    preferred_element_type=F32) for g in range(0, len(diffs), 2)]
    pgate = jnp.dot(uc, w_ref[:, c0 + dp:c0 + 2 * dp], preferred_element_type=F32) + b_ref[:, c0 + dp:c0 + 2 * dp]
    yp = jnp.concatenate(maps, axis=1) * ps_ref[...] * _silu(pgate)
    yp_ref[...] = yp.astype(yp_ref.dtype)


def _input_projection(x, mod3, w_in, b_in, conv_w, conv_b, pool_w, pool_scale, n2len):
    bsz, seq, d = x.shape
    ch = conv_w.shape[1] // 3
    dp = pool_scale.shape[0]
    n1h = seq // n2len
    tile = SUBLANES * n2len
    nt = seq // tile
    hb = tile // HALO
    n_proj = w_in.shape[1]
    cbn = ch // LANES
    assert max(POOL_WINDOWS) // 2 <= HALO and dp == ch and tile % BAND_ROWS == 0
    band = jnp.asarray(_band_matrices(tile))
    assert pool_w.shape[0] % 2 == 0
    pool_pairs = jnp.stack([_block_diag(pool_w[g], pool_w[g + 1])
                            for g in range(0, pool_w.shape[0], 2)]).astype(BF16)
    perm = jax.ShapeDtypeStruct((bsz, n1h // SUBLANES, cbn, tile, LANES), F32)
    perm_spec = pl.BlockSpec((None, None, cbn, tile, LANES), lambda b, i: (b, i, 0, 0, 0))
    full = lambda shape: pl.BlockSpec(shape, lambda b, i: (0,) * len(shape))
    kern = functools.partial(_proj_kernel, tile=tile, n2len=n2len, ch=ch, seq=seq)
    return pl.pallas_call(
        kern,
        grid=(bsz, nt),
        in_specs=[
            pl.BlockSpec((None, HALO, d), lambda b, i: (b, jnp.maximum(i * hb - 1, 0), 0)),
            pl.BlockSpec((None, tile, d), lambda b, i: (b, i, 0)),
            pl.BlockSpec((None, HALO, d), lambda b, i: (b, jnp.minimum((i + 1) * hb, seq // HALO - 1), 0)),
            pl.BlockSpec((None, 3, d), lambda b, i: (b, 0, 0)),
            full((d, n_proj)), full((1, n_proj)), full((3, 3 * ch)), full((1, 3 * ch)),
            full(pool_pairs.shape), full((1, dp)), full(band.shape),
        ],
        out_specs=[perm_spec, perm_spec, perm_spec,
                   pl.BlockSpec((None, tile, dp), lambda b, i: (b, i, 0))],
        out_shape=[perm, perm, perm, jax.ShapeDtypeStruct((bsz, seq, dp), BF16)],
        scratch_shapes=[pltpu.VMEM((tile + 2 * HALO, ch), F32),
                        pltpu.VMEM((tile + 2 * HALO, dp), BF16)],
        compiler_params=pltpu.CompilerParams(
            dimension_semantics=("parallel", "arbitrary"), vmem_limit_bytes=VMEM_LIMIT),
        name="in_proj_conv_pool",
    )(x, x, x, mod3, w_in.astype(BF16), b_in.reshape(1, n_proj), conv_w,
      conv_b.reshape(1, 3 * ch), pool_pairs, pool_scale.reshape(1, dp), band)


def _filter_mlp_kernel(z_ref, w1_ref, b1_ref, wi_ref, bi_ref, fr_ref, wo_ref, ad_ref,
                       of_ref, or_ref, *, n2len):
    z = z_ref[...]
    fr = fr_ref[...]
    h = jnp.sin(fr * (jnp.dot(z.astype(BF16), w1_ref[...],
                              preferred_element_type=F32) + b1_ref[...]))
    for l in range(wi_ref.shape[0]):
        h = jnp.sin(fr * (jnp.dot(h.astype(BF16), wi_ref[l],
                                  preferred_element_type=F32) + bi_ref[l:l + 1, :]))
    hb = h.astype(BF16)
    half_lanes = LANES // 2
    n1_half = z.shape[0] // n2len
    for half in range(2):
        t = z[:, half * half_lanes:half * half_lanes + 1]
        decay = jnp.exp(-t * ad_ref[...])
        for d, o_ref in enumerate((of_ref, or_ref)):
            k = jnp.dot(hb, wo_ref[d, half], preferred_element_type=F32)
            _store_permuted(o_ref, k * decay, n2len, first=half * n1_half)


def _block_diag(a, b):
    zero = jnp.zeros_like(a)
    return jnp.concatenate([jnp.concatenate([a, zero], axis=-1),
                            jnp.concatenate([zero, b], axis=-1)], axis=-2)


def _block_diag2(w):
    return _block_diag(w, w)


def _filter_taps(seq, n2len, w1, b1, w_inner, b_inner, w_out, freq, n_ch, order):
    emb, hid = w1.shape
    n_inner = w_inner.shape[0]
    oc = order * n_ch
    hl = LANES // 2
    assert emb <= hl and hid <= hl
    tile = SUBLANES * n2len
    half = tile // 2
    groups = seq // tile
    zp = jnp.asarray(_packed_pos_features(seq, emb, tile))
    ph = hl - hid
    w1d = _block_diag2(jnp.pad(w1, ((0, hl - emb), (0, ph))))
    b1d = jnp.tile(jnp.pad(b1, (0, ph)), 2).reshape(1, LANES)
    wid = _block_diag2(jnp.pad(w_inner, ((0, 0), (0, ph), (0, ph))))
    bid = jnp.tile(jnp.pad(b_inner, ((0, 0), (0, ph))), (1, 2))
    frd = jnp.tile(jnp.pad(freq, (0, ph), constant_values=1.0), 2).reshape(1, LANES)
    wo = jnp.transpose(w_out.reshape(hid, order, 2, n_ch), (2, 0, 1, 3)).reshape(2, hid, oc)
    wo = jnp.pad(wo, ((0, 0), (0, ph), (0, 0)))
    zero = jnp.zeros_like(wo)
    wo4 = jnp.stack([jnp.concatenate([wo, zero], axis=1),
                     jnp.concatenate([zero, wo], axis=1)], axis=1)
    min_decay = math.log(DECAY_TARGET) / SLOW_DECAY_PCT
    max_decay = math.log(DECAY_TARGET) / FAST_DECAY_PCT
    absdelta = jnp.abs(jnp.linspace(min_decay, max_decay, n_ch, dtype=F32))
    absdelta = jnp.tile(absdelta, order).reshape(1, oc)

    full = lambda shape: pl.BlockSpec(shape, lambda i: (0,) * len(shape))
    kern = functools.partial(_filter_mlp_kernel, n2len=n2len)
    taps = jax.ShapeDtypeStruct((groups, oc // LANES, tile, LANES), F32)
    taps_spec = pl.BlockSpec((None, oc // LANES, tile, LANES), lambda i: (i, 0, 0, 0))
    return pl.pallas_call(
        kern,
        grid=(groups,),
        in_specs=[
            pl.BlockSpec((half, LANES), lambda i: (i, 0)),
            full((LANES, LANES)), full((1, LANES)), full((n_inner, LANES, LANES)),
            full((n_inner, LANES)), full((1, LANES)), full((2, 2, LANES, oc)), full((1, oc)),
        ],
        out_specs=[taps_spec, taps_spec],
        out_shape=[taps, taps],
        compiler_params=pltpu.CompilerParams(
            dimension_semantics=("arbitrary",), vmem_limit_bytes=VMEM_LIMIT),
        name="filter_mlp",
    )(zp, w1d.astype(BF16), b1d, wid.astype(BF16), bid, frd, wo4.astype(BF16), absdelta)


def _resident(shape):
    return pl.BlockSpec(shape, lambda *_: (0,) * len(shape), pipeline_mode=pl.Buffered(1))


def _dot_rows(a, b):
    return lax.dot_general(a, b, (((0,), (0,)), ((), ())), preferred_element_type=F32)


def _pack_pairs(x):
    return pltpu.bitcast(x.astype(BF16), jnp.uint32)


def _unpack_pairs(w):
    return pltpu.bitcast(w, BF16)


def _spectrum_view(a_scr, chunk, ck1, pitch):
    start = pl.multiple_of(chunk * (ck1 * pitch), SUBLANES)
    return a_scr.at[pl.ds(start, ck1 * pitch)]


def _spectrum_pair(view, j, n2len, pitch):
    return jnp.concatenate([_unpack_pairs(view[jj * pitch:jj * pitch + n2len, :])
                            for jj in (j, j + 1)], axis=1)


def _filter_fft_kernel(hf_ref, hr_ref, g_ref, gr_ref, h2_ref, o_ref, a_scr,
                       *, ns, cn2, ck1, n1h, n2len, pitch):
    s = pl.program_id(1)
    ncb = a_scr.shape[0]

    @pl.when(s < ns)
    def _():
        for j in range(cn2):
            jr = cn2 - 1 - j
            m = s * cn2 + j
            g = jnp.concatenate([g_ref[m], gr_ref[n2len - 1 - m]], axis=0)
            x = jnp.concatenate(
                [jnp.concatenate([_load_minor(hf_ref.at[:, cb], j),
                                  _load_minor(hr_ref.at[:, cb], jr)], axis=0)
                 for cb in range(ncb)], axis=1)
            packed = _pack_pairs(_dot_rows(g, x.astype(BF16)))
            for cb in range(ncb):
                a_scr[cb, pl.ds(m, n1h, stride=pitch), :] = packed[:, cb * LANES:(cb + 1) * LANES]

    @pl.when(s >= ns)
    def _():
        for cb in range(ncb):
            view = _spectrum_view(a_scr.at[cb], s - ns, ck1, pitch)
            lanes = slice(cb * LANES, (cb + 1) * LANES)
            for j in range(0, ck1, 2):
                rows = _spectrum_pair(view, j, n2len, pitch)
                u = jnp.dot(h2_ref[...], rows, preferred_element_type=F32)
                u = (u * (1.0 / (n1h * n2len))).astype(o_ref.dtype)
                o_ref[j, :, lanes] = u[:, :LANES]
                o_ref[j + 1, :, lanes] = u[:, LANES:]


def _filter_spectrum(taps_f, taps_r, g_fwd, g_rev, h2, n2len):
    groups, ocb, _, _ = taps_f.shape
    oc = ocb * LANES
    n1h = groups * SUBLANES
    ns, ns2 = FILTER_STEPS
    cn2, ck1 = n2len // ns, n1h // ns2
    pitch = n2len + SUBLANES
    kern = functools.partial(_filter_fft_kernel, ns=ns, cn2=cn2, ck1=ck1, n1h=n1h,
                             n2len=n2len, pitch=pitch)
    ncb = FILTER_COL_BLOCKS
    return pl.pallas_call(
        kern,
        grid=(ocb // ncb, ns + ns2),
        in_specs=[
            pl.BlockSpec((groups, ncb, cn2 * SUBLANES, LANES),
                         lambda c, s: (0, c, jnp.minimum(s, ns - 1), 0)),
            pl.BlockSpec((groups, ncb, cn2 * SUBLANES, LANES),
                         lambda c, s: (0, c, jnp.maximum(ns - 1 - s, 0), 0)),
            _resident(g_fwd.shape), _resident(g_rev.shape), _resident(h2.shape),
        ],
        out_specs=pl.BlockSpec((ck1, 2 * n2len, ncb * LANES),
                               lambda c, s: (jnp.maximum(s - ns, 0), 0, c)),
        out_shape=jax.ShapeDtypeStruct((n1h, 2 * n2len, oc), BF16),
        scratch_shapes=[pltpu.VMEM((ncb, n1h * pitch, LANES), jnp.uint32)],
        compiler_params=pltpu.CompilerParams(
            dimension_semantics=("parallel", "arbitrary"), vmem_limit_bytes=CONV_VMEM_LIMIT),
        name="filter_spectrum",
    )(taps_f, taps_r, g_fwd, g_rev, h2)


def _short_conv_slab(x_ref, lo_ref, hi_ref, taps_ref, j, count, first_chunk, last_chunk):
    cur = _load_minor(x_ref, j)
    zero_row = jnp.zeros((1, cur.shape[1]), F32)
    if j > 0:
        prev = _load_minor(x_ref, j - 1)
    else:
        prev = _load_minor(lo_ref, 0)
        prev = jnp.where(first_chunk, jnp.concatenate([zero_row, prev[:-1]], axis=0), prev)
    if j < count - 1:
        nxt = _load_minor(x_ref, j + 1)
    else:
        nxt = _load_minor(hi_ref, 0)
        nxt = jnp.where(last_chunk, jnp.concatenate([nxt[1:], zero_row], axis=0), nxt)
    return (taps_ref[3:4, :] + taps_ref[0:1, :] * prev + taps_ref[1:2, :] * cur
            + taps_ref[2:3, :] * nxt)


def _long_conv_kernel(*refs, steps, n1h, n2len, pitch, conv_in, conv_mult):
    refs = list(refs)
    vf_ref = refs.pop(0)
    vlo_ref, vhi_ref, vtaps_ref = (refs.pop(0), refs.pop(0), refs.pop(0)) if conv_in else (None,) * 3
    g_ref, kf_ref, h2_ref, h2i_ref, xm_ref = (refs.pop(0) for _ in range(5))
    xlo_ref, xhi_ref, xtaps_ref = (refs.pop(0), refs.pop(0), refs.pop(0)) if conv_mult else (None,) * 3
    hb_ref, o_ref, a_scr, keep_scr = refs
    ns1, ns2, ns3 = steps
    cn2a, ck1, cn2 = n2len // ns1, n1h // ns2, n2len // ns3
    s = pl.program_id(2)

    @pl.when(s < ns1)
    def _():
        base = s * cn2a
        for j in range(cn2a):
            if conv_in:
                x = _short_conv_slab(vf_ref, vlo_ref, vhi_ref, vtaps_ref, j, cn2a,
                                     s == 0, s == ns1 - 1)
            else:
                x = _load_minor(vf_ref, j)
            keep_scr[base + j] = x
            r = _dot_rows(g_ref[base + j], x.astype(BF16))
            a_scr[pl.ds(base + j, n1h, stride=pitch), :] = _pack_pairs(r)

    @pl.when((s >= ns1) & (s < ns1 + ns2))
    def _():
        view = _spectrum_view(a_scr, s - ns1, ck1, pitch)
        for j in range(0, ck1, 2):
            rows = _spectrum_pair(view, j, n2len, pitch)
            u = jnp.dot(h2_ref[...], rows, preferred_element_type=F32)
            ur, ui = u[:n2len], u[n2len:]
            kf = jnp.concatenate([kf_ref[j], kf_ref[j + 1]], axis=1).astype(F32)
            kr, ki = kf[:n2len], kf[n2len:]
            y = jnp.concatenate([ur * kr - ui * ki, ur * ki + ui * kr], axis=0)
            bb = _pack_pairs(jnp.dot(h2i_ref[...], y.astype(BF16), preferred_element_type=F32))
            for jj in range(2):
                view[(j + jj) * pitch:(j + jj) * pitch + n2len, :] = bb[
                    :, jj * LANES:(jj + 1) * LANES]

    @pl.when(s >= ns1 + ns2)
    def _():
        base = (s - ns1 - ns2) * cn2
        for j in range(cn2):
            rows = _unpack_pairs(a_scr[pl.ds(base + j, n1h, stride=pitch), :])
            y = jnp.dot(g_ref[base + j], rows, preferred_element_type=F32)
            if conv_mult:
                xm = _short_conv_slab(xm_ref, xlo_ref, xhi_ref, xtaps_ref, j, cn2,
                                      s == ns1 + ns2, s == ns1 + ns2 + ns3 - 1)
            else:
                xm = _load_minor(xm_ref, j)
            z = xm * (y + hb_ref[...] * keep_scr[base + j])
            o_ref[:, j * SUBLANES:(j + 1) * SUBLANES, :] = z.reshape(
                n1h // SUBLANES, SUBLANES, z.shape[1])


def _long_conv(u_perm, mult_perm, kf, hbias, order_idx, tables, in_taps=None, mult_taps=None):
    g_fwd, _, h2, h2i = tables
    bsz, groups, cbn, _, _ = u_perm.shape
    n2len = h2.shape[0] // 2
    n1h = groups * SUBLANES
    steps = CONV_STEPS
    ns1, ns2, ns3 = steps
    cn2a, cn2 = n2len // ns1, n2len // ns3
    pitch = n2len + SUBLANES

    def early(s):
        return jnp.minimum(s, ns1 - 1)

    def late(s):
        return jnp.clip(s - ns1 - ns2, 0, ns3 - 1)

    def chunk_specs(chunk_of, width):
        slab = (None, groups, None, SUBLANES, LANES)
        return [
            pl.BlockSpec((None, groups, None, width * SUBLANES, LANES),
                         lambda c, b, s: (b, 0, c, chunk_of(s), 0)),
            pl.BlockSpec(slab, lambda c, b, s: (b, 0, c, (chunk_of(s) * width - 1) % n2len, 0)),
            pl.BlockSpec(slab, lambda c, b, s: (b, 0, c, ((chunk_of(s) + 1) * width) % n2len, 0)),
        ]

    taps_spec = pl.BlockSpec((4, LANES), lambda c, b, s: (0, c))
    in_specs, args = [], []
    specs = chunk_specs(early, cn2a)
    if in_taps is None:
        in_specs += specs[:1]
        args += [u_perm]
    else:
        in_specs += specs + [taps_spec]
        args += [u_perm, u_perm, u_perm, in_taps]
    in_specs += [
        _resident(g_fwd.shape),
        pl.BlockSpec((n1h // ns2, 2 * n2len, LANES),
                     lambda c, b, s: (jnp.clip(s - ns1, 0, ns2 - 1), 0, order_idx * cbn + c)),
        _resident(h2.shape), _resident(h2i.shape),
    ]
    args += [g_fwd, kf, h2, h2i]
    specs = chunk_specs(late, cn2)
    if mult_taps is None:
        in_specs += specs[:1]
        args += [mult_perm]
    else:
        in_specs += specs + [taps_spec]
        args += [mult_perm, mult_perm, mult_perm, mult_taps]
    in_specs.append(pl.BlockSpec((None, 1, LANES), lambda c, b, s: (order_idx, 0, c)))
    args.append(hbias)
    out_spec = specs[0]
    out_shape = jax.ShapeDtypeStruct(u_perm.shape, F32)
    kern = functools.partial(_long_conv_kernel, steps=steps, n1h=n1h, n2len=n2len,
                             pitch=pitch, conv_in=in_taps is not None,
                             conv_mult=mult_taps is not None)
    return pl.pallas_call(
        kern,
        grid=(cbn, bsz, sum(steps)),
        in_specs=in_specs,
        out_specs=out_spec,
        out_shape=out_shape,
        scratch_shapes=[pltpu.VMEM((n1h * pitch, LANES), jnp.uint32),
                        pltpu.VMEM((n2len, n1h, LANES), F32)],
        compiler_params=pltpu.CompilerParams(
            dimension_semantics=("parallel", "parallel", "arbitrary"),
            vmem_limit_bytes=CONV_VMEM_LIMIT),
        name="long_conv",
    )(*args)


def _out_kernel(x_ref, yh_ref, yp_ref, mod_ref, w_ref, b_ref, g_ref, beta_ref, o_ref,
                *, n2len, alpha):
    cbn = yh_ref.shape[0]
    yh = jnp.concatenate(
        [jnp.concatenate([yh_ref[cb, pl.ds(j, n2len, stride=SUBLANES), :]
                          for cb in range(cbn)], axis=1)
         for j in range(SUBLANES)], axis=0)
    y = jnp.concatenate([yh.astype(BF16), yp_ref[...]], axis=1)
    acc = jnp.dot(y, w_ref[...], preferred_element_type=F32)
    h = alpha * x_ref[...] + mod_ref[2:3, :] * (acc + b_ref[...])
    mu = jnp.mean(h, axis=-1, keepdims=True)
    hc = h - mu
    var = jnp.mean(hc * hc, axis=-1, keepdims=True)
    o_ref[...] = hc * lax.rsqrt(var + LN_EPS) * g_ref[...] + beta_ref[...]


def _output_projection(x, yh_perm, yp, mod3, w_out, b_out, ln_g, ln_b, alpha):
    bsz, seq, d = x.shape
    _, _, cbn, tile, _ = yh_perm.shape
    ch = cbn * LANES
    n2len = tile // SUBLANES
    dp = yp.shape[2]
    full = lambda shape: pl.BlockSpec(shape, lambda b, i: (0,) * len(shape))
    kern = functools.partial(_out_kernel, n2len=n2len, alpha=alpha)
    return pl.pallas_call(
        kern,
        grid=(bsz, seq // tile),
        in_specs=[
            pl.BlockSpec((None, tile, d), lambda b, i: (b, i, 0)),
            pl.BlockSpec((None, None, cbn, tile, LANES), lambda b, i: (b, i, 0, 0, 0)),
            pl.BlockSpec((None, tile, dp), lambda b, i: (b, i, 0)),
            pl.BlockSpec((None, 3, d), lambda b, i: (b, 0, 0)),
            full((ch + dp, d)), full((1, d)), full((1, d)), full((1, d)),
        ],
        out_specs=pl.BlockSpec((None, tile, d), lambda b, i: (b, i, 0)),
        out_shape=jax.ShapeDtypeStruct(x.shape, x.dtype),
        compiler_params=pltpu.CompilerParams(
            dimension_semantics=("parallel", "arbitrary"), vmem_limit_bytes=VMEM_LIMIT),
        name="out_proj_deepnorm",
    )(x, yh_perm, yp, mod3, w_out.astype(BF16), b_out.reshape(1, d), ln_g.reshape(1, d),
      ln_b.reshape(1, d))


def _forward(x, c, w_ada, b_ada, w_in, b_in, conv_w, conv_b, filt_w1, filt_b1, filt_w_inner,
             filt_b_inner, filt_w_out, filt_freq, hyena_bias, pool_w, pool_scale, w_out,
             b_out, ln_g, ln_b, *, n2len):
    bsz, seq, d = x.shape
    depth = w_ada.shape[0]
    order, n_ch = hyena_bias.shape[1], hyena_bias.shape[2]
    alpha = (2.0 * depth) ** 0.25
    tables = _dft_tables(seq, n2len)
    h = x
    for layer in range(depth):
        mod3 = _modulation(c, w_ada[layer], b_ada[layer]).reshape(bsz, 3, d)
        v, x1, x2g, yp = _input_projection(
            h, mod3, w_in[layer], b_in[layer], conv_w[layer], conv_b[layer], pool_w[layer],
            pool_scale[layer], n2len)
        taps_f, taps_r = _filter_taps(
            seq, n2len, filt_w1[layer], filt_b1[layer], filt_w_inner[layer],
            filt_b_inner[layer], filt_w_out[layer], filt_freq[layer], n_ch, order)
        kf = _filter_spectrum(taps_f, taps_r, tables[0], tables[1], tables[2], n2len)
        hbias = hyena_bias[layer].reshape(order, 1, n_ch)
        taps = [jnp.concatenate([conv_w[layer][:, k * n_ch:(k + 1) * n_ch],
                                 conv_b[layer][None, k * n_ch:(k + 1) * n_ch]], axis=0)
                for k in range(2)]
        z = _long_conv(v, x1, kf, hbias, 0, tables, in_taps=taps[0], mult_taps=taps[1])
        for o in range(1, order):
            z = _long_conv(z, x2g, kf, hbias, o, tables)
        h = _output_projection(h, z, yp, mod3, w_out[layer], b_out[layer], ln_g[layer],
                               ln_b[layer], alpha)
    return h


def kernel(x, c, w_ada, b_ada, w_in, b_in, conv_w, conv_b, filt_w1, filt_b1, filt_w_inner,
           filt_b_inner, filt_w_out, filt_freq, hyena_bias, pool_w, pool_scale, w_out, b_out,
           ln_g, ln_b):
    return _forward(x, c, w_ada, b_ada, w_in, b_in, conv_w, conv_b, filt_w1, filt_b1,
                    filt_w_inner, filt_b_inner, filt_w_out, filt_freq, hyena_bias, pool_w,
                    pool_scale, w_out, b_out, ln_g, ln_b, n2len=MINOR_LEN)
```

```python
import functools
import math

import jax
import jax.numpy as jnp
import numpy as np
from jax import lax
from jax.experimental import pallas as pl
from jax.experimental.pallas import tpu as pltpu

F32 = jnp.float32
BF16 = jnp.bfloat16
HIGHEST = lax.Precision.HIGHEST

POOL_WINDOWS = (2, 4, 8, 16)
LN_EPS = 1e-5
DECAY_TARGET = 1e-2
FAST_DECAY_PCT = 0.3
SLOW_DECAY_PCT = 1.5

LANES = 128
SUBLANES = 8
HALO = 16
BAND_ROWS = 128
BAND_SPAN = 256
MINOR_LEN = 128
FILTER_STEPS = (4, 4)
FILTER_COL_BLOCKS = 2
CONV_STEPS = (2, 2, 2)
V7X_VMEM_BYTES = 64 * 1024 * 1024
VMEM_LIMIT = V7X_VMEM_BYTES * 7 // 8
CONV_VMEM_LIMIT = V7X_VMEM_BYTES * 15 // 16


def _silu(x):
    h = 0.5 * x
    return h + h * jnp.tanh(h)


@functools.lru_cache(maxsize=None)
def _host_dft_tables(seq, n2len):
    n1h = seq // n2len
    n_fft = 2 * seq
    k1 = np.arange(n1h, dtype=np.int64)
    odd = 2 * k1 + 1
    alpha = ((odd[:, None] * k1[None, :]) % (4 * n1h)) * (2.0 * math.pi / (4 * n1h))
    n2e = np.arange(n2len + 1, dtype=np.int64)
    beta = ((n2e[:, None] * odd[None, :]) % (2 * n_fft)) * (2.0 * math.pi / (2 * n_fft))
    n2i = np.arange(n2len, dtype=np.int64)
    phi = ((n2i[:, None] * n2i[None, :]) % n2len) * (2.0 * math.pi / n2len)
    cm, sm = np.cos(phi), np.sin(phi)
    h2 = np.block([[cm, sm], [-sm, cm]])
    h2 = h2.reshape(2 * n2len, 2, n2len).transpose(0, 2, 1).reshape(2 * n2len, 2 * n2len)
    h2i = np.block([[cm, -sm], [sm, cm]])
    h2i = h2i.reshape(2, n2len, 2 * n2len).transpose(1, 0, 2).reshape(2 * n2len, 2 * n2len)
    h2, h2i = h2.astype(BF16), h2i.astype(BF16)
    quarter = 0.5 * math.pi * np.tile(np.array([0.0, 1.0]), n1h)[:, None]
    alpha2 = np.repeat(alpha, 2, axis=0)
    beta2 = np.repeat(beta, 2, axis=1)
    f32 = lambda a: a.astype(np.float32)
    seeds = tuple(f32(t) for t in (
        np.cos(alpha2 + quarter), np.sin(alpha2 + quarter),
        np.cos(alpha2 - quarter), np.sin(alpha2 - quarter), np.cos(beta2), np.sin(beta2)))
    return seeds + (h2, h2i)


def _dft_tables(seq, n2len):
    caf, saf, car, sar, cb, sb, h2, h2i = _host_dft_tables(seq, n2len)
    cb, sb = jnp.asarray(cb)[:, None, :], jnp.asarray(sb)[:, None, :]
    g_fwd = (caf.T[None] * cb[:-1] - saf.T[None] * sb[:-1]).astype(BF16)
    g_rev = (car.T[None] * cb[1:] - sar.T[None] * sb[1:]).astype(BF16)
    return g_fwd, g_rev, jnp.asarray(h2), jnp.asarray(h2i)


@functools.lru_cache(maxsize=None)
def _packed_pos_features(seq, emb, tile):
    hl = LANES // 2
    half = tile // 2
    bands = (emb - 1) // 2
    row = np.arange((seq // tile) * half, dtype=np.int64)[:, None]
    lane = np.arange(LANES, dtype=np.int64)[None, :]
    pos = ((row // half) * tile + (lane // hl) * half + row % half).astype(np.float64)
    feat = np.broadcast_to(lane % hl, pos.shape)
    t = pos / (seq - 1)
    w = (2.0 * math.pi / seq) * pos
    f = np.linspace(1e-4, bands - 1, bands)[(feat - 1) % bands]
    z = np.where(feat == 0, t,
                 np.where(feat <= bands, np.cos(f * w),
                          np.where(feat <= 2 * bands, -np.sin(f * w), 0.0)))
    return z.astype(np.float32)


def _mod_kernel(c_ref, w_ref, b_ref, o_ref):
    s = _silu(c_ref[...])
    o_ref[...] = jnp.dot(s, w_ref[...], precision=HIGHEST,
                         preferred_element_type=F32) + b_ref[...]


def _modulation(c, w_ada, b_ada):
    bsz, d = c.shape
    n_out = w_ada.shape[1]
    return pl.pallas_call(
        _mod_kernel,
        grid=(n_out // d,),
        in_specs=[pl.BlockSpec((bsz, d), lambda j: (0, 0)),
                  pl.BlockSpec((d, d), lambda j: (0, j)),
                  pl.BlockSpec((1, d), lambda j: (0, j))],
        out_specs=pl.BlockSpec((bsz, d), lambda j: (0, j)),
        out_shape=jax.ShapeDtypeStruct((bsz, n_out), F32),
        name="adaln_mod",
    )(c, w_ada, b_ada.reshape(1, n_out))


def _store_permuted(o_ref, val, n2len, first=0):
    for cb in range(val.shape[1] // LANES):
        for j in range(val.shape[0] // n2len):
            o_ref[cb, pl.ds(first + j, n2len, stride=SUBLANES), :] = val[
                j * n2len:(j + 1) * n2len, cb * LANES:(cb + 1) * LANES]


def _load_minor(x_ref, j):
    blk = x_ref[:, j * SUBLANES:(j + 1) * SUBLANES, :]
    return blk.reshape(blk.shape[0] * SUBLANES, blk.shape[2])


def _band_window(b, tile):
    last = tile + 2 * HALO - BAND_SPAN
    start = min(b * BAND_ROWS, last)
    return start, 0 if start == b * BAND_ROWS else 1


@functools.lru_cache(maxsize=None)
def _band_matrices(tile):
    t = np.arange(BAND_ROWS)[:, None]
    k = np.arange(BAND_SPAN)[None, :]
    mats = []
    for win in POOL_WINDOWS:
        half = win // 2
        per_variant = []
        for b in (0, tile // BAND_ROWS - 1):
            start, _ = _band_window(b, tile)
            rel = k + start - HALO - (b * BAND_ROWS + t)
            per_variant.append(((rel >= -half) & (rel < half)).astype(BF16))
        mats.append(np.stack(per_variant))
    return np.stack(mats)


def _proj_kernel(xp_ref, xc_ref, xn_ref, mod_ref, w_ref, b_ref, cw_ref, cb_ref, pw_ref,
                 ps_ref, band_ref, v_ref, x1_ref, x2_ref, yp_ref, p_scr, q_scr,
                 *, tile, n2len, ch, seq):
    i = pl.program_id(1)
    nt = pl.num_programs(1)
    shift = mod_ref[0:1, :]
    scale1 = 1.0 + mod_ref[1:2, :]
    xe = jnp.concatenate([xp_ref[...], xc_ref[...], xn_ref[...]], axis=0)
    ue = (xe * scale1 + shift).astype(BF16)
    uc = ue[HALO:HALO + tile, :]

    def zero_outside_sequence(p):
        return jnp.concatenate([jnp.where(i > 0, p[:HALO], 0.0), p[HALO:HALO + tile],
                                jnp.where(i < nt - 1, p[HALO + tile:], 0.0)], axis=0)

    for k, o_ref in enumerate((v_ref, x1_ref)):
        cols = slice(k * ch, (k + 1) * ch)
        p = jnp.dot(uc, w_ref[:, cols], preferred_element_type=F32) + b_ref[:, cols]
        _store_permuted(o_ref, p, n2len)

    cols = slice(2 * ch, 3 * ch)
    p = jnp.dot(ue, w_ref[:, cols], preferred_element_type=F32) + b_ref[:, cols]
    p_scr[...] = zero_outside_sequence(p)
    hg = jnp.dot(uc, w_ref[:, 3 * ch:4 * ch], preferred_element_type=F32) + b_ref[:, 3 * ch:4 * ch]
    gate = _silu(hg)
    s = (cb_ref[:, cols]
         + cw_ref[0:1, cols] * p_scr[HALO - 1:HALO - 1 + tile, :]
         + cw_ref[1:2, cols] * p_scr[HALO:HALO + tile, :]
         + cw_ref[2:3, cols] * p_scr[HALO + 1:HALO + 1 + tile, :])
    _store_permuted(x2_ref, s * gate, n2len)

    dp = ps_ref.shape[1]
    c0 = 4 * ch
    pin = jnp.dot(ue, w_ref[:, c0:c0 + dp], preferred_element_type=F32) + b_ref[:, c0:c0 + dp]
    pin = zero_outside_sequence(pin)
    q_scr[...] = pin.astype(BF16)
    p_scr[...] = pin
    pos = i * tile + lax.broadcasted_iota(jnp.int32, (tile, 1), 0)
    pg = dp // len(POOL_WINDOWS)
    groups = []
    for g, win in enumerate(POOL_WINDOWS):
        lanes = slice(g * pg, (g + 1) * pg)
        half = win // 2
        sums = []
        for b in range(tile // BAND_ROWS):
            start, variant = _band_window(b, tile)
            sums.append(jnp.dot(band_ref[g, variant], q_scr[start:start + BAND_SPAN, lanes],
                                preferred_element_type=F32))
        acc = jnp.concatenate(sums, axis=0)
        cnt = (jnp.minimum(pos + half, seq) - jnp.maximum(pos - half, 0)).astype(F32)
        diff = acc / cnt - p_scr[HALO:HALO + tile, lanes]
        groups.append(jnp.dot(diff.astype(BF16), pw_ref[g], preferred_element_type=F32))
    pgate = jnp.dot(uc, w_ref[:, c0 + dp:c0 + 2 * dp], preferred_element_type=F32) + b_ref[:, c0 + dp:c0 + 2 * dp]
    yp = jnp.concatenate(groups, axis=1) * ps_ref[...] * _silu(pgate)
    yp_ref[...] = yp.astype(yp_ref.dtype)


def _input_projection(x, mod3, w_in, b_in, conv_w, conv_b, pool_w, pool_scale, n2len):
    bsz, seq, d = x.shape
    ch = conv_w.shape[1] // 3
    dp = pool_scale.shape[0]
    n1h = seq // n2len
    tile = SUBLANES * n2len
    nt = seq // tile
    hb = tile // HALO
    n_proj = w_in.shape[1]
    cbn = ch // LANES
    assert max(POOL_WINDOWS) // 2 <= HALO and dp == ch and tile % BAND_ROWS == 0
    band = jnp.asarray(_band_matrices(tile))
    perm = jax.ShapeDtypeStruct((bsz, n1h // SUBLANES, cbn, tile, LANES), F32)
    perm_spec = pl.BlockSpec((None, None, cbn, tile, LANES), lambda b, i: (b, i, 0, 0, 0))
    full = lambda shape: pl.BlockSpec(shape, lambda b, i: (0,) * len(shape))
    kern = functools.partial(_proj_kernel, tile=tile, n2len=n2len, ch=ch, seq=seq)
    return pl.pallas_call(
        kern,
        grid=(bsz, nt),
        in_specs=[
            pl.BlockSpec((None, HALO, d), lambda b, i: (b, jnp.maximum(i * hb - 1, 0), 0)),
            pl.BlockSpec((None, tile, d), lambda b, i: (b, i, 0)),
            pl.BlockSpec((None, HALO, d), lambda b, i: (b, jnp.minimum((i + 1) * hb, seq // HALO - 1), 0)),
            pl.BlockSpec((None, 3, d), lambda b, i: (b, 0, 0)),
            full((d, n_proj)), full((1, n_proj)), full((3, 3 * ch)), full((1, 3 * ch)),
            full(pool_w.shape), full((1, dp)), full(band.shape),
        ],
        out_specs=[perm_spec, perm_spec, perm_spec,
                   pl.BlockSpec((None, tile, dp), lambda b, i: (b, i, 0))],
        out_shape=[perm, perm, perm, jax.ShapeDtypeStruct((bsz, seq, dp), BF16)],
        scratch_shapes=[pltpu.VMEM((tile + 2 * HALO, ch), F32),
                        pltpu.VMEM((tile + 2 * HALO, dp), BF16)],
        compiler_params=pltpu.CompilerParams(
            dimension_semantics=("parallel", "arbitrary"), vmem_limit_bytes=VMEM_LIMIT),
        name="in_proj_conv_pool",
    )(x, x, x, mod3, w_in.astype(BF16), b_in.reshape(1, n_proj), conv_w,
      conv_b.reshape(1, 3 * ch), pool_w.astype(BF16), pool_scale.reshape(1, dp), band)


def _filter_mlp_kernel(z_ref, w1_ref, b1_ref, wi_ref, bi_ref, fr_ref, wo_ref, ad_ref,
                       of_ref, or_ref, *, n2len):
    z = z_ref[...]
    fr = fr_ref[...]
    h = jnp.sin(fr * (jnp.dot(z.astype(BF16), w1_ref[...],
                              preferred_element_type=F32) + b1_ref[...]))
    for l in range(wi_ref.shape[0]):
        h = jnp.sin(fr * (jnp.dot(h.astype(BF16), wi_ref[l],
                                  preferred_element_type=F32) + bi_ref[l:l + 1, :]))
    hb = h.astype(BF16)
    half_lanes = LANES // 2
    n1_half = z.shape[0] // n2len
    for half in range(2):
        t = z[:, half * half_lanes:half * half_lanes + 1]
        decay = jnp.exp(-t * ad_ref[...])
        for d, o_ref in enumerate((of_ref, or_ref)):
            k = jnp.dot(hb, wo_ref[d, half], preferred_element_type=F32)
            _store_permuted(o_ref, k * decay, n2len, first=half * n1_half)


def _block_diag2(w):
    zero = jnp.zeros_like(w)
    return jnp.concatenate([jnp.concatenate([w, zero], axis=-1),
                            jnp.concatenate([zero, w], axis=-1)], axis=-2)


def _filter_taps(seq, n2len, w1, b1, w_inner, b_inner, w_out, freq, n_ch, order):
    emb, hid = w1.shape
    n_inner = w_inner.shape[0]
    oc = order * n_ch
    hl = LANES // 2
    assert emb <= hl and hid <= hl
    tile = SUBLANES * n2len
    half = tile // 2
    groups = seq // tile
    zp = jnp.asarray(_packed_pos_features(seq, emb, tile))
    ph = hl - hid
    w1d = _block_diag2(jnp.pad(w1, ((0, hl - emb), (0, ph))))
    b1d = jnp.tile(jnp.pad(b1, (0, ph)), 2).reshape(1, LANES)
    wid = _block_diag2(jnp.pad(w_inner, ((0, 0), (0, ph), (0, ph))))
    bid = jnp.tile(jnp.pad(b_inner, ((0, 0), (0, ph))), (1, 2))
    frd = jnp.tile(jnp.pad(freq, (0, ph), constant_values=1.0), 2).reshape(1, LANES)
    wo = jnp.transpose(w_out.reshape(hid, order, 2, n_ch), (2, 0, 1, 3)).reshape(2, hid, oc)
    wo = jnp.pad(wo, ((0, 0), (0, ph), (0, 0)))
    zero = jnp.zeros_like(wo)
    wo4 = jnp.stack([jnp.concatenate([wo, zero], axis=1),
                     jnp.concatenate([zero, wo], axis=1)], axis=1)
    min_decay = math.log(DECAY_TARGET) / SLOW_DECAY_PCT
    max_decay = math.log(DECAY_TARGET) / FAST_DECAY_PCT
    absdelta = jnp.abs(jnp.linspace(min_decay, max_decay, n_ch, dtype=F32))
    absdelta = jnp.tile(absdelta, order).reshape(1, oc)

    full = lambda shape: pl.BlockSpec(shape, lambda i: (0,) * len(shape))
    kern = functools.partial(_filter_mlp_kernel, n2len=n2len)
    taps = jax.ShapeDtypeStruct((groups, oc // LANES, tile, LANES), F32)
    taps_spec = pl.BlockSpec((None, oc // LANES, tile, LANES), lambda i: (i, 0, 0, 0))
    return pl.pallas_call(
        kern,
        grid=(groups,),
        in_specs=[
            pl.BlockSpec((half, LANES), lambda i: (i, 0)),
            full((LANES, LANES)), full((1, LANES)), full((n_inner, LANES, LANES)),
            full((n_inner, LANES)), full((1, LANES)), full((2, 2, LANES, oc)), full((1, oc)),
        ],
        out_specs=[taps_spec, taps_spec],
        out_shape=[taps, taps],
        compiler_params=pltpu.CompilerParams(
            dimension_semantics=("arbitrary",), vmem_limit_bytes=VMEM_LIMIT),
        name="filter_mlp",
    )(zp, w1d.astype(BF16), b1d, wid.astype(BF16), bid, frd, wo4.astype(BF16), absdelta)


def _resident(shape):
    return pl.BlockSpec(shape, lambda *_: (0,) * len(shape), pipeline_mode=pl.Buffered(1))


def _dot_rows(a, b):
    return lax.dot_general(a, b, (((0,), (0,)), ((), ())), preferred_element_type=F32)


def _pack_pairs(x):
    return pltpu.bitcast(x.astype(BF16), jnp.uint32)


def _unpack_pairs(w):
    return pltpu.bitcast(w, BF16)


def _spectrum_view(a_scr, chunk, ck1, pitch):
    start = pl.multiple_of(chunk * (ck1 * pitch), SUBLANES)
    return a_scr.at[pl.ds(start, ck1 * pitch)]


def _spectrum_pair(view, j, n2len, pitch):
    return jnp.concatenate([_unpack_pairs(view[jj * pitch:jj * pitch + n2len, :])
                            for jj in (j, j + 1)], axis=1)


def _filter_fft_kernel(hf_ref, hr_ref, g_ref, gr_ref, h2_ref, o_ref, a_scr,
                       *, ns, cn2, ck1, n1h, n2len, pitch):
    s = pl.program_id(1)
    ncb = a_scr.shape[0]

    @pl.when(s < ns)
    def _():
        for j in range(cn2):
            jr = cn2 - 1 - j
            m = s * cn2 + j
            g = jnp.concatenate([g_ref[m], gr_ref[n2len - 1 - m]], axis=0)
            x = jnp.concatenate(
                [jnp.concatenate([_load_minor(hf_ref.at[:, cb], j),
                                  _load_minor(hr_ref.at[:, cb], jr)], axis=0)
                 for cb in range(ncb)], axis=1)
            packed = _pack_pairs(_dot_rows(g, x.astype(BF16)))
            for cb in range(ncb):
                a_scr[cb, pl.ds(m, n1h, stride=pitch), :] = packed[:, cb * LANES:(cb + 1) * LANES]

    @pl.when(s >= ns)
    def _():
        for cb in range(ncb):
            view = _spectrum_view(a_scr.at[cb], s - ns, ck1, pitch)
            lanes = slice(cb * LANES, (cb + 1) * LANES)
            for j in range(0, ck1, 2):
                rows = _spectrum_pair(view, j, n2len, pitch)
                u = jnp.dot(h2_ref[...], rows, preferred_element_type=F32)
                u = (u * (1.0 / (n1h * n2len))).astype(o_ref.dtype)
                o_ref[j, :, lanes] = u[:, :LANES]
                o_ref[j + 1, :, lanes] = u[:, LANES:]


def _filter_spectrum(taps_f, taps_r, g_fwd, g_rev, h2, n2len):
    groups, ocb, _, _ = taps_f.shape
    oc = ocb * LANES
    n1h = groups * SUBLANES
    ns, ns2 = FILTER_STEPS
    cn2, ck1 = n2len // ns, n1h // ns2
    pitch = n2len + SUBLANES
    kern = functools.partial(_filter_fft_kernel, ns=ns, cn2=cn2, ck1=ck1, n1h=n1h,
                             n2len=n2len, pitch=pitch)
    ncb = FILTER_COL_BLOCKS
    return pl.pallas_call(
        kern,
        grid=(ocb // ncb, ns + ns2),
        in_specs=[
            pl.BlockSpec((groups, ncb, cn2 * SUBLANES, LANES),
                         lambda c, s: (0, c, jnp.minimum(s, ns - 1), 0)),
            pl.BlockSpec((groups, ncb, cn2 * SUBLANES, LANES),
                         lambda c, s: (0, c, jnp.maximum(ns - 1 - s, 0), 0)),
            _resident(g_fwd.shape), _resident(g_rev.shape), _resident(h2.shape),
        ],
        out_specs=pl.BlockSpec((ck1, 2 * n2len, ncb * LANES),
                               lambda c, s: (jnp.maximum(s - ns, 0), 0, c)),
        out_shape=jax.ShapeDtypeStruct((n1h, 2 * n2len, oc), BF16),
        scratch_shapes=[pltpu.VMEM((ncb, n1h * pitch, LANES), jnp.uint32)],
        compiler_params=pltpu.CompilerParams(
            dimension_semantics=("parallel", "arbitrary"), vmem_limit_bytes=CONV_VMEM_LIMIT),
        name="filter_spectrum",
    )(taps_f, taps_r, g_fwd, g_rev, h2)


def _short_conv_slab(x_ref, lo_ref, hi_ref, taps_ref, j, count, first_chunk, last_chunk):
    cur = _load_minor(x_ref, j)
    zero_row = jnp.zeros((1, cur.shape[1]), F32)
    if j > 0:
        prev = _load_minor(x_ref, j - 1)
    else:
        prev = _load_minor(lo_ref, 0)
        prev = jnp.where(first_chunk, jnp.concatenate([zero_row, prev[:-1]], axis=0), prev)
    if j < count - 1:
        nxt = _load_minor(x_ref, j + 1)
    else:
        nxt = _load_minor(hi_ref, 0)
        nxt = jnp.where(last_chunk, jnp.concatenate([nxt[1:], zero_row], axis=0), nxt)
    return (taps_ref[3:4, :] + taps_ref[0:1, :] * prev + taps_ref[1:2, :] * cur
            + taps_ref[2:3, :] * nxt)


def _long_conv_kernel(*refs, steps, n1h, n2len, pitch, conv_in, conv_mult):
    refs = list(refs)
    vf_ref = refs.pop(0)
    vlo_ref, vhi_ref, vtaps_ref = (refs.pop(0), refs.pop(0), refs.pop(0)) if conv_in else (None,) * 3
    g_ref, kf_ref, h2_ref, h2i_ref, xm_ref = (refs.pop(0) for _ in range(5))
    xlo_ref, xhi_ref, xtaps_ref = (refs.pop(0), refs.pop(0), refs.pop(0)) if conv_mult else (None,) * 3
    hb_ref, o_ref, a_scr, keep_scr = refs
    ns1, ns2, ns3 = steps
    cn2a, ck1, cn2 = n2len // ns1, n1h // ns2, n2len // ns3
    s = pl.program_id(2)

    @pl.when(s < ns1)
    def _():
        base = s * cn2a
        for j in range(cn2a):
            if conv_in:
                x = _short_conv_slab(vf_ref, vlo_ref, vhi_ref, vtaps_ref, j, cn2a,
                                     s == 0, s == ns1 - 1)
            elif vf_ref.dtype == jnp.uint32:
                x = _unpack_pairs(_load_minor(vf_ref, j)).astype(F32)
            else:
                x = _load_minor(vf_ref, j)
            keep_scr[base + j] = x
            r = _dot_rows(g_ref[base + j], x.astype(BF16))
            a_scr[pl.ds(base + j, n1h, stride=pitch), :] = _pack_pairs(r)

    @pl.when((s >= ns1) & (s < ns1 + ns2))
    def _():
        view = _spectrum_view(a_scr, s - ns1, ck1, pitch)
        for j in range(0, ck1, 2):
            rows = _spectrum_pair(view, j, n2len, pitch)
            u = jnp.dot(h2_ref[...], rows, preferred_element_type=F32)
            ur, ui = u[:n2len], u[n2len:]
            kf = jnp.concatenate([kf_ref[j], kf_ref[j + 1]], axis=1).astype(F32)
            kr, ki = kf[:n2len], kf[n2len:]
            y = jnp.concatenate([ur * kr - ui * ki, ur * ki + ui * kr], axis=0)
            bb = _pack_pairs(jnp.dot(h2i_ref[...], y.astype(BF16), preferred_element_type=F32))
            for jj in range(2):
                view[(j + jj) * pitch:(j + jj) * pitch + n2len, :] = bb[
                    :, jj * LANES:(jj + 1) * LANES]

    @pl.when(s >= ns1 + ns2)
    def _():
        base = (s - ns1 - ns2) * cn2
        for j in range(cn2):
            rows = _unpack_pairs(a_scr[pl.ds(base + j, n1h, stride=pitch), :])
            y = jnp.dot(g_ref[base + j], rows, preferred_element_type=F32)
            if conv_mult:
                xm = _short_conv_slab(xm_ref, xlo_ref, xhi_ref, xtaps_ref, j, cn2,
                                      s == ns1 + ns2, s == ns1 + ns2 + ns3 - 1)
            else:
                xm = _load_minor(xm_ref, j)
            z = xm * (y + hb_ref[...] * keep_scr[base + j])
            if o_ref.dtype == jnp.uint32:
                z = _pack_pairs(z)
            o_ref[:, j * SUBLANES:(j + 1) * SUBLANES, :] = z.reshape(
                z.shape[0] // SUBLANES, SUBLANES, z.shape[1])


def _long_conv(u_perm, mult_perm, kf, hbias, order_idx, tables, in_taps=None, mult_taps=None,
               pack_out=False):
    g_fwd, _, h2, h2i = tables
    bsz, _, cbn, _, _ = u_perm.shape
    n2len, n1h = g_fwd.shape[0], g_fwd.shape[1]
    steps = CONV_STEPS
    ns1, ns2, ns3 = steps
    cn2a, cn2 = n2len // ns1, n2len // ns3
    pitch = n2len + SUBLANES

    def early(s):
        return jnp.minimum(s, ns1 - 1)

    def late(s):
        return jnp.clip(s - ns1 - ns2, 0, ns3 - 1)

    def chunk_specs(chunk_of, width, groups):
        slab = (None, groups, None, SUBLANES, LANES)
        return [
            pl.BlockSpec((None, groups, None, width * SUBLANES, LANES),
                         lambda c, b, s: (b, 0, c, chunk_of(s), 0)),
            pl.BlockSpec(slab, lambda c, b, s: (b, 0, c, (chunk_of(s) * width - 1) % n2len, 0)),
            pl.BlockSpec(slab, lambda c, b, s: (b, 0, c, ((chunk_of(s) + 1) * width) % n2len, 0)),
        ]

    taps_spec = pl.BlockSpec((4, LANES), lambda c, b, s: (0, c))
    in_specs, args = [], []
    specs = chunk_specs(early, cn2a, u_perm.shape[1])
    if in_taps is None:
        in_specs += specs[:1]
        args += [u_perm]
    else:
        in_specs += specs + [taps_spec]
        args += [u_perm, u_perm, u_perm, in_taps]
    in_specs += [
        _resident(g_fwd.shape),
        pl.BlockSpec((n1h // ns2, 2 * n2len, LANES),
                     lambda c, b, s: (jnp.clip(s - ns1, 0, ns2 - 1), 0, order_idx * cbn + c)),
        _resident(h2.shape), _resident(h2i.shape),
    ]
    args += [g_fwd, kf, h2, h2i]
    specs = chunk_specs(late, cn2, mult_perm.shape[1])
    if mult_taps is None:
        in_specs += specs[:1]
        args += [mult_perm]
    else:
        in_specs += specs + [taps_spec]
        args += [mult_perm, mult_perm, mult_perm, mult_taps]
    in_specs.append(pl.BlockSpec((None, 1, LANES), lambda c, b, s: (order_idx, 0, c)))
    args.append(hbias)
    out_groups = n1h // SUBLANES // (2 if pack_out else 1)
    out_spec = chunk_specs(late, cn2, out_groups)[0]
    out_shape = jax.ShapeDtypeStruct((bsz, out_groups, cbn, n2len * SUBLANES, LANES),
                                     jnp.uint32 if pack_out else F32)
    kern = functools.partial(_long_conv_kernel, steps=steps, n1h=n1h, n2len=n2len,
                             pitch=pitch, conv_in=in_taps is not None,
                             conv_mult=mult_taps is not None)
    return pl.pallas_call(
        kern,
        grid=(cbn, bsz, sum(steps)),
        in_specs=in_specs,
        out_specs=out_spec,
        out_shape=out_shape,
        scratch_shapes=[pltpu.VMEM((n1h * pitch, LANES), jnp.uint32),
                        pltpu.VMEM((n2len, n1h, LANES), F32)],
        compiler_params=pltpu.CompilerParams(
            dimension_semantics=("parallel", "parallel", "arbitrary"),
            vmem_limit_bytes=CONV_VMEM_LIMIT),
        name="long_conv",
    )(*args)


def _out_kernel(x_ref, yh_ref, yp_ref, mod_ref, w_ref, b_ref, g_ref, beta_ref, o_ref,
                *, n2len, alpha):
    cbn = yh_ref.shape[0]
    yh = jnp.concatenate(
        [jnp.concatenate([yh_ref[cb, pl.ds(j, n2len, stride=SUBLANES), :]
                          for cb in range(cbn)], axis=1)
         for j in range(SUBLANES)], axis=0)
    y = jnp.concatenate([yh.astype(BF16), yp_ref[...]], axis=1)
    acc = jnp.dot(y, w_ref[...], preferred_element_type=F32)
    h = alpha * x_ref[...] + mod_ref[2:3, :] * (acc + b_ref[...])
    mu = jnp.mean(h, axis=-1, keepdims=True)
    hc = h - mu
    var = jnp.mean(hc * hc, axis=-1, keepdims=True)
    o_ref[...] = hc * lax.rsqrt(var + LN_EPS) * g_ref[...] + beta_ref[...]


def _output_projection(x, yh_perm, yp, mod3, w_out, b_out, ln_g, ln_b, alpha):
    bsz, seq, d = x.shape
    _, _, cbn, tile, _ = yh_perm.shape
    ch = cbn * LANES
    n2len = tile // SUBLANES
    dp = yp.shape[2]
    full = lambda shape: pl.BlockSpec(shape, lambda b, i: (0,) * len(shape))
    kern = functools.partial(_out_kernel, n2len=n2len, alpha=alpha)
    return pl.pallas_call(
        kern,
        grid=(bsz, seq // tile),
        in_specs=[
            pl.BlockSpec((None, tile, d), lambda b, i: (b, i, 0)),
            pl.BlockSpec((None, None, cbn, tile, LANES), lambda b, i: (b, i, 0, 0, 0)),
            pl.BlockSpec((None, tile, dp), lambda b, i: (b, i, 0)),
            pl.BlockSpec((None, 3, d), lambda b, i: (b, 0, 0)),
            full((ch + dp, d)), full((1, d)), full((1, d)), full((1, d)),
        ],
        out_specs=pl.BlockSpec((None, tile, d), lambda b, i: (b, i, 0)),
        out_shape=jax.ShapeDtypeStruct(x.shape, x.dtype),
        compiler_params=pltpu.CompilerParams(
            dimension_semantics=("parallel", "arbitrary"), vmem_limit_bytes=VMEM_LIMIT),
        name="out_proj_deepnorm",
    )(x, yh_perm, yp, mod3, w_out.astype(BF16), b_out.reshape(1, d), ln_g.reshape(1, d),
      ln_b.reshape(1, d))


def _forward(x, c, w_ada, b_ada, w_in, b_in, conv_w, conv_b, filt_w1, filt_b1, filt_w_inner,
             filt_b_inner, filt_w_out, filt_freq, hyena_bias, pool_w, pool_scale, w_out,
             b_out, ln_g, ln_b, *, n2len):
    bsz, seq, d = x.shape
    depth = w_ada.shape[0]
    order, n_ch = hyena_bias.shape[1], hyena_bias.shape[2]
    alpha = (2.0 * depth) ** 0.25
    tables = _dft_tables(seq, n2len)
    h = x
    for layer in range(depth):
        mod3 = _modulation(c, w_ada[layer], b_ada[layer]).reshape(bsz, 3, d)
        v, x1, x2g, yp = _input_projection(
            h, mod3, w_in[layer], b_in[layer], conv_w[layer], conv_b[layer], pool_w[layer],
            pool_scale[layer], n2len)
        taps_f, taps_r = _filter_taps(
            seq, n2len, filt_w1[layer], filt_b1[layer], filt_w_inner[layer],
            filt_b_inner[layer], filt_w_out[layer], filt_freq[layer], n_ch, order)
        kf = _filter_spectrum(taps_f, taps_r, tables[0], tables[1], tables[2], n2len)
        hbias = hyena_bias[layer].reshape(order, 1, n_ch)
        taps = [jnp.concatenate([conv_w[layer][:, k * n_ch:(k + 1) * n_ch],
                                 conv_b[layer][None, k * n_ch:(k + 1) * n_ch]], axis=0)
                for k in range(2)]
        z = _long_conv(v, x1, kf, hbias, 0, tables, in_taps=taps[0], mult_taps=taps[1],
                       pack_out=order > 1)
        for o in range(1, order):
            z = _long_conv(z, x2g, kf, hbias, o, tables)
        h = _output_projection(h, z, yp, mod3, w_out[layer], b_out[layer], ln_g[layer],
                               ln_b[layer], alpha)
    return h


def kernel(x, c, w_ada, b_ada, w_in, b_in, conv_w, conv_b, filt_w1, filt_b1, filt_w_inner,
           filt_b_inner, filt_w_out, filt_freq, hyena_bias, pool_w, pool_scale, w_out, b_out,
           ln_g, ln_b):
    return _forward(x, c, w_ada, b_ada, w_in, b_in, conv_w, conv_b, filt_w1, filt_b1,
                    filt_w_inner, filt_b_inner, filt_w_out, filt_freq, hyena_bias, pool_w,
                    pool_scale, w_out, b_out, ln_g, ln_b, n2len=MINOR_LEN)
```

```python
import functools
import math

import jax
import jax.numpy as jnp
import numpy as np
from jax import lax
from jax.experimental import pallas as pl
from jax.experimental.pallas import tpu as pltpu

F32 = jnp.float32
BF16 = jnp.bfloat16
HIGHEST = lax.Precision.HIGHEST

POOL_WINDOWS = (2, 4, 8, 16)
LN_EPS = 1e-5
DECAY_TARGET = 1e-2
FAST_DECAY_PCT = 0.3
SLOW_DECAY_PCT = 1.5

LANES = 128
SUBLANES = 8
HALO = 16
BAND_ROWS = 128
BAND_SPAN = 256
MINOR_LEN = 128
FILTER_MLP_TILES = 2
FILTER_STEPS = (4, 4)
FILTER_COL_BLOCKS = 2
CONV_STEPS = (2, 2, 2)
CONV_STEPS_PACKED_IN = (1, 2, 2)
V7X_VMEM_BYTES = 64 * 1024 * 1024
VMEM_LIMIT = V7X_VMEM_BYTES * 7 // 8
CONV_VMEM_LIMIT = V7X_VMEM_BYTES * 15 // 16


def _silu(x):
    h = 0.5 * x
    return h + h * jnp.tanh(h)


@functools.lru_cache(maxsize=None)
def _host_dft_tables(seq, n2len):
    n1h = seq // n2len
    n_fft = 2 * seq
    k1 = np.arange(n1h, dtype=np.int64)
    odd = 2 * k1 + 1
    alpha = ((odd[:, None] * k1[None, :]) % (4 * n1h)) * (2.0 * math.pi / (4 * n1h))
    n2e = np.arange(n2len + 1, dtype=np.int64)
    beta = ((n2e[:, None] * odd[None, :]) % (2 * n_fft)) * (2.0 * math.pi / (2 * n_fft))
    n2i = np.arange(n2len, dtype=np.int64)
    phi = ((n2i[:, None] * n2i[None, :]) % n2len) * (2.0 * math.pi / n2len)
    cm, sm = np.cos(phi), np.sin(phi)
    h2 = np.block([[cm, sm], [-sm, cm]])
    h2 = h2.reshape(2 * n2len, 2, n2len).transpose(0, 2, 1).reshape(2 * n2len, 2 * n2len)
    h2i = np.block([[cm, -sm], [sm, cm]])
    h2i = h2i.reshape(2, n2len, 2 * n2len).transpose(1, 0, 2).reshape(2 * n2len, 2 * n2len)
    h2, h2i = h2.astype(BF16), h2i.astype(BF16)
    quarter = 0.5 * math.pi * np.tile(np.array([0.0, 1.0]), n1h)[:, None]
    alpha2 = np.repeat(alpha, 2, axis=0)
    beta2 = np.repeat(beta, 2, axis=1)
    f32 = lambda a: a.astype(np.float32)
    seeds = tuple(f32(t) for t in (
        np.cos(alpha2 + quarter), np.sin(alpha2 + quarter),
        np.cos(alpha2 - quarter), np.sin(alpha2 - quarter), np.cos(beta2), np.sin(beta2)))
    return seeds + (h2, h2i)


def _dft_tables(seq, n2len):
    caf, saf, car, sar, cb, sb, h2, h2i = _host_dft_tables(seq, n2len)
    cb, sb = jnp.asarray(cb)[:, None, :], jnp.asarray(sb)[:, None, :]
    g_fwd = (caf.T[None] * cb[:-1] - saf.T[None] * sb[:-1]).astype(BF16)
    g_rev = (car.T[None] * cb[1:] - sar.T[None] * sb[1:]).astype(BF16)
    return g_fwd, g_rev, jnp.asarray(h2), jnp.asarray(h2i)


@functools.lru_cache(maxsize=None)
def _packed_pos_features(seq, emb, tile):
    hl = LANES // 2
    half = tile // 2
    bands = (emb - 1) // 2
    row = np.arange((seq // tile) * half, dtype=np.int64)[:, None]
    lane = np.arange(LANES, dtype=np.int64)[None, :]
    pos = ((row // half) * tile + (lane // hl) * half + row % half).astype(np.float64)
    feat = np.broadcast_to(lane % hl, pos.shape)
    t = pos / (seq - 1)
    w = (2.0 * math.pi / seq) * pos
    f = np.linspace(1e-4, bands - 1, bands)[(feat - 1) % bands]
    z = np.where(feat == 0, t,
                 np.where(feat <= bands, np.cos(f * w),
                          np.where(feat <= 2 * bands, -np.sin(f * w), 0.0)))
    return z.astype(np.float32)


def _mod_kernel(c_ref, w_ref, b_ref, o_ref):
    s = _silu(c_ref[...])
    o_ref[...] = jnp.dot(s, w_ref[...], precision=HIGHEST,
                         preferred_element_type=F32) + b_ref[...]


def _modulation(c, w_ada, b_ada):
    bsz, d = c.shape
    n_out = w_ada.shape[1]
    return pl.pallas_call(
        _mod_kernel,
        grid=(n_out // d,),
        in_specs=[pl.BlockSpec((bsz, d), lambda j: (0, 0)),
                  pl.BlockSpec((d, d), lambda j: (0, j)),
                  pl.BlockSpec((1, d), lambda j: (0, j))],
        out_specs=pl.BlockSpec((bsz, d), lambda j: (0, j)),
        out_shape=jax.ShapeDtypeStruct((bsz, n_out), F32),
        name="adaln_mod",
    )(c, w_ada, b_ada.reshape(1, n_out))


def _store_permuted(o_ref, val, n2len, first=0):
    for cb in range(val.shape[1] // LANES):
        for j in range(val.shape[0] // n2len):
            o_ref[cb, pl.ds(first + j, n2len, stride=SUBLANES), :] = val[
                j * n2len:(j + 1) * n2len, cb * LANES:(cb + 1) * LANES]


def _load_minor(x_ref, j):
    blk = x_ref[:, j * SUBLANES:(j + 1) * SUBLANES, :]
    return blk.reshape(blk.shape[0] * SUBLANES, blk.shape[2])


def _band_window(b, tile):
    last = tile + 2 * HALO - BAND_SPAN
    start = min(b * BAND_ROWS, last)
    return start, 0 if start == b * BAND_ROWS else 1


@functools.lru_cache(maxsize=None)
def _band_matrices(tile):
    t = np.arange(BAND_ROWS)[:, None]
    k = np.arange(BAND_SPAN)[None, :]
    mats = []
    for win in POOL_WINDOWS:
        half = win // 2
        per_variant = []
        for b in (0, tile // BAND_ROWS - 1):
            start, _ = _band_window(b, tile)
            rel = k + start - HALO - (b * BAND_ROWS + t)
            per_variant.append(((rel >= -half) & (rel < half)).astype(BF16))
        mats.append(np.stack(per_variant))
    return np.stack(mats)


def _proj_kernel(xp_ref, xc_ref, xn_ref, mod_ref, w_ref, b_ref, cw_ref, cb_ref, pw_ref,
                 ps_ref, band_ref, v_ref, x1_ref, x2_ref, yp_ref, p_scr, q_scr,
                 *, tile, n2len, ch, seq):
    i = pl.program_id(1)
    nt = pl.num_programs(1)
    shift = mod_ref[0:1, :]
    scale1 = 1.0 + mod_ref[1:2, :]
    xe = jnp.concatenate([xp_ref[...], xc_ref[...], xn_ref[...]], axis=0)
    ue = (xe * scale1 + shift).astype(BF16)
    uc = ue[HALO:HALO + tile, :]

    def zero_outside_sequence(p):
        return jnp.concatenate([jnp.where(i > 0, p[:HALO], 0.0), p[HALO:HALO + tile],
                                jnp.where(i < nt - 1, p[HALO + tile:], 0.0)], axis=0)

    for k, o_ref in enumerate((v_ref, x1_ref)):
        cols = slice(k * ch, (k + 1) * ch)
        p = jnp.dot(uc, w_ref[:, cols], preferred_element_type=F32) + b_ref[:, cols]
        _store_permuted(o_ref, p, n2len)

    cols = slice(2 * ch, 3 * ch)
    p = jnp.dot(ue, w_ref[:, cols], preferred_element_type=F32) + b_ref[:, cols]
    p_scr[...] = zero_outside_sequence(p)
    hg = jnp.dot(uc, w_ref[:, 3 * ch:4 * ch], preferred_element_type=F32) + b_ref[:, 3 * ch:4 * ch]
    gate = _silu(hg)
    s = (cb_ref[:, cols]
         + cw_ref[0:1, cols] * p_scr[HALO - 1:HALO - 1 + tile, :]
         + cw_ref[1:2, cols] * p_scr[HALO:HALO + tile, :]
         + cw_ref[2:3, cols] * p_scr[HALO + 1:HALO + 1 + tile, :])
    _store_permuted(x2_ref, s * gate, n2len)

    dp = ps_ref.shape[1]
    c0 = 4 * ch
    pin = jnp.dot(ue, w_ref[:, c0:c0 + dp], preferred_element_type=F32) + b_ref[:, c0:c0 + dp]
    pin = zero_outside_sequence(pin)
    q_scr[...] = pin.astype(BF16)
    p_scr[...] = pin
    pos = i * tile + lax.broadcasted_iota(jnp.int32, (tile, 1), 0)
    pg = dp // len(POOL_WINDOWS)
    groups = []
    for g, win in enumerate(POOL_WINDOWS):
        lanes = slice(g * pg, (g + 1) * pg)
        half = win // 2
        sums = []
        for b in range(tile // BAND_ROWS):
            start, variant = _band_window(b, tile)
            sums.append(jnp.dot(band_ref[g, variant], q_scr[start:start + BAND_SPAN, lanes],
                                preferred_element_type=F32))
        acc = jnp.concatenate(sums, axis=0)
        cnt = (jnp.minimum(pos + half, seq) - jnp.maximum(pos - half, 0)).astype(F32)
        diff = acc / cnt - p_scr[HALO:HALO + tile, lanes]
        groups.append(jnp.dot(diff.astype(BF16), pw_ref[g], preferred_element_type=F32))
    pgate = jnp.dot(uc, w_ref[:, c0 + dp:c0 + 2 * dp], preferred_element_type=F32) + b_ref[:, c0 + dp:c0 + 2 * dp]
    yp = jnp.concatenate(groups, axis=1) * ps_ref[...] * _silu(pgate)
    yp_ref[...] = yp.astype(yp_ref.dtype)


def _input_projection(x, mod3, w_in, b_in, conv_w, conv_b, pool_w, pool_scale, n2len):
    bsz, seq, d = x.shape
    ch = conv_w.shape[1] // 3
    dp = pool_scale.shape[0]
    n1h = seq // n2len
    tile = SUBLANES * n2len
    nt = seq // tile
    hb = tile // HALO
    n_proj = w_in.shape[1]
    cbn = ch // LANES
    assert max(POOL_WINDOWS) // 2 <= HALO and dp == ch and tile % BAND_ROWS == 0
    band = jnp.asarray(_band_matrices(tile))
    perm = jax.ShapeDtypeStruct((bsz, n1h // SUBLANES, cbn, tile, LANES), F32)
    perm_spec = pl.BlockSpec((None, None, cbn, tile, LANES), lambda b, i: (b, i, 0, 0, 0))
    full = lambda shape: pl.BlockSpec(shape, lambda b, i: (0,) * len(shape))
    kern = functools.partial(_proj_kernel, tile=tile, n2len=n2len, ch=ch, seq=seq)
    return pl.pallas_call(
        kern,
        grid=(bsz, nt),
        in_specs=[
            pl.BlockSpec((None, HALO, d), lambda b, i: (b, jnp.maximum(i * hb - 1, 0), 0)),
            pl.BlockSpec((None, tile, d), lambda b, i: (b, i, 0)),
            pl.BlockSpec((None, HALO, d), lambda b, i: (b, jnp.minimum((i + 1) * hb, seq // HALO - 1), 0)),
            pl.BlockSpec((None, 3, d), lambda b, i: (b, 0, 0)),
            full((d, n_proj)), full((1, n_proj)), full((3, 3 * ch)), full((1, 3 * ch)),
            full(pool_w.shape), full((1, dp)), full(band.shape),
        ],
        out_specs=[perm_spec, perm_spec, perm_spec,
                   pl.BlockSpec((None, tile, dp), lambda b, i: (b, i, 0))],
        out_shape=[perm, perm, perm, jax.ShapeDtypeStruct((bsz, seq, dp), BF16)],
        scratch_shapes=[pltpu.VMEM((tile + 2 * HALO, ch), F32),
                        pltpu.VMEM((tile + 2 * HALO, dp), BF16)],
        compiler_params=pltpu.CompilerParams(
            dimension_semantics=("parallel", "arbitrary"), vmem_limit_bytes=VMEM_LIMIT),
        name="in_proj_conv_pool",
    )(x, x, x, mod3, w_in.astype(BF16), b_in.reshape(1, n_proj), conv_w,
      conv_b.reshape(1, 3 * ch), pool_w.astype(BF16), pool_scale.reshape(1, dp), band)


def _filter_mlp_kernel(z_ref, w1_ref, b1_ref, wi_ref, bi_ref, fr_ref, wo_ref, ad_ref,
                       of_ref, or_ref, *, n2len):
    fr = fr_ref[...]
    half_lanes = LANES // 2
    rows = z_ref.shape[0] // of_ref.shape[0]
    n1_half = rows // n2len
    for gi in range(of_ref.shape[0]):
        z = z_ref[gi * rows:(gi + 1) * rows, :]
        h = jnp.sin(fr * (jnp.dot(z.astype(BF16), w1_ref[...],
                                  preferred_element_type=F32) + b1_ref[...]))
        for l in range(wi_ref.shape[0]):
            h = jnp.sin(fr * (jnp.dot(h.astype(BF16), wi_ref[l],
                                      preferred_element_type=F32) + bi_ref[l:l + 1, :]))
        hb = h.astype(BF16)
        for half in range(2):
            t = z[:, half * half_lanes:half * half_lanes + 1]
            decay = jnp.exp(-t * ad_ref[...])
            for d, o_ref in enumerate((of_ref, or_ref)):
                k = jnp.dot(hb, wo_ref[d, half], preferred_element_type=F32)
                _store_permuted(o_ref.at[gi], k * decay, n2len, first=half * n1_half)


def _block_diag2(w):
    zero = jnp.zeros_like(w)
    return jnp.concatenate([jnp.concatenate([w, zero], axis=-1),
                            jnp.concatenate([zero, w], axis=-1)], axis=-2)


def _filter_taps(seq, n2len, w1, b1, w_inner, b_inner, w_out, freq, n_ch, order):
    emb, hid = w1.shape
    n_inner = w_inner.shape[0]
    oc = order * n_ch
    hl = LANES // 2
    assert emb <= hl and hid <= hl
    tile = SUBLANES * n2len
    half = tile // 2
    groups = seq // tile
    zp = jnp.asarray(_packed_pos_features(seq, emb, tile))
    ph = hl - hid
    w1d = _block_diag2(jnp.pad(w1, ((0, hl - emb), (0, ph))))
    b1d = jnp.tile(jnp.pad(b1, (0, ph)), 2).reshape(1, LANES)
    wid = _block_diag2(jnp.pad(w_inner, ((0, 0), (0, ph), (0, ph))))
    bid = jnp.tile(jnp.pad(b_inner, ((0, 0), (0, ph))), (1, 2))
    frd = jnp.tile(jnp.pad(freq, (0, ph), constant_values=1.0), 2).reshape(1, LANES)
    wo = jnp.transpose(w_out.reshape(hid, order, 2, n_ch), (2, 0, 1, 3)).reshape(2, hid, oc)
    wo = jnp.pad(wo, ((0, 0), (0, ph), (0, 0)))
    zero = jnp.zeros_like(wo)
    wo4 = jnp.stack([jnp.concatenate([wo, zero], axis=1),
                     jnp.concatenate([zero, wo], axis=1)], axis=1)
    min_decay = math.log(DECAY_TARGET) / SLOW_DECAY_PCT
    max_decay = math.log(DECAY_TARGET) / FAST_DECAY_PCT
    absdelta = jnp.abs(jnp.linspace(min_decay, max_decay, n_ch, dtype=F32))
    absdelta = jnp.tile(absdelta, order).reshape(1, oc)

    full = lambda shape: pl.BlockSpec(shape, lambda i: (0,) * len(shape))
    kern = functools.partial(_filter_mlp_kernel, n2len=n2len)
    taps = jax.ShapeDtypeStruct((groups, oc // LANES, tile, LANES), F32)
    gps = FILTER_MLP_TILES
    taps_spec = pl.BlockSpec((gps, oc // LANES, tile, LANES), lambda i: (i, 0, 0, 0))
    return pl.pallas_call(
        kern,
        grid=(groups // gps,),
        in_specs=[
            pl.BlockSpec((gps * half, LANES), lambda i: (i, 0)),
            full((LANES, LANES)), full((1, LANES)), full((n_inner, LANES, LANES)),
            full((n_inner, LANES)), full((1, LANES)), full((2, 2, LANES, oc)), full((1, oc)),
        ],
        out_specs=[taps_spec, taps_spec],
        out_shape=[taps, taps],
        compiler_params=pltpu.CompilerParams(
            dimension_semantics=("arbitrary",), vmem_limit_bytes=VMEM_LIMIT),
        name="filter_mlp",
    )(zp, w1d.astype(BF16), b1d, wid.astype(BF16), bid, frd, wo4.astype(BF16), absdelta)


def _resident(shape):
    return pl.BlockSpec(shape, lambda *_: (0,) * len(shape), pipeline_mode=pl.Buffered(1))


def _dot_rows(a, b):
    return lax.dot_general(a, b, (((0,), (0,)), ((), ())), preferred_element_type=F32)


def _pack_pairs(x):
    return pltpu.bitcast(x.astype(BF16), jnp.uint32)


def _unpack_pairs(w):
    return pltpu.bitcast(w, BF16)


def _spectrum_view(a_scr, chunk, ck1, pitch):
    start = pl.multiple_of(chunk * (ck1 * pitch), SUBLANES)
    return a_scr.at[pl.ds(start, ck1 * pitch)]


def _spectrum_pair(view, j, n2len, pitch):
    return jnp.concatenate([_unpack_pairs(view[jj * pitch:jj * pitch + n2len, :])
                            for jj in (j, j + 1)], axis=1)


def _filter_fft_kernel(hf_ref, hr_ref, g_ref, gr_ref, h2_ref, o_ref, a_scr,
                       *, ns, cn2, ck1, n1h, n2len, pitch):
    s = pl.program_id(1)
    ncb = a_scr.shape[0]

    @pl.when(s < ns)
    def _():
        for j in range(cn2):
            jr = cn2 - 1 - j
            m = s * cn2 + j
            g = jnp.concatenate([g_ref[m], gr_ref[n2len - 1 - m]], axis=0)
            x = jnp.concatenate(
                [jnp.concatenate([_load_minor(hf_ref.at[:, cb], j),
                                  _load_minor(hr_ref.at[:, cb], jr)], axis=0)
                 for cb in range(ncb)], axis=1)
            packed = _pack_pairs(_dot_rows(g, x.astype(BF16)))
            for cb in range(ncb):
                a_scr[cb, pl.ds(m, n1h, stride=pitch), :] = packed[:, cb * LANES:(cb + 1) * LANES]

    @pl.when(s >= ns)
    def _():
        for cb in range(ncb):
            view = _spectrum_view(a_scr.at[cb], s - ns, ck1, pitch)
            lanes = slice(cb * LANES, (cb + 1) * LANES)
            for j in range(0, ck1, 2):
                rows = _spectrum_pair(view, j, n2len, pitch)
                u = jnp.dot(h2_ref[...], rows, preferred_element_type=F32)
                u = (u * (1.0 / (n1h * n2len))).astype(o_ref.dtype)
                o_ref[j, :, lanes] = u[:, :LANES]
                o_ref[j + 1, :, lanes] = u[:, LANES:]


def _filter_spectrum(taps_f, taps_r, g_fwd, g_rev, h2, n2len):
    groups, ocb, _, _ = taps_f.shape
    oc = ocb * LANES
    n1h = groups * SUBLANES
    ns, ns2 = FILTER_STEPS
    cn2, ck1 = n2len // ns, n1h // ns2
    pitch = n2len + SUBLANES
    kern = functools.partial(_filter_fft_kernel, ns=ns, cn2=cn2, ck1=ck1, n1h=n1h,
                             n2len=n2len, pitch=pitch)
    ncb = FILTER_COL_BLOCKS
    return pl.pallas_call(
        kern,
        grid=(ocb // ncb, ns + ns2),
        in_specs=[
            pl.BlockSpec((groups, ncb, cn2 * SUBLANES, LANES),
                         lambda c, s: (0, c, jnp.minimum(s, ns - 1), 0)),
            pl.BlockSpec((groups, ncb, cn2 * SUBLANES, LANES),
                         lambda c, s: (0, c, jnp.maximum(ns - 1 - s, 0), 0)),
            _resident(g_fwd.shape), _resident(g_rev.shape), _resident(h2.shape),
        ],
        out_specs=pl.BlockSpec((ck1, 2 * n2len, ncb * LANES),
                               lambda c, s: (jnp.maximum(s - ns, 0), 0, c)),
        out_shape=jax.ShapeDtypeStruct((n1h, 2 * n2len, oc), BF16),
        scratch_shapes=[pltpu.VMEM((ncb, n1h * pitch, LANES), jnp.uint32)],
        compiler_params=pltpu.CompilerParams(
            dimension_semantics=("parallel", "arbitrary"), vmem_limit_bytes=CONV_VMEM_LIMIT),
        name="filter_spectrum",
    )(taps_f, taps_r, g_fwd, g_rev, h2)


def _short_conv_slab(x_ref, lo_ref, hi_ref, taps_ref, j, count, first_chunk, last_chunk):
    cur = _load_minor(x_ref, j)
    zero_row = jnp.zeros((1, cur.shape[1]), F32)
    if j > 0:
        prev = _load_minor(x_ref, j - 1)
    else:
        prev = _load_minor(lo_ref, 0)
        prev = jnp.where(first_chunk, jnp.concatenate([zero_row, prev[:-1]], axis=0), prev)
    if j < count - 1:
        nxt = _load_minor(x_ref, j + 1)
    else:
        nxt = _load_minor(hi_ref, 0)
        nxt = jnp.where(last_chunk, jnp.concatenate([nxt[1:], zero_row], axis=0), nxt)
    return (taps_ref[3:4, :] + taps_ref[0:1, :] * prev + taps_ref[1:2, :] * cur
            + taps_ref[2:3, :] * nxt)


def _long_conv_kernel(*refs, steps, n1h, n2len, pitch, conv_in, conv_mult):
    refs = list(refs)
    vf_ref = refs.pop(0)
    vlo_ref, vhi_ref, vtaps_ref = (refs.pop(0), refs.pop(0), refs.pop(0)) if conv_in else (None,) * 3
    g_ref, kf_ref, h2_ref, h2i_ref, xm_ref = (refs.pop(0) for _ in range(5))
    xlo_ref, xhi_ref, xtaps_ref = (refs.pop(0), refs.pop(0), refs.pop(0)) if conv_mult else (None,) * 3
    hb_ref, o_ref, a_scr, keep_scr = refs
    ns1, ns2, ns3 = steps
    cn2a, ck1, cn2 = n2len // ns1, n1h // ns2, n2len // ns3
    s = pl.program_id(2)

    @pl.when(s < ns1)
    def _():
        base = s * cn2a
        for j in range(cn2a):
            if conv_in:
                x = _short_conv_slab(vf_ref, vlo_ref, vhi_ref, vtaps_ref, j, cn2a,
                                     s == 0, s == ns1 - 1)
            elif vf_ref.dtype == jnp.uint32:
                x = _unpack_pairs(_load_minor(vf_ref, j)).astype(F32)
            else:
                x = _load_minor(vf_ref, j)
            keep_scr[base + j] = x
            r = _dot_rows(g_ref[base + j], x.astype(BF16))
            a_scr[pl.ds(base + j, n1h, stride=pitch), :] = _pack_pairs(r)

    @pl.when((s >= ns1) & (s < ns1 + ns2))
    def _():
        view = _spectrum_view(a_scr, s - ns1, ck1, pitch)
        for j in range(0, ck1, 2):
            rows = _spectrum_pair(view, j, n2len, pitch)
            u = jnp.dot(h2_ref[...], rows, preferred_element_type=F32)
            ur, ui = u[:n2len], u[n2len:]
            kf = jnp.concatenate([kf_ref[j], kf_ref[j + 1]], axis=1).astype(F32)
            kr, ki = kf[:n2len], kf[n2len:]
            y = jnp.concatenate([ur * kr - ui * ki, ur * ki + ui * kr], axis=0)
            bb = _pack_pairs(jnp.dot(h2i_ref[...], y.astype(BF16), preferred_element_type=F32))
            for jj in range(2):
                view[(j + jj) * pitch:(j + jj) * pitch + n2len, :] = bb[
                    :, jj * LANES:(jj + 1) * LANES]

    @pl.when(s >= ns1 + ns2)
    def _():
        base = (s - ns1 - ns2) * cn2
        for j in range(cn2):
            rows = _unpack_pairs(a_scr[pl.ds(base + j, n1h, stride=pitch), :])
            y = jnp.dot(g_ref[base + j], rows, preferred_element_type=F32)
            if conv_mult:
                xm = _short_conv_slab(xm_ref, xlo_ref, xhi_ref, xtaps_ref, j, cn2,
                                      s == ns1 + ns2, s == ns1 + ns2 + ns3 - 1)
            else:
                xm = _load_minor(xm_ref, j)
            z = xm * (y + hb_ref[...] * keep_scr[base + j])
            if o_ref.dtype == jnp.uint32:
                z = _pack_pairs(z)
            o_ref[:, j * SUBLANES:(j + 1) * SUBLANES, :] = z.reshape(
                z.shape[0] // SUBLANES, SUBLANES, z.shape[1])


def _long_conv(u_perm, mult_perm, kf, hbias, order_idx, tables, in_taps=None, mult_taps=None,
               pack_out=False, steps=CONV_STEPS):
    g_fwd, _, h2, h2i = tables
    bsz, _, cbn, _, _ = u_perm.shape
    n2len, n1h = g_fwd.shape[0], g_fwd.shape[1]
    ns1, ns2, ns3 = steps
    cn2a, cn2 = n2len // ns1, n2len // ns3
    pitch = n2len + SUBLANES

    def early(s):
        return jnp.minimum(s, ns1 - 1)

    def late(s):
        return jnp.clip(s - ns1 - ns2, 0, ns3 - 1)

    def chunk_specs(chunk_of, width, groups):
        slab = (None, groups, None, SUBLANES, LANES)
        return [
            pl.BlockSpec((None, groups, None, width * SUBLANES, LANES),
                         lambda c, b, s: (b, 0, c, chunk_of(s), 0)),
            pl.BlockSpec(slab, lambda c, b, s: (b, 0, c, (chunk_of(s) * width - 1) % n2len, 0)),
            pl.BlockSpec(slab, lambda c, b, s: (b, 0, c, ((chunk_of(s) + 1) * width) % n2len, 0)),
        ]

    taps_spec = pl.BlockSpec((4, LANES), lambda c, b, s: (0, c))
    in_specs, args = [], []
    specs = chunk_specs(early, cn2a, u_perm.shape[1])
    if in_taps is None:
        in_specs += specs[:1]
        args += [u_perm]
    else:
        in_specs += specs + [taps_spec]
        args += [u_perm, u_perm, u_perm, in_taps]
    in_specs += [
        _resident(g_fwd.shape),
        pl.BlockSpec((n1h // ns2, 2 * n2len, LANES),
                     lambda c, b, s: (jnp.clip(s - ns1, 0, ns2 - 1), 0, order_idx * cbn + c)),
        _resident(h2.shape), _resident(h2i.shape),
    ]
    args += [g_fwd, kf, h2, h2i]
    specs = chunk_specs(late, cn2, mult_perm.shape[1])
    if mult_taps is None:
        in_specs += specs[:1]
        args += [mult_perm]
    else:
        in_specs += specs + [taps_spec]
        args += [mult_perm, mult_perm, mult_perm, mult_taps]
    in_specs.append(pl.BlockSpec((None, 1, LANES), lambda c, b, s: (order_idx, 0, c)))
    args.append(hbias)
    out_groups = n1h // SUBLANES // (2 if pack_out else 1)
    out_spec = chunk_specs(late, cn2, out_groups)[0]
    out_shape = jax.ShapeDtypeStruct((bsz, out_groups, cbn, n2len * SUBLANES, LANES),
                                     jnp.uint32 if pack_out else F32)
    kern = functools.partial(_long_conv_kernel, steps=steps, n1h=n1h, n2len=n2len,
                             pitch=pitch, conv_in=in_taps is not None,
                             conv_mult=mult_taps is not None)
    return pl.pallas_call(
        kern,
        grid=(cbn, bsz, sum(steps)),
        in_specs=in_specs,
        out_specs=out_spec,
        out_shape=out_shape,
        scratch_shapes=[pltpu.VMEM((n1h * pitch, LANES), jnp.uint32),
                        pltpu.VMEM((n2len, n1h, LANES), F32)],
        compiler_params=pltpu.CompilerParams(
            dimension_semantics=("parallel", "parallel", "arbitrary"),
            vmem_limit_bytes=CONV_VMEM_LIMIT),
        name="long_conv",
    )(*args)


def _out_kernel(x_ref, yh_ref, yp_ref, mod_ref, w_ref, b_ref, g_ref, beta_ref, o_ref,
                *, n2len, alpha):
    cbn = yh_ref.shape[0]
    yh = jnp.concatenate(
        [jnp.concatenate([yh_ref[cb, pl.ds(j, n2len, stride=SUBLANES), :]
                          for cb in range(cbn)], axis=1)
         for j in range(SUBLANES)], axis=0)
    y = jnp.concatenate([yh.astype(BF16), yp_ref[...]], axis=1)
    acc = jnp.dot(y, w_ref[...], preferred_element_type=F32)
    h = alpha * x_ref[...] + mod_ref[2:3, :] * (acc + b_ref[...])
    mu = jnp.mean(h, axis=-1, keepdims=True)
    hc = h - mu
    var = jnp.mean(hc * hc, axis=-1, keepdims=True)
    o_ref[...] = hc * lax.rsqrt(var + LN_EPS) * g_ref[...] + beta_ref[...]


def _output_projection(x, yh_perm, yp, mod3, w_out, b_out, ln_g, ln_b, alpha):
    bsz, seq, d = x.shape
    _, _, cbn, tile, _ = yh_perm.shape
    ch = cbn * LANES
    n2len = tile // SUBLANES
    dp = yp.shape[2]
    full = lambda shape: pl.BlockSpec(shape, lambda b, i: (0,) * len(shape))
    kern = functools.partial(_out_kernel, n2len=n2len, alpha=alpha)
    return pl.pallas_call(
        kern,
        grid=(bsz, seq // tile),
        in_specs=[
            pl.BlockSpec((None, tile, d), lambda b, i: (b, i, 0)),
            pl.BlockSpec((None, None, cbn, tile, LANES), lambda b, i: (b, i, 0, 0, 0)),
            pl.BlockSpec((None, tile, dp), lambda b, i: (b, i, 0)),
            pl.BlockSpec((None, 3, d), lambda b, i: (b, 0, 0)),
            full((ch + dp, d)), full((1, d)), full((1, d)), full((1, d)),
        ],
        out_specs=pl.BlockSpec((None, tile, d), lambda b, i: (b, i, 0)),
        out_shape=jax.ShapeDtypeStruct(x.shape, x.dtype),
        compiler_params=pltpu.CompilerParams(
            dimension_semantics=("parallel", "arbitrary"), vmem_limit_bytes=VMEM_LIMIT),
        name="out_proj_deepnorm",
    )(x, yh_perm, yp, mod3, w_out.astype(BF16), b_out.reshape(1, d), ln_g.reshape(1, d),
      ln_b.reshape(1, d))


def _forward(x, c, w_ada, b_ada, w_in, b_in, conv_w, conv_b, filt_w1, filt_b1, filt_w_inner,
             filt_b_inner, filt_w_out, filt_freq, hyena_bias, pool_w, pool_scale, w_out,
             b_out, ln_g, ln_b, *, n2len):
    bsz, seq, d = x.shape
    depth = w_ada.shape[0]
    order, n_ch = hyena_bias.shape[1], hyena_bias.shape[2]
    alpha = (2.0 * depth) ** 0.25
    tables = _dft_tables(seq, n2len)
    h = x
    for layer in range(depth):
        mod3 = _modulation(c, w_ada[layer], b_ada[layer]).reshape(bsz, 3, d)
        v, x1, x2g, yp = _input_projection(
            h, mod3, w_in[layer], b_in[layer], conv_w[layer], conv_b[layer], pool_w[layer],
            pool_scale[layer], n2len)
        taps_f, taps_r = _filter_taps(
            seq, n2len, filt_w1[layer], filt_b1[layer], filt_w_inner[layer],
            filt_b_inner[layer], filt_w_out[layer], filt_freq[layer], n_ch, order)
        kf = _filter_spectrum(taps_f, taps_r, tables[0], tables[1], tables[2], n2len)
        hbias = hyena_bias[layer].reshape(order, 1, n_ch)
        taps = [jnp.concatenate([conv_w[layer][:, k * n_ch:(k + 1) * n_ch],
                                 conv_b[layer][None, k * n_ch:(k + 1) * n_ch]], axis=0)
                for k in range(2)]
        z = _long_conv(v, x1, kf, hbias, 0, tables, in_taps=taps[0], mult_taps=taps[1],
                       pack_out=order > 1)
        for o in range(1, order):
            z = _long_conv(z, x2g, kf, hbias, o, tables, steps=CONV_STEPS_PACKED_IN)
        h = _output_projection(h, z, yp, mod3, w_out[layer], b_out[layer], ln_g[layer],
                               ln_b[layer], alpha)
    return h


def kernel(x, c, w_ada, b_ada, w_in, b_in, conv_w, conv_b, filt_w1, filt_b1, filt_w_inner,
           filt_b_inner, filt_w_out, filt_freq, hyena_bias, pool_w, pool_scale, w_out, b_out,
           ln_g, ln_b):
    return _forward(x, c, w_ada, b_ada, w_in, b_in, conv_w, conv_b, filt_w1, filt_b1,
                    filt_w_inner, filt_b_inner, filt_w_out, filt_freq, hyena_bias, pool_w,
                    pool_scale, w_out, b_out, ln_g, ln_b, n2len=MINOR_LEN)
```

```python
import functools
import math

import jax
import jax.numpy as jnp
import numpy as np
from jax import lax
from jax.experimental import pallas as pl
from jax.experimental.pallas import tpu as pltpu

F32 = jnp.float32
BF16 = jnp.bfloat16

POOL_WINDOWS = (2, 4, 8, 16)
LN_EPS = 1e-5
DECAY_TARGET = 1e-2
FAST_DECAY_PCT = 0.3
SLOW_DECAY_PCT = 1.5

LANES = 128
SUBLANES = 8
HALO = 16
BAND_ROWS = 128
BAND_SPAN = 256
MINOR_LEN = 128
FILTER_STEPS = (4, 4)
FILTER_COL_BLOCKS = 2
CONV_STEPS = (2, 2, 2)
V7X_VMEM_BYTES = 64 * 1024 * 1024
VMEM_LIMIT = V7X_VMEM_BYTES * 7 // 8
CONV_VMEM_LIMIT = V7X_VMEM_BYTES * 15 // 16


def _silu(x):
    h = 0.5 * x
    return h + h * jnp.tanh(h)


@functools.lru_cache(maxsize=None)
def _host_dft_tables(seq, n2len):
    n1h = seq // n2len
    n_fft = 2 * seq
    k1 = np.arange(n1h, dtype=np.int64)
    odd = 2 * k1 + 1
    alpha = ((odd[:, None] * k1[None, :]) % (4 * n1h)) * (2.0 * math.pi / (4 * n1h))
    n2e = np.arange(n2len + 1, dtype=np.int64)
    beta = ((n2e[:, None] * odd[None, :]) % (2 * n_fft)) * (2.0 * math.pi / (2 * n_fft))
    n2i = np.arange(n2len, dtype=np.int64)
    phi = ((n2i[:, None] * n2i[None, :]) % n2len) * (2.0 * math.pi / n2len)
    cm, sm = np.cos(phi), np.sin(phi)
    h2 = np.block([[cm, sm], [-sm, cm]])
    h2 = h2.reshape(2 * n2len, 2, n2len).transpose(0, 2, 1).reshape(2 * n2len, 2 * n2len)
    h2i = np.block([[cm, -sm], [sm, cm]])
    h2i = h2i.reshape(2, n2len, 2 * n2len).transpose(1, 0, 2).reshape(2 * n2len, 2 * n2len)
    h2, h2i = h2.astype(BF16), h2i.astype(BF16)
    quarter = 0.5 * math.pi * np.tile(np.array([0.0, 1.0]), n1h)[:, None]
    alpha2 = np.repeat(alpha, 2, axis=0)
    beta2 = np.repeat(beta, 2, axis=1)
    f32 = lambda a: a.astype(np.float32)
    seeds = tuple(f32(t) for t in (
        np.cos(alpha2 + quarter), np.sin(alpha2 + quarter),
        np.cos(alpha2 - quarter), np.sin(alpha2 - quarter), np.cos(beta2), np.sin(beta2)))
    return seeds + (h2, h2i)


def _dft_tables(seq, n2len):
    caf, saf, car, sar, cb, sb, h2, h2i = _host_dft_tables(seq, n2len)
    cb, sb = jnp.asarray(cb)[:, None, :], jnp.asarray(sb)[:, None, :]
    g_fwd = (caf.T[None] * cb[:-1] - saf.T[None] * sb[:-1]).astype(BF16)
    g_rev = (car.T[None] * cb[1:] - sar.T[None] * sb[1:]).astype(BF16)
    return g_fwd, g_rev, jnp.asarray(h2), jnp.asarray(h2i)


@functools.lru_cache(maxsize=None)
def _packed_pos_features(seq, emb, tile):
    hl = LANES // 2
    half = tile // 2
    bands = (emb - 1) // 2
    row = np.arange((seq // tile) * half, dtype=np.int64)[:, None]
    lane = np.arange(LANES, dtype=np.int64)[None, :]
    pos = ((row // half) * tile + (lane // hl) * half + row % half).astype(np.float64)
    feat = np.broadcast_to(lane % hl, pos.shape)
    t = pos / (seq - 1)
    w = (2.0 * math.pi / seq) * pos
    f = np.linspace(1e-4, bands - 1, bands)[(feat - 1) % bands]
    z = np.where(feat == 0, t,
                 np.where(feat <= bands, np.cos(f * w),
                          np.where(feat <= 2 * bands, -np.sin(f * w), 0.0)))
    return z.astype(np.float32)


def _mod_kernel(ct_ref, w_ref, b_ref, o_ref):
    s = _silu(ct_ref[...])
    w = w_ref[...]
    rows = [jnp.sum(s[:, b:b + 1] * w, axis=0, keepdims=True) for b in range(s.shape[1])]
    o_ref[...] = jnp.concatenate(rows, axis=0) + b_ref[...]


def _modulation(c, w_ada, b_ada):
    bsz, d = c.shape
    n_out = w_ada.shape[1]
    return pl.pallas_call(
        _mod_kernel,
        grid=(n_out // d,),
        in_specs=[pl.BlockSpec((d, bsz), lambda j: (0, 0)),
                  pl.BlockSpec((d, d), lambda j: (0, j)),
                  pl.BlockSpec((1, d), lambda j: (0, j))],
        out_specs=pl.BlockSpec((bsz, d), lambda j: (0, j)),
        out_shape=jax.ShapeDtypeStruct((bsz, n_out), F32),
        name="adaln_mod",
    )(c.T, w_ada, b_ada.reshape(1, n_out))


def _store_permuted(o_ref, val, n2len, first=0):
    for cb in range(val.shape[1] // LANES):
        for j in range(val.shape[0] // n2len):
            o_ref[cb, pl.ds(first + j, n2len, stride=SUBLANES), :] = val[
                j * n2len:(j + 1) * n2len, cb * LANES:(cb + 1) * LANES]


def _load_minor(x_ref, j):
    blk = x_ref[:, j * SUBLANES:(j + 1) * SUBLANES, :]
    return blk.reshape(blk.shape[0] * SUBLANES, blk.shape[2])


def _band_window(b, tile):
    last = tile + 2 * HALO - BAND_SPAN
    start = min(b * BAND_ROWS, last)
    return start, 0 if start == b * BAND_ROWS else 1


@functools.lru_cache(maxsize=None)
def _band_matrices(tile):
    t = np.arange(BAND_ROWS)[:, None]
    k = np.arange(BAND_SPAN)[None, :]
    mats = []
    for win in POOL_WINDOWS:
        half = win // 2
        per_variant = []
        for b in (0, tile // BAND_ROWS - 1):
            start, _ = _band_window(b, tile)
            rel = k + start - HALO - (b * BAND_ROWS + t)
            per_variant.append(((rel >= -half) & (rel < half)).astype(BF16))
        mats.append(np.stack(per_variant))
    return np.stack(mats)


def _proj_kernel(xp_ref, xc_ref, xn_ref, mod_ref, w_ref, b_ref, cw_ref, cb_ref, pw_ref,
                 ps_ref, band_ref, v_ref, x1_ref, x2_ref, yp_ref, p_scr, q_scr,
                 *, tile, n2len, ch, seq):
    i = pl.program_id(1)
    nt = pl.num_programs(1)
    shift = mod_ref[0:1, :]
    scale1 = 1.0 + mod_ref[1:2, :]
    xe = jnp.concatenate([xp_ref[...], xc_ref[...], xn_ref[...]], axis=0)
    ue = (xe * scale1 + shift).astype(BF16)
    uc = ue[HALO:HALO + tile, :]

    def zero_outside_sequence(p):
        return jnp.concatenate([jnp.where(i > 0, p[:HALO], 0.0), p[HALO:HALO + tile],
                                jnp.where(i < nt - 1, p[HALO + tile:], 0.0)], axis=0)

    for k, o_ref in enumerate((v_ref, x1_ref)):
        cols = slice(k * ch, (k + 1) * ch)
        p = jnp.dot(uc, w_ref[:, cols], preferred_element_type=F32) + b_ref[:, cols]
        _store_permuted(o_ref, p, n2len)

    cols = slice(2 * ch, 3 * ch)
    p = jnp.dot(ue, w_ref[:, cols], preferred_element_type=F32) + b_ref[:, cols]
    p_scr[...] = zero_outside_sequence(p)
    hg = jnp.dot(uc, w_ref[:, 3 * ch:4 * ch], preferred_element_type=F32) + b_ref[:, 3 * ch:4 * ch]
    gate = _silu(hg)
    s = (cb_ref[:, cols]
         + cw_ref[0:1, cols] * p_scr[HALO - 1:HALO - 1 + tile, :]
         + cw_ref[1:2, cols] * p_scr[HALO:HALO + tile, :]
         + cw_ref[2:3, cols] * p_scr[HALO + 1:HALO + 1 + tile, :])
    _store_permuted(x2_ref, s * gate, n2len)

    dp = ps_ref.shape[1]
    c0 = 4 * ch
    pin = jnp.dot(ue, w_ref[:, c0:c0 + dp], preferred_element_type=F32) + b_ref[:, c0:c0 + dp]
    pin = zero_outside_sequence(pin)
    q_scr[...] = pin.astype(BF16)
    p_scr[...] = pin
    pos = i * tile + lax.broadcasted_iota(jnp.int32, (tile, 1), 0)
    pg = dp // len(POOL_WINDOWS)
    groups = []
    for g, win in enumerate(POOL_WINDOWS):
        lanes = slice(g * pg, (g + 1) * pg)
        half = win // 2
        sums = []
        for b in range(tile // BAND_ROWS):
            start, variant = _band_window(b, tile)
            sums.append(jnp.dot(band_ref[g, variant], q_scr[start:start + BAND_SPAN, lanes],
                                preferred_element_type=F32))
        acc = jnp.concatenate(sums, axis=0)
        cnt = (jnp.minimum(pos + half, seq) - jnp.maximum(pos - half, 0)).astype(F32)
        diff = acc / cnt - p_scr[HALO:HALO + tile, lanes]
        groups.append(jnp.dot(diff.astype(BF16), pw_ref[g], preferred_element_type=F32))
    pgate = jnp.dot(uc, w_ref[:, c0 + dp:c0 + 2 * dp], preferred_element_type=F32) + b_ref[:, c0 + dp:c0 + 2 * dp]
    yp = jnp.concatenate(groups, axis=1) * ps_ref[...] * _silu(pgate)
    yp_ref[...] = yp.astype(yp_ref.dtype)


def _input_projection(x, mod3, w_in, b_in, conv_w, conv_b, pool_w, pool_scale, n2len):
    bsz, seq, d = x.shape
    ch = conv_w.shape[1] // 3
    dp = pool_scale.shape[0]
    n1h = seq // n2len
    tile = SUBLANES * n2len
    nt = seq // tile
    hb = tile // HALO
    n_proj = w_in.shape[1]
    cbn = ch // LANES
    assert max(POOL_WINDOWS) // 2 <= HALO and dp == ch and tile % BAND_ROWS == 0
    band = jnp.asarray(_band_matrices(tile))
    perm = jax.ShapeDtypeStruct((bsz, n1h // SUBLANES, cbn, tile, LANES), F32)
    perm_spec = pl.BlockSpec((None, None, cbn, tile, LANES), lambda b, i: (b, i, 0, 0, 0))
    full = lambda shape: pl.BlockSpec(shape, lambda b, i: (0,) * len(shape))
    kern = functools.partial(_proj_kernel, tile=tile, n2len=n2len, ch=ch, seq=seq)
    return pl.pallas_call(
        kern,
        grid=(bsz, nt),
        in_specs=[
            pl.BlockSpec((None, HALO, d), lambda b, i: (b, jnp.maximum(i * hb - 1, 0), 0)),
            pl.BlockSpec((None, tile, d), lambda b, i: (b, i, 0)),
            pl.BlockSpec((None, HALO, d), lambda b, i: (b, jnp.minimum((i + 1) * hb, seq // HALO - 1), 0)),
            pl.BlockSpec((None, 3, d), lambda b, i: (b, 0, 0)),
            full((d, n_proj)), full((1, n_proj)), full((3, 3 * ch)), full((1, 3 * ch)),
            full(pool_w.shape), full((1, dp)), full(band.shape),
        ],
        out_specs=[perm_spec, perm_spec, perm_spec,
                   pl.BlockSpec((None, tile, dp), lambda b, i: (b, i, 0))],
        out_shape=[perm, perm, perm, jax.ShapeDtypeStruct((bsz, seq, dp), BF16)],
        scratch_shapes=[pltpu.VMEM((tile + 2 * HALO, ch), F32),
                        pltpu.VMEM((tile + 2 * HALO, dp), BF16)],
        compiler_params=pltpu.CompilerParams(
            dimension_semantics=("parallel", "arbitrary"), vmem_limit_bytes=VMEM_LIMIT),
        name="in_proj_conv_pool",
    )(x, x, x, mod3, w_in.astype(BF16), b_in.reshape(1, n_proj), conv_w,
      conv_b.reshape(1, 3 * ch), pool_w.astype(BF16), pool_scale.reshape(1, dp), band)


def _filter_mlp_kernel(z_ref, w1_ref, b1_ref, wi_ref, bi_ref, fr_ref, wo_ref, ad_ref,
                       of_ref, or_ref, *, n2len):
    z = z_ref[...]
    fr = fr_ref[...]
    h = jnp.sin(fr * (jnp.dot(z.astype(BF16), w1_ref[...],
                              preferred_element_type=F32) + b1_ref[...]))
    for l in range(wi_ref.shape[0]):
        h = jnp.sin(fr * (jnp.dot(h.astype(BF16), wi_ref[l],
                                  preferred_element_type=F32) + bi_ref[l:l + 1, :]))
    hb = h.astype(BF16)
    half_lanes = LANES // 2
    n1_half = z.shape[0] // n2len
    for half in range(2):
        t = z[:, half * half_lanes:half * half_lanes + 1]
        decay = jnp.exp(-t * ad_ref[...])
        for d, o_ref in enumerate((of_ref, or_ref)):
            k = jnp.dot(hb, wo_ref[d, half], preferred_element_type=F32)
            _store_permuted(o_ref, k * decay, n2len, first=half * n1_half)


def _block_diag2(w):
    zero = jnp.zeros_like(w)
    return jnp.concatenate([jnp.concatenate([w, zero], axis=-1),
                            jnp.concatenate([zero, w], axis=-1)], axis=-2)


def _filter_taps(seq, n2len, w1, b1, w_inner, b_inner, w_out, freq, n_ch, order):
    emb, hid = w1.shape
    n_inner = w_inner.shape[0]
    oc = order * n_ch
    hl = LANES // 2
    assert emb <= hl and hid <= hl
    tile = SUBLANES * n2len
    half = tile // 2
    groups = seq // tile
    zp = jnp.asarray(_packed_pos_features(seq, emb, tile))
    ph = hl - hid
    w1d = _block_diag2(jnp.pad(w1, ((0, hl - emb), (0, ph))))
    b1d = jnp.tile(jnp.pad(b1, (0, ph)), 2).reshape(1, LANES)
    wid = _block_diag2(jnp.pad(w_inner, ((0, 0), (0, ph), (0, ph))))
    bid = jnp.tile(jnp.pad(b_inner, ((0, 0), (0, ph))), (1, 2))
    frd = jnp.tile(jnp.pad(freq, (0, ph), constant_values=1.0), 2).reshape(1, LANES)
    wo = jnp.transpose(w_out.reshape(hid, order, 2, n_ch), (2, 0, 1, 3)).reshape(2, hid, oc)
    wo = jnp.pad(wo, ((0, 0), (0, ph), (0, 0)))
    zero = jnp.zeros_like(wo)
    wo4 = jnp.stack([jnp.concatenate([wo, zero], axis=1),
                     jnp.concatenate([zero, wo], axis=1)], axis=1)
    min_decay = math.log(DECAY_TARGET) / SLOW_DECAY_PCT
    max_decay = math.log(DECAY_TARGET) / FAST_DECAY_PCT
    absdelta = jnp.abs(jnp.linspace(min_decay, max_decay, n_ch, dtype=F32))
    absdelta = jnp.tile(absdelta, order).reshape(1, oc)

    full = lambda shape: pl.BlockSpec(shape, lambda i: (0,) * len(shape))
    kern = functools.partial(_filter_mlp_kernel, n2len=n2len)
    taps = jax.ShapeDtypeStruct((groups, oc // LANES, tile, LANES), F32)
    taps_spec = pl.BlockSpec((None, oc // LANES, tile, LANES), lambda i: (i, 0, 0, 0))
    return pl.pallas_call(
        kern,
        grid=(groups,),
        in_specs=[
            pl.BlockSpec((half, LANES), lambda i: (i, 0)),
            full((LANES, LANES)), full((1, LANES)), full((n_inner, LANES, LANES)),
            full((n_inner, LANES)), full((1, LANES)), full((2, 2, LANES, oc)), full((1, oc)),
        ],
        out_specs=[taps_spec, taps_spec],
        out_shape=[taps, taps],
        compiler_params=pltpu.CompilerParams(
            dimension_semantics=("arbitrary",), vmem_limit_bytes=VMEM_LIMIT),
        name="filter_mlp",
    )(zp, w1d.astype(BF16), b1d, wid.astype(BF16), bid, frd, wo4.astype(BF16), absdelta)


def _resident(shape):
    return pl.BlockSpec(shape, lambda *_: (0,) * len(shape), pipeline_mode=pl.Buffered(1))


def _dot_rows(a, b):
    return lax.dot_general(a, b, (((0,), (0,)), ((), ())), preferred_element_type=F32)


def _pack_pairs(x):
    return pltpu.bitcast(x.astype(BF16), jnp.uint32)


def _unpack_pairs(w):
    return pltpu.bitcast(w, BF16)


def _spectrum_view(a_scr, chunk, ck1, pitch):
    start = pl.multiple_of(chunk * (ck1 * pitch), SUBLANES)
    return a_scr.at[pl.ds(start, ck1 * pitch)]


def _spectrum_pair(view, j, n2len, pitch):
    return jnp.concatenate([_unpack_pairs(view[jj * pitch:jj * pitch + n2len, :])
                            for jj in (j, j + 1)], axis=1)


def _filter_fft_kernel(hf_ref, hr_ref, g_ref, gr_ref, h2_ref, o_ref, a_scr,
                       *, ns, cn2, ck1, n1h, n2len, pitch):
    s = pl.program_id(1)
    ncb = a_scr.shape[0]

    @pl.when(s < ns)
    def _():
        for j in range(cn2):
            jr = cn2 - 1 - j
            m = s * cn2 + j
            g = jnp.concatenate([g_ref[m], gr_ref[n2len - 1 - m]], axis=0)
            x = jnp.concatenate(
                [jnp.concatenate([_load_minor(hf_ref.at[:, cb], j),
                                  _load_minor(hr_ref.at[:, cb], jr)], axis=0)
                 for cb in range(ncb)], axis=1)
            packed = _pack_pairs(_dot_rows(g, x.astype(BF16)))
            for cb in range(ncb):
                a_scr[cb, pl.ds(m, n1h, stride=pitch), :] = packed[:, cb * LANES:(cb + 1) * LANES]

    @pl.when(s >= ns)
    def _():
        for cb in range(ncb):
            view = _spectrum_view(a_scr.at[cb], s - ns, ck1, pitch)
            lanes = slice(cb * LANES, (cb + 1) * LANES)
            for j in range(0, ck1, 2):
                rows = _spectrum_pair(view, j, n2len, pitch)
                u = jnp.dot(h2_ref[...], rows, preferred_element_type=F32)
                u = (u * (1.0 / (n1h * n2len))).astype(o_ref.dtype)
                o_ref[j, :, lanes] = u[:, :LANES]
                o_ref[j + 1, :, lanes] = u[:, LANES:]


def _filter_spectrum(taps_f, taps_r, g_fwd, g_rev, h2, n2len):
    groups, ocb, _, _ = taps_f.shape
    oc = ocb * LANES
    n1h = groups * SUBLANES
    ns, ns2 = FILTER_STEPS
    cn2, ck1 = n2len // ns, n1h // ns2
    pitch = n2len + SUBLANES
    kern = functools.partial(_filter_fft_kernel, ns=ns, cn2=cn2, ck1=ck1, n1h=n1h,
                             n2len=n2len, pitch=pitch)
    ncb = FILTER_COL_BLOCKS
    return pl.pallas_call(
        kern,
        grid=(ocb // ncb, ns + ns2),
        in_specs=[
            pl.BlockSpec((groups, ncb, cn2 * SUBLANES, LANES),
                         lambda c, s: (0, c, jnp.minimum(s, ns - 1), 0)),
            pl.BlockSpec((groups, ncb, cn2 * SUBLANES, LANES),
                         lambda c, s: (0, c, jnp.maximum(ns - 1 - s, 0), 0)),
            _resident(g_fwd.shape), _resident(g_rev.shape), _resident(h2.shape),
        ],
        out_specs=pl.BlockSpec((ck1, 2 * n2len, ncb * LANES),
                               lambda c, s: (jnp.maximum(s - ns, 0), 0, c)),
        out_shape=jax.ShapeDtypeStruct((n1h, 2 * n2len, oc), BF16),
        scratch_shapes=[pltpu.VMEM((ncb, n1h * pitch, LANES), jnp.uint32)],
        compiler_params=pltpu.CompilerParams(
            dimension_semantics=("parallel", "arbitrary"), vmem_limit_bytes=CONV_VMEM_LIMIT),
        name="filter_spectrum",
    )(taps_f, taps_r, g_fwd, g_rev, h2)


def _short_conv_slab(x_ref, lo_ref, hi_ref, taps_ref, j, count, first_chunk, last_chunk):
    cur = _load_minor(x_ref, j)
    zero_row = jnp.zeros((1, cur.shape[1]), F32)
    if j > 0:
        prev = _load_minor(x_ref, j - 1)
    else:
        prev = _load_minor(lo_ref, 0)
        prev = jnp.where(first_chunk, jnp.concatenate([zero_row, prev[:-1]], axis=0), prev)
    if j < count - 1:
        nxt = _load_minor(x_ref, j + 1)
    else:
        nxt = _load_minor(hi_ref, 0)
        nxt = jnp.where(last_chunk, jnp.concatenate([nxt[1:], zero_row], axis=0), nxt)
    return (taps_ref[3:4, :] + taps_ref[0:1, :] * prev + taps_ref[1:2, :] * cur
            + taps_ref[2:3, :] * nxt)


def _long_conv_kernel(*refs, steps, n1h, n2len, pitch, conv_in, conv_mult):
    refs = list(refs)
    vf_ref = refs.pop(0)
    vlo_ref, vhi_ref, vtaps_ref = (refs.pop(0), refs.pop(0), refs.pop(0)) if conv_in else (None,) * 3
    g_ref, kf_ref, h2_ref, h2i_ref, xm_ref = (refs.pop(0) for _ in range(5))
    xlo_ref, xhi_ref, xtaps_ref = (refs.pop(0), refs.pop(0), refs.pop(0)) if conv_mult else (None,) * 3
    hb_ref, o_ref, a_scr, keep_scr = refs
    ns1, ns2, ns3 = steps
    cn2a, ck1, cn2 = n2len // ns1, n1h // ns2, n2len // ns3
    s = pl.program_id(2)

    @pl.when(s < ns1)
    def _():
        base = s * cn2a
        for j in range(cn2a):
            if conv_in:
                x = _short_conv_slab(vf_ref, vlo_ref, vhi_ref, vtaps_ref, j, cn2a,
                                     s == 0, s == ns1 - 1)
            elif vf_ref.dtype == jnp.uint32:
                x = _unpack_pairs(_load_minor(vf_ref, j)).astype(F32)
            else:
                x = _load_minor(vf_ref, j)
            keep_scr[base + j] = x
            r = _dot_rows(g_ref[base + j], x.astype(BF16))
            a_scr[pl.ds(base + j, n1h, stride=pitch), :] = _pack_pairs(r)

    @pl.when((s >= ns1) & (s < ns1 + ns2))
    def _():
        view = _spectrum_view(a_scr, s - ns1, ck1, pitch)
        for j in range(0, ck1, 2):
            rows = _spectrum_pair(view, j, n2len, pitch)
            u = jnp.dot(h2_ref[...], rows, preferred_element_type=F32)
            ur, ui = u[:n2len], u[n2len:]
            kf = jnp.concatenate([kf_ref[j], kf_ref[j + 1]], axis=1).astype(F32)
            kr, ki = kf[:n2len], kf[n2len:]
            y = jnp.concatenate([ur * kr - ui * ki, ur * ki + ui * kr], axis=0)
            bb = _pack_pairs(jnp.dot(h2i_ref[...], y.astype(BF16), preferred_element_type=F32))
            for jj in range(2):
                view[(j + jj) * pitch:(j + jj) * pitch + n2len, :] = bb[
                    :, jj * LANES:(jj + 1) * LANES]

    @pl.when(s >= ns1 + ns2)
    def _():
        base = (s - ns1 - ns2) * cn2
        for j in range(cn2):
            rows = _unpack_pairs(a_scr[pl.ds(base + j, n1h, stride=pitch), :])
            y = jnp.dot(g_ref[base + j], rows, preferred_element_type=F32)
            if conv_mult:
                xm = _short_conv_slab(xm_ref, xlo_ref, xhi_ref, xtaps_ref, j, cn2,
                                      s == ns1 + ns2, s == ns1 + ns2 + ns3 - 1)
            else:
                xm = _load_minor(xm_ref, j)
            z = xm * (y + hb_ref[...] * keep_scr[base + j])
            if o_ref.dtype == jnp.uint32:
                z = _pack_pairs(z)
            o_ref[:, j * SUBLANES:(j + 1) * SUBLANES, :] = z.reshape(
                z.shape[0] // SUBLANES, SUBLANES, z.shape[1])


def _long_conv(u_perm, mult_perm, kf, hbias, order_idx, tables, in_taps=None, mult_taps=None,
               pack_out=False):
    g_fwd, _, h2, h2i = tables
    bsz, _, cbn, _, _ = u_perm.shape
    n2len, n1h = g_fwd.shape[0], g_fwd.shape[1]
    steps = CONV_STEPS
    ns1, ns2, ns3 = steps
    cn2a, cn2 = n2len // ns1, n2len // ns3
    pitch = n2len + SUBLANES

    def early(s):
        return jnp.minimum(s, ns1 - 1)

    def late(s):
        return jnp.clip(s - ns1 - ns2, 0, ns3 - 1)

    def chunk_specs(chunk_of, width, groups):
        slab = (None, groups, None, SUBLANES, LANES)
        return [
            pl.BlockSpec((None, groups, None, width * SUBLANES, LANES),
                         lambda c, b, s: (b, 0, c, chunk_of(s), 0)),
            pl.BlockSpec(slab, lambda c, b, s: (b, 0, c, (chunk_of(s) * width - 1) % n2len, 0)),
            pl.BlockSpec(slab, lambda c, b, s: (b, 0, c, ((chunk_of(s) + 1) * width) % n2len, 0)),
        ]

    taps_spec = pl.BlockSpec((4, LANES), lambda c, b, s: (0, c))
    in_specs, args = [], []
    specs = chunk_specs(early, cn2a, u_perm.shape[1])
    if in_taps is None:
        in_specs += specs[:1]
        args += [u_perm]
    else:
        in_specs += specs + [taps_spec]
        args += [u_perm, u_perm, u_perm, in_taps]
    in_specs += [
        _resident(g_fwd.shape),
        pl.BlockSpec((n1h // ns2, 2 * n2len, LANES),
                     lambda c, b, s: (jnp.clip(s - ns1, 0, ns2 - 1), 0, order_idx * cbn + c)),
        _resident(h2.shape), _resident(h2i.shape),
    ]
    args += [g_fwd, kf, h2, h2i]
    specs = chunk_specs(late, cn2, mult_perm.shape[1])
    if mult_taps is None:
        in_specs += specs[:1]
        args += [mult_perm]
    else:
        in_specs += specs + [taps_spec]
        args += [mult_perm, mult_perm, mult_perm, mult_taps]
    in_specs.append(pl.BlockSpec((None, 1, LANES), lambda c, b, s: (order_idx, 0, c)))
    args.append(hbias)
    out_groups = n1h // SUBLANES // (2 if pack_out else 1)
    out_spec = chunk_specs(late, cn2, out_groups)[0]
    out_shape = jax.ShapeDtypeStruct((bsz, out_groups, cbn, n2len * SUBLANES, LANES),
                                     jnp.uint32 if pack_out else F32)
    kern = functools.partial(_long_conv_kernel, steps=steps, n1h=n1h, n2len=n2len,
                             pitch=pitch, conv_in=in_taps is not None,
                             conv_mult=mult_taps is not None)
    return pl.pallas_call(
        kern,
        grid=(cbn, bsz, sum(steps)),
        in_specs=in_specs,
        out_specs=out_spec,
        out_shape=out_shape,
        scratch_shapes=[pltpu.VMEM((n1h * pitch, LANES), jnp.uint32),
                        pltpu.VMEM((n2len, n1h, LANES), F32)],
        compiler_params=pltpu.CompilerParams(
            dimension_semantics=("parallel", "parallel", "arbitrary"),
            vmem_limit_bytes=CONV_VMEM_LIMIT),
        name="long_conv",
    )(*args)


def _out_kernel(x_ref, yh_ref, yp_ref, mod_ref, w_ref, b_ref, g_ref, beta_ref, o_ref,
                *, n2len, alpha):
    cbn = yh_ref.shape[0]
    yh = jnp.concatenate(
        [jnp.concatenate([yh_ref[cb, pl.ds(j, n2len, stride=SUBLANES), :]
                          for cb in range(cbn)], axis=1)
         for j in range(SUBLANES)], axis=0)
    y = jnp.concatenate([yh.astype(BF16), yp_ref[...]], axis=1)
    acc = jnp.dot(y, w_ref[...], preferred_element_type=F32)
    h = alpha * x_ref[...] + mod_ref[2:3, :] * (acc + b_ref[...])
    mu = jnp.mean(h, axis=-1, keepdims=True)
    hc = h - mu
    var = jnp.mean(hc * hc, axis=-1, keepdims=True)
    o_ref[...] = hc * lax.rsqrt(var + LN_EPS) * g_ref[...] + beta_ref[...]


def _output_projection(x, yh_perm, yp, mod3, w_out, b_out, ln_g, ln_b, alpha):
    bsz, seq, d = x.shape
    _, _, cbn, tile, _ = yh_perm.shape
    ch = cbn * LANES
    n2len = tile // SUBLANES
    dp = yp.shape[2]
    full = lambda shape: pl.BlockSpec(shape, lambda b, i: (0,) * len(shape))
    kern = functools.partial(_out_kernel, n2len=n2len, alpha=alpha)
    return pl.pallas_call(
        kern,
        grid=(bsz, seq // tile),
        in_specs=[
            pl.BlockSpec((None, tile, d), lambda b, i: (b, i, 0)),
            pl.BlockSpec((None, None, cbn, tile, LANES), lambda b, i: (b, i, 0, 0, 0)),
            pl.BlockSpec((None, tile, dp), lambda b, i: (b, i, 0)),
            pl.BlockSpec((None, 3, d), lambda b, i: (b, 0, 0)),
            full((ch + dp, d)), full((1, d)), full((1, d)), full((1, d)),
        ],
        out_specs=pl.BlockSpec((None, tile, d), lambda b, i: (b, i, 0)),
        out_shape=jax.ShapeDtypeStruct(x.shape, x.dtype),
        compiler_params=pltpu.CompilerParams(
            dimension_semantics=("parallel", "arbitrary"), vmem_limit_bytes=VMEM_LIMIT),
        name="out_proj_deepnorm",
    )(x, yh_perm, yp, mod3, w_out.astype(BF16), b_out.reshape(1, d), ln_g.reshape(1, d),
      ln_b.reshape(1, d))


def _forward(x, c, w_ada, b_ada, w_in, b_in, conv_w, conv_b, filt_w1, filt_b1, filt_w_inner,
             filt_b_inner, filt_w_out, filt_freq, hyena_bias, pool_w, pool_scale, w_out,
             b_out, ln_g, ln_b, *, n2len):
    bsz, seq, d = x.shape
    depth = w_ada.shape[0]
    order, n_ch = hyena_bias.shape[1], hyena_bias.shape[2]
    alpha = (2.0 * depth) ** 0.25
    tables = _dft_tables(seq, n2len)
    h = x
    for layer in range(depth):
        mod3 = _modulation(c, w_ada[layer], b_ada[layer]).reshape(bsz, 3, d)
        v, x1, x2g, yp = _input_projection(
            h, mod3, w_in[layer], b_in[layer], conv_w[layer], conv_b[layer], pool_w[layer],
            pool_scale[layer], n2len)
        taps_f, taps_r = _filter_taps(
            seq, n2len, filt_w1[layer], filt_b1[layer], filt_w_inner[layer],
            filt_b_inner[layer], filt_w_out[layer], filt_freq[layer], n_ch, order)
        kf = _filter_spectrum(taps_f, taps_r, tables[0], tables[1], tables[2], n2len)
        hbias = hyena_bias[layer].reshape(order, 1, n_ch)
        taps = [jnp.concatenate([conv_w[layer][:, k * n_ch:(k + 1) * n_ch],
                                 conv_b[layer][None, k * n_ch:(k + 1) * n_ch]], axis=0)
                for k in range(2)]
        z = _long_conv(v, x1, kf, hbias, 0, tables, in_taps=taps[0], mult_taps=taps[1],
                       pack_out=order > 1)
        for o in range(1, order):
            z = _long_conv(z, x2g, kf, hbias, o, tables)
        h = _output_projection(h, z, yp, mod3, w_out[layer], b_out[layer], ln_g[layer],
                               ln_b[layer], alpha)
    return h


def kernel(x, c, w_ada, b_ada, w_in, b_in, conv_w, conv_b, filt_w1, filt_b1, filt_w_inner,
           filt_b_inner, filt_w_out, filt_freq, hyena_bias, pool_w, pool_scale, w_out, b_out,
           ln_g, ln_b):
    return _forward(x, c, w_ada, b_ada, w_in, b_in, conv_w, conv_b, filt_w1, filt_b1,
                    filt_w_inner, filt_b_inner, filt_w_out, filt_freq, hyena_bias, pool_w,
                    pool_scale, w_out, b_out, ln_g, ln_b, n2len=MINOR_LEN)
```

```python
import functools
import math

import jax
import jax.numpy as jnp
import numpy as np
from jax import lax
from jax.experimental import pallas as pl
from jax.experimental.pallas import tpu as pltpu

F32 = jnp.float32
BF16 = jnp.bfloat16
HIGHEST = lax.Precision.HIGHEST

POOL_WINDOWS = (2, 4, 8, 16)
LN_EPS = 1e-5
DECAY_TARGET = 1e-2
FAST_DECAY_PCT = 0.3
SLOW_DECAY_PCT = 1.5

LANES = 128
SUBLANES = 8
HALO = 16
BAND_ROWS = 128
BAND_SPAN = 256
MINOR_LEN = 128
FILTER_STEPS = (4, 4)
FILTER_COL_BLOCKS = 2
CONV_STEPS = (2, 2, 2)
V7X_VMEM_BYTES = 64 * 1024 * 1024
VMEM_LIMIT = V7X_VMEM_BYTES * 7 // 8
CONV_VMEM_LIMIT = V7X_VMEM_BYTES * 15 // 16


def _silu(x):
    h = 0.5 * x
    return h + h * jnp.tanh(h)


@functools.lru_cache(maxsize=None)
def _host_dft_tables(seq, n2len):
    n1h = seq // n2len
    n_fft = 2 * seq
    k1 = np.arange(n1h, dtype=np.int64)
    odd = 2 * k1 + 1
    alpha = ((odd[:, None] * k1[None, :]) % (4 * n1h)) * (2.0 * math.pi / (4 * n1h))
    n2e = np.arange(n2len + 1, dtype=np.int64)
    beta = ((n2e[:, None] * odd[None, :]) % (2 * n_fft)) * (2.0 * math.pi / (2 * n_fft))
    n2i = np.arange(n2len, dtype=np.int64)
    phi = ((n2i[:, None] * n2i[None, :]) % n2len) * (2.0 * math.pi / n2len)
    cm, sm = np.cos(phi), np.sin(phi)
    h2 = np.block([[cm, sm], [-sm, cm]])
    h2 = h2.reshape(2 * n2len, 2, n2len).transpose(0, 2, 1).reshape(2 * n2len, 2 * n2len)
    h2i = np.block([[cm, -sm], [sm, cm]])
    h2i = h2i.reshape(2, n2len, 2 * n2len).transpose(1, 0, 2).reshape(2 * n2len, 2 * n2len)
    h2, h2i = h2.astype(BF16), h2i.astype(BF16)
    quarter = 0.5 * math.pi * np.tile(np.array([0.0, 1.0]), n1h)[:, None]
    alpha2 = np.repeat(alpha, 2, axis=0)
    beta2 = np.repeat(beta, 2, axis=1)
    f32 = lambda a: a.astype(np.float32)
    seeds = tuple(f32(t) for t in (
        np.cos(alpha2 + quarter), np.sin(alpha2 + quarter),
        np.cos(alpha2 - quarter), np.sin(alpha2 - quarter), np.cos(beta2), np.sin(beta2)))
    return seeds + (h2, h2i)


def _dft_tables(seq, n2len):
    caf, saf, car, sar, cb, sb, h2, h2i = _host_dft_tables(seq, n2len)
    cb, sb = jnp.asarray(cb)[:, None, :], jnp.asarray(sb)[:, None, :]
    g_fwd = (caf.T[None] * cb[:-1] - saf.T[None] * sb[:-1]).astype(BF16)
    g_rev = (car.T[None] * cb[1:] - sar.T[None] * sb[1:]).astype(BF16)
    return g_fwd, g_rev, jnp.asarray(h2), jnp.asarray(h2i)


@functools.lru_cache(maxsize=None)
def _packed_pos_features(seq, emb, n2len):
    hl = LANES // 2
    step_rows = SUBLANES * n2len
    bands = (emb - 1) // 2
    row = np.arange(seq // 2, dtype=np.int64)[:, None]
    lane = np.arange(LANES, dtype=np.int64)[None, :]
    local = row % step_rows
    n1_local = 8 * (lane // hl) + 2 * (local // 2 // n2len) + local % 2
    pos = ((row // step_rows) * 2 * step_rows + n1_local * n2len
           + (local // 2) % n2len).astype(np.float64)
    feat = np.broadcast_to(lane % hl, pos.shape)
    t = pos / (seq - 1)
    w = (2.0 * math.pi / seq) * pos
    f = np.linspace(1e-4, bands - 1, bands)[(feat - 1) % bands]
    z = np.where(feat == 0, t,
                 np.where(feat <= bands, np.cos(f * w),
                          np.where(feat <= 2 * bands, -np.sin(f * w), 0.0)))
    return z.astype(np.float32)


def _mod_kernel(c_ref, w_ref, b_ref, o_ref):
    s = _silu(c_ref[...])
    o_ref[...] = jnp.dot(s, w_ref[...], precision=HIGHEST,
                         preferred_element_type=F32) + b_ref[...]


def _modulation(c, w_ada, b_ada):
    bsz, d = c.shape
    n_out = w_ada.shape[1]
    return pl.pallas_call(
        _mod_kernel,
        grid=(n_out // d,),
        in_specs=[pl.BlockSpec((bsz, d), lambda j: (0, 0)),
                  pl.BlockSpec((d, d), lambda j: (0, j)),
                  pl.BlockSpec((1, d), lambda j: (0, j))],
        out_specs=pl.BlockSpec((bsz, d), lambda j: (0, j)),
        out_shape=jax.ShapeDtypeStruct((bsz, n_out), F32),
        name="adaln_mod",
    )(c, w_ada, b_ada.reshape(1, n_out))


def _store_permuted(o_ref, val, n2len, first=0):
    for cb in range(val.shape[1] // LANES):
        for j in range(val.shape[0] // n2len):
            o_ref[cb, pl.ds(first + j, n2len, stride=SUBLANES), :] = val[
                j * n2len:(j + 1) * n2len, cb * LANES:(cb + 1) * LANES]


def _load_minor(x_ref, j):
    blk = x_ref[:, j * SUBLANES:(j + 1) * SUBLANES, :]
    return blk.reshape(blk.shape[0] * SUBLANES, blk.shape[2])


def _band_window(b, tile):
    last = tile + 2 * HALO - BAND_SPAN
    start = min(b * BAND_ROWS, last)
    return start, 0 if start == b * BAND_ROWS else 1


@functools.lru_cache(maxsize=None)
def _band_matrices(tile):
    t = np.arange(BAND_ROWS)[:, None]
    k = np.arange(BAND_SPAN)[None, :]
    mats = []
    for win in POOL_WINDOWS:
        half = win // 2
        per_variant = []
        for b in (0, tile // BAND_ROWS - 1):
            start, _ = _band_window(b, tile)
            rel = k + start - HALO - (b * BAND_ROWS + t)
            per_variant.append(((rel >= -half) & (rel < half)).astype(BF16))
        mats.append(np.stack(per_variant))
    return np.stack(mats)


def _proj_kernel(xp_ref, xc_ref, xn_ref, mod_ref, w_ref, b_ref, cw_ref, cb_ref, pw_ref,
                 ps_ref, band_ref, v_ref, x1_ref, x2_ref, yp_ref, p_scr, q_scr,
                 *, tile, n2len, ch, seq):
    i = pl.program_id(1)
    nt = pl.num_programs(1)
    shift = mod_ref[0:1, :]
    scale1 = 1.0 + mod_ref[1:2, :]
    xe = jnp.concatenate([xp_ref[...], xc_ref[...], xn_ref[...]], axis=0)
    ue = (xe * scale1 + shift).astype(BF16)
    uc = ue[HALO:HALO + tile, :]

    def zero_outside_sequence(p):
        return jnp.concatenate([jnp.where(i > 0, p[:HALO], 0.0), p[HALO:HALO + tile],
                                jnp.where(i < nt - 1, p[HALO + tile:], 0.0)], axis=0)

    for k, o_ref in enumerate((v_ref, x1_ref)):
        cols = slice(k * ch, (k + 1) * ch)
        p = jnp.dot(uc, w_ref[:, cols], preferred_element_type=F32) + b_ref[:, cols]
        _store_permuted(o_ref, p, n2len)

    cols = slice(2 * ch, 3 * ch)
    p = jnp.dot(ue, w_ref[:, cols], preferred_element_type=F32) + b_ref[:, cols]
    p_scr[...] = zero_outside_sequence(p)
    hg = jnp.dot(uc, w_ref[:, 3 * ch:4 * ch], preferred_element_type=F32) + b_ref[:, 3 * ch:4 * ch]
    gate = _silu(hg)
    s = (cb_ref[:, cols]
         + cw_ref[0:1, cols] * p_scr[HALO - 1:HALO - 1 + tile, :]
         + cw_ref[1:2, cols] * p_scr[HALO:HALO + tile, :]
         + cw_ref[2:3, cols] * p_scr[HALO + 1:HALO + 1 + tile, :])
    _store_permuted(x2_ref, s * gate, n2len)

    dp = ps_ref.shape[1]
    c0 = 4 * ch
    pin = jnp.dot(ue, w_ref[:, c0:c0 + dp], preferred_element_type=F32) + b_ref[:, c0:c0 + dp]
    pin = zero_outside_sequence(pin)
    q_scr[...] = pin.astype(BF16)
    p_scr[...] = pin
    pos = i * tile + lax.broadcasted_iota(jnp.int32, (tile, 1), 0)
    pg = dp // len(POOL_WINDOWS)
    groups = []
    for g, win in enumerate(POOL_WINDOWS):
        lanes = slice(g * pg, (g + 1) * pg)
        half = win // 2
        sums = []
        for b in range(tile // BAND_ROWS):
            start, variant = _band_window(b, tile)
            sums.append(jnp.dot(band_ref[g, variant], q_scr[start:start + BAND_SPAN, lanes],
                                preferred_element_type=F32))
        acc = jnp.concatenate(sums, axis=0)
        cnt = (jnp.minimum(pos + half, seq) - jnp.maximum(pos - half, 0)).astype(F32)
        diff = acc / cnt - p_scr[HALO:HALO + tile, lanes]
        groups.append(jnp.dot(diff.astype(BF16), pw_ref[g], preferred_element_type=F32))
    pgate = jnp.dot(uc, w_ref[:, c0 + dp:c0 + 2 * dp], preferred_element_type=F32) + b_ref[:, c0 + dp:c0 + 2 * dp]
    yp = jnp.concatenate(groups, axis=1) * ps_ref[...] * _silu(pgate)
    yp_ref[...] = yp.astype(yp_ref.dtype)


def _input_projection(x, mod3, w_in, b_in, conv_w, conv_b, pool_w, pool_scale, n2len):
    bsz, seq, d = x.shape
    ch = conv_w.shape[1] // 3
    dp = pool_scale.shape[0]
    n1h = seq // n2len
    tile = SUBLANES * n2len
    nt = seq // tile
    hb = tile // HALO
    n_proj = w_in.shape[1]
    cbn = ch // LANES
    assert max(POOL_WINDOWS) // 2 <= HALO and dp == ch and tile % BAND_ROWS == 0
    band = jnp.asarray(_band_matrices(tile))
    perm = jax.ShapeDtypeStruct((bsz, n1h // SUBLANES, cbn, tile, LANES), F32)
    perm_spec = pl.BlockSpec((None, None, cbn, tile, LANES), lambda b, i: (b, i, 0, 0, 0))
    full = lambda shape: pl.BlockSpec(shape, lambda b, i: (0,) * len(shape))
    kern = functools.partial(_proj_kernel, tile=tile, n2len=n2len, ch=ch, seq=seq)
    return pl.pallas_call(
        kern,
        grid=(bsz, nt),
        in_specs=[
            pl.BlockSpec((None, HALO, d), lambda b, i: (b, jnp.maximum(i * hb - 1, 0), 0)),
            pl.BlockSpec((None, tile, d), lambda b, i: (b, i, 0)),
            pl.BlockSpec((None, HALO, d), lambda b, i: (b, jnp.minimum((i + 1) * hb, seq // HALO - 1), 0)),
            pl.BlockSpec((None, 3, d), lambda b, i: (b, 0, 0)),
            full((d, n_proj)), full((1, n_proj)), full((3, 3 * ch)), full((1, 3 * ch)),
            full(pool_w.shape), full((1, dp)), full(band.shape),
        ],
        out_specs=[perm_spec, perm_spec, perm_spec,
                   pl.BlockSpec((None, tile, dp), lambda b, i: (b, i, 0))],
        out_shape=[perm, perm, perm, jax.ShapeDtypeStruct((bsz, seq, dp), BF16)],
        scratch_shapes=[pltpu.VMEM((tile + 2 * HALO, ch), F32),
                        pltpu.VMEM((tile + 2 * HALO, dp), BF16)],
        compiler_params=pltpu.CompilerParams(
            dimension_semantics=("parallel", "arbitrary"), vmem_limit_bytes=VMEM_LIMIT),
        name="in_proj_conv_pool",
    )(x, x, x, mod3, w_in.astype(BF16), b_in.reshape(1, n_proj), conv_w,
      conv_b.reshape(1, 3 * ch), pool_w.astype(BF16), pool_scale.reshape(1, dp), band)


def _filter_mlp_kernel(z_ref, w1_ref, b1_ref, wi_ref, bi_ref, fr_ref, wo_ref, ad_ref,
                       of_ref, or_ref, *, n2len):
    z = z_ref[...]
    fr = fr_ref[...]
    h = jnp.sin(fr * (jnp.dot(z.astype(BF16), w1_ref[...],
                              preferred_element_type=F32) + b1_ref[...]))
    for l in range(wi_ref.shape[0]):
        h = jnp.sin(fr * (jnp.dot(h.astype(BF16), wi_ref[l],
                                  preferred_element_type=F32) + bi_ref[l:l + 1, :]))
    hb = h.astype(BF16)
    half_lanes = LANES // 2
    pairs_per_half = z.shape[0] // 2 // n2len
    for half in range(2):
        t = z[:, half * half_lanes:half * half_lanes + 1]
        decay = jnp.exp(-t * ad_ref[...])
        for d, o_ref in enumerate((of_ref, or_ref)):
            k = jnp.dot(hb, wo_ref[d, half], preferred_element_type=F32)
            words = _pack_pairs(k * decay)
            _store_permuted(o_ref, words, n2len, first=half * pairs_per_half)


def _block_diag2(w):
    zero = jnp.zeros_like(w)
    return jnp.concatenate([jnp.concatenate([w, zero], axis=-1),
                            jnp.concatenate([zero, w], axis=-1)], axis=-2)


def _filter_taps(seq, n2len, w1, b1, w_inner, b_inner, w_out, freq, n_ch, order):
    emb, hid = w1.shape
    n_inner = w_inner.shape[0]
    oc = order * n_ch
    hl = LANES // 2
    assert emb <= hl and hid <= hl
    tile = SUBLANES * n2len
    groups = seq // (2 * tile)
    zp = jnp.asarray(_packed_pos_features(seq, emb, n2len))
    ph = hl - hid
    w1d = _block_diag2(jnp.pad(w1, ((0, hl - emb), (0, ph))))
    b1d = jnp.tile(jnp.pad(b1, (0, ph)), 2).reshape(1, LANES)
    wid = _block_diag2(jnp.pad(w_inner, ((0, 0), (0, ph), (0, ph))))
    bid = jnp.tile(jnp.pad(b_inner, ((0, 0), (0, ph))), (1, 2))
    frd = jnp.tile(jnp.pad(freq, (0, ph), constant_values=1.0), 2).reshape(1, LANES)
    wo = jnp.transpose(w_out.reshape(hid, order, 2, n_ch), (2, 0, 1, 3)).reshape(2, hid, oc)
    wo = jnp.pad(wo, ((0, 0), (0, ph), (0, 0)))
    zero = jnp.zeros_like(wo)
    wo4 = jnp.stack([jnp.concatenate([wo, zero], axis=1),
                     jnp.concatenate([zero, wo], axis=1)], axis=1)
    min_decay = math.log(DECAY_TARGET) / SLOW_DECAY_PCT
    max_decay = math.log(DECAY_TARGET) / FAST_DECAY_PCT
    absdelta = jnp.abs(jnp.linspace(min_decay, max_decay, n_ch, dtype=F32))
    absdelta = jnp.tile(absdelta, order).reshape(1, oc)

    full = lambda shape: pl.BlockSpec(shape, lambda i: (0,) * len(shape))
    kern = functools.partial(_filter_mlp_kernel, n2len=n2len)
    taps = jax.ShapeDtypeStruct((groups, oc // LANES, tile, LANES), jnp.uint32)
    taps_spec = pl.BlockSpec((None, oc // LANES, tile, LANES), lambda i: (i, 0, 0, 0))
    return pl.pallas_call(
        kern,
        grid=(groups,),
        in_specs=[
            pl.BlockSpec((tile, LANES), lambda i: (i, 0)),
            full((LANES, LANES)), full((1, LANES)), full((n_inner, LANES, LANES)),
            full((n_inner, LANES)), full((1, LANES)), full((2, 2, LANES, oc)), full((1, oc)),
        ],
        out_specs=[taps_spec, taps_spec],
        out_shape=[taps, taps],
        compiler_params=pltpu.CompilerParams(
            dimension_semantics=("arbitrary",), vmem_limit_bytes=VMEM_LIMIT),
        name="filter_mlp",
    )(zp, w1d.astype(BF16), b1d, wid.astype(BF16), bid, frd, wo4.astype(BF16), absdelta)


def _resident(shape):
    return pl.BlockSpec(shape, lambda *_: (0,) * len(shape), pipeline_mode=pl.Buffered(1))


def _dot_rows(a, b):
    return lax.dot_general(a, b, (((0,), (0,)), ((), ())), preferred_element_type=F32)


def _pack_pairs(x):
    return pltpu.bitcast(x.astype(BF16), jnp.uint32)


def _unpack_pairs(w):
    return pltpu.bitcast(w, BF16)


def _spectrum_view(a_scr, chunk, ck1, pitch):
    start = pl.multiple_of(chunk * (ck1 * pitch), SUBLANES)
    return a_scr.at[pl.ds(start, ck1 * pitch)]


def _spectrum_pair(view, j, n2len, pitch):
    return jnp.concatenate([_unpack_pairs(view[jj * pitch:jj * pitch + n2len, :])
                            for jj in (j, j + 1)], axis=1)


def _filter_fft_kernel(hf_ref, hr_ref, g_ref, gr_ref, h2_ref, o_ref, a_scr,
                       *, ns, cn2, ck1, n1h, n2len, pitch):
    s = pl.program_id(1)
    ncb = a_scr.shape[0]

    @pl.when(s < ns)
    def _():
        for j in range(cn2):
            jr = cn2 - 1 - j
            m = s * cn2 + j
            g = jnp.concatenate([g_ref[m], gr_ref[n2len - 1 - m]], axis=0)
            x = jnp.concatenate(
                [jnp.concatenate([_unpack_pairs(_load_minor(hf_ref.at[:, cb], j)),
                                  _unpack_pairs(_load_minor(hr_ref.at[:, cb], jr))], axis=0)
                 for cb in range(ncb)], axis=1)
            packed = _pack_pairs(_dot_rows(g, x))
            for cb in range(ncb):
                a_scr[cb, pl.ds(m, n1h, stride=pitch), :] = packed[:, cb * LANES:(cb + 1) * LANES]

    @pl.when(s >= ns)
    def _():
        for cb in range(ncb):
            view = _spectrum_view(a_scr.at[cb], s - ns, ck1, pitch)
            lanes = slice(cb * LANES, (cb + 1) * LANES)
            for j in range(0, ck1, 2):
                rows = _spectrum_pair(view, j, n2len, pitch)
                u = jnp.dot(h2_ref[...], rows, preferred_element_type=F32)
                u = (u * (1.0 / (n1h * n2len))).astype(o_ref.dtype)
                o_ref[j, :, lanes] = u[:, :LANES]
                o_ref[j + 1, :, lanes] = u[:, LANES:]


def _filter_spectrum(taps_f, taps_r, g_fwd, g_rev, h2, n2len):
    groups, ocb, _, _ = taps_f.shape
    oc = ocb * LANES
    n1h = groups * 2 * SUBLANES
    ns, ns2 = FILTER_STEPS
    cn2, ck1 = n2len // ns, n1h // ns2
    pitch = n2len + SUBLANES
    kern = functools.partial(_filter_fft_kernel, ns=ns, cn2=cn2, ck1=ck1, n1h=n1h,
                             n2len=n2len, pitch=pitch)
    ncb = FILTER_COL_BLOCKS
    return pl.pallas_call(
        kern,
        grid=(ocb // ncb, ns + ns2),
        in_specs=[
            pl.BlockSpec((groups, ncb, cn2 * SUBLANES, LANES),
                         lambda c, s: (0, c, jnp.minimum(s, ns - 1), 0)),
            pl.BlockSpec((groups, ncb, cn2 * SUBLANES, LANES),
                         lambda c, s: (0, c, jnp.maximum(ns - 1 - s, 0), 0)),
            _resident(g_fwd.shape), _resident(g_rev.shape), _resident(h2.shape),
        ],
        out_specs=pl.BlockSpec((ck1, 2 * n2len, ncb * LANES),
                               lambda c, s: (jnp.maximum(s - ns, 0), 0, c)),
        out_shape=jax.ShapeDtypeStruct((n1h, 2 * n2len, oc), BF16),
        scratch_shapes=[pltpu.VMEM((ncb, n1h * pitch, LANES), jnp.uint32)],
        compiler_params=pltpu.CompilerParams(
            dimension_semantics=("parallel", "arbitrary"), vmem_limit_bytes=CONV_VMEM_LIMIT),
        name="filter_spectrum",
    )(taps_f, taps_r, g_fwd, g_rev, h2)


def _short_conv_slab(x_ref, lo_ref, hi_ref, taps_ref, j, count, first_chunk, last_chunk):
    cur = _load_minor(x_ref, j)
    zero_row = jnp.zeros((1, cur.shape[1]), F32)
    if j > 0:
        prev = _load_minor(x_ref, j - 1)
    else:
        prev = _load_minor(lo_ref, 0)
        prev = jnp.where(first_chunk, jnp.concatenate([zero_row, prev[:-1]], axis=0), prev)
    if j < count - 1:
        nxt = _load_minor(x_ref, j + 1)
    else:
        nxt = _load_minor(hi_ref, 0)
        nxt = jnp.where(last_chunk, jnp.concatenate([nxt[1:], zero_row], axis=0), nxt)
    return (taps_ref[3:4, :] + taps_ref[0:1, :] * prev + taps_ref[1:2, :] * cur
            + taps_ref[2:3, :] * nxt)


def _long_conv_kernel(*refs, steps, n1h, n2len, pitch, conv_in, conv_mult):
    refs = list(refs)
    vf_ref = refs.pop(0)
    vlo_ref, vhi_ref, vtaps_ref = (refs.pop(0), refs.pop(0), refs.pop(0)) if conv_in else (None,) * 3
    g_ref, kf_ref, h2_ref, h2i_ref, xm_ref = (refs.pop(0) for _ in range(5))
    xlo_ref, xhi_ref, xtaps_ref = (refs.pop(0), refs.pop(0), refs.pop(0)) if conv_mult else (None,) * 3
    hb_ref, o_ref, a_scr, keep_scr = refs
    ns1, ns2, ns3 = steps
    cn2a, ck1, cn2 = n2len // ns1, n1h // ns2, n2len // ns3
    s = pl.program_id(2)

    @pl.when(s < ns1)
    def _():
        base = s * cn2a
        for j in range(cn2a):
            if conv_in:
                x = _short_conv_slab(vf_ref, vlo_ref, vhi_ref, vtaps_ref, j, cn2a,
                                     s == 0, s == ns1 - 1)
            elif vf_ref.dtype == jnp.uint32:
                x = _unpack_pairs(_load_minor(vf_ref, j)).astype(F32)
            else:
                x = _load_minor(vf_ref, j)
            keep_scr[base + j] = x
            r = _dot_rows(g_ref[base + j], x.astype(BF16))
            a_scr[pl.ds(base + j, n1h, stride=pitch), :] = _pack_pairs(r)

    @pl.when((s >= ns1) & (s < ns1 + ns2))
    def _():
        view = _spectrum_view(a_scr, s - ns1, ck1, pitch)
        for j in range(0, ck1, 2):
            rows = _spectrum_pair(view, j, n2len, pitch)
            u = jnp.dot(h2_ref[...], rows, preferred_element_type=F32)
            ur, ui = u[:n2len], u[n2len:]
            kf = jnp.concatenate([kf_ref[j], kf_ref[j + 1]], axis=1).astype(F32)
            kr, ki = kf[:n2len], kf[n2len:]
            y = jnp.concatenate([ur * kr - ui * ki, ur * ki + ui * kr], axis=0)
            bb = _pack_pairs(jnp.dot(h2i_ref[...], y.astype(BF16), preferred_element_type=F32))
            for jj in range(2):
                view[(j + jj) * pitch:(j + jj) * pitch + n2len, :] = bb[
                    :, jj * LANES:(jj + 1) * LANES]

    @pl.when(s >= ns1 + ns2)
    def _():
        base = (s - ns1 - ns2) * cn2
        for j in range(cn2):
            rows = _unpack_pairs(a_scr[pl.ds(base + j, n1h, stride=pitch), :])
            y = jnp.dot(g_ref[base + j], rows, preferred_element_type=F32)
            if conv_mult:
                xm = _short_conv_slab(xm_ref, xlo_ref, xhi_ref, xtaps_ref, j, cn2,
                                      s == ns1 + ns2, s == ns1 + ns2 + ns3 - 1)
            else:
                xm = _load_minor(xm_ref, j)
            z = xm * (y + hb_ref[...] * keep_scr[base + j])
            if o_ref.dtype == jnp.uint32:
                z = _pack_pairs(z)
            o_ref[:, j * SUBLANES:(j + 1) * SUBLANES, :] = z.reshape(
                z.shape[0] // SUBLANES, SUBLANES, z.shape[1])


def _long_conv(u_perm, mult_perm, kf, hbias, order_idx, tables, in_taps=None, mult_taps=None,
               pack_out=False):
    g_fwd, _, h2, h2i = tables
    bsz, _, cbn, _, _ = u_perm.shape
    n2len, n1h = g_fwd.shape[0], g_fwd.shape[1]
    steps = CONV_STEPS
    ns1, ns2, ns3 = steps
    cn2a, cn2 = n2len // ns1, n2len // ns3
    pitch = n2len + SUBLANES

    def early(s):
        return jnp.minimum(s, ns1 - 1)

    def late(s):
        return jnp.clip(s - ns1 - ns2, 0, ns3 - 1)

    def chunk_specs(chunk_of, width, groups):
        slab = (None, groups, None, SUBLANES, LANES)
        return [
            pl.BlockSpec((None, groups, None, width * SUBLANES, LANES),
                         lambda c, b, s: (b, 0, c, chunk_of(s), 0)),
            pl.BlockSpec(slab, lambda c, b, s: (b, 0, c, (chunk_of(s) * width - 1) % n2len, 0)),
            pl.BlockSpec(slab, lambda c, b, s: (b, 0, c, ((chunk_of(s) + 1) * width) % n2len, 0)),
        ]

    taps_spec = pl.BlockSpec((4, LANES), lambda c, b, s: (0, c))
    in_specs, args = [], []
    specs = chunk_specs(early, cn2a, u_perm.shape[1])
    if in_taps is None:
        in_specs += specs[:1]
        args += [u_perm]
    else:
        in_specs += specs + [taps_spec]
        args += [u_perm, u_perm, u_perm, in_taps]
    in_specs += [
        _resident(g_fwd.shape),
        pl.BlockSpec((n1h // ns2, 2 * n2len, LANES),
                     lambda c, b, s: (jnp.clip(s - ns1, 0, ns2 - 1), 0, order_idx * cbn + c)),
        _resident(h2.shape), _resident(h2i.shape),
    ]
    args += [g_fwd, kf, h2, h2i]
    specs = chunk_specs(late, cn2, mult_perm.shape[1])
    if mult_taps is None:
        in_specs += specs[:1]
        args += [mult_perm]
    else:
        in_specs += specs + [taps_spec]
        args += [mult_perm, mult_perm, mult_perm, mult_taps]
    in_specs.append(pl.BlockSpec((None, 1, LANES), lambda c, b, s: (order_idx, 0, c)))
    args.append(hbias)
    out_groups = n1h // SUBLANES // (2 if pack_out else 1)
    out_spec = chunk_specs(late, cn2, out_groups)[0]
    out_shape = jax.ShapeDtypeStruct((bsz, out_groups, cbn, n2len * SUBLANES, LANES),
                                     jnp.uint32 if pack_out else F32)
    kern = functools.partial(_long_conv_kernel, steps=steps, n1h=n1h, n2len=n2len,
                             pitch=pitch, conv_in=in_taps is not None,
                             conv_mult=mult_taps is not None)
    return pl.pallas_call(
        kern,
        grid=(cbn, bsz, sum(steps)),
        in_specs=in_specs,
        out_specs=out_spec,
        out_shape=out_shape,
        scratch_shapes=[pltpu.VMEM((n1h * pitch, LANES), jnp.uint32),
                        pltpu.VMEM((n2len, n1h, LANES), F32)],
        compiler_params=pltpu.CompilerParams(
            dimension_semantics=("parallel", "parallel", "arbitrary"),
            vmem_limit_bytes=CONV_VMEM_LIMIT),
        name="long_conv",
    )(*args)


def _out_kernel(x_ref, yh_ref, yp_ref, mod_ref, w_ref, b_ref, g_ref, beta_ref, o_ref,
                *, n2len, alpha):
    cbn = yh_ref.shape[0]
    yh = jnp.concatenate(
        [jnp.concatenate([yh_ref[cb, pl.ds(j, n2len, stride=SUBLANES), :]
                          for cb in range(cbn)], axis=1)
         for j in range(SUBLANES)], axis=0)
    y = jnp.concatenate([yh.astype(BF16), yp_ref[...]], axis=1)
    acc = jnp.dot(y, w_ref[...], preferred_element_type=F32)
    h = alpha * x_ref[...] + mod_ref[2:3, :] * (acc + b_ref[...])
    mu = jnp.mean(h, axis=-1, keepdims=True)
    hc = h - mu
    var = jnp.mean(hc * hc, axis=-1, keepdims=True)
    o_ref[...] = hc * lax.rsqrt(var + LN_EPS) * g_ref[...] + beta_ref[...]


def _output_projection(x, yh_perm, yp, mod3, w_out, b_out, ln_g, ln_b, alpha):
    bsz, seq, d = x.shape
    _, _, cbn, tile, _ = yh_perm.shape
    ch = cbn * LANES
    n2len = tile // SUBLANES
    dp = yp.shape[2]
    full = lambda shape: pl.BlockSpec(shape, lambda b, i: (0,) * len(shape))
    kern = functools.partial(_out_kernel, n2len=n2len, alpha=alpha)
    return pl.pallas_call(
        kern,
        grid=(bsz, seq // tile),
        in_specs=[
            pl.BlockSpec((None, tile, d), lambda b, i: (b, i, 0)),
            pl.BlockSpec((None, None, cbn, tile, LANES), lambda b, i: (b, i, 0, 0, 0)),
            pl.BlockSpec((None, tile, dp), lambda b, i: (b, i, 0)),
            pl.BlockSpec((None, 3, d), lambda b, i: (b, 0, 0)),
            full((ch + dp, d)), full((1, d)), full((1, d)), full((1, d)),
        ],
        out_specs=pl.BlockSpec((None, tile, d), lambda b, i: (b, i, 0)),
        out_shape=jax.ShapeDtypeStruct(x.shape, x.dtype),
        compiler_params=pltpu.CompilerParams(
            dimension_semantics=("parallel", "arbitrary"), vmem_limit_bytes=VMEM_LIMIT),
        name="out_proj_deepnorm",
    )(x, yh_perm, yp, mod3, w_out.astype(BF16), b_out.reshape(1, d), ln_g.reshape(1, d),
      ln_b.reshape(1, d))


def _forward(x, c, w_ada, b_ada, w_in, b_in, conv_w, conv_b, filt_w1, filt_b1, filt_w_inner,
             filt_b_inner, filt_w_out, filt_freq, hyena_bias, pool_w, pool_scale, w_out,
             b_out, ln_g, ln_b, *, n2len):
    bsz, seq, d = x.shape
    depth = w_ada.shape[0]
    order, n_ch = hyena_bias.shape[1], hyena_bias.shape[2]
    alpha = (2.0 * depth) ** 0.25
    tables = _dft_tables(seq, n2len)
    h = x
    for layer in range(depth):
        mod3 = _modulation(c, w_ada[layer], b_ada[layer]).reshape(bsz, 3, d)
        v, x1, x2g, yp = _input_projection(
            h, mod3, w_in[layer], b_in[layer], conv_w[layer], conv_b[layer], pool_w[layer],
            pool_scale[layer], n2len)
        taps_f, taps_r = _filter_taps(
            seq, n2len, filt_w1[layer], filt_b1[layer], filt_w_inner[layer],
            filt_b_inner[layer], filt_w_out[layer], filt_freq[layer], n_ch, order)
        kf = _filter_spectrum(taps_f, taps_r, tables[0], tables[1], tables[2], n2len)
        hbias = hyena_bias[layer].reshape(order, 1, n_ch)
        taps = [jnp.concatenate([conv_w[layer][:, k * n_ch:(k + 1) * n_ch],
                                 conv_b[layer][None, k * n_ch:(k + 1) * n_ch]], axis=0)
                for k in range(2)]
        z = _long_conv(v, x1, kf, hbias, 0, tables, in_taps=taps[0], mult_taps=taps[1],
                       pack_out=order > 1)
        for o in range(1, order):
            z = _long_conv(z, x2g, kf, hbias, o, tables)
        h = _output_projection(h, z, yp, mod3, w_out[layer], b_out[layer], ln_g[layer],
                               ln_b[layer], alpha)
    return h


def kernel(x, c, w_ada, b_ada, w_in, b_in, conv_w, conv_b, filt_w1, filt_b1, filt_w_inner,
           filt_b_inner, filt_w_out, filt_freq, hyena_bias, pool_w, pool_scale, w_out, b_out,
           ln_g, ln_b):
    return _forward(x, c, w_ada, b_ada, w_in, b_in, conv_w, conv_b, filt_w1, filt_b1,
                    filt_w_inner, filt_b_inner, filt_w_out, filt_freq, hyena_bias, pool_w,
                    pool_scale, w_out, b_out, ln_g, ln_b, n2len=MINOR_LEN)
```

```python
import functools
import math

import jax
import jax.numpy as jnp
import numpy as np
from jax import lax
from jax.experimental import pallas as pl
from jax.experimental.pallas import tpu as pltpu

F32 = jnp.float32
BF16 = jnp.bfloat16

POOL_WINDOWS = (2, 4, 8, 16)
LN_EPS = 1e-5
DECAY_TARGET = 1e-2
FAST_DECAY_PCT = 0.3
SLOW_DECAY_PCT = 1.5

LANES = 128
SUBLANES = 8
HALO = 16
BAND_ROWS = 128
BAND_SPAN = 256
MINOR_LEN = 128
FILTER_STEPS = (2, 4)
FILTER_COL_BLOCKS = 2
CONV_STEPS = (2, 2, 2)
V7X_VMEM_BYTES = 64 * 1024 * 1024
VMEM_LIMIT = V7X_VMEM_BYTES * 7 // 8
CONV_VMEM_LIMIT = V7X_VMEM_BYTES * 15 // 16


def _silu(x):
    h = 0.5 * x
    return h + h * jnp.tanh(h)


@functools.lru_cache(maxsize=None)
def _host_dft_tables(seq, n2len):
    n1h = seq // n2len
    n_fft = 2 * seq
    k1 = np.arange(n1h, dtype=np.int64)
    odd = 2 * k1 + 1
    alpha = ((odd[:, None] * k1[None, :]) % (4 * n1h)) * (2.0 * math.pi / (4 * n1h))
    n2e = np.arange(n2len + 1, dtype=np.int64)
    beta = ((n2e[:, None] * odd[None, :]) % (2 * n_fft)) * (2.0 * math.pi / (2 * n_fft))
    n2i = np.arange(n2len, dtype=np.int64)
    phi = ((n2i[:, None] * n2i[None, :]) % n2len) * (2.0 * math.pi / n2len)
    cm, sm = np.cos(phi), np.sin(phi)
    h2 = np.block([[cm, sm], [-sm, cm]])
    h2 = h2.reshape(2 * n2len, 2, n2len).transpose(0, 2, 1).reshape(2 * n2len, 2 * n2len)
    h2i = np.block([[cm, -sm], [sm, cm]])
    h2i = h2i.reshape(2, n2len, 2 * n2len).transpose(1, 0, 2).reshape(2 * n2len, 2 * n2len)
    h2, h2i = h2.astype(BF16), h2i.astype(BF16)
    quarter = 0.5 * math.pi * np.tile(np.array([0.0, 1.0]), n1h)[:, None]
    alpha2 = np.repeat(alpha, 2, axis=0)
    beta2 = np.repeat(beta, 2, axis=1)
    f32 = lambda a: a.astype(np.float32)
    seeds = tuple(f32(t) for t in (
        np.cos(alpha2 + quarter), np.sin(alpha2 + quarter),
        np.cos(alpha2 - quarter), np.sin(alpha2 - quarter), np.cos(beta2), np.sin(beta2)))
    return seeds + (h2, h2i)


def _dft_tables(seq, n2len):
    caf, saf, car, sar, cb, sb, h2, h2i = _host_dft_tables(seq, n2len)
    cb, sb = jnp.asarray(cb)[:, None, :], jnp.asarray(sb)[:, None, :]
    g_fwd = (caf.T[None] * cb[:-1] - saf.T[None] * sb[:-1]).astype(BF16)
    g_rev = (car.T[None] * cb[1:] - sar.T[None] * sb[1:]).astype(BF16)
    return g_fwd, g_rev, jnp.asarray(h2), jnp.asarray(h2i)


@functools.lru_cache(maxsize=None)
def _packed_pos_features(seq, emb, n2len):
    hl = LANES // 2
    step_rows = SUBLANES * n2len
    bands = (emb - 1) // 2
    row = np.arange(seq // 2, dtype=np.int64)[:, None]
    lane = np.arange(LANES, dtype=np.int64)[None, :]
    local = row % step_rows
    n1_local = 8 * (lane // hl) + 2 * (local // 2 // n2len) + local % 2
    pos = ((row // step_rows) * 2 * step_rows + n1_local * n2len
           + (local // 2) % n2len).astype(np.float64)
    feat = np.broadcast_to(lane % hl, pos.shape)
    t = pos / (seq - 1)
    w = (2.0 * math.pi / seq) * pos
    f = np.linspace(1e-4, bands - 1, bands)[(feat - 1) % bands]
    z = np.where(feat == 0, t,
                 np.where(feat <= bands, np.cos(f * w),
                          np.where(feat <= 2 * bands, -np.sin(f * w), 0.0)))
    return z.astype(np.float32)


def _mod_kernel(ct_ref, w_ref, b_ref, o_ref):
    s = _silu(ct_ref[...])
    w = w_ref[...]
    rows = [jnp.sum(s[:, b:b + 1] * w, axis=0, keepdims=True) for b in range(s.shape[1])]
    o_ref[...] = jnp.concatenate(rows, axis=0) + b_ref[...]


def _modulation(c, w_ada, b_ada):
    bsz, d = c.shape
    n_out = w_ada.shape[1]
    return pl.pallas_call(
        _mod_kernel,
        grid=(n_out // d,),
        in_specs=[pl.BlockSpec((d, bsz), lambda j: (0, 0)),
                  pl.BlockSpec((d, d), lambda j: (0, j)),
                  pl.BlockSpec((1, d), lambda j: (0, j))],
        out_specs=pl.BlockSpec((bsz, d), lambda j: (0, j)),
        out_shape=jax.ShapeDtypeStruct((bsz, n_out), F32),
        name="adaln_mod",
    )(c.T, w_ada, b_ada.reshape(1, n_out))


def _store_permuted(o_ref, val, n2len, first=0):
    for cb in range(val.shape[1] // LANES):
        for j in range(val.shape[0] // n2len):
            o_ref[cb, pl.ds(first + j, n2len, stride=SUBLANES), :] = val[
                j * n2len:(j + 1) * n2len, cb * LANES:(cb + 1) * LANES]


def _load_minor(x_ref, j):
    blk = x_ref[:, j * SUBLANES:(j + 1) * SUBLANES, :]
    return blk.reshape(blk.shape[0] * SUBLANES, blk.shape[2])


def _band_window(b, tile):
    last = tile + 2 * HALO - BAND_SPAN
    start = min(b * BAND_ROWS, last)
    return start, 0 if start == b * BAND_ROWS else 1


@functools.lru_cache(maxsize=None)
def _band_matrices(tile):
    t = np.arange(BAND_ROWS)[:, None]
    k = np.arange(BAND_SPAN)[None, :]
    mats = []
    for win in POOL_WINDOWS:
        half = win // 2
        per_variant = []
        for b in (0, tile // BAND_ROWS - 1):
            start, _ = _band_window(b, tile)
            rel = k + start - HALO - (b * BAND_ROWS + t)
            per_variant.append(((rel >= -half) & (rel < half)).astype(BF16))
        mats.append(np.stack(per_variant))
    return np.stack(mats)


def _proj_kernel(xp_ref, xc_ref, xn_ref, mod_ref, w_ref, b_ref, cw_ref, cb_ref, pw_ref,
                 ps_ref, band_ref, v_ref, x1_ref, x2_ref, yp_ref, p_scr, q_scr,
                 *, tile, n2len, ch, seq):
    i = pl.program_id(1)
    nt = pl.num_programs(1)
    shift = mod_ref[0:1, :]
    scale1 = 1.0 + mod_ref[1:2, :]
    xe = jnp.concatenate([xp_ref[...], xc_ref[...], xn_ref[...]], axis=0)
    ue = (xe * scale1 + shift).astype(BF16)
    uc = ue[HALO:HALO + tile, :]

    def zero_outside_sequence(p):
        return jnp.concatenate([jnp.where(i > 0, p[:HALO], 0.0), p[HALO:HALO + tile],
                                jnp.where(i < nt - 1, p[HALO + tile:], 0.0)], axis=0)

    for k, o_ref in enumerate((v_ref, x1_ref)):
        cols = slice(k * ch, (k + 1) * ch)
        p = jnp.dot(uc, w_ref[:, cols], preferred_element_type=F32) + b_ref[:, cols]
        _store_permuted(o_ref, p, n2len)

    cols = slice(2 * ch, 3 * ch)
    p = jnp.dot(ue, w_ref[:, cols], preferred_element_type=F32) + b_ref[:, cols]
    p_scr[...] = zero_outside_sequence(p)
    hg = jnp.dot(uc, w_ref[:, 3 * ch:4 * ch], preferred_element_type=F32) + b_ref[:, 3 * ch:4 * ch]
    gate = _silu(hg)
    s = (cb_ref[:, cols]
         + cw_ref[0:1, cols] * p_scr[HALO - 1:HALO - 1 + tile, :]
         + cw_ref[1:2, cols] * p_scr[HALO:HALO + tile, :]
         + cw_ref[2:3, cols] * p_scr[HALO + 1:HALO + 1 + tile, :])
    _store_permuted(x2_ref, s * gate, n2len)

    dp = ps_ref.shape[1]
    c0 = 4 * ch
    pin = jnp.dot(ue, w_ref[:, c0:c0 + dp], preferred_element_type=F32) + b_ref[:, c0:c0 + dp]
    pin = zero_outside_sequence(pin)
    q_scr[...] = pin.astype(BF16)
    p_scr[...] = pin
    pos = i * tile + lax.broadcasted_iota(jnp.int32, (tile, 1), 0)
    pg = dp // len(POOL_WINDOWS)
    groups = []
    for g, win in enumerate(POOL_WINDOWS):
        lanes = slice(g * pg, (g + 1) * pg)
        half = win // 2
        sums = []
        for b in range(tile // BAND_ROWS):
            start, variant = _band_window(b, tile)
            sums.append(jnp.dot(band_ref[g, variant], q_scr[start:start + BAND_SPAN, lanes],
                                preferred_element_type=F32))
        acc = jnp.concatenate(sums, axis=0)
        cnt = (jnp.minimum(pos + half, seq) - jnp.maximum(pos - half, 0)).astype(F32)
        diff = acc / cnt - p_scr[HALO:HALO + tile, lanes]
        groups.append(jnp.dot(diff.astype(BF16), pw_ref[g], preferred_element_type=F32))
    pgate = jnp.dot(uc, w_ref[:, c0 + dp:c0 + 2 * dp], preferred_element_type=F32) + b_ref[:, c0 + dp:c0 + 2 * dp]
    yp = jnp.concatenate(groups, axis=1) * ps_ref[...] * _silu(pgate)
    yp_ref[...] = yp.astype(yp_ref.dtype)


def _input_projection(x, mod3, w_in, b_in, conv_w, conv_b, pool_w, pool_scale, n2len):
    bsz, seq, d = x.shape
    ch = conv_w.shape[1] // 3
    dp = pool_scale.shape[0]
    n1h = seq // n2len
    tile = SUBLANES * n2len
    nt = seq // tile
    hb = tile // HALO
    n_proj = w_in.shape[1]
    cbn = ch // LANES
    assert max(POOL_WINDOWS) // 2 <= HALO and dp == ch and tile % BAND_ROWS == 0
    band = jnp.asarray(_band_matrices(tile))
    perm = jax.ShapeDtypeStruct((bsz, n1h // SUBLANES, cbn, tile, LANES), F32)
    perm_spec = pl.BlockSpec((None, None, cbn, tile, LANES), lambda b, i: (b, i, 0, 0, 0))
    full = lambda shape: pl.BlockSpec(shape, lambda b, i: (0,) * len(shape))
    kern = functools.partial(_proj_kernel, tile=tile, n2len=n2len, ch=ch, seq=seq)
    return pl.pallas_call(
        kern,
        grid=(bsz, nt),
        in_specs=[
            pl.BlockSpec((None, HALO, d), lambda b, i: (b, jnp.maximum(i * hb - 1, 0), 0)),
            pl.BlockSpec((None, tile, d), lambda b, i: (b, i, 0)),
            pl.BlockSpec((None, HALO, d), lambda b, i: (b, jnp.minimum((i + 1) * hb, seq // HALO - 1), 0)),
            pl.BlockSpec((None, 3, d), lambda b, i: (b, 0, 0)),
            full((d, n_proj)), full((1, n_proj)), full((3, 3 * ch)), full((1, 3 * ch)),
            full(pool_w.shape), full((1, dp)), full(band.shape),
        ],
        out_specs=[perm_spec, perm_spec, perm_spec,
                   pl.BlockSpec((None, tile, dp), lambda b, i: (b, i, 0))],
        out_shape=[perm, perm, perm, jax.ShapeDtypeStruct((bsz, seq, dp), BF16)],
        scratch_shapes=[pltpu.VMEM((tile + 2 * HALO, ch), F32),
                        pltpu.VMEM((tile + 2 * HALO, dp), BF16)],
        compiler_params=pltpu.CompilerParams(
            dimension_semantics=("parallel", "arbitrary"), vmem_limit_bytes=VMEM_LIMIT),
        name="in_proj_conv_pool",
    )(x, x, x, mod3, w_in.astype(BF16), b_in.reshape(1, n_proj), conv_w,
      conv_b.reshape(1, 3 * ch), pool_w.astype(BF16), pool_scale.reshape(1, dp), band)


def _filter_mlp_kernel(z_ref, w1_ref, b1_ref, wi_ref, bi_ref, fr_ref, wo_ref, ad_ref,
                       of_ref, or_ref, *, n2len):
    z = z_ref[...]
    fr = fr_ref[...]
    h = jnp.sin(fr * (jnp.dot(z.astype(BF16), w1_ref[...],
                              preferred_element_type=F32) + b1_ref[...]))
    for l in range(wi_ref.shape[0]):
        h = jnp.sin(fr * (jnp.dot(h.astype(BF16), wi_ref[l],
                                  preferred_element_type=F32) + bi_ref[l:l + 1, :]))
    hb = h.astype(BF16)
    half_lanes = LANES // 2
    pairs_per_half = z.shape[0] // 2 // n2len
    for half in range(2):
        t = z[:, half * half_lanes:half * half_lanes + 1]
        decay = jnp.exp(-t * ad_ref[...])
        for d, o_ref in enumerate((of_ref, or_ref)):
            k = jnp.dot(hb, wo_ref[d, half], preferred_element_type=F32)
            words = _pack_pairs(k * decay)
            _store_permuted(o_ref, words, n2len, first=half * pairs_per_half)


def _block_diag2(w):
    zero = jnp.zeros_like(w)
    return jnp.concatenate([jnp.concatenate([w, zero], axis=-1),
                            jnp.concatenate([zero, w], axis=-1)], axis=-2)


def _filter_taps(seq, n2len, w1, b1, w_inner, b_inner, w_out, freq, n_ch, order):
    emb, hid = w1.shape
    n_inner = w_inner.shape[0]
    oc = order * n_ch
    hl = LANES // 2
    assert emb <= hl and hid <= hl
    tile = SUBLANES * n2len
    groups = seq // (2 * tile)
    zp = jnp.asarray(_packed_pos_features(seq, emb, n2len))
    ph = hl - hid
    w1d = _block_diag2(jnp.pad(w1, ((0, hl - emb), (0, ph))))
    b1d = jnp.tile(jnp.pad(b1, (0, ph)), 2).reshape(1, LANES)
    wid = _block_diag2(jnp.pad(w_inner, ((0, 0), (0, ph), (0, ph))))
    bid = jnp.tile(jnp.pad(b_inner, ((0, 0), (0, ph))), (1, 2))
    frd = jnp.tile(jnp.pad(freq, (0, ph), constant_values=1.0), 2).reshape(1, LANES)
    wo = jnp.transpose(w_out.reshape(hid, order, 2, n_ch), (2, 0, 1, 3)).reshape(2, hid, oc)
    wo = jnp.pad(wo, ((0, 0), (0, ph), (0, 0)))
    zero = jnp.zeros_like(wo)
    wo4 = jnp.stack([jnp.concatenate([wo, zero], axis=1),
                     jnp.concatenate([zero, wo], axis=1)], axis=1)
    min_decay = math.log(DECAY_TARGET) / SLOW_DECAY_PCT
    max_decay = math.log(DECAY_TARGET) / FAST_DECAY_PCT
    absdelta = jnp.abs(jnp.linspace(min_decay, max_decay, n_ch, dtype=F32))
    absdelta = jnp.tile(absdelta, order).reshape(1, oc)

    full = lambda shape: pl.BlockSpec(shape, lambda i: (0,) * len(shape))
    kern = functools.partial(_filter_mlp_kernel, n2len=n2len)
    taps = jax.ShapeDtypeStruct((groups, oc // LANES, tile, LANES), jnp.uint32)
    taps_spec = pl.BlockSpec((None, oc // LANES, tile, LANES), lambda i: (i, 0, 0, 0))
    return pl.pallas_call(
        kern,
        grid=(groups,),
        in_specs=[
            pl.BlockSpec((tile, LANES), lambda i: (i, 0)),
            full((LANES, LANES)), full((1, LANES)), full((n_inner, LANES, LANES)),
            full((n_inner, LANES)), full((1, LANES)), full((2, 2, LANES, oc)), full((1, oc)),
        ],
        out_specs=[taps_spec, taps_spec],
        out_shape=[taps, taps],
        compiler_params=pltpu.CompilerParams(
            dimension_semantics=("arbitrary",), vmem_limit_bytes=VMEM_LIMIT),
        name="filter_mlp",
    )(zp, w1d.astype(BF16), b1d, wid.astype(BF16), bid, frd, wo4.astype(BF16), absdelta)


def _resident(shape):
    return pl.BlockSpec(shape, lambda *_: (0,) * len(shape), pipeline_mode=pl.Buffered(1))


def _dot_rows(a, b):
    return lax.dot_general(a, b, (((0,), (0,)), ((), ())), preferred_element_type=F32)


def _pack_pairs(x):
    return pltpu.bitcast(x.astype(BF16), jnp.uint32)


def _unpack_pairs(w):
    return pltpu.bitcast(w, BF16)


def _spectrum_view(a_scr, chunk, ck1, pitch):
    start = pl.multiple_of(chunk * (ck1 * pitch), SUBLANES)
    return a_scr.at[pl.ds(start, ck1 * pitch)]


def _spectrum_pair(view, j, n2len, pitch):
    return jnp.concatenate([_unpack_pairs(view[jj * pitch:jj * pitch + n2len, :])
                            for jj in (j, j + 1)], axis=1)


def _filter_fft_kernel(hf_ref, hr_ref, g_ref, gr_ref, h2_ref, o_ref, a_scr,
                       *, ns, cn2, ck1, n1h, n2len, pitch):
    s = pl.program_id(1)
    ncb = a_scr.shape[0]

    @pl.when(s < ns)
    def _():
        for j in range(cn2):
            jr = cn2 - 1 - j
            m = s * cn2 + j
            g = jnp.concatenate([g_ref[m], gr_ref[n2len - 1 - m]], axis=0)
            x = jnp.concatenate(
                [jnp.concatenate([_unpack_pairs(_load_minor(hf_ref.at[:, cb], j)),
                                  _unpack_pairs(_load_minor(hr_ref.at[:, cb], jr))], axis=0)
                 for cb in range(ncb)], axis=1)
            packed = _pack_pairs(_dot_rows(g, x))
            for cb in range(ncb):
                a_scr[cb, pl.ds(m, n1h, stride=pitch), :] = packed[:, cb * LANES:(cb + 1) * LANES]

    @pl.when(s >= ns)
    def _():
        for cb in range(ncb):
            view = _spectrum_view(a_scr.at[cb], s - ns, ck1, pitch)
            lanes = slice(cb * LANES, (cb + 1) * LANES)
            for j in range(0, ck1, 2):
                rows = _spectrum_pair(view, j, n2len, pitch)
                u = jnp.dot(h2_ref[...], rows, preferred_element_type=F32)
                u = (u * (1.0 / (n1h * n2len))).astype(o_ref.dtype)
                o_ref[j, :, lanes] = u[:, :LANES]
                o_ref[j + 1, :, lanes] = u[:, LANES:]


def _filter_spectrum(taps_f, taps_r, g_fwd, g_rev, h2, n2len):
    groups, ocb, _, _ = taps_f.shape
    oc = ocb * LANES
    n1h = groups * 2 * SUBLANES
    ns, ns2 = FILTER_STEPS
    cn2, ck1 = n2len // ns, n1h // ns2
    pitch = n2len + SUBLANES
    kern = functools.partial(_filter_fft_kernel, ns=ns, cn2=cn2, ck1=ck1, n1h=n1h,
                             n2len=n2len, pitch=pitch)
    ncb = FILTER_COL_BLOCKS
    return pl.pallas_call(
        kern,
        grid=(ocb // ncb, ns + ns2),
        in_specs=[
            pl.BlockSpec((groups, ncb, cn2 * SUBLANES, LANES),
                         lambda c, s: (0, c, jnp.minimum(s, ns - 1), 0)),
            pl.BlockSpec((groups, ncb, cn2 * SUBLANES, LANES),
                         lambda c, s: (0, c, jnp.maximum(ns - 1 - s, 0), 0)),
            _resident(g_fwd.shape), _resident(g_rev.shape), _resident(h2.shape),
        ],
        out_specs=pl.BlockSpec((ck1, 2 * n2len, ncb * LANES),
                               lambda c, s: (jnp.maximum(s - ns, 0), 0, c)),
        out_shape=jax.ShapeDtypeStruct((n1h, 2 * n2len, oc), BF16),
        scratch_shapes=[pltpu.VMEM((ncb, n1h * pitch, LANES), jnp.uint32)],
        compiler_params=pltpu.CompilerParams(
            dimension_semantics=("parallel", "arbitrary"), vmem_limit_bytes=CONV_VMEM_LIMIT),
        name="filter_spectrum",
    )(taps_f, taps_r, g_fwd, g_rev, h2)


def _short_conv_slab(x_ref, lo_ref, hi_ref, taps_ref, j, count, first_chunk, last_chunk):
    cur = _load_minor(x_ref, j)
    zero_row = jnp.zeros((1, cur.shape[1]), F32)
    if j > 0:
        prev = _load_minor(x_ref, j - 1)
    else:
        prev = _load_minor(lo_ref, 0)
        prev = jnp.where(first_chunk, jnp.concatenate([zero_row, prev[:-1]], axis=0), prev)
    if j < count - 1:
        nxt = _load_minor(x_ref, j + 1)
    else:
        nxt = _load_minor(hi_ref, 0)
        nxt = jnp.where(last_chunk, jnp.concatenate([nxt[1:], zero_row], axis=0), nxt)
    return (taps_ref[3:4, :] + taps_ref[0:1, :] * prev + taps_ref[1:2, :] * cur
            + taps_ref[2:3, :] * nxt)


def _long_conv_kernel(*refs, steps, n1h, n2len, pitch, conv_in, conv_mult):
    refs = list(refs)
    vf_ref = refs.pop(0)
    vlo_ref, vhi_ref, vtaps_ref = (refs.pop(0), refs.pop(0), refs.pop(0)) if conv_in else (None,) * 3
    g_ref, kf_ref, h2_ref, h2i_ref, xm_ref = (refs.pop(0) for _ in range(5))
    xlo_ref, xhi_ref, xtaps_ref = (refs.pop(0), refs.pop(0), refs.pop(0)) if conv_mult else (None,) * 3
    hb_ref, o_ref, a_scr, keep_scr = refs
    ns1, ns2, ns3 = steps
    cn2a, ck1, cn2 = n2len // ns1, n1h // ns2, n2len // ns3
    s = pl.program_id(2)

    @pl.when(s < ns1)
    def _():
        base = s * cn2a
        for j in range(cn2a):
            if conv_in:
                x = _short_conv_slab(vf_ref, vlo_ref, vhi_ref, vtaps_ref, j, cn2a,
                                     s == 0, s == ns1 - 1)
            elif vf_ref.dtype == jnp.uint32:
                x = _unpack_pairs(_load_minor(vf_ref, j)).astype(F32)
            else:
                x = _load_minor(vf_ref, j)
            keep_scr[base + j] = x
            r = _dot_rows(g_ref[base + j], x.astype(BF16))
            a_scr[pl.ds(base + j, n1h, stride=pitch), :] = _pack_pairs(r)

    @pl.when((s >= ns1) & (s < ns1 + ns2))
    def _():
        view = _spectrum_view(a_scr, s - ns1, ck1, pitch)
        for j in range(0, ck1, 2):
            rows = _spectrum_pair(view, j, n2len, pitch)
            u = jnp.dot(h2_ref[...], rows, preferred_element_type=F32)
            ur, ui = u[:n2len], u[n2len:]
            kf = jnp.concatenate([kf_ref[j], kf_ref[j + 1]], axis=1).astype(F32)
            kr, ki = kf[:n2len], kf[n2len:]
            y = jnp.concatenate([ur * kr - ui * ki, ur * ki + ui * kr], axis=0)
            bb = _pack_pairs(jnp.dot(h2i_ref[...], y.astype(BF16), preferred_element_type=F32))
            for jj in range(2):
                view[(j + jj) * pitch:(j + jj) * pitch + n2len, :] = bb[
                    :, jj * LANES:(jj + 1) * LANES]

    @pl.when(s >= ns1 + ns2)
    def _():
        base = (s - ns1 - ns2) * cn2
        for j in range(cn2):
            rows = _unpack_pairs(a_scr[pl.ds(base + j, n1h, stride=pitch), :])
            y = jnp.dot(g_ref[base + j], rows, preferred_element_type=F32)
            if conv_mult:
                xm = _short_conv_slab(xm_ref, xlo_ref, xhi_ref, xtaps_ref, j, cn2,
                                      s == ns1 + ns2, s == ns1 + ns2 + ns3 - 1)
            else:
                xm = _load_minor(xm_ref, j)
            z = xm * (y + hb_ref[...] * keep_scr[base + j])
            if o_ref.dtype == jnp.uint32:
                z = _pack_pairs(z)
            o_ref[:, j * SUBLANES:(j + 1) * SUBLANES, :] = z.reshape(
                z.shape[0] // SUBLANES, SUBLANES, z.shape[1])


def _long_conv(u_perm, mult_perm, kf, hbias, order_idx, tables, in_taps=None, mult_taps=None,
               pack_out=False):
    g_fwd, _, h2, h2i = tables
    bsz, _, cbn, _, _ = u_perm.shape
    n2len, n1h = g_fwd.shape[0], g_fwd.shape[1]
    steps = CONV_STEPS
    ns1, ns2, ns3 = steps
    cn2a, cn2 = n2len // ns1, n2len // ns3
    pitch = n2len + SUBLANES

    def early(s):
        return jnp.minimum(s, ns1 - 1)

    def late(s):
        return jnp.clip(s - ns1 - ns2, 0, ns3 - 1)

    def chunk_specs(chunk_of, width, groups):
        slab = (None, groups, None, SUBLANES, LANES)
        return [
            pl.BlockSpec((None, groups, None, width * SUBLANES, LANES),
                         lambda c, b, s: (b, 0, c, chunk_of(s), 0)),
            pl.BlockSpec(slab, lambda c, b, s: (b, 0, c, (chunk_of(s) * width - 1) % n2len, 0)),
            pl.BlockSpec(slab, lambda c, b, s: (b, 0, c, ((chunk_of(s) + 1) * width) % n2len, 0)),
        ]

    taps_spec = pl.BlockSpec((4, LANES), lambda c, b, s: (0, c))
    in_specs, args = [], []
    specs = chunk_specs(early, cn2a, u_perm.shape[1])
    if in_taps is None:
        in_specs += specs[:1]
        args += [u_perm]
    else:
        in_specs += specs + [taps_spec]
        args += [u_perm, u_perm, u_perm, in_taps]
    in_specs += [
        _resident(g_fwd.shape),
        pl.BlockSpec((n1h // ns2, 2 * n2len, LANES),
                     lambda c, b, s: (jnp.clip(s - ns1, 0, ns2 - 1), 0, order_idx * cbn + c)),
        _resident(h2.shape), _resident(h2i.shape),
    ]
    args += [g_fwd, kf, h2, h2i]
    specs = chunk_specs(late, cn2, mult_perm.shape[1])
    if mult_taps is None:
        in_specs += specs[:1]
        args += [mult_perm]
    else:
        in_specs += specs + [taps_spec]
        args += [mult_perm, mult_perm, mult_perm, mult_taps]
    in_specs.append(pl.BlockSpec((None, 1, LANES), lambda c, b, s: (order_idx, 0, c)))
    args.append(hbias)
    out_groups = n1h // SUBLANES // (2 if pack_out else 1)
    out_spec = chunk_specs(late, cn2, out_groups)[0]
    out_shape = jax.ShapeDtypeStruct((bsz, out_groups, cbn, n2len * SUBLANES, LANES),
                                     jnp.uint32 if pack_out else F32)
    kern = functools.partial(_long_conv_kernel, steps=steps, n1h=n1h, n2len=n2len,
                             pitch=pitch, conv_in=in_taps is not None,
                             conv_mult=mult_taps is not None)
    return pl.pallas_call(
        kern,
        grid=(cbn, bsz, sum(steps)),
        in_specs=in_specs,
        out_specs=out_spec,
        out_shape=out_shape,
        scratch_shapes=[pltpu.VMEM((n1h * pitch, LANES), jnp.uint32),
                        pltpu.VMEM((n2len, n1h, LANES), F32)],
        compiler_params=pltpu.CompilerParams(
            dimension_semantics=("parallel", "parallel", "arbitrary"),
            vmem_limit_bytes=CONV_VMEM_LIMIT),
        name="long_conv",
    )(*args)


def _out_kernel(x_ref, yh_ref, yp_ref, mod_ref, w_ref, b_ref, g_ref, beta_ref, o_ref,
                *, n2len, alpha):
    cbn = yh_ref.shape[0]
    yh = jnp.concatenate(
        [jnp.concatenate([yh_ref[cb, pl.ds(j, n2len, stride=SUBLANES), :]
                          for cb in range(cbn)], axis=1)
         for j in range(SUBLANES)], axis=0)
    y = jnp.concatenate([yh.astype(BF16), yp_ref[...]], axis=1)
    acc = jnp.dot(y, w_ref[...], preferred_element_type=F32)
    h = alpha * x_ref[...] + mod_ref[2:3, :] * (acc + b_ref[...])
    mu = jnp.mean(h, axis=-1, keepdims=True)
    hc = h - mu
    var = jnp.mean(hc * hc, axis=-1, keepdims=True)
    o_ref[...] = hc * lax.rsqrt(var + LN_EPS) * g_ref[...] + beta_ref[...]


def _output_projection(x, yh_perm, yp, mod3, w_out, b_out, ln_g, ln_b, alpha):
    bsz, seq, d = x.shape
    _, _, cbn, tile, _ = yh_perm.shape
    ch = cbn * LANES
    n2len = tile // SUBLANES
    dp = yp.shape[2]
    full = lambda shape: pl.BlockSpec(shape, lambda b, i: (0,) * len(shape))
    kern = functools.partial(_out_kernel, n2len=n2len, alpha=alpha)
    return pl.pallas_call(
        kern,
        grid=(bsz, seq // tile),
        in_specs=[
            pl.BlockSpec((None, tile, d), lambda b, i: (b, i, 0)),
            pl.BlockSpec((None, None, cbn, tile, LANES), lambda b, i: (b, i, 0, 0, 0)),
            pl.BlockSpec((None, tile, dp), lambda b, i: (b, i, 0)),
            pl.BlockSpec((None, 3, d), lambda b, i: (b, 0, 0)),
            full((ch + dp, d)), full((1, d)), full((1, d)), full((1, d)),
        ],
        out_specs=pl.BlockSpec((None, tile, d), lambda b, i: (b, i, 0)),
        out_shape=jax.ShapeDtypeStruct(x.shape, x.dtype),
        compiler_params=pltpu.CompilerParams(
            dimension_semantics=("parallel", "arbitrary"), vmem_limit_bytes=VMEM_LIMIT),
        name="out_proj_deepnorm",
    )(x, yh_perm, yp, mod3, w_out.astype(BF16), b_out.reshape(1, d), ln_g.reshape(1, d),
      ln_b.reshape(1, d))


def _forward(x, c, w_ada, b_ada, w_in, b_in, conv_w, conv_b, filt_w1, filt_b1, filt_w_inner,
             filt_b_inner, filt_w_out, filt_freq, hyena_bias, pool_w, pool_scale, w_out,
             b_out, ln_g, ln_b, *, n2len):
    bsz, seq, d = x.shape
    depth = w_ada.shape[0]
    order, n_ch = hyena_bias.shape[1], hyena_bias.shape[2]
    alpha = (2.0 * depth) ** 0.25
    tables = _dft_tables(seq, n2len)
    h = x
    for layer in range(depth):
        mod3 = _modulation(c, w_ada[layer], b_ada[layer]).reshape(bsz, 3, d)
        v, x1, x2g, yp = _input_projection(
            h, mod3, w_in[layer], b_in[layer], conv_w[layer], conv_b[layer], pool_w[layer],
            pool_scale[layer], n2len)
        taps_f, taps_r = _filter_taps(
            seq, n2len, filt_w1[layer], filt_b1[layer], filt_w_inner[layer],
            filt_b_inner[layer], filt_w_out[layer], filt_freq[layer], n_ch, order)
        kf = _filter_spectrum(taps_f, taps_r, tables[0], tables[1], tables[2], n2len)
        hbias = hyena_bias[layer].reshape(order, 1, n_ch)
        taps = [jnp.concatenate([conv_w[layer][:, k * n_ch:(k + 1) * n_ch],
                                 conv_b[layer][None, k * n_ch:(k + 1) * n_ch]], axis=0)
                for k in range(2)]
        z = _long_conv(v, x1, kf, hbias, 0, tables, in_taps=taps[0], mult_taps=taps[1],
                       pack_out=order > 1)
        for o in range(1, order):
            z = _long_conv(z, x2g, kf, hbias, o, tables)
        h = _output_projection(h, z, yp, mod3, w_out[layer], b_out[layer], ln_g[layer],
                               ln_b[layer], alpha)
    return h


def kernel(x, c, w_ada, b_ada, w_in, b_in, conv_w, conv_b, filt_w1, filt_b1, filt_w_inner,
           filt_b_inner, filt_w_out, filt_freq, hyena_bias, pool_w, pool_scale, w_out, b_out,
           ln_g, ln_b):
    return _forward(x, c, w_ada, b_ada, w_in, b_in, conv_w, conv_b, filt_w1, filt_b1,
                    filt_w_inner, filt_b_inner, filt_w_out, filt_freq, hyena_bias, pool_w,
                    pool_scale, w_out, b_out, ln_g, ln_b, n2len=MINOR_LEN)
```

```python
import functools
import math

import jax
import jax.numpy as jnp
import numpy as np
from jax import lax
from jax.experimental import pallas as pl
from jax.experimental.pallas import tpu as pltpu

F32 = jnp.float32
BF16 = jnp.bfloat16
HIGHEST = lax.Precision.HIGHEST

POOL_WINDOWS = (2, 4, 8, 16)
LN_EPS = 1e-5
DECAY_TARGET = 1e-2
FAST_DECAY_PCT = 0.3
SLOW_DECAY_PCT = 1.5

LANES = 128
SUBLANES = 8
HALO = 16
BAND_ROWS = 128
BAND_SPAN = 256
MINOR_LEN = 128
FILTER_STEPS = (4, 4)
FILTER_COL_BLOCKS = 2
CONV_STEPS = (2, 2, 2)
V7X_VMEM_BYTES = 64 * 1024 * 1024
VMEM_LIMIT = V7X_VMEM_BYTES * 7 // 8
CONV_VMEM_LIMIT = V7X_VMEM_BYTES * 15 // 16


def _silu(x):
    h = 0.5 * x
    return h + h * jnp.tanh(h)


@functools.lru_cache(maxsize=None)
def _host_dft_tables(seq, n2len):
    n1h = seq // n2len
    n_fft = 2 * seq
    k1 = np.arange(n1h, dtype=np.int64)
    odd = 2 * k1 + 1
    alpha = ((odd[:, None] * k1[None, :]) % (4 * n1h)) * (2.0 * math.pi / (4 * n1h))
    n2e = np.arange(n2len + 1, dtype=np.int64)
    beta = ((n2e[:, None] * odd[None, :]) % (2 * n_fft)) * (2.0 * math.pi / (2 * n_fft))
    n2i = np.arange(n2len, dtype=np.int64)
    phi = ((n2i[:, None] * n2i[None, :]) % n2len) * (2.0 * math.pi / n2len)
    cm, sm = np.cos(phi), np.sin(phi)
    h2 = np.block([[cm, sm], [-sm, cm]])
    h2 = h2.reshape(2 * n2len, 2, n2len).transpose(0, 2, 1).reshape(2 * n2len, 2 * n2len)
    h2i = np.block([[cm, -sm], [sm, cm]])
    h2i = h2i.reshape(2, n2len, 2 * n2len).transpose(1, 0, 2).reshape(2 * n2len, 2 * n2len)
    h2, h2i = h2.astype(BF16), h2i.astype(BF16)
    quarter = 0.5 * math.pi * np.tile(np.array([0.0, 1.0]), n1h)[:, None]
    alpha2 = np.repeat(alpha, 2, axis=0)
    beta2 = np.repeat(beta, 2, axis=1)
    f32 = lambda a: a.astype(np.float32)
    seeds = tuple(f32(t) for t in (
        np.cos(alpha2 + quarter), np.sin(alpha2 + quarter),
        np.cos(alpha2 - quarter), np.sin(alpha2 - quarter), np.cos(beta2), np.sin(beta2)))
    return seeds + (h2, h2i)


def _dft_tables(seq, n2len):
    caf, saf, car, sar, cb, sb, h2, h2i = _host_dft_tables(seq, n2len)
    cb, sb = jnp.asarray(cb)[:, None, :], jnp.asarray(sb)[:, None, :]
    g_fwd = (caf.T[None] * cb[:-1] - saf.T[None] * sb[:-1]).astype(BF16)
    g_rev = (car.T[None] * cb[1:] - sar.T[None] * sb[1:]).astype(BF16)
    return g_fwd, g_rev, jnp.asarray(h2), jnp.asarray(h2i)


@functools.lru_cache(maxsize=None)
def _packed_pos_features(seq, emb, n2len):
    hl = LANES // 2
    step_rows = SUBLANES * n2len
    bands = (emb - 1) // 2
    row = np.arange(seq // 2, dtype=np.int64)[:, None]
    lane = np.arange(LANES, dtype=np.int64)[None, :]
    local = row % step_rows
    n1_local = 8 * (lane // hl) + 2 * (local // 2 // n2len) + local % 2
    pos = ((row // step_rows) * 2 * step_rows + n1_local * n2len
           + (local // 2) % n2len).astype(np.float64)
    feat = np.broadcast_to(lane % hl, pos.shape)
    t = pos / (seq - 1)
    w = (2.0 * math.pi / seq) * pos
    f = np.linspace(1e-4, bands - 1, bands)[(feat - 1) % bands]
    z = np.where(feat == 0, t,
                 np.where(feat <= bands, np.cos(f * w),
                          np.where(feat <= 2 * bands, -np.sin(f * w), 0.0)))
    return z.astype(np.float32)


def _mod_kernel(c_ref, w_ref, b_ref, o_ref):
    s = _silu(c_ref[...])
    o_ref[...] = jnp.dot(s, w_ref[...], precision=HIGHEST,
                         preferred_element_type=F32) + b_ref[...]


def _modulation(c, w_ada, b_ada):
    bsz, d = c.shape
    n_out = w_ada.shape[1]
    return pl.pallas_call(
        _mod_kernel,
        grid=(n_out // d,),
        in_specs=[pl.BlockSpec((bsz, d), lambda j: (0, 0)),
                  pl.BlockSpec((d, d), lambda j: (0, j)),
                  pl.BlockSpec((1, d), lambda j: (0, j))],
        out_specs=pl.BlockSpec((bsz, d), lambda j: (0, j)),
        out_shape=jax.ShapeDtypeStruct((bsz, n_out), F32),
        name="adaln_mod",
    )(c, w_ada, b_ada.reshape(1, n_out))


def _store_permuted(o_ref, val, n2len, first=0):
    for cb in range(val.shape[1] // LANES):
        for j in range(val.shape[0] // n2len):
            o_ref[cb, pl.ds(first + j, n2len, stride=SUBLANES), :] = val[
                j * n2len:(j + 1) * n2len, cb * LANES:(cb + 1) * LANES]


def _load_minor(x_ref, j):
    blk = x_ref[:, j * SUBLANES:(j + 1) * SUBLANES, :]
    return blk.reshape(blk.shape[0] * SUBLANES, blk.shape[2])


def _band_window(b, tile):
    last = tile + 2 * HALO - BAND_SPAN
    start = min(b * BAND_ROWS, last)
    return start, 0 if start == b * BAND_ROWS else 1


@functools.lru_cache(maxsize=None)
def _band_matrices(tile):
    t = np.arange(BAND_ROWS)[:, None]
    k = np.arange(BAND_SPAN)[None, :]
    mats = []
    for win in POOL_WINDOWS:
        half = win // 2
        per_variant = []
        for b in (0, tile // BAND_ROWS - 1):
            start, _ = _band_window(b, tile)
            rel = k + start - HALO - (b * BAND_ROWS + t)
            per_variant.append(((rel >= -half) & (rel < half)).astype(BF16))
        mats.append(np.stack(per_variant))
    return np.stack(mats)


def _proj_kernel(xp_ref, xc_ref, xn_ref, mod_ref, wf_ref, b_ref, cw_ref, cb_ref, pw_ref,
                 ps_ref, band_ref, v_ref, x1_ref, x2_ref, yp_ref, p_scr, q_scr, w_ref,
                 *, tile, n2len, ch, seq):
    i = pl.program_id(1)
    nt = pl.num_programs(1)

    @pl.when((pl.program_id(0) == 0) & (i == 0))
    def _():
        w_ref[...] = wf_ref[...].astype(BF16)

    shift = mod_ref[0:1, :]
    scale1 = 1.0 + mod_ref[1:2, :]
    xe = jnp.concatenate([xp_ref[...], xc_ref[...], xn_ref[...]], axis=0)
    ue = (xe * scale1 + shift).astype(BF16)
    uc = ue[HALO:HALO + tile, :]

    def zero_outside_sequence(p):
        return jnp.concatenate([jnp.where(i > 0, p[:HALO], 0.0), p[HALO:HALO + tile],
                                jnp.where(i < nt - 1, p[HALO + tile:], 0.0)], axis=0)

    for k, o_ref in enumerate((v_ref, x1_ref)):
        cols = slice(k * ch, (k + 1) * ch)
        p = jnp.dot(uc, w_ref[:, cols], preferred_element_type=F32) + b_ref[:, cols]
        _store_permuted(o_ref, p, n2len)

    cols = slice(2 * ch, 3 * ch)
    p = jnp.dot(ue, w_ref[:, cols], preferred_element_type=F32) + b_ref[:, cols]
    p_scr[...] = zero_outside_sequence(p)
    hg = jnp.dot(uc, w_ref[:, 3 * ch:4 * ch], preferred_element_type=F32) + b_ref[:, 3 * ch:4 * ch]
    gate = _silu(hg)
    s = (cb_ref[:, cols]
         + cw_ref[0:1, cols] * p_scr[HALO - 1:HALO - 1 + tile, :]
         + cw_ref[1:2, cols] * p_scr[HALO:HALO + tile, :]
         + cw_ref[2:3, cols] * p_scr[HALO + 1:HALO + 1 + tile, :])
    _store_permuted(x2_ref, s * gate, n2len)

    dp = ps_ref.shape[1]
    c0 = 4 * ch
    pin = jnp.dot(ue, w_ref[:, c0:c0 + dp], preferred_element_type=F32) + b_ref[:, c0:c0 + dp]
    pin = zero_outside_sequence(pin)
    q_scr[...] = pin.astype(BF16)
    p_scr[...] = pin
    pos = i * tile + lax.broadcasted_iota(jnp.int32, (tile, 1), 0)
    pg = dp // len(POOL_WINDOWS)
    groups = []
    for g, win in enumerate(POOL_WINDOWS):
        lanes = slice(g * pg, (g + 1) * pg)
        half = win // 2
        sums = []
        for b in range(tile // BAND_ROWS):
            start, variant = _band_window(b, tile)
            sums.append(jnp.dot(band_ref[g, variant], q_scr[start:start + BAND_SPAN, lanes],
                                preferred_element_type=F32))
        acc = jnp.concatenate(sums, axis=0)
        cnt = (jnp.minimum(pos + half, seq) - jnp.maximum(pos - half, 0)).astype(F32)
        diff = acc / cnt - p_scr[HALO:HALO + tile, lanes]
        groups.append(jnp.dot(diff.astype(BF16), pw_ref[g], preferred_element_type=F32))
    pgate = jnp.dot(uc, w_ref[:, c0 + dp:c0 + 2 * dp], preferred_element_type=F32) + b_ref[:, c0 + dp:c0 + 2 * dp]
    yp = jnp.concatenate(groups, axis=1) * ps_ref[...] * _silu(pgate)
    yp_ref[...] = yp.astype(yp_ref.dtype)


def _input_projection(x, mod3, w_in, b_in, conv_w, conv_b, pool_w, pool_scale, n2len):
    bsz, seq, d = x.shape
    ch = conv_w.shape[1] // 3
    dp = pool_scale.shape[0]
    n1h = seq // n2len
    tile = SUBLANES * n2len
    nt = seq // tile
    hb = tile // HALO
    n_proj = w_in.shape[1]
    cbn = ch // LANES
    assert max(POOL_WINDOWS) // 2 <= HALO and dp == ch and tile % BAND_ROWS == 0
    band = jnp.asarray(_band_matrices(tile))
    perm = jax.ShapeDtypeStruct((bsz, n1h // SUBLANES, cbn, tile, LANES), F32)
    perm_spec = pl.BlockSpec((None, None, cbn, tile, LANES), lambda b, i: (b, i, 0, 0, 0))
    full = lambda shape: pl.BlockSpec(shape, lambda b, i: (0,) * len(shape))
    kern = functools.partial(_proj_kernel, tile=tile, n2len=n2len, ch=ch, seq=seq)
    return pl.pallas_call(
        kern,
        grid=(bsz, nt),
        in_specs=[
            pl.BlockSpec((None, HALO, d), lambda b, i: (b, jnp.maximum(i * hb - 1, 0), 0)),
            pl.BlockSpec((None, tile, d), lambda b, i: (b, i, 0)),
            pl.BlockSpec((None, HALO, d), lambda b, i: (b, jnp.minimum((i + 1) * hb, seq // HALO - 1), 0)),
            pl.BlockSpec((None, 3, d), lambda b, i: (b, 0, 0)),
            _resident((d, n_proj)), full((1, n_proj)), full((3, 3 * ch)), full((1, 3 * ch)),
            full(pool_w.shape), full((1, dp)), full(band.shape),
        ],
        out_specs=[perm_spec, perm_spec, perm_spec,
                   pl.BlockSpec((None, tile, dp), lambda b, i: (b, i, 0))],
        out_shape=[perm, perm, perm, jax.ShapeDtypeStruct((bsz, seq, dp), BF16)],
        scratch_shapes=[pltpu.VMEM((tile + 2 * HALO, ch), F32),
                        pltpu.VMEM((tile + 2 * HALO, dp), BF16),
                        pltpu.VMEM((d, n_proj), BF16)],
        compiler_params=pltpu.CompilerParams(
            dimension_semantics=("arbitrary", "arbitrary"), vmem_limit_bytes=VMEM_LIMIT),
        name="in_proj_conv_pool",
    )(x, x, x, mod3, w_in, b_in.reshape(1, n_proj), conv_w,
      conv_b.reshape(1, 3 * ch), pool_w.astype(BF16), pool_scale.reshape(1, dp), band)


def _filter_mlp_kernel(z_ref, w1_ref, b1_ref, wi_ref, bi_ref, fr_ref, wo_ref, ad_ref,
                       of_ref, or_ref, *, n2len):
    z = z_ref[...]
    fr = fr_ref[...]
    h = jnp.sin(fr * (jnp.dot(z.astype(BF16), w1_ref[...],
                              preferred_element_type=F32) + b1_ref[...]))
    for l in range(wi_ref.shape[0]):
        h = jnp.sin(fr * (jnp.dot(h.astype(BF16), wi_ref[l],
                                  preferred_element_type=F32) + bi_ref[l:l + 1, :]))
    hb = h.astype(BF16)
    half_lanes = LANES // 2
    pairs_per_half = z.shape[0] // 2 // n2len
    for half in range(2):
        t = z[:, half * half_lanes:half * half_lanes + 1]
        decay = jnp.exp(-t * ad_ref[...])
        for d, o_ref in enumerate((of_ref, or_ref)):
            k = jnp.dot(hb, wo_ref[d, half], preferred_element_type=F32)
            words = _pack_pairs(k * decay)
            _store_permuted(o_ref, words, n2len, first=half * pairs_per_half)


def _block_diag2(w):
    zero = jnp.zeros_like(w)
    return jnp.concatenate([jnp.concatenate([w, zero], axis=-1),
                            jnp.concatenate([zero, w], axis=-1)], axis=-2)


def _filter_taps(seq, n2len, w1, b1, w_inner, b_inner, w_out, freq, n_ch, order):
    emb, hid = w1.shape
    n_inner = w_inner.shape[0]
    oc = order * n_ch
    hl = LANES // 2
    assert emb <= hl and hid <= hl
    tile = SUBLANES * n2len
    groups = seq // (2 * tile)
    zp = jnp.asarray(_packed_pos_features(seq, emb, n2len))
    ph = hl - hid
    w1d = _block_diag2(jnp.pad(w1, ((0, hl - emb), (0, ph))))
    b1d = jnp.tile(jnp.pad(b1, (0, ph)), 2).reshape(1, LANES)
    wid = _block_diag2(jnp.pad(w_inner, ((0, 0), (0, ph), (0, ph))))
    bid = jnp.tile(jnp.pad(b_inner, ((0, 0), (0, ph))), (1, 2))
    frd = jnp.tile(jnp.pad(freq, (0, ph), constant_values=1.0), 2).reshape(1, LANES)
    wo = jnp.transpose(w_out.reshape(hid, order, 2, n_ch), (2, 0, 1, 3)).reshape(2, hid, oc)
    wo = jnp.pad(wo, ((0, 0), (0, ph), (0, 0)))
    zero = jnp.zeros_like(wo)
    wo4 = jnp.stack([jnp.concatenate([wo, zero], axis=1),
                     jnp.concatenate([zero, wo], axis=1)], axis=1)
    min_decay = math.log(DECAY_TARGET) / SLOW_DECAY_PCT
    max_decay = math.log(DECAY_TARGET) / FAST_DECAY_PCT
    absdelta = jnp.abs(jnp.linspace(min_decay, max_decay, n_ch, dtype=F32))
    absdelta = jnp.tile(absdelta, order).reshape(1, oc)

    full = lambda shape: pl.BlockSpec(shape, lambda i: (0,) * len(shape))
    kern = functools.partial(_filter_mlp_kernel, n2len=n2len)
    taps = jax.ShapeDtypeStruct((groups, oc // LANES, tile, LANES), jnp.uint32)
    taps_spec = pl.BlockSpec((None, oc // LANES, tile, LANES), lambda i: (i, 0, 0, 0))
    return pl.pallas_call(
        kern,
        grid=(groups,),
        in_specs=[
            pl.BlockSpec((tile, LANES), lambda i: (i, 0)),
            full((LANES, LANES)), full((1, LANES)), full((n_inner, LANES, LANES)),
            full((n_inner, LANES)), full((1, LANES)), full((2, 2, LANES, oc)), full((1, oc)),
        ],
        out_specs=[taps_spec, taps_spec],
        out_shape=[taps, taps],
        compiler_params=pltpu.CompilerParams(
            dimension_semantics=("arbitrary",), vmem_limit_bytes=VMEM_LIMIT),
        name="filter_mlp",
    )(zp, w1d.astype(BF16), b1d, wid.astype(BF16), bid, frd, wo4.astype(BF16), absdelta)


def _resident(shape):
    return pl.BlockSpec(shape, lambda *_: (0,) * len(shape), pipeline_mode=pl.Buffered(1))


def _dot_rows(a, b):
    return lax.dot_general(a, b, (((0,), (0,)), ((), ())), preferred_element_type=F32)


def _pack_pairs(x):
    return pltpu.bitcast(x.astype(BF16), jnp.uint32)


def _unpack_pairs(w):
    return pltpu.bitcast(w, BF16)


def _spectrum_view(a_scr, chunk, ck1, pitch):
    start = pl.multiple_of(chunk * (ck1 * pitch), SUBLANES)
    return a_scr.at[pl.ds(start, ck1 * pitch)]


def _spectrum_pair(view, j, n2len, pitch):
    return jnp.concatenate([_unpack_pairs(view[jj * pitch:jj * pitch + n2len, :])
                            for jj in (j, j + 1)], axis=1)


def _filter_fft_kernel(hf_ref, hr_ref, g_ref, gr_ref, h2_ref, o_ref, a_scr,
                       *, ns, cn2, ck1, n1h, n2len, pitch):
    s = pl.program_id(1)
    ncb = a_scr.shape[0]

    @pl.when(s < ns)
    def _():
        for j in range(cn2):
            jr = cn2 - 1 - j
            m = s * cn2 + j
            g = jnp.concatenate([g_ref[m], gr_ref[n2len - 1 - m]], axis=0)
            x = jnp.concatenate(
                [jnp.concatenate([_unpack_pairs(_load_minor(hf_ref.at[:, cb], j)),
                                  _unpack_pairs(_load_minor(hr_ref.at[:, cb], jr))], axis=0)
                 for cb in range(ncb)], axis=1)
            packed = _pack_pairs(_dot_rows(g, x))
            for cb in range(ncb):
                a_scr[cb, pl.ds(m, n1h, stride=pitch), :] = packed[:, cb * LANES:(cb + 1) * LANES]

    @pl.when(s >= ns)
    def _():
        for cb in range(ncb):
            view = _spectrum_view(a_scr.at[cb], s - ns, ck1, pitch)
            lanes = slice(cb * LANES, (cb + 1) * LANES)
            for j in range(0, ck1, 2):
                rows = _spectrum_pair(view, j, n2len, pitch)
                u = jnp.dot(h2_ref[...], rows, preferred_element_type=F32)
                u = (u * (1.0 / (n1h * n2len))).astype(o_ref.dtype)
                o_ref[j, :, lanes] = u[:, :LANES]
                o_ref[j + 1, :, lanes] = u[:, LANES:]


def _filter_spectrum(taps_f, taps_r, g_fwd, g_rev, h2, n2len):
    groups, ocb, _, _ = taps_f.shape
    oc = ocb * LANES
    n1h = groups * 2 * SUBLANES
    ns, ns2 = FILTER_STEPS
    cn2, ck1 = n2len // ns, n1h // ns2
    pitch = n2len + SUBLANES
    kern = functools.partial(_filter_fft_kernel, ns=ns, cn2=cn2, ck1=ck1, n1h=n1h,
                             n2len=n2len, pitch=pitch)
    ncb = FILTER_COL_BLOCKS
    return pl.pallas_call(
        kern,
        grid=(ocb // ncb, ns + ns2),
        in_specs=[
            pl.BlockSpec((groups, ncb, cn2 * SUBLANES, LANES),
                         lambda c, s: (0, c, jnp.minimum(s, ns - 1), 0)),
            pl.BlockSpec((groups, ncb, cn2 * SUBLANES, LANES),
                         lambda c, s: (0, c, jnp.maximum(ns - 1 - s, 0), 0)),
            _resident(g_fwd.shape), _resident(g_rev.shape), _resident(h2.shape),
        ],
        out_specs=pl.BlockSpec((ck1, 2 * n2len, ncb * LANES),
                               lambda c, s: (jnp.maximum(s - ns, 0), 0, c)),
        out_shape=jax.ShapeDtypeStruct((n1h, 2 * n2len, oc), BF16),
        scratch_shapes=[pltpu.VMEM((ncb, n1h * pitch, LANES), jnp.uint32)],
        compiler_params=pltpu.CompilerParams(
            dimension_semantics=("parallel", "arbitrary"), vmem_limit_bytes=CONV_VMEM_LIMIT),
        name="filter_spectrum",
    )(taps_f, taps_r, g_fwd, g_rev, h2)


def _short_conv_slab(x_ref, lo_ref, hi_ref, taps_ref, j, count, first_chunk, last_chunk):
    cur = _load_minor(x_ref, j)
    zero_row = jnp.zeros((1, cur.shape[1]), F32)
    if j > 0:
        prev = _load_minor(x_ref, j - 1)
    else:
        prev = _load_minor(lo_ref, 0)
        prev = jnp.where(first_chunk, jnp.concatenate([zero_row, prev[:-1]], axis=0), prev)
    if j < count - 1:
        nxt = _load_minor(x_ref, j + 1)
    else:
        nxt = _load_minor(hi_ref, 0)
        nxt = jnp.where(last_chunk, jnp.concatenate([nxt[1:], zero_row], axis=0), nxt)
    return (taps_ref[3:4, :] + taps_ref[0:1, :] * prev + taps_ref[1:2, :] * cur
            + taps_ref[2:3, :] * nxt)


def _long_conv_kernel(*refs, steps, n1h, n2len, pitch, conv_in, conv_mult):
    refs = list(refs)
    vf_ref = refs.pop(0)
    vlo_ref, vhi_ref, vtaps_ref = (refs.pop(0), refs.pop(0), refs.pop(0)) if conv_in else (None,) * 3
    g_ref, kf_ref, h2_ref, h2i_ref, xm_ref = (refs.pop(0) for _ in range(5))
    xlo_ref, xhi_ref, xtaps_ref = (refs.pop(0), refs.pop(0), refs.pop(0)) if conv_mult else (None,) * 3
    hb_ref, o_ref, a_scr, keep_scr = refs
    ns1, ns2, ns3 = steps
    cn2a, ck1, cn2 = n2len // ns1, n1h // ns2, n2len // ns3
    s = pl.program_id(2)

    @pl.when(s < ns1)
    def _():
        base = s * cn2a
        for j in range(cn2a):
            if conv_in:
                x = _short_conv_slab(vf_ref, vlo_ref, vhi_ref, vtaps_ref, j, cn2a,
                                     s == 0, s == ns1 - 1)
            elif vf_ref.dtype == jnp.uint32:
                x = _unpack_pairs(_load_minor(vf_ref, j)).astype(F32)
            else:
                x = _load_minor(vf_ref, j)
            keep_scr[base + j] = x
            r = _dot_rows(g_ref[base + j], x.astype(BF16))
            a_scr[pl.ds(base + j, n1h, stride=pitch), :] = _pack_pairs(r)

    @pl.when((s >= ns1) & (s < ns1 + ns2))
    def _():
        view = _spectrum_view(a_scr, s - ns1, ck1, pitch)
        for j in range(0, ck1, 2):
            rows = _spectrum_pair(view, j, n2len, pitch)
            u = jnp.dot(h2_ref[...], rows, preferred_element_type=F32)
            ur, ui = u[:n2len], u[n2len:]
            kf = jnp.concatenate([kf_ref[j], kf_ref[j + 1]], axis=1).astype(F32)
            kr, ki = kf[:n2len], kf[n2len:]
            y = jnp.concatenate([ur * kr - ui * ki, ur * ki + ui * kr], axis=0)
            bb = _pack_pairs(jnp.dot(h2i_ref[...], y.astype(BF16), preferred_element_type=F32))
            for jj in range(2):
                view[(j + jj) * pitch:(j + jj) * pitch + n2len, :] = bb[
                    :, jj * LANES:(jj + 1) * LANES]

    @pl.when(s >= ns1 + ns2)
    def _():
        base = (s - ns1 - ns2) * cn2
        for j in range(cn2):
            rows = _unpack_pairs(a_scr[pl.ds(base + j, n1h, stride=pitch), :])
            y = jnp.dot(g_ref[base + j], rows, preferred_element_type=F32)
            if conv_mult:
                xm = _short_conv_slab(xm_ref, xlo_ref, xhi_ref, xtaps_ref, j, cn2,
                                      s == ns1 + ns2, s == ns1 + ns2 + ns3 - 1)
            else:
                xm = _load_minor(xm_ref, j)
            z = xm * (y + hb_ref[...] * keep_scr[base + j])
            if o_ref.dtype == jnp.uint32:
                z = _pack_pairs(z)
            o_ref[:, j * SUBLANES:(j + 1) * SUBLANES, :] = z.reshape(
                z.shape[0] // SUBLANES, SUBLANES, z.shape[1])


def _long_conv(u_perm, mult_perm, kf, hbias, order_idx, tables, in_taps=None, mult_taps=None,
               pack_out=False):
    g_fwd, _, h2, h2i = tables
    bsz, _, cbn, _, _ = u_perm.shape
    n2len, n1h = g_fwd.shape[0], g_fwd.shape[1]
    steps = CONV_STEPS
    ns1, ns2, ns3 = steps
    cn2a, cn2 = n2len // ns1, n2len // ns3
    pitch = n2len + SUBLANES

    def early(s):
        return jnp.minimum(s, ns1 - 1)

    def late(s):
        return jnp.clip(s - ns1 - ns2, 0, ns3 - 1)

    def chunk_specs(chunk_of, width, groups):
        slab = (None, groups, None, SUBLANES, LANES)
        return [
            pl.BlockSpec((None, groups, None, width * SUBLANES, LANES),
                         lambda c, b, s: (b, 0, c, chunk_of(s), 0)),
            pl.BlockSpec(slab, lambda c, b, s: (b, 0, c, (chunk_of(s) * width - 1) % n2len, 0)),
            pl.BlockSpec(slab, lambda c, b, s: (b, 0, c, ((chunk_of(s) + 1) * width) % n2len, 0)),
        ]

    taps_spec = pl.BlockSpec((4, LANES), lambda c, b, s: (0, c))
    in_specs, args = [], []
    specs = chunk_specs(early, cn2a, u_perm.shape[1])
    if in_taps is None:
        in_specs += specs[:1]
        args += [u_perm]
    else:
        in_specs += specs + [taps_spec]
        args += [u_perm, u_perm, u_perm, in_taps]
    in_specs += [
        _resident(g_fwd.shape),
        pl.BlockSpec((n1h // ns2, 2 * n2len, LANES),
                     lambda c, b, s: (jnp.clip(s - ns1, 0, ns2 - 1), 0, order_idx * cbn + c)),
        _resident(h2.shape), _resident(h2i.shape),
    ]
    args += [g_fwd, kf, h2, h2i]
    specs = chunk_specs(late, cn2, mult_perm.shape[1])
    if mult_taps is None:
        in_specs += specs[:1]
        args += [mult_perm]
    else:
        in_specs += specs + [taps_spec]
        args += [mult_perm, mult_perm, mult_perm, mult_taps]
    in_specs.append(pl.BlockSpec((None, 1, LANES), lambda c, b, s: (order_idx, 0, c)))
    args.append(hbias)
    out_groups = n1h // SUBLANES // (2 if pack_out else 1)
    out_spec = chunk_specs(late, cn2, out_groups)[0]
    out_shape = jax.ShapeDtypeStruct((bsz, out_groups, cbn, n2len * SUBLANES, LANES),
                                     jnp.uint32 if pack_out else F32)
    kern = functools.partial(_long_conv_kernel, steps=steps, n1h=n1h, n2len=n2len,
                             pitch=pitch, conv_in=in_taps is not None,
                             conv_mult=mult_taps is not None)
    return pl.pallas_call(
        kern,
        grid=(cbn, bsz, sum(steps)),
        in_specs=in_specs,
        out_specs=out_spec,
        out_shape=out_shape,
        scratch_shapes=[pltpu.VMEM((n1h * pitch, LANES), jnp.uint32),
                        pltpu.VMEM((n2len, n1h, LANES), F32)],
        compiler_params=pltpu.CompilerParams(
            dimension_semantics=("parallel", "parallel", "arbitrary"),
            vmem_limit_bytes=CONV_VMEM_LIMIT),
        name="long_conv",
    )(*args)


def _out_kernel(x_ref, yh_ref, yp_ref, mod_ref, w_ref, b_ref, g_ref, beta_ref, o_ref,
                *, n2len, alpha):
    cbn = yh_ref.shape[0]
    yh = jnp.concatenate(
        [jnp.concatenate([yh_ref[cb, pl.ds(j, n2len, stride=SUBLANES), :]
                          for cb in range(cbn)], axis=1)
         for j in range(SUBLANES)], axis=0)
    y = jnp.concatenate([yh.astype(BF16), yp_ref[...]], axis=1)
    acc = jnp.dot(y, w_ref[...], preferred_element_type=F32)
    h = alpha * x_ref[...] + mod_ref[2:3, :] * (acc + b_ref[...])
    mu = jnp.mean(h, axis=-1, keepdims=True)
    hc = h - mu
    var = jnp.mean(hc * hc, axis=-1, keepdims=True)
    o_ref[...] = hc * lax.rsqrt(var + LN_EPS) * g_ref[...] + beta_ref[...]


def _output_projection(x, yh_perm, yp, mod3, w_out, b_out, ln_g, ln_b, alpha):
    bsz, seq, d = x.shape
    _, _, cbn, tile, _ = yh_perm.shape
    ch = cbn * LANES
    n2len = tile // SUBLANES
    dp = yp.shape[2]
    full = lambda shape: pl.BlockSpec(shape, lambda b, i: (0,) * len(shape))
    kern = functools.partial(_out_kernel, n2len=n2len, alpha=alpha)
    return pl.pallas_call(
        kern,
        grid=(bsz, seq // tile),
        in_specs=[
            pl.BlockSpec((None, tile, d), lambda b, i: (b, i, 0)),
            pl.BlockSpec((None, None, cbn, tile, LANES), lambda b, i: (b, i, 0, 0, 0)),
            pl.BlockSpec((None, tile, dp), lambda b, i: (b, i, 0)),
            pl.BlockSpec((None, 3, d), lambda b, i: (b, 0, 0)),
            full((ch + dp, d)), full((1, d)), full((1, d)), full((1, d)),
        ],
        out_specs=pl.BlockSpec((None, tile, d), lambda b, i: (b, i, 0)),
        out_shape=jax.ShapeDtypeStruct(x.shape, x.dtype),
        compiler_params=pltpu.CompilerParams(
            dimension_semantics=("parallel", "arbitrary"), vmem_limit_bytes=VMEM_LIMIT),
        name="out_proj_deepnorm",
    )(x, yh_perm, yp, mod3, w_out.astype(BF16), b_out.reshape(1, d), ln_g.reshape(1, d),
      ln_b.reshape(1, d))


def _forward(x, c, w_ada, b_ada, w_in, b_in, conv_w, conv_b, filt_w1, filt_b1, filt_w_inner,
             filt_b_inner, filt_w_out, filt_freq, hyena_bias, pool_w, pool_scale, w_out,
             b_out, ln_g, ln_b, *, n2len):
    bsz, seq, d = x.shape
    depth = w_ada.shape[0]
    order, n_ch = hyena_bias.shape[1], hyena_bias.shape[2]
    alpha = (2.0 * depth) ** 0.25
    tables = _dft_tables(seq, n2len)
    h = x
    for layer in range(depth):
        mod3 = _modulation(c, w_ada[layer], b_ada[layer]).reshape(bsz, 3, d)
        v, x1, x2g, yp = _input_projection(
            h, mod3, w_in[layer], b_in[layer], conv_w[layer], conv_b[layer], pool_w[layer],
            pool_scale[layer], n2len)
        taps_f, taps_r = _filter_taps(
            seq, n2len, filt_w1[layer], filt_b1[layer], filt_w_inner[layer],
            filt_b_inner[layer], filt_w_out[layer], filt_freq[layer], n_ch, order)
        kf = _filter_spectrum(taps_f, taps_r, tables[0], tables[1], tables[2], n2len)
        hbias = hyena_bias[layer].reshape(order, 1, n_ch)
        taps = [jnp.concatenate([conv_w[layer][:, k * n_ch:(k + 1) * n_ch],
                                 conv_b[layer][None, k * n_ch:(k + 1) * n_ch]], axis=0)
                for k in range(2)]
        z = _long_conv(v, x1, kf, hbias, 0, tables, in_taps=taps[0], mult_taps=taps[1],
                       pack_out=order > 1)
        for o in range(1, order):
            z = _long_conv(z, x2g, kf, hbias, o, tables)
        h = _output_projection(h, z, yp, mod3, w_out[layer], b_out[layer], ln_g[layer],
                               ln_b[layer], alpha)
    return h


def kernel(x, c, w_ada, b_ada, w_in, b_in, conv_w, conv_b, filt_w1, filt_b1, filt_w_inner,
           filt_b_inner, filt_w_out, filt_freq, hyena_bias, pool_w, pool_scale, w_out, b_out,
           ln_g, ln_b):
    return _forward(x, c, w_ada, b_ada, w_in, b_in, conv_w, conv_b, filt_w1, filt_b1,
                    filt_w_inner, filt_b_inner, filt_w_out, filt_freq, hyena_bias, pool_w,
                    pool_scale, w_out, b_out, ln_g, ln_b, n2len=MINOR_LEN)
```

```python
import functools
import math

import jax
import jax.numpy as jnp
import numpy as np
from jax import lax
from jax.experimental import pallas as pl
from jax.experimental.pallas import tpu as pltpu

F32 = jnp.float32
BF16 = jnp.bfloat16

POOL_WINDOWS = (2, 4, 8, 16)
LN_EPS = 1e-5
DECAY_TARGET = 1e-2
FAST_DECAY_PCT = 0.3
SLOW_DECAY_PCT = 1.5

LANES = 128
SUBLANES = 8
HALO = 16
BAND_ROWS = 128
BAND_SPAN = 256
MINOR_LEN = 128
FILTER_STEPS = (4, 4)
FILTER_COL_BLOCKS = 2
CONV_STEPS = (2, 2, 2)
V7X_VMEM_BYTES = 64 * 1024 * 1024
VMEM_LIMIT = V7X_VMEM_BYTES * 7 // 8
CONV_VMEM_LIMIT = V7X_VMEM_BYTES * 15 // 16


def _silu(x):
    h = 0.5 * x
    return h + h * jnp.tanh(h)


@functools.lru_cache(maxsize=None)
def _host_dft_tables(seq, n2len):
    n1h = seq // n2len
    n_fft = 2 * seq
    k1 = np.arange(n1h, dtype=np.int64)
    odd = 2 * k1 + 1
    alpha = ((odd[:, None] * k1[None, :]) % (4 * n1h)) * (2.0 * math.pi / (4 * n1h))
    n2e = np.arange(n2len + 1, dtype=np.int64)
    beta = ((n2e[:, None] * odd[None, :]) % (2 * n_fft)) * (2.0 * math.pi / (2 * n_fft))
    n2i = np.arange(n2len, dtype=np.int64)
    phi = ((n2i[:, None] * n2i[None, :]) % n2len) * (2.0 * math.pi / n2len)
    cm, sm = np.cos(phi), np.sin(phi)
    h2 = np.block([[cm, sm], [-sm, cm]])
    h2 = h2.reshape(2 * n2len, 2, n2len).transpose(0, 2, 1).reshape(2 * n2len, 2 * n2len)
    h2i = np.block([[cm, -sm], [sm, cm]])
    h2i = h2i.reshape(2, n2len, 2 * n2len).transpose(1, 0, 2).reshape(2 * n2len, 2 * n2len)
    h2, h2i = h2.astype(BF16), h2i.astype(BF16)
    quarter = 0.5 * math.pi * np.tile(np.array([0.0, 1.0]), n1h)[:, None]
    alpha2 = np.repeat(alpha, 2, axis=0)
    beta2 = np.repeat(beta, 2, axis=1)
    f32 = lambda a: a.astype(np.float32)
    seeds = tuple(f32(t) for t in (
        np.cos(alpha2 + quarter), np.sin(alpha2 + quarter),
        np.cos(alpha2 - quarter), np.sin(alpha2 - quarter), np.cos(beta2), np.sin(beta2)))
    return seeds + (h2, h2i)


def _dft_tables(seq, n2len):
    caf, saf, car, sar, cb, sb, h2, h2i = _host_dft_tables(seq, n2len)
    cb, sb = jnp.asarray(cb)[:, None, :], jnp.asarray(sb)[:, None, :]
    g_fwd = (caf.T[None] * cb[:-1] - saf.T[None] * sb[:-1]).astype(BF16)
    g_rev = (car.T[None] * cb[1:] - sar.T[None] * sb[1:]).astype(BF16)
    return g_fwd, g_rev, jnp.asarray(h2), jnp.asarray(h2i)


@functools.lru_cache(maxsize=None)
def _packed_pos_features(seq, emb, n2len):
    hl = LANES // 2
    step_rows = SUBLANES * n2len
    bands = (emb - 1) // 2
    row = np.arange(seq // 2, dtype=np.int64)[:, None]
    lane = np.arange(LANES, dtype=np.int64)[None, :]
    local = row % step_rows
    n1_local = 8 * (lane // hl) + 2 * (local // 2 // n2len) + local % 2
    pos = ((row // step_rows) * 2 * step_rows + n1_local * n2len
           + (local // 2) % n2len).astype(np.float64)
    feat = np.broadcast_to(lane % hl, pos.shape)
    t = pos / (seq - 1)
    w = (2.0 * math.pi / seq) * pos
    f = np.linspace(1e-4, bands - 1, bands)[(feat - 1) % bands]
    z = np.where(feat == 0, t,
                 np.where(feat <= bands, np.cos(f * w),
                          np.where(feat <= 2 * bands, -np.sin(f * w), 0.0)))
    return z.astype(np.float32)


def _mod_kernel(ct_ref, w_ref, b_ref, o_ref):
    s = _silu(ct_ref[...])
    w = w_ref[...]
    rows = [jnp.sum(s[:, b:b + 1] * w, axis=0, keepdims=True) for b in range(s.shape[1])]
    o_ref[...] = jnp.concatenate(rows, axis=0) + b_ref[...]


def _modulation(c, w_ada, b_ada):
    bsz, d = c.shape
    n_out = w_ada.shape[1]
    return pl.pallas_call(
        _mod_kernel,
        grid=(n_out // d,),
        in_specs=[pl.BlockSpec((d, bsz), lambda j: (0, 0)),
                  pl.BlockSpec((d, d), lambda j: (0, j)),
                  pl.BlockSpec((1, d), lambda j: (0, j))],
        out_specs=pl.BlockSpec((bsz, d), lambda j: (0, j)),
        out_shape=jax.ShapeDtypeStruct((bsz, n_out), F32),
        name="adaln_mod",
    )(c.T, w_ada, b_ada.reshape(1, n_out))


def _store_permuted(o_ref, val, n2len, first=0):
    for cb in range(val.shape[1] // LANES):
        for j in range(val.shape[0] // n2len):
            o_ref[cb, pl.ds(first + j, n2len, stride=SUBLANES), :] = val[
                j * n2len:(j + 1) * n2len, cb * LANES:(cb + 1) * LANES]


def _load_minor(x_ref, j):
    blk = x_ref[:, j * SUBLANES:(j + 1) * SUBLANES, :]
    return blk.reshape(blk.shape[0] * SUBLANES, blk.shape[2])


def _band_window(b, tile):
    last = tile + 2 * HALO - BAND_SPAN
    start = min(b * BAND_ROWS, last)
    return start, 0 if start == b * BAND_ROWS else 1


@functools.lru_cache(maxsize=None)
def _band_matrices(tile):
    t = np.arange(BAND_ROWS)[:, None]
    k = np.arange(BAND_SPAN)[None, :]
    mats = []
    for win in POOL_WINDOWS:
        half = win // 2
        per_variant = []
        for b in (0, tile // BAND_ROWS - 1):
            start, _ = _band_window(b, tile)
            rel = k + start - HALO - (b * BAND_ROWS + t)
            per_variant.append(((rel >= -half) & (rel < half)).astype(BF16))
        mats.append(np.stack(per_variant))
    return np.stack(mats)


def _proj_kernel(xp_ref, xc_ref, xn_ref, mod_ref, wf_ref, b_ref, cw_ref, cb_ref, pw_ref,
                 ps_ref, band_ref, v_ref, x1_ref, x2_ref, yp_ref, p_scr, q_scr, w_ref,
                 *, tile, n2len, ch, seq):
    i = pl.program_id(1)
    nt = pl.num_programs(1)

    @pl.when((pl.program_id(0) == 0) & (i == 0))
    def _():
        w_ref[...] = wf_ref[...].astype(BF16)

    shift = mod_ref[0:1, :]
    scale1 = 1.0 + mod_ref[1:2, :]
    xe = jnp.concatenate([xp_ref[...], xc_ref[...], xn_ref[...]], axis=0)
    ue = (xe * scale1 + shift).astype(BF16)
    uc = ue[HALO:HALO + tile, :]

    def zero_outside_sequence(p):
        return jnp.concatenate([jnp.where(i > 0, p[:HALO], 0.0), p[HALO:HALO + tile],
                                jnp.where(i < nt - 1, p[HALO + tile:], 0.0)], axis=0)

    for k, o_ref in enumerate((v_ref, x1_ref)):
        cols = slice(k * ch, (k + 1) * ch)
        p = jnp.dot(uc, w_ref[:, cols], preferred_element_type=F32) + b_ref[:, cols]
        _store_permuted(o_ref, p, n2len)

    cols = slice(2 * ch, 3 * ch)
    p = jnp.dot(ue, w_ref[:, cols], preferred_element_type=F32) + b_ref[:, cols]
    p_scr[...] = zero_outside_sequence(p)
    hg = jnp.dot(uc, w_ref[:, 3 * ch:4 * ch], preferred_element_type=F32) + b_ref[:, 3 * ch:4 * ch]
    gate = _silu(hg)
    s = (cb_ref[:, cols]
         + cw_ref[0:1, cols] * p_scr[HALO - 1:HALO - 1 + tile, :]
         + cw_ref[1:2, cols] * p_scr[HALO:HALO + tile, :]
         + cw_ref[2:3, cols] * p_scr[HALO + 1:HALO + 1 + tile, :])
    _store_permuted(x2_ref, s * gate, n2len)

    dp = ps_ref.shape[1]
    c0 = 4 * ch
    pin = jnp.dot(ue, w_ref[:, c0:c0 + dp], preferred_element_type=F32) + b_ref[:, c0:c0 + dp]
    pin = zero_outside_sequence(pin)
    q_scr[...] = pin.astype(BF16)
    p_scr[...] = pin
    pos = i * tile + lax.broadcasted_iota(jnp.int32, (tile, 1), 0)
    pg = dp // len(POOL_WINDOWS)
    groups = []
    for g, win in enumerate(POOL_WINDOWS):
        lanes = slice(g * pg, (g + 1) * pg)
        half = win // 2
        sums = []
        for b in range(tile // BAND_ROWS):
            start, variant = _band_window(b, tile)
            sums.append(jnp.dot(band_ref[g, variant], q_scr[start:start + BAND_SPAN, lanes],
                                preferred_element_type=F32))
        acc = jnp.concatenate(sums, axis=0)
        cnt = (jnp.minimum(pos + half, seq) - jnp.maximum(pos - half, 0)).astype(F32)
        diff = acc / cnt - p_scr[HALO:HALO + tile, lanes]
        groups.append(jnp.dot(diff.astype(BF16), pw_ref[g], preferred_element_type=F32))
    pgate = jnp.dot(uc, w_ref[:, c0 + dp:c0 + 2 * dp], preferred_element_type=F32) + b_ref[:, c0 + dp:c0 + 2 * dp]
    yp = jnp.concatenate(groups, axis=1) * ps_ref[...] * _silu(pgate)
    yp_ref[...] = yp.astype(yp_ref.dtype)


def _input_projection(x, mod3, w_in, b_in, conv_w, conv_b, pool_w, pool_scale, n2len):
    bsz, seq, d = x.shape
    ch = conv_w.shape[1] // 3
    dp = pool_scale.shape[0]
    n1h = seq // n2len
    tile = SUBLANES * n2len
    nt = seq // tile
    hb = tile // HALO
    n_proj = w_in.shape[1]
    cbn = ch // LANES
    assert max(POOL_WINDOWS) // 2 <= HALO and dp == ch and tile % BAND_ROWS == 0
    band = jnp.asarray(_band_matrices(tile))
    perm = jax.ShapeDtypeStruct((bsz, n1h // SUBLANES, cbn, tile, LANES), F32)
    perm_spec = pl.BlockSpec((None, None, cbn, tile, LANES), lambda b, i: (b, i, 0, 0, 0))
    full = lambda shape: pl.BlockSpec(shape, lambda b, i: (0,) * len(shape))
    kern = functools.partial(_proj_kernel, tile=tile, n2len=n2len, ch=ch, seq=seq)
    return pl.pallas_call(
        kern,
        grid=(bsz, nt),
        in_specs=[
            pl.BlockSpec((None, HALO, d), lambda b, i: (b, jnp.maximum(i * hb - 1, 0), 0)),
            pl.BlockSpec((None, tile, d), lambda b, i: (b, i, 0)),
            pl.BlockSpec((None, HALO, d), lambda b, i: (b, jnp.minimum((i + 1) * hb, seq // HALO - 1), 0)),
            pl.BlockSpec((None, 3, d), lambda b, i: (b, 0, 0)),
            _resident((d, n_proj)), full((1, n_proj)), full((3, 3 * ch)), full((1, 3 * ch)),
            full(pool_w.shape), full((1, dp)), full(band.shape),
        ],
        out_specs=[perm_spec, perm_spec, perm_spec,
                   pl.BlockSpec((None, tile, dp), lambda b, i: (b, i, 0))],
        out_shape=[perm, perm, perm, jax.ShapeDtypeStruct((bsz, seq, dp), BF16)],
        scratch_shapes=[pltpu.VMEM((tile + 2 * HALO, ch), F32),
                        pltpu.VMEM((tile + 2 * HALO, dp), BF16),
                        pltpu.VMEM((d, n_proj), BF16)],
        compiler_params=pltpu.CompilerParams(
            dimension_semantics=("arbitrary", "arbitrary"), vmem_limit_bytes=VMEM_LIMIT),
        name="in_proj_conv_pool",
    )(x, x, x, mod3, w_in, b_in.reshape(1, n_proj), conv_w,
      conv_b.reshape(1, 3 * ch), pool_w.astype(BF16), pool_scale.reshape(1, dp), band)


def _filter_mlp_kernel(z_ref, w1_ref, b1_ref, wi_ref, bi_ref, fr_ref, wo_ref, ad_ref,
                       of_ref, or_ref, *, n2len):
    z = z_ref[...]
    fr = fr_ref[...]
    h = jnp.sin(fr * (jnp.dot(z.astype(BF16), w1_ref[...],
                              preferred_element_type=F32) + b1_ref[...]))
    for l in range(wi_ref.shape[0]):
        h = jnp.sin(fr * (jnp.dot(h.astype(BF16), wi_ref[l],
                                  preferred_element_type=F32) + bi_ref[l:l + 1, :]))
    hb = h.astype(BF16)
    half_lanes = LANES // 2
    pairs_per_half = z.shape[0] // 2 // n2len
    for half in range(2):
        t = z[:, half * half_lanes:half * half_lanes + 1]
        decay = jnp.exp(-t * ad_ref[...])
        for d, o_ref in enumerate((of_ref, or_ref)):
            k = jnp.dot(hb, wo_ref[d, half], preferred_element_type=F32)
            words = _pack_pairs(k * decay)
            _store_permuted(o_ref, words, n2len, first=half * pairs_per_half)


def _block_diag2(w):
    zero = jnp.zeros_like(w)
    return jnp.concatenate([jnp.concatenate([w, zero], axis=-1),
                            jnp.concatenate([zero, w], axis=-1)], axis=-2)


def _filter_taps(seq, n2len, w1, b1, w_inner, b_inner, w_out, freq, n_ch, order):
    emb, hid = w1.shape
    n_inner = w_inner.shape[0]
    oc = order * n_ch
    hl = LANES // 2
    assert emb <= hl and hid <= hl
    tile = SUBLANES * n2len
    groups = seq // (2 * tile)
    zp = jnp.asarray(_packed_pos_features(seq, emb, n2len))
    ph = hl - hid
    w1d = _block_diag2(jnp.pad(w1, ((0, hl - emb), (0, ph))))
    b1d = jnp.tile(jnp.pad(b1, (0, ph)), 2).reshape(1, LANES)
    wid = _block_diag2(jnp.pad(w_inner, ((0, 0), (0, ph), (0, ph))))
    bid = jnp.tile(jnp.pad(b_inner, ((0, 0), (0, ph))), (1, 2))
    frd = jnp.tile(jnp.pad(freq, (0, ph), constant_values=1.0), 2).reshape(1, LANES)
    wo = jnp.transpose(w_out.reshape(hid, order, 2, n_ch), (2, 0, 1, 3)).reshape(2, hid, oc)
    wo = jnp.pad(wo, ((0, 0), (0, ph), (0, 0)))
    zero = jnp.zeros_like(wo)
    wo4 = jnp.stack([jnp.concatenate([wo, zero], axis=1),
                     jnp.concatenate([zero, wo], axis=1)], axis=1)
    min_decay = math.log(DECAY_TARGET) / SLOW_DECAY_PCT
    max_decay = math.log(DECAY_TARGET) / FAST_DECAY_PCT
    absdelta = jnp.abs(jnp.linspace(min_decay, max_decay, n_ch, dtype=F32))
    absdelta = jnp.tile(absdelta, order).reshape(1, oc)

    full = lambda shape: pl.BlockSpec(shape, lambda i: (0,) * len(shape))
    kern = functools.partial(_filter_mlp_kernel, n2len=n2len)
    taps = jax.ShapeDtypeStruct((groups, oc // LANES, tile, LANES), jnp.uint32)
    taps_spec = pl.BlockSpec((None, oc // LANES, tile, LANES), lambda i: (i, 0, 0, 0))
    return pl.pallas_call(
        kern,
        grid=(groups,),
        in_specs=[
            pl.BlockSpec((tile, LANES), lambda i: (i, 0)),
            full((LANES, LANES)), full((1, LANES)), full((n_inner, LANES, LANES)),
            full((n_inner, LANES)), full((1, LANES)), full((2, 2, LANES, oc)), full((1, oc)),
        ],
        out_specs=[taps_spec, taps_spec],
        out_shape=[taps, taps],
        compiler_params=pltpu.CompilerParams(
            dimension_semantics=("arbitrary",), vmem_limit_bytes=VMEM_LIMIT),
        name="filter_mlp",
    )(zp, w1d.astype(BF16), b1d, wid.astype(BF16), bid, frd, wo4.astype(BF16), absdelta)


def _resident(shape):
    return pl.BlockSpec(shape, lambda *_: (0,) * len(shape), pipeline_mode=pl.Buffered(1))


def _dot_rows(a, b):
    return lax.dot_general(a, b, (((0,), (0,)), ((), ())), preferred_element_type=F32)


def _pack_pairs(x):
    return pltpu.bitcast(x.astype(BF16), jnp.uint32)


def _unpack_pairs(w):
    return pltpu.bitcast(w, BF16)


def _spectrum_view(a_scr, chunk, ck1, pitch):
    start = pl.multiple_of(chunk * (ck1 * pitch), SUBLANES)
    return a_scr.at[pl.ds(start, ck1 * pitch)]


def _spectrum_pair(view, j, n2len, pitch):
    return jnp.concatenate([_unpack_pairs(view[jj * pitch:jj * pitch + n2len, :])
                            for jj in (j, j + 1)], axis=1)


def _filter_fft_kernel(hf_ref, hr_ref, g_ref, gr_ref, h2_ref, o_ref, a_scr,
                       *, ns, cn2, ck1, n1h, n2len, pitch):
    s = pl.program_id(1)
    ncb = a_scr.shape[0]

    @pl.when(s < ns)
    def _():
        for j in range(cn2):
            jr = cn2 - 1 - j
            m = s * cn2 + j
            g = jnp.concatenate([g_ref[m], gr_ref[n2len - 1 - m]], axis=0)
            x = jnp.concatenate(
                [jnp.concatenate([_unpack_pairs(_load_minor(hf_ref.at[:, cb], j)),
                                  _unpack_pairs(_load_minor(hr_ref.at[:, cb], jr))], axis=0)
                 for cb in range(ncb)], axis=1)
            packed = _pack_pairs(_dot_rows(g, x))
            for cb in range(ncb):
                a_scr[cb, pl.ds(m, n1h, stride=pitch), :] = packed[:, cb * LANES:(cb + 1) * LANES]

    @pl.when(s >= ns)
    def _():
        for cb in range(ncb):
            view = _spectrum_view(a_scr.at[cb], s - ns, ck1, pitch)
            lanes = slice(cb * LANES, (cb + 1) * LANES)
            for j in range(0, ck1, 2):
                rows = _spectrum_pair(view, j, n2len, pitch)
                u = jnp.dot(h2_ref[...], rows, preferred_element_type=F32)
                u = (u * (1.0 / (n1h * n2len))).astype(o_ref.dtype)
                o_ref[j, :, lanes] = u[:, :LANES]
                o_ref[j + 1, :, lanes] = u[:, LANES:]


def _filter_spectrum(taps_f, taps_r, g_fwd, g_rev, h2, n2len):
    groups, ocb, _, _ = taps_f.shape
    oc = ocb * LANES
    n1h = groups * 2 * SUBLANES
    ns, ns2 = FILTER_STEPS
    cn2, ck1 = n2len // ns, n1h // ns2
    pitch = n2len + SUBLANES
    kern = functools.partial(_filter_fft_kernel, ns=ns, cn2=cn2, ck1=ck1, n1h=n1h,
                             n2len=n2len, pitch=pitch)
    ncb = FILTER_COL_BLOCKS
    return pl.pallas_call(
        kern,
        grid=(ocb // ncb, ns + ns2),
        in_specs=[
            pl.BlockSpec((groups, ncb, cn2 * SUBLANES, LANES),
                         lambda c, s: (0, c, jnp.minimum(s, ns - 1), 0)),
            pl.BlockSpec((groups, ncb, cn2 * SUBLANES, LANES),
                         lambda c, s: (0, c, jnp.maximum(ns - 1 - s, 0), 0)),
            _resident(g_fwd.shape), _resident(g_rev.shape), _resident(h2.shape),
        ],
        out_specs=pl.BlockSpec((ck1, 2 * n2len, ncb * LANES),
                               lambda c, s: (jnp.maximum(s - ns, 0), 0, c)),
        out_shape=jax.ShapeDtypeStruct((n1h, 2 * n2len, oc), BF16),
        scratch_shapes=[pltpu.VMEM((ncb, n1h * pitch, LANES), jnp.uint32)],
        compiler_params=pltpu.CompilerParams(
            dimension_semantics=("parallel", "arbitrary"), vmem_limit_bytes=CONV_VMEM_LIMIT),
        name="filter_spectrum",
    )(taps_f, taps_r, g_fwd, g_rev, h2)


def _short_conv_slab(x_ref, lo_ref, hi_ref, taps_ref, j, count, first_chunk, last_chunk):
    cur = _load_minor(x_ref, j)
    zero_row = jnp.zeros((1, cur.shape[1]), F32)
    if j > 0:
        prev = _load_minor(x_ref, j - 1)
    else:
        prev = _load_minor(lo_ref, 0)
        prev = jnp.where(first_chunk, jnp.concatenate([zero_row, prev[:-1]], axis=0), prev)
    if j < count - 1:
        nxt = _load_minor(x_ref, j + 1)
    else:
        nxt = _load_minor(hi_ref, 0)
        nxt = jnp.where(last_chunk, jnp.concatenate([nxt[1:], zero_row], axis=0), nxt)
    return (taps_ref[3:4, :] + taps_ref[0:1, :] * prev + taps_ref[1:2, :] * cur
            + taps_ref[2:3, :] * nxt)


def _long_conv_kernel(*refs, steps, n1h, n2len, pitch, conv_in, conv_mult):
    refs = list(refs)
    vf_ref = refs.pop(0)
    vlo_ref, vhi_ref, vtaps_ref = (refs.pop(0), refs.pop(0), refs.pop(0)) if conv_in else (None,) * 3
    g_ref, kf_ref, h2_ref, h2i_ref, xm_ref = (refs.pop(0) for _ in range(5))
    xlo_ref, xhi_ref, xtaps_ref = (refs.pop(0), refs.pop(0), refs.pop(0)) if conv_mult else (None,) * 3
    hb_ref, o_ref, a_scr, keep_scr = refs
    ns1, ns2, ns3 = steps
    cn2a, ck1, cn2 = n2len // ns1, n1h // ns2, n2len // ns3
    s = pl.program_id(2)

    @pl.when(s < ns1)
    def _():
        base = s * cn2a
        for j in range(cn2a):
            if conv_in:
                x = _short_conv_slab(vf_ref, vlo_ref, vhi_ref, vtaps_ref, j, cn2a,
                                     s == 0, s == ns1 - 1)
            elif vf_ref.dtype == jnp.uint32:
                x = _unpack_pairs(_load_minor(vf_ref, j)).astype(F32)
            else:
                x = _load_minor(vf_ref, j)
            keep_scr[base + j] = x
            r = _dot_rows(g_ref[base + j], x.astype(BF16))
            a_scr[pl.ds(base + j, n1h, stride=pitch), :] = _pack_pairs(r)

    @pl.when((s >= ns1) & (s < ns1 + ns2))
    def _():
        view = _spectrum_view(a_scr, s - ns1, ck1, pitch)
        for j in range(0, ck1, 2):
            rows = _spectrum_pair(view, j, n2len, pitch)
            u = jnp.dot(h2_ref[...], rows, preferred_element_type=F32)
            ur, ui = u[:n2len], u[n2len:]
            kf = jnp.concatenate([kf_ref[j], kf_ref[j + 1]], axis=1).astype(F32)
            kr, ki = kf[:n2len], kf[n2len:]
            y = jnp.concatenate([ur * kr - ui * ki, ur * ki + ui * kr], axis=0)
            bb = _pack_pairs(jnp.dot(h2i_ref[...], y.astype(BF16), preferred_element_type=F32))
            for jj in range(2):
                view[(j + jj) * pitch:(j + jj) * pitch + n2len, :] = bb[
                    :, jj * LANES:(jj + 1) * LANES]

    @pl.when(s >= ns1 + ns2)
    def _():
        base = (s - ns1 - ns2) * cn2
        for j in range(cn2):
            rows = _unpack_pairs(a_scr[pl.ds(base + j, n1h, stride=pitch), :])
            y = jnp.dot(g_ref[base + j], rows, preferred_element_type=F32)
            if conv_mult:
                xm = _short_conv_slab(xm_ref, xlo_ref, xhi_ref, xtaps_ref, j, cn2,
                                      s == ns1 + ns2, s == ns1 + ns2 + ns3 - 1)
            else:
                xm = _load_minor(xm_ref, j)
            z = xm * (y + hb_ref[...] * keep_scr[base + j])
            if o_ref.dtype == jnp.uint32:
                z = _pack_pairs(z)
            o_ref[:, j * SUBLANES:(j + 1) * SUBLANES, :] = z.reshape(
                z.shape[0] // SUBLANES, SUBLANES, z.shape[1])


def _long_conv(u_perm, mult_perm, kf, hbias, order_idx, tables, in_taps=None, mult_taps=None,
               pack_out=False):
    g_fwd, _, h2, h2i = tables
    bsz, _, cbn, _, _ = u_perm.shape
    n2len, n1h = g_fwd.shape[0], g_fwd.shape[1]
    steps = CONV_STEPS
    ns1, ns2, ns3 = steps
    cn2a, cn2 = n2len // ns1, n2len // ns3
    pitch = n2len + SUBLANES

    def early(s):
        return jnp.minimum(s, ns1 - 1)

    def late(s):
        return jnp.clip(s - ns1 - ns2, 0, ns3 - 1)

    def chunk_specs(chunk_of, width, groups):
        slab = (None, groups, None, SUBLANES, LANES)
        return [
            pl.BlockSpec((None, groups, None, width * SUBLANES, LANES),
                         lambda c, b, s: (b, 0, c, chunk_of(s), 0)),
            pl.BlockSpec(slab, lambda c, b, s: (b, 0, c, (chunk_of(s) * width - 1) % n2len, 0)),
            pl.BlockSpec(slab, lambda c, b, s: (b, 0, c, ((chunk_of(s) + 1) * width) % n2len, 0)),
        ]

    taps_spec = pl.BlockSpec((4, LANES), lambda c, b, s: (0, c))
    in_specs, args = [], []
    specs = chunk_specs(early, cn2a, u_perm.shape[1])
    if in_taps is None:
        in_specs += specs[:1]
        args += [u_perm]
    else:
        in_specs += specs + [taps_spec]
        args += [u_perm, u_perm, u_perm, in_taps]
    in_specs += [
        _resident(g_fwd.shape),
        pl.BlockSpec((n1h // ns2, 2 * n2len, LANES),
                     lambda c, b, s: (jnp.clip(s - ns1, 0, ns2 - 1), 0, order_idx * cbn + c)),
        _resident(h2.shape), _resident(h2i.shape),
    ]
    args += [g_fwd, kf, h2, h2i]
    specs = chunk_specs(late, cn2, mult_perm.shape[1])
    if mult_taps is None:
        in_specs += specs[:1]
        args += [mult_perm]
    else:
        in_specs += specs + [taps_spec]
        args += [mult_perm, mult_perm, mult_perm, mult_taps]
    in_specs.append(pl.BlockSpec((None, 1, LANES), lambda c, b, s: (order_idx, 0, c)))
    args.append(hbias)
    out_groups = n1h // SUBLANES // (2 if pack_out else 1)
    out_spec = chunk_specs(late, cn2, out_groups)[0]
    out_shape = jax.ShapeDtypeStruct((bsz, out_groups, cbn, n2len * SUBLANES, LANES),
                                     jnp.uint32 if pack_out else F32)
    kern = functools.partial(_long_conv_kernel, steps=steps, n1h=n1h, n2len=n2len,
                             pitch=pitch, conv_in=in_taps is not None,
                             conv_mult=mult_taps is not None)
    return pl.pallas_call(
        kern,
        grid=(cbn, bsz, sum(steps)),
        in_specs=in_specs,
        out_specs=out_spec,
        out_shape=out_shape,
        scratch_shapes=[pltpu.VMEM((n1h * pitch, LANES), jnp.uint32),
                        pltpu.VMEM((n2len, n1h, LANES), F32)],
        compiler_params=pltpu.CompilerParams(
            dimension_semantics=("parallel", "parallel", "arbitrary"),
            vmem_limit_bytes=CONV_VMEM_LIMIT),
        name="long_conv",
    )(*args)


def _out_kernel(x_ref, yh_ref, yp_ref, mod_ref, wf_ref, b_ref, g_ref, beta_ref, o_ref, w_ref,
                *, n2len, alpha):
    @pl.when((pl.program_id(0) == 0) & (pl.program_id(1) == 0))
    def _():
        w_ref[...] = wf_ref[...].astype(BF16)

    cbn = yh_ref.shape[0]
    yh = jnp.concatenate(
        [jnp.concatenate([yh_ref[cb, pl.ds(j, n2len, stride=SUBLANES), :]
                          for cb in range(cbn)], axis=1)
         for j in range(SUBLANES)], axis=0)
    y = jnp.concatenate([yh.astype(BF16), yp_ref[...]], axis=1)
    acc = jnp.dot(y, w_ref[...], preferred_element_type=F32)
    h = alpha * x_ref[...] + mod_ref[2:3, :] * (acc + b_ref[...])
    mu = jnp.mean(h, axis=-1, keepdims=True)
    hc = h - mu
    var = jnp.mean(hc * hc, axis=-1, keepdims=True)
    o_ref[...] = hc * lax.rsqrt(var + LN_EPS) * g_ref[...] + beta_ref[...]


def _output_projection(x, yh_perm, yp, mod3, w_out, b_out, ln_g, ln_b, alpha):
    bsz, seq, d = x.shape
    _, _, cbn, tile, _ = yh_perm.shape
    ch = cbn * LANES
    n2len = tile // SUBLANES
    dp = yp.shape[2]
    full = lambda shape: pl.BlockSpec(shape, lambda b, i: (0,) * len(shape))
    kern = functools.partial(_out_kernel, n2len=n2len, alpha=alpha)
    return pl.pallas_call(
        kern,
        grid=(bsz, seq // tile),
        in_specs=[
            pl.BlockSpec((None, tile, d), lambda b, i: (b, i, 0)),
            pl.BlockSpec((None, None, cbn, tile, LANES), lambda b, i: (b, i, 0, 0, 0)),
            pl.BlockSpec((None, tile, dp), lambda b, i: (b, i, 0)),
            pl.BlockSpec((None, 3, d), lambda b, i: (b, 0, 0)),
            _resident((ch + dp, d)), full((1, d)), full((1, d)), full((1, d)),
        ],
        out_specs=pl.BlockSpec((None, tile, d), lambda b, i: (b, i, 0)),
        out_shape=jax.ShapeDtypeStruct(x.shape, x.dtype),
        scratch_shapes=[pltpu.VMEM((ch + dp, d), BF16)],
        compiler_params=pltpu.CompilerParams(
            dimension_semantics=("arbitrary", "arbitrary"), vmem_limit_bytes=VMEM_LIMIT),
        name="out_proj_deepnorm",
    )(x, yh_perm, yp, mod3, w_out, b_out.reshape(1, d), ln_g.reshape(1, d),
      ln_b.reshape(1, d))


def _forward(x, c, w_ada, b_ada, w_in, b_in, conv_w, conv_b, filt_w1, filt_b1, filt_w_inner,
             filt_b_inner, filt_w_out, filt_freq, hyena_bias, pool_w, pool_scale, w_out,
             b_out, ln_g, ln_b, *, n2len):
    bsz, seq, d = x.shape
    depth = w_ada.shape[0]
    order, n_ch = hyena_bias.shape[1], hyena_bias.shape[2]
    alpha = (2.0 * depth) ** 0.25
    tables = _dft_tables(seq, n2len)
    h = x
    for layer in range(depth):
        mod3 = _modulation(c, w_ada[layer], b_ada[layer]).reshape(bsz, 3, d)
        v, x1, x2g, yp = _input_projection(
            h, mod3, w_in[layer], b_in[layer], conv_w[layer], conv_b[layer], pool_w[layer],
            pool_scale[layer], n2len)
        taps_f, taps_r = _filter_taps(
            seq, n2len, filt_w1[layer], filt_b1[layer], filt_w_inner[layer],
            filt_b_inner[layer], filt_w_out[layer], filt_freq[layer], n_ch, order)
        kf = _filter_spectrum(taps_f, taps_r, tables[0], tables[1], tables[2], n2len)
        hbias = hyena_bias[layer].reshape(order, 1, n_ch)
        taps = [jnp.concatenate([conv_w[layer][:, k * n_ch:(k + 1) * n_ch],
                                 conv_b[layer][None, k * n_ch:(k + 1) * n_ch]], axis=0)
                for k in range(2)]
        z = _long_conv(v, x1, kf, hbias, 0, tables, in_taps=taps[0], mult_taps=taps[1],
                       pack_out=order > 1)
        for o in range(1, order):
            z = _long_conv(z, x2g, kf, hbias, o, tables)
        h = _output_projection(h, z, yp, mod3, w_out[layer], b_out[layer], ln_g[layer],
                               ln_b[layer], alpha)
    return h


def kernel(x, c, w_ada, b_ada, w_in, b_in, conv_w, conv_b, filt_w1, filt_b1, filt_w_inner,
           filt_b_inner, filt_w_out, filt_freq, hyena_bias, pool_w, pool_scale, w_out, b_out,
           ln_g, ln_b):
    return _forward(x, c, w_ada, b_ada, w_in, b_in, conv_w, conv_b, filt_w1, filt_b1,
                    filt_w_inner, filt_b_inner, filt_w_out, filt_freq, hyena_bias, pool_w,
                    pool_scale, w_out, b_out, ln_g, ln_b, n2len=MINOR_LEN)
```

```python
import functools
import math

import jax
import jax.numpy as jnp
import numpy as np
from jax import lax
from jax.experimental import pallas as pl
from jax.experimental.pallas import tpu as pltpu

F32 = jnp.float32
BF16 = jnp.bfloat16
HIGHEST = lax.Precision.HIGHEST

POOL_WINDOWS = (2, 4, 8, 16)
LN_EPS = 1e-5
DECAY_TARGET = 1e-2
FAST_DECAY_PCT = 0.3
SLOW_DECAY_PCT = 1.5

LANES = 128
SUBLANES = 8
HALO = 16
BAND_ROWS = 128
BAND_SPAN = 256
MINOR_LEN = 128
FILTER_STEPS = (4, 4)
FILTER_COL_BLOCKS = 2
CONV_STEPS = (2, 2, 2)
STREAM_DEPTH = 3
V7X_VMEM_BYTES = 64 * 1024 * 1024
VMEM_LIMIT = V7X_VMEM_BYTES * 7 // 8
CONV_VMEM_LIMIT = V7X_VMEM_BYTES * 15 // 16


def _silu(x):
    h = 0.5 * x
    return h + h * jnp.tanh(h)


@functools.lru_cache(maxsize=None)
def _host_dft_tables(seq, n2len):
    n1h = seq // n2len
    n_fft = 2 * seq
    k1 = np.arange(n1h, dtype=np.int64)
    odd = 2 * k1 + 1
    alpha = ((odd[:, None] * k1[None, :]) % (4 * n1h)) * (2.0 * math.pi / (4 * n1h))
    n2e = np.arange(n2len + 1, dtype=np.int64)
    beta = ((n2e[:, None] * odd[None, :]) % (2 * n_fft)) * (2.0 * math.pi / (2 * n_fft))
    n2i = np.arange(n2len, dtype=np.int64)
    phi = ((n2i[:, None] * n2i[None, :]) % n2len) * (2.0 * math.pi / n2len)
    cm, sm = np.cos(phi), np.sin(phi)
    h2 = np.block([[cm, sm], [-sm, cm]])
    h2 = h2.reshape(2 * n2len, 2, n2len).transpose(0, 2, 1).reshape(2 * n2len, 2 * n2len)
    h2i = np.block([[cm, -sm], [sm, cm]])
    h2i = h2i.reshape(2, n2len, 2 * n2len).transpose(1, 0, 2).reshape(2 * n2len, 2 * n2len)
    h2, h2i = h2.astype(BF16), h2i.astype(BF16)
    quarter = 0.5 * math.pi * np.tile(np.array([0.0, 1.0]), n1h)[:, None]
    alpha2 = np.repeat(alpha, 2, axis=0)
    beta2 = np.repeat(beta, 2, axis=1)
    f32 = lambda a: a.astype(np.float32)
    seeds = tuple(f32(t) for t in (
        np.cos(alpha2 + quarter), np.sin(alpha2 + quarter),
        np.cos(alpha2 - quarter), np.sin(alpha2 - quarter), np.cos(beta2), np.sin(beta2)))
    return seeds + (h2, h2i)


def _dft_tables(seq, n2len):
    caf, saf, car, sar, cb, sb, h2, h2i = _host_dft_tables(seq, n2len)
    cb, sb = jnp.asarray(cb)[:, None, :], jnp.asarray(sb)[:, None, :]
    g_fwd = (caf.T[None] * cb[:-1] - saf.T[None] * sb[:-1]).astype(BF16)
    g_rev = (car.T[None] * cb[1:] - sar.T[None] * sb[1:]).astype(BF16)
    return g_fwd, g_rev, jnp.asarray(h2), jnp.asarray(h2i)


@functools.lru_cache(maxsize=None)
def _packed_pos_features(seq, emb, n2len):
    hl = LANES // 2
    step_rows = SUBLANES * n2len
    bands = (emb - 1) // 2
    row = np.arange(seq // 2, dtype=np.int64)[:, None]
    lane = np.arange(LANES, dtype=np.int64)[None, :]
    local = row % step_rows
    n1_local = 8 * (lane // hl) + 2 * (local // 2 // n2len) + local % 2
    pos = ((row // step_rows) * 2 * step_rows + n1_local * n2len
           + (local // 2) % n2len).astype(np.float64)
    feat = np.broadcast_to(lane % hl, pos.shape)
    t = pos / (seq - 1)
    w = (2.0 * math.pi / seq) * pos
    f = np.linspace(1e-4, bands - 1, bands)[(feat - 1) % bands]
    z = np.where(feat == 0, t,
                 np.where(feat <= bands, np.cos(f * w),
                          np.where(feat <= 2 * bands, -np.sin(f * w), 0.0)))
    return z.astype(np.float32)


def _mod_kernel(c_ref, w_ref, b_ref, o_ref):
    s = _silu(c_ref[...])
    o_ref[...] = jnp.dot(s, w_ref[...], precision=HIGHEST,
                         preferred_element_type=F32) + b_ref[...]


def _modulation(c, w_ada, b_ada):
    bsz, d = c.shape
    n_out = w_ada.shape[1]
    return pl.pallas_call(
        _mod_kernel,
        grid=(n_out // d,),
        in_specs=[pl.BlockSpec((bsz, d), lambda j: (0, 0)),
                  pl.BlockSpec((d, d), lambda j: (0, j)),
                  pl.BlockSpec((1, d), lambda j: (0, j))],
        out_specs=pl.BlockSpec((bsz, d), lambda j: (0, j)),
        out_shape=jax.ShapeDtypeStruct((bsz, n_out), F32),
        name="adaln_mod",
    )(c, w_ada, b_ada.reshape(1, n_out))


def _store_permuted(o_ref, val, n2len, first=0):
    for cb in range(val.shape[1] // LANES):
        for j in range(val.shape[0] // n2len):
            o_ref[cb, pl.ds(first + j, n2len, stride=SUBLANES), :] = val[
                j * n2len:(j + 1) * n2len, cb * LANES:(cb + 1) * LANES]


def _load_minor(x_ref, j):
    blk = x_ref[:, j * SUBLANES:(j + 1) * SUBLANES, :]
    return blk.reshape(blk.shape[0] * SUBLANES, blk.shape[2])


def _band_window(b, tile):
    last = tile + 2 * HALO - BAND_SPAN
    start = min(b * BAND_ROWS, last)
    return start, 0 if start == b * BAND_ROWS else 1


@functools.lru_cache(maxsize=None)
def _band_matrices(tile):
    t = np.arange(BAND_ROWS)[:, None]
    k = np.arange(BAND_SPAN)[None, :]
    mats = []
    for win in POOL_WINDOWS:
        half = win // 2
        per_variant = []
        for b in (0, tile // BAND_ROWS - 1):
            start, _ = _band_window(b, tile)
            rel = k + start - HALO - (b * BAND_ROWS + t)
            per_variant.append(((rel >= -half) & (rel < half)).astype(BF16))
        mats.append(np.stack(per_variant))
    return np.stack(mats)


def _proj_kernel(xp_ref, xc_ref, xn_ref, mod_ref, wf_ref, b_ref, cw_ref, cb_ref, pw_ref,
                 ps_ref, band_ref, v_ref, x1_ref, x2_ref, yp_ref, p_scr, q_scr, w_ref,
                 *, tile, n2len, ch, seq):
    i = pl.program_id(1)
    nt = pl.num_programs(1)

    @pl.when((pl.program_id(0) == 0) & (i == 0))
    def _():
        w_ref[...] = wf_ref[...].astype(BF16)

    shift = mod_ref[0:1, :]
    scale1 = 1.0 + mod_ref[1:2, :]
    xe = jnp.concatenate([xp_ref[...], xc_ref[...], xn_ref[...]], axis=0)
    ue = (xe * scale1 + shift).astype(BF16)
    uc = ue[HALO:HALO + tile, :]

    def zero_outside_sequence(p):
        return jnp.concatenate([jnp.where(i > 0, p[:HALO], 0.0), p[HALO:HALO + tile],
                                jnp.where(i < nt - 1, p[HALO + tile:], 0.0)], axis=0)

    for k, o_ref in enumerate((v_ref, x1_ref)):
        cols = slice(k * ch, (k + 1) * ch)
        p = jnp.dot(uc, w_ref[:, cols], preferred_element_type=F32) + b_ref[:, cols]
        _store_permuted(o_ref, p, n2len)

    cols = slice(2 * ch, 3 * ch)
    p = jnp.dot(ue, w_ref[:, cols], preferred_element_type=F32) + b_ref[:, cols]
    p_scr[...] = zero_outside_sequence(p)
    hg = jnp.dot(uc, w_ref[:, 3 * ch:4 * ch], preferred_element_type=F32) + b_ref[:, 3 * ch:4 * ch]
    gate = _silu(hg)
    s = (cb_ref[:, cols]
         + cw_ref[0:1, cols] * p_scr[HALO - 1:HALO - 1 + tile, :]
         + cw_ref[1:2, cols] * p_scr[HALO:HALO + tile, :]
         + cw_ref[2:3, cols] * p_scr[HALO + 1:HALO + 1 + tile, :])
    _store_permuted(x2_ref, s * gate, n2len)

    dp = ps_ref.shape[1]
    c0 = 4 * ch
    pin = jnp.dot(ue, w_ref[:, c0:c0 + dp], preferred_element_type=F32) + b_ref[:, c0:c0 + dp]
    pin = zero_outside_sequence(pin)
    q_scr[...] = pin.astype(BF16)
    p_scr[...] = pin
    pos = i * tile + lax.broadcasted_iota(jnp.int32, (tile, 1), 0)
    pg = dp // len(POOL_WINDOWS)
    groups = []
    for g, win in enumerate(POOL_WINDOWS):
        lanes = slice(g * pg, (g + 1) * pg)
        half = win // 2
        sums = []
        for b in range(tile // BAND_ROWS):
            start, variant = _band_window(b, tile)
            sums.append(jnp.dot(band_ref[g, variant], q_scr[start:start + BAND_SPAN, lanes],
                                preferred_element_type=F32))
        acc = jnp.concatenate(sums, axis=0)
        cnt = (jnp.minimum(pos + half, seq) - jnp.maximum(pos - half, 0)).astype(F32)
        diff = acc / cnt - p_scr[HALO:HALO + tile, lanes]
        groups.append(jnp.dot(diff.astype(BF16), pw_ref[g], preferred_element_type=F32))
    pgate = jnp.dot(uc, w_ref[:, c0 + dp:c0 + 2 * dp], preferred_element_type=F32) + b_ref[:, c0 + dp:c0 + 2 * dp]
    yp = jnp.concatenate(groups, axis=1) * ps_ref[...] * _silu(pgate)
    yp_ref[...] = yp.astype(yp_ref.dtype)


def _input_projection(x, mod3, w_in, b_in, conv_w, conv_b, pool_w, pool_scale, n2len):
    bsz, seq, d = x.shape
    ch = conv_w.shape[1] // 3
    dp = pool_scale.shape[0]
    n1h = seq // n2len
    tile = SUBLANES * n2len
    nt = seq // tile
    hb = tile // HALO
    n_proj = w_in.shape[1]
    cbn = ch // LANES
    assert max(POOL_WINDOWS) // 2 <= HALO and dp == ch and tile % BAND_ROWS == 0
    band = jnp.asarray(_band_matrices(tile))
    perm = jax.ShapeDtypeStruct((bsz, n1h // SUBLANES, cbn, tile, LANES), F32)
    perm_spec = pl.BlockSpec((None, None, cbn, tile, LANES), lambda b, i: (b, i, 0, 0, 0))
    full = lambda shape: pl.BlockSpec(shape, lambda b, i: (0,) * len(shape))
    kern = functools.partial(_proj_kernel, tile=tile, n2len=n2len, ch=ch, seq=seq)
    return pl.pallas_call(
        kern,
        grid=(bsz, nt),
        in_specs=[
            pl.BlockSpec((None, HALO, d), lambda b, i: (b, jnp.maximum(i * hb - 1, 0), 0)),
            pl.BlockSpec((None, tile, d), lambda b, i: (b, i, 0)),
            pl.BlockSpec((None, HALO, d), lambda b, i: (b, jnp.minimum((i + 1) * hb, seq // HALO - 1), 0)),
            pl.BlockSpec((None, 3, d), lambda b, i: (b, 0, 0)),
            _resident((d, n_proj)), full((1, n_proj)), full((3, 3 * ch)), full((1, 3 * ch)),
            full(pool_w.shape), full((1, dp)), full(band.shape),
        ],
        out_specs=[perm_spec, perm_spec, perm_spec,
                   pl.BlockSpec((None, tile, dp), lambda b, i: (b, i, 0))],
        out_shape=[perm, perm, perm, jax.ShapeDtypeStruct((bsz, seq, dp), BF16)],
        scratch_shapes=[pltpu.VMEM((tile + 2 * HALO, ch), F32),
                        pltpu.VMEM((tile + 2 * HALO, dp), BF16),
                        pltpu.VMEM((d, n_proj), BF16)],
        compiler_params=pltpu.CompilerParams(
            dimension_semantics=("arbitrary", "arbitrary"), vmem_limit_bytes=VMEM_LIMIT),
        name="in_proj_conv_pool",
    )(x, x, x, mod3, w_in, b_in.reshape(1, n_proj), conv_w,
      conv_b.reshape(1, 3 * ch), pool_w.astype(BF16), pool_scale.reshape(1, dp), band)


def _filter_mlp_kernel(z_ref, w1_ref, b1_ref, wi_ref, bi_ref, fr_ref, wo_ref, ad_ref,
                       of_ref, or_ref, *, n2len):
    z = z_ref[...]
    fr = fr_ref[...]
    h = jnp.sin(fr * (jnp.dot(z.astype(BF16), w1_ref[...],
                              preferred_element_type=F32) + b1_ref[...]))
    for l in range(wi_ref.shape[0]):
        h = jnp.sin(fr * (jnp.dot(h.astype(BF16), wi_ref[l],
                                  preferred_element_type=F32) + bi_ref[l:l + 1, :]))
    hb = h.astype(BF16)
    half_lanes = LANES // 2
    pairs_per_half = z.shape[0] // 2 // n2len
    for half in range(2):
        t = z[:, half * half_lanes:half * half_lanes + 1]
        decay = jnp.exp(-t * ad_ref[...])
        for d, o_ref in enumerate((of_ref, or_ref)):
            k = jnp.dot(hb, wo_ref[d, half], preferred_element_type=F32)
            words = _pack_pairs(k * decay)
            _store_permuted(o_ref, words, n2len, first=half * pairs_per_half)


def _block_diag2(w):
    zero = jnp.zeros_like(w)
    return jnp.concatenate([jnp.concatenate([w, zero], axis=-1),
                            jnp.concatenate([zero, w], axis=-1)], axis=-2)


def _filter_taps(seq, n2len, w1, b1, w_inner, b_inner, w_out, freq, n_ch, order):
    emb, hid = w1.shape
    n_inner = w_inner.shape[0]
    oc = order * n_ch
    hl = LANES // 2
    assert emb <= hl and hid <= hl
    tile = SUBLANES * n2len
    groups = seq // (2 * tile)
    zp = jnp.asarray(_packed_pos_features(seq, emb, n2len))
    ph = hl - hid
    w1d = _block_diag2(jnp.pad(w1, ((0, hl - emb), (0, ph))))
    b1d = jnp.tile(jnp.pad(b1, (0, ph)), 2).reshape(1, LANES)
    wid = _block_diag2(jnp.pad(w_inner, ((0, 0), (0, ph), (0, ph))))
    bid = jnp.tile(jnp.pad(b_inner, ((0, 0), (0, ph))), (1, 2))
    frd = jnp.tile(jnp.pad(freq, (0, ph), constant_values=1.0), 2).reshape(1, LANES)
    wo = jnp.transpose(w_out.reshape(hid, order, 2, n_ch), (2, 0, 1, 3)).reshape(2, hid, oc)
    wo = jnp.pad(wo, ((0, 0), (0, ph), (0, 0)))
    zero = jnp.zeros_like(wo)
    wo4 = jnp.stack([jnp.concatenate([wo, zero], axis=1),
                     jnp.concatenate([zero, wo], axis=1)], axis=1)
    min_decay = math.log(DECAY_TARGET) / SLOW_DECAY_PCT
    max_decay = math.log(DECAY_TARGET) / FAST_DECAY_PCT
    absdelta = jnp.abs(jnp.linspace(min_decay, max_decay, n_ch, dtype=F32))
    absdelta = jnp.tile(absdelta, order).reshape(1, oc)

    full = lambda shape: pl.BlockSpec(shape, lambda i: (0,) * len(shape))
    kern = functools.partial(_filter_mlp_kernel, n2len=n2len)
    taps = jax.ShapeDtypeStruct((groups, oc // LANES, tile, LANES), jnp.uint32)
    taps_spec = pl.BlockSpec((None, oc // LANES, tile, LANES), lambda i: (i, 0, 0, 0))
    return pl.pallas_call(
        kern,
        grid=(groups,),
        in_specs=[
            pl.BlockSpec((tile, LANES), lambda i: (i, 0)),
            full((LANES, LANES)), full((1, LANES)), full((n_inner, LANES, LANES)),
            full((n_inner, LANES)), full((1, LANES)), full((2, 2, LANES, oc)), full((1, oc)),
        ],
        out_specs=[taps_spec, taps_spec],
        out_shape=[taps, taps],
        compiler_params=pltpu.CompilerParams(
            dimension_semantics=("arbitrary",), vmem_limit_bytes=VMEM_LIMIT),
        name="filter_mlp",
    )(zp, w1d.astype(BF16), b1d, wid.astype(BF16), bid, frd, wo4.astype(BF16), absdelta)


def _resident(shape):
    return pl.BlockSpec(shape, lambda *_: (0,) * len(shape), pipeline_mode=pl.Buffered(1))


def _dot_rows(a, b):
    return lax.dot_general(a, b, (((0,), (0,)), ((), ())), preferred_element_type=F32)


def _pack_pairs(x):
    return pltpu.bitcast(x.astype(BF16), jnp.uint32)


def _unpack_pairs(w):
    return pltpu.bitcast(w, BF16)


def _spectrum_view(a_scr, chunk, ck1, pitch):
    start = pl.multiple_of(chunk * (ck1 * pitch), SUBLANES)
    return a_scr.at[pl.ds(start, ck1 * pitch)]


def _spectrum_pair(view, j, n2len, pitch):
    return jnp.concatenate([_unpack_pairs(view[jj * pitch:jj * pitch + n2len, :])
                            for jj in (j, j + 1)], axis=1)


def _filter_fft_kernel(hf_ref, hr_ref, g_ref, gr_ref, h2_ref, o_ref, a_scr,
                       *, ns, cn2, ck1, n1h, n2len, pitch):
    s = pl.program_id(1)
    ncb = a_scr.shape[0]

    @pl.when(s < ns)
    def _():
        for j in range(cn2):
            jr = cn2 - 1 - j
            m = s * cn2 + j
            g = jnp.concatenate([g_ref[m], gr_ref[n2len - 1 - m]], axis=0)
            x = jnp.concatenate(
                [jnp.concatenate([_unpack_pairs(_load_minor(hf_ref.at[:, cb], j)),
                                  _unpack_pairs(_load_minor(hr_ref.at[:, cb], jr))], axis=0)
                 for cb in range(ncb)], axis=1)
            packed = _pack_pairs(_dot_rows(g, x))
            for cb in range(ncb):
                a_scr[cb, pl.ds(m, n1h, stride=pitch), :] = packed[:, cb * LANES:(cb + 1) * LANES]

    @pl.when(s >= ns)
    def _():
        for cb in range(ncb):
            view = _spectrum_view(a_scr.at[cb], s - ns, ck1, pitch)
            lanes = slice(cb * LANES, (cb + 1) * LANES)
            for j in range(0, ck1, 2):
                rows = _spectrum_pair(view, j, n2len, pitch)
                u = jnp.dot(h2_ref[...], rows, preferred_element_type=F32)
                u = (u * (1.0 / (n1h * n2len))).astype(o_ref.dtype)
                o_ref[j, :, lanes] = u[:, :LANES]
                o_ref[j + 1, :, lanes] = u[:, LANES:]


def _filter_spectrum(taps_f, taps_r, g_fwd, g_rev, h2, n2len):
    groups, ocb, _, _ = taps_f.shape
    oc = ocb * LANES
    n1h = groups * 2 * SUBLANES
    ns, ns2 = FILTER_STEPS
    cn2, ck1 = n2len // ns, n1h // ns2
    pitch = n2len + SUBLANES
    kern = functools.partial(_filter_fft_kernel, ns=ns, cn2=cn2, ck1=ck1, n1h=n1h,
                             n2len=n2len, pitch=pitch)
    ncb = FILTER_COL_BLOCKS
    return pl.pallas_call(
        kern,
        grid=(ocb // ncb, ns + ns2),
        in_specs=[
            pl.BlockSpec((groups, ncb, cn2 * SUBLANES, LANES),
                         lambda c, s: (0, c, jnp.minimum(s, ns - 1), 0)),
            pl.BlockSpec((groups, ncb, cn2 * SUBLANES, LANES),
                         lambda c, s: (0, c, jnp.maximum(ns - 1 - s, 0), 0)),
            _resident(g_fwd.shape), _resident(g_rev.shape), _resident(h2.shape),
        ],
        out_specs=pl.BlockSpec((ck1, 2 * n2len, ncb * LANES),
                               lambda c, s: (jnp.maximum(s - ns, 0), 0, c)),
        out_shape=jax.ShapeDtypeStruct((n1h, 2 * n2len, oc), BF16),
        scratch_shapes=[pltpu.VMEM((ncb, n1h * pitch, LANES), jnp.uint32)],
        compiler_params=pltpu.CompilerParams(
            dimension_semantics=("parallel", "arbitrary"), vmem_limit_bytes=CONV_VMEM_LIMIT),
        name="filter_spectrum",
    )(taps_f, taps_r, g_fwd, g_rev, h2)


def _short_conv_slab(x_ref, lo_ref, hi_ref, taps_ref, j, count, first_chunk, last_chunk):
    cur = _load_minor(x_ref, j)
    zero_row = jnp.zeros((1, cur.shape[1]), F32)
    if j > 0:
        prev = _load_minor(x_ref, j - 1)
    else:
        prev = _load_minor(lo_ref, 0)
        prev = jnp.where(first_chunk, jnp.concatenate([zero_row, prev[:-1]], axis=0), prev)
    if j < count - 1:
        nxt = _load_minor(x_ref, j + 1)
    else:
        nxt = _load_minor(hi_ref, 0)
        nxt = jnp.where(last_chunk, jnp.concatenate([nxt[1:], zero_row], axis=0), nxt)
    return (taps_ref[3:4, :] + taps_ref[0:1, :] * prev + taps_ref[1:2, :] * cur
            + taps_ref[2:3, :] * nxt)


def _long_conv_kernel(*refs, steps, n1h, n2len, pitch, conv_in, conv_mult):
    refs = list(refs)
    vf_ref = refs.pop(0)
    vlo_ref, vhi_ref, vtaps_ref = (refs.pop(0), refs.pop(0), refs.pop(0)) if conv_in else (None,) * 3
    g_ref, kf_ref, h2_ref, h2i_ref, xm_ref = (refs.pop(0) for _ in range(5))
    xlo_ref, xhi_ref, xtaps_ref = (refs.pop(0), refs.pop(0), refs.pop(0)) if conv_mult else (None,) * 3
    hb_ref, o_ref, a_scr, keep_scr = refs
    ns1, ns2, ns3 = steps
    cn2a, ck1, cn2 = n2len // ns1, n1h // ns2, n2len // ns3
    s = pl.program_id(2)

    @pl.when(s < ns1)
    def _():
        base = s * cn2a
        for j in range(cn2a):
            if conv_in:
                x = _short_conv_slab(vf_ref, vlo_ref, vhi_ref, vtaps_ref, j, cn2a,
                                     s == 0, s == ns1 - 1)
            elif vf_ref.dtype == jnp.uint32:
                x = _unpack_pairs(_load_minor(vf_ref, j)).astype(F32)
            else:
                x = _load_minor(vf_ref, j)
            keep_scr[base + j] = x
            r = _dot_rows(g_ref[base + j], x.astype(BF16))
            a_scr[pl.ds(base + j, n1h, stride=pitch), :] = _pack_pairs(r)

    @pl.when((s >= ns1) & (s < ns1 + ns2))
    def _():
        view = _spectrum_view(a_scr, s - ns1, ck1, pitch)
        for j in range(0, ck1, 2):
            rows = _spectrum_pair(view, j, n2len, pitch)
            u = jnp.dot(h2_ref[...], rows, preferred_element_type=F32)
            ur, ui = u[:n2len], u[n2len:]
            kf = jnp.concatenate([kf_ref[j], kf_ref[j + 1]], axis=1).astype(F32)
            kr, ki = kf[:n2len], kf[n2len:]
            y = jnp.concatenate([ur * kr - ui * ki, ur * ki + ui * kr], axis=0)
            bb = _pack_pairs(jnp.dot(h2i_ref[...], y.astype(BF16), preferred_element_type=F32))
            for jj in range(2):
                view[(j + jj) * pitch:(j + jj) * pitch + n2len, :] = bb[
                    :, jj * LANES:(jj + 1) * LANES]

    @pl.when(s >= ns1 + ns2)
    def _():
        base = (s - ns1 - ns2) * cn2
        for j in range(cn2):
            rows = _unpack_pairs(a_scr[pl.ds(base + j, n1h, stride=pitch), :])
            y = jnp.dot(g_ref[base + j], rows, preferred_element_type=F32)
            if conv_mult:
                xm = _short_conv_slab(xm_ref, xlo_ref, xhi_ref, xtaps_ref, j, cn2,
                                      s == ns1 + ns2, s == ns1 + ns2 + ns3 - 1)
            else:
                xm = _load_minor(xm_ref, j)
            z = xm * (y + hb_ref[...] * keep_scr[base + j])
            if o_ref.dtype == jnp.uint32:
                z = _pack_pairs(z)
            o_ref[:, j * SUBLANES:(j + 1) * SUBLANES, :] = z.reshape(
                z.shape[0] // SUBLANES, SUBLANES, z.shape[1])


def _long_conv(u_perm, mult_perm, kf, hbias, order_idx, tables, in_taps=None, mult_taps=None,
               pack_out=False):
    g_fwd, _, h2, h2i = tables
    bsz, _, cbn, _, _ = u_perm.shape
    n2len, n1h = g_fwd.shape[0], g_fwd.shape[1]
    steps = CONV_STEPS
    ns1, ns2, ns3 = steps
    cn2a, cn2 = n2len // ns1, n2len // ns3
    pitch = n2len + SUBLANES

    def early(s):
        return jnp.minimum(s, ns1 - 1)

    def late(s):
        return jnp.clip(s - ns1 - ns2, 0, ns3 - 1)

    def chunk_specs(chunk_of, width, groups):
        slab = (None, groups, None, SUBLANES, LANES)
        return [
            pl.BlockSpec((None, groups, None, width * SUBLANES, LANES),
                         lambda c, b, s: (b, 0, c, chunk_of(s), 0)),
            pl.BlockSpec(slab, lambda c, b, s: (b, 0, c, (chunk_of(s) * width - 1) % n2len, 0)),
            pl.BlockSpec(slab, lambda c, b, s: (b, 0, c, ((chunk_of(s) + 1) * width) % n2len, 0)),
        ]

    taps_spec = pl.BlockSpec((4, LANES), lambda c, b, s: (0, c))
    in_specs, args = [], []
    specs = chunk_specs(early, cn2a, u_perm.shape[1])
    if in_taps is None:
        in_specs += specs[:1]
        args += [u_perm]
    else:
        in_specs += specs + [taps_spec]
        args += [u_perm, u_perm, u_perm, in_taps]
    in_specs += [
        _resident(g_fwd.shape),
        pl.BlockSpec((n1h // ns2, 2 * n2len, LANES),
                     lambda c, b, s: (jnp.clip(s - ns1, 0, ns2 - 1), 0, order_idx * cbn + c)),
        _resident(h2.shape), _resident(h2i.shape),
    ]
    args += [g_fwd, kf, h2, h2i]
    specs = chunk_specs(late, cn2, mult_perm.shape[1])
    if mult_taps is None:
        in_specs += specs[:1]
        args += [mult_perm]
    else:
        in_specs += specs + [taps_spec]
        args += [mult_perm, mult_perm, mult_perm, mult_taps]
    in_specs.append(pl.BlockSpec((None, 1, LANES), lambda c, b, s: (order_idx, 0, c)))
    args.append(hbias)
    out_groups = n1h // SUBLANES // (2 if pack_out else 1)
    out_spec = chunk_specs(late, cn2, out_groups)[0]
    out_shape = jax.ShapeDtypeStruct((bsz, out_groups, cbn, n2len * SUBLANES, LANES),
                                     jnp.uint32 if pack_out else F32)
    kern = functools.partial(_long_conv_kernel, steps=steps, n1h=n1h, n2len=n2len,
                             pitch=pitch, conv_in=in_taps is not None,
                             conv_mult=mult_taps is not None)
    return pl.pallas_call(
        kern,
        grid=(cbn, bsz, sum(steps)),
        in_specs=in_specs,
        out_specs=out_spec,
        out_shape=out_shape,
        scratch_shapes=[pltpu.VMEM((n1h * pitch, LANES), jnp.uint32),
                        pltpu.VMEM((n2len, n1h, LANES), F32)],
        compiler_params=pltpu.CompilerParams(
            dimension_semantics=("parallel", "parallel", "arbitrary"),
            vmem_limit_bytes=CONV_VMEM_LIMIT),
        name="long_conv",
    )(*args)


def _out_kernel(x_hbm, yh_ref, yp_ref, mod_ref, w_ref, b_ref, g_ref, beta_ref, o_ref,
                x_buf, x_sem, *, n2len, alpha):
    nt = pl.num_programs(1)
    total = pl.num_programs(0) * nt
    t = pl.program_id(0) * nt + pl.program_id(1)
    tile = x_buf.shape[1]

    def x_copy(step):
        slot = step % STREAM_DEPTH
        return pltpu.make_async_copy(
            x_hbm.at[step // nt, pl.ds((step % nt) * tile, tile), :], x_buf.at[slot],
            x_sem.at[slot])

    @pl.when(t == 0)
    def _():
        for k in range(STREAM_DEPTH - 1):
            x_copy(k).start()

    @pl.when(t + STREAM_DEPTH - 1 < total)
    def _():
        x_copy(t + STREAM_DEPTH - 1).start()

    x_copy(t).wait()
    x_ref = x_buf.at[t % STREAM_DEPTH]
    cbn = yh_ref.shape[0]
    yh = jnp.concatenate(
        [jnp.concatenate([yh_ref[cb, pl.ds(j, n2len, stride=SUBLANES), :]
                          for cb in range(cbn)], axis=1)
         for j in range(SUBLANES)], axis=0)
    y = jnp.concatenate([yh.astype(BF16), yp_ref[...]], axis=1)
    acc = jnp.dot(y, w_ref[...], preferred_element_type=F32)
    h = alpha * x_ref[...] + mod_ref[2:3, :] * (acc + b_ref[...])
    mu = jnp.mean(h, axis=-1, keepdims=True)
    hc = h - mu
    var = jnp.mean(hc * hc, axis=-1, keepdims=True)
    o_ref[...] = hc * lax.rsqrt(var + LN_EPS) * g_ref[...] + beta_ref[...]


def _output_projection(x, yh_perm, yp, mod3, w_out, b_out, ln_g, ln_b, alpha):
    bsz, seq, d = x.shape
    _, _, cbn, tile, _ = yh_perm.shape
    ch = cbn * LANES
    n2len = tile // SUBLANES
    dp = yp.shape[2]
    full = lambda shape: pl.BlockSpec(shape, lambda b, i: (0,) * len(shape))
    kern = functools.partial(_out_kernel, n2len=n2len, alpha=alpha)
    return pl.pallas_call(
        kern,
        grid=(bsz, seq // tile),
        in_specs=[
            pl.BlockSpec(memory_space=pl.ANY),
            pl.BlockSpec((None, None, cbn, tile, LANES), lambda b, i: (b, i, 0, 0, 0)),
            pl.BlockSpec((None, tile, dp), lambda b, i: (b, i, 0)),
            pl.BlockSpec((None, 3, d), lambda b, i: (b, 0, 0)),
            full((ch + dp, d)), full((1, d)), full((1, d)), full((1, d)),
        ],
        out_specs=pl.BlockSpec((None, tile, d), lambda b, i: (b, i, 0)),
        out_shape=jax.ShapeDtypeStruct(x.shape, x.dtype),
        scratch_shapes=[pltpu.VMEM((STREAM_DEPTH, tile, d), x.dtype),
                        pltpu.SemaphoreType.DMA((STREAM_DEPTH,))],
        compiler_params=pltpu.CompilerParams(
            dimension_semantics=("arbitrary", "arbitrary"), vmem_limit_bytes=VMEM_LIMIT),
        name="out_proj_deepnorm",
    )(x, yh_perm, yp, mod3, w_out.astype(BF16), b_out.reshape(1, d), ln_g.reshape(1, d),
      ln_b.reshape(1, d))


def _forward(x, c, w_ada, b_ada, w_in, b_in, conv_w, conv_b, filt_w1, filt_b1, filt_w_inner,
             filt_b_inner, filt_w_out, filt_freq, hyena_bias, pool_w, pool_scale, w_out,
             b_out, ln_g, ln_b, *, n2len):
    bsz, seq, d = x.shape
    depth = w_ada.shape[0]
    order, n_ch = hyena_bias.shape[1], hyena_bias.shape[2]
    alpha = (2.0 * depth) ** 0.25
    tables = _dft_tables(seq, n2len)
    h = x
    for layer in range(depth):
        mod3 = _modulation(c, w_ada[layer], b_ada[layer]).reshape(bsz, 3, d)
        v, x1, x2g, yp = _input_projection(
            h, mod3, w_in[layer], b_in[layer], conv_w[layer], conv_b[layer], pool_w[layer],
            pool_scale[layer], n2len)
        taps_f, taps_r = _filter_taps(
            seq, n2len, filt_w1[layer], filt_b1[layer], filt_w_inner[layer],
            filt_b_inner[layer], filt_w_out[layer], filt_freq[layer], n_ch, order)
        kf = _filter_spectrum(taps_f, taps_r, tables[0], tables[1], tables[2], n2len)
        hbias = hyena_bias[layer].reshape(order, 1, n_ch)
        taps = [jnp.concatenate([conv_w[layer][:, k * n_ch:(k + 1) * n_ch],
                                 conv_b[layer][None, k * n_ch:(k + 1) * n_ch]], axis=0)
                for k in range(2)]
        z = _long_conv(v, x1, kf, hbias, 0, tables, in_taps=taps[0], mult_taps=taps[1],
                       pack_out=order > 1)
        for o in range(1, order):
            z = _long_conv(z, x2g, kf, hbias, o, tables)
        h = _output_projection(h, z, yp, mod3, w_out[layer], b_out[layer], ln_g[layer],
                               ln_b[layer], alpha)
    return h


def kernel(x, c, w_ada, b_ada, w_in, b_in, conv_w, conv_b, filt_w1, filt_b1, filt_w_inner,
           filt_b_inner, filt_w_out, filt_freq, hyena_bias, pool_w, pool_scale, w_out, b_out,
           ln_g, ln_b):
    return _forward(x, c, w_ada, b_ada, w_in, b_in, conv_w, conv_b, filt_w1, filt_b1,
                    filt_w_inner, filt_b_inner, filt_w_out, filt_freq, hyena_bias, pool_w,
                    pool_scale, w_out, b_out, ln_g, ln_b, n2len=MINOR_LEN)
```

```python
import functools
import math

import jax
import jax.numpy as jnp
import numpy as np
from jax import lax
from jax.experimental import pallas as pl
from jax.experimental.pallas import tpu as pltpu

F32 = jnp.float32
BF16 = jnp.bfloat16
HIGHEST = lax.Precision.HIGHEST

POOL_WINDOWS = (2, 4, 8, 16)
LN_EPS = 1e-5
DECAY_TARGET = 1e-2
FAST_DECAY_PCT = 0.3
SLOW_DECAY_PCT = 1.5

LANES = 128
SUBLANES = 8
HALO = 16
BAND_ROWS = 128
BAND_SPAN = 256
MINOR_LEN = 128
FILTER_STEPS = (4, 4)
FILTER_COL_BLOCKS = 2
CONV_STEPS = (2, 2, 2)
STREAM_DEPTH = 3
V7X_VMEM_BYTES = 64 * 1024 * 1024
VMEM_LIMIT = V7X_VMEM_BYTES * 7 // 8
CONV_VMEM_LIMIT = V7X_VMEM_BYTES * 15 // 16


def _silu(x):
    h = 0.5 * x
    return h + h * jnp.tanh(h)


@functools.lru_cache(maxsize=None)
def _host_dft_tables(seq, n2len):
    n1h = seq // n2len
    n_fft = 2 * seq
    k1 = np.arange(n1h, dtype=np.int64)
    odd = 2 * k1 + 1
    alpha = ((odd[:, None] * k1[None, :]) % (4 * n1h)) * (2.0 * math.pi / (4 * n1h))
    n2e = np.arange(n2len + 1, dtype=np.int64)
    beta = ((n2e[:, None] * odd[None, :]) % (2 * n_fft)) * (2.0 * math.pi / (2 * n_fft))
    n2i = np.arange(n2len, dtype=np.int64)
    phi = ((n2i[:, None] * n2i[None, :]) % n2len) * (2.0 * math.pi / n2len)
    cm, sm = np.cos(phi), np.sin(phi)
    h2 = np.block([[cm, sm], [-sm, cm]])
    h2 = h2.reshape(2 * n2len, 2, n2len).transpose(0, 2, 1).reshape(2 * n2len, 2 * n2len)
    h2i = np.block([[cm, -sm], [sm, cm]])
    h2i = h2i.reshape(2, n2len, 2 * n2len).transpose(1, 0, 2).reshape(2 * n2len, 2 * n2len)
    h2, h2i = h2.astype(BF16), h2i.astype(BF16)
    quarter = 0.5 * math.pi * np.tile(np.array([0.0, 1.0]), n1h)[:, None]
    alpha2 = np.repeat(alpha, 2, axis=0)
    beta2 = np.repeat(beta, 2, axis=1)
    f32 = lambda a: a.astype(np.float32)
    seeds = tuple(f32(t) for t in (
        np.cos(alpha2 + quarter), np.sin(alpha2 + quarter),
        np.cos(alpha2 - quarter), np.sin(alpha2 - quarter), np.cos(beta2), np.sin(beta2)))
    return seeds + (h2, h2i)


def _dft_tables(seq, n2len):
    caf, saf, car, sar, cb, sb, h2, h2i = _host_dft_tables(seq, n2len)
    cb, sb = jnp.asarray(cb)[:, None, :], jnp.asarray(sb)[:, None, :]
    g_fwd = (caf.T[None] * cb[:-1] - saf.T[None] * sb[:-1]).astype(BF16)
    g_rev = (car.T[None] * cb[1:] - sar.T[None] * sb[1:]).astype(BF16)
    return g_fwd, g_rev, jnp.asarray(h2), jnp.asarray(h2i)


@functools.lru_cache(maxsize=None)
def _packed_pos_features(seq, emb, n2len):
    hl = LANES // 2
    step_rows = SUBLANES * n2len
    bands = (emb - 1) // 2
    row = np.arange(seq // 2, dtype=np.int64)[:, None]
    lane = np.arange(LANES, dtype=np.int64)[None, :]
    local = row % step_rows
    n1_local = 8 * (lane // hl) + 2 * (local // 2 // n2len) + local % 2
    pos = ((row // step_rows) * 2 * step_rows + n1_local * n2len
           + (local // 2) % n2len).astype(np.float64)
    feat = np.broadcast_to(lane % hl, pos.shape)
    t = pos / (seq - 1)
    w = (2.0 * math.pi / seq) * pos
    f = np.linspace(1e-4, bands - 1, bands)[(feat - 1) % bands]
    z = np.where(feat == 0, t,
                 np.where(feat <= bands, np.cos(f * w),
                          np.where(feat <= 2 * bands, -np.sin(f * w), 0.0)))
    return z.astype(np.float32)


def _mod_kernel(c_ref, w_ref, b_ref, o_ref):
    s = _silu(c_ref[...])
    o_ref[...] = jnp.dot(s, w_ref[...], precision=HIGHEST,
                         preferred_element_type=F32) + b_ref[...]


def _modulation(c, w_ada, b_ada):
    bsz, d = c.shape
    n_out = w_ada.shape[1]
    return pl.pallas_call(
        _mod_kernel,
        grid=(n_out // d,),
        in_specs=[pl.BlockSpec((bsz, d), lambda j: (0, 0)),
                  pl.BlockSpec((d, d), lambda j: (0, j)),
                  pl.BlockSpec((1, d), lambda j: (0, j))],
        out_specs=pl.BlockSpec((bsz, d), lambda j: (0, j)),
        out_shape=jax.ShapeDtypeStruct((bsz, n_out), F32),
        name="adaln_mod",
    )(c, w_ada, b_ada.reshape(1, n_out))


def _store_permuted(o_ref, val, n2len, first=0):
    for cb in range(val.shape[1] // LANES):
        for j in range(val.shape[0] // n2len):
            o_ref[cb, pl.ds(first + j, n2len, stride=SUBLANES), :] = val[
                j * n2len:(j + 1) * n2len, cb * LANES:(cb + 1) * LANES]


def _load_minor(x_ref, j):
    blk = x_ref[:, j * SUBLANES:(j + 1) * SUBLANES, :]
    return blk.reshape(blk.shape[0] * SUBLANES, blk.shape[2])


def _band_window(b, tile):
    last = tile + 2 * HALO - BAND_SPAN
    start = min(b * BAND_ROWS, last)
    return start, 0 if start == b * BAND_ROWS else 1


@functools.lru_cache(maxsize=None)
def _band_matrices(tile):
    t = np.arange(BAND_ROWS)[:, None]
    k = np.arange(BAND_SPAN)[None, :]
    mats = []
    for win in POOL_WINDOWS:
        half = win // 2
        per_variant = []
        for b in (0, tile // BAND_ROWS - 1):
            start, _ = _band_window(b, tile)
            rel = k + start - HALO - (b * BAND_ROWS + t)
            per_variant.append(((rel >= -half) & (rel < half)).astype(BF16))
        mats.append(np.stack(per_variant))
    return np.stack(mats)


def _proj_kernel(xp_ref, xc_ref, xn_ref, mod_ref, wf_ref, b_ref, cw_ref, cb_ref, pw_ref,
                 ps_ref, band_ref, v_ref, x1_ref, x2_ref, yp_ref, p_scr, q_scr, w_ref,
                 *, tile, n2len, ch, seq):
    i = pl.program_id(1)
    nt = pl.num_programs(1)

    @pl.when((pl.program_id(0) == 0) & (i == 0))
    def _():
        w_ref[...] = wf_ref[...].astype(BF16)

    shift = mod_ref[0:1, :]
    scale1 = 1.0 + mod_ref[1:2, :]
    xe = jnp.concatenate([xp_ref[...], xc_ref[...], xn_ref[...]], axis=0)
    ue = (xe * scale1 + shift).astype(BF16)
    uc = ue[HALO:HALO + tile, :]

    def zero_outside_sequence(p):
        return jnp.concatenate([jnp.where(i > 0, p[:HALO], 0.0), p[HALO:HALO + tile],
                                jnp.where(i < nt - 1, p[HALO + tile:], 0.0)], axis=0)

    for k, o_ref in enumerate((v_ref, x1_ref)):
        cols = slice(k * ch, (k + 1) * ch)
        p = jnp.dot(uc, w_ref[:, cols], preferred_element_type=F32) + b_ref[:, cols]
        _store_permuted(o_ref, p, n2len)

    cols = slice(2 * ch, 3 * ch)
    p = jnp.dot(ue, w_ref[:, cols], preferred_element_type=F32) + b_ref[:, cols]
    p_scr[...] = zero_outside_sequence(p)
    hg = jnp.dot(uc, w_ref[:, 3 * ch:4 * ch], preferred_element_type=F32) + b_ref[:, 3 * ch:4 * ch]
    gate = _silu(hg)
    s = (cb_ref[:, cols]
         + cw_ref[0:1, cols] * p_scr[HALO - 1:HALO - 1 + tile, :]
         + cw_ref[1:2, cols] * p_scr[HALO:HALO + tile, :]
         + cw_ref[2:3, cols] * p_scr[HALO + 1:HALO + 1 + tile, :])
    _store_permuted(x2_ref, s * gate, n2len)

    dp = ps_ref.shape[1]
    c0 = 4 * ch
    pin = jnp.dot(ue, w_ref[:, c0:c0 + dp], preferred_element_type=F32) + b_ref[:, c0:c0 + dp]
    pin = zero_outside_sequence(pin)
    q_scr[...] = pin.astype(BF16)
    p_scr[...] = pin
    pos = i * tile + lax.broadcasted_iota(jnp.int32, (tile, 1), 0)
    pg = dp // len(POOL_WINDOWS)
    groups = []
    for g, win in enumerate(POOL_WINDOWS):
        lanes = slice(g * pg, (g + 1) * pg)
        half = win // 2
        sums = []
        for b in range(tile // BAND_ROWS):
            start, variant = _band_window(b, tile)
            sums.append(jnp.dot(band_ref[g, variant], q_scr[start:start + BAND_SPAN, lanes],
                                preferred_element_type=F32))
        acc = jnp.concatenate(sums, axis=0)
        cnt = (jnp.minimum(pos + half, seq) - jnp.maximum(pos - half, 0)).astype(F32)
        diff = acc / cnt - p_scr[HALO:HALO + tile, lanes]
        groups.append(jnp.dot(diff.astype(BF16), pw_ref[g], preferred_element_type=F32))
    pgate = jnp.dot(uc, w_ref[:, c0 + dp:c0 + 2 * dp], preferred_element_type=F32) + b_ref[:, c0 + dp:c0 + 2 * dp]
    yp = jnp.concatenate(groups, axis=1) * ps_ref[...] * _silu(pgate)
    yp_ref[...] = yp.astype(yp_ref.dtype)


def _input_projection(x, mod3, w_in, b_in, conv_w, conv_b, pool_w, pool_scale, n2len):
    bsz, seq, d = x.shape
    ch = conv_w.shape[1] // 3
    dp = pool_scale.shape[0]
    n1h = seq // n2len
    tile = SUBLANES * n2len
    nt = seq // tile
    hb = tile // HALO
    n_proj = w_in.shape[1]
    cbn = ch // LANES
    assert max(POOL_WINDOWS) // 2 <= HALO and dp == ch and tile % BAND_ROWS == 0
    band = jnp.asarray(_band_matrices(tile))
    perm = jax.ShapeDtypeStruct((bsz, n1h // SUBLANES, cbn, tile, LANES), F32)
    perm_spec = pl.BlockSpec((None, None, cbn, tile, LANES), lambda b, i: (b, i, 0, 0, 0))
    full = lambda shape: pl.BlockSpec(shape, lambda b, i: (0,) * len(shape))
    kern = functools.partial(_proj_kernel, tile=tile, n2len=n2len, ch=ch, seq=seq)
    return pl.pallas_call(
        kern,
        grid=(bsz, nt),
        in_specs=[
            pl.BlockSpec((None, HALO, d), lambda b, i: (b, jnp.maximum(i * hb - 1, 0), 0)),
            pl.BlockSpec((None, tile, d), lambda b, i: (b, i, 0)),
            pl.BlockSpec((None, HALO, d), lambda b, i: (b, jnp.minimum((i + 1) * hb, seq // HALO - 1), 0)),
            pl.BlockSpec((None, 3, d), lambda b, i: (b, 0, 0)),
            _resident((d, n_proj)), full((1, n_proj)), full((3, 3 * ch)), full((1, 3 * ch)),
            full(pool_w.shape), full((1, dp)), full(band.shape),
        ],
        out_specs=[perm_spec, perm_spec, perm_spec,
                   pl.BlockSpec((None, tile, dp), lambda b, i: (b, i, 0))],
        out_shape=[perm, perm, perm, jax.ShapeDtypeStruct((bsz, seq, dp), BF16)],
        scratch_shapes=[pltpu.VMEM((tile + 2 * HALO, ch), F32),
                        pltpu.VMEM((tile + 2 * HALO, dp), BF16),
                        pltpu.VMEM((d, n_proj), BF16)],
        compiler_params=pltpu.CompilerParams(
            dimension_semantics=("arbitrary", "arbitrary"), vmem_limit_bytes=VMEM_LIMIT),
        name="in_proj_conv_pool",
    )(x, x, x, mod3, w_in, b_in.reshape(1, n_proj), conv_w,
      conv_b.reshape(1, 3 * ch), pool_w.astype(BF16), pool_scale.reshape(1, dp), band)


def _filter_mlp_kernel(z_ref, w1_ref, b1_ref, wi_ref, bi_ref, fr_ref, wo_ref, ad_ref,
                       of_ref, or_ref, *, n2len):
    z = z_ref[...]
    fr = fr_ref[...]
    h = jnp.sin(fr * (jnp.dot(z.astype(BF16), w1_ref[...],
                              preferred_element_type=F32) + b1_ref[...]))
    for l in range(wi_ref.shape[0]):
        h = jnp.sin(fr * (jnp.dot(h.astype(BF16), wi_ref[l],
                                  preferred_element_type=F32) + bi_ref[l:l + 1, :]))
    hb = h.astype(BF16)
    half_lanes = LANES // 2
    pairs_per_half = z.shape[0] // 2 // n2len
    for half in range(2):
        t = z[:, half * half_lanes:half * half_lanes + 1]
        decay = jnp.exp(-t * ad_ref[...])
        for d, o_ref in enumerate((of_ref, or_ref)):
            k = jnp.dot(hb, wo_ref[d, half], preferred_element_type=F32)
            words = _pack_pairs(k * decay)
            _store_permuted(o_ref, words, n2len, first=half * pairs_per_half)


def _block_diag2(w):
    zero = jnp.zeros_like(w)
    return jnp.concatenate([jnp.concatenate([w, zero], axis=-1),
                            jnp.concatenate([zero, w], axis=-1)], axis=-2)


def _filter_taps(seq, n2len, w1, b1, w_inner, b_inner, w_out, freq, n_ch, order):
    emb, hid = w1.shape
    n_inner = w_inner.shape[0]
    oc = order * n_ch
    hl = LANES // 2
    assert emb <= hl and hid <= hl
    tile = SUBLANES * n2len
    groups = seq // (2 * tile)
    zp = jnp.asarray(_packed_pos_features(seq, emb, n2len))
    ph = hl - hid
    w1d = _block_diag2(jnp.pad(w1, ((0, hl - emb), (0, ph))))
    b1d = jnp.tile(jnp.pad(b1, (0, ph)), 2).reshape(1, LANES)
    wid = _block_diag2(jnp.pad(w_inner, ((0, 0), (0, ph), (0, ph))))
    bid = jnp.tile(jnp.pad(b_inner, ((0, 0), (0, ph))), (1, 2))
    frd = jnp.tile(jnp.pad(freq, (0, ph), constant_values=1.0), 2).reshape(1, LANES)
    wo = jnp.transpose(w_out.reshape(hid, order, 2, n_ch), (2, 0, 1, 3)).reshape(2, hid, oc)
    wo = jnp.pad(wo, ((0, 0), (0, ph), (0, 0)))
    zero = jnp.zeros_like(wo)
    wo4 = jnp.stack([jnp.concatenate([wo, zero], axis=1),
                     jnp.concatenate([zero, wo], axis=1)], axis=1)
    min_decay = math.log(DECAY_TARGET) / SLOW_DECAY_PCT
    max_decay = math.log(DECAY_TARGET) / FAST_DECAY_PCT
    absdelta = jnp.abs(jnp.linspace(min_decay, max_decay, n_ch, dtype=F32))
    absdelta = jnp.tile(absdelta, order).reshape(1, oc)

    full = lambda shape: pl.BlockSpec(shape, lambda i: (0,) * len(shape))
    kern = functools.partial(_filter_mlp_kernel, n2len=n2len)
    taps = jax.ShapeDtypeStruct((groups, oc // LANES, tile, LANES), jnp.uint32)
    taps_spec = pl.BlockSpec((None, oc // LANES, tile, LANES), lambda i: (i, 0, 0, 0))
    return pl.pallas_call(
        kern,
        grid=(groups,),
        in_specs=[
            pl.BlockSpec((tile, LANES), lambda i: (i, 0)),
            full((LANES, LANES)), full((1, LANES)), full((n_inner, LANES, LANES)),
            full((n_inner, LANES)), full((1, LANES)), full((2, 2, LANES, oc)), full((1, oc)),
        ],
        out_specs=[taps_spec, taps_spec],
        out_shape=[taps, taps],
        compiler_params=pltpu.CompilerParams(
            dimension_semantics=("arbitrary",), vmem_limit_bytes=VMEM_LIMIT),
        name="filter_mlp",
    )(zp, w1d.astype(BF16), b1d, wid.astype(BF16), bid, frd, wo4.astype(BF16), absdelta)


def _resident(shape):
    return pl.BlockSpec(shape, lambda *_: (0,) * len(shape), pipeline_mode=pl.Buffered(1))


def _dot_rows(a, b):
    return lax.dot_general(a, b, (((0,), (0,)), ((), ())), preferred_element_type=F32)


def _pack_pairs(x):
    return pltpu.bitcast(x.astype(BF16), jnp.uint32)


def _unpack_pairs(w):
    return pltpu.bitcast(w, BF16)


def _spectrum_view(a_scr, chunk, ck1, pitch):
    start = pl.multiple_of(chunk * (ck1 * pitch), SUBLANES)
    return a_scr.at[pl.ds(start, ck1 * pitch)]


def _spectrum_pair(view, j, n2len, pitch):
    return jnp.concatenate([_unpack_pairs(view[jj * pitch:jj * pitch + n2len, :])
                            for jj in (j, j + 1)], axis=1)


def _filter_fft_kernel(hf_ref, hr_ref, g_ref, gr_ref, h2_ref, o_ref, a_scr,
                       *, ns, cn2, ck1, n1h, n2len, pitch):
    s = pl.program_id(1)
    ncb = a_scr.shape[0]

    @pl.when(s < ns)
    def _():
        for j in range(cn2):
            jr = cn2 - 1 - j
            m = s * cn2 + j
            g = jnp.concatenate([g_ref[m], gr_ref[n2len - 1 - m]], axis=0)
            x = jnp.concatenate(
                [jnp.concatenate([_unpack_pairs(_load_minor(hf_ref.at[:, cb], j)),
                                  _unpack_pairs(_load_minor(hr_ref.at[:, cb], jr))], axis=0)
                 for cb in range(ncb)], axis=1)
            packed = _pack_pairs(_dot_rows(g, x))
            for cb in range(ncb):
                a_scr[cb, pl.ds(m, n1h, stride=pitch), :] = packed[:, cb * LANES:(cb + 1) * LANES]

    @pl.when(s >= ns)
    def _():
        for cb in range(ncb):
            view = _spectrum_view(a_scr.at[cb], s - ns, ck1, pitch)
            lanes = slice(cb * LANES, (cb + 1) * LANES)
            for j in range(0, ck1, 2):
                rows = _spectrum_pair(view, j, n2len, pitch)
                u = jnp.dot(h2_ref[...], rows, preferred_element_type=F32)
                u = (u * (1.0 / (n1h * n2len))).astype(o_ref.dtype)
                o_ref[j, :, lanes] = u[:, :LANES]
                o_ref[j + 1, :, lanes] = u[:, LANES:]


def _filter_spectrum(taps_f, taps_r, g_fwd, g_rev, h2, n2len):
    groups, ocb, _, _ = taps_f.shape
    oc = ocb * LANES
    n1h = groups * 2 * SUBLANES
    ns, ns2 = FILTER_STEPS
    cn2, ck1 = n2len // ns, n1h // ns2
    pitch = n2len + SUBLANES
    kern = functools.partial(_filter_fft_kernel, ns=ns, cn2=cn2, ck1=ck1, n1h=n1h,
                             n2len=n2len, pitch=pitch)
    ncb = FILTER_COL_BLOCKS
    return pl.pallas_call(
        kern,
        grid=(ocb // ncb, ns + ns2),
        in_specs=[
            pl.BlockSpec((groups, ncb, cn2 * SUBLANES, LANES),
                         lambda c, s: (0, c, jnp.minimum(s, ns - 1), 0)),
            pl.BlockSpec((groups, ncb, cn2 * SUBLANES, LANES),
                         lambda c, s: (0, c, jnp.maximum(ns - 1 - s, 0), 0)),
            _resident(g_fwd.shape), _resident(g_rev.shape), _resident(h2.shape),
        ],
        out_specs=pl.BlockSpec((ck1, 2 * n2len, ncb * LANES),
                               lambda c, s: (jnp.maximum(s - ns, 0), 0, c)),
        out_shape=jax.ShapeDtypeStruct((n1h, 2 * n2len, oc), BF16),
        scratch_shapes=[pltpu.VMEM((ncb, n1h * pitch, LANES), jnp.uint32)],
        compiler_params=pltpu.CompilerParams(
            dimension_semantics=("parallel", "arbitrary"), vmem_limit_bytes=CONV_VMEM_LIMIT),
        name="filter_spectrum",
    )(taps_f, taps_r, g_fwd, g_rev, h2)


def _short_conv_slab(x_ref, lo_ref, hi_ref, taps_ref, j, count, first_chunk, last_chunk):
    cur = _load_minor(x_ref, j)
    zero_row = jnp.zeros((1, cur.shape[1]), F32)
    if j > 0:
        prev = _load_minor(x_ref, j - 1)
    else:
        prev = _load_minor(lo_ref, 0)
        prev = jnp.where(first_chunk, jnp.concatenate([zero_row, prev[:-1]], axis=0), prev)
    if j < count - 1:
        nxt = _load_minor(x_ref, j + 1)
    else:
        nxt = _load_minor(hi_ref, 0)
        nxt = jnp.where(last_chunk, jnp.concatenate([nxt[1:], zero_row], axis=0), nxt)
    return (taps_ref[3:4, :] + taps_ref[0:1, :] * prev + taps_ref[1:2, :] * cur
            + taps_ref[2:3, :] * nxt)


def _long_conv_kernel(*refs, steps, n1h, n2len, pitch, conv_in, conv_mult):
    refs = list(refs)
    vf_ref = refs.pop(0)
    vlo_ref, vhi_ref, vtaps_ref = (refs.pop(0), refs.pop(0), refs.pop(0)) if conv_in else (None,) * 3
    g_ref, kf_ref, h2_ref, h2i_ref, xm_ref = (refs.pop(0) for _ in range(5))
    xlo_ref, xhi_ref, xtaps_ref = (refs.pop(0), refs.pop(0), refs.pop(0)) if conv_mult else (None,) * 3
    hb_ref, o_ref, a_scr, keep_scr = refs
    ns1, ns2, ns3 = steps
    cn2a, ck1, cn2 = n2len // ns1, n1h // ns2, n2len // ns3
    s = pl.program_id(2)

    @pl.when(s < ns1)
    def _():
        base = s * cn2a
        for j in range(cn2a):
            if conv_in:
                x = _short_conv_slab(vf_ref, vlo_ref, vhi_ref, vtaps_ref, j, cn2a,
                                     s == 0, s == ns1 - 1)
            elif vf_ref.dtype == jnp.uint32:
                x = _unpack_pairs(_load_minor(vf_ref, j)).astype(F32)
            else:
                x = _load_minor(vf_ref, j)
            keep_scr[base + j] = x
            r = _dot_rows(g_ref[base + j], x.astype(BF16))
            a_scr[pl.ds(base + j, n1h, stride=pitch), :] = _pack_pairs(r)

    @pl.when((s >= ns1) & (s < ns1 + ns2))
    def _():
        view = _spectrum_view(a_scr, s - ns1, ck1, pitch)
        for j in range(0, ck1, 2):
            rows = _spectrum_pair(view, j, n2len, pitch)
            u = jnp.dot(h2_ref[...], rows, preferred_element_type=F32)
            ur, ui = u[:n2len], u[n2len:]
            kf = jnp.concatenate([kf_ref[j], kf_ref[j + 1]], axis=1).astype(F32)
            kr, ki = kf[:n2len], kf[n2len:]
            y = jnp.concatenate([ur * kr - ui * ki, ur * ki + ui * kr], axis=0)
            bb = _pack_pairs(jnp.dot(h2i_ref[...], y.astype(BF16), preferred_element_type=F32))
            for jj in range(2):
                view[(j + jj) * pitch:(j + jj) * pitch + n2len, :] = bb[
                    :, jj * LANES:(jj + 1) * LANES]

    @pl.when(s >= ns1 + ns2)
    def _():
        base = (s - ns1 - ns2) * cn2
        for j in range(cn2):
            rows = _unpack_pairs(a_scr[pl.ds(base + j, n1h, stride=pitch), :])
            y = jnp.dot(g_ref[base + j], rows, preferred_element_type=F32)
            if conv_mult:
                xm = _short_conv_slab(xm_ref, xlo_ref, xhi_ref, xtaps_ref, j, cn2,
                                      s == ns1 + ns2, s == ns1 + ns2 + ns3 - 1)
            else:
                xm = _load_minor(xm_ref, j)
            z = xm * (y + hb_ref[...] * keep_scr[base + j])
            if o_ref.dtype == jnp.uint32:
                z = _pack_pairs(z)
            o_ref[:, j * SUBLANES:(j + 1) * SUBLANES, :] = z.reshape(
                z.shape[0] // SUBLANES, SUBLANES, z.shape[1])


def _long_conv(u_perm, mult_perm, kf, hbias, order_idx, tables, in_taps=None, mult_taps=None,
               pack_out=False):
    g_fwd, _, h2, h2i = tables
    bsz, _, cbn, _, _ = u_perm.shape
    n2len, n1h = g_fwd.shape[0], g_fwd.shape[1]
    steps = CONV_STEPS
    ns1, ns2, ns3 = steps
    cn2a, cn2 = n2len // ns1, n2len // ns3
    pitch = n2len + SUBLANES

    def early(s):
        return jnp.minimum(s, ns1 - 1)

    def late(s):
        return jnp.clip(s - ns1 - ns2, 0, ns3 - 1)

    def chunk_specs(chunk_of, width, groups):
        slab = (None, groups, None, SUBLANES, LANES)
        return [
            pl.BlockSpec((None, groups, None, width * SUBLANES, LANES),
                         lambda c, b, s: (b, 0, c, chunk_of(s), 0)),
            pl.BlockSpec(slab, lambda c, b, s: (b, 0, c, (chunk_of(s) * width - 1) % n2len, 0)),
            pl.BlockSpec(slab, lambda c, b, s: (b, 0, c, ((chunk_of(s) + 1) * width) % n2len, 0)),
        ]

    taps_spec = pl.BlockSpec((4, LANES), lambda c, b, s: (0, c))
    in_specs, args = [], []
    specs = chunk_specs(early, cn2a, u_perm.shape[1])
    if in_taps is None:
        in_specs += specs[:1]
        args += [u_perm]
    else:
        in_specs += specs + [taps_spec]
        args += [u_perm, u_perm, u_perm, in_taps]
    in_specs += [
        _resident(g_fwd.shape),
        pl.BlockSpec((n1h // ns2, 2 * n2len, LANES),
                     lambda c, b, s: (jnp.clip(s - ns1, 0, ns2 - 1), 0, order_idx * cbn + c)),
        _resident(h2.shape), _resident(h2i.shape),
    ]
    args += [g_fwd, kf, h2, h2i]
    specs = chunk_specs(late, cn2, mult_perm.shape[1])
    if mult_taps is None:
        in_specs += specs[:1]
        args += [mult_perm]
    else:
        in_specs += specs + [taps_spec]
        args += [mult_perm, mult_perm, mult_perm, mult_taps]
    in_specs.append(pl.BlockSpec((None, 1, LANES), lambda c, b, s: (order_idx, 0, c)))
    args.append(hbias)
    out_groups = n1h // SUBLANES // (2 if pack_out else 1)
    out_spec = chunk_specs(late, cn2, out_groups)[0]
    out_shape = jax.ShapeDtypeStruct((bsz, out_groups, cbn, n2len * SUBLANES, LANES),
                                     jnp.uint32 if pack_out else F32)
    kern = functools.partial(_long_conv_kernel, steps=steps, n1h=n1h, n2len=n2len,
                             pitch=pitch, conv_in=in_taps is not None,
                             conv_mult=mult_taps is not None)
    return pl.pallas_call(
        kern,
        grid=(cbn, bsz, sum(steps)),
        in_specs=in_specs,
        out_specs=out_spec,
        out_shape=out_shape,
        scratch_shapes=[pltpu.VMEM((n1h * pitch, LANES), jnp.uint32),
                        pltpu.VMEM((n2len, n1h, LANES), F32)],
        compiler_params=pltpu.CompilerParams(
            dimension_semantics=("parallel", "parallel", "arbitrary"),
            vmem_limit_bytes=CONV_VMEM_LIMIT),
        name="long_conv",
    )(*args)


def _out_kernel(x_hbm, yh_hbm, yp_ref, mod_ref, w_ref, b_ref, g_ref, beta_ref, o_ref,
                x_buf, yh_buf, sems, *, n2len, alpha):
    nt = pl.num_programs(1)
    total = pl.num_programs(0) * nt
    t = pl.program_id(0) * nt + pl.program_id(1)
    tile = x_buf.shape[1]

    def copies(step):
        slot = step % STREAM_DEPTH
        return (
            pltpu.make_async_copy(
                x_hbm.at[step // nt, pl.ds((step % nt) * tile, tile), :], x_buf.at[slot],
                sems.at[0, slot]),
            pltpu.make_async_copy(
                yh_hbm.at[step // nt, step % nt], yh_buf.at[slot], sems.at[1, slot]))

    @pl.when(t == 0)
    def _():
        for k in range(STREAM_DEPTH - 1):
            for cp in copies(k):
                cp.start()

    @pl.when(t + STREAM_DEPTH - 1 < total)
    def _():
        for cp in copies(t + STREAM_DEPTH - 1):
            cp.start()

    for cp in copies(t):
        cp.wait()
    x_ref = x_buf.at[t % STREAM_DEPTH]
    yh_ref = yh_buf.at[t % STREAM_DEPTH]
    cbn = yh_ref.shape[0]
    yh = jnp.concatenate(
        [jnp.concatenate([yh_ref[cb, pl.ds(j, n2len, stride=SUBLANES), :]
                          for cb in range(cbn)], axis=1)
         for j in range(SUBLANES)], axis=0)
    y = jnp.concatenate([yh.astype(BF16), yp_ref[...]], axis=1)
    acc = jnp.dot(y, w_ref[...], preferred_element_type=F32)
    h = alpha * x_ref[...] + mod_ref[2:3, :] * (acc + b_ref[...])
    mu = jnp.mean(h, axis=-1, keepdims=True)
    hc = h - mu
    var = jnp.mean(hc * hc, axis=-1, keepdims=True)
    o_ref[...] = hc * lax.rsqrt(var + LN_EPS) * g_ref[...] + beta_ref[...]


def _output_projection(x, yh_perm, yp, mod3, w_out, b_out, ln_g, ln_b, alpha):
    bsz, seq, d = x.shape
    _, _, cbn, tile, _ = yh_perm.shape
    ch = cbn * LANES
    n2len = tile // SUBLANES
    dp = yp.shape[2]
    full = lambda shape: pl.BlockSpec(shape, lambda b, i: (0,) * len(shape))
    kern = functools.partial(_out_kernel, n2len=n2len, alpha=alpha)
    return pl.pallas_call(
        kern,
        grid=(bsz, seq // tile),
        in_specs=[
            pl.BlockSpec(memory_space=pl.ANY),
            pl.BlockSpec(memory_space=pl.ANY),
            pl.BlockSpec((None, tile, dp), lambda b, i: (b, i, 0)),
            pl.BlockSpec((None, 3, d), lambda b, i: (b, 0, 0)),
            full((ch + dp, d)), full((1, d)), full((1, d)), full((1, d)),
        ],
        out_specs=pl.BlockSpec((None, tile, d), lambda b, i: (b, i, 0)),
        out_shape=jax.ShapeDtypeStruct(x.shape, x.dtype),
        scratch_shapes=[pltpu.VMEM((STREAM_DEPTH, tile, d), x.dtype),
                        pltpu.VMEM((STREAM_DEPTH, cbn, tile, LANES), yh_perm.dtype),
                        pltpu.SemaphoreType.DMA((2, STREAM_DEPTH))],
        compiler_params=pltpu.CompilerParams(
            dimension_semantics=("arbitrary", "arbitrary"), vmem_limit_bytes=VMEM_LIMIT),
        name="out_proj_deepnorm",
    )(x, yh_perm, yp, mod3, w_out.astype(BF16), b_out.reshape(1, d), ln_g.reshape(1, d),
      ln_b.reshape(1, d))


def _forward(x, c, w_ada, b_ada, w_in, b_in, conv_w, conv_b, filt_w1, filt_b1, filt_w_inner,
             filt_b_inner, filt_w_out, filt_freq, hyena_bias, pool_w, pool_scale, w_out,
             b_out, ln_g, ln_b, *, n2len):
    bsz, seq, d = x.shape
    depth = w_ada.shape[0]
    order, n_ch = hyena_bias.shape[1], hyena_bias.shape[2]
    alpha = (2.0 * depth) ** 0.25
    tables = _dft_tables(seq, n2len)
    h = x
    for layer in range(depth):
        mod3 = _modulation(c, w_ada[layer], b_ada[layer]).reshape(bsz, 3, d)
        v, x1, x2g, yp = _input_projection(
            h, mod3, w_in[layer], b_in[layer], conv_w[layer], conv_b[layer], pool_w[layer],
            pool_scale[layer], n2len)
        taps_f, taps_r = _filter_taps(
            seq, n2len, filt_w1[layer], filt_b1[layer], filt_w_inner[layer],
            filt_b_inner[layer], filt_w_out[layer], filt_freq[layer], n_ch, order)
        kf = _filter_spectrum(taps_f, taps_r, tables[0], tables[1], tables[2], n2len)
        hbias = hyena_bias[layer].reshape(order, 1, n_ch)
        taps = [jnp.concatenate([conv_w[layer][:, k * n_ch:(k + 1) * n_ch],
                                 conv_b[layer][None, k * n_ch:(k + 1) * n_ch]], axis=0)
                for k in range(2)]
        z = _long_conv(v, x1, kf, hbias, 0, tables, in_taps=taps[0], mult_taps=taps[1],
                       pack_out=order > 1)
        for o in range(1, order):
            z = _long_conv(z, x2g, kf, hbias, o, tables)
        h = _output_projection(h, z, yp, mod3, w_out[layer], b_out[layer], ln_g[layer],
                               ln_b[layer], alpha)
    return h


def kernel(x, c, w_ada, b_ada, w_in, b_in, conv_w, conv_b, filt_w1, filt_b1, filt_w_inner,
           filt_b_inner, filt_w_out, filt_freq, hyena_bias, pool_w, pool_scale, w_out, b_out,
           ln_g, ln_b):
    return _forward(x, c, w_ada, b_ada, w_in, b_in, conv_w, conv_b, filt_w1, filt_b1,
                    filt_w_inner, filt_b_inner, filt_w_out, filt_freq, hyena_bias, pool_w,
                    pool_scale, w_out, b_out, ln_g, ln_b, n2len=MINOR_LEN)
```

```python
import functools
import math

import jax
import jax.numpy as jnp
import numpy as np
from jax import lax
from jax.experimental import pallas as pl
from jax.experimental.pallas import tpu as pltpu

F32 = jnp.float32
BF16 = jnp.bfloat16
HIGHEST = lax.Precision.HIGHEST

POOL_WINDOWS = (2, 4, 8, 16)
LN_EPS = 1e-5
DECAY_TARGET = 1e-2
FAST_DECAY_PCT = 0.3
SLOW_DECAY_PCT = 1.5

LANES = 128
SUBLANES = 8
HALO = 16
BAND_ROWS = 128
BAND_SPAN = 256
MINOR_LEN = 128
FILTER_STEPS = (4, 4)
FILTER_COL_BLOCKS = 2
CONV_STEPS = (2, 2, 2)
STREAM_DEPTH = 3
OUT_SUBTILES = 2
V7X_VMEM_BYTES = 64 * 1024 * 1024
VMEM_LIMIT = V7X_VMEM_BYTES * 7 // 8
CONV_VMEM_LIMIT = V7X_VMEM_BYTES * 15 // 16


def _silu(x):
    h = 0.5 * x
    return h + h * jnp.tanh(h)


@functools.lru_cache(maxsize=None)
def _host_dft_tables(seq, n2len):
    n1h = seq // n2len
    n_fft = 2 * seq
    k1 = np.arange(n1h, dtype=np.int64)
    odd = 2 * k1 + 1
    alpha = ((odd[:, None] * k1[None, :]) % (4 * n1h)) * (2.0 * math.pi / (4 * n1h))
    n2e = np.arange(n2len + 1, dtype=np.int64)
    beta = ((n2e[:, None] * odd[None, :]) % (2 * n_fft)) * (2.0 * math.pi / (2 * n_fft))
    n2i = np.arange(n2len, dtype=np.int64)
    phi = ((n2i[:, None] * n2i[None, :]) % n2len) * (2.0 * math.pi / n2len)
    cm, sm = np.cos(phi), np.sin(phi)
    h2 = np.block([[cm, sm], [-sm, cm]])
    h2 = h2.reshape(2 * n2len, 2, n2len).transpose(0, 2, 1).reshape(2 * n2len, 2 * n2len)
    h2i = np.block([[cm, -sm], [sm, cm]])
    h2i = h2i.reshape(2, n2len, 2 * n2len).transpose(1, 0, 2).reshape(2 * n2len, 2 * n2len)
    h2, h2i = h2.astype(BF16), h2i.astype(BF16)
    quarter = 0.5 * math.pi * np.tile(np.array([0.0, 1.0]), n1h)[:, None]
    alpha2 = np.repeat(alpha, 2, axis=0)
    beta2 = np.repeat(beta, 2, axis=1)
    f32 = lambda a: a.astype(np.float32)
    seeds = tuple(f32(t) for t in (
        np.cos(alpha2 + quarter), np.sin(alpha2 + quarter),
        np.cos(alpha2 - quarter), np.sin(alpha2 - quarter), np.cos(beta2), np.sin(beta2)))
    return seeds + (h2, h2i)


def _dft_tables(seq, n2len):
    caf, saf, car, sar, cb, sb, h2, h2i = _host_dft_tables(seq, n2len)
    cb, sb = jnp.asarray(cb)[:, None, :], jnp.asarray(sb)[:, None, :]
    g_fwd = (caf.T[None] * cb[:-1] - saf.T[None] * sb[:-1]).astype(BF16)
    g_rev = (car.T[None] * cb[1:] - sar.T[None] * sb[1:]).astype(BF16)
    return g_fwd, g_rev, jnp.asarray(h2), jnp.asarray(h2i)


@functools.lru_cache(maxsize=None)
def _packed_pos_features(seq, emb, n2len):
    hl = LANES // 2
    step_rows = SUBLANES * n2len
    bands = (emb - 1) // 2
    row = np.arange(seq // 2, dtype=np.int64)[:, None]
    lane = np.arange(LANES, dtype=np.int64)[None, :]
    local = row % step_rows
    n1_local = 8 * (lane // hl) + 2 * (local // 2 // n2len) + local % 2
    pos = ((row // step_rows) * 2 * step_rows + n1_local * n2len
           + (local // 2) % n2len).astype(np.float64)
    feat = np.broadcast_to(lane % hl, pos.shape)
    t = pos / (seq - 1)
    w = (2.0 * math.pi / seq) * pos
    f = np.linspace(1e-4, bands - 1, bands)[(feat - 1) % bands]
    z = np.where(feat == 0, t,
                 np.where(feat <= bands, np.cos(f * w),
                          np.where(feat <= 2 * bands, -np.sin(f * w), 0.0)))
    return z.astype(np.float32)


def _mod_kernel(c_ref, w_ref, b_ref, o_ref):
    s = _silu(c_ref[...])
    o_ref[...] = jnp.dot(s, w_ref[...], precision=HIGHEST,
                         preferred_element_type=F32) + b_ref[...]


def _modulation(c, w_ada, b_ada):
    bsz, d = c.shape
    n_out = w_ada.shape[1]
    return pl.pallas_call(
        _mod_kernel,
        grid=(n_out // d,),
        in_specs=[pl.BlockSpec((bsz, d), lambda j: (0, 0)),
                  pl.BlockSpec((d, d), lambda j: (0, j)),
                  pl.BlockSpec((1, d), lambda j: (0, j))],
        out_specs=pl.BlockSpec((bsz, d), lambda j: (0, j)),
        out_shape=jax.ShapeDtypeStruct((bsz, n_out), F32),
        name="adaln_mod",
    )(c, w_ada, b_ada.reshape(1, n_out))


def _store_permuted(o_ref, val, n2len, first=0):
    for cb in range(val.shape[1] // LANES):
        for j in range(val.shape[0] // n2len):
            o_ref[cb, pl.ds(first + j, n2len, stride=SUBLANES), :] = val[
                j * n2len:(j + 1) * n2len, cb * LANES:(cb + 1) * LANES]


def _load_minor(x_ref, j):
    blk = x_ref[:, j * SUBLANES:(j + 1) * SUBLANES, :]
    return blk.reshape(blk.shape[0] * SUBLANES, blk.shape[2])


def _band_window(b, tile):
    last = tile + 2 * HALO - BAND_SPAN
    start = min(b * BAND_ROWS, last)
    return start, 0 if start == b * BAND_ROWS else 1


@functools.lru_cache(maxsize=None)
def _band_matrices(tile):
    t = np.arange(BAND_ROWS)[:, None]
    k = np.arange(BAND_SPAN)[None, :]
    mats = []
    for win in POOL_WINDOWS:
        half = win // 2
        per_variant = []
        for b in (0, tile // BAND_ROWS - 1):
            start, _ = _band_window(b, tile)
            rel = k + start - HALO - (b * BAND_ROWS + t)
            per_variant.append(((rel >= -half) & (rel < half)).astype(BF16))
        mats.append(np.stack(per_variant))
    return np.stack(mats)


def _proj_kernel(xp_ref, xc_ref, xn_ref, mod_ref, wf_ref, b_ref, cw_ref, cb_ref, pw_ref,
                 ps_ref, band_ref, v_ref, x1_ref, x2_ref, yp_ref, p_scr, q_scr, w_ref,
                 *, tile, n2len, ch, seq):
    i = pl.program_id(1)
    nt = pl.num_programs(1)

    @pl.when((pl.program_id(0) == 0) & (i == 0))
    def _():
        w_ref[...] = wf_ref[...].astype(BF16)

    shift = mod_ref[0:1, :]
    scale1 = 1.0 + mod_ref[1:2, :]
    xe = jnp.concatenate([xp_ref[...], xc_ref[...], xn_ref[...]], axis=0)
    ue = (xe * scale1 + shift).astype(BF16)
    uc = ue[HALO:HALO + tile, :]

    def zero_outside_sequence(p):
        return jnp.concatenate([jnp.where(i > 0, p[:HALO], 0.0), p[HALO:HALO + tile],
                                jnp.where(i < nt - 1, p[HALO + tile:], 0.0)], axis=0)

    for k, o_ref in enumerate((v_ref, x1_ref)):
        cols = slice(k * ch, (k + 1) * ch)
        p = jnp.dot(uc, w_ref[:, cols], preferred_element_type=F32) + b_ref[:, cols]
        _store_permuted(o_ref, p, n2len)

    cols = slice(2 * ch, 3 * ch)
    p = jnp.dot(ue, w_ref[:, cols], preferred_element_type=F32) + b_ref[:, cols]
    p_scr[...] = zero_outside_sequence(p)
    hg = jnp.dot(uc, w_ref[:, 3 * ch:4 * ch], preferred_element_type=F32) + b_ref[:, 3 * ch:4 * ch]
    gate = _silu(hg)
    s = (cb_ref[:, cols]
         + cw_ref[0:1, cols] * p_scr[HALO - 1:HALO - 1 + tile, :]
         + cw_ref[1:2, cols] * p_scr[HALO:HALO + tile, :]
         + cw_ref[2:3, cols] * p_scr[HALO + 1:HALO + 1 + tile, :])
    _store_permuted(x2_ref, s * gate, n2len)

    dp = ps_ref.shape[1]
    c0 = 4 * ch
    pin = jnp.dot(ue, w_ref[:, c0:c0 + dp], preferred_element_type=F32) + b_ref[:, c0:c0 + dp]
    pin = zero_outside_sequence(pin)
    q_scr[...] = pin.astype(BF16)
    p_scr[...] = pin
    pos = i * tile + lax.broadcasted_iota(jnp.int32, (tile, 1), 0)
    pg = dp // len(POOL_WINDOWS)
    groups = []
    for g, win in enumerate(POOL_WINDOWS):
        lanes = slice(g * pg, (g + 1) * pg)
        half = win // 2
        sums = []
        for b in range(tile // BAND_ROWS):
            start, variant = _band_window(b, tile)
            sums.append(jnp.dot(band_ref[g, variant], q_scr[start:start + BAND_SPAN, lanes],
                                preferred_element_type=F32))
        acc = jnp.concatenate(sums, axis=0)
        cnt = (jnp.minimum(pos + half, seq) - jnp.maximum(pos - half, 0)).astype(F32)
        diff = acc / cnt - p_scr[HALO:HALO + tile, lanes]
        groups.append(jnp.dot(diff.astype(BF16), pw_ref[g], preferred_element_type=F32))
    pgate = jnp.dot(uc, w_ref[:, c0 + dp:c0 + 2 * dp], preferred_element_type=F32) + b_ref[:, c0 + dp:c0 + 2 * dp]
    yp = jnp.concatenate(groups, axis=1) * ps_ref[...] * _silu(pgate)
    yp_ref[...] = yp.astype(yp_ref.dtype)


def _input_projection(x, mod3, w_in, b_in, conv_w, conv_b, pool_w, pool_scale, n2len):
    bsz, seq, d = x.shape
    ch = conv_w.shape[1] // 3
    dp = pool_scale.shape[0]
    n1h = seq // n2len
    tile = SUBLANES * n2len
    nt = seq // tile
    hb = tile // HALO
    n_proj = w_in.shape[1]
    cbn = ch // LANES
    assert max(POOL_WINDOWS) // 2 <= HALO and dp == ch and tile % BAND_ROWS == 0
    band = jnp.asarray(_band_matrices(tile))
    perm = jax.ShapeDtypeStruct((bsz, n1h // SUBLANES, cbn, tile, LANES), F32)
    perm_spec = pl.BlockSpec((None, None, cbn, tile, LANES), lambda b, i: (b, i, 0, 0, 0))
    full = lambda shape: pl.BlockSpec(shape, lambda b, i: (0,) * len(shape))
    kern = functools.partial(_proj_kernel, tile=tile, n2len=n2len, ch=ch, seq=seq)
    return pl.pallas_call(
        kern,
        grid=(bsz, nt),
        in_specs=[
            pl.BlockSpec((None, HALO, d), lambda b, i: (b, jnp.maximum(i * hb - 1, 0), 0)),
            pl.BlockSpec((None, tile, d), lambda b, i: (b, i, 0)),
            pl.BlockSpec((None, HALO, d), lambda b, i: (b, jnp.minimum((i + 1) * hb, seq // HALO - 1), 0)),
            pl.BlockSpec((None, 3, d), lambda b, i: (b, 0, 0)),
            _resident((d, n_proj)), full((1, n_proj)), full((3, 3 * ch)), full((1, 3 * ch)),
            full(pool_w.shape), full((1, dp)), full(band.shape),
        ],
        out_specs=[perm_spec, perm_spec, perm_spec,
                   pl.BlockSpec((None, tile, dp), lambda b, i: (b, i, 0))],
        out_shape=[perm, perm, perm, jax.ShapeDtypeStruct((bsz, seq, dp), BF16)],
        scratch_shapes=[pltpu.VMEM((tile + 2 * HALO, ch), F32),
                        pltpu.VMEM((tile + 2 * HALO, dp), BF16),
                        pltpu.VMEM((d, n_proj), BF16)],
        compiler_params=pltpu.CompilerParams(
            dimension_semantics=("arbitrary", "arbitrary"), vmem_limit_bytes=VMEM_LIMIT),
        name="in_proj_conv_pool",
    )(x, x, x, mod3, w_in, b_in.reshape(1, n_proj), conv_w,
      conv_b.reshape(1, 3 * ch), pool_w.astype(BF16), pool_scale.reshape(1, dp), band)


def _filter_mlp_kernel(z_ref, w1_ref, b1_ref, wi_ref, bi_ref, fr_ref, wo_ref, ad_ref,
                       of_ref, or_ref, *, n2len):
    z = z_ref[...]
    fr = fr_ref[...]
    h = jnp.sin(fr * (jnp.dot(z.astype(BF16), w1_ref[...],
                              preferred_element_type=F32) + b1_ref[...]))
    for l in range(wi_ref.shape[0]):
        h = jnp.sin(fr * (jnp.dot(h.astype(BF16), wi_ref[l],
                                  preferred_element_type=F32) + bi_ref[l:l + 1, :]))
    hb = h.astype(BF16)
    half_lanes = LANES // 2
    pairs_per_half = z.shape[0] // 2 // n2len
    for half in range(2):
        t = z[:, half * half_lanes:half * half_lanes + 1]
        decay = jnp.exp(-t * ad_ref[...])
        for d, o_ref in enumerate((of_ref, or_ref)):
            k = jnp.dot(hb, wo_ref[d, half], preferred_element_type=F32)
            words = _pack_pairs(k * decay)
            _store_permuted(o_ref, words, n2len, first=half * pairs_per_half)


def _block_diag2(w):
    zero = jnp.zeros_like(w)
    return jnp.concatenate([jnp.concatenate([w, zero], axis=-1),
                            jnp.concatenate([zero, w], axis=-1)], axis=-2)


def _filter_taps(seq, n2len, w1, b1, w_inner, b_inner, w_out, freq, n_ch, order):
    emb, hid = w1.shape
    n_inner = w_inner.shape[0]
    oc = order * n_ch
    hl = LANES // 2
    assert emb <= hl and hid <= hl
    tile = SUBLANES * n2len
    groups = seq // (2 * tile)
    zp = jnp.asarray(_packed_pos_features(seq, emb, n2len))
    ph = hl - hid
    w1d = _block_diag2(jnp.pad(w1, ((0, hl - emb), (0, ph))))
    b1d = jnp.tile(jnp.pad(b1, (0, ph)), 2).reshape(1, LANES)
    wid = _block_diag2(jnp.pad(w_inner, ((0, 0), (0, ph), (0, ph))))
    bid = jnp.tile(jnp.pad(b_inner, ((0, 0), (0, ph))), (1, 2))
    frd = jnp.tile(jnp.pad(freq, (0, ph), constant_values=1.0), 2).reshape(1, LANES)
    wo = jnp.transpose(w_out.reshape(hid, order, 2, n_ch), (2, 0, 1, 3)).reshape(2, hid, oc)
    wo = jnp.pad(wo, ((0, 0), (0, ph), (0, 0)))
    zero = jnp.zeros_like(wo)
    wo4 = jnp.stack([jnp.concatenate([wo, zero], axis=1),
                     jnp.concatenate([zero, wo], axis=1)], axis=1)
    min_decay = math.log(DECAY_TARGET) / SLOW_DECAY_PCT
    max_decay = math.log(DECAY_TARGET) / FAST_DECAY_PCT
    absdelta = jnp.abs(jnp.linspace(min_decay, max_decay, n_ch, dtype=F32))
    absdelta = jnp.tile(absdelta, order).reshape(1, oc)

    full = lambda shape: pl.BlockSpec(shape, lambda i: (0,) * len(shape))
    kern = functools.partial(_filter_mlp_kernel, n2len=n2len)
    taps = jax.ShapeDtypeStruct((groups, oc // LANES, tile, LANES), jnp.uint32)
    taps_spec = pl.BlockSpec((None, oc // LANES, tile, LANES), lambda i: (i, 0, 0, 0))
    return pl.pallas_call(
        kern,
        grid=(groups,),
        in_specs=[
            pl.BlockSpec((tile, LANES), lambda i: (i, 0)),
            full((LANES, LANES)), full((1, LANES)), full((n_inner, LANES, LANES)),
            full((n_inner, LANES)), full((1, LANES)), full((2, 2, LANES, oc)), full((1, oc)),
        ],
        out_specs=[taps_spec, taps_spec],
        out_shape=[taps, taps],
        compiler_params=pltpu.CompilerParams(
            dimension_semantics=("arbitrary",), vmem_limit_bytes=VMEM_LIMIT),
        name="filter_mlp",
    )(zp, w1d.astype(BF16), b1d, wid.astype(BF16), bid, frd, wo4.astype(BF16), absdelta)


def _resident(shape):
    return pl.BlockSpec(shape, lambda *_: (0,) * len(shape), pipeline_mode=pl.Buffered(1))


def _dot_rows(a, b):
    return lax.dot_general(a, b, (((0,), (0,)), ((), ())), preferred_element_type=F32)


def _pack_pairs(x):
    return pltpu.bitcast(x.astype(BF16), jnp.uint32)


def _unpack_pairs(w):
    return pltpu.bitcast(w, BF16)


def _spectrum_view(a_scr, chunk, ck1, pitch):
    start = pl.multiple_of(chunk * (ck1 * pitch), SUBLANES)
    return a_scr.at[pl.ds(start, ck1 * pitch)]


def _spectrum_pair(view, j, n2len, pitch):
    return jnp.concatenate([_unpack_pairs(view[jj * pitch:jj * pitch + n2len, :])
                            for jj in (j, j + 1)], axis=1)


def _filter_fft_kernel(hf_ref, hr_ref, g_ref, gr_ref, h2_ref, o_ref, a_scr,
                       *, ns, cn2, ck1, n1h, n2len, pitch):
    s = pl.program_id(1)
    ncb = a_scr.shape[0]

    @pl.when(s < ns)
    def _():
        for j in range(cn2):
            jr = cn2 - 1 - j
            m = s * cn2 + j
            g = jnp.concatenate([g_ref[m], gr_ref[n2len - 1 - m]], axis=0)
            x = jnp.concatenate(
                [jnp.concatenate([_unpack_pairs(_load_minor(hf_ref.at[:, cb], j)),
                                  _unpack_pairs(_load_minor(hr_ref.at[:, cb], jr))], axis=0)
                 for cb in range(ncb)], axis=1)
            packed = _pack_pairs(_dot_rows(g, x))
            for cb in range(ncb):
                a_scr[cb, pl.ds(m, n1h, stride=pitch), :] = packed[:, cb * LANES:(cb + 1) * LANES]

    @pl.when(s >= ns)
    def _():
        for cb in range(ncb):
            view = _spectrum_view(a_scr.at[cb], s - ns, ck1, pitch)
            lanes = slice(cb * LANES, (cb + 1) * LANES)
            for j in range(0, ck1, 2):
                rows = _spectrum_pair(view, j, n2len, pitch)
                u = jnp.dot(h2_ref[...], rows, preferred_element_type=F32)
                u = (u * (1.0 / (n1h * n2len))).astype(o_ref.dtype)
                o_ref[j, :, lanes] = u[:, :LANES]
                o_ref[j + 1, :, lanes] = u[:, LANES:]


def _filter_spectrum(taps_f, taps_r, g_fwd, g_rev, h2, n2len):
    groups, ocb, _, _ = taps_f.shape
    oc = ocb * LANES
    n1h = groups * 2 * SUBLANES
    ns, ns2 = FILTER_STEPS
    cn2, ck1 = n2len // ns, n1h // ns2
    pitch = n2len + SUBLANES
    kern = functools.partial(_filter_fft_kernel, ns=ns, cn2=cn2, ck1=ck1, n1h=n1h,
                             n2len=n2len, pitch=pitch)
    ncb = FILTER_COL_BLOCKS
    return pl.pallas_call(
        kern,
        grid=(ocb // ncb, ns + ns2),
        in_specs=[
            pl.BlockSpec((groups, ncb, cn2 * SUBLANES, LANES),
                         lambda c, s: (0, c, jnp.minimum(s, ns - 1), 0)),
            pl.BlockSpec((groups, ncb, cn2 * SUBLANES, LANES),
                         lambda c, s: (0, c, jnp.maximum(ns - 1 - s, 0), 0)),
            _resident(g_fwd.shape), _resident(g_rev.shape), _resident(h2.shape),
        ],
        out_specs=pl.BlockSpec((ck1, 2 * n2len, ncb * LANES),
                               lambda c, s: (jnp.maximum(s - ns, 0), 0, c)),
        out_shape=jax.ShapeDtypeStruct((n1h, 2 * n2len, oc), BF16),
        scratch_shapes=[pltpu.VMEM((ncb, n1h * pitch, LANES), jnp.uint32)],
        compiler_params=pltpu.CompilerParams(
            dimension_semantics=("parallel", "arbitrary"), vmem_limit_bytes=CONV_VMEM_LIMIT),
        name="filter_spectrum",
    )(taps_f, taps_r, g_fwd, g_rev, h2)


def _short_conv_slab(x_ref, lo_ref, hi_ref, taps_ref, j, count, first_chunk, last_chunk):
    cur = _load_minor(x_ref, j)
    zero_row = jnp.zeros((1, cur.shape[1]), F32)
    if j > 0:
        prev = _load_minor(x_ref, j - 1)
    else:
        prev = _load_minor(lo_ref, 0)
        prev = jnp.where(first_chunk, jnp.concatenate([zero_row, prev[:-1]], axis=0), prev)
    if j < count - 1:
        nxt = _load_minor(x_ref, j + 1)
    else:
        nxt = _load_minor(hi_ref, 0)
        nxt = jnp.where(last_chunk, jnp.concatenate([nxt[1:], zero_row], axis=0), nxt)
    return (taps_ref[3:4, :] + taps_ref[0:1, :] * prev + taps_ref[1:2, :] * cur
            + taps_ref[2:3, :] * nxt)


def _long_conv_kernel(*refs, steps, n1h, n2len, pitch, conv_in, conv_mult):
    refs = list(refs)
    vf_ref = refs.pop(0)
    vlo_ref, vhi_ref, vtaps_ref = (refs.pop(0), refs.pop(0), refs.pop(0)) if conv_in else (None,) * 3
    g_ref, kf_ref, h2_ref, h2i_ref, xm_ref = (refs.pop(0) for _ in range(5))
    xlo_ref, xhi_ref, xtaps_ref = (refs.pop(0), refs.pop(0), refs.pop(0)) if conv_mult else (None,) * 3
    hb_ref, o_ref, a_scr, keep_scr = refs
    ns1, ns2, ns3 = steps
    cn2a, ck1, cn2 = n2len // ns1, n1h // ns2, n2len // ns3
    s = pl.program_id(2)

    @pl.when(s < ns1)
    def _():
        base = s * cn2a
        for j in range(cn2a):
            if conv_in:
                x = _short_conv_slab(vf_ref, vlo_ref, vhi_ref, vtaps_ref, j, cn2a,
                                     s == 0, s == ns1 - 1)
            elif vf_ref.dtype == jnp.uint32:
                x = _unpack_pairs(_load_minor(vf_ref, j)).astype(F32)
            else:
                x = _load_minor(vf_ref, j)
            keep_scr[base + j] = x
            r = _dot_rows(g_ref[base + j], x.astype(BF16))
            a_scr[pl.ds(base + j, n1h, stride=pitch), :] = _pack_pairs(r)

    @pl.when((s >= ns1) & (s < ns1 + ns2))
    def _():
        view = _spectrum_view(a_scr, s - ns1, ck1, pitch)
        for j in range(0, ck1, 2):
            rows = _spectrum_pair(view, j, n2len, pitch)
            u = jnp.dot(h2_ref[...], rows, preferred_element_type=F32)
            ur, ui = u[:n2len], u[n2len:]
            kf = jnp.concatenate([kf_ref[j], kf_ref[j + 1]], axis=1).astype(F32)
            kr, ki = kf[:n2len], kf[n2len:]
            y = jnp.concatenate([ur * kr - ui * ki, ur * ki + ui * kr], axis=0)
            bb = _pack_pairs(jnp.dot(h2i_ref[...], y.astype(BF16), preferred_element_type=F32))
            for jj in range(2):
                view[(j + jj) * pitch:(j + jj) * pitch + n2len, :] = bb[
                    :, jj * LANES:(jj + 1) * LANES]

    @pl.when(s >= ns1 + ns2)
    def _():
        base = (s - ns1 - ns2) * cn2
        for j in range(cn2):
            rows = _unpack_pairs(a_scr[pl.ds(base + j, n1h, stride=pitch), :])
            y = jnp.dot(g_ref[base + j], rows, preferred_element_type=F32)
            if conv_mult:
                xm = _short_conv_slab(xm_ref, xlo_ref, xhi_ref, xtaps_ref, j, cn2,
                                      s == ns1 + ns2, s == ns1 + ns2 + ns3 - 1)
            else:
                xm = _load_minor(xm_ref, j)
            z = xm * (y + hb_ref[...] * keep_scr[base + j])
            if o_ref.dtype == jnp.uint32:
                z = _pack_pairs(z)
            o_ref[:, j * SUBLANES:(j + 1) * SUBLANES, :] = z.reshape(
                z.shape[0] // SUBLANES, SUBLANES, z.shape[1])


def _long_conv(u_perm, mult_perm, kf, hbias, order_idx, tables, in_taps=None, mult_taps=None,
               pack_out=False):
    g_fwd, _, h2, h2i = tables
    bsz, _, cbn, _, _ = u_perm.shape
    n2len, n1h = g_fwd.shape[0], g_fwd.shape[1]
    steps = CONV_STEPS
    ns1, ns2, ns3 = steps
    cn2a, cn2 = n2len // ns1, n2len // ns3
    pitch = n2len + SUBLANES

    def early(s):
        return jnp.minimum(s, ns1 - 1)

    def late(s):
        return jnp.clip(s - ns1 - ns2, 0, ns3 - 1)

    def chunk_specs(chunk_of, width, groups):
        slab = (None, groups, None, SUBLANES, LANES)
        return [
            pl.BlockSpec((None, groups, None, width * SUBLANES, LANES),
                         lambda c, b, s: (b, 0, c, chunk_of(s), 0)),
            pl.BlockSpec(slab, lambda c, b, s: (b, 0, c, (chunk_of(s) * width - 1) % n2len, 0)),
            pl.BlockSpec(slab, lambda c, b, s: (b, 0, c, ((chunk_of(s) + 1) * width) % n2len, 0)),
        ]

    taps_spec = pl.BlockSpec((4, LANES), lambda c, b, s: (0, c))
    in_specs, args = [], []
    specs = chunk_specs(early, cn2a, u_perm.shape[1])
    if in_taps is None:
        in_specs += specs[:1]
        args += [u_perm]
    else:
        in_specs += specs + [taps_spec]
        args += [u_perm, u_perm, u_perm, in_taps]
    in_specs += [
        _resident(g_fwd.shape),
        pl.BlockSpec((n1h // ns2, 2 * n2len, LANES),
                     lambda c, b, s: (jnp.clip(s - ns1, 0, ns2 - 1), 0, order_idx * cbn + c)),
        _resident(h2.shape), _resident(h2i.shape),
    ]
    args += [g_fwd, kf, h2, h2i]
    specs = chunk_specs(late, cn2, mult_perm.shape[1])
    if mult_taps is None:
        in_specs += specs[:1]
        args += [mult_perm]
    else:
        in_specs += specs + [taps_spec]
        args += [mult_perm, mult_perm, mult_perm, mult_taps]
    in_specs.append(pl.BlockSpec((None, 1, LANES), lambda c, b, s: (order_idx, 0, c)))
    args.append(hbias)
    out_groups = n1h // SUBLANES // (2 if pack_out else 1)
    out_spec = chunk_specs(late, cn2, out_groups)[0]
    out_shape = jax.ShapeDtypeStruct((bsz, out_groups, cbn, n2len * SUBLANES, LANES),
                                     jnp.uint32 if pack_out else F32)
    kern = functools.partial(_long_conv_kernel, steps=steps, n1h=n1h, n2len=n2len,
                             pitch=pitch, conv_in=in_taps is not None,
                             conv_mult=mult_taps is not None)
    return pl.pallas_call(
        kern,
        grid=(cbn, bsz, sum(steps)),
        in_specs=in_specs,
        out_specs=out_spec,
        out_shape=out_shape,
        scratch_shapes=[pltpu.VMEM((n1h * pitch, LANES), jnp.uint32),
                        pltpu.VMEM((n2len, n1h, LANES), F32)],
        compiler_params=pltpu.CompilerParams(
            dimension_semantics=("parallel", "parallel", "arbitrary"),
            vmem_limit_bytes=CONV_VMEM_LIMIT),
        name="long_conv",
    )(*args)


def _out_kernel(x_hbm, yh_ref, yp_ref, mod_ref, w_ref, b_ref, g_ref, beta_ref, o_ref,
                x_buf, x_sem, *, n2len, alpha):
    nt = pl.num_programs(1) * OUT_SUBTILES
    total = pl.num_programs(0) * nt
    t = (pl.program_id(0) * pl.num_programs(1) + pl.program_id(1)) * OUT_SUBTILES
    tile = x_buf.shape[1]
    cbn = yh_ref.shape[1]

    def x_copy(step):
        slot = step % STREAM_DEPTH
        return pltpu.make_async_copy(
            x_hbm.at[step // nt, pl.ds((step % nt) * tile, tile), :], x_buf.at[slot],
            x_sem.at[slot])

    @pl.when(t == 0)
    def _():
        for k in range(STREAM_DEPTH - 1):
            x_copy(k).start()

    for sub in range(OUT_SUBTILES):
        u = t + sub

        @pl.when(u + STREAM_DEPTH - 1 < total)
        def _():
            x_copy(u + STREAM_DEPTH - 1).start()

        x_copy(u).wait()
        x_ref = x_buf.at[u % STREAM_DEPTH]
        rows = pl.ds(sub * tile, tile)
        yh = jnp.concatenate(
            [jnp.concatenate([yh_ref[sub, cb, pl.ds(j, n2len, stride=SUBLANES), :]
                              for cb in range(cbn)], axis=1)
             for j in range(SUBLANES)], axis=0)
        y = jnp.concatenate([yh.astype(BF16), yp_ref[rows, :]], axis=1)
        acc = jnp.dot(y, w_ref[...], preferred_element_type=F32)
        h = alpha * x_ref[...] + mod_ref[2:3, :] * (acc + b_ref[...])
        mu = jnp.mean(h, axis=-1, keepdims=True)
        hc = h - mu
        var = jnp.mean(hc * hc, axis=-1, keepdims=True)
        o_ref[rows, :] = hc * lax.rsqrt(var + LN_EPS) * g_ref[...] + beta_ref[...]


def _output_projection(x, yh_perm, yp, mod3, w_out, b_out, ln_g, ln_b, alpha):
    bsz, seq, d = x.shape
    _, _, cbn, tile, _ = yh_perm.shape
    ch = cbn * LANES
    n2len = tile // SUBLANES
    dp = yp.shape[2]
    sub = OUT_SUBTILES
    full = lambda shape: pl.BlockSpec(shape, lambda b, i: (0,) * len(shape))
    kern = functools.partial(_out_kernel, n2len=n2len, alpha=alpha)
    return pl.pallas_call(
        kern,
        grid=(bsz, seq // (sub * tile)),
        in_specs=[
            pl.BlockSpec(memory_space=pl.ANY),
            pl.BlockSpec((None, sub, cbn, tile, LANES), lambda b, i: (b, i, 0, 0, 0)),
            pl.BlockSpec((None, sub * tile, dp), lambda b, i: (b, i, 0)),
            pl.BlockSpec((None, 3, d), lambda b, i: (b, 0, 0)),
            _resident((ch + dp, d)), full((1, d)), full((1, d)), full((1, d)),
        ],
        out_specs=pl.BlockSpec((None, sub * tile, d), lambda b, i: (b, i, 0)),
        out_shape=jax.ShapeDtypeStruct(x.shape, x.dtype),
        scratch_shapes=[pltpu.VMEM((STREAM_DEPTH, tile, d), x.dtype),
                        pltpu.SemaphoreType.DMA((STREAM_DEPTH,))],
        compiler_params=pltpu.CompilerParams(
            dimension_semantics=("arbitrary", "arbitrary"), vmem_limit_bytes=VMEM_LIMIT),
        name="out_proj_deepnorm",
    )(x, yh_perm, yp, mod3, w_out.astype(BF16), b_out.reshape(1, d), ln_g.reshape(1, d),
      ln_b.reshape(1, d))


def _forward(x, c, w_ada, b_ada, w_in, b_in, conv_w, conv_b, filt_w1, filt_b1, filt_w_inner,
             filt_b_inner, filt_w_out, filt_freq, hyena_bias, pool_w, pool_scale, w_out,
             b_out, ln_g, ln_b, *, n2len):
    bsz, seq, d = x.shape
    depth = w_ada.shape[0]
    order, n_ch = hyena_bias.shape[1], hyena_bias.shape[2]
    alpha = (2.0 * depth) ** 0.25
    tables = _dft_tables(seq, n2len)
    h = x
    for layer in range(depth):
        mod3 = _modulation(c, w_ada[layer], b_ada[layer]).reshape(bsz, 3, d)
        v, x1, x2g, yp = _input_projection(
            h, mod3, w_in[layer], b_in[layer], conv_w[layer], conv_b[layer], pool_w[layer],
            pool_scale[layer], n2len)
        taps_f, taps_r = _filter_taps(
            seq, n2len, filt_w1[layer], filt_b1[layer], filt_w_inner[layer],
            filt_b_inner[layer], filt_w_out[layer], filt_freq[layer], n_ch, order)
        kf = _filter_spectrum(taps_f, taps_r, tables[0], tables[1], tables[2], n2len)
        hbias = hyena_bias[layer].reshape(order, 1, n_ch)
        taps = [jnp.concatenate([conv_w[layer][:, k * n_ch:(k + 1) * n_ch],
                                 conv_b[layer][None, k * n_ch:(k + 1) * n_ch]], axis=0)
                for k in range(2)]
        z = _long_conv(v, x1, kf, hbias, 0, tables, in_taps=taps[0], mult_taps=taps[1],
                       pack_out=order > 1)
        for o in range(1, order):
            z = _long_conv(z, x2g, kf, hbias, o, tables)
        h = _output_projection(h, z, yp, mod3, w_out[layer], b_out[layer], ln_g[layer],
                               ln_b[layer], alpha)
    return h


def kernel(x, c, w_ada, b_ada, w_in, b_in, conv_w, conv_b, filt_w1, filt_b1, filt_w_inner,
           filt_b_inner, filt_w_out, filt_freq, hyena_bias, pool_w, pool_scale, w_out, b_out,
           ln_g, ln_b):
    return _forward(x, c, w_ada, b_ada, w_in, b_in, conv_w, conv_b, filt_w1, filt_b1,
                    filt_w_inner, filt_b_inner, filt_w_out, filt_freq, hyena_bias, pool_w,
                    pool_scale, w_out, b_out, ln_g, ln_b, n2len=MINOR_LEN)
```
